```python
import jax, jax.numpy as jnp
from jax import lax
import numpy as np

D_MODEL = 1024
BATCH = 8
SEQ = 4096
DEPTH = 4

CONV_WIDTH = D_MODEL
CONV_K = 3
N_Q_HEADS = 16
N_KV_HEADS = 4
HEAD_DIM = 64
ATTN_WIDTH = N_Q_HEADS * HEAD_DIM
KV_WIDTH = N_KV_HEADS * HEAD_DIM
WINDOW = 128
BLOCK = 128
N_BRANCHES = 2
EPS = 1e-6
NEG_INF = -1e30

IN_SIZES = (CONV_WIDTH, CONV_WIDTH, CONV_WIDTH, CONV_WIDTH,
            ATTN_WIDTH, KV_WIDTH, KV_WIDTH, ATTN_WIDTH,
            N_BRANCHES * D_MODEL)
IN_COLS = sum(IN_SIZES)
SPLIT_POINTS = tuple(int(c) for c in np.cumsum(IN_SIZES)[:-1])

kernel_name = "hybrid_shortconv_swa_sink_gated_block"


def rms_norm(x, g):
    xf = x.astype(jnp.float32)
    y = xf * lax.rsqrt(jnp.mean(xf * xf, axis=-1, keepdims=True) + EPS)
    return (y * g.astype(jnp.float32)).astype(x.dtype)


def causal_depthwise_conv(u, w):
    s = u.shape[1]
    up = jnp.pad(u, ((0, 0), (CONV_K - 1, 0), (0, 0)))
    y = up[:, 0:s] * w[0]
    for k in range(1, CONV_K):
        y = y + up[:, k:k + s] * w[k]
    return y


def sliding_window_attention(q, k, v, sinks):
    b, s = q.shape[0], q.shape[1]
    nb = s // BLOCK
    g = N_Q_HEADS // N_KV_HEADS
    qb = q.reshape(b, nb, BLOCK, N_KV_HEADS, g, HEAD_DIM)

    def band(t):
        tb = t.reshape(b, nb, BLOCK, N_KV_HEADS, HEAD_DIM)
        prev = jnp.pad(tb[:, :-1], ((0, 0), (1, 0), (0, 0), (0, 0), (0, 0)))
        return jnp.concatenate([prev, tb], axis=2)

    kb, vb = band(k), band(v)
    scale = HEAD_DIM ** -0.5
    scores = jnp.einsum('bnqhgd,bnkhd->bnhgqk', qb.astype(jnp.float32),
                        kb.astype(jnp.float32)) * scale
    blk = jnp.arange(nb)[:, None, None]
    q_pos = blk * BLOCK + jnp.arange(BLOCK)[None, :, None]
    k_pos = (blk - 1) * BLOCK + jnp.arange(2 * BLOCK)[None, None, :]
    diff = q_pos - k_pos
    valid = (diff >= 0) & (diff < WINDOW) & (k_pos >= 0)
    scores = jnp.where(valid[None, :, None, None], scores, NEG_INF)
    sink = sinks.astype(jnp.float32).reshape(N_KV_HEADS, g)[None, None, :, :, None, None]
    m = jnp.maximum(jnp.max(scores, axis=-1, keepdims=True), sink)
    p = jnp.exp(scores - m)
    p = p / (jnp.sum(p, axis=-1, keepdims=True) + jnp.exp(sink - m))
    out = jnp.einsum('bnhgqk,bnkhd->bnqhgd', p.astype(v.dtype), vb)
    return out.reshape(b, s, ATTN_WIDTH)


def hybrid_layer(x, norm_g, w_in, conv_w, q_norm_g, k_norm_g, sinks,
                 w_conv_out, w_attn_out, gate_b, w_out):
    b, s, _ = x.shape
    h = rms_norm(x, norm_g)
    u = jnp.einsum('bsd,dc->bsc', h, w_in)
    v_c, b_c, c_c, z_c, q, k, v, z_a, gate_logits = jnp.split(u, SPLIT_POINTS, axis=-1)

    y_c = b_c * causal_depthwise_conv(c_c * v_c, conv_w)
    y_c = y_c * jax.nn.silu(z_c)
    y_a = jnp.einsum('bsc,cd->bsd', y_c, w_conv_out)

    q = rms_norm(q.reshape(b, s, N_Q_HEADS, HEAD_DIM), q_norm_g)
    k = rms_norm(k.reshape(b, s, N_KV_HEADS, HEAD_DIM), k_norm_g)
    v = v.reshape(b, s, N_KV_HEADS, HEAD_DIM)
    o = sliding_window_attention(q, k, v, sinks) * jax.nn.silu(z_a)
    y_b = jnp.einsum('bsc,cd->bsd', o, w_attn_out)

    gates = jax.nn.sigmoid(gate_logits + gate_b)
    g_a, g_b = jnp.split(gates, N_BRANCHES, axis=-1)
    merged = g_a * y_a + g_b * y_b
    return x + jnp.einsum('bsd,de->bse', merged, w_out)


def _fwd_setup_inputs(seed: int = 0) -> dict:
    key = jax.random.key(seed)
    ks = jax.random.split(key, 12)
    f32 = jnp.float32
    x = jax.random.normal(ks[0], (BATCH, SEQ, D_MODEL), f32)
    norm_g = 1.0 + 0.05 * jax.random.normal(ks[1], (DEPTH, D_MODEL), f32)
    w_in = jax.random.normal(ks[2], (DEPTH, D_MODEL, IN_COLS), f32) * D_MODEL ** -0.5
    conv_w = jax.random.normal(ks[3], (DEPTH, CONV_K, CONV_WIDTH), f32) * CONV_K ** -0.5
    q_norm_g = 1.0 + 0.05 * jax.random.normal(ks[4], (DEPTH, HEAD_DIM), f32)
    k_norm_g = 1.0 + 0.05 * jax.random.normal(ks[5], (DEPTH, HEAD_DIM), f32)
    sinks = 0.5 * jax.random.normal(ks[6], (DEPTH, N_Q_HEADS), f32)
    w_conv_out = jax.random.normal(ks[7], (DEPTH, CONV_WIDTH, D_MODEL), f32) * CONV_WIDTH ** -0.5
    w_attn_out = jax.random.normal(ks[8], (DEPTH, ATTN_WIDTH, D_MODEL), f32) * ATTN_WIDTH ** -0.5
    gate_b = 0.02 * jax.random.normal(ks[9], (DEPTH, N_BRANCHES * D_MODEL), f32)
    w_out = jax.random.normal(ks[10], (DEPTH, D_MODEL, D_MODEL), f32) * D_MODEL ** -0.5
    return {"x": x, "norm_g": norm_g, "w_in": w_in, "conv_w": conv_w,
            "q_norm_g": q_norm_g, "k_norm_g": k_norm_g, "sinks": sinks,
            "w_conv_out": w_conv_out, "w_attn_out": w_attn_out,
            "gate_b": gate_b, "w_out": w_out}


def _fwd_reference(x, norm_g, w_in, conv_w, q_norm_g, k_norm_g, sinks,
              w_conv_out, w_attn_out, gate_b, w_out):
    for l in range(DEPTH):
        x = hybrid_layer(x, norm_g[l], w_in[l], conv_w[l], q_norm_g[l], k_norm_g[l],
                         sinks[l], w_conv_out[l], w_attn_out[l], gate_b[l], w_out[l])
    return x


import jax as _jax
import jax.numpy as _jnp

TWIN_FORMAT = 'train_step'
FWD_PARAMS = ['x', 'norm_g', 'w_in', 'conv_w', 'q_norm_g', 'k_norm_g', 'sinks', 'w_conv_out', 'w_attn_out', 'gate_b', 'w_out']
TWIN_WEIGHTS = ['norm_g', 'w_in', 'conv_w', 'q_norm_g', 'k_norm_g', 'sinks', 'w_conv_out', 'w_attn_out', 'gate_b', 'w_out']
TWIN_DIFF_INPUT = 'x'
TWIN_INPUTS = ['x', 'norm_g', 'w_in', 'conv_w', 'q_norm_g', 'k_norm_g', 'sinks', 'w_conv_out', 'w_attn_out', 'gate_b', 'w_out', 'loss_target', 'm_norm_g', 'm_w_in', 'm_conv_w', 'm_q_norm_g', 'm_k_norm_g', 'm_sinks', 'm_w_conv_out', 'm_w_attn_out', 'm_gate_b', 'm_w_out', 'v_norm_g', 'v_w_in', 'v_conv_w', 'v_q_norm_g', 'v_k_norm_g', 'v_sinks', 'v_w_conv_out', 'v_w_attn_out', 'v_gate_b', 'v_w_out']
TWIN_OUTPUTS = ['loss', 'grad_x', 'grad_norm_g', 'grad_w_in', 'grad_conv_w', 'grad_q_norm_g', 'grad_k_norm_g', 'grad_sinks', 'grad_w_conv_out', 'grad_w_attn_out', 'grad_gate_b', 'grad_w_out', 'delta_norm_g', 'delta_w_in', 'delta_conv_w', 'delta_q_norm_g', 'delta_k_norm_g', 'delta_sinks', 'delta_w_conv_out', 'delta_w_attn_out', 'delta_gate_b', 'delta_w_out', 'new_m_norm_g', 'new_m_w_in', 'new_m_conv_w', 'new_m_q_norm_g', 'new_m_k_norm_g', 'new_m_sinks', 'new_m_w_conv_out', 'new_m_w_attn_out', 'new_m_gate_b', 'new_m_w_out', 'new_v_norm_g', 'new_v_w_in', 'new_v_conv_w', 'new_v_q_norm_g', 'new_v_k_norm_g', 'new_v_sinks', 'new_v_w_conv_out', 'new_v_w_attn_out', 'new_v_gate_b', 'new_v_w_out']
TWIN_LEAF_KINDS = {'loss': 'loss', 'grad_x': 'grad_x', 'grad_norm_g': 'grad_w', 'grad_w_in': 'grad_w', 'grad_conv_w': 'grad_w', 'grad_q_norm_g': 'grad_w', 'grad_k_norm_g': 'grad_w', 'grad_sinks': 'grad_w', 'grad_w_conv_out': 'grad_w', 'grad_w_attn_out': 'grad_w', 'grad_gate_b': 'grad_w', 'grad_w_out': 'grad_w', 'delta_norm_g': 'delta_w', 'delta_w_in': 'delta_w', 'delta_conv_w': 'delta_w', 'delta_q_norm_g': 'delta_w', 'delta_k_norm_g': 'delta_w', 'delta_sinks': 'delta_w', 'delta_w_conv_out': 'delta_w', 'delta_w_attn_out': 'delta_w', 'delta_gate_b': 'delta_w', 'delta_w_out': 'delta_w', 'new_m_norm_g': 'new_m', 'new_m_w_in': 'new_m', 'new_m_conv_w': 'new_m', 'new_m_q_norm_g': 'new_m', 'new_m_k_norm_g': 'new_m', 'new_m_sinks': 'new_m', 'new_m_w_conv_out': 'new_m', 'new_m_w_attn_out': 'new_m', 'new_m_gate_b': 'new_m', 'new_m_w_out': 'new_m', 'new_v_norm_g': 'new_v', 'new_v_w_in': 'new_v', 'new_v_conv_w': 'new_v', 'new_v_q_norm_g': 'new_v', 'new_v_k_norm_g': 'new_v', 'new_v_sinks': 'new_v', 'new_v_w_conv_out': 'new_v', 'new_v_w_attn_out': 'new_v', 'new_v_gate_b': 'new_v', 'new_v_w_out': 'new_v'}


def _forward(args):
    return _fwd_reference(*[args[k] for k in FWD_PARAMS])


def _output_shape():
    def fwd():
        inp = _fwd_setup_inputs(0)
        return _fwd_reference(*[inp[k] for k in FWD_PARAMS])
    out = _jax.eval_shape(fwd)
    return out.shape, out.dtype

N_MICROBATCH = 1
ADAM_LR = 0.001
ADAM_B1 = 0.9
ADAM_B2 = 0.999
ADAM_EPS = 1e-08
ADAM_WD = 0.01
ADAM_STEP = 10
PER_EXAMPLE_BATCH_AXIS = {'x': 0, 'loss_target': 0}
SHARED_INPUTS = []
_WEIGHT_DTYPES = {'norm_g': _jnp.float32, 'w_in': _jnp.float32, 'conv_w': _jnp.float32, 'q_norm_g': _jnp.float32, 'k_norm_g': _jnp.float32, 'sinks': _jnp.float32, 'w_conv_out': _jnp.float32, 'w_attn_out': _jnp.float32, 'gate_b': _jnp.float32, 'w_out': _jnp.float32}
MOMENT_SCALE = {'norm_g': 1.452510e+01, 'w_in': 1.847174e-01, 'conv_w': 1.867045e+00, 'q_norm_g': 7.641870e-01, 'k_norm_g': 7.586628e-01, 'sinks': 1.166243e-01, 'w_conv_out': 2.194212e-01, 'w_attn_out': 3.647184e-02, 'gate_b': 8.381339e-01, 'w_out': 2.116657e-01}


def _to_microbatches(a, axis):
    t = _jnp.moveaxis(a, axis, 0)
    t = t.reshape((N_MICROBATCH, t.shape[0] // N_MICROBATCH) + t.shape[1:])
    return _jnp.moveaxis(t, 1, axis + 1)


def setup_inputs(seed: int = 0) -> dict:
    inp = _fwd_setup_inputs(seed)
    key = _jax.random.fold_in(_jax.random.key(seed), 7919)
    shape, _ = _output_shape()
    out = dict(inp)
    out["loss_target"] = _jax.random.normal(_jax.random.fold_in(key, 0), shape, _jnp.float32)
    for i, name in enumerate(TWIN_WEIGHTS):
        w = inp[name].astype(_jnp.float32)
        if MOMENT_SCALE is None:
            s = _jnp.sqrt(_jnp.mean(_jnp.square(w)) + 1e-30)
        else:
            s = MOMENT_SCALE[name]
        km, kv = _jax.random.split(_jax.random.fold_in(key, i + 1))
        out[name] = w
        out["m_" + name] = s * _jax.random.normal(km, w.shape, _jnp.float32)
        out["v_" + name] = (s * s) * _jax.random.uniform(kv, w.shape, _jnp.float32, 0.5, 1.5)
    if N_MICROBATCH > 1:
        for name, axis in PER_EXAMPLE_BATCH_AXIS.items():
            out[name] = _to_microbatches(out[name], axis)
    return {'x': out['x'], 'norm_g': out['norm_g'], 'w_in': out['w_in'], 'conv_w': out['conv_w'], 'q_norm_g': out['q_norm_g'], 'k_norm_g': out['k_norm_g'], 'sinks': out['sinks'], 'w_conv_out': out['w_conv_out'], 'w_attn_out': out['w_attn_out'], 'gate_b': out['gate_b'], 'w_out': out['w_out'], 'loss_target': out['loss_target'], 'm_norm_g': out['m_norm_g'], 'm_w_in': out['m_w_in'], 'm_conv_w': out['m_conv_w'], 'm_q_norm_g': out['m_q_norm_g'], 'm_k_norm_g': out['m_k_norm_g'], 'm_sinks': out['m_sinks'], 'm_w_conv_out': out['m_w_conv_out'], 'm_w_attn_out': out['m_w_attn_out'], 'm_gate_b': out['m_gate_b'], 'm_w_out': out['m_w_out'], 'v_norm_g': out['v_norm_g'], 'v_w_in': out['v_w_in'], 'v_conv_w': out['v_conv_w'], 'v_q_norm_g': out['v_q_norm_g'], 'v_k_norm_g': out['v_k_norm_g'], 'v_sinks': out['v_sinks'], 'v_w_conv_out': out['v_w_conv_out'], 'v_w_attn_out': out['v_w_attn_out'], 'v_gate_b': out['v_gate_b'], 'v_w_out': out['v_w_out']}


def _loss(weights, diff, rest, loss_target):
    with _jax.named_scope("forward"):
        args = {**rest, TWIN_DIFF_INPUT: diff, **{k: w.astype(_WEIGHT_DTYPES[k]) for k, w in weights.items()}}
        y = _forward(args)
    with _jax.named_scope("loss_head"):
        err = _jnp.square(y.astype(_jnp.float32) - loss_target)
        return 0.5 * _jnp.sum(_jnp.mean(err, axis=-1)) if err.ndim else 0.5 * err


def _adamw(w, g, m, v):
    m = ADAM_B1 * m + (1.0 - ADAM_B1) * g
    v = ADAM_B2 * v + (1.0 - ADAM_B2) * _jnp.square(g)
    m_hat = m / (1.0 - ADAM_B1 ** ADAM_STEP)
    v_hat = v / (1.0 - ADAM_B2 ** ADAM_STEP)
    delta = -ADAM_LR * (m_hat / (_jnp.sqrt(v_hat) + ADAM_EPS) + ADAM_WD * w)
    return delta, m, v


def reference(x, norm_g, w_in, conv_w, q_norm_g, k_norm_g, sinks, w_conv_out, w_attn_out, gate_b, w_out, loss_target, m_norm_g, m_w_in, m_conv_w, m_q_norm_g, m_k_norm_g, m_sinks, m_w_conv_out, m_w_attn_out, m_gate_b, m_w_out, v_norm_g, v_w_in, v_conv_w, v_q_norm_g, v_k_norm_g, v_sinks, v_w_conv_out, v_w_attn_out, v_gate_b, v_w_out):
    given = dict(x=x, norm_g=norm_g, w_in=w_in, conv_w=conv_w, q_norm_g=q_norm_g, k_norm_g=k_norm_g, sinks=sinks, w_conv_out=w_conv_out, w_attn_out=w_attn_out, gate_b=gate_b, w_out=w_out, loss_target=loss_target, m_norm_g=m_norm_g, m_w_in=m_w_in, m_conv_w=m_conv_w, m_q_norm_g=m_q_norm_g, m_k_norm_g=m_k_norm_g, m_sinks=m_sinks, m_w_conv_out=m_w_conv_out, m_w_attn_out=m_w_attn_out, m_gate_b=m_gate_b, m_w_out=m_w_out, v_norm_g=v_norm_g, v_w_in=v_w_in, v_conv_w=v_conv_w, v_q_norm_g=v_q_norm_g, v_k_norm_g=v_k_norm_g, v_sinks=v_sinks, v_w_conv_out=v_w_conv_out, v_w_attn_out=v_w_attn_out, v_gate_b=v_gate_b, v_w_out=v_w_out)
    weights = {n: given[n] for n in TWIN_WEIGHTS}
    shared = {n: given[n] for n in SHARED_INPUTS}
    per_example = {n: given[n] for n in ['x']}
    grad_fn = _jax.value_and_grad(_loss, argnums=(0, 1))

    def one_microbatch(ex, loss_target):
        ex = dict(ex)
        diff = ex.pop(TWIN_DIFF_INPUT)
        return grad_fn(weights, diff, {**shared, **ex}, loss_target)

    if N_MICROBATCH == 1:
        loss, (grad_w, grad_x) = one_microbatch(per_example, given["loss_target"])
    else:
        def body(carry, xs):
            loss_sum, grad_sum = carry
            l_k, (gw_k, gx_k) = one_microbatch(xs[0], xs[1])
            with _jax.named_scope("update"):
                return (loss_sum + l_k, _jax.tree.map(_jnp.add, grad_sum, gw_k)), gx_k

        init = (_jnp.zeros((), _jnp.float32), _jax.tree.map(_jnp.zeros_like, weights))
        (loss, grad_w), grad_x = _jax.lax.scan(body, init, (per_example, given["loss_target"]))
    with _jax.named_scope("update"):
        delta_w, new_m, new_v = {}, {}, {}
        for n in TWIN_WEIGHTS:
            delta_w[n], new_m[n], new_v[n] = _adamw(weights[n], grad_w[n], given["m_" + n], given["v_" + n])
    return (loss, grad_x, *[grad_w[n] for n in TWIN_WEIGHTS], *[delta_w[n] for n in TWIN_WEIGHTS],
            *[new_m[n] for n in TWIN_WEIGHTS], *[new_v[n] for n in TWIN_WEIGHTS])
```

```python
import functools

import jax
import jax.numpy as jnp
from jax import lax
from jax.experimental import pallas as pl
from jax.experimental.pallas import tpu as pltpu

F32, BF16 = jnp.float32, jnp.bfloat16
SDS = jax.ShapeDtypeStruct
MESH = pl.DeviceIdType.MESH
ANY = pl.BlockSpec(memory_space=pl.ANY)

D = 1024
N_IN = 8704
N_LAYERS = 4
N_Q, N_KV, HEAD = 16, 4, 64
GROUP = N_Q // N_KV
BLK = 128
EPS = 1e-6
NEG = -1e30
SCALE = HEAD ** -0.5
SH_IN = N_IN // 4
SH_ROW = D // 4
CB = 512
SEG_CONV, SEG_Q, SEG_KV, SEG_ZA, SEG_GL = (0, 8), (8, 2), (10, 1), (11, 2), (13, 4)
VMEM_BIG = 56 * 1024 * 1024

ADAM_LR, ADAM_B1, ADAM_B2, ADAM_EPS, ADAM_WD, ADAM_STEP = 0.001, 0.9, 0.999, 1e-08, 0.01, 10


def _cp(*sem, vmem=None):
    return pltpu.CompilerParams(dimension_semantics=sem if sem else None, vmem_limit_bytes=vmem)


def _sigmoid(z):
    return 1.0 / (1.0 + jnp.exp(-z))


def _dot(a, b):
    return jnp.dot(a, b, preferred_element_type=F32)


def _dot_nt(a, b):
    return lax.dot_general(a, b, (((1,), (1,)), ((), ())), preferred_element_type=F32)


def _dot_tn(a, b):
    return lax.dot_general(a, b, (((0,), (0,)), ((), ())), preferred_element_type=F32)


def _rms(xh):
    r = lax.rsqrt(jnp.mean(xh * xh, axis=-1, keepdims=True) + EPS)
    return xh * r, r


def _fold8(v):
    return jnp.sum(v.reshape(v.shape[0] // 8, 8, v.shape[1]), axis=0)


def _cast_w_in(w):
    rows = N_LAYERS * D

    def body(i_ref, o_ref):
        o_ref[...] = i_ref[...].astype(BF16)

    out = pl.pallas_call(
        body, name="cast_w_in", grid=(rows // 512,),
        in_specs=[pl.BlockSpec((512, SH_IN), lambda i: (i, 0))],
        out_specs=pl.BlockSpec((512, SH_IN), lambda i: (i, 0)),
        out_shape=SDS((rows, SH_IN), BF16), compiler_params=_cp("parallel"),
    )(w.reshape(rows, SH_IN))
    return out.reshape(N_LAYERS, D, SH_IN)


def _cast_w_small(a, b, c):
    def body(a_ref, b_ref, c_ref, o_ref):
        o_ref[0, 0] = a_ref[0].astype(BF16)
        o_ref[0, 1] = b_ref[0].astype(BF16)
        o_ref[0, 2] = c_ref[0].astype(BF16)

    spec = pl.BlockSpec((1, SH_ROW, D), lambda l: (l, 0, 0))
    return pl.pallas_call(
        body, name="cast_w_small", grid=(N_LAYERS,), in_specs=[spec, spec, spec],
        out_specs=pl.BlockSpec((1, 3, SH_ROW, D), lambda l: (l, 0, 0, 0)),
        out_shape=SDS((N_LAYERS, 3, SH_ROW, D), BF16), compiler_params=_cp("parallel"),
    )(a, b, c)


def _mesh_pos():
    return lax.axis_index("x"), lax.axis_index("y"), lax.axis_index("c")


def _other_chips(x, y):
    return [(1 - x, y), (x, 1 - y), (1 - x, 1 - y)]


def _gather_weights(win_b, wsm_b):
    n_cp = N_LAYERS * 2 * 3

    def body(win, wsm, *rest):
        outs = rest[:2 * N_LAYERS]
        send_a, recv_a, send_b, recv_b, loc = rest[2 * N_LAYERS:]
        x, y, c = _mesh_pos()
        me_chip = 2 * x + y
        sib = (x, y, 1 - c)
        chips = _other_chips(x, y)

        def src_half(l, a, half):
            if a == 0:
                return win.at[l, pl.ds(pl.multiple_of(half * 512, 512), 512), :]
            return wsm.at[l, :, pl.ds(pl.multiple_of(half * 128, 128), 128), :]

        def region(l, a, chip, half):
            full = outs[2 * l + a]
            if a == 0:
                return full.at[pl.ds(pl.multiple_of(half * 512, 512), 512),
                               pl.ds(pl.multiple_of(chip * SH_IN, 128), SH_IN)]
            return full.at[:, pl.ds(pl.multiple_of(chip * SH_ROW + half * 128, 128), 128), :]

        def own_region(l, a):
            full = outs[2 * l + a]
            if a == 0:
                return full.at[:, pl.ds(pl.multiple_of(me_chip * SH_IN, 128), SH_IN)]
            return full.at[:, pl.ds(pl.multiple_of(me_chip * SH_ROW, 128), SH_ROW), :]

        def rcopy(src, dst, ssem, rsem, to):
            return pltpu.make_async_remote_copy(src_ref=src, dst_ref=dst, send_sem=ssem, recv_sem=rsem,
                                                device_id=to, device_id_type=MESH)

        locals_, first, passed = [], [], []
        for l in range(N_LAYERS):
            for a in range(2):
                src = win.at[l] if a == 0 else wsm.at[l]
                cp = pltpu.make_async_copy(src, own_region(l, a), loc.at[2 * l + a])
                cp.start()
                locals_.append(cp)
                for k, chip in enumerate(chips):
                    s = (2 * l + a) * 3 + k
                    cp = rcopy(src_half(l, a, c), region(l, a, me_chip, c), send_a.at[s], recv_a.at[s], (*chip, c))
                    cp.start()
                    first.append(cp)
        for l in range(N_LAYERS):
            for a in range(2):
                for k, chip in enumerate(chips):
                    s = (2 * l + a) * 3 + k
                    their = 2 * chip[0] + chip[1]
                    landed = region(l, a, their, c)
                    rcopy(landed, landed, send_a.at[s], recv_a.at[s], sib).wait_recv()
                    cp = rcopy(landed, landed, send_b.at[s], recv_b.at[s], sib)
                    cp.start()
                    passed.append(cp)
        for l in range(N_LAYERS):
            for a in range(2):
                for k, chip in enumerate(chips):
                    s = (2 * l + a) * 3 + k
                    their = 2 * chip[0] + chip[1]
                    other = region(l, a, their, 1 - c)
                    rcopy(other, other, send_b.at[s], recv_b.at[s], sib).wait_recv()
        for cp in first + passed:
            cp.wait_send()
        for cp in locals_:
            cp.wait()

    out_shape = []
    for _ in range(N_LAYERS):
        out_shape += [SDS((D, N_IN), BF16), SDS((3, D, D), BF16)]
    res = pl.pallas_call(
        body, name="gather_weights", in_specs=[ANY, ANY], out_specs=[ANY] * (2 * N_LAYERS), out_shape=out_shape,
        scratch_shapes=[pltpu.SemaphoreType.DMA((n_cp,))] * 4 + [pltpu.SemaphoreType.DMA((2 * N_LAYERS,))],
    )(win_b, wsm_b)
    return [(res[2 * l], res[2 * l + 1]) for l in range(N_LAYERS)]


def _allreduce_small(pack):
    rows = pack.shape[0]

    def body(p_ref, o_ref, buf, send, recv):
        x, y, c = _mesh_pos()
        me = 4 * x + 2 * y + c
        sends = []
        for r in range(1, 8):
            to = (x if not (r & 4) else 1 - x, y if not (r & 2) else 1 - y, c if not (r & 1) else 1 - c)
            cp = pltpu.make_async_remote_copy(src_ref=p_ref, dst_ref=buf.at[me], send_sem=send.at[r - 1],
                                              recv_sem=recv.at[r - 1], device_id=to, device_id_type=MESH)
            cp.start()
            sends.append(cp)
        buf[me] = p_ref[...]
        for r in range(1, 8):
            frm = (4 * x + 2 * y + c) ^ r
            pltpu.make_async_remote_copy(src_ref=p_ref, dst_ref=buf.at[frm], send_sem=send.at[r - 1],
                                         recv_sem=recv.at[r - 1], device_id=(x, y, c), device_id_type=MESH).wait_recv()
        acc = buf[0]
        for d in range(1, 8):
            acc = acc + buf[d]
        o_ref[...] = acc
        for cp in sends:
            cp.wait_send()

    vm = pl.BlockSpec(memory_space=pltpu.VMEM)
    return pl.pallas_call(
        body, name="allreduce_small", in_specs=[vm], out_specs=vm, out_shape=SDS((rows, 128), F32),
        scratch_shapes=[pltpu.VMEM((8, rows, 128), F32), pltpu.SemaphoreType.DMA((7,)), pltpu.SemaphoreType.DMA((7,))],
    )(pack)


def _rmsnorm_fwd(x, g):
    t = x.shape[0]
    tm = min(512, t)

    def body(x_ref, g_ref, h_ref):
        xv = x_ref[...]
        r = lax.rsqrt(jnp.mean(xv * xv, axis=-1, keepdims=True) + EPS)
        h_ref[...] = (xv * r * g_ref[...]).astype(BF16)

    return pl.pallas_call(
        body, name="rmsnorm_fwd", grid=(t // tm,),
        in_specs=[pl.BlockSpec((tm, D), lambda i: (i, 0)), pl.BlockSpec((1, D), lambda i: (0, 0))],
        out_specs=pl.BlockSpec((tm, D), lambda i: (i, 0)), out_shape=SDS((t, D), BF16),
        compiler_params=_cp("parallel"),
    )(x, g)


def _in_proj(h, w_full, seg, name):
    t = h.shape[0]
    off, nblk = seg
    tm = min(2048, t)

    def body(a_ref, b_ref, o_ref):
        o_ref[...] = _dot(a_ref[...], b_ref[...]).astype(BF16)

    return pl.pallas_call(
        body, name=name, grid=(t // tm, nblk),
        in_specs=[pl.BlockSpec((tm, D), lambda i, j: (i, 0)), pl.BlockSpec((D, CB), lambda i, j: (0, off + j))],
        out_specs=pl.BlockSpec((tm, CB), lambda i, j: (i, j)), out_shape=SDS((t, nblk * CB), BF16),
        compiler_params=_cp("parallel", "parallel", vmem=VMEM_BIG),
    )(h, w_full)


def _conv_fwd(u_conv, conv_w):
    t = u_conv.shape[0]
    tm = min(256, t)
    hb = tm // 16

    def body(v_ref, b_ref, c_ref, z_ref, hv_ref, hc_ref, w_ref, y_ref):
        i = pl.program_id(0)
        cv = c_ref[...].astype(F32) * v_ref[...].astype(F32)
        halo = hc_ref[...].astype(F32) * hv_ref[...].astype(F32)
        halo = jnp.where(i > 0, halo, 0.0)
        row = lax.broadcasted_iota(jnp.int32, (tm, 1), 0)
        s1 = jnp.where(row == 0, halo[15:16], pltpu.roll(cv, 1, 0))
        s2 = jnp.where(row == 0, halo[14:15], jnp.where(row == 1, halo[15:16], pltpu.roll(cv, 2, 0)))
        conv = w_ref[0:1, :] * s2 + w_ref[1:2, :] * s1 + w_ref[2:3, :] * cv
        z = z_ref[...].astype(F32)
        y_ref[...] = (b_ref[...].astype(F32) * conv * (z * _sigmoid(z))).astype(BF16)

    def col(k):
        return pl.BlockSpec((tm, D), lambda i: (i, k))

    def halo(k):
        return pl.BlockSpec((16, D), lambda i: (jnp.maximum(i * hb - 1, 0), k))

    return pl.pallas_call(
        body, name="conv_fwd", grid=(t // tm,),
        in_specs=[col(0), col(1), col(2), col(3), halo(0), halo(2), pl.BlockSpec((3, D), lambda i: (0, 0))],
        out_specs=pl.BlockSpec((tm, D), lambda i: (i, 0)), out_shape=SDS((t, D), BF16),
        compiler_params=_cp("parallel", vmem=VMEM_BIG),
    )(u_conv, u_conv, u_conv, u_conv, u_conv, u_conv, conv_w)


def _band_mask(n):
    r = lax.broadcasted_iota(jnp.int32, (GROUP * BLK, 2 * BLK), 0)
    j = lax.broadcasted_iota(jnp.int32, (GROUP * BLK, 2 * BLK), 1)
    diff = (r & (BLK - 1)) - j + BLK
    return (diff >= 0) & (diff < BLK) & ((j >= BLK) | (n > 0))


def _sink_col(sink_ref, hk):
    return jnp.concatenate([jnp.full((BLK, 1), sink_ref[0, GROUP * hk + i], F32) for i in range(GROUP)], axis=0)


def _attn_fwd(u_qkv, u_za, qg, kg, sinks):
    t = u_qkv.shape[0]
    nb = t // BLK

    def body(q_ref, kc_ref, kp_ref, vc_ref, vp_ref, za_ref, qg_ref, kg_ref, sink_ref, o_ref):
        n = pl.program_id(0)
        q = q_ref[...].astype(F32)
        kb = jnp.concatenate([kp_ref[...], kc_ref[...]], axis=0).astype(F32)
        vb = jnp.concatenate([vp_ref[...], vc_ref[...]], axis=0)
        valid = _band_mask(n)
        qgv, kgv = qg_ref[...], kg_ref[...]
        parts = [None] * N_Q
        for hk in range(N_KV):
            kn = (_rms(kb[:, HEAD * hk:HEAD * (hk + 1)])[0] * kgv).astype(BF16)
            v_h = vb[:, HEAD * hk:HEAD * (hk + 1)]
            qs = jnp.concatenate(
                [_rms(q[:, HEAD * (GROUP * hk + i):HEAD * (GROUP * hk + i + 1)])[0] for i in range(GROUP)], axis=0)
            qs = (qs * qgv).astype(BF16)
            s = jnp.where(valid, _dot_nt(qs, kn) * SCALE, NEG)
            sink = _sink_col(sink_ref, hk)
            m = jnp.maximum(jnp.max(s, axis=-1, keepdims=True), sink)
            p = jnp.exp(s - m)
            den = jnp.sum(p, axis=-1, keepdims=True) + jnp.exp(sink - m)
            o = _dot(p.astype(BF16), v_h) / den
            for i in range(GROUP):
                parts[GROUP * hk + i] = o[BLK * i:BLK * (i + 1)]
        attn = jnp.concatenate(parts, axis=1)
        za = za_ref[...].astype(F32)
        o_ref[...] = (attn * (za * _sigmoid(za))).astype(BF16)

    prev = lambda n: jnp.maximum(n - 1, 0)
    return pl.pallas_call(
        body, name="attn_fwd", grid=(nb,),
        in_specs=[pl.BlockSpec((BLK, D), lambda n: (n, 0)),
                  pl.BlockSpec((BLK, 256), lambda n: (n, 4)), pl.BlockSpec((BLK, 256), lambda n: (prev(n), 4)),
                  pl.BlockSpec((BLK, 256), lambda n: (n, 5)), pl.BlockSpec((BLK, 256), lambda n: (prev(n), 5)),
                  pl.BlockSpec((BLK, D), lambda n: (n, 0)),
                  pl.BlockSpec((1, HEAD), lambda n: (0, 0)), pl.BlockSpec((1, HEAD), lambda n: (0, 0)),
                  pl.BlockSpec(memory_space=pltpu.SMEM)],
        out_specs=pl.BlockSpec((BLK, D), lambda n: (n, 0)), out_shape=SDS((t, D), BF16),
        compiler_params=_cp("parallel", vmem=VMEM_BIG),
    )(u_qkv, u_qkv, u_qkv, u_qkv, u_qkv, u_za, qg, kg, sinks)


def _out_proj_fwd(x, y_c, o, u_gl, gate_b, w_sm):
    t = x.shape[0]
    tm = min(512, t)

    def body(x_ref, yc_ref, o_ref, gla_ref, glb_ref, gb_ref, wco_ref, wao_ref, wout_ref,
             xn_ref, ya_ref, yb_ref, mg_ref):
        ya = _dot(yc_ref[...], wco_ref[...])
        yb = _dot(o_ref[...], wao_ref[...])
        gb = gb_ref[...]
        ga_ = _sigmoid(gla_ref[...].astype(F32) + gb[:, :D])
        gb_ = _sigmoid(glb_ref[...].astype(F32) + gb[:, D:])
        merged = (ga_ * ya + gb_ * yb).astype(BF16)
        ya_ref[...] = ya.astype(BF16)
        yb_ref[...] = yb.astype(BF16)
        mg_ref[...] = merged
        xn_ref[...] = x_ref[...] + _dot(merged, wout_ref[...])

    row = pl.BlockSpec((tm, D), lambda i: (i, 0))
    wspec = lambda a: pl.BlockSpec((None, D, D), lambda i: (a, 0, 0))
    return pl.pallas_call(
        body, name="out_proj_fwd", grid=(t // tm,),
        in_specs=[row, row, row, pl.BlockSpec((tm, D), lambda i: (i, 0)), pl.BlockSpec((tm, D), lambda i: (i, 1)),
                  pl.BlockSpec((1, 2 * D), lambda i: (0, 0)), wspec(0), wspec(1), wspec(2)],
        out_specs=[row, row, row, row],
        out_shape=[SDS((t, D), F32), SDS((t, D), BF16), SDS((t, D), BF16), SDS((t, D), BF16)],
        compiler_params=_cp("parallel", vmem=VMEM_BIG),
    )(x, y_c, o, u_gl, u_gl, gate_b, w_sm, w_sm, w_sm)


def _loss_head(y, tgt):
    t = y.shape[0]
    tm = min(512, t)

    def body(y_ref, t_ref, dy_ref, acc_ref):
        @pl.when(pl.program_id(0) == 0)
        def _():
            acc_ref[...] = jnp.zeros_like(acc_ref)
        err = y_ref[...] - t_ref[...]
        dy_ref[...] = err * (1.0 / D)
        sq = _fold8(err * err)
        tot = sq[:, 0:128]
        for k in range(1, D // 128):
            tot = tot + sq[:, 128 * k:128 * (k + 1)]
        acc_ref[...] += tot

    row = pl.BlockSpec((tm, D), lambda i: (i, 0))
    return pl.pallas_call(
        body, name="loss_head", grid=(t // tm,), in_specs=[row, row],
        out_specs=[row, pl.BlockSpec((8, 128), lambda i: (0, 0))],
        out_shape=[SDS((t, D), F32), SDS((8, 128), F32)], compiler_params=_cp("arbitrary"),
    )(y, tgt)


def _out_proj_bwd(dout, y_a, y_b, u_gl, gate_b, w_sm):
    t = dout.shape[0]
    tm = min(512, t)

    def body(do_ref, ya_ref, yb_ref, gla_ref, glb_ref, gb_ref, wco_ref, wao_ref, wout_ref,
             dya_ref, dyb_ref, dgl_ref, dyc_ref, dob_ref, dgb_ref):
        @pl.when(pl.program_id(0) == 0)
        def _():
            dgb_ref[...] = jnp.zeros_like(dgb_ref)
        dm = _dot_nt(do_ref[...].astype(BF16), wout_ref[...])
        gb = gb_ref[...]
        ga_ = _sigmoid(gla_ref[...].astype(F32) + gb[:, :D])
        gb_ = _sigmoid(glb_ref[...].astype(F32) + gb[:, D:])
        dya = (ga_ * dm).astype(BF16)
        dyb = (gb_ * dm).astype(BF16)
        dgla = ya_ref[...].astype(F32) * dm * (ga_ * (1.0 - ga_))
        dglb = yb_ref[...].astype(F32) * dm * (gb_ * (1.0 - gb_))
        dya_ref[...] = dya
        dyb_ref[...] = dyb
        dgl_ref[:, :D] = dgla.astype(BF16)
        dgl_ref[:, D:] = dglb.astype(BF16)
        dgb_ref[:, :D] += _fold8(dgla)
        dgb_ref[:, D:] += _fold8(dglb)
        dyc_ref[...] = _dot_nt(dya, wco_ref[...]).astype(BF16)
        dob_ref[...] = _dot_nt(dyb, wao_ref[...]).astype(BF16)

    row = pl.BlockSpec((tm, D), lambda i: (i, 0))
    wspec = lambda a: pl.BlockSpec((None, D, D), lambda i: (a, 0, 0))
    return pl.pallas_call(
        body, name="out_proj_bwd", grid=(t // tm,),
        in_specs=[row, row, row, pl.BlockSpec((tm, D), lambda i: (i, 0)), pl.BlockSpec((tm, D), lambda i: (i, 1)),
                  pl.BlockSpec((1, 2 * D), lambda i: (0, 0)), wspec(0), wspec(1), wspec(2)],
        out_specs=[row, row, pl.BlockSpec((tm, 2 * D), lambda i: (i, 0)), row, row,
                   pl.BlockSpec((8, 2 * D), lambda i: (0, 0))],
        out_shape=[SDS((t, D), BF16), SDS((t, D), BF16), SDS((t, 2 * D), BF16), SDS((t, D), BF16), SDS((t, D), BF16),
                   SDS((8, 2 * D), F32)],
        compiler_params=_cp("arbitrary", vmem=VMEM_BIG),
    )(dout, y_a, y_b, u_gl, u_gl, gate_b, w_sm, w_sm, w_sm)


def _small_wgrads(y_c, d_ya, o, d_yb, merged, dout):
    t = y_c.shape[0]
    tk = min(512, t)

    def body(yc_ref, dya_ref, o_ref, dyb_ref, mg_ref, do_ref, g_ref):
        @pl.when(pl.program_id(0) == 0)
        def _():
            g_ref[...] = jnp.zeros_like(g_ref)
        g_ref[0] += _dot_tn(yc_ref[...], dya_ref[...])
        g_ref[1] += _dot_tn(o_ref[...], dyb_ref[...])
        g_ref[2] += _dot_tn(mg_ref[...], do_ref[...].astype(BF16))

    row = pl.BlockSpec((tk, D), lambda k: (k, 0))
    return pl.pallas_call(
        body, name="small_wgrads", grid=(t // tk,), in_specs=[row] * 6,
        out_specs=pl.BlockSpec((3, D, D), lambda k: (0, 0, 0)), out_shape=SDS((3, D, D), F32),
        compiler_params=_cp("arbitrary", vmem=VMEM_BIG),
    )(y_c, d_ya, o, d_yb, merged, dout)


def _conv_bwd(d_yc, u_conv, conv_w):
    t = d_yc.shape[0]
    tm = min(256, t)
    hb = tm // 16
    last_halo = t // 16 - 1
    n_steps = t // tm

    def body(dy_ref, v_ref, b_ref, c_ref, z_ref, hv_ref, hc_ref, ndy_ref, nb_ref, nz_ref, w_ref, du_ref, dw_ref):
        i = pl.program_id(0)

        @pl.when(i == 0)
        def _():
            dw_ref[...] = jnp.zeros_like(dw_ref)
        v, c = v_ref[...].astype(F32), c_ref[...].astype(F32)
        b, z = b_ref[...].astype(F32), z_ref[...].astype(F32)
        cv = c * v
        halo = jnp.where(i > 0, hc_ref[...].astype(F32) * hv_ref[...].astype(F32), 0.0)
        row = lax.broadcasted_iota(jnp.int32, (tm, 1), 0)
        s1 = jnp.where(row == 0, halo[15:16], pltpu.roll(cv, 1, 0))
        s2 = jnp.where(row == 0, halo[14:15], jnp.where(row == 1, halo[15:16], pltpu.roll(cv, 2, 0)))
        w0, w1, w2 = w_ref[0:1, :], w_ref[1:2, :], w_ref[2:3, :]
        conv = w0 * s2 + w1 * s1 + w2 * cv
        sig = _sigmoid(z)
        sz = z * sig
        dsz = sig * (1.0 + z * (1.0 - sig))
        dy = dy_ref[...].astype(F32)
        dconv = dy * b * sz
        nz = nz_ref[...].astype(F32)
        nxt = ndy_ref[...].astype(F32) * nb_ref[...].astype(F32) * (nz * _sigmoid(nz))
        nxt = jnp.where(i < n_steps - 1, nxt, 0.0)
        a1 = jnp.where(row == tm - 1, nxt[0:1], pltpu.roll(dconv, tm - 1, 0))
        a2 = jnp.where(row == tm - 2, nxt[0:1], jnp.where(row == tm - 1, nxt[1:2], pltpu.roll(dconv, tm - 2, 0)))
        dcv = w2 * dconv + w1 * a1 + w0 * a2
        du_ref[:, 0:D] = (dcv * c).astype(BF16)
        du_ref[:, D:2 * D] = (dy * conv * sz).astype(BF16)
        du_ref[:, 2 * D:3 * D] = (dcv * v).astype(BF16)
        du_ref[:, 3 * D:4 * D] = (dy * b * conv * dsz).astype(BF16)
        r8 = lax.broadcasted_iota(jnp.int32, (8, 1), 0)
        dw_ref[...] += jnp.where(r8 == 0, jnp.sum(dconv * s2, axis=0, keepdims=True),
                                 jnp.where(r8 == 1, jnp.sum(dconv * s1, axis=0, keepdims=True),
                                           jnp.where(r8 == 2, jnp.sum(dconv * cv, axis=0, keepdims=True), 0.0)))

    def col(k):
        return pl.BlockSpec((tm, D), lambda i: (i, k))

    def halo(k):
        return pl.BlockSpec((16, D), lambda i: (jnp.maximum(i * hb - 1, 0), k))

    def nxt(k):
        return pl.BlockSpec((16, D), lambda i: (jnp.minimum((i + 1) * hb, last_halo), k))

    return pl.pallas_call(
        body, name="conv_bwd", grid=(t // tm,),
        in_specs=[col(0), col(0), col(1), col(2), col(3), halo(0), halo(2), nxt(0), nxt(1), nxt(3),
                  pl.BlockSpec((3, D), lambda i: (0, 0))],
        out_specs=[pl.BlockSpec((tm, 4 * D), lambda i: (i, 0)), pl.BlockSpec((8, D), lambda i: (0, 0))],
        out_shape=[SDS((t, 4 * D), BF16), SDS((8, D), F32)],
        compiler_params=_cp("arbitrary", vmem=VMEM_BIG),
    )(d_yc, u_conv, u_conv, u_conv, u_conv, u_conv, u_conv, d_yc, u_conv, u_conv, conv_w)


def _attn_bwd(d_o, u_qkv, u_za, qg, kg, sinks):
    t = d_o.shape[0]
    nb = t // BLK

    def body(q_ref, kc_ref, kp_ref, vc_ref, vp_ref, za_ref, do_ref, qg_ref, kg_ref, sink_ref,
             dq_ref, dkv_ref, dza_ref, dqg_ref, dkg_ref, dsk_ref, carry_ref):
        n = pl.program_id(0)

        @pl.when(n == 0)
        def _():
            carry_ref[...] = jnp.zeros_like(carry_ref)
            dqg_ref[...] = jnp.zeros_like(dqg_ref)
            dkg_ref[...] = jnp.zeros_like(dkg_ref)
            dsk_ref[...] = jnp.zeros_like(dsk_ref)

        live = n < nb
        q = q_ref[...].astype(F32)
        kp = kp_ref[...].astype(F32)
        kb = jnp.concatenate([kp, kc_ref[...].astype(F32)], axis=0)
        vb = jnp.concatenate([vp_ref[...], vc_ref[...]], axis=0)
        za = za_ref[...].astype(F32)
        sig = _sigmoid(za)
        sa = za * sig
        dsa = sig * (1.0 + za * (1.0 - sig))
        do = jnp.where(live, do_ref[...].astype(F32), 0.0)
        dattn = do * sa
        valid = _band_mask(n)
        qgv, kgv = qg_ref[...], kg_ref[...]
        lane = lax.broadcasted_iota(jnp.int32, (1, 128), 1)
        dq_parts, attn_parts = [None] * N_Q, [None] * N_Q
        dkn_parts, dv_parts = [], []
        dqg = jnp.zeros((1, HEAD), F32)
        dsk = jnp.zeros((1, 128), F32)
        for hk in range(N_KV):
            kn = (_rms(kb[:, HEAD * hk:HEAD * (hk + 1)])[0] * kgv).astype(BF16)
            v_h = vb[:, HEAD * hk:HEAD * (hk + 1)]
            hats = [_rms(q[:, HEAD * (GROUP * hk + i):HEAD * (GROUP * hk + i + 1)]) for i in range(GROUP)]
            qhat = jnp.concatenate([h[0] for h in hats], axis=0)
            qr = jnp.concatenate([h[1] for h in hats], axis=0)
            qs = (qhat * qgv).astype(BF16)
            s = jnp.where(valid, _dot_nt(qs, kn) * SCALE, NEG)
            sink = _sink_col(sink_ref, hk)
            m = jnp.maximum(jnp.max(s, axis=-1, keepdims=True), sink)
            p = jnp.exp(s - m)
            es = jnp.exp(sink - m)
            inv = 1.0 / (jnp.sum(p, axis=-1, keepdims=True) + es)
            attn = _dot(p.astype(BF16), v_h) * inv
            prob = p * inv
            da = jnp.concatenate(
                [dattn[:, HEAD * (GROUP * hk + i):HEAD * (GROUP * hk + i + 1)] for i in range(GROUP)], axis=0)
            da_b = da.astype(BF16)
            dp = _dot_nt(da_b, v_h)
            drow = jnp.sum(da * attn, axis=-1, keepdims=True)
            ds = (prob * (dp - drow) * SCALE).astype(BF16)
            dsink = -(es * inv) * drow
            dqn = _dot(ds, kn)
            dkn_parts.append(_dot_tn(ds, qs))
            dv_parts.append(_dot_tn(prob.astype(BF16), da_b))
            dqg = dqg + jnp.sum(dqn * qhat, axis=0, keepdims=True)
            dxh = dqn * qgv
            dqraw = qr * (dxh - qhat * jnp.mean(dxh * qhat, axis=-1, keepdims=True))
            for i in range(GROUP):
                h = GROUP * hk + i
                dq_parts[h] = dqraw[BLK * i:BLK * (i + 1)]
                attn_parts[h] = attn[BLK * i:BLK * (i + 1)]
                dsk = dsk + jnp.where(lane == h, jnp.sum(dsink[BLK * i:BLK * (i + 1)], axis=0, keepdims=True), 0.0)

        @pl.when(live)
        def _():
            dq_ref[...] = jnp.concatenate(dq_parts, axis=1).astype(BF16)
            dza_ref[...] = (do * jnp.concatenate(attn_parts, axis=1) * dsa).astype(BF16)

        dkn_all = jnp.concatenate(dkn_parts, axis=1)
        dv_all = jnp.concatenate(dv_parts, axis=1)
        carry = carry_ref[...]
        fin_kn = dkn_all[:BLK] + carry[:, :256]
        fin_v = dv_all[:BLK] + carry[:, 256:]
        carry_ref[...] = jnp.concatenate([dkn_all[BLK:], dv_all[BLK:]], axis=1)
        dk_parts = []
        dkg = jnp.zeros((1, HEAD), F32)
        for hk in range(N_KV):
            khat, kr = _rms(kp[:, HEAD * hk:HEAD * (hk + 1)])
            dy = fin_kn[:, HEAD * hk:HEAD * (hk + 1)]
            dkg = dkg + jnp.sum(dy * khat, axis=0, keepdims=True)
            dxh = dy * kgv
            dk_parts.append(kr * (dxh - khat * jnp.mean(dxh * khat, axis=-1, keepdims=True)))
        dkv_ref[...] = jnp.concatenate(dk_parts + [fin_v], axis=1).astype(BF16)
        dqg_ref[...] += jnp.broadcast_to(dqg, (8, HEAD))
        dkg_ref[...] += jnp.broadcast_to(dkg, (8, HEAD))
        dsk_ref[...] += jnp.broadcast_to(dsk, (8, 128))

    cur = lambda n: jnp.minimum(n, nb - 1)
    prev = lambda n: jnp.maximum(n - 1, 0)
    return pl.pallas_call(
        body, name="attn_bwd", grid=(nb + 1,),
        in_specs=[pl.BlockSpec((BLK, D), lambda n: (cur(n), 0)),
                  pl.BlockSpec((BLK, 256), lambda n: (cur(n), 4)), pl.BlockSpec((BLK, 256), lambda n: (prev(n), 4)),
                  pl.BlockSpec((BLK, 256), lambda n: (cur(n), 5)), pl.BlockSpec((BLK, 256), lambda n: (prev(n), 5)),
                  pl.BlockSpec((BLK, D), lambda n: (cur(n), 0)), pl.BlockSpec((BLK, D), lambda n: (cur(n), 0)),
                  pl.BlockSpec((1, HEAD), lambda n: (0, 0)), pl.BlockSpec((1, HEAD), lambda n: (0, 0)),
                  pl.BlockSpec(memory_space=pltpu.SMEM)],
        out_specs=[pl.BlockSpec((BLK, D), lambda n: (cur(n), 0)), pl.BlockSpec((BLK, 512), lambda n: (prev(n), 0)),
                   pl.BlockSpec((BLK, D), lambda n: (cur(n), 0)),
                   pl.BlockSpec((8, HEAD), lambda n: (0, 0)), pl.BlockSpec((8, HEAD), lambda n: (0, 0)),
                   pl.BlockSpec((8, 128), lambda n: (0, 0))],
        out_shape=[SDS((t, D), BF16), SDS((t, 512), BF16), SDS((t, D), BF16),
                   SDS((8, HEAD), F32), SDS((8, HEAD), F32), SDS((8, 128), F32)],
        scratch_shapes=[pltpu.VMEM((BLK, 512), F32)],
        compiler_params=_cp("arbitrary", vmem=VMEM_BIG),
    )(u_qkv, u_qkv, u_qkv, u_qkv, u_qkv, u_za, d_o, qg, kg, sinks)


_SEGS = (SEG_CONV, SEG_Q, SEG_KV, SEG_ZA, SEG_GL)
N_GRAN = N_IN // CB


def _seg_spec(rows, seg, step_of):
    off, nblk = seg
    return pl.BlockSpec((rows, CB), lambda *g: (g[step_of[0]], jnp.clip(g[step_of[1]] - off, 0, nblk - 1)))


def _in_proj_bwd(du, w_full, x, g, dout):
    t = x.shape[0]
    tm = min(512, t)

    def body(a0, a1, a2, a3, a4, b_ref, x_ref, g_ref, do_ref, dx_ref, dg_ref, acc_ref):
        i, k = pl.program_id(0), pl.program_id(1)

        @pl.when((i == 0) & (k == 0))
        def _():
            dg_ref[...] = jnp.zeros_like(dg_ref)

        @pl.when(k == 0)
        def _():
            acc_ref[...] = jnp.zeros_like(acc_ref)

        for a_ref, (off, nblk) in zip((a0, a1, a2, a3, a4), _SEGS):
            @pl.when((k >= off) & (k < off + nblk))
            def _():
                acc_ref[...] += _dot_nt(a_ref[...], b_ref[...])

        @pl.when(k == N_GRAN - 1)
        def _():
            dh = acc_ref[...]
            xv = x_ref[...]
            r = lax.rsqrt(jnp.mean(xv * xv, axis=-1, keepdims=True) + EPS)
            xhat = xv * r
            dg_ref[...] += _fold8(dh * xhat)
            dxh = dh * g_ref[...]
            dx_ref[...] = do_ref[...] + r * (dxh - xhat * jnp.mean(dxh * xhat, axis=-1, keepdims=True))

    row = pl.BlockSpec((tm, D), lambda i, k: (i, 0))
    return pl.pallas_call(
        body, name="in_proj_bwd", grid=(t // tm, N_GRAN),
        in_specs=[_seg_spec(tm, s, (0, 1)) for s in _SEGS]
        + [pl.BlockSpec((D, CB), lambda i, k: (0, k)), row, pl.BlockSpec((1, D), lambda i, k: (0, 0)), row],
        out_specs=[row, pl.BlockSpec((8, D), lambda i, k: (0, 0))],
        out_shape=[SDS((t, D), F32), SDS((8, D), F32)],
        scratch_shapes=[pltpu.VMEM((tm, D), F32)],
        compiler_params=_cp("arbitrary", "arbitrary", vmem=VMEM_BIG),
    )(*du, w_full, x, g, dout)


def _in_proj_wgrad(h, du):
    t = h.shape[0]
    tk = min(1024, t)

    def body(h_ref, b0, b1, b2, b3, b4, g_ref):
        j, k = pl.program_id(0), pl.program_id(1)

        @pl.when(k == 0)
        def _():
            g_ref[...] = jnp.zeros_like(g_ref)

        for b_ref, (off, nblk) in zip((b0, b1, b2, b3, b4), _SEGS):
            @pl.when((j >= off) & (j < off + nblk))
            def _():
                g_ref[...] += _dot_tn(h_ref[...], b_ref[...])

    return pl.pallas_call(
        body, name="in_proj_wgrad", grid=(N_GRAN, t // tk),
        in_specs=[pl.BlockSpec((tk, D), lambda j, k: (k, 0))] + [_seg_spec(tk, s, (1, 0)) for s in _SEGS],
        out_specs=pl.BlockSpec((D, CB), lambda j, k: (0, j)), out_shape=SDS((D, N_IN), F32),
        compiler_params=_cp("parallel", "arbitrary", vmem=VMEM_BIG),
    )(h, *du)


def _swap_halves(g_in, g_sm):
    def body(*refs):
        ins, outs = refs[:2 * N_LAYERS], refs[2 * N_LAYERS:4 * N_LAYERS]
        send, recv = refs[4 * N_LAYERS:]
        x, y, c = _mesh_pos()
        cps = []
        for l in range(N_LAYERS):
            for a in range(2):
                s = 2 * l + a
                src = ins[s].at[1 - c] if a == 0 else ins[s].at[:, :, 1 - c]
                cp = pltpu.make_async_remote_copy(src_ref=src, dst_ref=outs[s], send_sem=send.at[s], recv_sem=recv.at[s],
                                                  device_id=(x, y, 1 - c), device_id_type=MESH)
                cp.start()
                cps.append(cp)
        for cp in cps:
            cp.wait()

    out_shape, args = [], []
    for l in range(N_LAYERS):
        out_shape += [SDS((512, N_IN), F32), SDS((3, 4, 128, D), F32)]
        args += [g_in[l], g_sm[l]]
    res = pl.pallas_call(
        body, name="swap_halves", in_specs=[ANY] * (2 * N_LAYERS), out_specs=[ANY] * (2 * N_LAYERS), out_shape=out_shape,
        scratch_shapes=[pltpu.SemaphoreType.DMA((2 * N_LAYERS,))] * 2,
    )(*args)
    return [(res[2 * l], res[2 * l + 1]) for l in range(N_LAYERS)]


def _add_halves_in(c_idx, g_in, r_in):
    def body(c_ref, a_ref, b_ref, f_ref, h_ref):
        s = a_ref[...] + b_ref[...]
        f_ref[...] = s
        h_ref[...] = s.astype(BF16)

    blk = pl.BlockSpec((128, N_IN), lambda i, c: (i, 0))
    return pl.pallas_call(
        body, name="add_halves_in",
        grid_spec=pltpu.PrefetchScalarGridSpec(
            num_scalar_prefetch=1, grid=(4,),
            in_specs=[pl.BlockSpec((None, 128, N_IN), lambda i, c: (c[0], i, 0)), blk], out_specs=[blk, blk]),
        out_shape=[SDS((512, N_IN), F32), SDS((512, N_IN), BF16)], compiler_params=_cp("parallel", vmem=VMEM_BIG),
    )(c_idx, g_in, r_in)


def _add_halves_sm(c_idx, g_sm, r_sm):
    def body(c_ref, a_ref, b_ref, f_ref, h_ref):
        s = a_ref[...] + b_ref[...]
        f_ref[...] = s
        h_ref[...] = s.astype(BF16)

    blk = pl.BlockSpec((1, 4, 128, D), lambda a, c: (a, 0, 0, 0))
    return pl.pallas_call(
        body, name="add_halves_sm",
        grid_spec=pltpu.PrefetchScalarGridSpec(
            num_scalar_prefetch=1, grid=(3,),
            in_specs=[pl.BlockSpec((1, 4, None, 128, D), lambda a, c: (a, 0, c[0], 0, 0)), blk], out_specs=[blk, blk]),
        out_shape=[SDS((3, 4, 128, D), F32), SDS((3, 4, 128, D), BF16)], compiler_params=_cp("parallel"),
    )(c_idx, g_sm, r_sm)


def _scatter_chips(h_in, h_sm):
    n_cp = N_LAYERS * 2 * 3

    def body(*refs):
        ins, outs = refs[:2 * N_LAYERS], refs[2 * N_LAYERS:4 * N_LAYERS]
        send, recv = refs[4 * N_LAYERS:]
        x, y, c = _mesh_pos()
        cps = []
        for l in range(N_LAYERS):
            for a in range(2):
                for k, chip in enumerate(_other_chips(x, y)):
                    s = (2 * l + a) * 3 + k
                    their = 2 * chip[0] + chip[1]
                    if a == 0:
                        src = ins[2 * l].at[:, pl.ds(pl.multiple_of(their * SH_IN, 128), SH_IN)]
                    else:
                        src = ins[2 * l + 1].at[:, their]
                    cp = pltpu.make_async_remote_copy(src_ref=src, dst_ref=outs[2 * l + a].at[k], send_sem=send.at[s],
                                                      recv_sem=recv.at[s], device_id=(*chip, c), device_id_type=MESH)
                    cp.start()
                    cps.append(cp)
        for cp in cps:
            cp.wait()

    out_shape, args = [], []
    for l in range(N_LAYERS):
        out_shape += [SDS((3, 512, SH_IN), BF16), SDS((3, 3, 128, D), BF16)]
        args += [h_in[l], h_sm[l]]
    res = pl.pallas_call(
        body, name="scatter_chips", in_specs=[ANY] * (2 * N_LAYERS), out_specs=[ANY] * (2 * N_LAYERS), out_shape=out_shape,
        scratch_shapes=[pltpu.SemaphoreType.DMA((n_cp,))] * 2,
    )(*args)
    return [(res[2 * l], res[2 * l + 1]) for l in range(N_LAYERS)]


def _final_sum_in(chip_idx, f_in, r_in):
    def body(j_ref, a_ref, r_ref, o_ref):
        o_ref[...] = a_ref[...] + r_ref[0].astype(F32) + r_ref[1].astype(F32) + r_ref[2].astype(F32)

    return pl.pallas_call(
        body, name="final_sum_in",
        grid_spec=pltpu.PrefetchScalarGridSpec(
            num_scalar_prefetch=1, grid=(4,),
            in_specs=[pl.BlockSpec((128, SH_IN), lambda i, j: (i, j[0])), pl.BlockSpec((3, 128, SH_IN), lambda i, j: (0, i, 0))],
            out_specs=pl.BlockSpec((128, SH_IN), lambda i, j: (i, 0))),
        out_shape=SDS((512, SH_IN), F32), compiler_params=_cp("parallel"),
    )(chip_idx, f_in, r_in)


def _final_sum_sm(chip_idx, f_sm, r_sm):
    def body(j_ref, a_ref, r_ref, o_ref):
        o_ref[...] = a_ref[...] + r_ref[0].astype(F32) + r_ref[1].astype(F32) + r_ref[2].astype(F32)

    return pl.pallas_call(
        body, name="final_sum_sm",
        grid_spec=pltpu.PrefetchScalarGridSpec(
            num_scalar_prefetch=1, grid=(3,),
            in_specs=[pl.BlockSpec((1, None, 128, D), lambda a, j: (a, j[0], 0, 0)),
                      pl.BlockSpec((3, 1, 128, D), lambda a, j: (0, a, 0, 0))],
            out_specs=pl.BlockSpec((1, 128, D), lambda a, j: (a, 0, 0))),
        out_shape=SDS((3, 128, D), F32), compiler_params=_cp("parallel"),
    )(chip_idx, f_sm, r_sm)


def _join_halves(t_in, t_sm):
    n_cp = N_LAYERS * 4

    def body(*refs):
        ins, outs = refs[:2 * N_LAYERS], refs[2 * N_LAYERS:2 * N_LAYERS + 4]
        send, recv, loc = refs[2 * N_LAYERS + 4:]
        x, y, c = _mesh_pos()
        cps, lcs = [], []
        for l in range(N_LAYERS):
            for a in range(4):
                s = 4 * l + a
                if a == 0:
                    src = ins[2 * l]
                    dst = outs[0].at[l, pl.ds(pl.multiple_of(c * 512, 512), 512), :]
                else:
                    src = ins[2 * l + 1].at[a - 1]
                    dst = outs[a].at[l, pl.ds(pl.multiple_of(c * 128, 128), 128), :]
                lc = pltpu.make_async_copy(src, dst, loc.at[s])
                lc.start()
                lcs.append(lc)
                cp = pltpu.make_async_remote_copy(src_ref=src, dst_ref=dst, send_sem=send.at[s], recv_sem=recv.at[s],
                                                  device_id=(x, y, 1 - c), device_id_type=MESH)
                cp.start()
                cps.append(cp)
        for l in range(N_LAYERS):
            for a in range(4):
                s = 4 * l + a
                if a == 0:
                    got = outs[0].at[l, pl.ds(pl.multiple_of((1 - c) * 512, 512), 512), :]
                else:
                    got = outs[a].at[l, pl.ds(pl.multiple_of((1 - c) * 128, 128), 128), :]
                pltpu.make_async_remote_copy(src_ref=got, dst_ref=got, send_sem=send.at[s], recv_sem=recv.at[s],
                                             device_id=(x, y, 1 - c), device_id_type=MESH).wait_recv()
        for cp in cps:
            cp.wait_send()
        for lc in lcs:
            lc.wait()

    args = []
    for l in range(N_LAYERS):
        args += [t_in[l], t_sm[l]]
    sm = SDS((N_LAYERS, SH_ROW, D), F32)
    return pl.pallas_call(
        body, name="join_halves", in_specs=[ANY] * (2 * N_LAYERS), out_specs=[ANY] * 4,
        out_shape=[SDS((N_LAYERS, D, SH_IN), F32), sm, sm, sm],
        scratch_shapes=[pltpu.SemaphoreType.DMA((n_cp,))] * 3,
    )(*args)


def _adam_math(w, g, m, v):
    m = ADAM_B1 * m + (1.0 - ADAM_B1) * g
    v = ADAM_B2 * v + (1.0 - ADAM_B2) * (g * g)
    m_hat = m / (1.0 - ADAM_B1 ** ADAM_STEP)
    v_hat = v / (1.0 - ADAM_B2 ** ADAM_STEP)
    delta = -ADAM_LR * (m_hat / (jnp.sqrt(v_hat) + ADAM_EPS) + ADAM_WD * w)
    return delta, m, v


def _adamw_big(w, g, m, v, name):
    rows, cols = w.shape
    tr = 128

    def body(w_ref, g_ref, m_ref, v_ref, d_ref, nm_ref, nv_ref):
        d_ref[...], nm_ref[...], nv_ref[...] = _adam_math(w_ref[...], g_ref[...], m_ref[...], v_ref[...])

    blk = pl.BlockSpec((tr, cols), lambda i: (i, 0))
    return pl.pallas_call(
        body, name=name, grid=(rows // tr,), in_specs=[blk] * 4, out_specs=[blk] * 3,
        out_shape=[SDS((rows, cols), F32)] * 3, compiler_params=_cp("parallel", vmem=VMEM_BIG),
    )(w, g, m, v)


def _adamw_small(ws, gs, ms, vs):
    n = len(ws)

    def body(*refs):
        for k in range(n):
            w_ref, g_ref, m_ref, v_ref = (refs[q * n + k] for q in range(4))
            d, nm, nv = _adam_math(w_ref[...], g_ref[...], m_ref[...], v_ref[...])
            refs[4 * n + k][...] = d
            refs[5 * n + k][...] = nm
            refs[6 * n + k][...] = nv

    vm = pl.BlockSpec(memory_space=pltpu.VMEM)
    shapes = [SDS(w.shape, F32) for w in ws]
    res = pl.pallas_call(
        body, name="adamw_small", in_specs=[vm] * (4 * n), out_specs=[vm] * (3 * n), out_shape=shapes * 3,
    )(*ws, *gs, *ms, *vs)
    return res[:n], res[n:2 * n], res[2 * n:]


def _pad_rows(a, rows):
    flat = a.reshape(-1)
    return jnp.pad(flat, (0, rows * 128 - flat.shape[0])).reshape(rows, 128)


def kernel(x, norm_g, w_in, conv_w, q_norm_g, k_norm_g, sinks, w_conv_out, w_attn_out, gate_b, w_out, loss_target, m_norm_g, m_w_in, m_conv_w, m_q_norm_g, m_k_norm_g, m_sinks, m_w_conv_out, m_w_attn_out, m_gate_b, m_w_out, v_norm_g, v_w_in, v_conv_w, v_q_norm_g, v_k_norm_g, v_sinks, v_w_conv_out, v_w_attn_out, v_gate_b, v_w_out):
    xi, yi, ci = _mesh_pos()
    chip = 2 * xi + yi
    c_idx = jnp.reshape(ci, (1,)).astype(jnp.int32)
    chip_idx = jnp.reshape(chip, (1,)).astype(jnp.int32)
    t = x.shape[1]
    xs = [x.reshape(t, D)]
    tgt = loss_target.reshape(t, D)

    full_w = _gather_weights(_cast_w_in(w_in), _cast_w_small(w_conv_out, w_attn_out, w_out))
    placed = lax.dynamic_update_slice(jnp.zeros((N_LAYERS, 3, D), F32),
                                      jnp.where(ci == 0, conv_w, 0.0), (0, 0, chip * SH_ROW))
    conv_full = _allreduce_small(placed.reshape(96, 128)).reshape(N_LAYERS, 3, D)

    saved = []
    for l in range(N_LAYERS):
        w_full, w_sm = full_w[l]
        h = _rmsnorm_fwd(xs[l], norm_g[l:l + 1])
        u_conv = _in_proj(h, w_full, (0, 8), "in_proj_conv")
        u_qkv = _in_proj(h, w_full, (8, 3), "in_proj_qkv")
        u_za = _in_proj(h, w_full, SEG_ZA, "in_proj_za")
        u_gl = _in_proj(h, w_full, SEG_GL, "in_proj_gl")
        y_c = _conv_fwd(u_conv, conv_full[l])
        o = _attn_fwd(u_qkv, u_za, q_norm_g[l:l + 1], k_norm_g[l:l + 1], sinks[l:l + 1])
        x_next, y_a, y_b, merged = _out_proj_fwd(xs[l], y_c, o, u_gl, gate_b[l:l + 1], w_sm)
        xs.append(x_next)
        saved.append((h, u_conv, u_qkv, u_za, u_gl, y_c, o, y_a, y_b, merged))

    dout, sq = _loss_head(xs[N_LAYERS], tgt)
    loss = lax.psum(jnp.sum(sq) * (0.5 / D), ("x", "y", "c"))

    g_in, g_sm, small = [None] * N_LAYERS, [None] * N_LAYERS, [None] * N_LAYERS
    for l in reversed(range(N_LAYERS)):
        w_full, w_sm = full_w[l]
        h, u_conv, u_qkv, u_za, u_gl, y_c, o, y_a, y_b, merged = saved[l]
        d_ya, d_yb, du_gl, d_yc, d_o, dgb = _out_proj_bwd(dout, y_a, y_b, u_gl, gate_b[l:l + 1], w_sm)
        g_sm[l] = _small_wgrads(y_c, d_ya, o, d_yb, merged, dout)
        du_conv, dcw = _conv_bwd(d_yc, u_conv, conv_full[l])
        du_q, du_kv, du_za, dqg, dkg, dsk = _attn_bwd(d_o, u_qkv, u_za, q_norm_g[l:l + 1], k_norm_g[l:l + 1],
                                                     sinks[l:l + 1])
        du = (du_conv, du_q, du_kv, du_za, du_gl)
        g_in[l] = _in_proj_wgrad(h, du)
        dout, dng = _in_proj_bwd(du, w_full, xs[l], norm_g[l:l + 1], dout)
        small[l] = (jnp.sum(dng, axis=0), dqg[0], dkg[0], dsk[0, :N_Q], jnp.sum(dgb, axis=0), dcw[:3])
    grad_x = dout.reshape(1, t, D)

    stack = lambda k: jnp.stack([small[l][k] for l in range(N_LAYERS)])
    pack = jnp.concatenate([_pad_rows(stack(0), 32), _pad_rows(stack(1), 8), _pad_rows(stack(2), 8),
                            _pad_rows(stack(3), 8), _pad_rows(stack(4), 64), _pad_rows(stack(5), 96)], axis=0)
    red = _allreduce_small(pack)
    g_norm_g = red[0:32].reshape(N_LAYERS, D)
    g_q_norm_g = red[32:40].reshape(-1)[:N_LAYERS * HEAD].reshape(N_LAYERS, HEAD)
    g_k_norm_g = red[40:48].reshape(-1)[:N_LAYERS * HEAD].reshape(N_LAYERS, HEAD)
    g_sinks = red[48:56].reshape(-1)[:N_LAYERS * N_Q].reshape(N_LAYERS, N_Q)
    g_gate_b = red[56:120].reshape(N_LAYERS, 2 * D)
    g_conv_full = red[120:216].reshape(N_LAYERS, 3, D)
    g_conv_w = lax.dynamic_slice(g_conv_full, (0, 0, chip * SH_ROW), (N_LAYERS, 3, SH_ROW))

    g_in_v = [g.reshape(2, 512, N_IN) for g in g_in]
    g_sm_v = [g.reshape(3, 4, 2, 128, D) for g in g_sm]
    got = _swap_halves(g_in_v, g_sm_v)
    h_in, h_sm, f_in, f_sm = [], [], [], []
    for l in range(N_LAYERS):
        f, hb = _add_halves_in(c_idx, g_in_v[l], got[l][0])
        f_in.append(f)
        h_in.append(hb)
        f, hb = _add_halves_sm(c_idx, g_sm_v[l], got[l][1])
        f_sm.append(f)
        h_sm.append(hb)
    parts = _scatter_chips(h_in, h_sm)
    t_in = [_final_sum_in(chip_idx, f_in[l], parts[l][0]) for l in range(N_LAYERS)]
    t_sm = [_final_sum_sm(chip_idx, f_sm[l], parts[l][1]) for l in range(N_LAYERS)]
    g_w_in, g_w_co, g_w_ao, g_w_out = _join_halves(t_in, t_sm)

    r_in = N_LAYERS * D
    d_in, nm_in, nv_in = (a.reshape(N_LAYERS, D, SH_IN) for a in _adamw_big(
        w_in.reshape(r_in, SH_IN), g_w_in.reshape(r_in, SH_IN), m_w_in.reshape(r_in, SH_IN),
        v_w_in.reshape(r_in, SH_IN), "adamw_w_in"))
    r_sm = N_LAYERS * SH_ROW
    big = {}
    for nm, w, g, m, v in (("co", w_conv_out, g_w_co, m_w_conv_out, v_w_conv_out),
                           ("ao", w_attn_out, g_w_ao, m_w_attn_out, v_w_attn_out),
                           ("out", w_out, g_w_out, m_w_out, v_w_out)):
        big[nm] = tuple(a.reshape(N_LAYERS, SH_ROW, D) for a in _adamw_big(
            w.reshape(r_sm, D), g.reshape(r_sm, D), m.reshape(r_sm, D), v.reshape(r_sm, D), "adamw_w_small"))
    sm_w = [norm_g, conv_w, q_norm_g, k_norm_g, sinks, gate_b]
    sm_g = [g_norm_g, g_conv_w, g_q_norm_g, g_k_norm_g, g_sinks, g_gate_b]
    sm_m = [m_norm_g, m_conv_w, m_q_norm_g, m_k_norm_g, m_sinks, m_gate_b]
    sm_v = [v_norm_g, v_conv_w, v_q_norm_g, v_k_norm_g, v_sinks, v_gate_b]
    sd, snm, snv = _adamw_small(sm_w, sm_g, sm_m, sm_v)

    def order(norm, w_in_, conv, qn, kn, sk, co, ao, gb, wo):
        return [norm, w_in_, conv, qn, kn, sk, co, ao, gb, wo]

    grads = order(g_norm_g, g_w_in, g_conv_w, g_q_norm_g, g_k_norm_g, g_sinks, g_w_co, g_w_ao, g_gate_b, g_w_out)
    deltas = order(sd[0], d_in, sd[1], sd[2], sd[3], sd[4], big["co"][0], big["ao"][0], sd[5], big["out"][0])
    new_m = order(snm[0], nm_in, snm[1], snm[2], snm[3], snm[4], big["co"][1], big["ao"][1], snm[5], big["out"][1])
    new_v = order(snv[0], nv_in, snv[1], snv[2], snv[3], snv[4], big["co"][2], big["ao"][2], snv[5], big["out"][2])
    return (loss, grad_x, *grads, *deltas, *new_m, *new_v)
```

```python
import functools

import jax
import jax.numpy as jnp
from jax import lax
from jax.experimental import pallas as pl
from jax.experimental.pallas import tpu as pltpu

F32, BF16 = jnp.float32, jnp.bfloat16
SDS = jax.ShapeDtypeStruct
MESH = pl.DeviceIdType.MESH
ANY = pl.BlockSpec(memory_space=pl.ANY)

D = 1024
N_IN = 8704
N_LAYERS = 4
N_Q, N_KV, HEAD = 16, 4, 64
GROUP = N_Q // N_KV
BLK = 128
EPS = 1e-6
NEG = -1e30
SCALE = HEAD ** -0.5
SH_IN = N_IN // 4
SH_ROW = D // 4
CB = 512
SEG_CONV, SEG_Q, SEG_KV, SEG_ZA, SEG_GL = (0, 8), (8, 2), (10, 1), (11, 2), (13, 4)
VMEM_BIG = 56 * 1024 * 1024

ADAM_LR, ADAM_B1, ADAM_B2, ADAM_EPS, ADAM_WD, ADAM_STEP = 0.001, 0.9, 0.999, 1e-08, 0.01, 10


def _cp(*sem, vmem=None):
    return pltpu.CompilerParams(dimension_semantics=sem if sem else None, vmem_limit_bytes=vmem)


def _sigmoid(z):
    return 1.0 / (1.0 + jnp.exp(-z))


def _dot(a, b):
    return jnp.dot(a, b, preferred_element_type=F32)


def _dot_nt(a, b):
    return lax.dot_general(a, b, (((1,), (1,)), ((), ())), preferred_element_type=F32)


def _dot_tn(a, b):
    return lax.dot_general(a, b, (((0,), (0,)), ((), ())), preferred_element_type=F32)


def _rms(xh):
    r = lax.rsqrt(jnp.mean(xh * xh, axis=-1, keepdims=True) + EPS)
    return xh * r, r


def _fold8(v):
    return jnp.sum(v.reshape(v.shape[0] // 8, 8, v.shape[1]), axis=0)


def _cast_w_in(w):
    rows = N_LAYERS * D

    def body(i_ref, o_ref):
        o_ref[...] = i_ref[...].astype(BF16)

    out = pl.pallas_call(
        body, name="cast_w_in", grid=(rows // 512,),
        in_specs=[pl.BlockSpec((512, SH_IN), lambda i: (i, 0))],
        out_specs=pl.BlockSpec((512, SH_IN), lambda i: (i, 0)),
        out_shape=SDS((rows, SH_IN), BF16), compiler_params=_cp("parallel"),
    )(w.reshape(rows, SH_IN))
    return out.reshape(N_LAYERS, D, SH_IN)


def _cast_w_small(a, b, c):
    def body(a_ref, b_ref, c_ref, o_ref):
        o_ref[0, 0] = a_ref[0].astype(BF16)
        o_ref[0, 1] = b_ref[0].astype(BF16)
        o_ref[0, 2] = c_ref[0].astype(BF16)

    spec = pl.BlockSpec((1, SH_ROW, D), lambda l: (l, 0, 0))
    return pl.pallas_call(
        body, name="cast_w_small", grid=(N_LAYERS,), in_specs=[spec, spec, spec],
        out_specs=pl.BlockSpec((1, 3, SH_ROW, D), lambda l: (l, 0, 0, 0)),
        out_shape=SDS((N_LAYERS, 3, SH_ROW, D), BF16), compiler_params=_cp("parallel"),
    )(a, b, c)


def _mesh_pos():
    return lax.axis_index("x"), lax.axis_index("y"), lax.axis_index("c")


def _other_chips(x, y):
    return [(1 - x, y), (x, 1 - y), (1 - x, 1 - y)]


def _staged_copies(copies, stages, sem_in, sem_out):
    busy, count = {}, {}
    for idx, (src, dst, kind) in enumerate(copies):
        slot = count.get(kind, 0) % 2
        count[kind] = count.get(kind, 0) + 1
        if (kind, slot) in busy:
            busy.pop((kind, slot)).wait()
        buf = stages[kind].at[slot]
        cin = pltpu.make_async_copy(src, buf, sem_in.at[idx])
        cin.start()
        cin.wait()
        cout = pltpu.make_async_copy(buf, dst, sem_out.at[idx])
        cout.start()
        busy[(kind, slot)] = cout
    for cp in busy.values():
        cp.wait()


def _gather_weights(win_b, wsm_b):
    n_cp = N_LAYERS * 2 * 3

    def body(win, wsm, *rest):
        outs = rest[:2 * N_LAYERS]
        send_a, recv_a, send_b, recv_b, loc_in, loc_out, stage_in, stage_sm = rest[2 * N_LAYERS:]
        x, y, c = _mesh_pos()
        me_chip = 2 * x + y
        sib = (x, y, 1 - c)
        chips = _other_chips(x, y)

        def src_half(l, a, half):
            if a == 0:
                return win.at[l, pl.ds(pl.multiple_of(half * 512, 512), 512), :]
            return wsm.at[l, :, pl.ds(pl.multiple_of(half * 128, 128), 128), :]

        def region(l, a, chip, half):
            full = outs[2 * l + a]
            if a == 0:
                return full.at[pl.ds(pl.multiple_of(half * 512, 512), 512),
                               pl.ds(pl.multiple_of(chip * SH_IN, 128), SH_IN)]
            return full.at[:, pl.ds(pl.multiple_of(chip * SH_ROW + half * 128, 128), 128), :]

        def own_region(l, a):
            full = outs[2 * l + a]
            if a == 0:
                return full.at[:, pl.ds(pl.multiple_of(me_chip * SH_IN, 128), SH_IN)]
            return full.at[:, pl.ds(pl.multiple_of(me_chip * SH_ROW, 128), SH_ROW), :]

        def rcopy(src, dst, ssem, rsem, to):
            return pltpu.make_async_remote_copy(src_ref=src, dst_ref=dst, send_sem=ssem, recv_sem=rsem,
                                                device_id=to, device_id_type=MESH)

        first, passed = [], []
        for l in range(N_LAYERS):
            for a in range(2):
                for k, chip in enumerate(chips):
                    s = (2 * l + a) * 3 + k
                    cp = rcopy(src_half(l, a, c), region(l, a, me_chip, c), send_a.at[s], recv_a.at[s], (*chip, c))
                    cp.start()
                    first.append(cp)
        own = [(win.at[l] if a == 0 else wsm.at[l], own_region(l, a), a) for l in range(N_LAYERS) for a in range(2)]
        _staged_copies(own, (stage_in, stage_sm), loc_in, loc_out)
        for l in range(N_LAYERS):
            for a in range(2):
                for k, chip in enumerate(chips):
                    s = (2 * l + a) * 3 + k
                    their = 2 * chip[0] + chip[1]
                    landed = region(l, a, their, c)
                    rcopy(landed, landed, send_a.at[s], recv_a.at[s], sib).wait_recv()
                    cp = rcopy(landed, landed, send_b.at[s], recv_b.at[s], sib)
                    cp.start()
                    passed.append(cp)
        for l in range(N_LAYERS):
            for a in range(2):
                for k, chip in enumerate(chips):
                    s = (2 * l + a) * 3 + k
                    their = 2 * chip[0] + chip[1]
                    other = region(l, a, their, 1 - c)
                    rcopy(other, other, send_b.at[s], recv_b.at[s], sib).wait_recv()
        for cp in first + passed:
            cp.wait_send()

    out_shape = []
    for _ in range(N_LAYERS):
        out_shape += [SDS((D, N_IN), BF16), SDS((3, D, D), BF16)]
    res = pl.pallas_call(
        body, name="gather_weights", in_specs=[ANY, ANY], out_specs=[ANY] * (2 * N_LAYERS), out_shape=out_shape,
        scratch_shapes=[pltpu.SemaphoreType.DMA((n_cp,))] * 4 + [pltpu.SemaphoreType.DMA((2 * N_LAYERS,))] * 2
        + [pltpu.VMEM((2, D, SH_IN), BF16), pltpu.VMEM((2, 3, SH_ROW, D), BF16)],
        compiler_params=_cp(vmem=VMEM_BIG),
    )(win_b, wsm_b)
    return [(res[2 * l], res[2 * l + 1]) for l in range(N_LAYERS)]


def _allreduce_small(pack):
    rows = pack.shape[0]

    def body(p_ref, o_ref, buf, send, recv):
        x, y, c = _mesh_pos()
        me = 4 * x + 2 * y + c
        sends = []
        for r in range(1, 8):
            to = (x if not (r & 4) else 1 - x, y if not (r & 2) else 1 - y, c if not (r & 1) else 1 - c)
            cp = pltpu.make_async_remote_copy(src_ref=p_ref, dst_ref=buf.at[me], send_sem=send.at[r - 1],
                                              recv_sem=recv.at[r - 1], device_id=to, device_id_type=MESH)
            cp.start()
            sends.append(cp)
        buf[me] = p_ref[...]
        for r in range(1, 8):
            frm = (4 * x + 2 * y + c) ^ r
            pltpu.make_async_remote_copy(src_ref=p_ref, dst_ref=buf.at[frm], send_sem=send.at[r - 1],
                                         recv_sem=recv.at[r - 1], device_id=(x, y, c), device_id_type=MESH).wait_recv()
        acc = buf[0]
        for d in range(1, 8):
            acc = acc + buf[d]
        o_ref[...] = acc
        for cp in sends:
            cp.wait_send()

    vm = pl.BlockSpec(memory_space=pltpu.VMEM)
    return pl.pallas_call(
        body, name="allreduce_small", in_specs=[vm], out_specs=vm, out_shape=SDS((rows, 128), F32),
        scratch_shapes=[pltpu.VMEM((8, rows, 128), F32), pltpu.SemaphoreType.DMA((7,)), pltpu.SemaphoreType.DMA((7,))],
    )(pack)


def _rmsnorm_fwd(x, g):
    t = x.shape[0]
    tm = min(512, t)

    def body(x_ref, g_ref, h_ref):
        xv = x_ref[...]
        r = lax.rsqrt(jnp.mean(xv * xv, axis=-1, keepdims=True) + EPS)
        h_ref[...] = (xv * r * g_ref[...]).astype(BF16)

    return pl.pallas_call(
        body, name="rmsnorm_fwd", grid=(t // tm,),
        in_specs=[pl.BlockSpec((tm, D), lambda i: (i, 0)), pl.BlockSpec((1, D), lambda i: (0, 0))],
        out_specs=pl.BlockSpec((tm, D), lambda i: (i, 0)), out_shape=SDS((t, D), BF16),
        compiler_params=_cp("parallel"),
    )(x, g)


def _in_proj(h, w_full, seg, name):
    t = h.shape[0]
    off, nblk = seg
    tm = min(2048, t)

    def body(a_ref, b_ref, o_ref):
        o_ref[...] = _dot(a_ref[...], b_ref[...]).astype(BF16)

    return pl.pallas_call(
        body, name=name, grid=(t // tm, nblk),
        in_specs=[pl.BlockSpec((tm, D), lambda i, j: (i, 0)), pl.BlockSpec((D, CB), lambda i, j: (0, off + j))],
        out_specs=pl.BlockSpec((tm, CB), lambda i, j: (i, j)), out_shape=SDS((t, nblk * CB), BF16),
        compiler_params=_cp("parallel", "parallel", vmem=VMEM_BIG),
    )(h, w_full)


def _conv_fwd(u_conv, conv_w):
    t = u_conv.shape[0]
    tm = min(256, t)
    hb = tm // 16

    def body(v_ref, b_ref, c_ref, z_ref, hv_ref, hc_ref, w_ref, y_ref):
        i = pl.program_id(0)
        cv = c_ref[...].astype(F32) * v_ref[...].astype(F32)
        halo = hc_ref[...].astype(F32) * hv_ref[...].astype(F32)
        halo = jnp.where(i > 0, halo, 0.0)
        row = lax.broadcasted_iota(jnp.int32, (tm, 1), 0)
        s1 = jnp.where(row == 0, halo[15:16], pltpu.roll(cv, 1, 0))
        s2 = jnp.where(row == 0, halo[14:15], jnp.where(row == 1, halo[15:16], pltpu.roll(cv, 2, 0)))
        conv = w_ref[0:1, :] * s2 + w_ref[1:2, :] * s1 + w_ref[2:3, :] * cv
        z = z_ref[...].astype(F32)
        y_ref[...] = (b_ref[...].astype(F32) * conv * (z * _sigmoid(z))).astype(BF16)

    def col(k):
        return pl.BlockSpec((tm, D), lambda i: (i, k))

    def halo(k):
        return pl.BlockSpec((16, D), lambda i: (jnp.maximum(i * hb - 1, 0), k))

    return pl.pallas_call(
        body, name="conv_fwd", grid=(t // tm,),
        in_specs=[col(0), col(1), col(2), col(3), halo(0), halo(2), pl.BlockSpec((3, D), lambda i: (0, 0))],
        out_specs=pl.BlockSpec((tm, D), lambda i: (i, 0)), out_shape=SDS((t, D), BF16),
        compiler_params=_cp("parallel", vmem=VMEM_BIG),
    )(u_conv, u_conv, u_conv, u_conv, u_conv, u_conv, conv_w)


def _band_mask(n):
    r = lax.broadcasted_iota(jnp.int32, (GROUP * BLK, 2 * BLK), 0)
    j = lax.broadcasted_iota(jnp.int32, (GROUP * BLK, 2 * BLK), 1)
    diff = (r & (BLK - 1)) - j + BLK
    return (diff >= 0) & (diff < BLK) & ((j >= BLK) | (n > 0))


def _sink_col(sink_ref, hk):
    return jnp.concatenate([jnp.full((BLK, 1), sink_ref[0, GROUP * hk + i], F32) for i in range(GROUP)], axis=0)


def _attn_fwd(u_qkv, u_za, qg, kg, sinks):
    t = u_qkv.shape[0]
    nb = t // BLK

    def body(q_ref, kc_ref, kp_ref, vc_ref, vp_ref, za_ref, qg_ref, kg_ref, sink_ref, o_ref):
        n = pl.program_id(0)
        q = q_ref[...].astype(F32)
        kb = jnp.concatenate([kp_ref[...], kc_ref[...]], axis=0).astype(F32)
        vb = jnp.concatenate([vp_ref[...], vc_ref[...]], axis=0)
        valid = _band_mask(n)
        qgv, kgv = qg_ref[...], kg_ref[...]
        parts = [None] * N_Q
        for hk in range(N_KV):
            kn = (_rms(kb[:, HEAD * hk:HEAD * (hk + 1)])[0] * kgv).astype(BF16)
            v_h = vb[:, HEAD * hk:HEAD * (hk + 1)]
            qs = jnp.concatenate(
                [_rms(q[:, HEAD * (GROUP * hk + i):HEAD * (GROUP * hk + i + 1)])[0] for i in range(GROUP)], axis=0)
            qs = (qs * qgv).astype(BF16)
            s = jnp.where(valid, _dot_nt(qs, kn) * SCALE, NEG)
            sink = _sink_col(sink_ref, hk)
            m = jnp.maximum(jnp.max(s, axis=-1, keepdims=True), sink)
            p = jnp.exp(s - m)
            den = jnp.sum(p, axis=-1, keepdims=True) + jnp.exp(sink - m)
            o = _dot(p.astype(BF16), v_h) / den
            for i in range(GROUP):
                parts[GROUP * hk + i] = o[BLK * i:BLK * (i + 1)]
        attn = jnp.concatenate(parts, axis=1)
        za = za_ref[...].astype(F32)
        o_ref[...] = (attn * (za * _sigmoid(za))).astype(BF16)

    prev = lambda n: jnp.maximum(n - 1, 0)
    return pl.pallas_call(
        body, name="attn_fwd", grid=(nb,),
        in_specs=[pl.BlockSpec((BLK, D), lambda n: (n, 0)),
                  pl.BlockSpec((BLK, 256), lambda n: (n, 4)), pl.BlockSpec((BLK, 256), lambda n: (prev(n), 4)),
                  pl.BlockSpec((BLK, 256), lambda n: (n, 5)), pl.BlockSpec((BLK, 256), lambda n: (prev(n), 5)),
                  pl.BlockSpec((BLK, D), lambda n: (n, 0)),
                  pl.BlockSpec((1, HEAD), lambda n: (0, 0)), pl.BlockSpec((1, HEAD), lambda n: (0, 0)),
                  pl.BlockSpec(memory_space=pltpu.SMEM)],
        out_specs=pl.BlockSpec((BLK, D), lambda n: (n, 0)), out_shape=SDS((t, D), BF16),
        compiler_params=_cp("parallel", vmem=VMEM_BIG),
    )(u_qkv, u_qkv, u_qkv, u_qkv, u_qkv, u_za, qg, kg, sinks)


def _out_proj_fwd(x, y_c, o, u_gl, gate_b, w_sm):
    t = x.shape[0]
    tm = min(512, t)

    def body(x_ref, yc_ref, o_ref, gla_ref, glb_ref, gb_ref, wco_ref, wao_ref, wout_ref,
             xn_ref, ya_ref, yb_ref, mg_ref):
        ya = _dot(yc_ref[...], wco_ref[...])
        yb = _dot(o_ref[...], wao_ref[...])
        gb = gb_ref[...]
        ga_ = _sigmoid(gla_ref[...].astype(F32) + gb[:, :D])
        gb_ = _sigmoid(glb_ref[...].astype(F32) + gb[:, D:])
        merged = (ga_ * ya + gb_ * yb).astype(BF16)
        ya_ref[...] = ya.astype(BF16)
        yb_ref[...] = yb.astype(BF16)
        mg_ref[...] = merged
        xn_ref[...] = x_ref[...] + _dot(merged, wout_ref[...])

    row = pl.BlockSpec((tm, D), lambda i: (i, 0))
    wspec = lambda a: pl.BlockSpec((None, D, D), lambda i: (a, 0, 0))
    return pl.pallas_call(
        body, name="out_proj_fwd", grid=(t // tm,),
        in_specs=[row, row, row, pl.BlockSpec((tm, D), lambda i: (i, 0)), pl.BlockSpec((tm, D), lambda i: (i, 1)),
                  pl.BlockSpec((1, 2 * D), lambda i: (0, 0)), wspec(0), wspec(1), wspec(2)],
        out_specs=[row, row, row, row],
        out_shape=[SDS((t, D), F32), SDS((t, D), BF16), SDS((t, D), BF16), SDS((t, D), BF16)],
        compiler_params=_cp("parallel", vmem=VMEM_BIG),
    )(x, y_c, o, u_gl, u_gl, gate_b, w_sm, w_sm, w_sm)


def _loss_head(y, tgt):
    t = y.shape[0]
    tm = min(512, t)

    def body(y_ref, t_ref, dy_ref, acc_ref):
        @pl.when(pl.program_id(0) == 0)
        def _():
            acc_ref[...] = jnp.zeros_like(acc_ref)
        err = y_ref[...] - t_ref[...]
        dy_ref[...] = err * (1.0 / D)
        sq = _fold8(err * err)
        tot = sq[:, 0:128]
        for k in range(1, D // 128):
            tot = tot + sq[:, 128 * k:128 * (k + 1)]
        acc_ref[...] += tot

    row = pl.BlockSpec((tm, D), lambda i: (i, 0))
    return pl.pallas_call(
        body, name="loss_head", grid=(t // tm,), in_specs=[row, row],
        out_specs=[row, pl.BlockSpec((8, 128), lambda i: (0, 0))],
        out_shape=[SDS((t, D), F32), SDS((8, 128), F32)], compiler_params=_cp("arbitrary"),
    )(y, tgt)


def _out_proj_bwd(dout, y_a, y_b, u_gl, gate_b, w_sm):
    t = dout.shape[0]
    tm = min(512, t)

    def body(do_ref, ya_ref, yb_ref, gla_ref, glb_ref, gb_ref, wco_ref, wao_ref, wout_ref,
             dya_ref, dyb_ref, dgl_ref, dyc_ref, dob_ref, dgb_ref):
        @pl.when(pl.program_id(0) == 0)
        def _():
            dgb_ref[...] = jnp.zeros_like(dgb_ref)
        dm = _dot_nt(do_ref[...].astype(BF16), wout_ref[...])
        gb = gb_ref[...]
        ga_ = _sigmoid(gla_ref[...].astype(F32) + gb[:, :D])
        gb_ = _sigmoid(glb_ref[...].astype(F32) + gb[:, D:])
        dya = (ga_ * dm).astype(BF16)
        dyb = (gb_ * dm).astype(BF16)
        dgla = ya_ref[...].astype(F32) * dm * (ga_ * (1.0 - ga_))
        dglb = yb_ref[...].astype(F32) * dm * (gb_ * (1.0 - gb_))
        dya_ref[...] = dya
        dyb_ref[...] = dyb
        dgl_ref[:, :D] = dgla.astype(BF16)
        dgl_ref[:, D:] = dglb.astype(BF16)
        dgb_ref[:, :D] += _fold8(dgla)
        dgb_ref[:, D:] += _fold8(dglb)
        dyc_ref[...] = _dot_nt(dya, wco_ref[...]).astype(BF16)
        dob_ref[...] = _dot_nt(dyb, wao_ref[...]).astype(BF16)

    row = pl.BlockSpec((tm, D), lambda i: (i, 0))
    wspec = lambda a: pl.BlockSpec((None, D, D), lambda i: (a, 0, 0))
    return pl.pallas_call(
        body, name="out_proj_bwd", grid=(t // tm,),
        in_specs=[row, row, row, pl.BlockSpec((tm, D), lambda i: (i, 0)), pl.BlockSpec((tm, D), lambda i: (i, 1)),
                  pl.BlockSpec((1, 2 * D), lambda i: (0, 0)), wspec(0), wspec(1), wspec(2)],
        out_specs=[row, row, pl.BlockSpec((tm, 2 * D), lambda i: (i, 0)), row, row,
                   pl.BlockSpec((8, 2 * D), lambda i: (0, 0))],
        out_shape=[SDS((t, D), BF16), SDS((t, D), BF16), SDS((t, 2 * D), BF16), SDS((t, D), BF16), SDS((t, D), BF16),
                   SDS((8, 2 * D), F32)],
        compiler_params=_cp("arbitrary", vmem=VMEM_BIG),
    )(dout, y_a, y_b, u_gl, u_gl, gate_b, w_sm, w_sm, w_sm)


def _small_wgrads(y_c, d_ya, o, d_yb, merged, dout):
    t = y_c.shape[0]
    tk = min(512, t)

    def body(yc_ref, dya_ref, o_ref, dyb_ref, mg_ref, do_ref, g_ref):
        @pl.when(pl.program_id(0) == 0)
        def _():
            g_ref[...] = jnp.zeros_like(g_ref)
        g_ref[0] += _dot_tn(yc_ref[...], dya_ref[...])
        g_ref[1] += _dot_tn(o_ref[...], dyb_ref[...])
        g_ref[2] += _dot_tn(mg_ref[...], do_ref[...].astype(BF16))

    row = pl.BlockSpec((tk, D), lambda k: (k, 0))
    return pl.pallas_call(
        body, name="small_wgrads", grid=(t // tk,), in_specs=[row] * 6,
        out_specs=pl.BlockSpec((3, D, D), lambda k: (0, 0, 0)), out_shape=SDS((3, D, D), F32),
        compiler_params=_cp("arbitrary", vmem=VMEM_BIG),
    )(y_c, d_ya, o, d_yb, merged, dout)


def _conv_bwd(d_yc, u_conv, conv_w):
    t = d_yc.shape[0]
    tm = min(256, t)
    hb = tm // 16
    last_halo = t // 16 - 1
    n_steps = t // tm

    def body(dy_ref, v_ref, b_ref, c_ref, z_ref, hv_ref, hc_ref, ndy_ref, nb_ref, nz_ref, w_ref, du_ref, dw_ref):
        i = pl.program_id(0)

        @pl.when(i == 0)
        def _():
            dw_ref[...] = jnp.zeros_like(dw_ref)
        v, c = v_ref[...].astype(F32), c_ref[...].astype(F32)
        b, z = b_ref[...].astype(F32), z_ref[...].astype(F32)
        cv = c * v
        halo = jnp.where(i > 0, hc_ref[...].astype(F32) * hv_ref[...].astype(F32), 0.0)
        row = lax.broadcasted_iota(jnp.int32, (tm, 1), 0)
        s1 = jnp.where(row == 0, halo[15:16], pltpu.roll(cv, 1, 0))
        s2 = jnp.where(row == 0, halo[14:15], jnp.where(row == 1, halo[15:16], pltpu.roll(cv, 2, 0)))
        w0, w1, w2 = w_ref[0:1, :], w_ref[1:2, :], w_ref[2:3, :]
        conv = w0 * s2 + w1 * s1 + w2 * cv
        sig = _sigmoid(z)
        sz = z * sig
        dsz = sig * (1.0 + z * (1.0 - sig))
        dy = dy_ref[...].astype(F32)
        dconv = dy * b * sz
        nz = nz_ref[...].astype(F32)
        nxt = ndy_ref[...].astype(F32) * nb_ref[...].astype(F32) * (nz * _sigmoid(nz))
        nxt = jnp.where(i < n_steps - 1, nxt, 0.0)
        a1 = jnp.where(row == tm - 1, nxt[0:1], pltpu.roll(dconv, tm - 1, 0))
        a2 = jnp.where(row == tm - 2, nxt[0:1], jnp.where(row == tm - 1, nxt[1:2], pltpu.roll(dconv, tm - 2, 0)))
        dcv = w2 * dconv + w1 * a1 + w0 * a2
        du_ref[:, 0:D] = (dcv * c).astype(BF16)
        du_ref[:, D:2 * D] = (dy * conv * sz).astype(BF16)
        du_ref[:, 2 * D:3 * D] = (dcv * v).astype(BF16)
        du_ref[:, 3 * D:4 * D] = (dy * b * conv * dsz).astype(BF16)
        r8 = lax.broadcasted_iota(jnp.int32, (8, 1), 0)
        dw_ref[...] += jnp.where(r8 == 0, jnp.sum(dconv * s2, axis=0, keepdims=True),
                                 jnp.where(r8 == 1, jnp.sum(dconv * s1, axis=0, keepdims=True),
                                           jnp.where(r8 == 2, jnp.sum(dconv * cv, axis=0, keepdims=True), 0.0)))

    def col(k):
        return pl.BlockSpec((tm, D), lambda i: (i, k))

    def halo(k):
        return pl.BlockSpec((16, D), lambda i: (jnp.maximum(i * hb - 1, 0), k))

    def nxt(k):
        return pl.BlockSpec((16, D), lambda i: (jnp.minimum((i + 1) * hb, last_halo), k))

    return pl.pallas_call(
        body, name="conv_bwd", grid=(t // tm,),
        in_specs=[col(0), col(0), col(1), col(2), col(3), halo(0), halo(2), nxt(0), nxt(1), nxt(3),
                  pl.BlockSpec((3, D), lambda i: (0, 0))],
        out_specs=[pl.BlockSpec((tm, 4 * D), lambda i: (i, 0)), pl.BlockSpec((8, D), lambda i: (0, 0))],
        out_shape=[SDS((t, 4 * D), BF16), SDS((8, D), F32)],
        compiler_params=_cp("arbitrary", vmem=VMEM_BIG),
    )(d_yc, u_conv, u_conv, u_conv, u_conv, u_conv, u_conv, d_yc, u_conv, u_conv, conv_w)


def _attn_bwd(d_o, u_qkv, u_za, qg, kg, sinks):
    t = d_o.shape[0]
    nb = t // BLK

    def body(q_ref, kc_ref, kp_ref, vc_ref, vp_ref, za_ref, do_ref, qg_ref, kg_ref, sink_ref,
             dq_ref, dkv_ref, dza_ref, dqg_ref, dkg_ref, dsk_ref, carry_ref):
        n = pl.program_id(0)

        @pl.when(n == 0)
        def _():
            carry_ref[...] = jnp.zeros_like(carry_ref)
            dqg_ref[...] = jnp.zeros_like(dqg_ref)
            dkg_ref[...] = jnp.zeros_like(dkg_ref)
            dsk_ref[...] = jnp.zeros_like(dsk_ref)

        live = n < nb
        q = q_ref[...].astype(F32)
        kp = kp_ref[...].astype(F32)
        kb = jnp.concatenate([kp, kc_ref[...].astype(F32)], axis=0)
        vb = jnp.concatenate([vp_ref[...], vc_ref[...]], axis=0)
        za = za_ref[...].astype(F32)
        sig = _sigmoid(za)
        sa = za * sig
        dsa = sig * (1.0 + za * (1.0 - sig))
        do = jnp.where(live, do_ref[...].astype(F32), 0.0)
        dattn = do * sa
        valid = _band_mask(n)
        qgv, kgv = qg_ref[...], kg_ref[...]
        lane = lax.broadcasted_iota(jnp.int32, (1, 128), 1)
        dq_parts, attn_parts = [None] * N_Q, [None] * N_Q
        dkn_parts, dv_parts = [], []
        dqg = jnp.zeros((1, HEAD), F32)
        dsk = jnp.zeros((1, 128), F32)
        for hk in range(N_KV):
            kn = (_rms(kb[:, HEAD * hk:HEAD * (hk + 1)])[0] * kgv).astype(BF16)
            v_h = vb[:, HEAD * hk:HEAD * (hk + 1)]
            hats = [_rms(q[:, HEAD * (GROUP * hk + i):HEAD * (GROUP * hk + i + 1)]) for i in range(GROUP)]
            qhat = jnp.concatenate([h[0] for h in hats], axis=0)
            qr = jnp.concatenate([h[1] for h in hats], axis=0)
            qs = (qhat * qgv).astype(BF16)
            s = jnp.where(valid, _dot_nt(qs, kn) * SCALE, NEG)
            sink = _sink_col(sink_ref, hk)
            m = jnp.maximum(jnp.max(s, axis=-1, keepdims=True), sink)
            p = jnp.exp(s - m)
            es = jnp.exp(sink - m)
            inv = 1.0 / (jnp.sum(p, axis=-1, keepdims=True) + es)
            attn = _dot(p.astype(BF16), v_h) * inv
            prob = p * inv
            da = jnp.concatenate(
                [dattn[:, HEAD * (GROUP * hk + i):HEAD * (GROUP * hk + i + 1)] for i in range(GROUP)], axis=0)
            da_b = da.astype(BF16)
            dp = _dot_nt(da_b, v_h)
            drow = jnp.sum(da * attn, axis=-1, keepdims=True)
            ds = (prob * (dp - drow) * SCALE).astype(BF16)
            dsink = -(es * inv) * drow
            dqn = _dot(ds, kn)
            dkn_parts.append(_dot_tn(ds, qs))
            dv_parts.append(_dot_tn(prob.astype(BF16), da_b))
            dqg = dqg + jnp.sum(dqn * qhat, axis=0, keepdims=True)
            dxh = dqn * qgv
            dqraw = qr * (dxh - qhat * jnp.mean(dxh * qhat, axis=-1, keepdims=True))
            for i in range(GROUP):
                h = GROUP * hk + i
                dq_parts[h] = dqraw[BLK * i:BLK * (i + 1)]
                attn_parts[h] = attn[BLK * i:BLK * (i + 1)]
                dsk = dsk + jnp.where(lane == h, jnp.sum(dsink[BLK * i:BLK * (i + 1)], axis=0, keepdims=True), 0.0)

        @pl.when(live)
        def _():
            dq_ref[...] = jnp.concatenate(dq_parts, axis=1).astype(BF16)
            dza_ref[...] = (do * jnp.concatenate(attn_parts, axis=1) * dsa).astype(BF16)

        dkn_all = jnp.concatenate(dkn_parts, axis=1)
        dv_all = jnp.concatenate(dv_parts, axis=1)
        carry = carry_ref[...]
        fin_kn = dkn_all[:BLK] + carry[:, :256]
        fin_v = dv_all[:BLK] + carry[:, 256:]
        carry_ref[...] = jnp.concatenate([dkn_all[BLK:], dv_all[BLK:]], axis=1)
        dk_parts = []
        dkg = jnp.zeros((1, HEAD), F32)
        for hk in range(N_KV):
            khat, kr = _rms(kp[:, HEAD * hk:HEAD * (hk + 1)])
            dy = fin_kn[:, HEAD * hk:HEAD * (hk + 1)]
            dkg = dkg + jnp.sum(dy * khat, axis=0, keepdims=True)
            dxh = dy * kgv
            dk_parts.append(kr * (dxh - khat * jnp.mean(dxh * khat, axis=-1, keepdims=True)))
        dkv_ref[...] = jnp.concatenate(dk_parts + [fin_v], axis=1).astype(BF16)
        dqg_ref[...] += jnp.broadcast_to(dqg, (8, HEAD))
        dkg_ref[...] += jnp.broadcast_to(dkg, (8, HEAD))
        dsk_ref[...] += jnp.broadcast_to(dsk, (8, 128))

    cur = lambda n: jnp.minimum(n, nb - 1)
    prev = lambda n: jnp.maximum(n - 1, 0)
    return pl.pallas_call(
        body, name="attn_bwd", grid=(nb + 1,),
        in_specs=[pl.BlockSpec((BLK, D), lambda n: (cur(n), 0)),
                  pl.BlockSpec((BLK, 256), lambda n: (cur(n), 4)), pl.BlockSpec((BLK, 256), lambda n: (prev(n), 4)),
                  pl.BlockSpec((BLK, 256), lambda n: (cur(n), 5)), pl.BlockSpec((BLK, 256), lambda n: (prev(n), 5)),
                  pl.BlockSpec((BLK, D), lambda n: (cur(n), 0)), pl.BlockSpec((BLK, D), lambda n: (cur(n), 0)),
                  pl.BlockSpec((1, HEAD), lambda n: (0, 0)), pl.BlockSpec((1, HEAD), lambda n: (0, 0)),
                  pl.BlockSpec(memory_space=pltpu.SMEM)],
        out_specs=[pl.BlockSpec((BLK, D), lambda n: (cur(n), 0)), pl.BlockSpec((BLK, 512), lambda n: (prev(n), 0)),
                   pl.BlockSpec((BLK, D), lambda n: (cur(n), 0)),
                   pl.BlockSpec((8, HEAD), lambda n: (0, 0)), pl.BlockSpec((8, HEAD), lambda n: (0, 0)),
                   pl.BlockSpec((8, 128), lambda n: (0, 0))],
        out_shape=[SDS((t, D), BF16), SDS((t, 512), BF16), SDS((t, D), BF16),
                   SDS((8, HEAD), F32), SDS((8, HEAD), F32), SDS((8, 128), F32)],
        scratch_shapes=[pltpu.VMEM((BLK, 512), F32)],
        compiler_params=_cp("arbitrary", vmem=VMEM_BIG),
    )(u_qkv, u_qkv, u_qkv, u_qkv, u_qkv, u_za, d_o, qg, kg, sinks)


_SEGS = (SEG_CONV, SEG_Q, SEG_KV, SEG_ZA, SEG_GL)
N_GRAN = N_IN // CB


def _seg_spec(rows, seg, step_of):
    off, nblk = seg
    return pl.BlockSpec((rows, CB), lambda *g: (g[step_of[0]], jnp.clip(g[step_of[1]] - off, 0, nblk - 1)))


def _in_proj_bwd(du, w_full, x, g, dout):
    t = x.shape[0]
    tm = min(512, t)

    def body(a0, a1, a2, a3, a4, b_ref, x_ref, g_ref, do_ref, dx_ref, dg_ref, acc_ref):
        i, k = pl.program_id(0), pl.program_id(1)

        @pl.when((i == 0) & (k == 0))
        def _():
            dg_ref[...] = jnp.zeros_like(dg_ref)

        @pl.when(k == 0)
        def _():
            acc_ref[...] = jnp.zeros_like(acc_ref)

        for a_ref, (off, nblk) in zip((a0, a1, a2, a3, a4), _SEGS):
            @pl.when((k >= off) & (k < off + nblk))
            def _():
                acc_ref[...] += _dot_nt(a_ref[...], b_ref[...])

        @pl.when(k == N_GRAN - 1)
        def _():
            dh = acc_ref[...]
            xv = x_ref[...]
            r = lax.rsqrt(jnp.mean(xv * xv, axis=-1, keepdims=True) + EPS)
            xhat = xv * r
            dg_ref[...] += _fold8(dh * xhat)
            dxh = dh * g_ref[...]
            dx_ref[...] = do_ref[...] + r * (dxh - xhat * jnp.mean(dxh * xhat, axis=-1, keepdims=True))

    row = pl.BlockSpec((tm, D), lambda i, k: (i, 0))
    return pl.pallas_call(
        body, name="in_proj_bwd", grid=(t // tm, N_GRAN),
        in_specs=[_seg_spec(tm, s, (0, 1)) for s in _SEGS]
        + [pl.BlockSpec((D, CB), lambda i, k: (0, k)), row, pl.BlockSpec((1, D), lambda i, k: (0, 0)), row],
        out_specs=[row, pl.BlockSpec((8, D), lambda i, k: (0, 0))],
        out_shape=[SDS((t, D), F32), SDS((8, D), F32)],
        scratch_shapes=[pltpu.VMEM((tm, D), F32)],
        compiler_params=_cp("arbitrary", "arbitrary", vmem=VMEM_BIG),
    )(*du, w_full, x, g, dout)


def _in_proj_wgrad(h, du):
    t = h.shape[0]
    tk = min(1024, t)

    def body(h_ref, b0, b1, b2, b3, b4, g_ref):
        j, k = pl.program_id(0), pl.program_id(1)

        @pl.when(k == 0)
        def _():
            g_ref[...] = jnp.zeros_like(g_ref)

        for b_ref, (off, nblk) in zip((b0, b1, b2, b3, b4), _SEGS):
            @pl.when((j >= off) & (j < off + nblk))
            def _():
                g_ref[...] += _dot_tn(h_ref[...], b_ref[...])

    return pl.pallas_call(
        body, name="in_proj_wgrad", grid=(N_GRAN, t // tk),
        in_specs=[pl.BlockSpec((tk, D), lambda j, k: (k, 0))] + [_seg_spec(tk, s, (1, 0)) for s in _SEGS],
        out_specs=pl.BlockSpec((D, CB), lambda j, k: (0, j)), out_shape=SDS((D, N_IN), F32),
        compiler_params=_cp("parallel", "arbitrary", vmem=VMEM_BIG),
    )(h, *du)


def _swap_halves(g_in, g_sm):
    def body(*refs):
        ins, outs = refs[:2 * N_LAYERS], refs[2 * N_LAYERS:4 * N_LAYERS]
        send, recv = refs[4 * N_LAYERS:]
        x, y, c = _mesh_pos()
        cps = []
        for l in range(N_LAYERS):
            for a in range(2):
                s = 2 * l + a
                src = ins[s].at[1 - c] if a == 0 else ins[s].at[:, :, 1 - c]
                cp = pltpu.make_async_remote_copy(src_ref=src, dst_ref=outs[s], send_sem=send.at[s], recv_sem=recv.at[s],
                                                  device_id=(x, y, 1 - c), device_id_type=MESH)
                cp.start()
                cps.append(cp)
        for cp in cps:
            cp.wait()

    out_shape, args = [], []
    for l in range(N_LAYERS):
        out_shape += [SDS((512, N_IN), F32), SDS((3, 4, 128, D), F32)]
        args += [g_in[l], g_sm[l]]
    res = pl.pallas_call(
        body, name="swap_halves", in_specs=[ANY] * (2 * N_LAYERS), out_specs=[ANY] * (2 * N_LAYERS), out_shape=out_shape,
        scratch_shapes=[pltpu.SemaphoreType.DMA((2 * N_LAYERS,))] * 2,
    )(*args)
    return [(res[2 * l], res[2 * l + 1]) for l in range(N_LAYERS)]


def _add_halves_in(c_idx, g_in, r_in):
    def body(c_ref, a_ref, b_ref, f_ref, h_ref):
        s = a_ref[...] + b_ref[...]
        f_ref[...] = s
        h_ref[...] = s.astype(BF16)

    blk = pl.BlockSpec((128, N_IN), lambda i, c: (i, 0))
    return pl.pallas_call(
        body, name="add_halves_in",
        grid_spec=pltpu.PrefetchScalarGridSpec(
            num_scalar_prefetch=1, grid=(4,),
            in_specs=[pl.BlockSpec((None, 128, N_IN), lambda i, c: (c[0], i, 0)), blk], out_specs=[blk, blk]),
        out_shape=[SDS((512, N_IN), F32), SDS((512, N_IN), BF16)], compiler_params=_cp("parallel", vmem=VMEM_BIG),
    )(c_idx, g_in, r_in)


def _add_halves_sm(c_idx, g_sm, r_sm):
    def body(c_ref, a_ref, b_ref, f_ref, h_ref):
        s = a_ref[...] + b_ref[...]
        f_ref[...] = s
        h_ref[...] = s.astype(BF16)

    blk = pl.BlockSpec((1, 4, 128, D), lambda a, c: (a, 0, 0, 0))
    return pl.pallas_call(
        body, name="add_halves_sm",
        grid_spec=pltpu.PrefetchScalarGridSpec(
            num_scalar_prefetch=1, grid=(3,),
            in_specs=[pl.BlockSpec((1, 4, None, 128, D), lambda a, c: (a, 0, c[0], 0, 0)), blk], out_specs=[blk, blk]),
        out_shape=[SDS((3, 4, 128, D), F32), SDS((3, 4, 128, D), BF16)], compiler_params=_cp("parallel"),
    )(c_idx, g_sm, r_sm)


def _scatter_chips(h_in, h_sm):
    n_cp = N_LAYERS * 2 * 3

    def body(*refs):
        ins, outs = refs[:2 * N_LAYERS], refs[2 * N_LAYERS:4 * N_LAYERS]
        send, recv = refs[4 * N_LAYERS:]
        x, y, c = _mesh_pos()
        cps = []
        for l in range(N_LAYERS):
            for a in range(2):
                for k, chip in enumerate(_other_chips(x, y)):
                    s = (2 * l + a) * 3 + k
                    their = 2 * chip[0] + chip[1]
                    if a == 0:
                        src = ins[2 * l].at[:, pl.ds(pl.multiple_of(their * SH_IN, 128), SH_IN)]
                    else:
                        src = ins[2 * l + 1].at[:, their]
                    cp = pltpu.make_async_remote_copy(src_ref=src, dst_ref=outs[2 * l + a].at[k], send_sem=send.at[s],
                                                      recv_sem=recv.at[s], device_id=(*chip, c), device_id_type=MESH)
                    cp.start()
                    cps.append(cp)
        for cp in cps:
            cp.wait()

    out_shape, args = [], []
    for l in range(N_LAYERS):
        out_shape += [SDS((3, 512, SH_IN), BF16), SDS((3, 3, 128, D), BF16)]
        args += [h_in[l], h_sm[l]]
    res = pl.pallas_call(
        body, name="scatter_chips", in_specs=[ANY] * (2 * N_LAYERS), out_specs=[ANY] * (2 * N_LAYERS), out_shape=out_shape,
        scratch_shapes=[pltpu.SemaphoreType.DMA((n_cp,))] * 2,
    )(*args)
    return [(res[2 * l], res[2 * l + 1]) for l in range(N_LAYERS)]


def _final_sum_in(chip_idx, f_in, r_in):
    def body(j_ref, a_ref, r_ref, o_ref):
        o_ref[...] = a_ref[...] + r_ref[0].astype(F32) + r_ref[1].astype(F32) + r_ref[2].astype(F32)

    return pl.pallas_call(
        body, name="final_sum_in",
        grid_spec=pltpu.PrefetchScalarGridSpec(
            num_scalar_prefetch=1, grid=(4,),
            in_specs=[pl.BlockSpec((128, SH_IN), lambda i, j: (i, j[0])), pl.BlockSpec((3, 128, SH_IN), lambda i, j: (0, i, 0))],
            out_specs=pl.BlockSpec((128, SH_IN), lambda i, j: (i, 0))),
        out_shape=SDS((512, SH_IN), F32), compiler_params=_cp("parallel"),
    )(chip_idx, f_in, r_in)


def _final_sum_sm(chip_idx, f_sm, r_sm):
    def body(j_ref, a_ref, r_ref, o_ref):
        o_ref[...] = a_ref[...] + r_ref[0].astype(F32) + r_ref[1].astype(F32) + r_ref[2].astype(F32)

    return pl.pallas_call(
        body, name="final_sum_sm",
        grid_spec=pltpu.PrefetchScalarGridSpec(
            num_scalar_prefetch=1, grid=(3,),
            in_specs=[pl.BlockSpec((1, None, 128, D), lambda a, j: (a, j[0], 0, 0)),
                      pl.BlockSpec((3, 1, 128, D), lambda a, j: (0, a, 0, 0))],
            out_specs=pl.BlockSpec((1, 128, D), lambda a, j: (a, 0, 0))),
        out_shape=SDS((3, 128, D), F32), compiler_params=_cp("parallel"),
    )(chip_idx, f_sm, r_sm)


def _join_halves(t_in, t_sm):
    n_cp = N_LAYERS * 4

    def body(*refs):
        ins, outs = refs[:2 * N_LAYERS], refs[2 * N_LAYERS:2 * N_LAYERS + 4]
        send, recv, loc_in, loc_out, stage_in, stage_sm = refs[2 * N_LAYERS + 4:]
        x, y, c = _mesh_pos()
        cps, own = [], []
        for l in range(N_LAYERS):
            for a in range(4):
                s = 4 * l + a
                if a == 0:
                    src = ins[2 * l]
                    dst = outs[0].at[l, pl.ds(pl.multiple_of(c * 512, 512), 512), :]
                else:
                    src = ins[2 * l + 1].at[a - 1]
                    dst = outs[a].at[l, pl.ds(pl.multiple_of(c * 128, 128), 128), :]
                own.append((src, dst, min(a, 1)))
                cp = pltpu.make_async_remote_copy(src_ref=src, dst_ref=dst, send_sem=send.at[s], recv_sem=recv.at[s],
                                                  device_id=(x, y, 1 - c), device_id_type=MESH)
                cp.start()
                cps.append(cp)
        _staged_copies(own, (stage_in, stage_sm), loc_in, loc_out)
        for l in range(N_LAYERS):
            for a in range(4):
                s = 4 * l + a
                if a == 0:
                    got = outs[0].at[l, pl.ds(pl.multiple_of((1 - c) * 512, 512), 512), :]
                else:
                    got = outs[a].at[l, pl.ds(pl.multiple_of((1 - c) * 128, 128), 128), :]
                pltpu.make_async_remote_copy(src_ref=got, dst_ref=got, send_sem=send.at[s], recv_sem=recv.at[s],
                                             device_id=(x, y, 1 - c), device_id_type=MESH).wait_recv()
        for cp in cps:
            cp.wait_send()

    args = []
    for l in range(N_LAYERS):
        args += [t_in[l], t_sm[l]]
    sm = SDS((N_LAYERS, SH_ROW, D), F32)
    return pl.pallas_call(
        body, name="join_halves", in_specs=[ANY] * (2 * N_LAYERS), out_specs=[ANY] * 4,
        out_shape=[SDS((N_LAYERS, D, SH_IN), F32), sm, sm, sm],
        scratch_shapes=[pltpu.SemaphoreType.DMA((n_cp,))] * 4
        + [pltpu.VMEM((2, 512, SH_IN), F32), pltpu.VMEM((2, 128, D), F32)],
        compiler_params=_cp(vmem=VMEM_BIG),
    )(*args)


def _adam_math(w, g, m, v):
    m = ADAM_B1 * m + (1.0 - ADAM_B1) * g
    v = ADAM_B2 * v + (1.0 - ADAM_B2) * (g * g)
    m_hat = m / (1.0 - ADAM_B1 ** ADAM_STEP)
    v_hat = v / (1.0 - ADAM_B2 ** ADAM_STEP)
    delta = -ADAM_LR * (m_hat / (jnp.sqrt(v_hat) + ADAM_EPS) + ADAM_WD * w)
    return delta, m, v


def _adamw_big(w, g, m, v, name):
    rows, cols = w.shape
    tr = 128

    def body(w_ref, g_ref, m_ref, v_ref, d_ref, nm_ref, nv_ref):
        d_ref[...], nm_ref[...], nv_ref[...] = _adam_math(w_ref[...], g_ref[...], m_ref[...], v_ref[...])

    blk = pl.BlockSpec((tr, cols), lambda i: (i, 0))
    return pl.pallas_call(
        body, name=name, grid=(rows // tr,), in_specs=[blk] * 4, out_specs=[blk] * 3,
        out_shape=[SDS((rows, cols), F32)] * 3, compiler_params=_cp("parallel", vmem=VMEM_BIG),
    )(w, g, m, v)


def _adamw_small(ws, gs, ms, vs):
    n = len(ws)

    def body(*refs):
        for k in range(n):
            w_ref, g_ref, m_ref, v_ref = (refs[q * n + k] for q in range(4))
            d, nm, nv = _adam_math(w_ref[...], g_ref[...], m_ref[...], v_ref[...])
            refs[4 * n + k][...] = d
            refs[5 * n + k][...] = nm
            refs[6 * n + k][...] = nv

    vm = pl.BlockSpec(memory_space=pltpu.VMEM)
    shapes = [SDS(w.shape, F32) for w in ws]
    res = pl.pallas_call(
        body, name="adamw_small", in_specs=[vm] * (4 * n), out_specs=[vm] * (3 * n), out_shape=shapes * 3,
    )(*ws, *gs, *ms, *vs)
    return res[:n], res[n:2 * n], res[2 * n:]


def _pad_rows(a, rows):
    flat = a.reshape(-1)
    return jnp.pad(flat, (0, rows * 128 - flat.shape[0])).reshape(rows, 128)


def kernel(x, norm_g, w_in, conv_w, q_norm_g, k_norm_g, sinks, w_conv_out, w_attn_out, gate_b, w_out, loss_target, m_norm_g, m_w_in, m_conv_w, m_q_norm_g, m_k_norm_g, m_sinks, m_w_conv_out, m_w_attn_out, m_gate_b, m_w_out, v_norm_g, v_w_in, v_conv_w, v_q_norm_g, v_k_norm_g, v_sinks, v_w_conv_out, v_w_attn_out, v_gate_b, v_w_out):
    xi, yi, ci = _mesh_pos()
    chip = 2 * xi + yi
    c_idx = jnp.reshape(ci, (1,)).astype(jnp.int32)
    chip_idx = jnp.reshape(chip, (1,)).astype(jnp.int32)
    t = x.shape[1]
    xs = [x.reshape(t, D)]
    tgt = loss_target.reshape(t, D)

    full_w = _gather_weights(_cast_w_in(w_in), _cast_w_small(w_conv_out, w_attn_out, w_out))
    placed = lax.dynamic_update_slice(jnp.zeros((N_LAYERS, 3, D), F32),
                                      jnp.where(ci == 0, conv_w, 0.0), (0, 0, chip * SH_ROW))
    conv_full = _allreduce_small(placed.reshape(96, 128)).reshape(N_LAYERS, 3, D)

    saved = []
    for l in range(N_LAYERS):
        w_full, w_sm = full_w[l]
        h = _rmsnorm_fwd(xs[l], norm_g[l:l + 1])
        u_conv = _in_proj(h, w_full, (0, 8), "in_proj_conv")
        u_qkv = _in_proj(h, w_full, (8, 3), "in_proj_qkv")
        u_za = _in_proj(h, w_full, SEG_ZA, "in_proj_za")
        u_gl = _in_proj(h, w_full, SEG_GL, "in_proj_gl")
        y_c = _conv_fwd(u_conv, conv_full[l])
        o = _attn_fwd(u_qkv, u_za, q_norm_g[l:l + 1], k_norm_g[l:l + 1], sinks[l:l + 1])
        x_next, y_a, y_b, merged = _out_proj_fwd(xs[l], y_c, o, u_gl, gate_b[l:l + 1], w_sm)
        xs.append(x_next)
        saved.append((h, u_conv, u_qkv, u_za, u_gl, y_c, o, y_a, y_b, merged))

    dout, sq = _loss_head(xs[N_LAYERS], tgt)
    loss = lax.psum(jnp.sum(sq) * (0.5 / D), ("x", "y", "c"))

    g_in, g_sm, small = [None] * N_LAYERS, [None] * N_LAYERS, [None] * N_LAYERS
    for l in reversed(range(N_LAYERS)):
        w_full, w_sm = full_w[l]
        h, u_conv, u_qkv, u_za, u_gl, y_c, o, y_a, y_b, merged = saved[l]
        d_ya, d_yb, du_gl, d_yc, d_o, dgb = _out_proj_bwd(dout, y_a, y_b, u_gl, gate_b[l:l + 1], w_sm)
        g_sm[l] = _small_wgrads(y_c, d_ya, o, d_yb, merged, dout)
        du_conv, dcw = _conv_bwd(d_yc, u_conv, conv_full[l])
        du_q, du_kv, du_za, dqg, dkg, dsk = _attn_bwd(d_o, u_qkv, u_za, q_norm_g[l:l + 1], k_norm_g[l:l + 1],
                                                     sinks[l:l + 1])
        du = (du_conv, du_q, du_kv, du_za, du_gl)
        g_in[l] = _in_proj_wgrad(h, du)
        dout, dng = _in_proj_bwd(du, w_full, xs[l], norm_g[l:l + 1], dout)
        small[l] = (jnp.sum(dng, axis=0), dqg[0], dkg[0], dsk[0, :N_Q], jnp.sum(dgb, axis=0), dcw[:3])
    grad_x = dout.reshape(1, t, D)

    stack = lambda k: jnp.stack([small[l][k] for l in range(N_LAYERS)])
    pack = jnp.concatenate([_pad_rows(stack(0), 32), _pad_rows(stack(1), 8), _pad_rows(stack(2), 8),
                            _pad_rows(stack(3), 8), _pad_rows(stack(4), 64), _pad_rows(stack(5), 96)], axis=0)
    red = _allreduce_small(pack)
    g_norm_g = red[0:32].reshape(N_LAYERS, D)
    g_q_norm_g = red[32:40].reshape(-1)[:N_LAYERS * HEAD].reshape(N_LAYERS, HEAD)
    g_k_norm_g = red[40:48].reshape(-1)[:N_LAYERS * HEAD].reshape(N_LAYERS, HEAD)
    g_sinks = red[48:56].reshape(-1)[:N_LAYERS * N_Q].reshape(N_LAYERS, N_Q)
    g_gate_b = red[56:120].reshape(N_LAYERS, 2 * D)
    g_conv_full = red[120:216].reshape(N_LAYERS, 3, D)
    g_conv_w = lax.dynamic_slice(g_conv_full, (0, 0, chip * SH_ROW), (N_LAYERS, 3, SH_ROW))

    g_in_v = [g.reshape(2, 512, N_IN) for g in g_in]
    g_sm_v = [g.reshape(3, 4, 2, 128, D) for g in g_sm]
    got = _swap_halves(g_in_v, g_sm_v)
    h_in, h_sm, f_in, f_sm = [], [], [], []
    for l in range(N_LAYERS):
        f, hb = _add_halves_in(c_idx, g_in_v[l], got[l][0])
        f_in.append(f)
        h_in.append(hb)
        f, hb = _add_halves_sm(c_idx, g_sm_v[l], got[l][1])
        f_sm.append(f)
        h_sm.append(hb)
    parts = _scatter_chips(h_in, h_sm)
    t_in = [_final_sum_in(chip_idx, f_in[l], parts[l][0]) for l in range(N_LAYERS)]
    t_sm = [_final_sum_sm(chip_idx, f_sm[l], parts[l][1]) for l in range(N_LAYERS)]
    g_w_in, g_w_co, g_w_ao, g_w_out = _join_halves(t_in, t_sm)

    r_in = N_LAYERS * D
    d_in, nm_in, nv_in = (a.reshape(N_LAYERS, D, SH_IN) for a in _adamw_big(
        w_in.reshape(r_in, SH_IN), g_w_in.reshape(r_in, SH_IN), m_w_in.reshape(r_in, SH_IN),
        v_w_in.reshape(r_in, SH_IN), "adamw_w_in"))
    r_sm = N_LAYERS * SH_ROW
    big = {}
    for nm, w, g, m, v in (("co", w_conv_out, g_w_co, m_w_conv_out, v_w_conv_out),
                           ("ao", w_attn_out, g_w_ao, m_w_attn_out, v_w_attn_out),
                           ("out", w_out, g_w_out, m_w_out, v_w_out)):
        big[nm] = tuple(a.reshape(N_LAYERS, SH_ROW, D) for a in _adamw_big(
            w.reshape(r_sm, D), g.reshape(r_sm, D), m.reshape(r_sm, D), v.reshape(r_sm, D), "adamw_w_small"))
    sm_w = [norm_g, conv_w, q_norm_g, k_norm_g, sinks, gate_b]
    sm_g = [g_norm_g, g_conv_w, g_q_norm_g, g_k_norm_g, g_sinks, g_gate_b]
    sm_m = [m_norm_g, m_conv_w, m_q_norm_g, m_k_norm_g, m_sinks, m_gate_b]
    sm_v = [v_norm_g, v_conv_w, v_q_norm_g, v_k_norm_g, v_sinks, v_gate_b]
    sd, snm, snv = _adamw_small(sm_w, sm_g, sm_m, sm_v)

    def order(norm, w_in_, conv, qn, kn, sk, co, ao, gb, wo):
        return [norm, w_in_, conv, qn, kn, sk, co, ao, gb, wo]

    grads = order(g_norm_g, g_w_in, g_conv_w, g_q_norm_g, g_k_norm_g, g_sinks, g_w_co, g_w_ao, g_gate_b, g_w_out)
    deltas = order(sd[0], d_in, sd[1], sd[2], sd[3], sd[4], big["co"][0], big["ao"][0], sd[5], big["out"][0])
    new_m = order(snm[0], nm_in, snm[1], snm[2], snm[3], snm[4], big["co"][1], big["ao"][1], snm[5], big["out"][1])
    new_v = order(snv[0], nv_in, snv[1], snv[2], snv[3], snv[4], big["co"][2], big["ao"][2], snv[5], big["out"][2])
    return (loss, grad_x, *grads, *deltas, *new_m, *new_v)
```

```python
import functools

import jax
import jax.numpy as jnp
from jax import lax
from jax.experimental import pallas as pl
from jax.experimental.pallas import tpu as pltpu

F32, BF16 = jnp.float32, jnp.bfloat16
SDS = jax.ShapeDtypeStruct
MESH = pl.DeviceIdType.MESH
ANY = pl.BlockSpec(memory_space=pl.ANY)

D = 1024
N_IN = 8704
N_LAYERS = 4
N_Q, N_KV, HEAD = 16, 4, 64
GROUP = N_Q // N_KV
BLK = 128
EPS = 1e-6
NEG = -1e30
SCALE = HEAD ** -0.5
SH_IN = N_IN // 4
SH_ROW = D // 4
CB = 512
SEG_CONV, SEG_Q, SEG_KV, SEG_ZA, SEG_GL = (0, 8), (8, 2), (10, 1), (11, 2), (13, 4)
VMEM_BIG = 56 * 1024 * 1024

ADAM_LR, ADAM_B1, ADAM_B2, ADAM_EPS, ADAM_WD, ADAM_STEP = 0.001, 0.9, 0.999, 1e-08, 0.01, 10


def _cp(*sem, vmem=None):
    return pltpu.CompilerParams(dimension_semantics=sem if sem else None, vmem_limit_bytes=vmem)


def _sigmoid(z):
    return 1.0 / (1.0 + jnp.exp(-z))


def _dot(a, b):
    return jnp.dot(a, b, preferred_element_type=F32)


def _dot_nt(a, b):
    return lax.dot_general(a, b, (((1,), (1,)), ((), ())), preferred_element_type=F32)


def _dot_tn(a, b):
    return lax.dot_general(a, b, (((0,), (0,)), ((), ())), preferred_element_type=F32)


def _rms(xh):
    r = lax.rsqrt(jnp.mean(xh * xh, axis=-1, keepdims=True) + EPS)
    return xh * r, r


def _fold8(v):
    return jnp.sum(v.reshape(v.shape[0] // 8, 8, v.shape[1]), axis=0)


def _cast_w_in(w):
    rows = N_LAYERS * D

    def body(i_ref, o_ref):
        o_ref[...] = i_ref[...].astype(BF16)

    out = pl.pallas_call(
        body, name="cast_w_in", grid=(rows // 512,),
        in_specs=[pl.BlockSpec((512, SH_IN), lambda i: (i, 0))],
        out_specs=pl.BlockSpec((512, SH_IN), lambda i: (i, 0)),
        out_shape=SDS((rows, SH_IN), BF16), compiler_params=_cp("parallel"),
    )(w.reshape(rows, SH_IN))
    return out.reshape(N_LAYERS, D, SH_IN)


def _cast_w_small(a, b, c):
    def body(a_ref, b_ref, c_ref, o_ref):
        o_ref[0, 0] = a_ref[0].astype(BF16)
        o_ref[0, 1] = b_ref[0].astype(BF16)
        o_ref[0, 2] = c_ref[0].astype(BF16)

    spec = pl.BlockSpec((1, SH_ROW, D), lambda l: (l, 0, 0))
    return pl.pallas_call(
        body, name="cast_w_small", grid=(N_LAYERS,), in_specs=[spec, spec, spec],
        out_specs=pl.BlockSpec((1, 3, SH_ROW, D), lambda l: (l, 0, 0, 0)),
        out_shape=SDS((N_LAYERS, 3, SH_ROW, D), BF16), compiler_params=_cp("parallel"),
    )(a, b, c)


def _mesh_pos():
    return lax.axis_index("x"), lax.axis_index("y"), lax.axis_index("c")


def _other_chips(x, y):
    return [(1 - x, y), (x, 1 - y), (1 - x, 1 - y)]


def _staged_copies(copies, stages, sem_in, sem_out):
    busy, count = {}, {}
    for idx, (src, dst, kind) in enumerate(copies):
        slot = count.get(kind, 0) % 2
        count[kind] = count.get(kind, 0) + 1
        if (kind, slot) in busy:
            busy.pop((kind, slot)).wait()
        buf = stages[kind].at[slot]
        cin = pltpu.make_async_copy(src, buf, sem_in.at[idx])
        cin.start()
        cin.wait()
        cout = pltpu.make_async_copy(buf, dst, sem_out.at[idx])
        cout.start()
        busy[(kind, slot)] = cout
    for cp in busy.values():
        cp.wait()


def _gather_weights(win_b, wsm_b):
    n_cp = N_LAYERS * 2 * 3

    def body(win, wsm, *rest):
        outs = rest[:2 * N_LAYERS]
        send_a, recv_a, send_b, recv_b, loc_in, loc_out, stage_in, stage_sm = rest[2 * N_LAYERS:]
        x, y, c = _mesh_pos()
        me_chip = 2 * x + y
        sib = (x, y, 1 - c)
        chips = _other_chips(x, y)

        def src_half(l, a, half):
            if a == 0:
                return win.at[l, pl.ds(pl.multiple_of(half * 512, 512), 512), :]
            return wsm.at[l, :, pl.ds(pl.multiple_of(half * 128, 128), 128), :]

        def region(l, a, chip, half):
            full = outs[2 * l + a]
            if a == 0:
                return full.at[pl.ds(pl.multiple_of(half * 512, 512), 512),
                               pl.ds(pl.multiple_of(chip * SH_IN, 128), SH_IN)]
            return full.at[:, pl.ds(pl.multiple_of(chip * SH_ROW + half * 128, 128), 128), :]

        def own_region(l, a):
            full = outs[2 * l + a]
            if a == 0:
                return full.at[:, pl.ds(pl.multiple_of(me_chip * SH_IN, 128), SH_IN)]
            return full.at[:, pl.ds(pl.multiple_of(me_chip * SH_ROW, 128), SH_ROW), :]

        def rcopy(src, dst, ssem, rsem, to):
            return pltpu.make_async_remote_copy(src_ref=src, dst_ref=dst, send_sem=ssem, recv_sem=rsem,
                                                device_id=to, device_id_type=MESH)

        first, passed = [], []
        for l in range(N_LAYERS):
            for a in range(2):
                for k, chip in enumerate(chips):
                    s = (2 * l + a) * 3 + k
                    cp = rcopy(src_half(l, a, c), region(l, a, me_chip, c), send_a.at[s], recv_a.at[s], (*chip, c))
                    cp.start()
                    first.append(cp)
        own = [(win.at[l] if a == 0 else wsm.at[l], own_region(l, a), a) for l in range(N_LAYERS) for a in range(2)]
        _staged_copies(own, (stage_in, stage_sm), loc_in, loc_out)
        for l in range(N_LAYERS):
            for a in range(2):
                for k, chip in enumerate(chips):
                    s = (2 * l + a) * 3 + k
                    their = 2 * chip[0] + chip[1]
                    landed = region(l, a, their, c)
                    rcopy(landed, landed, send_a.at[s], recv_a.at[s], sib).wait_recv()
                    cp = rcopy(landed, landed, send_b.at[s], recv_b.at[s], sib)
                    cp.start()
                    passed.append(cp)
        for l in range(N_LAYERS):
            for a in range(2):
                for k, chip in enumerate(chips):
                    s = (2 * l + a) * 3 + k
                    their = 2 * chip[0] + chip[1]
                    other = region(l, a, their, 1 - c)
                    rcopy(other, other, send_b.at[s], recv_b.at[s], sib).wait_recv()
        for cp in first + passed:
            cp.wait_send()

    out_shape = []
    for _ in range(N_LAYERS):
        out_shape += [SDS((D, N_IN), BF16), SDS((3, D, D), BF16)]
    res = pl.pallas_call(
        body, name="gather_weights", in_specs=[ANY, ANY], out_specs=[ANY] * (2 * N_LAYERS), out_shape=out_shape,
        scratch_shapes=[pltpu.SemaphoreType.DMA((n_cp,))] * 4 + [pltpu.SemaphoreType.DMA((2 * N_LAYERS,))] * 2
        + [pltpu.VMEM((2, D, SH_IN), BF16), pltpu.VMEM((2, 3, SH_ROW, D), BF16)],
        compiler_params=_cp(vmem=VMEM_BIG),
    )(win_b, wsm_b)
    return [(res[2 * l], res[2 * l + 1]) for l in range(N_LAYERS)]


def _allreduce_small(pack):
    rows = pack.shape[0]

    def body(p_ref, o_ref, buf, send, recv):
        x, y, c = _mesh_pos()
        me = 4 * x + 2 * y + c
        sends = []
        for r in range(1, 8):
            to = (x if not (r & 4) else 1 - x, y if not (r & 2) else 1 - y, c if not (r & 1) else 1 - c)
            cp = pltpu.make_async_remote_copy(src_ref=p_ref, dst_ref=buf.at[me], send_sem=send.at[r - 1],
                                              recv_sem=recv.at[r - 1], device_id=to, device_id_type=MESH)
            cp.start()
            sends.append(cp)
        buf[me] = p_ref[...]
        for r in range(1, 8):
            frm = (4 * x + 2 * y + c) ^ r
            pltpu.make_async_remote_copy(src_ref=p_ref, dst_ref=buf.at[frm], send_sem=send.at[r - 1],
                                         recv_sem=recv.at[r - 1], device_id=(x, y, c), device_id_type=MESH).wait_recv()
        acc = buf[0]
        for d in range(1, 8):
            acc = acc + buf[d]
        o_ref[...] = acc
        for cp in sends:
            cp.wait_send()

    vm = pl.BlockSpec(memory_space=pltpu.VMEM)
    return pl.pallas_call(
        body, name="allreduce_small", in_specs=[vm], out_specs=vm, out_shape=SDS((rows, 128), F32),
        scratch_shapes=[pltpu.VMEM((8, rows, 128), F32), pltpu.SemaphoreType.DMA((7,)), pltpu.SemaphoreType.DMA((7,))],
    )(pack)


def _rmsnorm_fwd(x, g):
    t = x.shape[0]
    tm = min(512, t)

    def body(x_ref, g_ref, h_ref, ht_ref):
        xv = x_ref[...]
        r = lax.rsqrt(jnp.mean(xv * xv, axis=-1, keepdims=True) + EPS)
        h = xv * r * g_ref[...]
        h_ref[...] = h.astype(BF16)
        ht_ref[...] = h.T.astype(BF16)

    return pl.pallas_call(
        body, name="rmsnorm_fwd", grid=(t // tm,),
        in_specs=[pl.BlockSpec((tm, D), lambda i: (i, 0)), pl.BlockSpec((1, D), lambda i: (0, 0))],
        out_specs=[pl.BlockSpec((tm, D), lambda i: (i, 0)), pl.BlockSpec((D, tm), lambda i: (0, i))],
        out_shape=[SDS((t, D), BF16), SDS((D, t), BF16)],
        compiler_params=_cp("parallel", vmem=VMEM_BIG),
    )(x, g)


def _in_proj(h, w_full, seg, name):
    t = h.shape[0]
    off, nblk = seg
    tm = min(2048, t)

    def body(a_ref, b_ref, o_ref):
        o_ref[...] = _dot(a_ref[...], b_ref[...]).astype(BF16)

    return pl.pallas_call(
        body, name=name, grid=(t // tm, nblk),
        in_specs=[pl.BlockSpec((tm, D), lambda i, j: (i, 0)), pl.BlockSpec((D, CB), lambda i, j: (0, off + j))],
        out_specs=pl.BlockSpec((tm, CB), lambda i, j: (i, j)), out_shape=SDS((t, nblk * CB), BF16),
        compiler_params=_cp("parallel", "parallel", vmem=VMEM_BIG),
    )(h, w_full)


def _conv_fwd(u_conv, conv_w):
    t = u_conv.shape[0]
    tm = min(256, t)
    hb = tm // 16

    def body(v_ref, b_ref, c_ref, z_ref, hv_ref, hc_ref, w_ref, y_ref):
        i = pl.program_id(0)
        cv = c_ref[...].astype(F32) * v_ref[...].astype(F32)
        halo = hc_ref[...].astype(F32) * hv_ref[...].astype(F32)
        halo = jnp.where(i > 0, halo, 0.0)
        row = lax.broadcasted_iota(jnp.int32, (tm, 1), 0)
        s1 = jnp.where(row == 0, halo[15:16], pltpu.roll(cv, 1, 0))
        s2 = jnp.where(row == 0, halo[14:15], jnp.where(row == 1, halo[15:16], pltpu.roll(cv, 2, 0)))
        conv = w_ref[0:1, :] * s2 + w_ref[1:2, :] * s1 + w_ref[2:3, :] * cv
        z = z_ref[...].astype(F32)
        y_ref[...] = (b_ref[...].astype(F32) * conv * (z * _sigmoid(z))).astype(BF16)

    def col(k):
        return pl.BlockSpec((tm, D), lambda i: (i, k))

    def halo(k):
        return pl.BlockSpec((16, D), lambda i: (jnp.maximum(i * hb - 1, 0), k))

    return pl.pallas_call(
        body, name="conv_fwd", grid=(t // tm,),
        in_specs=[col(0), col(1), col(2), col(3), halo(0), halo(2), pl.BlockSpec((3, D), lambda i: (0, 0))],
        out_specs=pl.BlockSpec((tm, D), lambda i: (i, 0)), out_shape=SDS((t, D), BF16),
        compiler_params=_cp("parallel", vmem=VMEM_BIG),
    )(u_conv, u_conv, u_conv, u_conv, u_conv, u_conv, conv_w)


KVX = 4 * N_KV * 128


def _iota2(shape):
    return lax.broadcasted_iota(jnp.int32, shape, 0), lax.broadcasted_iota(jnp.int32, shape, 1)


def _head_sum(v):
    r, c = _iota2((128, 128))
    ones = ((r >> 6) == (c >> 6)).astype(BF16)
    hi = v.astype(BF16)
    lo = (v - hi.astype(F32)).astype(BF16)
    return jnp.concatenate([_dot(hi[:, g:g + 128], ones) + _dot(lo[:, g:g + 128], ones)
                            for g in range(0, v.shape[1], 128)], axis=1)


def _expand_mats():
    r, c = _iota2((N_KV * HEAD, N_KV * 128))
    base = ((r >> 6) << 7) + (r & 63)
    return (c == base).astype(BF16), (c == base + 64).astype(BF16)


def _fold_mat():
    r, c = _iota2((N_KV * 128, N_KV * HEAD))
    return (((r >> 7) == (c >> 6)) & ((r & 63) == (c & 63))).astype(BF16)


def _qkv_prep(u_qkv, qg_s, kg_t):
    t = u_qkv.shape[0]
    tm = min(512, t)

    def body(u_ref, qg_ref, kg_ref, qs_ref, kvx_ref):
        q = u_ref[:, 0:D].astype(F32)
        rq = lax.rsqrt(_head_sum(q * q) * (1.0 / HEAD) + EPS)
        qs_ref[...] = (q * rq * qg_ref[...]).astype(BF16)
        k = u_ref[:, D:D + 256].astype(F32)
        rk = lax.rsqrt(_head_sum(k * k) * (1.0 / HEAD) + EPS)
        kn = (k * rk * kg_ref[...]).astype(BF16)
        v = u_ref[:, D + 256:D + 512]
        e_lo, e_hi = _expand_mats()
        kvx_ref[:, 0:512] = _dot(kn, e_lo).astype(BF16)
        kvx_ref[:, 512:1024] = _dot(kn, e_hi).astype(BF16)
        kvx_ref[:, 1024:1536] = _dot(v, e_lo).astype(BF16)
        kvx_ref[:, 1536:2048] = _dot(v, e_hi).astype(BF16)

    return pl.pallas_call(
        body, name="qkv_prep", grid=(t // tm,),
        in_specs=[pl.BlockSpec((tm, 1536), lambda i: (i, 0)), pl.BlockSpec((1, D), lambda i: (0, 0)),
                  pl.BlockSpec((1, 256), lambda i: (0, 0))],
        out_specs=[pl.BlockSpec((tm, D), lambda i: (i, 0)), pl.BlockSpec((tm, KVX), lambda i: (i, 0))],
        out_shape=[SDS((t, D), BF16), SDS((t, KVX), BF16)], compiler_params=_cp("parallel", vmem=VMEM_BIG),
    )(u_qkv, qg_s, kg_t)


def _band_mask(n):
    r, j = _iota2((2 * BLK, 2 * BLK))
    diff = (r & (BLK - 1)) - j + BLK
    return (diff >= 0) & (diff < BLK) & ((j >= BLK) | (n > 0))


def _pair_rows(ref_or_val, hk):
    return jnp.concatenate([ref_or_val[:, 256 * hk:256 * hk + 128], ref_or_val[:, 256 * hk + 128:256 * hk + 256]], axis=0)


def _sink_col(sink_ref, hk, half):
    return jnp.concatenate([jnp.full((BLK, 1), sink_ref[0, GROUP * hk + half], F32),
                            jnp.full((BLK, 1), sink_ref[0, GROUP * hk + 2 + half], F32)], axis=0)


def _softmax_half(qpp, kvb, hk, half, valid, sink_ref):
    kx = kvb[:, 512 * half + 128 * hk:512 * half + 128 * hk + 128]
    vx = kvb[:, 1024 + 512 * half + 128 * hk:1024 + 512 * half + 128 * hk + 128]
    s = jnp.where(valid, _dot_nt(qpp, kx), NEG)
    sink = _sink_col(sink_ref, hk, half)
    m = jnp.maximum(jnp.max(s, axis=-1, keepdims=True), sink)
    p = jnp.exp(s - m)
    es = jnp.exp(sink - m)
    inv = 1.0 / (jnp.sum(p, axis=-1, keepdims=True) + es)
    return p, es, inv, kx, vx


def _attn_fwd(qs, kvx, u_za, sinks):
    t = qs.shape[0]
    nb = t // BLK

    def body(q_ref, kc_ref, kp_ref, za_ref, sink_ref, o_ref):
        n = pl.program_id(0)
        kvb = jnp.concatenate([kp_ref[...], kc_ref[...]], axis=0)
        valid = _band_mask(n)
        cols = []
        for hk in range(N_KV):
            qpp = _pair_rows(q_ref, hk)
            opp = None
            for half in range(2):
                p, _, inv, _, vx = _softmax_half(qpp, kvb, hk, half, valid, sink_ref)
                o = _dot(p.astype(BF16), vx) * inv
                opp = o if opp is None else opp + o
            cols += [opp[:BLK], opp[BLK:]]
        za = za_ref[...].astype(F32)
        o_ref[...] = (jnp.concatenate(cols, axis=1) * (za * _sigmoid(za))).astype(BF16)

    prev = lambda n: jnp.maximum(n - 1, 0)
    return pl.pallas_call(
        body, name="attn_fwd", grid=(nb,),
        in_specs=[pl.BlockSpec((BLK, D), lambda n: (n, 0)),
                  pl.BlockSpec((BLK, KVX), lambda n: (n, 0)), pl.BlockSpec((BLK, KVX), lambda n: (prev(n), 0)),
                  pl.BlockSpec((BLK, D), lambda n: (n, 0)), pl.BlockSpec(memory_space=pltpu.SMEM)],
        out_specs=pl.BlockSpec((BLK, D), lambda n: (n, 0)), out_shape=SDS((t, D), BF16),
        compiler_params=_cp("parallel", vmem=VMEM_BIG),
    )(qs, kvx, kvx, u_za, sinks)


def _out_proj_fwd(x, y_c, o, u_gl, gate_b, w_sm):
    t = x.shape[0]
    tm = min(512, t)

    def body(x_ref, yc_ref, o_ref, gla_ref, glb_ref, gb_ref, wco_ref, wao_ref, wout_ref,
             xn_ref, ya_ref, yb_ref, mg_ref):
        ya = _dot(yc_ref[...], wco_ref[...])
        yb = _dot(o_ref[...], wao_ref[...])
        gb = gb_ref[...]
        ga_ = _sigmoid(gla_ref[...].astype(F32) + gb[:, :D])
        gb_ = _sigmoid(glb_ref[...].astype(F32) + gb[:, D:])
        merged = (ga_ * ya + gb_ * yb).astype(BF16)
        ya_ref[...] = ya.astype(BF16)
        yb_ref[...] = yb.astype(BF16)
        mg_ref[...] = merged
        xn_ref[...] = x_ref[...] + _dot(merged, wout_ref[...])

    row = pl.BlockSpec((tm, D), lambda i: (i, 0))
    wspec = lambda a: pl.BlockSpec((None, D, D), lambda i: (a, 0, 0))
    return pl.pallas_call(
        body, name="out_proj_fwd", grid=(t // tm,),
        in_specs=[row, row, row, pl.BlockSpec((tm, D), lambda i: (i, 0)), pl.BlockSpec((tm, D), lambda i: (i, 1)),
                  pl.BlockSpec((1, 2 * D), lambda i: (0, 0)), wspec(0), wspec(1), wspec(2)],
        out_specs=[row, row, row, row],
        out_shape=[SDS((t, D), F32), SDS((t, D), BF16), SDS((t, D), BF16), SDS((t, D), BF16)],
        compiler_params=_cp("parallel", vmem=VMEM_BIG),
    )(x, y_c, o, u_gl, u_gl, gate_b, w_sm, w_sm, w_sm)


def _loss_head(y, tgt):
    t = y.shape[0]
    tm = min(512, t)

    def body(y_ref, t_ref, dy_ref, acc_ref):
        @pl.when(pl.program_id(0) == 0)
        def _():
            acc_ref[...] = jnp.zeros_like(acc_ref)
        err = y_ref[...] - t_ref[...]
        dy_ref[...] = err * (1.0 / D)
        sq = _fold8(err * err)
        tot = sq[:, 0:128]
        for k in range(1, D // 128):
            tot = tot + sq[:, 128 * k:128 * (k + 1)]
        acc_ref[...] += tot

    row = pl.BlockSpec((tm, D), lambda i: (i, 0))
    return pl.pallas_call(
        body, name="loss_head", grid=(t // tm,), in_specs=[row, row],
        out_specs=[row, pl.BlockSpec((8, 128), lambda i: (0, 0))],
        out_shape=[SDS((t, D), F32), SDS((8, 128), F32)], compiler_params=_cp("arbitrary"),
    )(y, tgt)


def _out_proj_bwd(dout, y_a, y_b, u_gl, gate_b, w_sm):
    t = dout.shape[0]
    tm = min(512, t)

    def body(do_ref, ya_ref, yb_ref, gla_ref, glb_ref, gb_ref, wco_ref, wao_ref, wout_ref,
             dya_ref, dyb_ref, dgl_ref, dyc_ref, dob_ref, dgb_ref):
        @pl.when(pl.program_id(0) == 0)
        def _():
            dgb_ref[...] = jnp.zeros_like(dgb_ref)
        dm = _dot_nt(do_ref[...].astype(BF16), wout_ref[...])
        gb = gb_ref[...]
        ga_ = _sigmoid(gla_ref[...].astype(F32) + gb[:, :D])
        gb_ = _sigmoid(glb_ref[...].astype(F32) + gb[:, D:])
        dya = (ga_ * dm).astype(BF16)
        dyb = (gb_ * dm).astype(BF16)
        dgla = ya_ref[...].astype(F32) * dm * (ga_ * (1.0 - ga_))
        dglb = yb_ref[...].astype(F32) * dm * (gb_ * (1.0 - gb_))
        dya_ref[...] = dya
        dyb_ref[...] = dyb
        dgl_ref[:, :D] = dgla.astype(BF16)
        dgl_ref[:, D:] = dglb.astype(BF16)
        dgb_ref[:, :D] += _fold8(dgla)
        dgb_ref[:, D:] += _fold8(dglb)
        dyc_ref[...] = _dot_nt(dya, wco_ref[...]).astype(BF16)
        dob_ref[...] = _dot_nt(dyb, wao_ref[...]).astype(BF16)

    row = pl.BlockSpec((tm, D), lambda i: (i, 0))
    wspec = lambda a: pl.BlockSpec((None, D, D), lambda i: (a, 0, 0))
    return pl.pallas_call(
        body, name="out_proj_bwd", grid=(t // tm,),
        in_specs=[row, row, row, pl.BlockSpec((tm, D), lambda i: (i, 0)), pl.BlockSpec((tm, D), lambda i: (i, 1)),
                  pl.BlockSpec((1, 2 * D), lambda i: (0, 0)), wspec(0), wspec(1), wspec(2)],
        out_specs=[row, row, pl.BlockSpec((tm, 2 * D), lambda i: (i, 0)), row, row,
                   pl.BlockSpec((8, 2 * D), lambda i: (0, 0))],
        out_shape=[SDS((t, D), BF16), SDS((t, D), BF16), SDS((t, 2 * D), BF16), SDS((t, D), BF16), SDS((t, D), BF16),
                   SDS((8, 2 * D), F32)],
        compiler_params=_cp("arbitrary", vmem=VMEM_BIG),
    )(dout, y_a, y_b, u_gl, u_gl, gate_b, w_sm, w_sm, w_sm)


def _small_wgrads(y_c, d_ya, o, d_yb, merged, dout):
    t = y_c.shape[0]
    tk = min(512, t)

    def body(yc_ref, dya_ref, o_ref, dyb_ref, mg_ref, do_ref, g_ref):
        @pl.when(pl.program_id(0) == 0)
        def _():
            g_ref[...] = jnp.zeros_like(g_ref)
        g_ref[0] += _dot_tn(yc_ref[...], dya_ref[...])
        g_ref[1] += _dot_tn(o_ref[...], dyb_ref[...])
        g_ref[2] += _dot_tn(mg_ref[...], do_ref[...].astype(BF16))

    row = pl.BlockSpec((tk, D), lambda k: (k, 0))
    return pl.pallas_call(
        body, name="small_wgrads", grid=(t // tk,), in_specs=[row] * 6,
        out_specs=pl.BlockSpec((3, D, D), lambda k: (0, 0, 0)), out_shape=SDS((3, D, D), F32),
        compiler_params=_cp("arbitrary", vmem=VMEM_BIG),
    )(y_c, d_ya, o, d_yb, merged, dout)


def _conv_bwd(d_yc, u_conv, conv_w):
    t = d_yc.shape[0]
    tm = min(256, t)
    hb = tm // 16
    last_halo = t // 16 - 1
    n_steps = t // tm

    def body(dy_ref, v_ref, b_ref, c_ref, z_ref, hv_ref, hc_ref, ndy_ref, nb_ref, nz_ref, w_ref, du_ref, dw_ref):
        i = pl.program_id(0)

        @pl.when(i == 0)
        def _():
            dw_ref[...] = jnp.zeros_like(dw_ref)
        v, c = v_ref[...].astype(F32), c_ref[...].astype(F32)
        b, z = b_ref[...].astype(F32), z_ref[...].astype(F32)
        cv = c * v
        halo = jnp.where(i > 0, hc_ref[...].astype(F32) * hv_ref[...].astype(F32), 0.0)
        row = lax.broadcasted_iota(jnp.int32, (tm, 1), 0)
        s1 = jnp.where(row == 0, halo[15:16], pltpu.roll(cv, 1, 0))
        s2 = jnp.where(row == 0, halo[14:15], jnp.where(row == 1, halo[15:16], pltpu.roll(cv, 2, 0)))
        w0, w1, w2 = w_ref[0:1, :], w_ref[1:2, :], w_ref[2:3, :]
        conv = w0 * s2 + w1 * s1 + w2 * cv
        sig = _sigmoid(z)
        sz = z * sig
        dsz = sig * (1.0 + z * (1.0 - sig))
        dy = dy_ref[...].astype(F32)
        dconv = dy * b * sz
        nz = nz_ref[...].astype(F32)
        nxt = ndy_ref[...].astype(F32) * nb_ref[...].astype(F32) * (nz * _sigmoid(nz))
        nxt = jnp.where(i < n_steps - 1, nxt, 0.0)
        a1 = jnp.where(row == tm - 1, nxt[0:1], pltpu.roll(dconv, tm - 1, 0))
        a2 = jnp.where(row == tm - 2, nxt[0:1], jnp.where(row == tm - 1, nxt[1:2], pltpu.roll(dconv, tm - 2, 0)))
        dcv = w2 * dconv + w1 * a1 + w0 * a2
        du_ref[:, 0:D] = (dcv * c).astype(BF16)
        du_ref[:, D:2 * D] = (dy * conv * sz).astype(BF16)
        du_ref[:, 2 * D:3 * D] = (dcv * v).astype(BF16)
        du_ref[:, 3 * D:4 * D] = (dy * b * conv * dsz).astype(BF16)
        r8 = lax.broadcasted_iota(jnp.int32, (8, 1), 0)
        dw_ref[...] += jnp.where(r8 == 0, jnp.sum(dconv * s2, axis=0, keepdims=True),
                                 jnp.where(r8 == 1, jnp.sum(dconv * s1, axis=0, keepdims=True),
                                           jnp.where(r8 == 2, jnp.sum(dconv * cv, axis=0, keepdims=True), 0.0)))

    def col(k):
        return pl.BlockSpec((tm, D), lambda i: (i, k))

    def halo(k):
        return pl.BlockSpec((16, D), lambda i: (jnp.maximum(i * hb - 1, 0), k))

    def nxt(k):
        return pl.BlockSpec((16, D), lambda i: (jnp.minimum((i + 1) * hb, last_halo), k))

    return pl.pallas_call(
        body, name="conv_bwd", grid=(t // tm,),
        in_specs=[col(0), col(0), col(1), col(2), col(3), halo(0), halo(2), nxt(0), nxt(1), nxt(3),
                  pl.BlockSpec((3, D), lambda i: (0, 0))],
        out_specs=[pl.BlockSpec((tm, 4 * D), lambda i: (i, 0)), pl.BlockSpec((8, D), lambda i: (0, 0))],
        out_shape=[SDS((t, 4 * D), BF16), SDS((8, D), F32)],
        compiler_params=_cp("arbitrary", vmem=VMEM_BIG),
    )(d_yc, u_conv, u_conv, u_conv, u_conv, u_conv, u_conv, d_yc, u_conv, u_conv, conv_w)


def _attn_bwd(d_o, qs, kvx, u_za, sinks):
    t = d_o.shape[0]
    nb = t // BLK

    def body(q_ref, kc_ref, kp_ref, za_ref, do_ref, sink_ref, dq_ref, dkv_ref, dza_ref, dsk_ref, carry_ref):
        n = pl.program_id(0)

        @pl.when(n == 0)
        def _():
            carry_ref[...] = jnp.zeros_like(carry_ref)
            dsk_ref[...] = jnp.zeros_like(dsk_ref)

        live = n < nb
        kvb = jnp.concatenate([kp_ref[...], kc_ref[...]], axis=0)
        za = za_ref[...].astype(F32)
        sig = _sigmoid(za)
        dsa = sig * (1.0 + za * (1.0 - sig))
        do = jnp.where(live, do_ref[...].astype(F32), 0.0)
        dattn = do * (za * sig)
        valid = _band_mask(n)
        lane = lax.broadcasted_iota(jnp.int32, (1, 128), 1)
        lo_lanes = lane < HEAD
        dq_cols, attn_cols, dk_cols, dv_cols = [], [], [], []
        dsk = jnp.zeros((1, 128), F32)
        for hk in range(N_KV):
            qpp = _pair_rows(q_ref, hk)
            dapp = _pair_rows(dattn, hk)
            dapp_b = dapp.astype(BF16)
            halves = [_softmax_half(qpp, kvb, hk, half, valid, sink_ref) for half in range(2)]
            app = _dot(halves[0][0].astype(BF16), halves[0][4]) * halves[0][2] \
                + _dot(halves[1][0].astype(BF16), halves[1][4]) * halves[1][2]
            prod = dapp * app
            drows = [jnp.sum(jnp.where(lo_lanes, prod, 0.0), axis=-1, keepdims=True),
                     jnp.sum(jnp.where(lo_lanes, 0.0, prod), axis=-1, keepdims=True)]
            dqpp, xk, xv = None, [], []
            for half in range(2):
                p, es, inv, kx, vx = halves[half]
                prob = p * inv
                ds = (prob * (_dot_nt(dapp_b, vx) - drows[half])).astype(BF16)
                dq_half = _dot(ds, kx)
                dqpp = dq_half if dqpp is None else dqpp + dq_half
                xk.append(_dot_tn(ds, qpp))
                xv.append(_dot_tn(prob.astype(BF16), dapp_b))
                dsink = -(es * inv) * drows[half]
                for pair in range(2):
                    tot = jnp.sum(dsink[BLK * pair:BLK * (pair + 1)], axis=0, keepdims=True)
                    dsk = dsk + jnp.where(lane == GROUP * hk + 2 * pair + half, tot, 0.0)
            dq_cols += [dqpp[:BLK], dqpp[BLK:]]
            attn_cols += [app[:BLK], app[BLK:]]
            dk_cols.append(jnp.where(lo_lanes, xk[0], xk[1]))
            dv_cols.append(jnp.where(lo_lanes, xv[0], xv[1]))

        @pl.when(live)
        def _():
            dq_ref[...] = jnp.concatenate(dq_cols, axis=1).astype(BF16)
            dza_ref[...] = (do * jnp.concatenate(attn_cols, axis=1) * dsa).astype(BF16)

        band = jnp.concatenate(dk_cols + dv_cols, axis=1)
        dkv_ref[...] = (band[:BLK] + carry_ref[...]).astype(BF16)
        carry_ref[...] = band[BLK:]
        dsk_ref[...] += jnp.broadcast_to(dsk, (8, 128))

    cur = lambda n: jnp.minimum(n, nb - 1)
    prev = lambda n: jnp.maximum(n - 1, 0)
    return pl.pallas_call(
        body, name="attn_bwd", grid=(nb + 1,),
        in_specs=[pl.BlockSpec((BLK, D), lambda n: (cur(n), 0)),
                  pl.BlockSpec((BLK, KVX), lambda n: (cur(n), 0)), pl.BlockSpec((BLK, KVX), lambda n: (prev(n), 0)),
                  pl.BlockSpec((BLK, D), lambda n: (cur(n), 0)), pl.BlockSpec((BLK, D), lambda n: (cur(n), 0)),
                  pl.BlockSpec(memory_space=pltpu.SMEM)],
        out_specs=[pl.BlockSpec((BLK, D), lambda n: (cur(n), 0)), pl.BlockSpec((BLK, D), lambda n: (prev(n), 0)),
                   pl.BlockSpec((BLK, D), lambda n: (cur(n), 0)), pl.BlockSpec((8, 128), lambda n: (0, 0))],
        out_shape=[SDS((t, D), BF16), SDS((t, D), BF16), SDS((t, D), BF16), SDS((8, 128), F32)],
        scratch_shapes=[pltpu.VMEM((BLK, D), F32)],
        compiler_params=_cp("arbitrary", vmem=VMEM_BIG),
    )(qs, kvx, kvx, u_za, d_o, sinks)


def _qkv_post(u_qkv, dqs, dkv, dza, qg_s, kg_t):
    t = u_qkv.shape[0]
    tm = min(512, t)

    def norm_bwd(x, dy, g):
        r = lax.rsqrt(_head_sum(x * x) * (1.0 / HEAD) + EPS)
        xhat = x * r
        dxh = dy * g
        return r * (dxh - xhat * (_head_sum(dxh * xhat) * (1.0 / HEAD))), _fold8(dy * xhat)

    def body(u_ref, dq_ref, dkv_ref, dza_ref, qg_ref, kg_ref, du_ref, dqg_ref, dkg_ref):
        @pl.when(pl.program_id(0) == 0)
        def _():
            dqg_ref[...] = jnp.zeros_like(dqg_ref)
            dkg_ref[...] = jnp.zeros_like(dkg_ref)
        dq, gq = norm_bwd(u_ref[:, 0:D].astype(F32), dq_ref[...].astype(F32), qg_ref[...])
        fold = _fold_mat()
        dk, gk = norm_bwd(u_ref[:, D:D + 256].astype(F32), _dot(dkv_ref[:, 0:512], fold), kg_ref[...])
        du_ref[:, 0:D] = dq.astype(BF16)
        du_ref[:, D:2 * D] = dza_ref[...]
        du_ref[:, 2 * D:2 * D + 256] = dk.astype(BF16)
        du_ref[:, 2 * D + 256:2 * D + 512] = _dot(dkv_ref[:, 512:1024], fold).astype(BF16)
        dqg_ref[...] += gq
        dkg_ref[...] += gk

    row = pl.BlockSpec((tm, D), lambda i: (i, 0))
    return pl.pallas_call(
        body, name="qkv_post", grid=(t // tm,),
        in_specs=[pl.BlockSpec((tm, 1536), lambda i: (i, 0)), row, row, row,
                  pl.BlockSpec((1, D), lambda i: (0, 0)), pl.BlockSpec((1, 256), lambda i: (0, 0))],
        out_specs=[pl.BlockSpec((tm, 2560), lambda i: (i, 0)), pl.BlockSpec((8, D), lambda i: (0, 0)),
                   pl.BlockSpec((8, 256), lambda i: (0, 0))],
        out_shape=[SDS((t, 2560), BF16), SDS((8, D), F32), SDS((8, 256), F32)],
        compiler_params=_cp("arbitrary", vmem=VMEM_BIG),
    )(u_qkv, dqs, dkv, dza, qg_s, kg_t)


N_GRAN = N_IN // CB


def _du_granule(j):
    attn = jnp.where(j == 10, 4, jnp.where(j >= 11, j - 9, j - 8))
    return jnp.clip(j, 0, 7), jnp.clip(attn, 0, 4), jnp.clip(j - 13, 0, 3)


def _du_select(j, refs, fn):
    for ref, lo, hi in zip(refs, (0, 8, 13), (8, 13, 17)):
        @pl.when((j >= lo) & (j < hi))
        def _():
            fn(ref)


def _in_proj_bwd(du, w_full):
    t = du[0].shape[0]
    tm = min(1024, t)

    def body(a0, a1, a2, b_ref, dh_ref, acc_ref):
        k = pl.program_id(1)

        @pl.when(k == 0)
        def _():
            acc_ref[...] = jnp.zeros_like(acc_ref)

        def add(a_ref):
            acc_ref[...] += _dot_nt(a_ref[...], b_ref[...])
        _du_select(k, (a0, a1, a2), add)

        @pl.when(k == N_GRAN - 1)
        def _():
            dh_ref[...] = acc_ref[...]

    seg = lambda q: pl.BlockSpec((tm, CB), lambda i, k: (i, _du_granule(k)[q]))
    return pl.pallas_call(
        body, name="in_proj_bwd", grid=(t // tm, N_GRAN),
        in_specs=[seg(0), seg(1), seg(2), pl.BlockSpec((D, CB), lambda i, k: (0, k))],
        out_specs=pl.BlockSpec((tm, D), lambda i, k: (i, 0)), out_shape=SDS((t, D), F32),
        scratch_shapes=[pltpu.VMEM((tm, D), F32)],
        compiler_params=_cp("parallel", "arbitrary", vmem=VMEM_BIG),
    )(*du, w_full)


def _rmsnorm_bwd(dh, x, g, dout):
    t = x.shape[0]
    tm = min(256, t)

    def body(dh_ref, x_ref, g_ref, do_ref, dx_ref, dg_ref):
        @pl.when(pl.program_id(0) == 0)
        def _():
            dg_ref[...] = jnp.zeros_like(dg_ref)
        dh = dh_ref[...]
        xv = x_ref[...]
        r = lax.rsqrt(jnp.mean(xv * xv, axis=-1, keepdims=True) + EPS)
        xhat = xv * r
        dg_ref[...] += _fold8(dh * xhat)
        dxh = dh * g_ref[...]
        dx_ref[...] = do_ref[...] + r * (dxh - xhat * jnp.mean(dxh * xhat, axis=-1, keepdims=True))

    row = pl.BlockSpec((tm, D), lambda i: (i, 0))
    return pl.pallas_call(
        body, name="rmsnorm_bwd", grid=(t // tm,),
        in_specs=[row, row, pl.BlockSpec((1, D), lambda i: (0, 0)), row],
        out_specs=[row, pl.BlockSpec((8, D), lambda i: (0, 0))],
        out_shape=[SDS((t, D), F32), SDS((8, D), F32)], compiler_params=_cp("arbitrary"),
    )(dh, x, g, dout)


def _in_proj_wgrad(ht, du):
    t = ht.shape[1]
    tk = min(2048, t)

    def body(h_ref, b0, b1, b2, g_ref):
        j, k = pl.program_id(0), pl.program_id(1)

        @pl.when(k == 0)
        def _():
            g_ref[...] = jnp.zeros_like(g_ref)

        def add(b_ref):
            g_ref[...] += _dot(h_ref[...], b_ref[...])
        _du_select(j, (b0, b1, b2), add)

    seg = lambda q: pl.BlockSpec((tk, CB), lambda j, k: (k, _du_granule(j)[q]))
    return pl.pallas_call(
        body, name="in_proj_wgrad", grid=(N_GRAN, t // tk),
        in_specs=[pl.BlockSpec((D, tk), lambda j, k: (0, k)), seg(0), seg(1), seg(2)],
        out_specs=pl.BlockSpec((D, CB), lambda j, k: (0, j)), out_shape=SDS((D, N_IN), F32),
        compiler_params=_cp("parallel", "arbitrary", vmem=VMEM_BIG),
    )(ht, *du)


def _swap_halves(g_in, g_sm):
    def body(*refs):
        ins, outs = refs[:2 * N_LAYERS], refs[2 * N_LAYERS:4 * N_LAYERS]
        send, recv = refs[4 * N_LAYERS:]
        x, y, c = _mesh_pos()
        cps = []
        for l in range(N_LAYERS):
            for a in range(2):
                s = 2 * l + a
                src = ins[s].at[1 - c] if a == 0 else ins[s].at[:, :, 1 - c]
                cp = pltpu.make_async_remote_copy(src_ref=src, dst_ref=outs[s], send_sem=send.at[s], recv_sem=recv.at[s],
                                                  device_id=(x, y, 1 - c), device_id_type=MESH)
                cp.start()
                cps.append(cp)
        for cp in cps:
            cp.wait()

    out_shape, args = [], []
    for l in range(N_LAYERS):
        out_shape += [SDS((512, N_IN), F32), SDS((3, 4, 128, D), F32)]
        args += [g_in[l], g_sm[l]]
    res = pl.pallas_call(
        body, name="swap_halves", in_specs=[ANY] * (2 * N_LAYERS), out_specs=[ANY] * (2 * N_LAYERS), out_shape=out_shape,
        scratch_shapes=[pltpu.SemaphoreType.DMA((2 * N_LAYERS,))] * 2,
    )(*args)
    return [(res[2 * l], res[2 * l + 1]) for l in range(N_LAYERS)]


def _add_halves_in(c_idx, g_in, r_in):
    def body(c_ref, a_ref, b_ref, f_ref, h_ref):
        s = a_ref[...] + b_ref[...]
        f_ref[...] = s
        h_ref[...] = s.astype(BF16)

    blk = pl.BlockSpec((128, N_IN), lambda i, c: (i, 0))
    return pl.pallas_call(
        body, name="add_halves_in",
        grid_spec=pltpu.PrefetchScalarGridSpec(
            num_scalar_prefetch=1, grid=(4,),
            in_specs=[pl.BlockSpec((None, 128, N_IN), lambda i, c: (c[0], i, 0)), blk], out_specs=[blk, blk]),
        out_shape=[SDS((512, N_IN), F32), SDS((512, N_IN), BF16)], compiler_params=_cp("parallel", vmem=VMEM_BIG),
    )(c_idx, g_in, r_in)


def _add_halves_sm(c_idx, g_sm, r_sm):
    def body(c_ref, a_ref, b_ref, f_ref, h_ref):
        s = a_ref[...] + b_ref[...]
        f_ref[...] = s
        h_ref[...] = s.astype(BF16)

    blk = pl.BlockSpec((1, 4, 128, D), lambda a, c: (a, 0, 0, 0))
    return pl.pallas_call(
        body, name="add_halves_sm",
        grid_spec=pltpu.PrefetchScalarGridSpec(
            num_scalar_prefetch=1, grid=(3,),
            in_specs=[pl.BlockSpec((1, 4, None, 128, D), lambda a, c: (a, 0, c[0], 0, 0)), blk], out_specs=[blk, blk]),
        out_shape=[SDS((3, 4, 128, D), F32), SDS((3, 4, 128, D), BF16)], compiler_params=_cp("parallel"),
    )(c_idx, g_sm, r_sm)


def _scatter_chips(h_in, h_sm):
    n_cp = N_LAYERS * 2 * 3

    def body(*refs):
        ins, outs = refs[:2 * N_LAYERS], refs[2 * N_LAYERS:4 * N_LAYERS]
        send, recv = refs[4 * N_LAYERS:]
        x, y, c = _mesh_pos()
        cps = []
        for l in range(N_LAYERS):
            for a in range(2):
                for k, chip in enumerate(_other_chips(x, y)):
                    s = (2 * l + a) * 3 + k
                    their = 2 * chip[0] + chip[1]
                    if a == 0:
                        src = ins[2 * l].at[:, pl.ds(pl.multiple_of(their * SH_IN, 128), SH_IN)]
                    else:
                        src = ins[2 * l + 1].at[:, their]
                    cp = pltpu.make_async_remote_copy(src_ref=src, dst_ref=outs[2 * l + a].at[k], send_sem=send.at[s],
                                                      recv_sem=recv.at[s], device_id=(*chip, c), device_id_type=MESH)
                    cp.start()
                    cps.append(cp)
        for cp in cps:
            cp.wait()

    out_shape, args = [], []
    for l in range(N_LAYERS):
        out_shape += [SDS((3, 512, SH_IN), BF16), SDS((3, 3, 128, D), BF16)]
        args += [h_in[l], h_sm[l]]
    res = pl.pallas_call(
        body, name="scatter_chips", in_specs=[ANY] * (2 * N_LAYERS), out_specs=[ANY] * (2 * N_LAYERS), out_shape=out_shape,
        scratch_shapes=[pltpu.SemaphoreType.DMA((n_cp,))] * 2,
    )(*args)
    return [(res[2 * l], res[2 * l + 1]) for l in range(N_LAYERS)]


def _final_sum_in(chip_idx, f_in, r_in):
    def body(j_ref, a_ref, r_ref, o_ref):
        o_ref[...] = a_ref[...] + r_ref[0].astype(F32) + r_ref[1].astype(F32) + r_ref[2].astype(F32)

    return pl.pallas_call(
        body, name="final_sum_in",
        grid_spec=pltpu.PrefetchScalarGridSpec(
            num_scalar_prefetch=1, grid=(4,),
            in_specs=[pl.BlockSpec((128, SH_IN), lambda i, j: (i, j[0])), pl.BlockSpec((3, 128, SH_IN), lambda i, j: (0, i, 0))],
            out_specs=pl.BlockSpec((128, SH_IN), lambda i, j: (i, 0))),
        out_shape=SDS((512, SH_IN), F32), compiler_params=_cp("parallel"),
    )(chip_idx, f_in, r_in)


def _final_sum_sm(chip_idx, f_sm, r_sm):
    def body(j_ref, a_ref, r_ref, o_ref):
        o_ref[...] = a_ref[...] + r_ref[0].astype(F32) + r_ref[1].astype(F32) + r_ref[2].astype(F32)

    return pl.pallas_call(
        body, name="final_sum_sm",
        grid_spec=pltpu.PrefetchScalarGridSpec(
            num_scalar_prefetch=1, grid=(3,),
            in_specs=[pl.BlockSpec((1, None, 128, D), lambda a, j: (a, j[0], 0, 0)),
                      pl.BlockSpec((3, 1, 128, D), lambda a, j: (0, a, 0, 0))],
            out_specs=pl.BlockSpec((1, 128, D), lambda a, j: (a, 0, 0))),
        out_shape=SDS((3, 128, D), F32), compiler_params=_cp("parallel"),
    )(chip_idx, f_sm, r_sm)


def _join_halves(t_in, t_sm):
    n_cp = N_LAYERS * 4

    def body(*refs):
        ins, outs = refs[:2 * N_LAYERS], refs[2 * N_LAYERS:2 * N_LAYERS + 4]
        send, recv, loc_in, loc_out, stage_in, stage_sm = refs[2 * N_LAYERS + 4:]
        x, y, c = _mesh_pos()
        cps, own = [], []
        for l in range(N_LAYERS):
            for a in range(4):
                s = 4 * l + a
                if a == 0:
                    src = ins[2 * l]
                    dst = outs[0].at[l, pl.ds(pl.multiple_of(c * 512, 512), 512), :]
                else:
                    src = ins[2 * l + 1].at[a - 1]
                    dst = outs[a].at[l, pl.ds(pl.multiple_of(c * 128, 128), 128), :]
                own.append((src, dst, min(a, 1)))
                cp = pltpu.make_async_remote_copy(src_ref=src, dst_ref=dst, send_sem=send.at[s], recv_sem=recv.at[s],
                                                  device_id=(x, y, 1 - c), device_id_type=MESH)
                cp.start()
                cps.append(cp)
        _staged_copies(own, (stage_in, stage_sm), loc_in, loc_out)
        for l in range(N_LAYERS):
            for a in range(4):
                s = 4 * l + a
                if a == 0:
                    got = outs[0].at[l, pl.ds(pl.multiple_of((1 - c) * 512, 512), 512), :]
                else:
                    got = outs[a].at[l, pl.ds(pl.multiple_of((1 - c) * 128, 128), 128), :]
                pltpu.make_async_remote_copy(src_ref=got, dst_ref=got, send_sem=send.at[s], recv_sem=recv.at[s],
                                             device_id=(x, y, 1 - c), device_id_type=MESH).wait_recv()
        for cp in cps:
            cp.wait_send()

    args = []
    for l in range(N_LAYERS):
        args += [t_in[l], t_sm[l]]
    sm = SDS((N_LAYERS, SH_ROW, D), F32)
    return pl.pallas_call(
        body, name="join_halves", in_specs=[ANY] * (2 * N_LAYERS), out_specs=[ANY] * 4,
        out_shape=[SDS((N_LAYERS, D, SH_IN), F32), sm, sm, sm],
        scratch_shapes=[pltpu.SemaphoreType.DMA((n_cp,))] * 4
        + [pltpu.VMEM((2, 512, SH_IN), F32), pltpu.VMEM((2, 128, D), F32)],
        compiler_params=_cp(vmem=VMEM_BIG),
    )(*args)


def _adam_math(w, g, m, v):
    m = ADAM_B1 * m + (1.0 - ADAM_B1) * g
    v = ADAM_B2 * v + (1.0 - ADAM_B2) * (g * g)
    m_hat = m / (1.0 - ADAM_B1 ** ADAM_STEP)
    v_hat = v / (1.0 - ADAM_B2 ** ADAM_STEP)
    delta = -ADAM_LR * (m_hat / (jnp.sqrt(v_hat) + ADAM_EPS) + ADAM_WD * w)
    return delta, m, v


def _adamw_big(w, g, m, v, name):
    rows, cols = w.shape
    tr = 128

    def body(w_ref, g_ref, m_ref, v_ref, d_ref, nm_ref, nv_ref):
        d_ref[...], nm_ref[...], nv_ref[...] = _adam_math(w_ref[...], g_ref[...], m_ref[...], v_ref[...])

    blk = pl.BlockSpec((tr, cols), lambda i: (i, 0))
    return pl.pallas_call(
        body, name=name, grid=(rows // tr,), in_specs=[blk] * 4, out_specs=[blk] * 3,
        out_shape=[SDS((rows, cols), F32)] * 3, compiler_params=_cp("parallel", vmem=VMEM_BIG),
    )(w, g, m, v)


def _adamw_small(ws, gs, ms, vs):
    n = len(ws)

    def body(*refs):
        for k in range(n):
            w_ref, g_ref, m_ref, v_ref = (refs[q * n + k] for q in range(4))
            d, nm, nv = _adam_math(w_ref[...], g_ref[...], m_ref[...], v_ref[...])
            refs[4 * n + k][...] = d
            refs[5 * n + k][...] = nm
            refs[6 * n + k][...] = nv

    vm = pl.BlockSpec(memory_space=pltpu.VMEM)
    shapes = [SDS(w.shape, F32) for w in ws]
    res = pl.pallas_call(
        body, name="adamw_small", in_specs=[vm] * (4 * n), out_specs=[vm] * (3 * n), out_shape=shapes * 3,
    )(*ws, *gs, *ms, *vs)
    return res[:n], res[n:2 * n], res[2 * n:]


def _pad_rows(a, rows):
    flat = a.reshape(-1)
    return jnp.pad(flat, (0, rows * 128 - flat.shape[0])).reshape(rows, 128)


def kernel(x, norm_g, w_in, conv_w, q_norm_g, k_norm_g, sinks, w_conv_out, w_attn_out, gate_b, w_out, loss_target, m_norm_g, m_w_in, m_conv_w, m_q_norm_g, m_k_norm_g, m_sinks, m_w_conv_out, m_w_attn_out, m_gate_b, m_w_out, v_norm_g, v_w_in, v_conv_w, v_q_norm_g, v_k_norm_g, v_sinks, v_w_conv_out, v_w_attn_out, v_gate_b, v_w_out):
    xi, yi, ci = _mesh_pos()
    chip = 2 * xi + yi
    c_idx = jnp.reshape(ci, (1,)).astype(jnp.int32)
    chip_idx = jnp.reshape(chip, (1,)).astype(jnp.int32)
    t = x.shape[1]
    xs = [x.reshape(t, D)]
    tgt = loss_target.reshape(t, D)

    full_w = _gather_weights(_cast_w_in(w_in), _cast_w_small(w_conv_out, w_attn_out, w_out))
    placed = lax.dynamic_update_slice(jnp.zeros((N_LAYERS, 3, D), F32),
                                      jnp.where(ci == 0, conv_w, 0.0), (0, 0, chip * SH_ROW))
    conv_full = _allreduce_small(placed.reshape(96, 128)).reshape(N_LAYERS, 3, D)

    qg_s = jnp.tile(q_norm_g, (1, N_Q)) * SCALE
    kg_t = jnp.tile(k_norm_g, (1, N_KV))
    saved = []
    for l in range(N_LAYERS):
        w_full, w_sm = full_w[l]
        h, ht = _rmsnorm_fwd(xs[l], norm_g[l:l + 1])
        u_conv = _in_proj(h, w_full, (0, 8), "in_proj_conv")
        u_qkv = _in_proj(h, w_full, (8, 3), "in_proj_qkv")
        u_za = _in_proj(h, w_full, SEG_ZA, "in_proj_za")
        u_gl = _in_proj(h, w_full, SEG_GL, "in_proj_gl")
        y_c = _conv_fwd(u_conv, conv_full[l])
        qs, kvx = _qkv_prep(u_qkv, qg_s[l:l + 1], kg_t[l:l + 1])
        o = _attn_fwd(qs, kvx, u_za, sinks[l:l + 1])
        x_next, y_a, y_b, merged = _out_proj_fwd(xs[l], y_c, o, u_gl, gate_b[l:l + 1], w_sm)
        xs.append(x_next)
        saved.append((ht, u_conv, u_qkv, u_za, u_gl, y_c, o, y_a, y_b, merged, qs, kvx))

    dout, sq = _loss_head(xs[N_LAYERS], tgt)
    loss = lax.psum(jnp.sum(sq) * (0.5 / D), ("x", "y", "c"))

    g_in, g_sm, small = [None] * N_LAYERS, [None] * N_LAYERS, [None] * N_LAYERS
    for l in reversed(range(N_LAYERS)):
        w_full, w_sm = full_w[l]
        ht, u_conv, u_qkv, u_za, u_gl, y_c, o, y_a, y_b, merged, qs, kvx = saved[l]
        d_ya, d_yb, du_gl, d_yc, d_o, dgb = _out_proj_bwd(dout, y_a, y_b, u_gl, gate_b[l:l + 1], w_sm)
        g_sm[l] = _small_wgrads(y_c, d_ya, o, d_yb, merged, dout)
        du_conv, dcw = _conv_bwd(d_yc, u_conv, conv_full[l])
        dqs, dkv, dza, dsk = _attn_bwd(d_o, qs, kvx, u_za, sinks[l:l + 1])
        du_attn, dqg, dkg = _qkv_post(u_qkv, dqs, dkv, dza, qg_s[l:l + 1], kg_t[l:l + 1])
        du = (du_conv, du_attn, du_gl)
        g_in[l] = _in_proj_wgrad(ht, du)
        dout, dng = _rmsnorm_bwd(_in_proj_bwd(du, w_full), xs[l], norm_g[l:l + 1], dout)
        small[l] = (jnp.sum(dng, axis=0), SCALE * jnp.sum(dqg.reshape(8 * N_Q, HEAD), axis=0),
                    jnp.sum(dkg.reshape(8 * N_KV, HEAD), axis=0), dsk[0, :N_Q], jnp.sum(dgb, axis=0), dcw[:3])
    grad_x = dout.reshape(1, t, D)

    stack = lambda k: jnp.stack([small[l][k] for l in range(N_LAYERS)])
    pack = jnp.concatenate([_pad_rows(stack(0), 32), _pad_rows(stack(1), 8), _pad_rows(stack(2), 8),
                            _pad_rows(stack(3), 8), _pad_rows(stack(4), 64), _pad_rows(stack(5), 96)], axis=0)
    red = _allreduce_small(pack)
    g_norm_g = red[0:32].reshape(N_LAYERS, D)
    g_q_norm_g = red[32:40].reshape(-1)[:N_LAYERS * HEAD].reshape(N_LAYERS, HEAD)
    g_k_norm_g = red[40:48].reshape(-1)[:N_LAYERS * HEAD].reshape(N_LAYERS, HEAD)
    g_sinks = red[48:56].reshape(-1)[:N_LAYERS * N_Q].reshape(N_LAYERS, N_Q)
    g_gate_b = red[56:120].reshape(N_LAYERS, 2 * D)
    g_conv_full = red[120:216].reshape(N_LAYERS, 3, D)
    g_conv_w = lax.dynamic_slice(g_conv_full, (0, 0, chip * SH_ROW), (N_LAYERS, 3, SH_ROW))

    g_in_v = [g.reshape(2, 512, N_IN) for g in g_in]
    g_sm_v = [g.reshape(3, 4, 2, 128, D) for g in g_sm]
    got = _swap_halves(g_in_v, g_sm_v)
    h_in, h_sm, f_in, f_sm = [], [], [], []
    for l in range(N_LAYERS):
        f, hb = _add_halves_in(c_idx, g_in_v[l], got[l][0])
        f_in.append(f)
        h_in.append(hb)
        f, hb = _add_halves_sm(c_idx, g_sm_v[l], got[l][1])
        f_sm.append(f)
        h_sm.append(hb)
    parts = _scatter_chips(h_in, h_sm)
    t_in = [_final_sum_in(chip_idx, f_in[l], parts[l][0]) for l in range(N_LAYERS)]
    t_sm = [_final_sum_sm(chip_idx, f_sm[l], parts[l][1]) for l in range(N_LAYERS)]
    g_w_in, g_w_co, g_w_ao, g_w_out = _join_halves(t_in, t_sm)

    r_in = N_LAYERS * D
    d_in, nm_in, nv_in = (a.reshape(N_LAYERS, D, SH_IN) for a in _adamw_big(
        w_in.reshape(r_in, SH_IN), g_w_in.reshape(r_in, SH_IN), m_w_in.reshape(r_in, SH_IN),
        v_w_in.reshape(r_in, SH_IN), "adamw_w_in"))
    r_sm = N_LAYERS * SH_ROW
    big = {}
    for nm, w, g, m, v in (("co", w_conv_out, g_w_co, m_w_conv_out, v_w_conv_out),
                           ("ao", w_attn_out, g_w_ao, m_w_attn_out, v_w_attn_out),
                           ("out", w_out, g_w_out, m_w_out, v_w_out)):
        big[nm] = tuple(a.reshape(N_LAYERS, SH_ROW, D) for a in _adamw_big(
            w.reshape(r_sm, D), g.reshape(r_sm, D), m.reshape(r_sm, D), v.reshape(r_sm, D), "adamw_w_small"))
    sm_w = [norm_g, conv_w, q_norm_g, k_norm_g, sinks, gate_b]
    sm_g = [g_norm_g, g_conv_w, g_q_norm_g, g_k_norm_g, g_sinks, g_gate_b]
    sm_m = [m_norm_g, m_conv_w, m_q_norm_g, m_k_norm_g, m_sinks, m_gate_b]
    sm_v = [v_norm_g, v_conv_w, v_q_norm_g, v_k_norm_g, v_sinks, v_gate_b]
    sd, snm, snv = _adamw_small(sm_w, sm_g, sm_m, sm_v)

    def order(norm, w_in_, conv, qn, kn, sk, co, ao, gb, wo):
        return [norm, w_in_, conv, qn, kn, sk, co, ao, gb, wo]

    grads = order(g_norm_g, g_w_in, g_conv_w, g_q_norm_g, g_k_norm_g, g_sinks, g_w_co, g_w_ao, g_gate_b, g_w_out)
    deltas = order(sd[0], d_in, sd[1], sd[2], sd[3], sd[4], big["co"][0], big["ao"][0], sd[5], big["out"][0])
    new_m = order(snm[0], nm_in, snm[1], snm[2], snm[3], snm[4], big["co"][1], big["ao"][1], snm[5], big["out"][1])
    new_v = order(snv[0], nv_in, snv[1], snv[2], snv[3], snv[4], big["co"][2], big["ao"][2], snv[5], big["out"][2])
    return (loss, grad_x, *grads, *deltas, *new_m, *new_v)
```

```python
import functools

import jax
import jax.numpy as jnp
from jax import lax
from jax.experimental import pallas as pl
from jax.experimental.pallas import tpu as pltpu

F32, BF16 = jnp.float32, jnp.bfloat16
SDS = jax.ShapeDtypeStruct
MESH = pl.DeviceIdType.MESH
ANY = pl.BlockSpec(memory_space=pl.ANY)

D = 1024
N_IN = 8704
N_LAYERS = 4
N_Q, N_KV, HEAD = 16, 4, 64
GROUP = N_Q // N_KV
BLK = 128
EPS = 1e-6
NEG = -1e30
SCALE = HEAD ** -0.5
SH_IN = N_IN // 4
SH_ROW = D // 4
CB = 512
SEG_CONV, SEG_Q, SEG_KV, SEG_ZA, SEG_GL = (0, 8), (8, 2), (10, 1), (11, 2), (13, 4)
VMEM_BIG = 56 * 1024 * 1024

ADAM_LR, ADAM_B1, ADAM_B2, ADAM_EPS, ADAM_WD, ADAM_STEP = 0.001, 0.9, 0.999, 1e-08, 0.01, 10


def _cp(*sem, vmem=None):
    return pltpu.CompilerParams(dimension_semantics=sem if sem else None, vmem_limit_bytes=vmem)


def _sigmoid(z):
    return 1.0 / (1.0 + jnp.exp(-z))


def _dot(a, b):
    return jnp.dot(a, b, preferred_element_type=F32)


def _dot_nt(a, b):
    return lax.dot_general(a, b, (((1,), (1,)), ((), ())), preferred_element_type=F32)


def _dot_tn(a, b):
    return lax.dot_general(a, b, (((0,), (0,)), ((), ())), preferred_element_type=F32)


def _rms(xh):
    r = lax.rsqrt(jnp.mean(xh * xh, axis=-1, keepdims=True) + EPS)
    return xh * r, r


def _fold8(v):
    return jnp.sum(v.reshape(v.shape[0] // 8, 8, v.shape[1]), axis=0)


def _cast_w_in(w):
    rows = N_LAYERS * D

    def body(i_ref, o_ref):
        o_ref[...] = i_ref[...].astype(BF16)

    out = pl.pallas_call(
        body, name="cast_w_in", grid=(rows // 512,),
        in_specs=[pl.BlockSpec((512, SH_IN), lambda i: (i, 0))],
        out_specs=pl.BlockSpec((512, SH_IN), lambda i: (i, 0)),
        out_shape=SDS((rows, SH_IN), BF16), compiler_params=_cp("parallel"),
    )(w.reshape(rows, SH_IN))
    return out.reshape(N_LAYERS, D, SH_IN)


def _cast_w_small(a, b, c):
    def body(a_ref, b_ref, c_ref, o_ref):
        o_ref[0, 0] = a_ref[0].astype(BF16)
        o_ref[0, 1] = b_ref[0].astype(BF16)
        o_ref[0, 2] = c_ref[0].astype(BF16)

    spec = pl.BlockSpec((1, SH_ROW, D), lambda l: (l, 0, 0))
    return pl.pallas_call(
        body, name="cast_w_small", grid=(N_LAYERS,), in_specs=[spec, spec, spec],
        out_specs=pl.BlockSpec((1, 3, SH_ROW, D), lambda l: (l, 0, 0, 0)),
        out_shape=SDS((N_LAYERS, 3, SH_ROW, D), BF16), compiler_params=_cp("parallel"),
    )(a, b, c)


def _mesh_pos():
    return lax.axis_index("x"), lax.axis_index("y"), lax.axis_index("c")


def _other_chips(x, y):
    return [(1 - x, y), (x, 1 - y), (1 - x, 1 - y)]


def _staged_copies(copies, stages, sem_in, sem_out):
    busy, count = {}, {}
    for idx, (src, dst, kind) in enumerate(copies):
        slot = count.get(kind, 0) % 2
        count[kind] = count.get(kind, 0) + 1
        if (kind, slot) in busy:
            busy.pop((kind, slot)).wait()
        buf = stages[kind].at[slot]
        cin = pltpu.make_async_copy(src, buf, sem_in.at[idx])
        cin.start()
        cin.wait()
        cout = pltpu.make_async_copy(buf, dst, sem_out.at[idx])
        cout.start()
        busy[(kind, slot)] = cout
    for cp in busy.values():
        cp.wait()


def _gather_weights(win_b, wsm_b):
    n_cp = N_LAYERS * 2 * 3

    def body(win, wsm, *rest):
        outs = rest[:2 * N_LAYERS]
        send_a, recv_a, send_b, recv_b, loc_in, loc_out, stage_in, stage_sm = rest[2 * N_LAYERS:]
        x, y, c = _mesh_pos()
        me_chip = 2 * x + y
        sib = (x, y, 1 - c)
        chips = _other_chips(x, y)

        def src_half(l, a, half):
            if a == 0:
                return win.at[l, pl.ds(pl.multiple_of(half * 512, 512), 512), :]
            return wsm.at[l, :, pl.ds(pl.multiple_of(half * 128, 128), 128), :]

        def region(l, a, chip, half):
            full = outs[2 * l + a]
            if a == 0:
                return full.at[pl.ds(pl.multiple_of(half * 512, 512), 512),
                               pl.ds(pl.multiple_of(chip * SH_IN, 128), SH_IN)]
            return full.at[:, pl.ds(pl.multiple_of(chip * SH_ROW + half * 128, 128), 128), :]

        def own_region(l, a):
            full = outs[2 * l + a]
            if a == 0:
                return full.at[:, pl.ds(pl.multiple_of(me_chip * SH_IN, 128), SH_IN)]
            return full.at[:, pl.ds(pl.multiple_of(me_chip * SH_ROW, 128), SH_ROW), :]

        def rcopy(src, dst, ssem, rsem, to):
            return pltpu.make_async_remote_copy(src_ref=src, dst_ref=dst, send_sem=ssem, recv_sem=rsem,
                                                device_id=to, device_id_type=MESH)

        first, passed = [], []
        for l in range(N_LAYERS):
            for a in range(2):
                for k, chip in enumerate(chips):
                    s = (2 * l + a) * 3 + k
                    cp = rcopy(src_half(l, a, c), region(l, a, me_chip, c), send_a.at[s], recv_a.at[s], (*chip, c))
                    cp.start()
                    first.append(cp)
        own = [(win.at[l] if a == 0 else wsm.at[l], own_region(l, a), a) for l in range(N_LAYERS) for a in range(2)]
        _staged_copies(own, (stage_in, stage_sm), loc_in, loc_out)
        for l in range(N_LAYERS):
            for a in range(2):
                for k, chip in enumerate(chips):
                    s = (2 * l + a) * 3 + k
                    their = 2 * chip[0] + chip[1]
                    landed = region(l, a, their, c)
                    rcopy(landed, landed, send_a.at[s], recv_a.at[s], sib).wait_recv()
                    cp = rcopy(landed, landed, send_b.at[s], recv_b.at[s], sib)
                    cp.start()
                    passed.append(cp)
        for l in range(N_LAYERS):
            for a in range(2):
                for k, chip in enumerate(chips):
                    s = (2 * l + a) * 3 + k
                    their = 2 * chip[0] + chip[1]
                    other = region(l, a, their, 1 - c)
                    rcopy(other, other, send_b.at[s], recv_b.at[s], sib).wait_recv()
        for cp in first + passed:
            cp.wait_send()

    out_shape = []
    for _ in range(N_LAYERS):
        out_shape += [SDS((D, N_IN), BF16), SDS((3, D, D), BF16)]
    res = pl.pallas_call(
        body, name="gather_weights", in_specs=[ANY, ANY], out_specs=[ANY] * (2 * N_LAYERS), out_shape=out_shape,
        scratch_shapes=[pltpu.SemaphoreType.DMA((n_cp,))] * 4 + [pltpu.SemaphoreType.DMA((2 * N_LAYERS,))] * 2
        + [pltpu.VMEM((2, D, SH_IN), BF16), pltpu.VMEM((2, 3, SH_ROW, D), BF16)],
        compiler_params=_cp(vmem=VMEM_BIG),
    )(win_b, wsm_b)
    return [(res[2 * l], res[2 * l + 1]) for l in range(N_LAYERS)]


def _allreduce_small(pack):
    rows = pack.shape[0]

    def body(p_ref, o_ref, buf, send, recv):
        x, y, c = _mesh_pos()
        me = 4 * x + 2 * y + c
        sends = []
        for r in range(1, 8):
            to = (x if not (r & 4) else 1 - x, y if not (r & 2) else 1 - y, c if not (r & 1) else 1 - c)
            cp = pltpu.make_async_remote_copy(src_ref=p_ref, dst_ref=buf.at[me], send_sem=send.at[r - 1],
                                              recv_sem=recv.at[r - 1], device_id=to, device_id_type=MESH)
            cp.start()
            sends.append(cp)
        buf[me] = p_ref[...]
        for r in range(1, 8):
            frm = (4 * x + 2 * y + c) ^ r
            pltpu.make_async_remote_copy(src_ref=p_ref, dst_ref=buf.at[frm], send_sem=send.at[r - 1],
                                         recv_sem=recv.at[r - 1], device_id=(x, y, c), device_id_type=MESH).wait_recv()
        acc = buf[0]
        for d in range(1, 8):
            acc = acc + buf[d]
        o_ref[...] = acc
        for cp in sends:
            cp.wait_send()

    vm = pl.BlockSpec(memory_space=pltpu.VMEM)
    return pl.pallas_call(
        body, name="allreduce_small", in_specs=[vm], out_specs=vm, out_shape=SDS((rows, 128), F32),
        scratch_shapes=[pltpu.VMEM((8, rows, 128), F32), pltpu.SemaphoreType.DMA((7,)), pltpu.SemaphoreType.DMA((7,))],
    )(pack)


def _rmsnorm_fwd(x, g):
    t = x.shape[0]
    tm = min(512, t)

    def body(x_ref, g_ref, h_ref, ht_ref):
        xv = x_ref[...]
        r = lax.rsqrt(jnp.mean(xv * xv, axis=-1, keepdims=True) + EPS)
        h = xv * r * g_ref[...]
        h_ref[...] = h.astype(BF16)
        ht_ref[...] = h.T.astype(BF16)

    return pl.pallas_call(
        body, name="rmsnorm_fwd", grid=(t // tm,),
        in_specs=[pl.BlockSpec((tm, D), lambda i: (i, 0)), pl.BlockSpec((1, D), lambda i: (0, 0))],
        out_specs=[pl.BlockSpec((tm, D), lambda i: (i, 0)), pl.BlockSpec((D, tm), lambda i: (0, i))],
        out_shape=[SDS((t, D), BF16), SDS((D, t), BF16)],
        compiler_params=_cp("parallel", vmem=VMEM_BIG),
    )(x, g)


def _in_proj(h, w_full, seg, name):
    t = h.shape[0]
    off, nblk = seg
    tm = min(2048, t)

    def body(a_ref, b_ref, o_ref):
        o_ref[...] = _dot(a_ref[...], b_ref[...]).astype(BF16)

    return pl.pallas_call(
        body, name=name, grid=(t // tm, nblk),
        in_specs=[pl.BlockSpec((tm, D), lambda i, j: (i, 0)), pl.BlockSpec((D, CB), lambda i, j: (0, off + j))],
        out_specs=pl.BlockSpec((tm, CB), lambda i, j: (i, j)), out_shape=SDS((t, nblk * CB), BF16),
        compiler_params=_cp("parallel", "parallel", vmem=VMEM_BIG),
    )(h, w_full)


def _conv_fwd(u_conv, conv_w):
    t = u_conv.shape[0]
    tm = min(256, t)
    hb = tm // 16

    def body(v_ref, b_ref, c_ref, z_ref, hv_ref, hc_ref, w_ref, y_ref):
        i = pl.program_id(0)
        cv = c_ref[...].astype(F32) * v_ref[...].astype(F32)
        halo = hc_ref[...].astype(F32) * hv_ref[...].astype(F32)
        halo = jnp.where(i > 0, halo, 0.0)
        row = lax.broadcasted_iota(jnp.int32, (tm, 1), 0)
        s1 = jnp.where(row == 0, halo[15:16], pltpu.roll(cv, 1, 0))
        s2 = jnp.where(row == 0, halo[14:15], jnp.where(row == 1, halo[15:16], pltpu.roll(cv, 2, 0)))
        conv = w_ref[0:1, :] * s2 + w_ref[1:2, :] * s1 + w_ref[2:3, :] * cv
        z = z_ref[...].astype(F32)
        y_ref[...] = (b_ref[...].astype(F32) * conv * (z * _sigmoid(z))).astype(BF16)

    def col(k):
        return pl.BlockSpec((tm, D), lambda i: (i, k))

    def halo(k):
        return pl.BlockSpec((16, D), lambda i: (jnp.maximum(i * hb - 1, 0), k))

    return pl.pallas_call(
        body, name="conv_fwd", grid=(t // tm,),
        in_specs=[col(0), col(1), col(2), col(3), halo(0), halo(2), pl.BlockSpec((3, D), lambda i: (0, 0))],
        out_specs=pl.BlockSpec((tm, D), lambda i: (i, 0)), out_shape=SDS((t, D), BF16),
        compiler_params=_cp("parallel", vmem=VMEM_BIG),
    )(u_conv, u_conv, u_conv, u_conv, u_conv, u_conv, conv_w)


KVX = 4 * N_KV * 128


def _iota2(shape):
    return lax.broadcasted_iota(jnp.int32, shape, 0), lax.broadcasted_iota(jnp.int32, shape, 1)


def _head_sum(v):
    r, c = _iota2((128, 128))
    ones = ((r >> 6) == (c >> 6)).astype(BF16)
    hi = v.astype(BF16)
    lo = (v - hi.astype(F32)).astype(BF16)
    return jnp.concatenate([_dot(hi[:, g:g + 128], ones) + _dot(lo[:, g:g + 128], ones)
                            for g in range(0, v.shape[1], 128)], axis=1)


def _expand_mats():
    r, c = _iota2((N_KV * HEAD, N_KV * 128))
    base = ((r >> 6) << 7) + (r & 63)
    return (c == base).astype(BF16), (c == base + 64).astype(BF16)


def _fold_mat():
    r, c = _iota2((N_KV * 128, N_KV * HEAD))
    return (((r >> 7) == (c >> 6)) & ((r & 63) == (c & 63))).astype(BF16)


def _qkv_prep(u_qkv, qg_s, kg_t):
    t = u_qkv.shape[0]
    tm = min(512, t)

    def body(u_ref, qg_ref, kg_ref, qs_ref, kvx_ref):
        q = u_ref[:, 0:D].astype(F32)
        rq = lax.rsqrt(_head_sum(q * q) * (1.0 / HEAD) + EPS)
        qs_ref[...] = (q * rq * qg_ref[...]).astype(BF16)
        k = u_ref[:, D:D + 256].astype(F32)
        rk = lax.rsqrt(_head_sum(k * k) * (1.0 / HEAD) + EPS)
        kn = (k * rk * kg_ref[...]).astype(BF16)
        v = u_ref[:, D + 256:D + 512]
        e_lo, e_hi = _expand_mats()
        kvx_ref[:, 0:512] = _dot(kn, e_lo).astype(BF16)
        kvx_ref[:, 512:1024] = _dot(kn, e_hi).astype(BF16)
        kvx_ref[:, 1024:1536] = _dot(v, e_lo).astype(BF16)
        kvx_ref[:, 1536:2048] = _dot(v, e_hi).astype(BF16)

    return pl.pallas_call(
        body, name="qkv_prep", grid=(t // tm,),
        in_specs=[pl.BlockSpec((tm, 1536), lambda i: (i, 0)), pl.BlockSpec((1, D), lambda i: (0, 0)),
                  pl.BlockSpec((1, 256), lambda i: (0, 0))],
        out_specs=[pl.BlockSpec((tm, D), lambda i: (i, 0)), pl.BlockSpec((tm, KVX), lambda i: (i, 0))],
        out_shape=[SDS((t, D), BF16), SDS((t, KVX), BF16)], compiler_params=_cp("parallel", vmem=VMEM_BIG),
    )(u_qkv, qg_s, kg_t)


def _band_bias():
    j, r = _iota2((2 * BLK, 2 * BLK))
    diff = (r & (BLK - 1)) - j + BLK
    band = (diff >= 0) & (diff < BLK)
    return jnp.stack([jnp.where(band & (j >= BLK), 0.0, NEG), jnp.where(band, 0.0, NEG)]).astype(F32)


def _pair_rows(ref_or_val, hk):
    return jnp.concatenate([ref_or_val[:, 256 * hk:256 * hk + 128], ref_or_val[:, 256 * hk + 128:256 * hk + 256]], axis=0)


def _sink_row(sink_ref, hk, half):
    return jnp.concatenate([jnp.full((1, BLK), sink_ref[0, GROUP * hk + half], F32),
                            jnp.full((1, BLK), sink_ref[0, GROUP * hk + 2 + half], F32)], axis=1)


def _softmax_half(qpp, kvb, hk, half, bias, sink_ref):
    kx = kvb[:, 512 * half + 128 * hk:512 * half + 128 * hk + 128]
    vx = kvb[:, 1024 + 512 * half + 128 * hk:1024 + 512 * half + 128 * hk + 128]
    s = _dot_nt(kx, qpp) + bias
    sink = _sink_row(sink_ref, hk, half)
    m = jnp.maximum(jnp.max(s, axis=0, keepdims=True), sink)
    p = jnp.exp(s - m)
    es = jnp.exp(sink - m)
    inv = 1.0 / (jnp.sum(p, axis=0, keepdims=True) + es)
    return p * inv, es * inv, kx, vx


def _attn_fwd(qs, kvx, u_za, sinks, bias):
    t = qs.shape[0]
    nb = t // BLK

    def body(q_ref, kc_ref, kp_ref, za_ref, sink_ref, bias_ref, o_ref):
        kvb = jnp.concatenate([kp_ref[...], kc_ref[...]], axis=0)
        bias_v = bias_ref[...]
        cols = []
        for hk in range(N_KV):
            qpp = _pair_rows(q_ref, hk)
            lo = _softmax_half(qpp, kvb, hk, 0, bias_v, sink_ref)
            hi = _softmax_half(qpp, kvb, hk, 1, bias_v, sink_ref)
            opp = _dot_tn(jnp.concatenate([lo[0], hi[0]], axis=0).astype(BF16), jnp.concatenate([lo[3], hi[3]], axis=0))
            cols += [opp[:BLK], opp[BLK:]]
        za = za_ref[...].astype(F32)
        o_ref[...] = (jnp.concatenate(cols, axis=1) * (za * _sigmoid(za))).astype(BF16)

    prev = lambda n: jnp.maximum(n - 1, 0)
    return pl.pallas_call(
        body, name="attn_fwd", grid=(nb,),
        in_specs=[pl.BlockSpec((BLK, D), lambda n: (n, 0)),
                  pl.BlockSpec((BLK, KVX), lambda n: (n, 0)), pl.BlockSpec((BLK, KVX), lambda n: (prev(n), 0)),
                  pl.BlockSpec((BLK, D), lambda n: (n, 0)), pl.BlockSpec(memory_space=pltpu.SMEM),
                  pl.BlockSpec((None, 2 * BLK, 2 * BLK), lambda n: (jnp.minimum(n, 1), 0, 0))],
        out_specs=pl.BlockSpec((BLK, D), lambda n: (n, 0)), out_shape=SDS((t, D), BF16),
        compiler_params=_cp("parallel", vmem=VMEM_BIG),
    )(qs, kvx, kvx, u_za, sinks, bias)


def _out_proj_fwd(x, y_c, o, u_gl, gate_b, w_sm):
    t = x.shape[0]
    tm = min(512, t)

    def body(x_ref, yc_ref, o_ref, gla_ref, glb_ref, gb_ref, wco_ref, wao_ref, wout_ref,
             xn_ref, ya_ref, yb_ref, mg_ref):
        ya = _dot(yc_ref[...], wco_ref[...])
        yb = _dot(o_ref[...], wao_ref[...])
        gb = gb_ref[...]
        ga_ = _sigmoid(gla_ref[...].astype(F32) + gb[:, :D])
        gb_ = _sigmoid(glb_ref[...].astype(F32) + gb[:, D:])
        merged = (ga_ * ya + gb_ * yb).astype(BF16)
        ya_ref[...] = ya.astype(BF16)
        yb_ref[...] = yb.astype(BF16)
        mg_ref[...] = merged
        xn_ref[...] = x_ref[...] + _dot(merged, wout_ref[...])

    row = pl.BlockSpec((tm, D), lambda i: (i, 0))
    wspec = lambda a: pl.BlockSpec((None, D, D), lambda i: (a, 0, 0))
    return pl.pallas_call(
        body, name="out_proj_fwd", grid=(t // tm,),
        in_specs=[row, row, row, pl.BlockSpec((tm, D), lambda i: (i, 0)), pl.BlockSpec((tm, D), lambda i: (i, 1)),
                  pl.BlockSpec((1, 2 * D), lambda i: (0, 0)), wspec(0), wspec(1), wspec(2)],
        out_specs=[row, row, row, row],
        out_shape=[SDS((t, D), F32), SDS((t, D), BF16), SDS((t, D), BF16), SDS((t, D), BF16)],
        compiler_params=_cp("parallel", vmem=VMEM_BIG),
    )(x, y_c, o, u_gl, u_gl, gate_b, w_sm, w_sm, w_sm)


def _loss_head(y, tgt):
    t = y.shape[0]
    tm = min(512, t)

    def body(y_ref, t_ref, dy_ref, acc_ref):
        @pl.when(pl.program_id(0) == 0)
        def _():
            acc_ref[...] = jnp.zeros_like(acc_ref)
        err = y_ref[...] - t_ref[...]
        dy_ref[...] = err * (1.0 / D)
        sq = _fold8(err * err)
        tot = sq[:, 0:128]
        for k in range(1, D // 128):
            tot = tot + sq[:, 128 * k:128 * (k + 1)]
        acc_ref[...] += tot

    row = pl.BlockSpec((tm, D), lambda i: (i, 0))
    return pl.pallas_call(
        body, name="loss_head", grid=(t // tm,), in_specs=[row, row],
        out_specs=[row, pl.BlockSpec((8, 128), lambda i: (0, 0))],
        out_shape=[SDS((t, D), F32), SDS((8, 128), F32)], compiler_params=_cp("arbitrary"),
    )(y, tgt)


def _out_proj_bwd(dout, y_a, y_b, u_gl, gate_b, w_sm):
    t = dout.shape[0]
    tm = min(512, t)

    def body(do_ref, ya_ref, yb_ref, gla_ref, glb_ref, gb_ref, wco_ref, wao_ref, wout_ref,
             dya_ref, dyb_ref, dgl_ref, dyc_ref, dob_ref, dgb_ref):
        @pl.when(pl.program_id(0) == 0)
        def _():
            dgb_ref[...] = jnp.zeros_like(dgb_ref)
        dm = _dot_nt(do_ref[...].astype(BF16), wout_ref[...])
        gb = gb_ref[...]
        ga_ = _sigmoid(gla_ref[...].astype(F32) + gb[:, :D])
        gb_ = _sigmoid(glb_ref[...].astype(F32) + gb[:, D:])
        dya = (ga_ * dm).astype(BF16)
        dyb = (gb_ * dm).astype(BF16)
        dgla = ya_ref[...].astype(F32) * dm * (ga_ * (1.0 - ga_))
        dglb = yb_ref[...].astype(F32) * dm * (gb_ * (1.0 - gb_))
        dya_ref[...] = dya
        dyb_ref[...] = dyb
        dgl_ref[:, :D] = dgla.astype(BF16)
        dgl_ref[:, D:] = dglb.astype(BF16)
        dgb_ref[:, :D] += _fold8(dgla)
        dgb_ref[:, D:] += _fold8(dglb)
        dyc_ref[...] = _dot_nt(dya, wco_ref[...]).astype(BF16)
        dob_ref[...] = _dot_nt(dyb, wao_ref[...]).astype(BF16)

    row = pl.BlockSpec((tm, D), lambda i: (i, 0))
    wspec = lambda a: pl.BlockSpec((None, D, D), lambda i: (a, 0, 0))
    return pl.pallas_call(
        body, name="out_proj_bwd", grid=(t // tm,),
        in_specs=[row, row, row, pl.BlockSpec((tm, D), lambda i: (i, 0)), pl.BlockSpec((tm, D), lambda i: (i, 1)),
                  pl.BlockSpec((1, 2 * D), lambda i: (0, 0)), wspec(0), wspec(1), wspec(2)],
        out_specs=[row, row, pl.BlockSpec((tm, 2 * D), lambda i: (i, 0)), row, row,
                   pl.BlockSpec((8, 2 * D), lambda i: (0, 0))],
        out_shape=[SDS((t, D), BF16), SDS((t, D), BF16), SDS((t, 2 * D), BF16), SDS((t, D), BF16), SDS((t, D), BF16),
                   SDS((8, 2 * D), F32)],
        compiler_params=_cp("arbitrary", vmem=VMEM_BIG),
    )(dout, y_a, y_b, u_gl, u_gl, gate_b, w_sm, w_sm, w_sm)


def _small_wgrads(y_c, d_ya, o, d_yb, merged, dout):
    t = y_c.shape[0]
    tk = min(512, t)

    def body(yc_ref, dya_ref, o_ref, dyb_ref, mg_ref, do_ref, g_ref):
        @pl.when(pl.program_id(0) == 0)
        def _():
            g_ref[...] = jnp.zeros_like(g_ref)
        g_ref[0] += _dot_tn(yc_ref[...], dya_ref[...])
        g_ref[1] += _dot_tn(o_ref[...], dyb_ref[...])
        g_ref[2] += _dot_tn(mg_ref[...], do_ref[...].astype(BF16))

    row = pl.BlockSpec((tk, D), lambda k: (k, 0))
    return pl.pallas_call(
        body, name="small_wgrads", grid=(t // tk,), in_specs=[row] * 6,
        out_specs=pl.BlockSpec((3, D, D), lambda k: (0, 0, 0)), out_shape=SDS((3, D, D), F32),
        compiler_params=_cp("arbitrary", vmem=VMEM_BIG),
    )(y_c, d_ya, o, d_yb, merged, dout)


def _conv_bwd(d_yc, u_conv, conv_w):
    t = d_yc.shape[0]
    tm = min(256, t)
    hb = tm // 16
    last_halo = t // 16 - 1
    n_steps = t // tm

    def body(dy_ref, v_ref, b_ref, c_ref, z_ref, hv_ref, hc_ref, ndy_ref, nb_ref, nz_ref, w_ref, du_ref, dw_ref):
        i = pl.program_id(0)

        @pl.when(i == 0)
        def _():
            dw_ref[...] = jnp.zeros_like(dw_ref)
        v, c = v_ref[...].astype(F32), c_ref[...].astype(F32)
        b, z = b_ref[...].astype(F32), z_ref[...].astype(F32)
        cv = c * v
        halo = jnp.where(i > 0, hc_ref[...].astype(F32) * hv_ref[...].astype(F32), 0.0)
        row = lax.broadcasted_iota(jnp.int32, (tm, 1), 0)
        s1 = jnp.where(row == 0, halo[15:16], pltpu.roll(cv, 1, 0))
        s2 = jnp.where(row == 0, halo[14:15], jnp.where(row == 1, halo[15:16], pltpu.roll(cv, 2, 0)))
        w0, w1, w2 = w_ref[0:1, :], w_ref[1:2, :], w_ref[2:3, :]
        conv = w0 * s2 + w1 * s1 + w2 * cv
        sig = _sigmoid(z)
        sz = z * sig
        dsz = sig * (1.0 + z * (1.0 - sig))
        dy = dy_ref[...].astype(F32)
        dconv = dy * b * sz
        nz = nz_ref[...].astype(F32)
        nxt = ndy_ref[...].astype(F32) * nb_ref[...].astype(F32) * (nz * _sigmoid(nz))
        nxt = jnp.where(i < n_steps - 1, nxt, 0.0)
        a1 = jnp.where(row == tm - 1, nxt[0:1], pltpu.roll(dconv, tm - 1, 0))
        a2 = jnp.where(row == tm - 2, nxt[0:1], jnp.where(row == tm - 1, nxt[1:2], pltpu.roll(dconv, tm - 2, 0)))
        dcv = w2 * dconv + w1 * a1 + w0 * a2
        du_ref[:, 0:D] = (dcv * c).astype(BF16)
        du_ref[:, D:2 * D] = (dy * conv * sz).astype(BF16)
        du_ref[:, 2 * D:3 * D] = (dcv * v).astype(BF16)
        du_ref[:, 3 * D:4 * D] = (dy * b * conv * dsz).astype(BF16)
        r8 = lax.broadcasted_iota(jnp.int32, (8, 1), 0)
        dw_ref[...] += jnp.where(r8 == 0, jnp.sum(dconv * s2, axis=0, keepdims=True),
                                 jnp.where(r8 == 1, jnp.sum(dconv * s1, axis=0, keepdims=True),
                                           jnp.where(r8 == 2, jnp.sum(dconv * cv, axis=0, keepdims=True), 0.0)))

    def col(k):
        return pl.BlockSpec((tm, D), lambda i: (i, k))

    def halo(k):
        return pl.BlockSpec((16, D), lambda i: (jnp.maximum(i * hb - 1, 0), k))

    def nxt(k):
        return pl.BlockSpec((16, D), lambda i: (jnp.minimum((i + 1) * hb, last_halo), k))

    return pl.pallas_call(
        body, name="conv_bwd", grid=(t // tm,),
        in_specs=[col(0), col(0), col(1), col(2), col(3), halo(0), halo(2), nxt(0), nxt(1), nxt(3),
                  pl.BlockSpec((3, D), lambda i: (0, 0))],
        out_specs=[pl.BlockSpec((tm, 4 * D), lambda i: (i, 0)), pl.BlockSpec((8, D), lambda i: (0, 0))],
        out_shape=[SDS((t, 4 * D), BF16), SDS((8, D), F32)],
        compiler_params=_cp("arbitrary", vmem=VMEM_BIG),
    )(d_yc, u_conv, u_conv, u_conv, u_conv, u_conv, u_conv, d_yc, u_conv, u_conv, conv_w)


def _attn_bwd(d_o, qs, kvx, u_za, sinks, bias):
    t = d_o.shape[0]
    nb = t // BLK

    def body(q_ref, kc_ref, kp_ref, za_ref, do_ref, sink_ref, bias_ref, dq_ref, dkv_ref, dza_ref, dsk_ref, carry_ref):
        n = pl.program_id(0)

        @pl.when(n == 0)
        def _():
            carry_ref[...] = jnp.zeros_like(carry_ref)
            dsk_ref[...] = jnp.zeros_like(dsk_ref)

        live = n < nb
        kvb = jnp.concatenate([kp_ref[...], kc_ref[...]], axis=0)
        bias_v = bias_ref[...]
        za = za_ref[...].astype(F32)
        sig = _sigmoid(za)
        dsa = sig * (1.0 + za * (1.0 - sig))
        do = jnp.where(live, do_ref[...].astype(F32), 0.0)
        dattn = (do * (za * sig)).astype(BF16)
        lo_lanes = lax.broadcasted_iota(jnp.int32, (1, 128), 1) < HEAD
        dq_cols, attn_cols, dk_cols, dv_cols, dsk_rows = [], [], [], [], []
        for hk in range(N_KV):
            qpp = _pair_rows(q_ref, hk)
            dapp = _pair_rows(dattn, hk)
            probs, dss, xk, xv = [], [], [], []
            for half in range(2):
                prob, psink, kx, vx = _softmax_half(qpp, kvb, hk, half, bias_v, sink_ref)
                dp = _dot_nt(vx, dapp)
                drow = jnp.sum(prob * dp, axis=0, keepdims=True)
                ds = (prob * (dp - drow)).astype(BF16)
                prob_b = prob.astype(BF16)
                xk.append(_dot(ds, qpp))
                xv.append(_dot(prob_b, dapp))
                probs.append(prob_b)
                dss.append(ds)
                dsk_rows.append(-psink * drow)
            kcat = jnp.concatenate([kvb[:, 128 * hk:128 * hk + 128], kvb[:, 512 + 128 * hk:512 + 128 * hk + 128]], axis=0)
            vcat = jnp.concatenate([kvb[:, 1024 + 128 * hk:1024 + 128 * hk + 128],
                                    kvb[:, 1536 + 128 * hk:1536 + 128 * hk + 128]], axis=0)
            app = _dot_tn(jnp.concatenate(probs, axis=0), vcat)
            dqpp = _dot_tn(jnp.concatenate(dss, axis=0), kcat)
            dq_cols += [dqpp[:BLK], dqpp[BLK:]]
            attn_cols += [app[:BLK], app[BLK:]]
            dk_cols.append(jnp.where(lo_lanes, xk[0], xk[1]))
            dv_cols.append(jnp.where(lo_lanes, xv[0], xv[1]))

        @pl.when(live)
        def _():
            dq_ref[...] = jnp.concatenate(dq_cols, axis=1).astype(BF16)
            dza_ref[...] = (do * jnp.concatenate(attn_cols, axis=1) * dsa).astype(BF16)

        band = jnp.concatenate(dk_cols + dv_cols, axis=1)
        dkv_ref[...] = (band[:BLK] + carry_ref[...]).astype(BF16)
        carry_ref[...] = band[BLK:]
        dsk_ref[...] += jnp.broadcast_to(jnp.concatenate(dsk_rows, axis=1), (8, 2 * N_KV * 2 * BLK))

    cur = lambda n: jnp.minimum(n, nb - 1)
    prev = lambda n: jnp.maximum(n - 1, 0)
    return pl.pallas_call(
        body, name="attn_bwd", grid=(nb + 1,),
        in_specs=[pl.BlockSpec((BLK, D), lambda n: (cur(n), 0)),
                  pl.BlockSpec((BLK, KVX), lambda n: (cur(n), 0)), pl.BlockSpec((BLK, KVX), lambda n: (prev(n), 0)),
                  pl.BlockSpec((BLK, D), lambda n: (cur(n), 0)), pl.BlockSpec((BLK, D), lambda n: (cur(n), 0)),
                  pl.BlockSpec(memory_space=pltpu.SMEM),
                  pl.BlockSpec((None, 2 * BLK, 2 * BLK), lambda n: (jnp.minimum(n, 1), 0, 0))],
        out_specs=[pl.BlockSpec((BLK, D), lambda n: (cur(n), 0)), pl.BlockSpec((BLK, D), lambda n: (prev(n), 0)),
                   pl.BlockSpec((BLK, D), lambda n: (cur(n), 0)), pl.BlockSpec((8, 2 * D), lambda n: (0, 0))],
        out_shape=[SDS((t, D), BF16), SDS((t, D), BF16), SDS((t, D), BF16), SDS((8, 2 * D), F32)],
        scratch_shapes=[pltpu.VMEM((BLK, D), F32)],
        compiler_params=_cp("arbitrary", vmem=VMEM_BIG),
    )(qs, kvx, kvx, u_za, d_o, sinks, bias)


def _qkv_post(u_qkv, dqs, dkv, dza, qg_s, kg_t):
    t = u_qkv.shape[0]
    tm = min(512, t)

    def norm_bwd(x, dy, g):
        r = lax.rsqrt(_head_sum(x * x) * (1.0 / HEAD) + EPS)
        xhat = x * r
        dxh = dy * g
        return r * (dxh - xhat * (_head_sum(dxh * xhat) * (1.0 / HEAD))), _fold8(dy * xhat)

    def body(u_ref, dq_ref, dkv_ref, dza_ref, qg_ref, kg_ref, du_ref, dqg_ref, dkg_ref):
        @pl.when(pl.program_id(0) == 0)
        def _():
            dqg_ref[...] = jnp.zeros_like(dqg_ref)
            dkg_ref[...] = jnp.zeros_like(dkg_ref)
        dq, gq = norm_bwd(u_ref[:, 0:D].astype(F32), dq_ref[...].astype(F32), qg_ref[...])
        fold = _fold_mat()
        dk, gk = norm_bwd(u_ref[:, D:D + 256].astype(F32), _dot(dkv_ref[:, 0:512], fold), kg_ref[...])
        du_ref[:, 0:D] = dq.astype(BF16)
        du_ref[:, D:2 * D] = dza_ref[...]
        du_ref[:, 2 * D:2 * D + 256] = dk.astype(BF16)
        du_ref[:, 2 * D + 256:2 * D + 512] = _dot(dkv_ref[:, 512:1024], fold).astype(BF16)
        dqg_ref[...] += gq
        dkg_ref[...] += gk

    row = pl.BlockSpec((tm, D), lambda i: (i, 0))
    return pl.pallas_call(
        body, name="qkv_post", grid=(t // tm,),
        in_specs=[pl.BlockSpec((tm, 1536), lambda i: (i, 0)), row, row, row,
                  pl.BlockSpec((1, D), lambda i: (0, 0)), pl.BlockSpec((1, 256), lambda i: (0, 0))],
        out_specs=[pl.BlockSpec((tm, 2560), lambda i: (i, 0)), pl.BlockSpec((8, D), lambda i: (0, 0)),
                   pl.BlockSpec((8, 256), lambda i: (0, 0))],
        out_shape=[SDS((t, 2560), BF16), SDS((8, D), F32), SDS((8, 256), F32)],
        compiler_params=_cp("arbitrary", vmem=VMEM_BIG),
    )(u_qkv, dqs, dkv, dza, qg_s, kg_t)


N_GRAN = N_IN // CB


def _du_granule(j):
    attn = jnp.where(j == 10, 4, jnp.where(j >= 11, j - 9, j - 8))
    return jnp.clip(j, 0, 7), jnp.clip(attn, 0, 4), jnp.clip(j - 13, 0, 3)


def _du_select(j, refs, fn):
    for ref, lo, hi in zip(refs, (0, 8, 13), (8, 13, 17)):
        @pl.when((j >= lo) & (j < hi))
        def _():
            fn(ref)


def _in_proj_bwd(du, w_full):
    t = du[0].shape[0]
    tm = min(1024, t)

    def body(a0, a1, a2, b_ref, dh_ref, acc_ref):
        k = pl.program_id(1)

        @pl.when(k == 0)
        def _():
            acc_ref[...] = jnp.zeros_like(acc_ref)

        def add(a_ref):
            acc_ref[...] += _dot_nt(a_ref[...], b_ref[...])
        _du_select(k, (a0, a1, a2), add)

        @pl.when(k == N_GRAN - 1)
        def _():
            dh_ref[...] = acc_ref[...]

    seg = lambda q: pl.BlockSpec((tm, CB), lambda i, k: (i, _du_granule(k)[q]))
    return pl.pallas_call(
        body, name="in_proj_bwd", grid=(t // tm, N_GRAN),
        in_specs=[seg(0), seg(1), seg(2), pl.BlockSpec((D, CB), lambda i, k: (0, k))],
        out_specs=pl.BlockSpec((tm, D), lambda i, k: (i, 0)), out_shape=SDS((t, D), F32),
        scratch_shapes=[pltpu.VMEM((tm, D), F32)],
        compiler_params=_cp("parallel", "arbitrary", vmem=VMEM_BIG),
    )(*du, w_full)


def _rmsnorm_bwd(dh, x, g, dout):
    t = x.shape[0]
    tm = min(256, t)

    def body(dh_ref, x_ref, g_ref, do_ref, dx_ref, dg_ref):
        @pl.when(pl.program_id(0) == 0)
        def _():
            dg_ref[...] = jnp.zeros_like(dg_ref)
        dh = dh_ref[...]
        xv = x_ref[...]
        r = lax.rsqrt(jnp.mean(xv * xv, axis=-1, keepdims=True) + EPS)
        xhat = xv * r
        dg_ref[...] += _fold8(dh * xhat)
        dxh = dh * g_ref[...]
        dx_ref[...] = do_ref[...] + r * (dxh - xhat * jnp.mean(dxh * xhat, axis=-1, keepdims=True))

    row = pl.BlockSpec((tm, D), lambda i: (i, 0))
    return pl.pallas_call(
        body, name="rmsnorm_bwd", grid=(t // tm,),
        in_specs=[row, row, pl.BlockSpec((1, D), lambda i: (0, 0)), row],
        out_specs=[row, pl.BlockSpec((8, D), lambda i: (0, 0))],
        out_shape=[SDS((t, D), F32), SDS((8, D), F32)], compiler_params=_cp("arbitrary"),
    )(dh, x, g, dout)


def _in_proj_wgrad(ht, du):
    t = ht.shape[1]
    tk = min(2048, t)

    def body(h_ref, b0, b1, b2, g_ref):
        j, k = pl.program_id(0), pl.program_id(1)

        @pl.when(k == 0)
        def _():
            g_ref[...] = jnp.zeros_like(g_ref)

        def add(b_ref):
            g_ref[...] += _dot(h_ref[...], b_ref[...])
        _du_select(j, (b0, b1, b2), add)

    seg = lambda q: pl.BlockSpec((tk, CB), lambda j, k: (k, _du_granule(j)[q]))
    return pl.pallas_call(
        body, name="in_proj_wgrad", grid=(N_GRAN, t // tk),
        in_specs=[pl.BlockSpec((D, tk), lambda j, k: (0, k)), seg(0), seg(1), seg(2)],
        out_specs=pl.BlockSpec((D, CB), lambda j, k: (0, j)), out_shape=SDS((D, N_IN), F32),
        compiler_params=_cp("parallel", "arbitrary", vmem=VMEM_BIG),
    )(ht, *du)


def _swap_halves(g_in, g_sm):
    def body(*refs):
        ins, outs = refs[:2 * N_LAYERS], refs[2 * N_LAYERS:4 * N_LAYERS]
        send, recv = refs[4 * N_LAYERS:]
        x, y, c = _mesh_pos()
        cps = []
        for l in range(N_LAYERS):
            for a in range(2):
                s = 2 * l + a
                src = ins[s].at[1 - c] if a == 0 else ins[s].at[:, :, 1 - c]
                cp = pltpu.make_async_remote_copy(src_ref=src, dst_ref=outs[s], send_sem=send.at[s], recv_sem=recv.at[s],
                                                  device_id=(x, y, 1 - c), device_id_type=MESH)
                cp.start()
                cps.append(cp)
        for cp in cps:
            cp.wait()

    out_shape, args = [], []
    for l in range(N_LAYERS):
        out_shape += [SDS((512, N_IN), F32), SDS((3, 4, 128, D), F32)]
        args += [g_in[l], g_sm[l]]
    res = pl.pallas_call(
        body, name="swap_halves", in_specs=[ANY] * (2 * N_LAYERS), out_specs=[ANY] * (2 * N_LAYERS), out_shape=out_shape,
        scratch_shapes=[pltpu.SemaphoreType.DMA((2 * N_LAYERS,))] * 2,
    )(*args)
    return [(res[2 * l], res[2 * l + 1]) for l in range(N_LAYERS)]


def _add_halves_in(c_idx, g_in, r_in):
    def body(c_ref, a_ref, b_ref, f_ref, h_ref):
        s = a_ref[...] + b_ref[...]
        f_ref[...] = s
        h_ref[...] = s.astype(BF16)

    blk = pl.BlockSpec((128, N_IN), lambda i, c: (i, 0))
    return pl.pallas_call(
        body, name="add_halves_in",
        grid_spec=pltpu.PrefetchScalarGridSpec(
            num_scalar_prefetch=1, grid=(4,),
            in_specs=[pl.BlockSpec((None, 128, N_IN), lambda i, c: (c[0], i, 0)), blk], out_specs=[blk, blk]),
        out_shape=[SDS((512, N_IN), F32), SDS((512, N_IN), BF16)], compiler_params=_cp("parallel", vmem=VMEM_BIG),
    )(c_idx, g_in, r_in)


def _add_halves_sm(c_idx, g_sm, r_sm):
    def body(c_ref, a_ref, b_ref, f_ref, h_ref):
        s = a_ref[...] + b_ref[...]
        f_ref[...] = s
        h_ref[...] = s.astype(BF16)

    blk = pl.BlockSpec((1, 4, 128, D), lambda a, c: (a, 0, 0, 0))
    return pl.pallas_call(
        body, name="add_halves_sm",
        grid_spec=pltpu.PrefetchScalarGridSpec(
            num_scalar_prefetch=1, grid=(3,),
            in_specs=[pl.BlockSpec((1, 4, None, 128, D), lambda a, c: (a, 0, c[0], 0, 0)), blk], out_specs=[blk, blk]),
        out_shape=[SDS((3, 4, 128, D), F32), SDS((3, 4, 128, D), BF16)], compiler_params=_cp("parallel"),
    )(c_idx, g_sm, r_sm)


def _scatter_chips(h_in, h_sm):
    n_cp = N_LAYERS * 2 * 3

    def body(*refs):
        ins, outs = refs[:2 * N_LAYERS], refs[2 * N_LAYERS:4 * N_LAYERS]
        send, recv = refs[4 * N_LAYERS:]
        x, y, c = _mesh_pos()
        cps = []
        for l in range(N_LAYERS):
            for a in range(2):
                for k, chip in enumerate(_other_chips(x, y)):
                    s = (2 * l + a) * 3 + k
                    their = 2 * chip[0] + chip[1]
                    if a == 0:
                        src = ins[2 * l].at[:, pl.ds(pl.multiple_of(their * SH_IN, 128), SH_IN)]
                    else:
                        src = ins[2 * l + 1].at[:, their]
                    cp = pltpu.make_async_remote_copy(src_ref=src, dst_ref=outs[2 * l + a].at[k], send_sem=send.at[s],
                                                      recv_sem=recv.at[s], device_id=(*chip, c), device_id_type=MESH)
                    cp.start()
                    cps.append(cp)
        for cp in cps:
            cp.wait()

    out_shape, args = [], []
    for l in range(N_LAYERS):
        out_shape += [SDS((3, 512, SH_IN), BF16), SDS((3, 3, 128, D), BF16)]
        args += [h_in[l], h_sm[l]]
    res = pl.pallas_call(
        body, name="scatter_chips", in_specs=[ANY] * (2 * N_LAYERS), out_specs=[ANY] * (2 * N_LAYERS), out_shape=out_shape,
        scratch_shapes=[pltpu.SemaphoreType.DMA((n_cp,))] * 2,
    )(*args)
    return [(res[2 * l], res[2 * l + 1]) for l in range(N_LAYERS)]


def _final_sum_in(chip_idx, f_in, r_in):
    def body(j_ref, a_ref, r_ref, o_ref):
        o_ref[...] = a_ref[...] + r_ref[0].astype(F32) + r_ref[1].astype(F32) + r_ref[2].astype(F32)

    return pl.pallas_call(
        body, name="final_sum_in",
        grid_spec=pltpu.PrefetchScalarGridSpec(
            num_scalar_prefetch=1, grid=(4,),
            in_specs=[pl.BlockSpec((128, SH_IN), lambda i, j: (i, j[0])), pl.BlockSpec((3, 128, SH_IN), lambda i, j: (0, i, 0))],
            out_specs=pl.BlockSpec((128, SH_IN), lambda i, j: (i, 0))),
        out_shape=SDS((512, SH_IN), F32), compiler_params=_cp("parallel"),
    )(chip_idx, f_in, r_in)


def _final_sum_sm(chip_idx, f_sm, r_sm):
    def body(j_ref, a_ref, r_ref, o_ref):
        o_ref[...] = a_ref[...] + r_ref[0].astype(F32) + r_ref[1].astype(F32) + r_ref[2].astype(F32)

    return pl.pallas_call(
        body, name="final_sum_sm",
        grid_spec=pltpu.PrefetchScalarGridSpec(
            num_scalar_prefetch=1, grid=(3,),
            in_specs=[pl.BlockSpec((1, None, 128, D), lambda a, j: (a, j[0], 0, 0)),
                      pl.BlockSpec((3, 1, 128, D), lambda a, j: (0, a, 0, 0))],
            out_specs=pl.BlockSpec((1, 128, D), lambda a, j: (a, 0, 0))),
        out_shape=SDS((3, 128, D), F32), compiler_params=_cp("parallel"),
    )(chip_idx, f_sm, r_sm)


def _join_halves(t_in, t_sm):
    n_cp = N_LAYERS * 4

    def body(*refs):
        ins, outs = refs[:2 * N_LAYERS], refs[2 * N_LAYERS:2 * N_LAYERS + 4]
        send, recv, loc_in, loc_out, stage_in, stage_sm = refs[2 * N_LAYERS + 4:]
        x, y, c = _mesh_pos()
        cps, own = [], []
        for l in range(N_LAYERS):
            for a in range(4):
                s = 4 * l + a
                if a == 0:
                    src = ins[2 * l]
                    dst = outs[0].at[l, pl.ds(pl.multiple_of(c * 512, 512), 512), :]
                else:
                    src = ins[2 * l + 1].at[a - 1]
                    dst = outs[a].at[l, pl.ds(pl.multiple_of(c * 128, 128), 128), :]
                own.append((src, dst, min(a, 1)))
                cp = pltpu.make_async_remote_copy(src_ref=src, dst_ref=dst, send_sem=send.at[s], recv_sem=recv.at[s],
                                                  device_id=(x, y, 1 - c), device_id_type=MESH)
                cp.start()
                cps.append(cp)
        _staged_copies(own, (stage_in, stage_sm), loc_in, loc_out)
        for l in range(N_LAYERS):
            for a in range(4):
                s = 4 * l + a
                if a == 0:
                    got = outs[0].at[l, pl.ds(pl.multiple_of((1 - c) * 512, 512), 512), :]
                else:
                    got = outs[a].at[l, pl.ds(pl.multiple_of((1 - c) * 128, 128), 128), :]
                pltpu.make_async_remote_copy(src_ref=got, dst_ref=got, send_sem=send.at[s], recv_sem=recv.at[s],
                                             device_id=(x, y, 1 - c), device_id_type=MESH).wait_recv()
        for cp in cps:
            cp.wait_send()

    args = []
    for l in range(N_LAYERS):
        args += [t_in[l], t_sm[l]]
    sm = SDS((N_LAYERS, SH_ROW, D), F32)
    return pl.pallas_call(
        body, name="join_halves", in_specs=[ANY] * (2 * N_LAYERS), out_specs=[ANY] * 4,
        out_shape=[SDS((N_LAYERS, D, SH_IN), F32), sm, sm, sm],
        scratch_shapes=[pltpu.SemaphoreType.DMA((n_cp,))] * 4
        + [pltpu.VMEM((2, 512, SH_IN), F32), pltpu.VMEM((2, 128, D), F32)],
        compiler_params=_cp(vmem=VMEM_BIG),
    )(*args)


def _adam_math(w, g, m, v):
    m = ADAM_B1 * m + (1.0 - ADAM_B1) * g
    v = ADAM_B2 * v + (1.0 - ADAM_B2) * (g * g)
    m_hat = m / (1.0 - ADAM_B1 ** ADAM_STEP)
    v_hat = v / (1.0 - ADAM_B2 ** ADAM_STEP)
    delta = -ADAM_LR * (m_hat / (jnp.sqrt(v_hat) + ADAM_EPS) + ADAM_WD * w)
    return delta, m, v


def _adamw_big(w, g, m, v, name):
    rows, cols = w.shape
    tr = 128

    def body(w_ref, g_ref, m_ref, v_ref, d_ref, nm_ref, nv_ref):
        d_ref[...], nm_ref[...], nv_ref[...] = _adam_math(w_ref[...], g_ref[...], m_ref[...], v_ref[...])

    blk = pl.BlockSpec((tr, cols), lambda i: (i, 0))
    return pl.pallas_call(
        body, name=name, grid=(rows // tr,), in_specs=[blk] * 4, out_specs=[blk] * 3,
        out_shape=[SDS((rows, cols), F32)] * 3, compiler_params=_cp("parallel", vmem=VMEM_BIG),
    )(w, g, m, v)


def _adamw_small(ws, gs, ms, vs):
    n = len(ws)

    def body(*refs):
        for k in range(n):
            w_ref, g_ref, m_ref, v_ref = (refs[q * n + k] for q in range(4))
            d, nm, nv = _adam_math(w_ref[...], g_ref[...], m_ref[...], v_ref[...])
            refs[4 * n + k][...] = d
            refs[5 * n + k][...] = nm
            refs[6 * n + k][...] = nv

    vm = pl.BlockSpec(memory_space=pltpu.VMEM)
    shapes = [SDS(w.shape, F32) for w in ws]
    res = pl.pallas_call(
        body, name="adamw_small", in_specs=[vm] * (4 * n), out_specs=[vm] * (3 * n), out_shape=shapes * 3,
    )(*ws, *gs, *ms, *vs)
    return res[:n], res[n:2 * n], res[2 * n:]


def _pad_rows(a, rows):
    flat = a.reshape(-1)
    return jnp.pad(flat, (0, rows * 128 - flat.shape[0])).reshape(rows, 128)


def kernel(x, norm_g, w_in, conv_w, q_norm_g, k_norm_g, sinks, w_conv_out, w_attn_out, gate_b, w_out, loss_target, m_norm_g, m_w_in, m_conv_w, m_q_norm_g, m_k_norm_g, m_sinks, m_w_conv_out, m_w_attn_out, m_gate_b, m_w_out, v_norm_g, v_w_in, v_conv_w, v_q_norm_g, v_k_norm_g, v_sinks, v_w_conv_out, v_w_attn_out, v_gate_b, v_w_out):
    xi, yi, ci = _mesh_pos()
    chip = 2 * xi + yi
    c_idx = jnp.reshape(ci, (1,)).astype(jnp.int32)
    chip_idx = jnp.reshape(chip, (1,)).astype(jnp.int32)
    t = x.shape[1]
    xs = [x.reshape(t, D)]
    tgt = loss_target.reshape(t, D)

    full_w = _gather_weights(_cast_w_in(w_in), _cast_w_small(w_conv_out, w_attn_out, w_out))
    placed = lax.dynamic_update_slice(jnp.zeros((N_LAYERS, 3, D), F32),
                                      jnp.where(ci == 0, conv_w, 0.0), (0, 0, chip * SH_ROW))
    conv_full = _allreduce_small(placed.reshape(96, 128)).reshape(N_LAYERS, 3, D)

    qg_s = jnp.tile(q_norm_g, (1, N_Q)) * SCALE
    kg_t = jnp.tile(k_norm_g, (1, N_KV))
    bias = _band_bias()
    saved = []
    for l in range(N_LAYERS):
        w_full, w_sm = full_w[l]
        h, ht = _rmsnorm_fwd(xs[l], norm_g[l:l + 1])
        u_conv = _in_proj(h, w_full, (0, 8), "in_proj_conv")
        u_qkv = _in_proj(h, w_full, (8, 3), "in_proj_qkv")
        u_za = _in_proj(h, w_full, SEG_ZA, "in_proj_za")
        u_gl = _in_proj(h, w_full, SEG_GL, "in_proj_gl")
        y_c = _conv_fwd(u_conv, conv_full[l])
        qs, kvx = _qkv_prep(u_qkv, qg_s[l:l + 1], kg_t[l:l + 1])
        o = _attn_fwd(qs, kvx, u_za, sinks[l:l + 1], bias)
        x_next, y_a, y_b, merged = _out_proj_fwd(xs[l], y_c, o, u_gl, gate_b[l:l + 1], w_sm)
        xs.append(x_next)
        saved.append((ht, u_conv, u_qkv, u_za, u_gl, y_c, o, y_a, y_b, merged, qs, kvx))

    dout, sq = _loss_head(xs[N_LAYERS], tgt)
    loss = lax.psum(jnp.sum(sq) * (0.5 / D), ("x", "y", "c"))

    g_in, g_sm, small = [None] * N_LAYERS, [None] * N_LAYERS, [None] * N_LAYERS
    for l in reversed(range(N_LAYERS)):
        w_full, w_sm = full_w[l]
        ht, u_conv, u_qkv, u_za, u_gl, y_c, o, y_a, y_b, merged, qs, kvx = saved[l]
        d_ya, d_yb, du_gl, d_yc, d_o, dgb = _out_proj_bwd(dout, y_a, y_b, u_gl, gate_b[l:l + 1], w_sm)
        g_sm[l] = _small_wgrads(y_c, d_ya, o, d_yb, merged, dout)
        du_conv, dcw = _conv_bwd(d_yc, u_conv, conv_full[l])
        dqs, dkv, dza, dsk = _attn_bwd(d_o, qs, kvx, u_za, sinks[l:l + 1], bias)
        dsk = jnp.sum(dsk[0].reshape(N_KV, 2, 2, BLK), axis=-1).transpose(0, 2, 1).reshape(N_Q)
        du_attn, dqg, dkg = _qkv_post(u_qkv, dqs, dkv, dza, qg_s[l:l + 1], kg_t[l:l + 1])
        du = (du_conv, du_attn, du_gl)
        g_in[l] = _in_proj_wgrad(ht, du)
        dout, dng = _rmsnorm_bwd(_in_proj_bwd(du, w_full), xs[l], norm_g[l:l + 1], dout)
        small[l] = (jnp.sum(dng, axis=0), SCALE * jnp.sum(dqg.reshape(8 * N_Q, HEAD), axis=0),
                    jnp.sum(dkg.reshape(8 * N_KV, HEAD), axis=0), dsk, jnp.sum(dgb, axis=0), dcw[:3])
    grad_x = dout.reshape(1, t, D)

    stack = lambda k: jnp.stack([small[l][k] for l in range(N_LAYERS)])
    pack = jnp.concatenate([_pad_rows(stack(0), 32), _pad_rows(stack(1), 8), _pad_rows(stack(2), 8),
                            _pad_rows(stack(3), 8), _pad_rows(stack(4), 64), _pad_rows(stack(5), 96)], axis=0)
    red = _allreduce_small(pack)
    g_norm_g = red[0:32].reshape(N_LAYERS, D)
    g_q_norm_g = red[32:40].reshape(-1)[:N_LAYERS * HEAD].reshape(N_LAYERS, HEAD)
    g_k_norm_g = red[40:48].reshape(-1)[:N_LAYERS * HEAD].reshape(N_LAYERS, HEAD)
    g_sinks = red[48:56].reshape(-1)[:N_LAYERS * N_Q].reshape(N_LAYERS, N_Q)
    g_gate_b = red[56:120].reshape(N_LAYERS, 2 * D)
    g_conv_full = red[120:216].reshape(N_LAYERS, 3, D)
    g_conv_w = lax.dynamic_slice(g_conv_full, (0, 0, chip * SH_ROW), (N_LAYERS, 3, SH_ROW))

    g_in_v = [g.reshape(2, 512, N_IN) for g in g_in]
    g_sm_v = [g.reshape(3, 4, 2, 128, D) for g in g_sm]
    got = _swap_halves(g_in_v, g_sm_v)
    h_in, h_sm, f_in, f_sm = [], [], [], []
    for l in range(N_LAYERS):
        f, hb = _add_halves_in(c_idx, g_in_v[l], got[l][0])
        f_in.append(f)
        h_in.append(hb)
        f, hb = _add_halves_sm(c_idx, g_sm_v[l], got[l][1])
        f_sm.append(f)
        h_sm.append(hb)
    parts = _scatter_chips(h_in, h_sm)
    t_in = [_final_sum_in(chip_idx, f_in[l], parts[l][0]) for l in range(N_LAYERS)]
    t_sm = [_final_sum_sm(chip_idx, f_sm[l], parts[l][1]) for l in range(N_LAYERS)]
    g_w_in, g_w_co, g_w_ao, g_w_out = _join_halves(t_in, t_sm)

    r_in = N_LAYERS * D
    d_in, nm_in, nv_in = (a.reshape(N_LAYERS, D, SH_IN) for a in _adamw_big(
        w_in.reshape(r_in, SH_IN), g_w_in.reshape(r_in, SH_IN), m_w_in.reshape(r_in, SH_IN),
        v_w_in.reshape(r_in, SH_IN), "adamw_w_in"))
    r_sm = N_LAYERS * SH_ROW
    big = {}
    for nm, w, g, m, v in (("co", w_conv_out, g_w_co, m_w_conv_out, v_w_conv_out),
                           ("ao", w_attn_out, g_w_ao, m_w_attn_out, v_w_attn_out),
                           ("out", w_out, g_w_out, m_w_out, v_w_out)):
        big[nm] = tuple(a.reshape(N_LAYERS, SH_ROW, D) for a in _adamw_big(
            w.reshape(r_sm, D), g.reshape(r_sm, D), m.reshape(r_sm, D), v.reshape(r_sm, D), "adamw_w_small"))
    sm_w = [norm_g, conv_w, q_norm_g, k_norm_g, sinks, gate_b]
    sm_g = [g_norm_g, g_conv_w, g_q_norm_g, g_k_norm_g, g_sinks, g_gate_b]
    sm_m = [m_norm_g, m_conv_w, m_q_norm_g, m_k_norm_g, m_sinks, m_gate_b]
    sm_v = [v_norm_g, v_conv_w, v_q_norm_g, v_k_norm_g, v_sinks, v_gate_b]
    sd, snm, snv = _adamw_small(sm_w, sm_g, sm_m, sm_v)

    def order(norm, w_in_, conv, qn, kn, sk, co, ao, gb, wo):
        return [norm, w_in_, conv, qn, kn, sk, co, ao, gb, wo]

    grads = order(g_norm_g, g_w_in, g_conv_w, g_q_norm_g, g_k_norm_g, g_sinks, g_w_co, g_w_ao, g_gate_b, g_w_out)
    deltas = order(sd[0], d_in, sd[1], sd[2], sd[3], sd[4], big["co"][0], big["ao"][0], sd[5], big["out"][0])
    new_m = order(snm[0], nm_in, snm[1], snm[2], snm[3], snm[4], big["co"][1], big["ao"][1], snm[5], big["out"][1])
    new_v = order(snv[0], nv_in, snv[1], snv[2], snv[3], snv[4], big["co"][2], big["ao"][2], snv[5], big["out"][2])
    return (loss, grad_x, *grads, *deltas, *new_m, *new_v)
```

```python
import functools

import jax
import jax.numpy as jnp
from jax import lax
from jax.experimental import pallas as pl
from jax.experimental.pallas import tpu as pltpu

F32, BF16 = jnp.float32, jnp.bfloat16
SDS = jax.ShapeDtypeStruct
MESH = pl.DeviceIdType.MESH
ANY = pl.BlockSpec(memory_space=pl.ANY)

D = 1024
N_IN = 8704
N_LAYERS = 4
N_Q, N_KV, HEAD = 16, 4, 64
GROUP = N_Q // N_KV
BLK = 128
EPS = 1e-6
NEG = -1e30
SCALE = HEAD ** -0.5
SH_IN = N_IN // 4
SH_ROW = D // 4
CB = 512
SEG_CONV, SEG_Q, SEG_KV, SEG_ZA, SEG_GL = (0, 8), (8, 2), (10, 1), (11, 2), (13, 4)
VMEM_BIG = 56 * 1024 * 1024

ADAM_LR, ADAM_B1, ADAM_B2, ADAM_EPS, ADAM_WD, ADAM_STEP = 0.001, 0.9, 0.999, 1e-08, 0.01, 10


def _cp(*sem, vmem=None):
    return pltpu.CompilerParams(dimension_semantics=sem if sem else None, vmem_limit_bytes=vmem)


def _sigmoid(z):
    return 1.0 / (1.0 + jnp.exp(-z))


def _dot(a, b):
    return jnp.dot(a, b, preferred_element_type=F32)


def _dot_nt(a, b):
    return lax.dot_general(a, b, (((1,), (1,)), ((), ())), preferred_element_type=F32)


def _dot_tn(a, b):
    return lax.dot_general(a, b, (((0,), (0,)), ((), ())), preferred_element_type=F32)


def _rms(xh):
    r = lax.rsqrt(jnp.mean(xh * xh, axis=-1, keepdims=True) + EPS)
    return xh * r, r


def _fold8(v):
    return jnp.sum(v.reshape(v.shape[0] // 8, 8, v.shape[1]), axis=0)


def _cast_w_in(chip_idx, w, layer):
    def body(j_ref, i_ref, o_ref):
        o_ref[...] = i_ref[...].astype(BF16)

    return pl.pallas_call(
        body, name="cast_w_in",
        grid_spec=pltpu.PrefetchScalarGridSpec(
            num_scalar_prefetch=1, grid=(2,),
            in_specs=[pl.BlockSpec((None, 512, SH_IN), lambda i, j: (layer, i, 0))],
            out_specs=pl.BlockSpec((512, SH_IN), lambda i, j: (i, j[0]))),
        out_shape=SDS((D, N_IN), BF16), compiler_params=_cp("parallel"),
    )(chip_idx, w)


def _cast_w_small(chip_idx, a, b, c, layer):
    def body(j_ref, a_ref, b_ref, c_ref, o_ref):
        o_ref[0] = a_ref[...].astype(BF16)
        o_ref[1] = b_ref[...].astype(BF16)
        o_ref[2] = c_ref[...].astype(BF16)

    spec = pl.BlockSpec((None, SH_ROW, D), lambda i, j: (layer, 0, 0))
    return pl.pallas_call(
        body, name="cast_w_small",
        grid_spec=pltpu.PrefetchScalarGridSpec(
            num_scalar_prefetch=1, grid=(1,), in_specs=[spec, spec, spec],
            out_specs=pl.BlockSpec((3, SH_ROW, D), lambda i, j: (0, j[0], 0))),
        out_shape=SDS((3, D, D), BF16), compiler_params=_cp("parallel"),
    )(chip_idx, a, b, c)


def _mesh_pos():
    return lax.axis_index("x"), lax.axis_index("y"), lax.axis_index("c")


def _other_chips(x, y):
    return [(1 - x, y), (x, 1 - y), (1 - x, 1 - y)]


class _Rider:
    def __init__(self, ins, out_shape, n, copies, aliases=()):
        self.ins, self.out_shape, self.n, self.copies, self.aliases = list(ins), list(out_shape), n, copies, aliases


def _rcopy(src, dst, send, recv, k, to):
    return pltpu.make_async_remote_copy(src_ref=src, dst_ref=dst, send_sem=send.at[k], recv_sem=recv.at[k],
                                        device_id=to, device_id_type=MESH)


def _hosted_call(body, rider, *, name, grid, in_specs, out_specs, out_shape, args, scratch_shapes=(), vmem=None):
    n_in, n_out, n_scr = len(in_specs), len(out_specs), len(scratch_shapes)
    r_in, r_out = len(rider.ins), len(rider.out_shape)

    def full_body(*refs):
        host_in, rid_in = refs[:n_in], refs[n_in:n_in + r_in]
        o0 = n_in + r_in
        host_out, rid_out = refs[o0:o0 + n_out], refs[o0 + n_out:o0 + n_out + r_out]
        s0 = o0 + n_out + r_out
        host_scr, (send, recv) = refs[s0:s0 + n_scr], refs[s0 + n_scr:]
        if body is None:
            cps = rider.copies(rid_in, rid_out, send, recv)
            for cp in cps:
                cp.start()
            for cp in cps:
                cp.wait()
            return
        ids = [pl.program_id(a) for a in range(len(grid))]
        first = functools.reduce(lambda p, q: p & q, [i == 0 for i in ids])
        last = functools.reduce(lambda p, q: p & q, [i == g - 1 for i, g in zip(ids, grid)])

        @pl.when(first)
        def _():
            for cp in rider.copies(rid_in, rid_out, send, recv):
                cp.start()

        body(*host_in, *host_out, *host_scr)

        @pl.when(last)
        def _():
            for cp in rider.copies(rid_in, rid_out, send, recv):
                cp.wait()

    res = pl.pallas_call(
        full_body, name=name, grid=grid if body is not None else (),
        in_specs=list(in_specs) + [ANY] * r_in, out_specs=list(out_specs) + [ANY] * r_out,
        out_shape=list(out_shape) + rider.out_shape,
        scratch_shapes=list(scratch_shapes) + [pltpu.SemaphoreType.DMA((rider.n,))] * 2,
        input_output_aliases={n_in + i: n_out + o for i, o in rider.aliases},
        compiler_params=_cp(*(("arbitrary",) * len(grid) if body is not None else ()), vmem=vmem),
    )(*args, *rider.ins)
    return res[:n_out], res[n_out:]


def _call(body, rider, **kw):
    if rider is not None:
        return _hosted_call(body, rider, **kw)
    res = pl.pallas_call(
        body, name=kw["name"], grid=kw["grid"], in_specs=list(kw["in_specs"]), out_specs=list(kw["out_specs"]),
        out_shape=list(kw["out_shape"]), scratch_shapes=list(kw.get("scratch_shapes", ())),
        compiler_params=_cp(*(("arbitrary",) * len(kw["grid"])), vmem=kw.get("vmem")),
    )(*kw["args"])
    return res, []


def _gather_rider(full_in, full_sm, stage, parts):
    def copies(ins, outs, send, recv):
        x, y, c = _mesh_pos()
        cps = []
        for a in parts:
            full = outs[parts.index(a)]
            for k, chip in enumerate(_other_chips(x, y)):
                whose = 2 * x + y if stage == "A" else 2 * chip[0] + chip[1]
                if a == 0:
                    reg = full.at[pl.ds(pl.multiple_of(c * 512, 512), 512), pl.ds(pl.multiple_of(whose * SH_IN, 128), SH_IN)]
                else:
                    reg = full.at[:, pl.ds(pl.multiple_of(whose * SH_ROW + c * 128, 128), 128), :]
                to = (*chip, c) if stage == "A" else (x, y, 1 - c)
                cps.append(_rcopy(reg, reg, send, recv, len(cps), to))
        return cps

    arrays = [(full_in, full_sm)[a] for a in parts]
    return _Rider(arrays, [SDS(v.shape, v.dtype) for v in arrays], 3 * len(parts), copies,
                  aliases=tuple((i, i) for i in range(len(parts))))


def _staged_copies(copies, stages, sem_in, sem_out):
    busy, count = {}, {}
    for idx, (src, dst, kind) in enumerate(copies):
        slot = count.get(kind, 0) % 2
        count[kind] = count.get(kind, 0) + 1
        if (kind, slot) in busy:
            busy.pop((kind, slot)).wait()
        buf = stages[kind].at[slot]
        cin = pltpu.make_async_copy(src, buf, sem_in.at[idx])
        cin.start()
        cin.wait()
        cout = pltpu.make_async_copy(buf, dst, sem_out.at[idx])
        cout.start()
        busy[(kind, slot)] = cout
    for cp in busy.values():
        cp.wait()


def _allreduce_small(pack):
    rows = pack.shape[0]

    def body(p_ref, o_ref, buf, send, recv):
        x, y, c = _mesh_pos()
        me = 4 * x + 2 * y + c
        sends = []
        for r in range(1, 8):
            to = (x if not (r & 4) else 1 - x, y if not (r & 2) else 1 - y, c if not (r & 1) else 1 - c)
            cp = pltpu.make_async_remote_copy(src_ref=p_ref, dst_ref=buf.at[me], send_sem=send.at[r - 1],
                                              recv_sem=recv.at[r - 1], device_id=to, device_id_type=MESH)
            cp.start()
            sends.append(cp)
        buf[me] = p_ref[...]
        for r in range(1, 8):
            frm = (4 * x + 2 * y + c) ^ r
            pltpu.make_async_remote_copy(src_ref=p_ref, dst_ref=buf.at[frm], send_sem=send.at[r - 1],
                                         recv_sem=recv.at[r - 1], device_id=(x, y, c), device_id_type=MESH).wait_recv()
        acc = buf[0]
        for d in range(1, 8):
            acc = acc + buf[d]
        o_ref[...] = acc
        for cp in sends:
            cp.wait_send()

    vm = pl.BlockSpec(memory_space=pltpu.VMEM)
    return pl.pallas_call(
        body, name="allreduce_small", in_specs=[vm], out_specs=vm, out_shape=SDS((rows, 128), F32),
        scratch_shapes=[pltpu.VMEM((8, rows, 128), F32), pltpu.SemaphoreType.DMA((7,)), pltpu.SemaphoreType.DMA((7,))],
    )(pack)


def _rmsnorm_fwd(x, g):
    t = x.shape[0]
    tm = min(512, t)

    def body(x_ref, g_ref, h_ref, ht_ref):
        xv = x_ref[...]
        r = lax.rsqrt(jnp.mean(xv * xv, axis=-1, keepdims=True) + EPS)
        h = xv * r * g_ref[...]
        h_ref[...] = h.astype(BF16)
        ht_ref[...] = h.T.astype(BF16)

    return pl.pallas_call(
        body, name="rmsnorm_fwd", grid=(t // tm,),
        in_specs=[pl.BlockSpec((tm, D), lambda i: (i, 0)), pl.BlockSpec((1, D), lambda i: (0, 0))],
        out_specs=[pl.BlockSpec((tm, D), lambda i: (i, 0)), pl.BlockSpec((D, tm), lambda i: (0, i))],
        out_shape=[SDS((t, D), BF16), SDS((D, t), BF16)],
        compiler_params=_cp("parallel", vmem=VMEM_BIG),
    )(x, g)


FWD_SEGS = ((0, 8), (8, 3), (11, 2), (13, 4))


def _in_proj(h, w_full, rider):
    t = h.shape[0]
    tm = min(2048, t)

    def body(a_ref, b_ref, *o_refs):
        j = pl.program_id(1)
        res = _dot(a_ref[...], b_ref[...]).astype(BF16)
        for o_ref, (off, nblk) in zip(o_refs, FWD_SEGS):
            @pl.when((j >= off) & (j < off + nblk))
            def _():
                o_ref[...] = res

    def out(seg):
        off, nblk = seg
        return pl.BlockSpec((tm, CB), lambda i, j: (i, jnp.clip(j - off, 0, nblk - 1)))

    res, got = _call(
        body, rider, name="in_proj", grid=(t // tm, N_IN // CB),
        in_specs=[pl.BlockSpec((tm, D), lambda i, j: (i, 0)), pl.BlockSpec((D, CB), lambda i, j: (0, j))],
        out_specs=[out(s) for s in FWD_SEGS], out_shape=[SDS((t, s[1] * CB), BF16) for s in FWD_SEGS],
        args=(h, w_full), vmem=VMEM_BIG)
    return res, got


def _conv_fwd(u_conv, conv_w):
    t = u_conv.shape[0]
    tm = min(256, t)
    hb = tm // 16

    def body(v_ref, b_ref, c_ref, z_ref, hv_ref, hc_ref, w_ref, y_ref):
        i = pl.program_id(0)
        cv = c_ref[...].astype(F32) * v_ref[...].astype(F32)
        halo = hc_ref[...].astype(F32) * hv_ref[...].astype(F32)
        halo = jnp.where(i > 0, halo, 0.0)
        row = lax.broadcasted_iota(jnp.int32, (tm, 1), 0)
        s1 = jnp.where(row == 0, halo[15:16], pltpu.roll(cv, 1, 0))
        s2 = jnp.where(row == 0, halo[14:15], jnp.where(row == 1, halo[15:16], pltpu.roll(cv, 2, 0)))
        conv = w_ref[0:1, :] * s2 + w_ref[1:2, :] * s1 + w_ref[2:3, :] * cv
        z = z_ref[...].astype(F32)
        y_ref[...] = (b_ref[...].astype(F32) * conv * (z * _sigmoid(z))).astype(BF16)

    def col(k):
        return pl.BlockSpec((tm, D), lambda i: (i, k))

    def halo(k):
        return pl.BlockSpec((16, D), lambda i: (jnp.maximum(i * hb - 1, 0), k))

    return pl.pallas_call(
        body, name="conv_fwd", grid=(t // tm,),
        in_specs=[col(0), col(1), col(2), col(3), halo(0), halo(2), pl.BlockSpec((3, D), lambda i: (0, 0))],
        out_specs=pl.BlockSpec((tm, D), lambda i: (i, 0)), out_shape=SDS((t, D), BF16),
        compiler_params=_cp("parallel", vmem=VMEM_BIG),
    )(u_conv, u_conv, u_conv, u_conv, u_conv, u_conv, conv_w)


KVX = 4 * N_KV * 128


def _iota2(shape):
    return lax.broadcasted_iota(jnp.int32, shape, 0), lax.broadcasted_iota(jnp.int32, shape, 1)


def _head_sum(v):
    r, c = _iota2((128, 128))
    ones = ((r >> 6) == (c >> 6)).astype(BF16)
    hi = v.astype(BF16)
    lo = (v - hi.astype(F32)).astype(BF16)
    return jnp.concatenate([_dot(hi[:, g:g + 128], ones) + _dot(lo[:, g:g + 128], ones)
                            for g in range(0, v.shape[1], 128)], axis=1)


def _expand_mats():
    r, c = _iota2((N_KV * HEAD, N_KV * 128))
    base = ((r >> 6) << 7) + (r & 63)
    return (c == base).astype(BF16), (c == base + 64).astype(BF16)


def _fold_mat():
    r, c = _iota2((N_KV * 128, N_KV * HEAD))
    return (((r >> 7) == (c >> 6)) & ((r & 63) == (c & 63))).astype(BF16)


def _qkv_prep(u_qkv, qg_s, kg_t):
    t = u_qkv.shape[0]
    tm = min(512, t)

    def body(u_ref, qg_ref, kg_ref, qs_ref, kvx_ref):
        q = u_ref[:, 0:D].astype(F32)
        rq = lax.rsqrt(_head_sum(q * q) * (1.0 / HEAD) + EPS)
        qs_ref[...] = (q * rq * qg_ref[...]).astype(BF16)
        k = u_ref[:, D:D + 256].astype(F32)
        rk = lax.rsqrt(_head_sum(k * k) * (1.0 / HEAD) + EPS)
        kn = (k * rk * kg_ref[...]).astype(BF16)
        v = u_ref[:, D + 256:D + 512]
        e_lo, e_hi = _expand_mats()
        kvx_ref[:, 0:512] = _dot(kn, e_lo).astype(BF16)
        kvx_ref[:, 512:1024] = _dot(kn, e_hi).astype(BF16)
        kvx_ref[:, 1024:1536] = _dot(v, e_lo).astype(BF16)
        kvx_ref[:, 1536:2048] = _dot(v, e_hi).astype(BF16)

    return pl.pallas_call(
        body, name="qkv_prep", grid=(t // tm,),
        in_specs=[pl.BlockSpec((tm, 1536), lambda i: (i, 0)), pl.BlockSpec((1, D), lambda i: (0, 0)),
                  pl.BlockSpec((1, 256), lambda i: (0, 0))],
        out_specs=[pl.BlockSpec((tm, D), lambda i: (i, 0)), pl.BlockSpec((tm, KVX), lambda i: (i, 0))],
        out_shape=[SDS((t, D), BF16), SDS((t, KVX), BF16)], compiler_params=_cp("parallel", vmem=VMEM_BIG),
    )(u_qkv, qg_s, kg_t)


def _band_bias():
    j, r = _iota2((2 * BLK, 2 * BLK))
    diff = (r & (BLK - 1)) - j + BLK
    band = (diff >= 0) & (diff < BLK)
    return jnp.stack([jnp.where(band & (j >= BLK), 0.0, NEG), jnp.where(band, 0.0, NEG)]).astype(F32)


def _pair_rows(ref_or_val, hk):
    return jnp.concatenate([ref_or_val[:, 256 * hk:256 * hk + 128], ref_or_val[:, 256 * hk + 128:256 * hk + 256]], axis=0)


def _sink_row(sink_ref, hk, half):
    return jnp.concatenate([jnp.full((1, BLK), sink_ref[0, GROUP * hk + half], F32),
                            jnp.full((1, BLK), sink_ref[0, GROUP * hk + 2 + half], F32)], axis=1)


def _softmax_half(qpp, kvb, hk, half, bias, sink_ref):
    kx = kvb[:, 512 * half + 128 * hk:512 * half + 128 * hk + 128]
    vx = kvb[:, 1024 + 512 * half + 128 * hk:1024 + 512 * half + 128 * hk + 128]
    s = _dot_nt(kx, qpp) + bias
    sink = _sink_row(sink_ref, hk, half)
    m = jnp.maximum(jnp.max(s, axis=0, keepdims=True), sink)
    p = jnp.exp(s - m)
    es = jnp.exp(sink - m)
    inv = 1.0 / (jnp.sum(p, axis=0, keepdims=True) + es)
    return p * inv, es * inv, kx, vx


def _attn_fwd(qs, kvx, u_za, sinks, bias, rider):
    t = qs.shape[0]
    nb = t // BLK

    def body(q_ref, kc_ref, kp_ref, za_ref, sink_ref, bias_ref, o_ref):
        kvb = jnp.concatenate([kp_ref[...], kc_ref[...]], axis=0)
        bias_v = bias_ref[...]
        cols = []
        for hk in range(N_KV):
            qpp = _pair_rows(q_ref, hk)
            lo = _softmax_half(qpp, kvb, hk, 0, bias_v, sink_ref)
            hi = _softmax_half(qpp, kvb, hk, 1, bias_v, sink_ref)
            opp = _dot_tn(jnp.concatenate([lo[0], hi[0]], axis=0).astype(BF16), jnp.concatenate([lo[3], hi[3]], axis=0))
            cols += [opp[:BLK], opp[BLK:]]
        za = za_ref[...].astype(F32)
        o_ref[...] = (jnp.concatenate(cols, axis=1) * (za * _sigmoid(za))).astype(BF16)

    prev = lambda n: jnp.maximum(n - 1, 0)
    (o,), got = _call(
        body, rider, name="attn_fwd", grid=(nb,),
        in_specs=[pl.BlockSpec((BLK, D), lambda n: (n, 0)),
                  pl.BlockSpec((BLK, KVX), lambda n: (n, 0)), pl.BlockSpec((BLK, KVX), lambda n: (prev(n), 0)),
                  pl.BlockSpec((BLK, D), lambda n: (n, 0)), pl.BlockSpec(memory_space=pltpu.SMEM),
                  pl.BlockSpec((None, 2 * BLK, 2 * BLK), lambda n: (jnp.minimum(n, 1), 0, 0))],
        out_specs=[pl.BlockSpec((BLK, D), lambda n: (n, 0))], out_shape=[SDS((t, D), BF16)],
        args=(qs, kvx, kvx, u_za, sinks, bias), vmem=VMEM_BIG)
    return o, got


def _out_proj_fwd(x, y_c, o, u_gl, gate_b, w_sm, rider):
    t = x.shape[0]
    tm = min(512, t)

    def body(x_ref, yc_ref, o_ref, gla_ref, glb_ref, gb_ref, wco_ref, wao_ref, wout_ref,
             xn_ref, ya_ref, yb_ref, mg_ref):
        ya = _dot(yc_ref[...], wco_ref[...])
        yb = _dot(o_ref[...], wao_ref[...])
        gb = gb_ref[...]
        ga_ = _sigmoid(gla_ref[...].astype(F32) + gb[:, :D])
        gb_ = _sigmoid(glb_ref[...].astype(F32) + gb[:, D:])
        merged = (ga_ * ya + gb_ * yb).astype(BF16)
        ya_ref[...] = ya.astype(BF16)
        yb_ref[...] = yb.astype(BF16)
        mg_ref[...] = merged
        xn_ref[...] = x_ref[...] + _dot(merged, wout_ref[...])

    row = pl.BlockSpec((tm, D), lambda i: (i, 0))
    wspec = lambda a: pl.BlockSpec((None, D, D), lambda i: (a, 0, 0))
    return _call(
        body, rider, name="out_proj_fwd", grid=(t // tm,),
        in_specs=[row, row, row, pl.BlockSpec((tm, D), lambda i: (i, 0)), pl.BlockSpec((tm, D), lambda i: (i, 1)),
                  pl.BlockSpec((1, 2 * D), lambda i: (0, 0)), wspec(0), wspec(1), wspec(2)],
        out_specs=[row, row, row, row],
        out_shape=[SDS((t, D), F32), SDS((t, D), BF16), SDS((t, D), BF16), SDS((t, D), BF16)],
        args=(x, y_c, o, u_gl, u_gl, gate_b, w_sm, w_sm, w_sm), vmem=VMEM_BIG)


def _loss_head(y, tgt):
    t = y.shape[0]
    tm = min(512, t)

    def body(y_ref, t_ref, dy_ref, acc_ref):
        @pl.when(pl.program_id(0) == 0)
        def _():
            acc_ref[...] = jnp.zeros_like(acc_ref)
        err = y_ref[...] - t_ref[...]
        dy_ref[...] = err * (1.0 / D)
        sq = _fold8(err * err)
        tot = sq[:, 0:128]
        for k in range(1, D // 128):
            tot = tot + sq[:, 128 * k:128 * (k + 1)]
        acc_ref[...] += tot

    row = pl.BlockSpec((tm, D), lambda i: (i, 0))
    return pl.pallas_call(
        body, name="loss_head", grid=(t // tm,), in_specs=[row, row],
        out_specs=[row, pl.BlockSpec((8, 128), lambda i: (0, 0))],
        out_shape=[SDS((t, D), F32), SDS((8, 128), F32)], compiler_params=_cp("arbitrary"),
    )(y, tgt)


def _out_proj_bwd(dout, y_a, y_b, u_gl, gate_b, w_sm, rider):
    t = dout.shape[0]
    tm = min(512, t)

    def body(do_ref, ya_ref, yb_ref, gla_ref, glb_ref, gb_ref, wco_ref, wao_ref, wout_ref,
             dya_ref, dyb_ref, dgl_ref, dyc_ref, dob_ref, dgb_ref):
        @pl.when(pl.program_id(0) == 0)
        def _():
            dgb_ref[...] = jnp.zeros_like(dgb_ref)
        dm = _dot_nt(do_ref[...].astype(BF16), wout_ref[...])
        gb = gb_ref[...]
        ga_ = _sigmoid(gla_ref[...].astype(F32) + gb[:, :D])
        gb_ = _sigmoid(glb_ref[...].astype(F32) + gb[:, D:])
        dya = (ga_ * dm).astype(BF16)
        dyb = (gb_ * dm).astype(BF16)
        dgla = ya_ref[...].astype(F32) * dm * (ga_ * (1.0 - ga_))
        dglb = yb_ref[...].astype(F32) * dm * (gb_ * (1.0 - gb_))
        dya_ref[...] = dya
        dyb_ref[...] = dyb
        dgl_ref[:, :D] = dgla.astype(BF16)
        dgl_ref[:, D:] = dglb.astype(BF16)
        dgb_ref[:, :D] += _fold8(dgla)
        dgb_ref[:, D:] += _fold8(dglb)
        dyc_ref[...] = _dot_nt(dya, wco_ref[...]).astype(BF16)
        dob_ref[...] = _dot_nt(dyb, wao_ref[...]).astype(BF16)

    row = pl.BlockSpec((tm, D), lambda i: (i, 0))
    wspec = lambda a: pl.BlockSpec((None, D, D), lambda i: (a, 0, 0))
    return _call(
        body, rider, name="out_proj_bwd", grid=(t // tm,),
        in_specs=[row, row, row, pl.BlockSpec((tm, D), lambda i: (i, 0)), pl.BlockSpec((tm, D), lambda i: (i, 1)),
                  pl.BlockSpec((1, 2 * D), lambda i: (0, 0)), wspec(0), wspec(1), wspec(2)],
        out_specs=[row, row, pl.BlockSpec((tm, 2 * D), lambda i: (i, 0)), row, row,
                   pl.BlockSpec((8, 2 * D), lambda i: (0, 0))],
        out_shape=[SDS((t, D), BF16), SDS((t, D), BF16), SDS((t, 2 * D), BF16), SDS((t, D), BF16), SDS((t, D), BF16),
                   SDS((8, 2 * D), F32)],
        args=(dout, y_a, y_b, u_gl, u_gl, gate_b, w_sm, w_sm, w_sm), vmem=VMEM_BIG)


def _small_wgrads(y_c, d_ya, o, d_yb, merged, dout):
    t = y_c.shape[0]
    tk = min(512, t)

    def body(yc_ref, dya_ref, o_ref, dyb_ref, mg_ref, do_ref, g_ref):
        @pl.when(pl.program_id(0) == 0)
        def _():
            g_ref[...] = jnp.zeros_like(g_ref)
        g_ref[0] += _dot_tn(yc_ref[...], dya_ref[...])
        g_ref[1] += _dot_tn(o_ref[...], dyb_ref[...])
        g_ref[2] += _dot_tn(mg_ref[...], do_ref[...].astype(BF16))

    row = pl.BlockSpec((tk, D), lambda k: (k, 0))
    return pl.pallas_call(
        body, name="small_wgrads", grid=(t // tk,), in_specs=[row] * 6,
        out_specs=pl.BlockSpec((3, D, D), lambda k: (0, 0, 0)), out_shape=SDS((3, D, D), F32),
        compiler_params=_cp("arbitrary", vmem=VMEM_BIG),
    )(y_c, d_ya, o, d_yb, merged, dout)


def _conv_bwd(d_yc, u_conv, conv_w):
    t = d_yc.shape[0]
    tm = min(256, t)
    hb = tm // 16
    last_halo = t // 16 - 1
    n_steps = t // tm

    def body(dy_ref, v_ref, b_ref, c_ref, z_ref, hv_ref, hc_ref, ndy_ref, nb_ref, nz_ref, w_ref, du_ref, dw_ref):
        i = pl.program_id(0)

        @pl.when(i == 0)
        def _():
            dw_ref[...] = jnp.zeros_like(dw_ref)
        v, c = v_ref[...].astype(F32), c_ref[...].astype(F32)
        b, z = b_ref[...].astype(F32), z_ref[...].astype(F32)
        cv = c * v
        halo = jnp.where(i > 0, hc_ref[...].astype(F32) * hv_ref[...].astype(F32), 0.0)
        row = lax.broadcasted_iota(jnp.int32, (tm, 1), 0)
        s1 = jnp.where(row == 0, halo[15:16], pltpu.roll(cv, 1, 0))
        s2 = jnp.where(row == 0, halo[14:15], jnp.where(row == 1, halo[15:16], pltpu.roll(cv, 2, 0)))
        w0, w1, w2 = w_ref[0:1, :], w_ref[1:2, :], w_ref[2:3, :]
        conv = w0 * s2 + w1 * s1 + w2 * cv
        sig = _sigmoid(z)
        sz = z * sig
        dsz = sig * (1.0 + z * (1.0 - sig))
        dy = dy_ref[...].astype(F32)
        dconv = dy * b * sz
        nz = nz_ref[...].astype(F32)
        nxt = ndy_ref[...].astype(F32) * nb_ref[...].astype(F32) * (nz * _sigmoid(nz))
        nxt = jnp.where(i < n_steps - 1, nxt, 0.0)
        a1 = jnp.where(row == tm - 1, nxt[0:1], pltpu.roll(dconv, tm - 1, 0))
        a2 = jnp.where(row == tm - 2, nxt[0:1], jnp.where(row == tm - 1, nxt[1:2], pltpu.roll(dconv, tm - 2, 0)))
        dcv = w2 * dconv + w1 * a1 + w0 * a2
        du_ref[:, 0:D] = (dcv * c).astype(BF16)
        du_ref[:, D:2 * D] = (dy * conv * sz).astype(BF16)
        du_ref[:, 2 * D:3 * D] = (dcv * v).astype(BF16)
        du_ref[:, 3 * D:4 * D] = (dy * b * conv * dsz).astype(BF16)
        r8 = lax.broadcasted_iota(jnp.int32, (8, 1), 0)
        dw_ref[...] += jnp.where(r8 == 0, jnp.sum(dconv * s2, axis=0, keepdims=True),
                                 jnp.where(r8 == 1, jnp.sum(dconv * s1, axis=0, keepdims=True),
                                           jnp.where(r8 == 2, jnp.sum(dconv * cv, axis=0, keepdims=True), 0.0)))

    def col(k):
        return pl.BlockSpec((tm, D), lambda i: (i, k))

    def halo(k):
        return pl.BlockSpec((16, D), lambda i: (jnp.maximum(i * hb - 1, 0), k))

    def nxt(k):
        return pl.BlockSpec((16, D), lambda i: (jnp.minimum((i + 1) * hb, last_halo), k))

    return pl.pallas_call(
        body, name="conv_bwd", grid=(t // tm,),
        in_specs=[col(0), col(0), col(1), col(2), col(3), halo(0), halo(2), nxt(0), nxt(1), nxt(3),
                  pl.BlockSpec((3, D), lambda i: (0, 0))],
        out_specs=[pl.BlockSpec((tm, 4 * D), lambda i: (i, 0)), pl.BlockSpec((8, D), lambda i: (0, 0))],
        out_shape=[SDS((t, 4 * D), BF16), SDS((8, D), F32)],
        compiler_params=_cp("arbitrary", vmem=VMEM_BIG),
    )(d_yc, u_conv, u_conv, u_conv, u_conv, u_conv, u_conv, d_yc, u_conv, u_conv, conv_w)


def _attn_bwd(d_o, qs, kvx, u_za, sinks, bias, rider):
    t = d_o.shape[0]
    nb = t // BLK

    def body(q_ref, kc_ref, kp_ref, za_ref, do_ref, sink_ref, bias_ref, dq_ref, dkv_ref, dza_ref, dsk_ref, carry_ref):
        n = pl.program_id(0)

        @pl.when(n == 0)
        def _():
            carry_ref[...] = jnp.zeros_like(carry_ref)
            dsk_ref[...] = jnp.zeros_like(dsk_ref)

        live = n < nb
        kvb = jnp.concatenate([kp_ref[...], kc_ref[...]], axis=0)
        bias_v = bias_ref[...]
        za = za_ref[...].astype(F32)
        sig = _sigmoid(za)
        dsa = sig * (1.0 + za * (1.0 - sig))
        do = jnp.where(live, do_ref[...].astype(F32), 0.0)
        dattn = (do * (za * sig)).astype(BF16)
        lo_lanes = lax.broadcasted_iota(jnp.int32, (1, 128), 1) < HEAD
        dq_cols, attn_cols, dk_cols, dv_cols, dsk_rows = [], [], [], [], []
        for hk in range(N_KV):
            qpp = _pair_rows(q_ref, hk)
            dapp = _pair_rows(dattn, hk)
            probs, dss, xk, xv = [], [], [], []
            for half in range(2):
                prob, psink, kx, vx = _softmax_half(qpp, kvb, hk, half, bias_v, sink_ref)
                dp = _dot_nt(vx, dapp)
                drow = jnp.sum(prob * dp, axis=0, keepdims=True)
                ds = (prob * (dp - drow)).astype(BF16)
                prob_b = prob.astype(BF16)
                xk.append(_dot(ds, qpp))
                xv.append(_dot(prob_b, dapp))
                probs.append(prob_b)
                dss.append(ds)
                dsk_rows.append(-psink * drow)
            kcat = jnp.concatenate([kvb[:, 128 * hk:128 * hk + 128], kvb[:, 512 + 128 * hk:512 + 128 * hk + 128]], axis=0)
            vcat = jnp.concatenate([kvb[:, 1024 + 128 * hk:1024 + 128 * hk + 128],
                                    kvb[:, 1536 + 128 * hk:1536 + 128 * hk + 128]], axis=0)
            app = _dot_tn(jnp.concatenate(probs, axis=0), vcat)
            dqpp = _dot_tn(jnp.concatenate(dss, axis=0), kcat)
            dq_cols += [dqpp[:BLK], dqpp[BLK:]]
            attn_cols += [app[:BLK], app[BLK:]]
            dk_cols.append(jnp.where(lo_lanes, xk[0], xk[1]))
            dv_cols.append(jnp.where(lo_lanes, xv[0], xv[1]))

        @pl.when(live)
        def _():
            dq_ref[...] = jnp.concatenate(dq_cols, axis=1).astype(BF16)
            dza_ref[...] = (do * jnp.concatenate(attn_cols, axis=1) * dsa).astype(BF16)

        band = jnp.concatenate(dk_cols + dv_cols, axis=1)
        dkv_ref[...] = (band[:BLK] + carry_ref[...]).astype(BF16)
        carry_ref[...] = band[BLK:]
        dsk_ref[...] += jnp.broadcast_to(jnp.concatenate(dsk_rows, axis=1), (8, 2 * N_KV * 2 * BLK))

    cur = lambda n: jnp.minimum(n, nb - 1)
    prev = lambda n: jnp.maximum(n - 1, 0)
    return _call(
        body, rider, name="attn_bwd", grid=(nb + 1,),
        in_specs=[pl.BlockSpec((BLK, D), lambda n: (cur(n), 0)),
                  pl.BlockSpec((BLK, KVX), lambda n: (cur(n), 0)), pl.BlockSpec((BLK, KVX), lambda n: (prev(n), 0)),
                  pl.BlockSpec((BLK, D), lambda n: (cur(n), 0)), pl.BlockSpec((BLK, D), lambda n: (cur(n), 0)),
                  pl.BlockSpec(memory_space=pltpu.SMEM),
                  pl.BlockSpec((None, 2 * BLK, 2 * BLK), lambda n: (jnp.minimum(n, 1), 0, 0))],
        out_specs=[pl.BlockSpec((BLK, D), lambda n: (cur(n), 0)), pl.BlockSpec((BLK, D), lambda n: (prev(n), 0)),
                   pl.BlockSpec((BLK, D), lambda n: (cur(n), 0)), pl.BlockSpec((8, 2 * D), lambda n: (0, 0))],
        out_shape=[SDS((t, D), BF16), SDS((t, D), BF16), SDS((t, D), BF16), SDS((8, 2 * D), F32)],
        scratch_shapes=[pltpu.VMEM((BLK, D), F32)],
        args=(qs, kvx, kvx, u_za, d_o, sinks, bias), vmem=VMEM_BIG)


def _qkv_post(u_qkv, dqs, dkv, dza, qg_s, kg_t):
    t = u_qkv.shape[0]
    tm = min(512, t)

    def norm_bwd(x, dy, g):
        r = lax.rsqrt(_head_sum(x * x) * (1.0 / HEAD) + EPS)
        xhat = x * r
        dxh = dy * g
        return r * (dxh - xhat * (_head_sum(dxh * xhat) * (1.0 / HEAD))), _fold8(dy * xhat)

    def body(u_ref, dq_ref, dkv_ref, dza_ref, qg_ref, kg_ref, du_ref, dqg_ref, dkg_ref):
        @pl.when(pl.program_id(0) == 0)
        def _():
            dqg_ref[...] = jnp.zeros_like(dqg_ref)
            dkg_ref[...] = jnp.zeros_like(dkg_ref)
        dq, gq = norm_bwd(u_ref[:, 0:D].astype(F32), dq_ref[...].astype(F32), qg_ref[...])
        fold = _fold_mat()
        dk, gk = norm_bwd(u_ref[:, D:D + 256].astype(F32), _dot(dkv_ref[:, 0:512], fold), kg_ref[...])
        du_ref[:, 0:D] = dq.astype(BF16)
        du_ref[:, D:2 * D] = dza_ref[...]
        du_ref[:, 2 * D:2 * D + 256] = dk.astype(BF16)
        du_ref[:, 2 * D + 256:2 * D + 512] = _dot(dkv_ref[:, 512:1024], fold).astype(BF16)
        dqg_ref[...] += gq
        dkg_ref[...] += gk

    row = pl.BlockSpec((tm, D), lambda i: (i, 0))
    return pl.pallas_call(
        body, name="qkv_post", grid=(t // tm,),
        in_specs=[pl.BlockSpec((tm, 1536), lambda i: (i, 0)), row, row, row,
                  pl.BlockSpec((1, D), lambda i: (0, 0)), pl.BlockSpec((1, 256), lambda i: (0, 0))],
        out_specs=[pl.BlockSpec((tm, 2560), lambda i: (i, 0)), pl.BlockSpec((8, D), lambda i: (0, 0)),
                   pl.BlockSpec((8, 256), lambda i: (0, 0))],
        out_shape=[SDS((t, 2560), BF16), SDS((8, D), F32), SDS((8, 256), F32)],
        compiler_params=_cp("arbitrary", vmem=VMEM_BIG),
    )(u_qkv, dqs, dkv, dza, qg_s, kg_t)


N_GRAN = N_IN // CB


def _du_granule(j):
    attn = jnp.where(j == 10, 4, jnp.where(j >= 11, j - 9, j - 8))
    return jnp.clip(j, 0, 7), jnp.clip(attn, 0, 4), jnp.clip(j - 13, 0, 3)


def _du_select(j, refs, fn):
    for ref, lo, hi in zip(refs, (0, 8, 13), (8, 13, 17)):
        @pl.when((j >= lo) & (j < hi))
        def _():
            fn(ref)


def _in_proj_bwd(du, w_full, rider):
    t = du[0].shape[0]
    tm = min(1024, t)

    def body(a0, a1, a2, b_ref, dh_ref, acc_ref):
        k = pl.program_id(1)

        @pl.when(k == 0)
        def _():
            acc_ref[...] = jnp.zeros_like(acc_ref)

        def add(a_ref):
            acc_ref[...] += _dot_nt(a_ref[...], b_ref[...])
        _du_select(k, (a0, a1, a2), add)

        @pl.when(k == N_GRAN - 1)
        def _():
            dh_ref[...] = acc_ref[...]

    seg = lambda q: pl.BlockSpec((tm, CB), lambda i, k: (i, _du_granule(k)[q]))
    (dh,), got = _call(
        body, rider, name="in_proj_bwd", grid=(t // tm, N_GRAN),
        in_specs=[seg(0), seg(1), seg(2), pl.BlockSpec((D, CB), lambda i, k: (0, k))],
        out_specs=[pl.BlockSpec((tm, D), lambda i, k: (i, 0))], out_shape=[SDS((t, D), F32)],
        scratch_shapes=[pltpu.VMEM((tm, D), F32)], args=(*du, w_full), vmem=VMEM_BIG)
    return dh, got


def _rmsnorm_bwd(dh, x, g, dout):
    t = x.shape[0]
    tm = min(256, t)

    def body(dh_ref, x_ref, g_ref, do_ref, dx_ref, dg_ref):
        @pl.when(pl.program_id(0) == 0)
        def _():
            dg_ref[...] = jnp.zeros_like(dg_ref)
        dh = dh_ref[...]
        xv = x_ref[...]
        r = lax.rsqrt(jnp.mean(xv * xv, axis=-1, keepdims=True) + EPS)
        xhat = xv * r
        dg_ref[...] += _fold8(dh * xhat)
        dxh = dh * g_ref[...]
        dx_ref[...] = do_ref[...] + r * (dxh - xhat * jnp.mean(dxh * xhat, axis=-1, keepdims=True))

    row = pl.BlockSpec((tm, D), lambda i: (i, 0))
    return pl.pallas_call(
        body, name="rmsnorm_bwd", grid=(t // tm,),
        in_specs=[row, row, pl.BlockSpec((1, D), lambda i: (0, 0)), row],
        out_specs=[row, pl.BlockSpec((8, D), lambda i: (0, 0))],
        out_shape=[SDS((t, D), F32), SDS((8, D), F32)], compiler_params=_cp("arbitrary"),
    )(dh, x, g, dout)


def _in_proj_wgrad(ht, du):
    t = ht.shape[1]
    tk = min(2048, t)

    def body(h_ref, b0, b1, b2, g_ref):
        j, k = pl.program_id(0), pl.program_id(1)

        @pl.when(k == 0)
        def _():
            g_ref[...] = jnp.zeros_like(g_ref)

        def add(b_ref):
            g_ref[...] += _dot(h_ref[...], b_ref[...])
        _du_select(j, (b0, b1, b2), add)

    seg = lambda q: pl.BlockSpec((tk, CB), lambda j, k: (k, _du_granule(j)[q]))
    return pl.pallas_call(
        body, name="in_proj_wgrad", grid=(N_GRAN, t // tk),
        in_specs=[pl.BlockSpec((D, tk), lambda j, k: (0, k)), seg(0), seg(1), seg(2)],
        out_specs=pl.BlockSpec((D, CB), lambda j, k: (0, j)), out_shape=SDS((D, N_IN), F32),
        compiler_params=_cp("parallel", "arbitrary", vmem=VMEM_BIG),
    )(ht, *du)


def _swap_rider(g_in, g_sm):
    def copies(ins, outs, send, recv):
        x, y, c = _mesh_pos()
        return [_rcopy(ins[0].at[1 - c], outs[0], send, recv, 0, (x, y, 1 - c)),
                _rcopy(ins[1].at[:, :, 1 - c], outs[1], send, recv, 1, (x, y, 1 - c))]

    return _Rider([g_in, g_sm], [SDS((512, N_IN), F32), SDS((3, 4, 128, D), F32)], 2, copies)


def _add_halves_in(c_idx, g_in, r_in):
    def body(c_ref, a_ref, b_ref, f_ref, h_ref):
        s = a_ref[...] + b_ref[...]
        f_ref[...] = s
        h_ref[...] = s.astype(BF16)

    blk = pl.BlockSpec((128, N_IN), lambda i, c: (i, 0))
    return pl.pallas_call(
        body, name="add_halves_in",
        grid_spec=pltpu.PrefetchScalarGridSpec(
            num_scalar_prefetch=1, grid=(4,),
            in_specs=[pl.BlockSpec((None, 128, N_IN), lambda i, c: (c[0], i, 0)), blk], out_specs=[blk, blk]),
        out_shape=[SDS((512, N_IN), F32), SDS((512, N_IN), BF16)], compiler_params=_cp("parallel", vmem=VMEM_BIG),
    )(c_idx, g_in, r_in)


def _add_halves_sm(c_idx, g_sm, r_sm):
    def body(c_ref, a_ref, b_ref, f_ref, h_ref):
        s = a_ref[...] + b_ref[...]
        f_ref[...] = s
        h_ref[...] = s.astype(BF16)

    blk = pl.BlockSpec((1, 4, 128, D), lambda a, c: (a, 0, 0, 0))
    return pl.pallas_call(
        body, name="add_halves_sm",
        grid_spec=pltpu.PrefetchScalarGridSpec(
            num_scalar_prefetch=1, grid=(3,),
            in_specs=[pl.BlockSpec((1, 4, None, 128, D), lambda a, c: (a, 0, c[0], 0, 0)), blk], out_specs=[blk, blk]),
        out_shape=[SDS((3, 4, 128, D), F32), SDS((3, 4, 128, D), BF16)], compiler_params=_cp("parallel"),
    )(c_idx, g_sm, r_sm)


def _scatter_rider(h_in, h_sm):
    def copies(ins, outs, send, recv):
        x, y, c = _mesh_pos()
        cps = []
        for k, chip in enumerate(_other_chips(x, y)):
            their = 2 * chip[0] + chip[1]
            cps.append(_rcopy(ins[0].at[:, pl.ds(pl.multiple_of(their * SH_IN, 128), SH_IN)], outs[0].at[k],
                              send, recv, 2 * k, (*chip, c)))
            cps.append(_rcopy(ins[1].at[:, their], outs[1].at[k], send, recv, 2 * k + 1, (*chip, c)))
        return cps

    return _Rider([h_in, h_sm], [SDS((3, 512, SH_IN), BF16), SDS((3, 3, 128, D), BF16)], 6, copies)


def _ride_alone(rider, name):
    return _hosted_call(None, rider, name=name, grid=(), in_specs=[], out_specs=[], out_shape=[], args=())[1]


def _final_sum_in(chip_idx, f_in, r_in):
    def body(j_ref, a_ref, r_ref, o_ref):
        o_ref[...] = a_ref[...] + r_ref[0].astype(F32) + r_ref[1].astype(F32) + r_ref[2].astype(F32)

    return pl.pallas_call(
        body, name="final_sum_in",
        grid_spec=pltpu.PrefetchScalarGridSpec(
            num_scalar_prefetch=1, grid=(4,),
            in_specs=[pl.BlockSpec((128, SH_IN), lambda i, j: (i, j[0])), pl.BlockSpec((3, 128, SH_IN), lambda i, j: (0, i, 0))],
            out_specs=pl.BlockSpec((128, SH_IN), lambda i, j: (i, 0))),
        out_shape=SDS((512, SH_IN), F32), compiler_params=_cp("parallel"),
    )(chip_idx, f_in, r_in)


def _final_sum_sm(chip_idx, f_sm, r_sm):
    def body(j_ref, a_ref, r_ref, o_ref):
        o_ref[...] = a_ref[...] + r_ref[0].astype(F32) + r_ref[1].astype(F32) + r_ref[2].astype(F32)

    return pl.pallas_call(
        body, name="final_sum_sm",
        grid_spec=pltpu.PrefetchScalarGridSpec(
            num_scalar_prefetch=1, grid=(3,),
            in_specs=[pl.BlockSpec((1, None, 128, D), lambda a, j: (a, j[0], 0, 0)),
                      pl.BlockSpec((3, 1, 128, D), lambda a, j: (0, a, 0, 0))],
            out_specs=pl.BlockSpec((1, 128, D), lambda a, j: (a, 0, 0))),
        out_shape=SDS((3, 128, D), F32), compiler_params=_cp("parallel"),
    )(chip_idx, f_sm, r_sm)


def _join_halves(t_in, t_sm):
    n_cp = N_LAYERS * 4

    def body(*refs):
        ins, outs = refs[:2 * N_LAYERS], refs[2 * N_LAYERS:2 * N_LAYERS + 4]
        send, recv, loc_in, loc_out, stage_in, stage_sm = refs[2 * N_LAYERS + 4:]
        x, y, c = _mesh_pos()
        cps, own = [], []
        for l in range(N_LAYERS):
            for a in range(4):
                s = 4 * l + a
                if a == 0:
                    src = ins[2 * l]
                    dst = outs[0].at[l, pl.ds(pl.multiple_of(c * 512, 512), 512), :]
                else:
                    src = ins[2 * l + 1].at[a - 1]
                    dst = outs[a].at[l, pl.ds(pl.multiple_of(c * 128, 128), 128), :]
                own.append((src, dst, min(a, 1)))
                cp = pltpu.make_async_remote_copy(src_ref=src, dst_ref=dst, send_sem=send.at[s], recv_sem=recv.at[s],
                                                  device_id=(x, y, 1 - c), device_id_type=MESH)
                cp.start()
                cps.append(cp)
        _staged_copies(own, (stage_in, stage_sm), loc_in, loc_out)
        for l in range(N_LAYERS):
            for a in range(4):
                s = 4 * l + a
                if a == 0:
                    got = outs[0].at[l, pl.ds(pl.multiple_of((1 - c) * 512, 512), 512), :]
                else:
                    got = outs[a].at[l, pl.ds(pl.multiple_of((1 - c) * 128, 128), 128), :]
                pltpu.make_async_remote_copy(src_ref=got, dst_ref=got, send_sem=send.at[s], recv_sem=recv.at[s],
                                             device_id=(x, y, 1 - c), device_id_type=MESH).wait_recv()
        for cp in cps:
            cp.wait_send()

    args = []
    for l in range(N_LAYERS):
        args += [t_in[l], t_sm[l]]
    sm = SDS((N_LAYERS, SH_ROW, D), F32)
    return pl.pallas_call(
        body, name="join_halves", in_specs=[ANY] * (2 * N_LAYERS), out_specs=[ANY] * 4,
        out_shape=[SDS((N_LAYERS, D, SH_IN), F32), sm, sm, sm],
        scratch_shapes=[pltpu.SemaphoreType.DMA((n_cp,))] * 4
        + [pltpu.VMEM((2, 512, SH_IN), F32), pltpu.VMEM((2, 128, D), F32)],
        compiler_params=_cp(vmem=VMEM_BIG),
    )(*args)


def _adam_math(w, g, m, v):
    m = ADAM_B1 * m + (1.0 - ADAM_B1) * g
    v = ADAM_B2 * v + (1.0 - ADAM_B2) * (g * g)
    m_hat = m / (1.0 - ADAM_B1 ** ADAM_STEP)
    v_hat = v / (1.0 - ADAM_B2 ** ADAM_STEP)
    delta = -ADAM_LR * (m_hat / (jnp.sqrt(v_hat) + ADAM_EPS) + ADAM_WD * w)
    return delta, m, v


def _adamw_big(w, g, m, v, name):
    rows, cols = w.shape
    tr = 128

    def body(w_ref, g_ref, m_ref, v_ref, d_ref, nm_ref, nv_ref):
        d_ref[...], nm_ref[...], nv_ref[...] = _adam_math(w_ref[...], g_ref[...], m_ref[...], v_ref[...])

    blk = pl.BlockSpec((tr, cols), lambda i: (i, 0))
    return pl.pallas_call(
        body, name=name, grid=(rows // tr,), in_specs=[blk] * 4, out_specs=[blk] * 3,
        out_shape=[SDS((rows, cols), F32)] * 3, compiler_params=_cp("parallel", vmem=VMEM_BIG),
    )(w, g, m, v)


def _adamw_small(ws, gs, ms, vs):
    n = len(ws)

    def body(*refs):
        for k in range(n):
            w_ref, g_ref, m_ref, v_ref = (refs[q * n + k] for q in range(4))
            d, nm, nv = _adam_math(w_ref[...], g_ref[...], m_ref[...], v_ref[...])
            refs[4 * n + k][...] = d
            refs[5 * n + k][...] = nm
            refs[6 * n + k][...] = nv

    vm = pl.BlockSpec(memory_space=pltpu.VMEM)
    shapes = [SDS(w.shape, F32) for w in ws]
    res = pl.pallas_call(
        body, name="adamw_small", in_specs=[vm] * (4 * n), out_specs=[vm] * (3 * n), out_shape=shapes * 3,
    )(*ws, *gs, *ms, *vs)
    return res[:n], res[n:2 * n], res[2 * n:]


def _pad_rows(a, rows):
    flat = a.reshape(-1)
    return jnp.pad(flat, (0, rows * 128 - flat.shape[0])).reshape(rows, 128)


def kernel(x, norm_g, w_in, conv_w, q_norm_g, k_norm_g, sinks, w_conv_out, w_attn_out, gate_b, w_out, loss_target, m_norm_g, m_w_in, m_conv_w, m_q_norm_g, m_k_norm_g, m_sinks, m_w_conv_out, m_w_attn_out, m_gate_b, m_w_out, v_norm_g, v_w_in, v_conv_w, v_q_norm_g, v_k_norm_g, v_sinks, v_w_conv_out, v_w_attn_out, v_gate_b, v_w_out):
    xi, yi, ci = _mesh_pos()
    chip = 2 * xi + yi
    c_idx = jnp.reshape(ci, (1,)).astype(jnp.int32)
    chip_idx = jnp.reshape(chip, (1,)).astype(jnp.int32)
    t = x.shape[1]
    xs = [x.reshape(t, D)]
    tgt = loss_target.reshape(t, D)

    full_w = [[_cast_w_in(chip_idx, w_in, l), _cast_w_small(chip_idx, w_conv_out, w_attn_out, w_out, l)]
              for l in range(N_LAYERS)]
    full_w[0] = _ride_alone(_gather_rider(*full_w[0], "A", (0, 1)), "gather_first_ici")
    full_w[0] = _ride_alone(_gather_rider(*full_w[0], "B", (0, 1)), "gather_first_d2d")
    placed = lax.dynamic_update_slice(jnp.zeros((N_LAYERS, 3, D), F32),
                                      jnp.where(ci == 0, conv_w, 0.0), (0, 0, chip * SH_ROW))
    conv_full = _allreduce_small(placed.reshape(96, 128)).reshape(N_LAYERS, 3, D)

    qg_s = jnp.tile(q_norm_g, (1, N_Q)) * SCALE
    kg_t = jnp.tile(k_norm_g, (1, N_KV))
    bias = _band_bias()
    saved = []
    for l in range(N_LAYERS):
        w_full, w_sm = full_w[l]
        nxt = full_w[l + 1] if l + 1 < N_LAYERS else None
        h, ht = _rmsnorm_fwd(xs[l], norm_g[l:l + 1])
        (u_conv, u_qkv, u_za, u_gl), got = _in_proj(h, w_full, _gather_rider(*nxt, "A", (0,)) if nxt else None)
        if nxt:
            nxt[0] = got[0]
        y_c = _conv_fwd(u_conv, conv_full[l])
        qs, kvx = _qkv_prep(u_qkv, qg_s[l:l + 1], kg_t[l:l + 1])
        o, got = _attn_fwd(qs, kvx, u_za, sinks[l:l + 1], bias, _gather_rider(*nxt, "A", (1,)) if nxt else None)
        if nxt:
            nxt[1] = got[0]
        (x_next, y_a, y_b, merged), got = _out_proj_fwd(xs[l], y_c, o, u_gl, gate_b[l:l + 1], w_sm,
                                                        _gather_rider(*nxt, "B", (0, 1)) if nxt else None)
        if nxt:
            nxt[0], nxt[1] = got
        xs.append(x_next)
        saved.append((ht, u_conv, u_qkv, u_za, u_gl, y_c, o, y_a, y_b, merged, qs, kvx))

    dout, sq = _loss_head(xs[N_LAYERS], tgt)
    loss = lax.psum(jnp.sum(sq) * (0.5 / D), ("x", "y", "c"))

    small, t_in, t_sm = [None] * N_LAYERS, [None] * N_LAYERS, [None] * N_LAYERS
    grads = None
    halves = None

    def add_halves(g, got):
        f_in, h_in = _add_halves_in(c_idx, g[0], got[0])
        f_sm, h_sm = _add_halves_sm(c_idx, g[1], got[1])
        return f_in, h_in, f_sm, h_sm

    def final_sums(hv, parts):
        return _final_sum_in(chip_idx, hv[0], parts[0]), _final_sum_sm(chip_idx, hv[2], parts[1])

    for l in reversed(range(N_LAYERS)):
        w_full, w_sm = full_w[l]
        ht, u_conv, u_qkv, u_za, u_gl, y_c, o, y_a, y_b, merged, qs, kvx = saved[l]
        (d_ya, d_yb, du_gl, d_yc, d_o, dgb), got = _out_proj_bwd(dout, y_a, y_b, u_gl, gate_b[l:l + 1], w_sm,
                                                                 _swap_rider(*grads) if grads else None)
        if grads:
            halves = add_halves(grads, got)
        g_sm = _small_wgrads(y_c, d_ya, o, d_yb, merged, dout).reshape(3, 4, 2, 128, D)
        du_conv, dcw = _conv_bwd(d_yc, u_conv, conv_full[l])
        (dqs, dkv, dza, dsk), got = _attn_bwd(d_o, qs, kvx, u_za, sinks[l:l + 1], bias,
                                              _scatter_rider(halves[1], halves[3]) if halves else None)
        if halves:
            t_in[l + 1], t_sm[l + 1] = final_sums(halves, got)
        dsk = jnp.sum(dsk[0].reshape(N_KV, 2, 2, BLK), axis=-1).transpose(0, 2, 1).reshape(N_Q)
        du_attn, dqg, dkg = _qkv_post(u_qkv, dqs, dkv, dza, qg_s[l:l + 1], kg_t[l:l + 1])
        du = (du_conv, du_attn, du_gl)
        grads = (_in_proj_wgrad(ht, du).reshape(2, 512, N_IN), g_sm)
        dh, got = _in_proj_bwd(du, w_full, _swap_rider(*grads) if l == 0 else None)
        dout, dng = _rmsnorm_bwd(dh, xs[l], norm_g[l:l + 1], dout)
        small[l] = (jnp.sum(dng, axis=0), SCALE * jnp.sum(dqg.reshape(8 * N_Q, HEAD), axis=0),
                    jnp.sum(dkg.reshape(8 * N_KV, HEAD), axis=0), dsk, jnp.sum(dgb, axis=0), dcw[:3])
    grad_x = dout.reshape(1, t, D)
    halves = add_halves(grads, got)
    t_in[0], t_sm[0] = final_sums(halves, _ride_alone(_scatter_rider(halves[1], halves[3]), "scatter_last"))

    stack = lambda k: jnp.stack([small[l][k] for l in range(N_LAYERS)])
    pack = jnp.concatenate([_pad_rows(stack(0), 32), _pad_rows(stack(1), 8), _pad_rows(stack(2), 8),
                            _pad_rows(stack(3), 8), _pad_rows(stack(4), 64), _pad_rows(stack(5), 96)], axis=0)
    red = _allreduce_small(pack)
    g_norm_g = red[0:32].reshape(N_LAYERS, D)
    g_q_norm_g = red[32:40].reshape(-1)[:N_LAYERS * HEAD].reshape(N_LAYERS, HEAD)
    g_k_norm_g = red[40:48].reshape(-1)[:N_LAYERS * HEAD].reshape(N_LAYERS, HEAD)
    g_sinks = red[48:56].reshape(-1)[:N_LAYERS * N_Q].reshape(N_LAYERS, N_Q)
    g_gate_b = red[56:120].reshape(N_LAYERS, 2 * D)
    g_conv_full = red[120:216].reshape(N_LAYERS, 3, D)
    g_conv_w = lax.dynamic_slice(g_conv_full, (0, 0, chip * SH_ROW), (N_LAYERS, 3, SH_ROW))

    g_w_in, g_w_co, g_w_ao, g_w_out = _join_halves(t_in, t_sm)

    r_in = N_LAYERS * D
    d_in, nm_in, nv_in = (a.reshape(N_LAYERS, D, SH_IN) for a in _adamw_big(
        w_in.reshape(r_in, SH_IN), g_w_in.reshape(r_in, SH_IN), m_w_in.reshape(r_in, SH_IN),
        v_w_in.reshape(r_in, SH_IN), "adamw_w_in"))
    r_sm = N_LAYERS * SH_ROW
    big = {}
    for nm, w, g, m, v in (("co", w_conv_out, g_w_co, m_w_conv_out, v_w_conv_out),
                           ("ao", w_attn_out, g_w_ao, m_w_attn_out, v_w_attn_out),
                           ("out", w_out, g_w_out, m_w_out, v_w_out)):
        big[nm] = tuple(a.reshape(N_LAYERS, SH_ROW, D) for a in _adamw_big(
            w.reshape(r_sm, D), g.reshape(r_sm, D), m.reshape(r_sm, D), v.reshape(r_sm, D), "adamw_w_small"))
    sm_w = [norm_g, conv_w, q_norm_g, k_norm_g, sinks, gate_b]
    sm_g = [g_norm_g, g_conv_w, g_q_norm_g, g_k_norm_g, g_sinks, g_gate_b]
    sm_m = [m_norm_g, m_conv_w, m_q_norm_g, m_k_norm_g, m_sinks, m_gate_b]
    sm_v = [v_norm_g, v_conv_w, v_q_norm_g, v_k_norm_g, v_sinks, v_gate_b]
    sd, snm, snv = _adamw_small(sm_w, sm_g, sm_m, sm_v)

    def order(norm, w_in_, conv, qn, kn, sk, co, ao, gb, wo):
        return [norm, w_in_, conv, qn, kn, sk, co, ao, gb, wo]

    grads = order(g_norm_g, g_w_in, g_conv_w, g_q_norm_g, g_k_norm_g, g_sinks, g_w_co, g_w_ao, g_gate_b, g_w_out)
    deltas = order(sd[0], d_in, sd[1], sd[2], sd[3], sd[4], big["co"][0], big["ao"][0], sd[5], big["out"][0])
    new_m = order(snm[0], nm_in, snm[1], snm[2], snm[3], snm[4], big["co"][1], big["ao"][1], snm[5], big["out"][1])
    new_v = order(snv[0], nv_in, snv[1], snv[2], snv[3], snv[4], big["co"][2], big["ao"][2], snv[5], big["out"][2])
    return (loss, grad_x, *grads, *deltas, *new_m, *new_v)
```

```python
import functools

import jax
import jax.numpy as jnp
from jax import lax
from jax.experimental import pallas as pl
from jax.experimental.pallas import tpu as pltpu

F32, BF16 = jnp.float32, jnp.bfloat16
SDS = jax.ShapeDtypeStruct
MESH = pl.DeviceIdType.MESH
ANY = pl.BlockSpec(memory_space=pl.ANY)

D = 1024
N_IN = 8704
N_LAYERS = 4
N_Q, N_KV, HEAD = 16, 4, 64
GROUP = N_Q // N_KV
BLK = 128
EPS = 1e-6
NEG = -1e30
SCALE = HEAD ** -0.5
SH_IN = N_IN // 4
SH_ROW = D // 4
CB = 512
SEG_CONV, SEG_Q, SEG_KV, SEG_ZA, SEG_GL = (0, 8), (8, 2), (10, 1), (11, 2), (13, 4)
VMEM_BIG = 56 * 1024 * 1024

ADAM_LR, ADAM_B1, ADAM_B2, ADAM_EPS, ADAM_WD, ADAM_STEP = 0.001, 0.9, 0.999, 1e-08, 0.01, 10


def _cp(*sem, vmem=None):
    return pltpu.CompilerParams(dimension_semantics=sem if sem else None, vmem_limit_bytes=vmem)


def _sigmoid(z):
    return 1.0 / (1.0 + jnp.exp(-z))


def _dot(a, b):
    return jnp.dot(a, b, preferred_element_type=F32)


def _dot_nt(a, b):
    return lax.dot_general(a, b, (((1,), (1,)), ((), ())), preferred_element_type=F32)


def _dot_tn(a, b):
    return lax.dot_general(a, b, (((0,), (0,)), ((), ())), preferred_element_type=F32)


def _rms(xh):
    r = lax.rsqrt(jnp.mean(xh * xh, axis=-1, keepdims=True) + EPS)
    return xh * r, r


def _fold8(v):
    return jnp.sum(v.reshape(v.shape[0] // 8, 8, v.shape[1]), axis=0)


def _cast_w_in(chip_idx, w, layer):
    def body(j_ref, i_ref, o_ref):
        o_ref[...] = i_ref[...].astype(BF16)

    return pl.pallas_call(
        body, name="cast_w_in",
        grid_spec=pltpu.PrefetchScalarGridSpec(
            num_scalar_prefetch=1, grid=(2,),
            in_specs=[pl.BlockSpec((None, 512, SH_IN), lambda i, j: (layer, i, 0))],
            out_specs=pl.BlockSpec((512, SH_IN), lambda i, j: (i, j[0]))),
        out_shape=SDS((D, N_IN), BF16), compiler_params=_cp("parallel"),
    )(chip_idx, w)


def _cast_w_small(chip_idx, a, b, c, layer):
    def body(j_ref, a_ref, b_ref, c_ref, o_ref):
        o_ref[0] = a_ref[...].astype(BF16)
        o_ref[1] = b_ref[...].astype(BF16)
        o_ref[2] = c_ref[...].astype(BF16)

    spec = pl.BlockSpec((None, SH_ROW, D), lambda i, j: (layer, 0, 0))
    return pl.pallas_call(
        body, name="cast_w_small",
        grid_spec=pltpu.PrefetchScalarGridSpec(
            num_scalar_prefetch=1, grid=(1,), in_specs=[spec, spec, spec],
            out_specs=pl.BlockSpec((3, SH_ROW, D), lambda i, j: (0, j[0], 0))),
        out_shape=SDS((3, D, D), BF16), compiler_params=_cp("parallel"),
    )(chip_idx, a, b, c)


def _mesh_pos():
    return lax.axis_index("x"), lax.axis_index("y"), lax.axis_index("c")


def _other_chips(x, y):
    return [(1 - x, y), (x, 1 - y), (1 - x, 1 - y)]


class _Rider:
    def __init__(self, ins, out_shape, n, copies, aliases=()):
        self.ins, self.out_shape, self.n, self.copies, self.aliases = list(ins), list(out_shape), n, copies, aliases


def _rcopy(src, dst, send, recv, k, to):
    return pltpu.make_async_remote_copy(src_ref=src, dst_ref=dst, send_sem=send.at[k], recv_sem=recv.at[k],
                                        device_id=to, device_id_type=MESH)


def _hosted_call(body, rider, *, name, grid, in_specs, out_specs, out_shape, args, scratch_shapes=(), vmem=None):
    n_in, n_out, n_scr = len(in_specs), len(out_specs), len(scratch_shapes)
    r_in, r_out = len(rider.ins), len(rider.out_shape)

    def full_body(*refs):
        host_in, rid_in = refs[:n_in], refs[n_in:n_in + r_in]
        o0 = n_in + r_in
        host_out, rid_out = refs[o0:o0 + n_out], refs[o0 + n_out:o0 + n_out + r_out]
        s0 = o0 + n_out + r_out
        host_scr, (send, recv) = refs[s0:s0 + n_scr], refs[s0 + n_scr:]
        if body is None:
            cps = rider.copies(rid_in, rid_out, send, recv)
            for cp in cps:
                cp.start()
            for cp in cps:
                cp.wait()
            return
        ids = [pl.program_id(a) for a in range(len(grid))]
        first = functools.reduce(lambda p, q: p & q, [i == 0 for i in ids])
        last = functools.reduce(lambda p, q: p & q, [i == g - 1 for i, g in zip(ids, grid)])

        @pl.when(first)
        def _():
            for cp in rider.copies(rid_in, rid_out, send, recv):
                cp.start()

        body(*host_in, *host_out, *host_scr)

        @pl.when(last)
        def _():
            for cp in rider.copies(rid_in, rid_out, send, recv):
                cp.wait()

    res = pl.pallas_call(
        full_body, name=name, grid=grid if body is not None else (),
        in_specs=list(in_specs) + [ANY] * r_in, out_specs=list(out_specs) + [ANY] * r_out,
        out_shape=list(out_shape) + rider.out_shape,
        scratch_shapes=list(scratch_shapes) + [pltpu.SemaphoreType.DMA((rider.n,))] * 2,
        input_output_aliases={n_in + i: n_out + o for i, o in rider.aliases},
        compiler_params=_cp(*(("arbitrary",) * len(grid) if body is not None else ()), vmem=vmem),
    )(*args, *rider.ins)
    return res[:n_out], res[n_out:]


def _call(body, rider, **kw):
    if rider is not None:
        return _hosted_call(body, rider, **kw)
    res = pl.pallas_call(
        body, name=kw["name"], grid=kw["grid"], in_specs=list(kw["in_specs"]), out_specs=list(kw["out_specs"]),
        out_shape=list(kw["out_shape"]), scratch_shapes=list(kw.get("scratch_shapes", ())),
        compiler_params=_cp(*(("arbitrary",) * len(kw["grid"])), vmem=kw.get("vmem")),
    )(*kw["args"])
    return res, []


def _gather_rider(arrays, stage):
    def copies(ins, outs, send, recv):
        x, y, c = _mesh_pos()
        cps = []
        for full in outs:
            for k, chip in enumerate(_other_chips(x, y)):
                whose = 2 * x + y if stage == "A" else 2 * chip[0] + chip[1]
                if len(full.shape) == 2:
                    reg = full.at[pl.ds(pl.multiple_of(c * 512, 512), 512), pl.ds(pl.multiple_of(whose * SH_IN, 128), SH_IN)]
                else:
                    reg = full.at[:, pl.ds(pl.multiple_of(whose * SH_ROW + c * 128, 128), 128), :]
                to = (*chip, c) if stage == "A" else (x, y, 1 - c)
                cps.append(_rcopy(reg, reg, send, recv, len(cps), to))
        return cps

    return _Rider(arrays, [SDS(v.shape, v.dtype) for v in arrays], 3 * len(arrays), copies,
                  aliases=tuple((i, i) for i in range(len(arrays))))


def _staged_copies(copies, stages, sem_in, sem_out):
    busy, count = {}, {}
    for idx, (src, dst, kind) in enumerate(copies):
        slot = count.get(kind, 0) % 2
        count[kind] = count.get(kind, 0) + 1
        if (kind, slot) in busy:
            busy.pop((kind, slot)).wait()
        buf = stages[kind].at[slot]
        cin = pltpu.make_async_copy(src, buf, sem_in.at[idx])
        cin.start()
        cin.wait()
        cout = pltpu.make_async_copy(buf, dst, sem_out.at[idx])
        cout.start()
        busy[(kind, slot)] = cout
    for cp in busy.values():
        cp.wait()


def _allreduce_small(pack):
    rows = pack.shape[0]

    def body(p_ref, o_ref, buf, send, recv):
        x, y, c = _mesh_pos()
        me = 4 * x + 2 * y + c
        sends = []
        for r in range(1, 8):
            to = (x if not (r & 4) else 1 - x, y if not (r & 2) else 1 - y, c if not (r & 1) else 1 - c)
            cp = pltpu.make_async_remote_copy(src_ref=p_ref, dst_ref=buf.at[me], send_sem=send.at[r - 1],
                                              recv_sem=recv.at[r - 1], device_id=to, device_id_type=MESH)
            cp.start()
            sends.append(cp)
        buf[me] = p_ref[...]
        for r in range(1, 8):
            frm = (4 * x + 2 * y + c) ^ r
            pltpu.make_async_remote_copy(src_ref=p_ref, dst_ref=buf.at[frm], send_sem=send.at[r - 1],
                                         recv_sem=recv.at[r - 1], device_id=(x, y, c), device_id_type=MESH).wait_recv()
        acc = buf[0]
        for d in range(1, 8):
            acc = acc + buf[d]
        o_ref[...] = acc
        for cp in sends:
            cp.wait_send()

    vm = pl.BlockSpec(memory_space=pltpu.VMEM)
    return pl.pallas_call(
        body, name="allreduce_small", in_specs=[vm], out_specs=vm, out_shape=SDS((rows, 128), F32),
        scratch_shapes=[pltpu.VMEM((8, rows, 128), F32), pltpu.SemaphoreType.DMA((7,)), pltpu.SemaphoreType.DMA((7,))],
    )(pack)


def _rmsnorm_fwd(x, g):
    t = x.shape[0]
    tm = min(512, t)

    def body(x_ref, g_ref, h_ref, ht_ref):
        xv = x_ref[...]
        r = lax.rsqrt(jnp.mean(xv * xv, axis=-1, keepdims=True) + EPS)
        h = xv * r * g_ref[...]
        h_ref[...] = h.astype(BF16)
        ht_ref[...] = h.T.astype(BF16)

    return pl.pallas_call(
        body, name="rmsnorm_fwd", grid=(t // tm,),
        in_specs=[pl.BlockSpec((tm, D), lambda i: (i, 0)), pl.BlockSpec((1, D), lambda i: (0, 0))],
        out_specs=[pl.BlockSpec((tm, D), lambda i: (i, 0)), pl.BlockSpec((D, tm), lambda i: (0, i))],
        out_shape=[SDS((t, D), BF16), SDS((D, t), BF16)],
        compiler_params=_cp("parallel", vmem=VMEM_BIG),
    )(x, g)


FWD_SEGS = ((0, 8), (8, 3), (11, 2), (13, 4))


def _in_proj(h, w_full, rider):
    t = h.shape[0]
    tm = min(2048, t)

    def body(a_ref, b_ref, *o_refs):
        j = pl.program_id(1)
        for o_ref, (off, nblk) in zip(o_refs, FWD_SEGS):
            @pl.when((j >= off) & (j < off + nblk))
            def _():
                o_ref[...] = _dot(a_ref[...], b_ref[...]).astype(BF16)

    def out(seg):
        off, nblk = seg
        return pl.BlockSpec((tm, CB), lambda i, j: (i, jnp.clip(j - off, 0, nblk - 1)))

    res, got = _call(
        body, rider, name="in_proj", grid=(t // tm, N_IN // CB),
        in_specs=[pl.BlockSpec((tm, D), lambda i, j: (i, 0)), pl.BlockSpec((D, CB), lambda i, j: (0, j))],
        out_specs=[out(s) for s in FWD_SEGS], out_shape=[SDS((t, s[1] * CB), BF16) for s in FWD_SEGS],
        args=(h, w_full), vmem=VMEM_BIG)
    return res, got


def _conv_fwd(u_conv, conv_w):
    t = u_conv.shape[0]
    tm = min(256, t)
    hb = tm // 16

    def body(v_ref, b_ref, c_ref, z_ref, hv_ref, hc_ref, w_ref, y_ref):
        i = pl.program_id(0)
        cv = c_ref[...].astype(F32) * v_ref[...].astype(F32)
        halo = hc_ref[...].astype(F32) * hv_ref[...].astype(F32)
        halo = jnp.where(i > 0, halo, 0.0)
        row = lax.broadcasted_iota(jnp.int32, (tm, 1), 0)
        s1 = jnp.where(row == 0, halo[15:16], pltpu.roll(cv, 1, 0))
        s2 = jnp.where(row == 0, halo[14:15], jnp.where(row == 1, halo[15:16], pltpu.roll(cv, 2, 0)))
        conv = w_ref[0:1, :] * s2 + w_ref[1:2, :] * s1 + w_ref[2:3, :] * cv
        z = z_ref[...].astype(F32)
        y_ref[...] = (b_ref[...].astype(F32) * conv * (z * _sigmoid(z))).astype(BF16)

    def col(k):
        return pl.BlockSpec((tm, D), lambda i: (i, k))

    def halo(k):
        return pl.BlockSpec((16, D), lambda i: (jnp.maximum(i * hb - 1, 0), k))

    return pl.pallas_call(
        body, name="conv_fwd", grid=(t // tm,),
        in_specs=[col(0), col(1), col(2), col(3), halo(0), halo(2), pl.BlockSpec((3, D), lambda i: (0, 0))],
        out_specs=pl.BlockSpec((tm, D), lambda i: (i, 0)), out_shape=SDS((t, D), BF16),
        compiler_params=_cp("parallel", vmem=VMEM_BIG),
    )(u_conv, u_conv, u_conv, u_conv, u_conv, u_conv, conv_w)


KVX = 4 * N_KV * 128


def _iota2(shape):
    return lax.broadcasted_iota(jnp.int32, shape, 0), lax.broadcasted_iota(jnp.int32, shape, 1)


def _head_sum(v):
    r, c = _iota2((128, 128))
    ones = ((r >> 6) == (c >> 6)).astype(BF16)
    hi = v.astype(BF16)
    lo = (v - hi.astype(F32)).astype(BF16)
    return jnp.concatenate([_dot(hi[:, g:g + 128], ones) + _dot(lo[:, g:g + 128], ones)
                            for g in range(0, v.shape[1], 128)], axis=1)


def _expand_mats():
    r, c = _iota2((N_KV * HEAD, N_KV * 128))
    base = ((r >> 6) << 7) + (r & 63)
    return (c == base).astype(BF16), (c == base + 64).astype(BF16)


def _fold_mat():
    r, c = _iota2((N_KV * 128, N_KV * HEAD))
    return (((r >> 7) == (c >> 6)) & ((r & 63) == (c & 63))).astype(BF16)


def _qkv_prep(u_qkv, qg_s, kg_t):
    t = u_qkv.shape[0]
    tm = min(512, t)

    def body(u_ref, qg_ref, kg_ref, qs_ref, kvx_ref):
        q = u_ref[:, 0:D].astype(F32)
        rq = lax.rsqrt(_head_sum(q * q) * (1.0 / HEAD) + EPS)
        qs_ref[...] = (q * rq * qg_ref[...]).astype(BF16)
        k = u_ref[:, D:D + 256].astype(F32)
        rk = lax.rsqrt(_head_sum(k * k) * (1.0 / HEAD) + EPS)
        kn = (k * rk * kg_ref[...]).astype(BF16)
        v = u_ref[:, D + 256:D + 512]
        e_lo, e_hi = _expand_mats()
        kvx_ref[:, 0:512] = _dot(kn, e_lo).astype(BF16)
        kvx_ref[:, 512:1024] = _dot(kn, e_hi).astype(BF16)
        kvx_ref[:, 1024:1536] = _dot(v, e_lo).astype(BF16)
        kvx_ref[:, 1536:2048] = _dot(v, e_hi).astype(BF16)

    return pl.pallas_call(
        body, name="qkv_prep", grid=(t // tm,),
        in_specs=[pl.BlockSpec((tm, 1536), lambda i: (i, 0)), pl.BlockSpec((1, D), lambda i: (0, 0)),
                  pl.BlockSpec((1, 256), lambda i: (0, 0))],
        out_specs=[pl.BlockSpec((tm, D), lambda i: (i, 0)), pl.BlockSpec((tm, KVX), lambda i: (i, 0))],
        out_shape=[SDS((t, D), BF16), SDS((t, KVX), BF16)], compiler_params=_cp("parallel", vmem=VMEM_BIG),
    )(u_qkv, qg_s, kg_t)


def _band_bias():
    j, r = _iota2((2 * BLK, 2 * BLK))
    diff = (r & (BLK - 1)) - j + BLK
    band = (diff >= 0) & (diff < BLK)
    return jnp.stack([jnp.where(band & (j >= BLK), 0.0, NEG), jnp.where(band, 0.0, NEG)]).astype(F32)


def _pair_rows(ref_or_val, hk):
    return jnp.concatenate([ref_or_val[:, 256 * hk:256 * hk + 128], ref_or_val[:, 256 * hk + 128:256 * hk + 256]], axis=0)


def _sink_row(sink_ref, hk, half):
    return jnp.concatenate([jnp.full((1, BLK), sink_ref[0, GROUP * hk + half], F32),
                            jnp.full((1, BLK), sink_ref[0, GROUP * hk + 2 + half], F32)], axis=1)


def _softmax_half(qpp, kvb, hk, half, bias, sink_ref):
    kx = kvb[:, 512 * half + 128 * hk:512 * half + 128 * hk + 128]
    vx = kvb[:, 1024 + 512 * half + 128 * hk:1024 + 512 * half + 128 * hk + 128]
    s = _dot_nt(kx, qpp) + bias
    sink = _sink_row(sink_ref, hk, half)
    m = jnp.maximum(jnp.max(s, axis=0, keepdims=True), sink)
    p = jnp.exp(s - m)
    es = jnp.exp(sink - m)
    inv = 1.0 / (jnp.sum(p, axis=0, keepdims=True) + es)
    return p * inv, es * inv, kx, vx


def _attn_fwd(qs, kvx, u_za, sinks, bias, rider):
    t = qs.shape[0]
    nb = t // BLK

    def body(q_ref, kc_ref, kp_ref, za_ref, sink_ref, bias_ref, o_ref):
        kvb = jnp.concatenate([kp_ref[...], kc_ref[...]], axis=0)
        bias_v = bias_ref[...]
        cols = []
        for hk in range(N_KV):
            qpp = _pair_rows(q_ref, hk)
            lo = _softmax_half(qpp, kvb, hk, 0, bias_v, sink_ref)
            hi = _softmax_half(qpp, kvb, hk, 1, bias_v, sink_ref)
            opp = _dot_tn(jnp.concatenate([lo[0], hi[0]], axis=0).astype(BF16), jnp.concatenate([lo[3], hi[3]], axis=0))
            cols += [opp[:BLK], opp[BLK:]]
        za = za_ref[...].astype(F32)
        o_ref[...] = (jnp.concatenate(cols, axis=1) * (za * _sigmoid(za))).astype(BF16)

    prev = lambda n: jnp.maximum(n - 1, 0)
    (o,), got = _call(
        body, rider, name="attn_fwd", grid=(nb,),
        in_specs=[pl.BlockSpec((BLK, D), lambda n: (n, 0)),
                  pl.BlockSpec((BLK, KVX), lambda n: (n, 0)), pl.BlockSpec((BLK, KVX), lambda n: (prev(n), 0)),
                  pl.BlockSpec((BLK, D), lambda n: (n, 0)), pl.BlockSpec(memory_space=pltpu.SMEM),
                  pl.BlockSpec((None, 2 * BLK, 2 * BLK), lambda n: (jnp.minimum(n, 1), 0, 0))],
        out_specs=[pl.BlockSpec((BLK, D), lambda n: (n, 0))], out_shape=[SDS((t, D), BF16)],
        args=(qs, kvx, kvx, u_za, sinks, bias), vmem=VMEM_BIG)
    return o, got


def _out_proj_fwd(x, y_c, o, u_gl, gate_b, w_sm, rider):
    t = x.shape[0]
    tm = min(512, t)

    def body(x_ref, yc_ref, o_ref, gla_ref, glb_ref, gb_ref, wco_ref, wao_ref, wout_ref,
             xn_ref, ya_ref, yb_ref, mg_ref):
        ya = _dot(yc_ref[...], wco_ref[...])
        yb = _dot(o_ref[...], wao_ref[...])
        gb = gb_ref[...]
        ga_ = _sigmoid(gla_ref[...].astype(F32) + gb[:, :D])
        gb_ = _sigmoid(glb_ref[...].astype(F32) + gb[:, D:])
        merged = (ga_ * ya + gb_ * yb).astype(BF16)
        ya_ref[...] = ya.astype(BF16)
        yb_ref[...] = yb.astype(BF16)
        mg_ref[...] = merged
        xn_ref[...] = x_ref[...] + _dot(merged, wout_ref[...])

    row = pl.BlockSpec((tm, D), lambda i: (i, 0))
    wspec = lambda a: pl.BlockSpec((None, D, D), lambda i: (a, 0, 0))
    return _call(
        body, rider, name="out_proj_fwd", grid=(t // tm,),
        in_specs=[row, row, row, pl.BlockSpec((tm, D), lambda i: (i, 0)), pl.BlockSpec((tm, D), lambda i: (i, 1)),
                  pl.BlockSpec((1, 2 * D), lambda i: (0, 0)), wspec(0), wspec(1), wspec(2)],
        out_specs=[row, row, row, row],
        out_shape=[SDS((t, D), F32), SDS((t, D), BF16), SDS((t, D), BF16), SDS((t, D), BF16)],
        args=(x, y_c, o, u_gl, u_gl, gate_b, w_sm, w_sm, w_sm), vmem=VMEM_BIG)


def _loss_head(y, tgt):
    t = y.shape[0]
    tm = min(512, t)

    def body(y_ref, t_ref, dy_ref, acc_ref):
        @pl.when(pl.program_id(0) == 0)
        def _():
            acc_ref[...] = jnp.zeros_like(acc_ref)
        err = y_ref[...] - t_ref[...]
        dy_ref[...] = err * (1.0 / D)
        sq = _fold8(err * err)
        tot = sq[:, 0:128]
        for k in range(1, D // 128):
            tot = tot + sq[:, 128 * k:128 * (k + 1)]
        acc_ref[...] += tot

    row = pl.BlockSpec((tm, D), lambda i: (i, 0))
    return pl.pallas_call(
        body, name="loss_head", grid=(t // tm,), in_specs=[row, row],
        out_specs=[row, pl.BlockSpec((8, 128), lambda i: (0, 0))],
        out_shape=[SDS((t, D), F32), SDS((8, 128), F32)], compiler_params=_cp("arbitrary"),
    )(y, tgt)


def _out_proj_bwd(dout, y_a, y_b, u_gl, gate_b, w_sm, rider):
    t = dout.shape[0]
    tm = min(512, t)

    def body(do_ref, ya_ref, yb_ref, gla_ref, glb_ref, gb_ref, wco_ref, wao_ref, wout_ref,
             dya_ref, dyb_ref, dgl_ref, dyc_ref, dob_ref, dgb_ref):
        @pl.when(pl.program_id(0) == 0)
        def _():
            dgb_ref[...] = jnp.zeros_like(dgb_ref)
        dm = _dot_nt(do_ref[...].astype(BF16), wout_ref[...])
        gb = gb_ref[...]
        ga_ = _sigmoid(gla_ref[...].astype(F32) + gb[:, :D])
        gb_ = _sigmoid(glb_ref[...].astype(F32) + gb[:, D:])
        dya = (ga_ * dm).astype(BF16)
        dyb = (gb_ * dm).astype(BF16)
        dgla = ya_ref[...].astype(F32) * dm * (ga_ * (1.0 - ga_))
        dglb = yb_ref[...].astype(F32) * dm * (gb_ * (1.0 - gb_))
        dya_ref[...] = dya
        dyb_ref[...] = dyb
        dgl_ref[:, :D] = dgla.astype(BF16)
        dgl_ref[:, D:] = dglb.astype(BF16)
        dgb_ref[:, :D] += _fold8(dgla)
        dgb_ref[:, D:] += _fold8(dglb)
        dyc_ref[...] = _dot_nt(dya, wco_ref[...]).astype(BF16)
        dob_ref[...] = _dot_nt(dyb, wao_ref[...]).astype(BF16)

    row = pl.BlockSpec((tm, D), lambda i: (i, 0))
    wspec = lambda a: pl.BlockSpec((None, D, D), lambda i: (a, 0, 0))
    return _call(
        body, rider, name="out_proj_bwd", grid=(t // tm,),
        in_specs=[row, row, row, pl.BlockSpec((tm, D), lambda i: (i, 0)), pl.BlockSpec((tm, D), lambda i: (i, 1)),
                  pl.BlockSpec((1, 2 * D), lambda i: (0, 0)), wspec(0), wspec(1), wspec(2)],
        out_specs=[row, row, pl.BlockSpec((tm, 2 * D), lambda i: (i, 0)), row, row,
                   pl.BlockSpec((8, 2 * D), lambda i: (0, 0))],
        out_shape=[SDS((t, D), BF16), SDS((t, D), BF16), SDS((t, 2 * D), BF16), SDS((t, D), BF16), SDS((t, D), BF16),
                   SDS((8, 2 * D), F32)],
        args=(dout, y_a, y_b, u_gl, u_gl, gate_b, w_sm, w_sm, w_sm), vmem=VMEM_BIG)


def _small_wgrads(y_c, d_ya, o, d_yb, merged, dout):
    t = y_c.shape[0]
    tk = min(512, t)

    def body(yc_ref, dya_ref, o_ref, dyb_ref, mg_ref, do_ref, g_ref):
        @pl.when(pl.program_id(0) == 0)
        def _():
            g_ref[...] = jnp.zeros_like(g_ref)
        g_ref[0] += _dot_tn(yc_ref[...], dya_ref[...])
        g_ref[1] += _dot_tn(o_ref[...], dyb_ref[...])
        g_ref[2] += _dot_tn(mg_ref[...], do_ref[...].astype(BF16))

    row = pl.BlockSpec((tk, D), lambda k: (k, 0))
    return pl.pallas_call(
        body, name="small_wgrads", grid=(t // tk,), in_specs=[row] * 6,
        out_specs=pl.BlockSpec((3, D, D), lambda k: (0, 0, 0)), out_shape=SDS((3, D, D), F32),
        compiler_params=_cp("arbitrary", vmem=VMEM_BIG),
    )(y_c, d_ya, o, d_yb, merged, dout)


def _conv_bwd(d_yc, u_conv, conv_w, rider):
    t = d_yc.shape[0]
    tm = min(256, t)
    hb = tm // 16
    last_halo = t // 16 - 1
    n_steps = t // tm

    def body(dy_ref, v_ref, b_ref, c_ref, z_ref, hv_ref, hc_ref, ndy_ref, nb_ref, nz_ref, w_ref, du_ref, dw_ref):
        i = pl.program_id(0)

        @pl.when(i == 0)
        def _():
            dw_ref[...] = jnp.zeros_like(dw_ref)
        v, c = v_ref[...].astype(F32), c_ref[...].astype(F32)
        b, z = b_ref[...].astype(F32), z_ref[...].astype(F32)
        cv = c * v
        halo = jnp.where(i > 0, hc_ref[...].astype(F32) * hv_ref[...].astype(F32), 0.0)
        row = lax.broadcasted_iota(jnp.int32, (tm, 1), 0)
        s1 = jnp.where(row == 0, halo[15:16], pltpu.roll(cv, 1, 0))
        s2 = jnp.where(row == 0, halo[14:15], jnp.where(row == 1, halo[15:16], pltpu.roll(cv, 2, 0)))
        w0, w1, w2 = w_ref[0:1, :], w_ref[1:2, :], w_ref[2:3, :]
        conv = w0 * s2 + w1 * s1 + w2 * cv
        sig = _sigmoid(z)
        sz = z * sig
        dsz = sig * (1.0 + z * (1.0 - sig))
        dy = dy_ref[...].astype(F32)
        dconv = dy * b * sz
        nz = nz_ref[...].astype(F32)
        nxt = ndy_ref[...].astype(F32) * nb_ref[...].astype(F32) * (nz * _sigmoid(nz))
        nxt = jnp.where(i < n_steps - 1, nxt, 0.0)
        a1 = jnp.where(row == tm - 1, nxt[0:1], pltpu.roll(dconv, tm - 1, 0))
        a2 = jnp.where(row == tm - 2, nxt[0:1], jnp.where(row == tm - 1, nxt[1:2], pltpu.roll(dconv, tm - 2, 0)))
        dcv = w2 * dconv + w1 * a1 + w0 * a2
        du_ref[:, 0:D] = (dcv * c).astype(BF16)
        du_ref[:, D:2 * D] = (dy * conv * sz).astype(BF16)
        du_ref[:, 2 * D:3 * D] = (dcv * v).astype(BF16)
        du_ref[:, 3 * D:4 * D] = (dy * b * conv * dsz).astype(BF16)
        r8 = lax.broadcasted_iota(jnp.int32, (8, 1), 0)
        dw_ref[...] += jnp.where(r8 == 0, jnp.sum(dconv * s2, axis=0, keepdims=True),
                                 jnp.where(r8 == 1, jnp.sum(dconv * s1, axis=0, keepdims=True),
                                           jnp.where(r8 == 2, jnp.sum(dconv * cv, axis=0, keepdims=True), 0.0)))

    def col(k):
        return pl.BlockSpec((tm, D), lambda i: (i, k))

    def halo(k):
        return pl.BlockSpec((16, D), lambda i: (jnp.maximum(i * hb - 1, 0), k))

    def nxt(k):
        return pl.BlockSpec((16, D), lambda i: (jnp.minimum((i + 1) * hb, last_halo), k))

    return _call(
        body, rider, name="conv_bwd", grid=(t // tm,),
        in_specs=[col(0), col(0), col(1), col(2), col(3), halo(0), halo(2), nxt(0), nxt(1), nxt(3),
                  pl.BlockSpec((3, D), lambda i: (0, 0))],
        out_specs=[pl.BlockSpec((tm, 4 * D), lambda i: (i, 0)), pl.BlockSpec((8, D), lambda i: (0, 0))],
        out_shape=[SDS((t, 4 * D), BF16), SDS((8, D), F32)],
        args=(d_yc, u_conv, u_conv, u_conv, u_conv, u_conv, u_conv, d_yc, u_conv, u_conv, conv_w), vmem=VMEM_BIG)


def _attn_bwd(d_o, qs, kvx, u_za, sinks, bias, rider):
    t = d_o.shape[0]
    nb = t // BLK

    def body(q_ref, kc_ref, kp_ref, za_ref, do_ref, sink_ref, bias_ref, dq_ref, dkv_ref, dza_ref, dsk_ref, carry_ref):
        n = pl.program_id(0)

        @pl.when(n == 0)
        def _():
            carry_ref[...] = jnp.zeros_like(carry_ref)
            dsk_ref[...] = jnp.zeros_like(dsk_ref)

        live = n < nb
        kvb = jnp.concatenate([kp_ref[...], kc_ref[...]], axis=0)
        bias_v = bias_ref[...]
        za = za_ref[...].astype(F32)
        sig = _sigmoid(za)
        dsa = sig * (1.0 + za * (1.0 - sig))
        do = jnp.where(live, do_ref[...].astype(F32), 0.0)
        dattn = (do * (za * sig)).astype(BF16)
        lo_lanes = lax.broadcasted_iota(jnp.int32, (1, 128), 1) < HEAD
        dq_cols, attn_cols, dk_cols, dv_cols, dsk_rows = [], [], [], [], []
        for hk in range(N_KV):
            qpp = _pair_rows(q_ref, hk)
            dapp = _pair_rows(dattn, hk)
            probs, dss, xk, xv = [], [], [], []
            for half in range(2):
                prob, psink, kx, vx = _softmax_half(qpp, kvb, hk, half, bias_v, sink_ref)
                dp = _dot_nt(vx, dapp)
                drow = jnp.sum(prob * dp, axis=0, keepdims=True)
                ds = (prob * (dp - drow)).astype(BF16)
                prob_b = prob.astype(BF16)
                xk.append(_dot(ds, qpp))
                xv.append(_dot(prob_b, dapp))
                probs.append(prob_b)
                dss.append(ds)
                dsk_rows.append(-psink * drow)
            kcat = jnp.concatenate([kvb[:, 128 * hk:128 * hk + 128], kvb[:, 512 + 128 * hk:512 + 128 * hk + 128]], axis=0)
            vcat = jnp.concatenate([kvb[:, 1024 + 128 * hk:1024 + 128 * hk + 128],
                                    kvb[:, 1536 + 128 * hk:1536 + 128 * hk + 128]], axis=0)
            app = _dot_tn(jnp.concatenate(probs, axis=0), vcat)
            dqpp = _dot_tn(jnp.concatenate(dss, axis=0), kcat)
            dq_cols += [dqpp[:BLK], dqpp[BLK:]]
            attn_cols += [app[:BLK], app[BLK:]]
            dk_cols.append(jnp.where(lo_lanes, xk[0], xk[1]))
            dv_cols.append(jnp.where(lo_lanes, xv[0], xv[1]))

        @pl.when(live)
        def _():
            dq_ref[...] = jnp.concatenate(dq_cols, axis=1).astype(BF16)
            dza_ref[...] = (do * jnp.concatenate(attn_cols, axis=1) * dsa).astype(BF16)

        band = jnp.concatenate(dk_cols + dv_cols, axis=1)
        dkv_ref[...] = (band[:BLK] + carry_ref[...]).astype(BF16)
        carry_ref[...] = band[BLK:]
        dsk_ref[...] += jnp.broadcast_to(jnp.concatenate(dsk_rows, axis=1), (8, 2 * N_KV * 2 * BLK))

    cur = lambda n: jnp.minimum(n, nb - 1)
    prev = lambda n: jnp.maximum(n - 1, 0)
    return _call(
        body, rider, name="attn_bwd", grid=(nb + 1,),
        in_specs=[pl.BlockSpec((BLK, D), lambda n: (cur(n), 0)),
                  pl.BlockSpec((BLK, KVX), lambda n: (cur(n), 0)), pl.BlockSpec((BLK, KVX), lambda n: (prev(n), 0)),
                  pl.BlockSpec((BLK, D), lambda n: (cur(n), 0)), pl.BlockSpec((BLK, D), lambda n: (cur(n), 0)),
                  pl.BlockSpec(memory_space=pltpu.SMEM),
                  pl.BlockSpec((None, 2 * BLK, 2 * BLK), lambda n: (jnp.minimum(n, 1), 0, 0))],
        out_specs=[pl.BlockSpec((BLK, D), lambda n: (cur(n), 0)), pl.BlockSpec((BLK, D), lambda n: (prev(n), 0)),
                   pl.BlockSpec((BLK, D), lambda n: (cur(n), 0)), pl.BlockSpec((8, 2 * D), lambda n: (0, 0))],
        out_shape=[SDS((t, D), BF16), SDS((t, D), BF16), SDS((t, D), BF16), SDS((8, 2 * D), F32)],
        scratch_shapes=[pltpu.VMEM((BLK, D), F32)],
        args=(qs, kvx, kvx, u_za, d_o, sinks, bias), vmem=VMEM_BIG)


def _qkv_post(u_qkv, dqs, dkv, dza, qg_s, kg_t, rider):
    t = u_qkv.shape[0]
    tm = min(512, t)

    def norm_bwd(x, dy, g):
        r = lax.rsqrt(_head_sum(x * x) * (1.0 / HEAD) + EPS)
        xhat = x * r
        dxh = dy * g
        return r * (dxh - xhat * (_head_sum(dxh * xhat) * (1.0 / HEAD))), _fold8(dy * xhat)

    def body(u_ref, dq_ref, dkv_ref, dza_ref, qg_ref, kg_ref, du_ref, dqg_ref, dkg_ref):
        @pl.when(pl.program_id(0) == 0)
        def _():
            dqg_ref[...] = jnp.zeros_like(dqg_ref)
            dkg_ref[...] = jnp.zeros_like(dkg_ref)
        dq, gq = norm_bwd(u_ref[:, 0:D].astype(F32), dq_ref[...].astype(F32), qg_ref[...])
        fold = _fold_mat()
        dk, gk = norm_bwd(u_ref[:, D:D + 256].astype(F32), _dot(dkv_ref[:, 0:512], fold), kg_ref[...])
        du_ref[:, 0:D] = dq.astype(BF16)
        du_ref[:, D:2 * D] = dza_ref[...]
        du_ref[:, 2 * D:2 * D + 256] = dk.astype(BF16)
        du_ref[:, 2 * D + 256:2 * D + 512] = _dot(dkv_ref[:, 512:1024], fold).astype(BF16)
        dqg_ref[...] += gq
        dkg_ref[...] += gk

    row = pl.BlockSpec((tm, D), lambda i: (i, 0))
    return _call(
        body, rider, name="qkv_post", grid=(t // tm,),
        in_specs=[pl.BlockSpec((tm, 1536), lambda i: (i, 0)), row, row, row,
                  pl.BlockSpec((1, D), lambda i: (0, 0)), pl.BlockSpec((1, 256), lambda i: (0, 0))],
        out_specs=[pl.BlockSpec((tm, 2560), lambda i: (i, 0)), pl.BlockSpec((8, D), lambda i: (0, 0)),
                   pl.BlockSpec((8, 256), lambda i: (0, 0))],
        out_shape=[SDS((t, 2560), BF16), SDS((8, D), F32), SDS((8, 256), F32)],
        args=(u_qkv, dqs, dkv, dza, qg_s, kg_t), vmem=VMEM_BIG)


N_GRAN = N_IN // CB


def _du_granule(j):
    attn = jnp.where(j == 10, 4, jnp.where(j >= 11, j - 9, j - 8))
    return jnp.clip(j, 0, 7), jnp.clip(attn, 0, 4), jnp.clip(j - 13, 0, 3)


def _du_select(j, refs, fn):
    for ref, lo, hi in zip(refs, (0, 8, 13), (8, 13, 17)):
        @pl.when((j >= lo) & (j < hi))
        def _():
            fn(ref)


def _in_proj_bwd(du, w_full, rider):
    t = du[0].shape[0]
    tm = min(1024, t)

    def body(a0, a1, a2, b_ref, dh_ref, acc_ref):
        k = pl.program_id(1)

        @pl.when(k == 0)
        def _():
            acc_ref[...] = jnp.zeros_like(acc_ref)

        def add(a_ref):
            acc_ref[...] += _dot_nt(a_ref[...], b_ref[...])
        _du_select(k, (a0, a1, a2), add)

        @pl.when(k == N_GRAN - 1)
        def _():
            dh_ref[...] = acc_ref[...]

    seg = lambda q: pl.BlockSpec((tm, CB), lambda i, k: (i, _du_granule(k)[q]))
    (dh,), got = _call(
        body, rider, name="in_proj_bwd", grid=(t // tm, N_GRAN),
        in_specs=[seg(0), seg(1), seg(2), pl.BlockSpec((D, CB), lambda i, k: (0, k))],
        out_specs=[pl.BlockSpec((tm, D), lambda i, k: (i, 0))], out_shape=[SDS((t, D), F32)],
        scratch_shapes=[pltpu.VMEM((tm, D), F32)], args=(*du, w_full), vmem=VMEM_BIG)
    return dh, got


def _rmsnorm_bwd(dh, x, g, dout):
    t = x.shape[0]
    tm = min(256, t)

    def body(dh_ref, x_ref, g_ref, do_ref, dx_ref, dg_ref):
        @pl.when(pl.program_id(0) == 0)
        def _():
            dg_ref[...] = jnp.zeros_like(dg_ref)
        dh = dh_ref[...]
        xv = x_ref[...]
        r = lax.rsqrt(jnp.mean(xv * xv, axis=-1, keepdims=True) + EPS)
        xhat = xv * r
        dg_ref[...] += _fold8(dh * xhat)
        dxh = dh * g_ref[...]
        dx_ref[...] = do_ref[...] + r * (dxh - xhat * jnp.mean(dxh * xhat, axis=-1, keepdims=True))

    row = pl.BlockSpec((tm, D), lambda i: (i, 0))
    return pl.pallas_call(
        body, name="rmsnorm_bwd", grid=(t // tm,),
        in_specs=[row, row, pl.BlockSpec((1, D), lambda i: (0, 0)), row],
        out_specs=[row, pl.BlockSpec((8, D), lambda i: (0, 0))],
        out_shape=[SDS((t, D), F32), SDS((8, D), F32)], compiler_params=_cp("arbitrary"),
    )(dh, x, g, dout)


def _in_proj_wgrad(ht, du):
    t = ht.shape[1]
    tk = min(4096, t)
    n_k = t // tk

    def body(h_ref, b0, b1, b2, g_ref):
        j, k = pl.program_id(0), pl.program_id(1)

        if n_k > 1:
            @pl.when(k == 0)
            def _():
                g_ref[...] = jnp.zeros_like(g_ref)

        def add(b_ref):
            if n_k > 1:
                g_ref[...] += _dot(h_ref[...], b_ref[...])
            else:
                g_ref[...] = _dot(h_ref[...], b_ref[...])
        _du_select(j, (b0, b1, b2), add)

    seg = lambda q: pl.BlockSpec((tk, CB), lambda j, k: (k, _du_granule(j)[q]))
    return pl.pallas_call(
        body, name="in_proj_wgrad", grid=(N_GRAN, t // tk),
        in_specs=[pl.BlockSpec((D, tk), lambda j, k: (0, k)), seg(0), seg(1), seg(2)],
        out_specs=pl.BlockSpec((D, CB), lambda j, k: (0, j)), out_shape=SDS((D, N_IN), F32),
        compiler_params=_cp("parallel", "arbitrary", vmem=VMEM_BIG),
    )(ht, *du)


def _swap_rider(g_in, g_sm):
    def copies(ins, outs, send, recv):
        x, y, c = _mesh_pos()
        cps = []
        for src, dst in zip(ins, outs):
            half = src.at[1 - c] if len(src.shape) == 3 else src.at[:, :, 1 - c]
            cps.append(_rcopy(half, dst, send, recv, len(cps), (x, y, 1 - c)))
        return cps

    arrays = [g for g in (g_in, g_sm) if g is not None]
    shapes = [SDS((512, N_IN), F32) if len(g.shape) == 3 else SDS((3, 4, 128, D), F32) for g in arrays]
    return _Rider(arrays, shapes, len(arrays), copies)


def _add_halves_in(c_idx, g_in, r_in):
    def body(c_ref, a_ref, b_ref, f_ref, h_ref):
        s = a_ref[...] + b_ref[...]
        f_ref[...] = s
        h_ref[...] = s.astype(BF16)

    blk = pl.BlockSpec((128, N_IN), lambda i, c: (i, 0))
    return pl.pallas_call(
        body, name="add_halves_in",
        grid_spec=pltpu.PrefetchScalarGridSpec(
            num_scalar_prefetch=1, grid=(4,),
            in_specs=[pl.BlockSpec((None, 128, N_IN), lambda i, c: (c[0], i, 0)), blk], out_specs=[blk, blk]),
        out_shape=[SDS((512, N_IN), F32), SDS((512, N_IN), BF16)], compiler_params=_cp("parallel", vmem=VMEM_BIG),
    )(c_idx, g_in, r_in)


def _add_halves_sm(c_idx, g_sm, r_sm):
    def body(c_ref, a_ref, b_ref, f_ref, h_ref):
        s = a_ref[...] + b_ref[...]
        f_ref[...] = s
        h_ref[...] = s.astype(BF16)

    blk = pl.BlockSpec((1, 4, 128, D), lambda a, c: (a, 0, 0, 0))
    return pl.pallas_call(
        body, name="add_halves_sm",
        grid_spec=pltpu.PrefetchScalarGridSpec(
            num_scalar_prefetch=1, grid=(3,),
            in_specs=[pl.BlockSpec((1, 4, None, 128, D), lambda a, c: (a, 0, c[0], 0, 0)), blk], out_specs=[blk, blk]),
        out_shape=[SDS((3, 4, 128, D), F32), SDS((3, 4, 128, D), BF16)], compiler_params=_cp("parallel"),
    )(c_idx, g_sm, r_sm)


def _scatter_rider(h_in, h_sm):
    def copies(ins, outs, send, recv):
        x, y, c = _mesh_pos()
        cps = []
        for src, dst in zip(ins, outs):
            for k, chip in enumerate(_other_chips(x, y)):
                their = 2 * chip[0] + chip[1]
                part = src.at[:, pl.ds(pl.multiple_of(their * SH_IN, 128), SH_IN)] if len(src.shape) == 2 else src.at[:, their]
                cps.append(_rcopy(part, dst.at[k], send, recv, len(cps), (*chip, c)))
        return cps

    arrays = [h for h in (h_in, h_sm) if h is not None]
    shapes = [SDS((3, 512, SH_IN), BF16) if len(h.shape) == 2 else SDS((3, 3, 128, D), BF16) for h in arrays]
    return _Rider(arrays, shapes, 3 * len(arrays), copies)


def _ride_alone(rider, name):
    return _hosted_call(None, rider, name=name, grid=(), in_specs=[], out_specs=[], out_shape=[], args=())[1]


def _final_sum_in(chip_idx, f_in, r_in):
    def body(j_ref, a_ref, r_ref, o_ref):
        o_ref[...] = a_ref[...] + r_ref[0].astype(F32) + r_ref[1].astype(F32) + r_ref[2].astype(F32)

    return pl.pallas_call(
        body, name="final_sum_in",
        grid_spec=pltpu.PrefetchScalarGridSpec(
            num_scalar_prefetch=1, grid=(4,),
            in_specs=[pl.BlockSpec((128, SH_IN), lambda i, j: (i, j[0])), pl.BlockSpec((3, 128, SH_IN), lambda i, j: (0, i, 0))],
            out_specs=pl.BlockSpec((128, SH_IN), lambda i, j: (i, 0))),
        out_shape=SDS((512, SH_IN), F32), compiler_params=_cp("parallel"),
    )(chip_idx, f_in, r_in)


def _final_sum_sm(chip_idx, f_sm, r_sm):
    def body(j_ref, a_ref, r_ref, o_ref):
        o_ref[...] = a_ref[...] + r_ref[0].astype(F32) + r_ref[1].astype(F32) + r_ref[2].astype(F32)

    return pl.pallas_call(
        body, name="final_sum_sm",
        grid_spec=pltpu.PrefetchScalarGridSpec(
            num_scalar_prefetch=1, grid=(3,),
            in_specs=[pl.BlockSpec((1, None, 128, D), lambda a, j: (a, j[0], 0, 0)),
                      pl.BlockSpec((3, 1, 128, D), lambda a, j: (0, a, 0, 0))],
            out_specs=pl.BlockSpec((1, 128, D), lambda a, j: (a, 0, 0))),
        out_shape=SDS((3, 128, D), F32), compiler_params=_cp("parallel"),
    )(chip_idx, f_sm, r_sm)


def _join_halves(t_in, t_sm):
    n_cp = N_LAYERS * 4

    def body(*refs):
        ins, outs = refs[:2 * N_LAYERS], refs[2 * N_LAYERS:2 * N_LAYERS + 4]
        send, recv, loc_in, loc_out, stage_in, stage_sm = refs[2 * N_LAYERS + 4:]
        x, y, c = _mesh_pos()
        cps, own = [], []
        for l in range(N_LAYERS):
            for a in range(4):
                s = 4 * l + a
                if a == 0:
                    src = ins[2 * l]
                    dst = outs[0].at[l, pl.ds(pl.multiple_of(c * 512, 512), 512), :]
                else:
                    src = ins[2 * l + 1].at[a - 1]
                    dst = outs[a].at[l, pl.ds(pl.multiple_of(c * 128, 128), 128), :]
                own.append((src, dst, min(a, 1)))
                cp = pltpu.make_async_remote_copy(src_ref=src, dst_ref=dst, send_sem=send.at[s], recv_sem=recv.at[s],
                                                  device_id=(x, y, 1 - c), device_id_type=MESH)
                cp.start()
                cps.append(cp)
        _staged_copies(own, (stage_in, stage_sm), loc_in, loc_out)
        for l in range(N_LAYERS):
            for a in range(4):
                s = 4 * l + a
                if a == 0:
                    got = outs[0].at[l, pl.ds(pl.multiple_of((1 - c) * 512, 512), 512), :]
                else:
                    got = outs[a].at[l, pl.ds(pl.multiple_of((1 - c) * 128, 128), 128), :]
                pltpu.make_async_remote_copy(src_ref=got, dst_ref=got, send_sem=send.at[s], recv_sem=recv.at[s],
                                             device_id=(x, y, 1 - c), device_id_type=MESH).wait_recv()
        for cp in cps:
            cp.wait_send()

    args = []
    for l in range(N_LAYERS):
        args += [t_in[l], t_sm[l]]
    sm = SDS((N_LAYERS, SH_ROW, D), F32)
    return pl.pallas_call(
        body, name="join_halves", in_specs=[ANY] * (2 * N_LAYERS), out_specs=[ANY] * 4,
        out_shape=[SDS((N_LAYERS, D, SH_IN), F32), sm, sm, sm],
        scratch_shapes=[pltpu.SemaphoreType.DMA((n_cp,))] * 4
        + [pltpu.VMEM((2, 512, SH_IN), F32), pltpu.VMEM((2, 128, D), F32)],
        compiler_params=_cp(vmem=VMEM_BIG),
    )(*args)


def _adam_math(w, g, m, v):
    m = ADAM_B1 * m + (1.0 - ADAM_B1) * g
    v = ADAM_B2 * v + (1.0 - ADAM_B2) * (g * g)
    m_hat = m / (1.0 - ADAM_B1 ** ADAM_STEP)
    v_hat = v / (1.0 - ADAM_B2 ** ADAM_STEP)
    delta = -ADAM_LR * (m_hat / (jnp.sqrt(v_hat) + ADAM_EPS) + ADAM_WD * w)
    return delta, m, v


def _adamw_big(w, g, m, v, name):
    rows, cols = w.shape
    tr = 128

    def body(w_ref, g_ref, m_ref, v_ref, d_ref, nm_ref, nv_ref):
        d_ref[...], nm_ref[...], nv_ref[...] = _adam_math(w_ref[...], g_ref[...], m_ref[...], v_ref[...])

    blk = pl.BlockSpec((tr, cols), lambda i: (i, 0))
    return pl.pallas_call(
        body, name=name, grid=(rows // tr,), in_specs=[blk] * 4, out_specs=[blk] * 3,
        out_shape=[SDS((rows, cols), F32)] * 3, compiler_params=_cp("parallel", vmem=VMEM_BIG),
    )(w, g, m, v)


def _adamw_small(ws, gs, ms, vs):
    n = len(ws)

    def body(*refs):
        for k in range(n):
            w_ref, g_ref, m_ref, v_ref = (refs[q * n + k] for q in range(4))
            d, nm, nv = _adam_math(w_ref[...], g_ref[...], m_ref[...], v_ref[...])
            refs[4 * n + k][...] = d
            refs[5 * n + k][...] = nm
            refs[6 * n + k][...] = nv

    vm = pl.BlockSpec(memory_space=pltpu.VMEM)
    shapes = [SDS(w.shape, F32) for w in ws]
    res = pl.pallas_call(
        body, name="adamw_small", in_specs=[vm] * (4 * n), out_specs=[vm] * (3 * n), out_shape=shapes * 3,
    )(*ws, *gs, *ms, *vs)
    return res[:n], res[n:2 * n], res[2 * n:]


def _pad_rows(a, rows):
    flat = a.reshape(-1)
    return jnp.pad(flat, (0, rows * 128 - flat.shape[0])).reshape(rows, 128)


def kernel(x, norm_g, w_in, conv_w, q_norm_g, k_norm_g, sinks, w_conv_out, w_attn_out, gate_b, w_out, loss_target, m_norm_g, m_w_in, m_conv_w, m_q_norm_g, m_k_norm_g, m_sinks, m_w_conv_out, m_w_attn_out, m_gate_b, m_w_out, v_norm_g, v_w_in, v_conv_w, v_q_norm_g, v_k_norm_g, v_sinks, v_w_conv_out, v_w_attn_out, v_gate_b, v_w_out):
    xi, yi, ci = _mesh_pos()
    chip = 2 * xi + yi
    c_idx = jnp.reshape(ci, (1,)).astype(jnp.int32)
    chip_idx = jnp.reshape(chip, (1,)).astype(jnp.int32)
    t = x.shape[1]
    xs = [x.reshape(t, D)]
    tgt = loss_target.reshape(t, D)

    full_w = [[_cast_w_in(chip_idx, w_in, l), _cast_w_small(chip_idx, w_conv_out, w_attn_out, w_out, l)]
              for l in range(N_LAYERS)]
    full_w[0][0] = _ride_alone(_gather_rider(full_w[0][:1], "A"), "gather_first_ici")[0]
    full_w[0][0] = _ride_alone(_gather_rider(full_w[0][:1], "B"), "gather_first_d2d")[0]
    placed = lax.dynamic_update_slice(jnp.zeros((N_LAYERS, 3, D), F32),
                                      jnp.where(ci == 0, conv_w, 0.0), (0, 0, chip * SH_ROW))
    conv_full = _allreduce_small(placed.reshape(96, 128)).reshape(N_LAYERS, 3, D)

    qg_s = jnp.tile(q_norm_g, (1, N_Q)) * SCALE
    kg_t = jnp.tile(k_norm_g, (1, N_KV))
    bias = _band_bias()
    saved = []
    for l in range(N_LAYERS):
        nxt = full_w[l + 1] if l + 1 < N_LAYERS else None
        h, ht = _rmsnorm_fwd(xs[l], norm_g[l:l + 1])
        (u_conv, u_qkv, u_za, u_gl), got = _in_proj(h, full_w[l][0], _gather_rider(nxt[:1], "A") if nxt else None)
        if nxt:
            nxt[0] = got[0]
        y_c = _conv_fwd(u_conv, conv_full[l])
        qs, kvx = _qkv_prep(u_qkv, qg_s[l:l + 1], kg_t[l:l + 1])
        small_w = ([full_w[0][1]] if l == 0 else []) + ([nxt[1]] if nxt else [])
        o, got = _attn_fwd(qs, kvx, u_za, sinks[l:l + 1], bias, _gather_rider(small_w, "A") if small_w else None)
        if nxt:
            nxt[1] = got[-1]
        if l == 0:
            full_w[0][1] = _ride_alone(_gather_rider(got[:1], "B"), "gather_first_small_d2d")[0]
        (x_next, y_a, y_b, merged), got = _out_proj_fwd(xs[l], y_c, o, u_gl, gate_b[l:l + 1], full_w[l][1],
                                                        _gather_rider(nxt, "B") if nxt else None)
        if nxt:
            nxt[0], nxt[1] = got
        xs.append(x_next)
        saved.append((ht, u_conv, u_qkv, u_za, u_gl, y_c, o, y_a, y_b, merged, qs, kvx))

    dout, sq = _loss_head(xs[N_LAYERS], tgt)
    loss = lax.psum(jnp.sum(sq) * (0.5 / D), ("x", "y", "c"))

    small, t_in, t_sm = [None] * N_LAYERS, [None] * N_LAYERS, [None] * N_LAYERS
    grads = None
    halves = None

    def add_halves(g, got):
        f_in, h_in = _add_halves_in(c_idx, g[0], got[0])
        f_sm, h_sm = _add_halves_sm(c_idx, g[1], got[1])
        return f_in, h_in, f_sm, h_sm

    def final_sums(hv, parts):
        return _final_sum_in(chip_idx, hv[0], parts[0]), _final_sum_sm(chip_idx, hv[2], parts[1])

    for l in reversed(range(N_LAYERS)):
        w_full, w_sm = full_w[l]
        last = l == 0
        ht, u_conv, u_qkv, u_za, u_gl, y_c, o, y_a, y_b, merged, qs, kvx = saved[l]
        (d_ya, d_yb, du_gl, d_yc, d_o, dgb), got = _out_proj_bwd(dout, y_a, y_b, u_gl, gate_b[l:l + 1], w_sm,
                                                                 _swap_rider(*grads) if grads else None)
        if grads:
            halves = add_halves(grads, got)
        g_sm = _small_wgrads(y_c, d_ya, o, d_yb, merged, dout).reshape(3, 4, 2, 128, D)
        (du_conv, dcw), got = _conv_bwd(d_yc, u_conv, conv_full[l], _swap_rider(None, g_sm) if last else None)
        if last:
            f_sm0, h_sm0 = _add_halves_sm(c_idx, g_sm, got[0])
        (dqs, dkv, dza, dsk), got = _attn_bwd(d_o, qs, kvx, u_za, sinks[l:l + 1], bias,
                                              _scatter_rider(halves[1], halves[3]) if halves else None)
        if halves:
            t_in[l + 1], t_sm[l + 1] = final_sums(halves, got)
        dsk = jnp.sum(dsk[0].reshape(N_KV, 2, 2, BLK), axis=-1).transpose(0, 2, 1).reshape(N_Q)
        (du_attn, dqg, dkg), got = _qkv_post(u_qkv, dqs, dkv, dza, qg_s[l:l + 1], kg_t[l:l + 1],
                                             _scatter_rider(None, h_sm0) if last else None)
        if last:
            t_sm[0] = _final_sum_sm(chip_idx, f_sm0, got[0])
        du = (du_conv, du_attn, du_gl)
        grads = (_in_proj_wgrad(ht, du).reshape(2, 512, N_IN), g_sm)
        if last:
            f_in0, h_in0 = _add_halves_in(c_idx, grads[0], _ride_alone(_swap_rider(grads[0], None), "swap_last")[0])
        dh, got = _in_proj_bwd(du, w_full, _scatter_rider(h_in0, None) if last else None)
        if last:
            t_in[0] = _final_sum_in(chip_idx, f_in0, got[0])
        dout, dng = _rmsnorm_bwd(dh, xs[l], norm_g[l:l + 1], dout)
        small[l] = (jnp.sum(dng, axis=0), SCALE * jnp.sum(dqg.reshape(8 * N_Q, HEAD), axis=0),
                    jnp.sum(dkg.reshape(8 * N_KV, HEAD), axis=0), dsk, jnp.sum(dgb, axis=0), dcw[:3])
    grad_x = dout.reshape(1, t, D)

    stack = lambda k: jnp.stack([small[l][k] for l in range(N_LAYERS)])
    pack = jnp.concatenate([_pad_rows(stack(0), 32), _pad_rows(stack(1), 8), _pad_rows(stack(2), 8),
                            _pad_rows(stack(3), 8), _pad_rows(stack(4), 64), _pad_rows(stack(5), 96)], axis=0)
    red = _allreduce_small(pack)
    g_norm_g = red[0:32].reshape(N_LAYERS, D)
    g_q_norm_g = red[32:40].reshape(-1)[:N_LAYERS * HEAD].reshape(N_LAYERS, HEAD)
    g_k_norm_g = red[40:48].reshape(-1)[:N_LAYERS * HEAD].reshape(N_LAYERS, HEAD)
    g_sinks = red[48:56].reshape(-1)[:N_LAYERS * N_Q].reshape(N_LAYERS, N_Q)
    g_gate_b = red[56:120].reshape(N_LAYERS, 2 * D)
    g_conv_full = red[120:216].reshape(N_LAYERS, 3, D)
    g_conv_w = lax.dynamic_slice(g_conv_full, (0, 0, chip * SH_ROW), (N_LAYERS, 3, SH_ROW))

    g_w_in, g_w_co, g_w_ao, g_w_out = _join_halves(t_in, t_sm)

    r_in = N_LAYERS * D
    d_in, nm_in, nv_in = (a.reshape(N_LAYERS, D, SH_IN) for a in _adamw_big(
        w_in.reshape(r_in, SH_IN), g_w_in.reshape(r_in, SH_IN), m_w_in.reshape(r_in, SH_IN),
        v_w_in.reshape(r_in, SH_IN), "adamw_w_in"))
    r_sm = N_LAYERS * SH_ROW
    big = {}
    for nm, w, g, m, v in (("co", w_conv_out, g_w_co, m_w_conv_out, v_w_conv_out),
                           ("ao", w_attn_out, g_w_ao, m_w_attn_out, v_w_attn_out),
                           ("out", w_out, g_w_out, m_w_out, v_w_out)):
        big[nm] = tuple(a.reshape(N_LAYERS, SH_ROW, D) for a in _adamw_big(
            w.reshape(r_sm, D), g.reshape(r_sm, D), m.reshape(r_sm, D), v.reshape(r_sm, D), "adamw_w_small"))
    sm_w = [norm_g, conv_w, q_norm_g, k_norm_g, sinks, gate_b]
    sm_g = [g_norm_g, g_conv_w, g_q_norm_g, g_k_norm_g, g_sinks, g_gate_b]
    sm_m = [m_norm_g, m_conv_w, m_q_norm_g, m_k_norm_g, m_sinks, m_gate_b]
    sm_v = [v_norm_g, v_conv_w, v_q_norm_g, v_k_norm_g, v_sinks, v_gate_b]
    sd, snm, snv = _adamw_small(sm_w, sm_g, sm_m, sm_v)

    def order(norm, w_in_, conv, qn, kn, sk, co, ao, gb, wo):
        return [norm, w_in_, conv, qn, kn, sk, co, ao, gb, wo]

    grads = order(g_norm_g, g_w_in, g_conv_w, g_q_norm_g, g_k_norm_g, g_sinks, g_w_co, g_w_ao, g_gate_b, g_w_out)
    deltas = order(sd[0], d_in, sd[1], sd[2], sd[3], sd[4], big["co"][0], big["ao"][0], sd[5], big["out"][0])
    new_m = order(snm[0], nm_in, snm[1], snm[2], snm[3], snm[4], big["co"][1], big["ao"][1], snm[5], big["out"][1])
    new_v = order(snv[0], nv_in, snv[1], snv[2], snv[3], snv[4], big["co"][2], big["ao"][2], snv[5], big["out"][2])
    return (loss, grad_x, *grads, *deltas, *new_m, *new_v)
```

```python
import functools

import jax
import jax.numpy as jnp
from jax import lax
from jax.experimental import pallas as pl
from jax.experimental.pallas import tpu as pltpu

F32, BF16 = jnp.float32, jnp.bfloat16
SDS = jax.ShapeDtypeStruct
MESH = pl.DeviceIdType.MESH
ANY = pl.BlockSpec(memory_space=pl.ANY)

D = 1024
N_IN = 8704
N_LAYERS = 4
N_Q, N_KV, HEAD = 16, 4, 64
GROUP = N_Q // N_KV
BLK = 128
EPS = 1e-6
NEG = -1e30
SCALE = HEAD ** -0.5
SH_IN = N_IN // 4
SH_ROW = D // 4
CB = 512
SEG_CONV, SEG_Q, SEG_KV, SEG_ZA, SEG_GL = (0, 8), (8, 2), (10, 1), (11, 2), (13, 4)
VMEM_BIG = 56 * 1024 * 1024

ADAM_LR, ADAM_B1, ADAM_B2, ADAM_EPS, ADAM_WD, ADAM_STEP = 0.001, 0.9, 0.999, 1e-08, 0.01, 10


def _cp(*sem, vmem=None):
    return pltpu.CompilerParams(dimension_semantics=sem if sem else None, vmem_limit_bytes=vmem)


def _sigmoid(z):
    return 1.0 / (1.0 + jnp.exp(-z))


def _dot(a, b):
    return jnp.dot(a, b, preferred_element_type=F32)


def _dot_nt(a, b):
    return lax.dot_general(a, b, (((1,), (1,)), ((), ())), preferred_element_type=F32)


def _dot_tn(a, b):
    return lax.dot_general(a, b, (((0,), (0,)), ((), ())), preferred_element_type=F32)


def _rms(xh):
    r = lax.rsqrt(jnp.mean(xh * xh, axis=-1, keepdims=True) + EPS)
    return xh * r, r


def _fold8(v):
    return jnp.sum(v.reshape(v.shape[0] // 8, 8, v.shape[1]), axis=0)


def _cast_w_in(chip_idx, w, layer):
    def body(j_ref, i_ref, o_ref):
        o_ref[...] = i_ref[...].astype(BF16)

    return pl.pallas_call(
        body, name="cast_w_in",
        grid_spec=pltpu.PrefetchScalarGridSpec(
            num_scalar_prefetch=1, grid=(2,),
            in_specs=[pl.BlockSpec((None, 512, SH_IN), lambda i, j: (layer, i, 0))],
            out_specs=pl.BlockSpec((512, SH_IN), lambda i, j: (i, j[0]))),
        out_shape=SDS((D, N_IN), BF16), compiler_params=_cp("parallel"),
    )(chip_idx, w)


def _cast_w_small(chip_idx, a, b, c, layer):
    def body(j_ref, a_ref, b_ref, c_ref, o_ref):
        o_ref[0] = a_ref[...].astype(BF16)
        o_ref[1] = b_ref[...].astype(BF16)
        o_ref[2] = c_ref[...].astype(BF16)

    spec = pl.BlockSpec((None, SH_ROW, D), lambda i, j: (layer, 0, 0))
    return pl.pallas_call(
        body, name="cast_w_small",
        grid_spec=pltpu.PrefetchScalarGridSpec(
            num_scalar_prefetch=1, grid=(1,), in_specs=[spec, spec, spec],
            out_specs=pl.BlockSpec((3, SH_ROW, D), lambda i, j: (0, j[0], 0))),
        out_shape=SDS((3, D, D), BF16), compiler_params=_cp("parallel"),
    )(chip_idx, a, b, c)


def _mesh_pos():
    return lax.axis_index("x"), lax.axis_index("y"), lax.axis_index("c")


def _other_chips(x, y):
    return [(1 - x, y), (x, 1 - y), (1 - x, 1 - y)]


class _Rider:
    def __init__(self, ins, out_shape, n, copies, aliases=()):
        self.ins, self.out_shape, self.n, self.copies, self.aliases = list(ins), list(out_shape), n, copies, aliases


def _rcopy(src, dst, send, recv, k, to):
    return pltpu.make_async_remote_copy(src_ref=src, dst_ref=dst, send_sem=send.at[k], recv_sem=recv.at[k],
                                        device_id=to, device_id_type=MESH)


def _hosted_call(body, rider, *, name, grid, in_specs, out_specs, out_shape, args, scratch_shapes=(), vmem=None):
    n_in, n_out, n_scr = len(in_specs), len(out_specs), len(scratch_shapes)
    r_in, r_out = len(rider.ins), len(rider.out_shape)

    def full_body(*refs):
        host_in, rid_in = refs[:n_in], refs[n_in:n_in + r_in]
        o0 = n_in + r_in
        host_out, rid_out = refs[o0:o0 + n_out], refs[o0 + n_out:o0 + n_out + r_out]
        s0 = o0 + n_out + r_out
        host_scr, (send, recv) = refs[s0:s0 + n_scr], refs[s0 + n_scr:]
        if body is None:
            cps = rider.copies(rid_in, rid_out, send, recv)
            for cp in cps:
                cp.start()
            for cp in cps:
                cp.wait()
            return
        ids = [pl.program_id(a) for a in range(len(grid))]
        first = functools.reduce(lambda p, q: p & q, [i == 0 for i in ids])
        last = functools.reduce(lambda p, q: p & q, [i == g - 1 for i, g in zip(ids, grid)])

        @pl.when(first)
        def _():
            for cp in rider.copies(rid_in, rid_out, send, recv):
                cp.start()

        body(*host_in, *host_out, *host_scr)

        @pl.when(last)
        def _():
            for cp in rider.copies(rid_in, rid_out, send, recv):
                cp.wait()

    res = pl.pallas_call(
        full_body, name=name, grid=grid if body is not None else (),
        in_specs=list(in_specs) + [ANY] * r_in, out_specs=list(out_specs) + [ANY] * r_out,
        out_shape=list(out_shape) + rider.out_shape,
        scratch_shapes=list(scratch_shapes) + [pltpu.SemaphoreType.DMA((rider.n,))] * 2,
        input_output_aliases={n_in + i: n_out + o for i, o in rider.aliases},
        compiler_params=_cp(*(("arbitrary",) * len(grid) if body is not None else ()), vmem=vmem),
    )(*args, *rider.ins)
    return res[:n_out], res[n_out:]


def _call(body, rider, **kw):
    if rider is not None:
        return _hosted_call(body, rider, **kw)
    res = pl.pallas_call(
        body, name=kw["name"], grid=kw["grid"], in_specs=list(kw["in_specs"]), out_specs=list(kw["out_specs"]),
        out_shape=list(kw["out_shape"]), scratch_shapes=list(kw.get("scratch_shapes", ())),
        compiler_params=_cp(*(("arbitrary",) * len(kw["grid"])), vmem=kw.get("vmem")),
    )(*kw["args"])
    return res, []


def _gather_rider(arrays, stage):
    def copies(ins, outs, send, recv):
        x, y, c = _mesh_pos()
        cps = []
        for full in outs:
            for k, chip in enumerate(_other_chips(x, y)):
                whose = 2 * x + y if stage == "A" else 2 * chip[0] + chip[1]
                if len(full.shape) == 2:
                    reg = full.at[pl.ds(pl.multiple_of(c * 512, 512), 512), pl.ds(pl.multiple_of(whose * SH_IN, 128), SH_IN)]
                else:
                    reg = full.at[:, pl.ds(pl.multiple_of(whose * SH_ROW + c * 128, 128), 128), :]
                to = (*chip, c) if stage == "A" else (x, y, 1 - c)
                cps.append(_rcopy(reg, reg, send, recv, len(cps), to))
        return cps

    return _Rider(arrays, [SDS(v.shape, v.dtype) for v in arrays], 3 * len(arrays), copies,
                  aliases=tuple((i, i) for i in range(len(arrays))))


def _staged_copies(copies, stages, sem_in, sem_out):
    busy, count = {}, {}
    for idx, (src, dst, kind) in enumerate(copies):
        slot = count.get(kind, 0) % 2
        count[kind] = count.get(kind, 0) + 1
        if (kind, slot) in busy:
            busy.pop((kind, slot)).wait()
        buf = stages[kind].at[slot]
        cin = pltpu.make_async_copy(src, buf, sem_in.at[idx])
        cin.start()
        cin.wait()
        cout = pltpu.make_async_copy(buf, dst, sem_out.at[idx])
        cout.start()
        busy[(kind, slot)] = cout
    for cp in busy.values():
        cp.wait()


def _allreduce_small(pack):
    rows = pack.shape[0]

    def body(p_ref, o_ref, buf, send, recv):
        x, y, c = _mesh_pos()
        me = 4 * x + 2 * y + c
        sends = []
        for r in range(1, 8):
            to = (x if not (r & 4) else 1 - x, y if not (r & 2) else 1 - y, c if not (r & 1) else 1 - c)
            cp = pltpu.make_async_remote_copy(src_ref=p_ref, dst_ref=buf.at[me], send_sem=send.at[r - 1],
                                              recv_sem=recv.at[r - 1], device_id=to, device_id_type=MESH)
            cp.start()
            sends.append(cp)
        buf[me] = p_ref[...]
        for r in range(1, 8):
            frm = (4 * x + 2 * y + c) ^ r
            pltpu.make_async_remote_copy(src_ref=p_ref, dst_ref=buf.at[frm], send_sem=send.at[r - 1],
                                         recv_sem=recv.at[r - 1], device_id=(x, y, c), device_id_type=MESH).wait_recv()
        acc = buf[0]
        for d in range(1, 8):
            acc = acc + buf[d]
        o_ref[...] = acc
        for cp in sends:
            cp.wait_send()

    vm = pl.BlockSpec(memory_space=pltpu.VMEM)
    return pl.pallas_call(
        body, name="allreduce_small", in_specs=[vm], out_specs=vm, out_shape=SDS((rows, 128), F32),
        scratch_shapes=[pltpu.VMEM((8, rows, 128), F32), pltpu.SemaphoreType.DMA((7,)), pltpu.SemaphoreType.DMA((7,))],
    )(pack)


def _rmsnorm_fwd(x, g):
    t = x.shape[0]
    tm = min(512, t)

    def body(x_ref, g_ref, h_ref, ht_ref):
        xv = x_ref[...]
        r = lax.rsqrt(jnp.mean(xv * xv, axis=-1, keepdims=True) + EPS)
        h = xv * r * g_ref[...]
        h_ref[...] = h.astype(BF16)
        ht_ref[...] = h.T.astype(BF16)

    return pl.pallas_call(
        body, name="rmsnorm_fwd", grid=(t // tm,),
        in_specs=[pl.BlockSpec((tm, D), lambda i: (i, 0)), pl.BlockSpec((1, D), lambda i: (0, 0))],
        out_specs=[pl.BlockSpec((tm, D), lambda i: (i, 0)), pl.BlockSpec((D, tm), lambda i: (0, i))],
        out_shape=[SDS((t, D), BF16), SDS((D, t), BF16)],
        compiler_params=_cp("parallel", vmem=VMEM_BIG),
    )(x, g)


FWD_SEGS = ((0, 8), (8, 3), (11, 2), (13, 4))


def _in_proj(h, w_full, rider):
    t = h.shape[0]
    tm = min(2048, t)

    def body(a_ref, b_ref, *o_refs):
        j = pl.program_id(1)
        for o_ref, (off, nblk) in zip(o_refs, FWD_SEGS):
            @pl.when((j >= off) & (j < off + nblk))
            def _():
                o_ref[...] = _dot(a_ref[...], b_ref[...]).astype(BF16)

    def out(seg):
        off, nblk = seg
        return pl.BlockSpec((tm, CB), lambda i, j: (i, jnp.clip(j - off, 0, nblk - 1)))

    res, got = _call(
        body, rider, name="in_proj", grid=(t // tm, N_IN // CB),
        in_specs=[pl.BlockSpec((tm, D), lambda i, j: (i, 0)), pl.BlockSpec((D, CB), lambda i, j: (0, j))],
        out_specs=[out(s) for s in FWD_SEGS], out_shape=[SDS((t, s[1] * CB), BF16) for s in FWD_SEGS],
        args=(h, w_full), vmem=VMEM_BIG)
    return res, got


def _conv_fwd(u_conv, conv_w):
    t = u_conv.shape[0]
    tm = min(256, t)
    hb = tm // 16

    def body(v_ref, b_ref, c_ref, z_ref, hv_ref, hc_ref, w_ref, y_ref):
        i = pl.program_id(0)
        cv = c_ref[...].astype(F32) * v_ref[...].astype(F32)
        halo = hc_ref[...].astype(F32) * hv_ref[...].astype(F32)
        halo = jnp.where(i > 0, halo, 0.0)
        row = lax.broadcasted_iota(jnp.int32, (tm, 1), 0)
        s1 = jnp.where(row == 0, halo[15:16], pltpu.roll(cv, 1, 0))
        s2 = jnp.where(row == 0, halo[14:15], jnp.where(row == 1, halo[15:16], pltpu.roll(cv, 2, 0)))
        conv = w_ref[0:1, :] * s2 + w_ref[1:2, :] * s1 + w_ref[2:3, :] * cv
        z = z_ref[...].astype(F32)
        y_ref[...] = (b_ref[...].astype(F32) * conv * (z * _sigmoid(z))).astype(BF16)

    def col(k):
        return pl.BlockSpec((tm, D), lambda i: (i, k))

    def halo(k):
        return pl.BlockSpec((16, D), lambda i: (jnp.maximum(i * hb - 1, 0), k))

    return pl.pallas_call(
        body, name="conv_fwd", grid=(t // tm,),
        in_specs=[col(0), col(1), col(2), col(3), halo(0), halo(2), pl.BlockSpec((3, D), lambda i: (0, 0))],
        out_specs=pl.BlockSpec((tm, D), lambda i: (i, 0)), out_shape=SDS((t, D), BF16),
        compiler_params=_cp("parallel", vmem=VMEM_BIG),
    )(u_conv, u_conv, u_conv, u_conv, u_conv, u_conv, conv_w)


KVX = 4 * N_KV * 128


def _iota2(shape):
    return lax.broadcasted_iota(jnp.int32, shape, 0), lax.broadcasted_iota(jnp.int32, shape, 1)


def _head_sum(v):
    r, c = _iota2((128, 128))
    ones = ((r >> 6) == (c >> 6)).astype(BF16)
    hi = v.astype(BF16)
    lo = (v - hi.astype(F32)).astype(BF16)
    return jnp.concatenate([_dot(hi[:, g:g + 128], ones) + _dot(lo[:, g:g + 128], ones)
                            for g in range(0, v.shape[1], 128)], axis=1)


def _expand_mats():
    r, c = _iota2((N_KV * HEAD, N_KV * 128))
    base = ((r >> 6) << 7) + (r & 63)
    return (c == base).astype(BF16), (c == base + 64).astype(BF16)


def _fold_mat():
    r, c = _iota2((N_KV * 128, N_KV * HEAD))
    return (((r >> 7) == (c >> 6)) & ((r & 63) == (c & 63))).astype(BF16)


def _qkv_prep(u_qkv, qg_s, kg_t):
    t = u_qkv.shape[0]
    tm = min(512, t)

    def body(u_ref, qg_ref, kg_ref, qs_ref, kvx_ref):
        q = u_ref[:, 0:D].astype(F32)
        rq = lax.rsqrt(_head_sum(q * q) * (1.0 / HEAD) + EPS)
        qs_ref[...] = (q * rq * qg_ref[...]).astype(BF16)
        k = u_ref[:, D:D + 256].astype(F32)
        rk = lax.rsqrt(_head_sum(k * k) * (1.0 / HEAD) + EPS)
        kn = (k * rk * kg_ref[...]).astype(BF16)
        v = u_ref[:, D + 256:D + 512]
        e_lo, e_hi = _expand_mats()
        kvx_ref[:, 0:512] = _dot(kn, e_lo).astype(BF16)
        kvx_ref[:, 512:1024] = _dot(kn, e_hi).astype(BF16)
        kvx_ref[:, 1024:1536] = _dot(v, e_lo).astype(BF16)
        kvx_ref[:, 1536:2048] = _dot(v, e_hi).astype(BF16)

    return pl.pallas_call(
        body, name="qkv_prep", grid=(t // tm,),
        in_specs=[pl.BlockSpec((tm, 1536), lambda i: (i, 0)), pl.BlockSpec((1, D), lambda i: (0, 0)),
                  pl.BlockSpec((1, 256), lambda i: (0, 0))],
        out_specs=[pl.BlockSpec((tm, D), lambda i: (i, 0)), pl.BlockSpec((tm, KVX), lambda i: (i, 0))],
        out_shape=[SDS((t, D), BF16), SDS((t, KVX), BF16)], compiler_params=_cp("parallel", vmem=VMEM_BIG),
    )(u_qkv, qg_s, kg_t)


def _band_bias():
    j, r = _iota2((2 * BLK, 2 * BLK))
    diff = (r & (BLK - 1)) - j + BLK
    band = (diff >= 0) & (diff < BLK)
    return jnp.stack([jnp.where(band & (j >= BLK), 0.0, NEG), jnp.where(band, 0.0, NEG)]).astype(F32)


def _pair_rows(ref_or_val, hk):
    return jnp.concatenate([ref_or_val[:, 256 * hk:256 * hk + 128], ref_or_val[:, 256 * hk + 128:256 * hk + 256]], axis=0)


def _sink_row(sink_ref, hk, half):
    return jnp.concatenate([jnp.full((1, BLK), sink_ref[0, GROUP * hk + half], F32),
                            jnp.full((1, BLK), sink_ref[0, GROUP * hk + 2 + half], F32)], axis=1)


def _softmax_half(qpp, kvb, hk, half, bias, sink_ref):
    kx = kvb[:, 512 * half + 128 * hk:512 * half + 128 * hk + 128]
    vx = kvb[:, 1024 + 512 * half + 128 * hk:1024 + 512 * half + 128 * hk + 128]
    s = _dot_nt(kx, qpp) + bias
    sink = _sink_row(sink_ref, hk, half)
    m = jnp.maximum(jnp.max(s, axis=0, keepdims=True), sink)
    p = jnp.exp(s - m)
    es = jnp.exp(sink - m)
    inv = 1.0 / (jnp.sum(p, axis=0, keepdims=True) + es)
    return p * inv, es * inv, kx, vx


def _attn_fwd(qs, kvx, u_za, sinks, bias, rider):
    t = qs.shape[0]
    nb = t // BLK

    def body(q_ref, kc_ref, kp_ref, za_ref, sink_ref, bias_ref, o_ref):
        kvb = jnp.concatenate([kp_ref[...], kc_ref[...]], axis=0)
        bias_v = bias_ref[...]
        cols = []
        for hk in range(N_KV):
            qpp = _pair_rows(q_ref, hk)
            lo = _softmax_half(qpp, kvb, hk, 0, bias_v, sink_ref)
            hi = _softmax_half(qpp, kvb, hk, 1, bias_v, sink_ref)
            opp = _dot_tn(jnp.concatenate([lo[0], hi[0]], axis=0).astype(BF16), jnp.concatenate([lo[3], hi[3]], axis=0))
            cols += [opp[:BLK], opp[BLK:]]
        za = za_ref[...].astype(F32)
        o_ref[...] = (jnp.concatenate(cols, axis=1) * (za * _sigmoid(za))).astype(BF16)

    prev = lambda n: jnp.maximum(n - 1, 0)
    (o,), got = _call(
        body, rider, name="attn_fwd", grid=(nb,),
        in_specs=[pl.BlockSpec((BLK, D), lambda n: (n, 0)),
                  pl.BlockSpec((BLK, KVX), lambda n: (n, 0)), pl.BlockSpec((BLK, KVX), lambda n: (prev(n), 0)),
                  pl.BlockSpec((BLK, D), lambda n: (n, 0)), pl.BlockSpec(memory_space=pltpu.SMEM),
                  pl.BlockSpec((None, 2 * BLK, 2 * BLK), lambda n: (jnp.minimum(n, 1), 0, 0))],
        out_specs=[pl.BlockSpec((BLK, D), lambda n: (n, 0))], out_shape=[SDS((t, D), BF16)],
        args=(qs, kvx, kvx, u_za, sinks, bias), vmem=VMEM_BIG)
    return o, got


def _out_proj_fwd(x, y_c, o, u_gl, gate_b, w_sm, rider):
    t = x.shape[0]
    tm = min(512, t)

    def body(x_ref, yc_ref, o_ref, gla_ref, glb_ref, gb_ref, wco_ref, wao_ref, wout_ref,
             xn_ref, ya_ref, yb_ref, mg_ref):
        ya = _dot(yc_ref[...], wco_ref[...])
        yb = _dot(o_ref[...], wao_ref[...])
        gb = gb_ref[...]
        ga_ = _sigmoid(gla_ref[...].astype(F32) + gb[:, :D])
        gb_ = _sigmoid(glb_ref[...].astype(F32) + gb[:, D:])
        merged = (ga_ * ya + gb_ * yb).astype(BF16)
        ya_ref[...] = ya.astype(BF16)
        yb_ref[...] = yb.astype(BF16)
        mg_ref[...] = merged
        xn_ref[...] = x_ref[...] + _dot(merged, wout_ref[...])

    row = pl.BlockSpec((tm, D), lambda i: (i, 0))
    wspec = lambda a: pl.BlockSpec((None, D, D), lambda i: (a, 0, 0))
    return _call(
        body, rider, name="out_proj_fwd", grid=(t // tm,),
        in_specs=[row, row, row, pl.BlockSpec((tm, D), lambda i: (i, 0)), pl.BlockSpec((tm, D), lambda i: (i, 1)),
                  pl.BlockSpec((1, 2 * D), lambda i: (0, 0)), wspec(0), wspec(1), wspec(2)],
        out_specs=[row, row, row, row],
        out_shape=[SDS((t, D), F32), SDS((t, D), BF16), SDS((t, D), BF16), SDS((t, D), BF16)],
        args=(x, y_c, o, u_gl, u_gl, gate_b, w_sm, w_sm, w_sm), vmem=VMEM_BIG)


def _loss_head(y, tgt):
    t = y.shape[0]
    tm = min(512, t)

    def body(y_ref, t_ref, dy_ref, acc_ref):
        @pl.when(pl.program_id(0) == 0)
        def _():
            acc_ref[...] = jnp.zeros_like(acc_ref)
        err = y_ref[...] - t_ref[...]
        dy_ref[...] = err * (1.0 / D)
        sq = _fold8(err * err)
        tot = sq[:, 0:128]
        for k in range(1, D // 128):
            tot = tot + sq[:, 128 * k:128 * (k + 1)]
        acc_ref[...] += tot

    row = pl.BlockSpec((tm, D), lambda i: (i, 0))
    return pl.pallas_call(
        body, name="loss_head", grid=(t // tm,), in_specs=[row, row],
        out_specs=[row, pl.BlockSpec((8, 128), lambda i: (0, 0))],
        out_shape=[SDS((t, D), F32), SDS((8, 128), F32)], compiler_params=_cp("arbitrary"),
    )(y, tgt)


def _out_proj_bwd(dout, y_a, y_b, u_gl, gate_b, w_sm, rider):
    t = dout.shape[0]
    tm = min(512, t)

    def body(do_ref, ya_ref, yb_ref, gla_ref, glb_ref, gb_ref, wco_ref, wao_ref, wout_ref,
             dya_ref, dyb_ref, dgl_ref, dyc_ref, dob_ref, dgb_ref):
        @pl.when(pl.program_id(0) == 0)
        def _():
            dgb_ref[...] = jnp.zeros_like(dgb_ref)
        dm = _dot_nt(do_ref[...].astype(BF16), wout_ref[...])
        gb = gb_ref[...]
        ga_ = _sigmoid(gla_ref[...].astype(F32) + gb[:, :D])
        gb_ = _sigmoid(glb_ref[...].astype(F32) + gb[:, D:])
        dya = (ga_ * dm).astype(BF16)
        dyb = (gb_ * dm).astype(BF16)
        dgla = ya_ref[...].astype(F32) * dm * (ga_ * (1.0 - ga_))
        dglb = yb_ref[...].astype(F32) * dm * (gb_ * (1.0 - gb_))
        dya_ref[...] = dya
        dyb_ref[...] = dyb
        dgl_ref[:, :D] = dgla.astype(BF16)
        dgl_ref[:, D:] = dglb.astype(BF16)
        dgb_ref[:, :D] += _fold8(dgla)
        dgb_ref[:, D:] += _fold8(dglb)
        dyc_ref[...] = _dot_nt(dya, wco_ref[...]).astype(BF16)
        dob_ref[...] = _dot_nt(dyb, wao_ref[...]).astype(BF16)

    row = pl.BlockSpec((tm, D), lambda i: (i, 0))
    wspec = lambda a: pl.BlockSpec((None, D, D), lambda i: (a, 0, 0))
    return _call(
        body, rider, name="out_proj_bwd", grid=(t // tm,),
        in_specs=[row, row, row, pl.BlockSpec((tm, D), lambda i: (i, 0)), pl.BlockSpec((tm, D), lambda i: (i, 1)),
                  pl.BlockSpec((1, 2 * D), lambda i: (0, 0)), wspec(0), wspec(1), wspec(2)],
        out_specs=[row, row, pl.BlockSpec((tm, 2 * D), lambda i: (i, 0)), row, row,
                   pl.BlockSpec((8, 2 * D), lambda i: (0, 0))],
        out_shape=[SDS((t, D), BF16), SDS((t, D), BF16), SDS((t, 2 * D), BF16), SDS((t, D), BF16), SDS((t, D), BF16),
                   SDS((8, 2 * D), F32)],
        args=(dout, y_a, y_b, u_gl, u_gl, gate_b, w_sm, w_sm, w_sm), vmem=VMEM_BIG)


def _small_wgrads(y_c, d_ya, o, d_yb, merged, dout):
    t = y_c.shape[0]
    tk = min(512, t)

    def body(yc_ref, dya_ref, o_ref, dyb_ref, mg_ref, do_ref, g_ref):
        @pl.when(pl.program_id(0) == 0)
        def _():
            g_ref[...] = jnp.zeros_like(g_ref)
        g_ref[0] += _dot_tn(yc_ref[...], dya_ref[...])
        g_ref[1] += _dot_tn(o_ref[...], dyb_ref[...])
        g_ref[2] += _dot_tn(mg_ref[...], do_ref[...].astype(BF16))

    row = pl.BlockSpec((tk, D), lambda k: (k, 0))
    return pl.pallas_call(
        body, name="small_wgrads", grid=(t // tk,), in_specs=[row] * 6,
        out_specs=pl.BlockSpec((3, D, D), lambda k: (0, 0, 0)), out_shape=SDS((3, D, D), F32),
        compiler_params=_cp("arbitrary", vmem=VMEM_BIG),
    )(y_c, d_ya, o, d_yb, merged, dout)


def _conv_bwd(d_yc, u_conv, conv_w, rider):
    t = d_yc.shape[0]
    tm = min(256, t)
    hb = tm // 16
    last_halo = t // 16 - 1
    n_steps = t // tm

    def body(dy_ref, v_ref, b_ref, c_ref, z_ref, hv_ref, hc_ref, ndy_ref, nb_ref, nz_ref, w_ref, du_ref, dw_ref):
        i = pl.program_id(0)

        @pl.when(i == 0)
        def _():
            dw_ref[...] = jnp.zeros_like(dw_ref)
        v, c = v_ref[...].astype(F32), c_ref[...].astype(F32)
        b, z = b_ref[...].astype(F32), z_ref[...].astype(F32)
        cv = c * v
        halo = jnp.where(i > 0, hc_ref[...].astype(F32) * hv_ref[...].astype(F32), 0.0)
        row = lax.broadcasted_iota(jnp.int32, (tm, 1), 0)
        s1 = jnp.where(row == 0, halo[15:16], pltpu.roll(cv, 1, 0))
        s2 = jnp.where(row == 0, halo[14:15], jnp.where(row == 1, halo[15:16], pltpu.roll(cv, 2, 0)))
        w0, w1, w2 = w_ref[0:1, :], w_ref[1:2, :], w_ref[2:3, :]
        conv = w0 * s2 + w1 * s1 + w2 * cv
        sig = _sigmoid(z)
        sz = z * sig
        dsz = sig * (1.0 + z * (1.0 - sig))
        dy = dy_ref[...].astype(F32)
        dconv = dy * b * sz
        nz = nz_ref[...].astype(F32)
        nxt = ndy_ref[...].astype(F32) * nb_ref[...].astype(F32) * (nz * _sigmoid(nz))
        nxt = jnp.where(i < n_steps - 1, nxt, 0.0)
        a1 = jnp.where(row == tm - 1, nxt[0:1], pltpu.roll(dconv, tm - 1, 0))
        a2 = jnp.where(row == tm - 2, nxt[0:1], jnp.where(row == tm - 1, nxt[1:2], pltpu.roll(dconv, tm - 2, 0)))
        dcv = w2 * dconv + w1 * a1 + w0 * a2
        du_ref[:, 0:D] = (dcv * c).astype(BF16)
        du_ref[:, D:2 * D] = (dy * conv * sz).astype(BF16)
        du_ref[:, 2 * D:3 * D] = (dcv * v).astype(BF16)
        du_ref[:, 3 * D:4 * D] = (dy * b * conv * dsz).astype(BF16)
        r8 = lax.broadcasted_iota(jnp.int32, (8, 1), 0)
        dw_ref[...] += jnp.where(r8 == 0, jnp.sum(dconv * s2, axis=0, keepdims=True),
                                 jnp.where(r8 == 1, jnp.sum(dconv * s1, axis=0, keepdims=True),
                                           jnp.where(r8 == 2, jnp.sum(dconv * cv, axis=0, keepdims=True), 0.0)))

    def col(k):
        return pl.BlockSpec((tm, D), lambda i: (i, k))

    def halo(k):
        return pl.BlockSpec((16, D), lambda i: (jnp.maximum(i * hb - 1, 0), k))

    def nxt(k):
        return pl.BlockSpec((16, D), lambda i: (jnp.minimum((i + 1) * hb, last_halo), k))

    return _call(
        body, rider, name="conv_bwd", grid=(t // tm,),
        in_specs=[col(0), col(0), col(1), col(2), col(3), halo(0), halo(2), nxt(0), nxt(1), nxt(3),
                  pl.BlockSpec((3, D), lambda i: (0, 0))],
        out_specs=[pl.BlockSpec((tm, 4 * D), lambda i: (i, 0)), pl.BlockSpec((8, D), lambda i: (0, 0))],
        out_shape=[SDS((t, 4 * D), BF16), SDS((8, D), F32)],
        args=(d_yc, u_conv, u_conv, u_conv, u_conv, u_conv, u_conv, d_yc, u_conv, u_conv, conv_w), vmem=VMEM_BIG)


def _attn_bwd(d_o, qs, kvx, u_za, sinks, bias, rider):
    t = d_o.shape[0]
    nb = t // BLK

    def body(q_ref, kc_ref, kp_ref, za_ref, do_ref, sink_ref, bias_ref, dq_ref, dkv_ref, dza_ref, dsk_ref, carry_ref):
        n = pl.program_id(0)

        @pl.when(n == 0)
        def _():
            carry_ref[...] = jnp.zeros_like(carry_ref)
            dsk_ref[...] = jnp.zeros_like(dsk_ref)

        live = n < nb
        kvb = jnp.concatenate([kp_ref[...], kc_ref[...]], axis=0)
        bias_v = bias_ref[...]
        za = za_ref[...].astype(F32)
        sig = _sigmoid(za)
        dsa = sig * (1.0 + za * (1.0 - sig))
        do = jnp.where(live, do_ref[...].astype(F32), 0.0)
        dattn = (do * (za * sig)).astype(BF16)
        lo_lanes = lax.broadcasted_iota(jnp.int32, (1, 128), 1) < HEAD
        dq_cols, attn_cols, dk_cols, dv_cols, dsk_rows = [], [], [], [], []
        for hk in range(N_KV):
            qpp = _pair_rows(q_ref, hk)
            dapp = _pair_rows(dattn, hk)
            probs, dss, xk, xv = [], [], [], []
            for half in range(2):
                prob, psink, kx, vx = _softmax_half(qpp, kvb, hk, half, bias_v, sink_ref)
                dp = _dot_nt(vx, dapp)
                drow = jnp.sum(prob * dp, axis=0, keepdims=True)
                ds = (prob * (dp - drow)).astype(BF16)
                prob_b = prob.astype(BF16)
                xk.append(_dot(ds, qpp))
                xv.append(_dot(prob_b, dapp))
                probs.append(prob_b)
                dss.append(ds)
                dsk_rows.append(-psink * drow)
            kcat = jnp.concatenate([kvb[:, 128 * hk:128 * hk + 128], kvb[:, 512 + 128 * hk:512 + 128 * hk + 128]], axis=0)
            vcat = jnp.concatenate([kvb[:, 1024 + 128 * hk:1024 + 128 * hk + 128],
                                    kvb[:, 1536 + 128 * hk:1536 + 128 * hk + 128]], axis=0)
            app = _dot_tn(jnp.concatenate(probs, axis=0), vcat)
            dqpp = _dot_tn(jnp.concatenate(dss, axis=0), kcat)
            dq_cols += [dqpp[:BLK], dqpp[BLK:]]
            attn_cols += [app[:BLK], app[BLK:]]
            dk_cols.append(jnp.where(lo_lanes, xk[0], xk[1]))
            dv_cols.append(jnp.where(lo_lanes, xv[0], xv[1]))

        @pl.when(live)
        def _():
            dq_ref[...] = jnp.concatenate(dq_cols, axis=1).astype(BF16)
            dza_ref[...] = (do * jnp.concatenate(attn_cols, axis=1) * dsa).astype(BF16)

        band = jnp.concatenate(dk_cols + dv_cols, axis=1)
        dkv_ref[...] = (band[:BLK] + carry_ref[...]).astype(BF16)
        carry_ref[...] = band[BLK:]
        dsk_ref[...] += jnp.broadcast_to(jnp.concatenate(dsk_rows, axis=1), (8, 2 * N_KV * 2 * BLK))

    cur = lambda n: jnp.minimum(n, nb - 1)
    prev = lambda n: jnp.maximum(n - 1, 0)
    return _call(
        body, rider, name="attn_bwd", grid=(nb + 1,),
        in_specs=[pl.BlockSpec((BLK, D), lambda n: (cur(n), 0)),
                  pl.BlockSpec((BLK, KVX), lambda n: (cur(n), 0)), pl.BlockSpec((BLK, KVX), lambda n: (prev(n), 0)),
                  pl.BlockSpec((BLK, D), lambda n: (cur(n), 0)), pl.BlockSpec((BLK, D), lambda n: (cur(n), 0)),
                  pl.BlockSpec(memory_space=pltpu.SMEM),
                  pl.BlockSpec((None, 2 * BLK, 2 * BLK), lambda n: (jnp.minimum(n, 1), 0, 0))],
        out_specs=[pl.BlockSpec((BLK, D), lambda n: (cur(n), 0)), pl.BlockSpec((BLK, D), lambda n: (prev(n), 0)),
                   pl.BlockSpec((BLK, D), lambda n: (cur(n), 0)), pl.BlockSpec((8, 2 * D), lambda n: (0, 0))],
        out_shape=[SDS((t, D), BF16), SDS((t, D), BF16), SDS((t, D), BF16), SDS((8, 2 * D), F32)],
        scratch_shapes=[pltpu.VMEM((BLK, D), F32)],
        args=(qs, kvx, kvx, u_za, d_o, sinks, bias), vmem=VMEM_BIG)


def _qkv_post(u_qkv, dqs, dkv, dza, qg_s, kg_t, rider):
    t = u_qkv.shape[0]
    tm = min(512, t)

    def norm_bwd(x, dy, g):
        r = lax.rsqrt(_head_sum(x * x) * (1.0 / HEAD) + EPS)
        xhat = x * r
        dxh = dy * g
        return r * (dxh - xhat * (_head_sum(dxh * xhat) * (1.0 / HEAD))), _fold8(dy * xhat)

    def body(u_ref, dq_ref, dkv_ref, dza_ref, qg_ref, kg_ref, du_ref, dqg_ref, dkg_ref):
        @pl.when(pl.program_id(0) == 0)
        def _():
            dqg_ref[...] = jnp.zeros_like(dqg_ref)
            dkg_ref[...] = jnp.zeros_like(dkg_ref)
        dq, gq = norm_bwd(u_ref[:, 0:D].astype(F32), dq_ref[...].astype(F32), qg_ref[...])
        fold = _fold_mat()
        dk, gk = norm_bwd(u_ref[:, D:D + 256].astype(F32), _dot(dkv_ref[:, 0:512], fold), kg_ref[...])
        du_ref[:, 0:D] = dq.astype(BF16)
        du_ref[:, D:D + 256] = dk.astype(BF16)
        du_ref[:, D + 256:D + 512] = _dot(dkv_ref[:, 512:1024], fold).astype(BF16)
        du_ref[:, D + 512:2 * D + 512] = dza_ref[...]
        dqg_ref[...] += gq
        dkg_ref[...] += gk

    row = pl.BlockSpec((tm, D), lambda i: (i, 0))
    return _call(
        body, rider, name="qkv_post", grid=(t // tm,),
        in_specs=[pl.BlockSpec((tm, 1536), lambda i: (i, 0)), row, row, row,
                  pl.BlockSpec((1, D), lambda i: (0, 0)), pl.BlockSpec((1, 256), lambda i: (0, 0))],
        out_specs=[pl.BlockSpec((tm, 2560), lambda i: (i, 0)), pl.BlockSpec((8, D), lambda i: (0, 0)),
                   pl.BlockSpec((8, 256), lambda i: (0, 0))],
        out_shape=[SDS((t, 2560), BF16), SDS((8, D), F32), SDS((8, 256), F32)],
        args=(u_qkv, dqs, dkv, dza, qg_s, kg_t), vmem=VMEM_BIG)


N_GRAN = N_IN // CB
DU_COLS = ((0, 4096), (4096, 6656), (6656, N_IN))


def _du_granule(j):
    return jnp.clip(j, 0, 7), jnp.clip(j - 8, 0, 4), jnp.clip(j - 13, 0, 3)


def _du_select(j, refs, fn):
    for ref, lo, hi in zip(refs, (0, 8, 13), (8, 13, 17)):
        @pl.when((j >= lo) & (j < hi))
        def _():
            fn(ref)


def _in_proj_bwd(du, w_full, rider):
    t = du[0].shape[0]
    tn = min(512, t)

    def body(a0, a1, a2, w_hbm, dht_ref, w_ref, sem):
        @pl.when(pl.program_id(0) == 0)
        def _():
            cp = pltpu.make_async_copy(w_hbm, w_ref, sem)
            cp.start()
            cp.wait()
        acc = None
        for a_ref, (lo, hi) in zip((a0, a1, a2), DU_COLS):
            part = _dot_nt(w_ref[:, lo:hi], a_ref[...])
            acc = part if acc is None else acc + part
        dht_ref[...] = acc

    (dht,), got = _call(
        body, rider, name="in_proj_bwd", grid=(t // tn,),
        in_specs=[pl.BlockSpec((tn, hi - lo), lambda i: (i, 0)) for lo, hi in DU_COLS] + [ANY],
        out_specs=[pl.BlockSpec((D, tn), lambda i: (0, i))], out_shape=[SDS((D, t), F32)],
        scratch_shapes=[pltpu.VMEM((D, N_IN), BF16), pltpu.SemaphoreType.DMA(())], args=(*du, w_full), vmem=VMEM_BIG)
    return dht, got


def _rmsnorm_bwd(dht, x, g, dout):
    t = x.shape[0]
    tm = min(256, t)

    def body(dht_ref, x_ref, g_ref, do_ref, dx_ref, dg_ref):
        @pl.when(pl.program_id(0) == 0)
        def _():
            dg_ref[...] = jnp.zeros_like(dg_ref)
        dh = dht_ref[...].T
        xv = x_ref[...]
        r = lax.rsqrt(jnp.mean(xv * xv, axis=-1, keepdims=True) + EPS)
        xhat = xv * r
        dg_ref[...] += _fold8(dh * xhat)
        dxh = dh * g_ref[...]
        dx_ref[...] = do_ref[...] + r * (dxh - xhat * jnp.mean(dxh * xhat, axis=-1, keepdims=True))

    row = pl.BlockSpec((tm, D), lambda i: (i, 0))
    return pl.pallas_call(
        body, name="rmsnorm_bwd", grid=(t // tm,),
        in_specs=[pl.BlockSpec((D, tm), lambda i: (0, i)), row, pl.BlockSpec((1, D), lambda i: (0, 0)), row],
        out_specs=[row, pl.BlockSpec((8, D), lambda i: (0, 0))],
        out_shape=[SDS((t, D), F32), SDS((8, D), F32)], compiler_params=_cp("arbitrary", vmem=VMEM_BIG),
    )(dht, x, g, dout)


def _in_proj_wgrad(ht, du):
    t = ht.shape[1]
    tk = min(4096, t)
    n_k = t // tk

    def body(h_ref, b0, b1, b2, g_ref):
        j, k = pl.program_id(0), pl.program_id(1)

        if n_k > 1:
            @pl.when(k == 0)
            def _():
                g_ref[...] = jnp.zeros_like(g_ref)

        def add(b_ref):
            if n_k > 1:
                g_ref[...] += _dot(h_ref[...], b_ref[...])
            else:
                g_ref[...] = _dot(h_ref[...], b_ref[...])
        _du_select(j, (b0, b1, b2), add)

    seg = lambda q: pl.BlockSpec((tk, CB), lambda j, k: (k, _du_granule(j)[q]))
    return pl.pallas_call(
        body, name="in_proj_wgrad", grid=(N_GRAN, t // tk),
        in_specs=[pl.BlockSpec((D, tk), lambda j, k: (0, k)), seg(0), seg(1), seg(2)],
        out_specs=pl.BlockSpec((D, CB), lambda j, k: (0, j)), out_shape=SDS((D, N_IN), F32),
        compiler_params=_cp("parallel", "arbitrary", vmem=VMEM_BIG),
    )(ht, *du)


def _swap_rider(g_in, g_sm):
    def copies(ins, outs, send, recv):
        x, y, c = _mesh_pos()
        cps = []
        for src, dst in zip(ins, outs):
            half = src.at[1 - c] if len(src.shape) == 3 else src.at[:, :, 1 - c]
            cps.append(_rcopy(half, dst, send, recv, len(cps), (x, y, 1 - c)))
        return cps

    arrays = [g for g in (g_in, g_sm) if g is not None]
    shapes = [SDS((512, N_IN), F32) if len(g.shape) == 3 else SDS((3, 4, 128, D), F32) for g in arrays]
    return _Rider(arrays, shapes, len(arrays), copies)


def _add_halves_in(c_idx, g_in, r_in):
    def body(c_ref, a_ref, b_ref, f_ref, h_ref):
        s = a_ref[...] + b_ref[...]
        f_ref[...] = s
        h_ref[...] = s.astype(BF16)

    blk = pl.BlockSpec((128, N_IN), lambda i, c: (i, 0))
    return pl.pallas_call(
        body, name="add_halves_in",
        grid_spec=pltpu.PrefetchScalarGridSpec(
            num_scalar_prefetch=1, grid=(4,),
            in_specs=[pl.BlockSpec((None, 128, N_IN), lambda i, c: (c[0], i, 0)), blk], out_specs=[blk, blk]),
        out_shape=[SDS((512, N_IN), F32), SDS((512, N_IN), BF16)], compiler_params=_cp("parallel", vmem=VMEM_BIG),
    )(c_idx, g_in, r_in)


def _add_halves_sm(c_idx, g_sm, r_sm):
    def body(c_ref, a_ref, b_ref, f_ref, h_ref):
        s = a_ref[...] + b_ref[...]
        f_ref[...] = s
        h_ref[...] = s.astype(BF16)

    blk = pl.BlockSpec((1, 4, 128, D), lambda a, c: (a, 0, 0, 0))
    return pl.pallas_call(
        body, name="add_halves_sm",
        grid_spec=pltpu.PrefetchScalarGridSpec(
            num_scalar_prefetch=1, grid=(3,),
            in_specs=[pl.BlockSpec((1, 4, None, 128, D), lambda a, c: (a, 0, c[0], 0, 0)), blk], out_specs=[blk, blk]),
        out_shape=[SDS((3, 4, 128, D), F32), SDS((3, 4, 128, D), BF16)], compiler_params=_cp("parallel"),
    )(c_idx, g_sm, r_sm)


def _scatter_rider(h_in, h_sm):
    def copies(ins, outs, send, recv):
        x, y, c = _mesh_pos()
        cps = []
        for src, dst in zip(ins, outs):
            for k, chip in enumerate(_other_chips(x, y)):
                their = 2 * chip[0] + chip[1]
                part = src.at[:, pl.ds(pl.multiple_of(their * SH_IN, 128), SH_IN)] if len(src.shape) == 2 else src.at[:, their]
                cps.append(_rcopy(part, dst.at[k], send, recv, len(cps), (*chip, c)))
        return cps

    arrays = [h for h in (h_in, h_sm) if h is not None]
    shapes = [SDS((3, 512, SH_IN), BF16) if len(h.shape) == 2 else SDS((3, 3, 128, D), BF16) for h in arrays]
    return _Rider(arrays, shapes, 3 * len(arrays), copies)


def _ride_alone(rider, name):
    return _hosted_call(None, rider, name=name, grid=(), in_specs=[], out_specs=[], out_shape=[], args=())[1]


def _final_sum_in(chip_idx, f_in, r_in):
    def body(j_ref, a_ref, r_ref, o_ref):
        o_ref[...] = a_ref[...] + r_ref[0].astype(F32) + r_ref[1].astype(F32) + r_ref[2].astype(F32)

    return pl.pallas_call(
        body, name="final_sum_in",
        grid_spec=pltpu.PrefetchScalarGridSpec(
            num_scalar_prefetch=1, grid=(4,),
            in_specs=[pl.BlockSpec((128, SH_IN), lambda i, j: (i, j[0])), pl.BlockSpec((3, 128, SH_IN), lambda i, j: (0, i, 0))],
            out_specs=pl.BlockSpec((128, SH_IN), lambda i, j: (i, 0))),
        out_shape=SDS((512, SH_IN), F32), compiler_params=_cp("parallel"),
    )(chip_idx, f_in, r_in)


def _final_sum_sm(chip_idx, f_sm, r_sm):
    def body(j_ref, a_ref, r_ref, o_ref):
        o_ref[...] = a_ref[...] + r_ref[0].astype(F32) + r_ref[1].astype(F32) + r_ref[2].astype(F32)

    return pl.pallas_call(
        body, name="final_sum_sm",
        grid_spec=pltpu.PrefetchScalarGridSpec(
            num_scalar_prefetch=1, grid=(3,),
            in_specs=[pl.BlockSpec((1, None, 128, D), lambda a, j: (a, j[0], 0, 0)),
                      pl.BlockSpec((3, 1, 128, D), lambda a, j: (0, a, 0, 0))],
            out_specs=pl.BlockSpec((1, 128, D), lambda a, j: (a, 0, 0))),
        out_shape=SDS((3, 128, D), F32), compiler_params=_cp("parallel"),
    )(chip_idx, f_sm, r_sm)


def _join_halves(t_in, t_sm):
    n_cp = N_LAYERS * 4

    def body(*refs):
        ins, outs = refs[:2 * N_LAYERS], refs[2 * N_LAYERS:2 * N_LAYERS + 4]
        send, recv, loc_in, loc_out, stage_in, stage_sm = refs[2 * N_LAYERS + 4:]
        x, y, c = _mesh_pos()
        cps, own = [], []
        for l in range(N_LAYERS):
            for a in range(4):
                s = 4 * l + a
                if a == 0:
                    src = ins[2 * l]
                    dst = outs[0].at[l, pl.ds(pl.multiple_of(c * 512, 512), 512), :]
                else:
                    src = ins[2 * l + 1].at[a - 1]
                    dst = outs[a].at[l, pl.ds(pl.multiple_of(c * 128, 128), 128), :]
                own.append((src, dst, min(a, 1)))
                cp = pltpu.make_async_remote_copy(src_ref=src, dst_ref=dst, send_sem=send.at[s], recv_sem=recv.at[s],
                                                  device_id=(x, y, 1 - c), device_id_type=MESH)
                cp.start()
                cps.append(cp)
        _staged_copies(own, (stage_in, stage_sm), loc_in, loc_out)
        for l in range(N_LAYERS):
            for a in range(4):
                s = 4 * l + a
                if a == 0:
                    got = outs[0].at[l, pl.ds(pl.multiple_of((1 - c) * 512, 512), 512), :]
                else:
                    got = outs[a].at[l, pl.ds(pl.multiple_of((1 - c) * 128, 128), 128), :]
                pltpu.make_async_remote_copy(src_ref=got, dst_ref=got, send_sem=send.at[s], recv_sem=recv.at[s],
                                             device_id=(x, y, 1 - c), device_id_type=MESH).wait_recv()
        for cp in cps:
            cp.wait_send()

    args = []
    for l in range(N_LAYERS):
        args += [t_in[l], t_sm[l]]
    sm = SDS((N_LAYERS, SH_ROW, D), F32)
    return pl.pallas_call(
        body, name="join_halves", in_specs=[ANY] * (2 * N_LAYERS), out_specs=[ANY] * 4,
        out_shape=[SDS((N_LAYERS, D, SH_IN), F32), sm, sm, sm],
        scratch_shapes=[pltpu.SemaphoreType.DMA((n_cp,))] * 4
        + [pltpu.VMEM((2, 512, SH_IN), F32), pltpu.VMEM((2, 128, D), F32)],
        compiler_params=_cp(vmem=VMEM_BIG),
    )(*args)


def _adam_math(w, g, m, v):
    m = ADAM_B1 * m + (1.0 - ADAM_B1) * g
    v = ADAM_B2 * v + (1.0 - ADAM_B2) * (g * g)
    m_hat = m / (1.0 - ADAM_B1 ** ADAM_STEP)
    v_hat = v / (1.0 - ADAM_B2 ** ADAM_STEP)
    delta = -ADAM_LR * (m_hat / (jnp.sqrt(v_hat) + ADAM_EPS) + ADAM_WD * w)
    return delta, m, v


def _adamw_big(w, g, m, v, name):
    rows, cols = w.shape
    tr = 128

    def body(w_ref, g_ref, m_ref, v_ref, d_ref, nm_ref, nv_ref):
        d_ref[...], nm_ref[...], nv_ref[...] = _adam_math(w_ref[...], g_ref[...], m_ref[...], v_ref[...])

    blk = pl.BlockSpec((tr, cols), lambda i: (i, 0))
    return pl.pallas_call(
        body, name=name, grid=(rows // tr,), in_specs=[blk] * 4, out_specs=[blk] * 3,
        out_shape=[SDS((rows, cols), F32)] * 3, compiler_params=_cp("parallel", vmem=VMEM_BIG),
    )(w, g, m, v)


def _adamw_small(ws, gs, ms, vs):
    n = len(ws)

    def body(*refs):
        for k in range(n):
            w_ref, g_ref, m_ref, v_ref = (refs[q * n + k] for q in range(4))
            d, nm, nv = _adam_math(w_ref[...], g_ref[...], m_ref[...], v_ref[...])
            refs[4 * n + k][...] = d
            refs[5 * n + k][...] = nm
            refs[6 * n + k][...] = nv

    vm = pl.BlockSpec(memory_space=pltpu.VMEM)
    shapes = [SDS(w.shape, F32) for w in ws]
    res = pl.pallas_call(
        body, name="adamw_small", in_specs=[vm] * (4 * n), out_specs=[vm] * (3 * n), out_shape=shapes * 3,
    )(*ws, *gs, *ms, *vs)
    return res[:n], res[n:2 * n], res[2 * n:]


def _pad_rows(a, rows):
    flat = a.reshape(-1)
    return jnp.pad(flat, (0, rows * 128 - flat.shape[0])).reshape(rows, 128)


def kernel(x, norm_g, w_in, conv_w, q_norm_g, k_norm_g, sinks, w_conv_out, w_attn_out, gate_b, w_out, loss_target, m_norm_g, m_w_in, m_conv_w, m_q_norm_g, m_k_norm_g, m_sinks, m_w_conv_out, m_w_attn_out, m_gate_b, m_w_out, v_norm_g, v_w_in, v_conv_w, v_q_norm_g, v_k_norm_g, v_sinks, v_w_conv_out, v_w_attn_out, v_gate_b, v_w_out):
    xi, yi, ci = _mesh_pos()
    chip = 2 * xi + yi
    c_idx = jnp.reshape(ci, (1,)).astype(jnp.int32)
    chip_idx = jnp.reshape(chip, (1,)).astype(jnp.int32)
    t = x.shape[1]
    xs = [x.reshape(t, D)]
    tgt = loss_target.reshape(t, D)

    full_w = [[_cast_w_in(chip_idx, w_in, l), _cast_w_small(chip_idx, w_conv_out, w_attn_out, w_out, l)]
              for l in range(N_LAYERS)]
    full_w[0][0] = _ride_alone(_gather_rider(full_w[0][:1], "A"), "gather_first_ici")[0]
    full_w[0][0] = _ride_alone(_gather_rider(full_w[0][:1], "B"), "gather_first_d2d")[0]
    placed = lax.dynamic_update_slice(jnp.zeros((N_LAYERS, 3, D), F32),
                                      jnp.where(ci == 0, conv_w, 0.0), (0, 0, chip * SH_ROW))
    conv_full = _allreduce_small(placed.reshape(96, 128)).reshape(N_LAYERS, 3, D)

    qg_s = jnp.tile(q_norm_g, (1, N_Q)) * SCALE
    kg_t = jnp.tile(k_norm_g, (1, N_KV))
    bias = _band_bias()
    saved = []
    for l in range(N_LAYERS):
        nxt = full_w[l + 1] if l + 1 < N_LAYERS else None
        h, ht = _rmsnorm_fwd(xs[l], norm_g[l:l + 1])
        (u_conv, u_qkv, u_za, u_gl), got = _in_proj(h, full_w[l][0], _gather_rider(nxt[:1], "A") if nxt else None)
        if nxt:
            nxt[0] = got[0]
        y_c = _conv_fwd(u_conv, conv_full[l])
        qs, kvx = _qkv_prep(u_qkv, qg_s[l:l + 1], kg_t[l:l + 1])
        small_w = ([full_w[0][1]] if l == 0 else []) + ([nxt[1]] if nxt else [])
        o, got = _attn_fwd(qs, kvx, u_za, sinks[l:l + 1], bias, _gather_rider(small_w, "A") if small_w else None)
        if nxt:
            nxt[1] = got[-1]
        if l == 0:
            full_w[0][1] = _ride_alone(_gather_rider(got[:1], "B"), "gather_first_small_d2d")[0]
        (x_next, y_a, y_b, merged), got = _out_proj_fwd(xs[l], y_c, o, u_gl, gate_b[l:l + 1], full_w[l][1],
                                                        _gather_rider(nxt, "B") if nxt else None)
        if nxt:
            nxt[0], nxt[1] = got
        xs.append(x_next)
        saved.append((ht, u_conv, u_qkv, u_za, u_gl, y_c, o, y_a, y_b, merged, qs, kvx))

    dout, sq = _loss_head(xs[N_LAYERS], tgt)
    loss = lax.psum(jnp.sum(sq) * (0.5 / D), ("x", "y", "c"))

    small, t_in, t_sm = [None] * N_LAYERS, [None] * N_LAYERS, [None] * N_LAYERS
    grads = None
    halves = None

    def add_halves(g, got):
        f_in, h_in = _add_halves_in(c_idx, g[0], got[0])
        f_sm, h_sm = _add_halves_sm(c_idx, g[1], got[1])
        return f_in, h_in, f_sm, h_sm

    def final_sums(hv, parts):
        return _final_sum_in(chip_idx, hv[0], parts[0]), _final_sum_sm(chip_idx, hv[2], parts[1])

    for l in reversed(range(N_LAYERS)):
        w_full, w_sm = full_w[l]
        last = l == 0
        ht, u_conv, u_qkv, u_za, u_gl, y_c, o, y_a, y_b, merged, qs, kvx = saved[l]
        (d_ya, d_yb, du_gl, d_yc, d_o, dgb), got = _out_proj_bwd(dout, y_a, y_b, u_gl, gate_b[l:l + 1], w_sm,
                                                                 _swap_rider(*grads) if grads else None)
        if grads:
            halves = add_halves(grads, got)
        g_sm = _small_wgrads(y_c, d_ya, o, d_yb, merged, dout).reshape(3, 4, 2, 128, D)
        (du_conv, dcw), got = _conv_bwd(d_yc, u_conv, conv_full[l], _swap_rider(None, g_sm) if last else None)
        if last:
            f_sm0, h_sm0 = _add_halves_sm(c_idx, g_sm, got[0])
        (dqs, dkv, dza, dsk), got = _attn_bwd(d_o, qs, kvx, u_za, sinks[l:l + 1], bias,
                                              _scatter_rider(halves[1], halves[3]) if halves else None)
        if halves:
            t_in[l + 1], t_sm[l + 1] = final_sums(halves, got)
        dsk = jnp.sum(dsk[0].reshape(N_KV, 2, 2, BLK), axis=-1).transpose(0, 2, 1).reshape(N_Q)
        (du_attn, dqg, dkg), got = _qkv_post(u_qkv, dqs, dkv, dza, qg_s[l:l + 1], kg_t[l:l + 1],
                                             _scatter_rider(None, h_sm0) if last else None)
        if last:
            t_sm[0] = _final_sum_sm(chip_idx, f_sm0, got[0])
        du = (du_conv, du_attn, du_gl)
        grads = (_in_proj_wgrad(ht, du).reshape(2, 512, N_IN), g_sm)
        if last:
            f_in0, h_in0 = _add_halves_in(c_idx, grads[0], _ride_alone(_swap_rider(grads[0], None), "swap_last")[0])
        dh, got = _in_proj_bwd(du, w_full, _scatter_rider(h_in0, None) if last else None)
        if last:
            t_in[0] = _final_sum_in(chip_idx, f_in0, got[0])
        dout, dng = _rmsnorm_bwd(dh, xs[l], norm_g[l:l + 1], dout)
        small[l] = (jnp.sum(dng, axis=0), SCALE * jnp.sum(dqg.reshape(8 * N_Q, HEAD), axis=0),
                    jnp.sum(dkg.reshape(8 * N_KV, HEAD), axis=0), dsk, jnp.sum(dgb, axis=0), dcw[:3])
    grad_x = dout.reshape(1, t, D)

    stack = lambda k: jnp.stack([small[l][k] for l in range(N_LAYERS)])
    pack = jnp.concatenate([_pad_rows(stack(0), 32), _pad_rows(stack(1), 8), _pad_rows(stack(2), 8),
                            _pad_rows(stack(3), 8), _pad_rows(stack(4), 64), _pad_rows(stack(5), 96)], axis=0)
    red = _allreduce_small(pack)
    g_norm_g = red[0:32].reshape(N_LAYERS, D)
    g_q_norm_g = red[32:40].reshape(-1)[:N_LAYERS * HEAD].reshape(N_LAYERS, HEAD)
    g_k_norm_g = red[40:48].reshape(-1)[:N_LAYERS * HEAD].reshape(N_LAYERS, HEAD)
    g_sinks = red[48:56].reshape(-1)[:N_LAYERS * N_Q].reshape(N_LAYERS, N_Q)
    g_gate_b = red[56:120].reshape(N_LAYERS, 2 * D)
    g_conv_full = red[120:216].reshape(N_LAYERS, 3, D)
    g_conv_w = lax.dynamic_slice(g_conv_full, (0, 0, chip * SH_ROW), (N_LAYERS, 3, SH_ROW))

    g_w_in, g_w_co, g_w_ao, g_w_out = _join_halves(t_in, t_sm)

    r_in = N_LAYERS * D
    d_in, nm_in, nv_in = (a.reshape(N_LAYERS, D, SH_IN) for a in _adamw_big(
        w_in.reshape(r_in, SH_IN), g_w_in.reshape(r_in, SH_IN), m_w_in.reshape(r_in, SH_IN),
        v_w_in.reshape(r_in, SH_IN), "adamw_w_in"))
    r_sm = N_LAYERS * SH_ROW
    big = {}
    for nm, w, g, m, v in (("co", w_conv_out, g_w_co, m_w_conv_out, v_w_conv_out),
                           ("ao", w_attn_out, g_w_ao, m_w_attn_out, v_w_attn_out),
                           ("out", w_out, g_w_out, m_w_out, v_w_out)):
        big[nm] = tuple(a.reshape(N_LAYERS, SH_ROW, D) for a in _adamw_big(
            w.reshape(r_sm, D), g.reshape(r_sm, D), m.reshape(r_sm, D), v.reshape(r_sm, D), "adamw_w_small"))
    sm_w = [norm_g, conv_w, q_norm_g, k_norm_g, sinks, gate_b]
    sm_g = [g_norm_g, g_conv_w, g_q_norm_g, g_k_norm_g, g_sinks, g_gate_b]
    sm_m = [m_norm_g, m_conv_w, m_q_norm_g, m_k_norm_g, m_sinks, m_gate_b]
    sm_v = [v_norm_g, v_conv_w, v_q_norm_g, v_k_norm_g, v_sinks, v_gate_b]
    sd, snm, snv = _adamw_small(sm_w, sm_g, sm_m, sm_v)

    def order(norm, w_in_, conv, qn, kn, sk, co, ao, gb, wo):
        return [norm, w_in_, conv, qn, kn, sk, co, ao, gb, wo]

    grads = order(g_norm_g, g_w_in, g_conv_w, g_q_norm_g, g_k_norm_g, g_sinks, g_w_co, g_w_ao, g_gate_b, g_w_out)
    deltas = order(sd[0], d_in, sd[1], sd[2], sd[3], sd[4], big["co"][0], big["ao"][0], sd[5], big["out"][0])
    new_m = order(snm[0], nm_in, snm[1], snm[2], snm[3], snm[4], big["co"][1], big["ao"][1], snm[5], big["out"][1])
    new_v = order(snv[0], nv_in, snv[1], snv[2], snv[3], snv[4], big["co"][2], big["ao"][2], snv[5], big["out"][2])
    return (loss, grad_x, *grads, *deltas, *new_m, *new_v)
```

```python
import functools

import jax
import jax.numpy as jnp
from jax import lax
from jax.experimental import pallas as pl
from jax.experimental.pallas import tpu as pltpu

F32, BF16 = jnp.float32, jnp.bfloat16
SDS = jax.ShapeDtypeStruct
MESH = pl.DeviceIdType.MESH
ANY = pl.BlockSpec(memory_space=pl.ANY)

D = 1024
N_IN = 8704
N_LAYERS = 4
N_Q, N_KV, HEAD = 16, 4, 64
GROUP = N_Q // N_KV
BLK = 128
EPS = 1e-6
NEG = -1e30
SCALE = HEAD ** -0.5
SH_IN = N_IN // 4
SH_ROW = D // 4
CB = 512
SEG_CONV, SEG_Q, SEG_KV, SEG_ZA, SEG_GL = (0, 8), (8, 2), (10, 1), (11, 2), (13, 4)
VMEM_BIG = 56 * 1024 * 1024

ADAM_LR, ADAM_B1, ADAM_B2, ADAM_EPS, ADAM_WD, ADAM_STEP = 0.001, 0.9, 0.999, 1e-08, 0.01, 10


def _cp(*sem, vmem=None):
    return pltpu.CompilerParams(dimension_semantics=sem if sem else None, vmem_limit_bytes=vmem)


def _sigmoid(z):
    return 1.0 / (1.0 + jnp.exp(-z))


def _dot(a, b):
    return jnp.dot(a, b, preferred_element_type=F32)


def _dot_nt(a, b):
    return lax.dot_general(a, b, (((1,), (1,)), ((), ())), preferred_element_type=F32)


def _dot_tn(a, b):
    return lax.dot_general(a, b, (((0,), (0,)), ((), ())), preferred_element_type=F32)


def _rms(xh):
    r = lax.rsqrt(jnp.mean(xh * xh, axis=-1, keepdims=True) + EPS)
    return xh * r, r


def _fold8(v):
    return jnp.sum(v.reshape(v.shape[0] // 8, 8, v.shape[1]), axis=0)


def _cast_w_in(chip_idx, w, layer):
    def body(j_ref, i_ref, o_ref):
        o_ref[...] = i_ref[...].astype(BF16)

    return pl.pallas_call(
        body, name="cast_w_in",
        grid_spec=pltpu.PrefetchScalarGridSpec(
            num_scalar_prefetch=1, grid=(2,),
            in_specs=[pl.BlockSpec((None, 512, SH_IN), lambda i, j: (layer, i, 0))],
            out_specs=pl.BlockSpec((512, SH_IN), lambda i, j: (i, j[0]))),
        out_shape=SDS((D, N_IN), BF16), compiler_params=_cp("parallel"),
    )(chip_idx, w)


def _cast_w_small(chip_idx, a, b, c, layer):
    def body(j_ref, a_ref, b_ref, c_ref, o_ref):
        o_ref[0] = a_ref[...].astype(BF16)
        o_ref[1] = b_ref[...].astype(BF16)
        o_ref[2] = c_ref[...].astype(BF16)

    spec = pl.BlockSpec((None, SH_ROW, D), lambda i, j: (layer, 0, 0))
    return pl.pallas_call(
        body, name="cast_w_small",
        grid_spec=pltpu.PrefetchScalarGridSpec(
            num_scalar_prefetch=1, grid=(1,), in_specs=[spec, spec, spec],
            out_specs=pl.BlockSpec((3, SH_ROW, D), lambda i, j: (0, j[0], 0))),
        out_shape=SDS((3, D, D), BF16), compiler_params=_cp("parallel"),
    )(chip_idx, a, b, c)


def _mesh_pos():
    return lax.axis_index("x"), lax.axis_index("y"), lax.axis_index("c")


def _other_chips(x, y):
    return [(1 - x, y), (x, 1 - y), (1 - x, 1 - y)]


class _Rider:
    def __init__(self, ins, out_shape, n, copies, aliases=()):
        self.ins, self.out_shape, self.n, self.copies, self.aliases = list(ins), list(out_shape), n, copies, aliases


def _rcopy(src, dst, send, recv, k, to):
    return pltpu.make_async_remote_copy(src_ref=src, dst_ref=dst, send_sem=send.at[k], recv_sem=recv.at[k],
                                        device_id=to, device_id_type=MESH)


def _hosted_call(body, rider, *, name, grid, in_specs, out_specs, out_shape, args, scratch_shapes=(), vmem=None):
    n_in, n_out, n_scr = len(in_specs), len(out_specs), len(scratch_shapes)
    r_in, r_out = len(rider.ins), len(rider.out_shape)

    def full_body(*refs):
        host_in, rid_in = refs[:n_in], refs[n_in:n_in + r_in]
        o0 = n_in + r_in
        host_out, rid_out = refs[o0:o0 + n_out], refs[o0 + n_out:o0 + n_out + r_out]
        s0 = o0 + n_out + r_out
        host_scr, (send, recv) = refs[s0:s0 + n_scr], refs[s0 + n_scr:]
        if body is None:
            cps = rider.copies(rid_in, rid_out, send, recv)
            for cp in cps:
                cp.start()
            for cp in cps:
                cp.wait()
            return
        ids = [pl.program_id(a) for a in range(len(grid))]
        first = functools.reduce(lambda p, q: p & q, [i == 0 for i in ids])
        last = functools.reduce(lambda p, q: p & q, [i == g - 1 for i, g in zip(ids, grid)])

        @pl.when(first)
        def _():
            for cp in rider.copies(rid_in, rid_out, send, recv):
                cp.start()

        body(*host_in, *host_out, *host_scr)

        @pl.when(last)
        def _():
            for cp in rider.copies(rid_in, rid_out, send, recv):
                cp.wait()

    res = pl.pallas_call(
        full_body, name=name, grid=grid if body is not None else (),
        in_specs=list(in_specs) + [ANY] * r_in, out_specs=list(out_specs) + [ANY] * r_out,
        out_shape=list(out_shape) + rider.out_shape,
        scratch_shapes=list(scratch_shapes) + [pltpu.SemaphoreType.DMA((rider.n,))] * 2,
        input_output_aliases={n_in + i: n_out + o for i, o in rider.aliases},
        compiler_params=_cp(*(("arbitrary",) * len(grid) if body is not None else ()), vmem=vmem),
    )(*args, *rider.ins)
    return res[:n_out], res[n_out:]


def _call(body, rider, **kw):
    if rider is not None:
        return _hosted_call(body, rider, **kw)
    res = pl.pallas_call(
        body, name=kw["name"], grid=kw["grid"], in_specs=list(kw["in_specs"]), out_specs=list(kw["out_specs"]),
        out_shape=list(kw["out_shape"]), scratch_shapes=list(kw.get("scratch_shapes", ())),
        compiler_params=_cp(*(("arbitrary",) * len(kw["grid"])), vmem=kw.get("vmem")),
    )(*kw["args"])
    return res, []


def _gather_rider(arrays, stage):
    def copies(ins, outs, send, recv):
        x, y, c = _mesh_pos()
        cps = []
        for full in outs:
            for k, chip in enumerate(_other_chips(x, y)):
                whose = 2 * x + y if stage == "A" else 2 * chip[0] + chip[1]
                if len(full.shape) == 2:
                    reg = full.at[pl.ds(pl.multiple_of(c * 512, 512), 512), pl.ds(pl.multiple_of(whose * SH_IN, 128), SH_IN)]
                else:
                    reg = full.at[:, pl.ds(pl.multiple_of(whose * SH_ROW + c * 128, 128), 128), :]
                to = (*chip, c) if stage == "A" else (x, y, 1 - c)
                cps.append(_rcopy(reg, reg, send, recv, len(cps), to))
        return cps

    return _Rider(arrays, [SDS(v.shape, v.dtype) for v in arrays], 3 * len(arrays), copies,
                  aliases=tuple((i, i) for i in range(len(arrays))))


def _staged_copies(copies, stages, sem_in, sem_out):
    busy, count = {}, {}
    for idx, (src, dst, kind) in enumerate(copies):
        slot = count.get(kind, 0) % 2
        count[kind] = count.get(kind, 0) + 1
        if (kind, slot) in busy:
            busy.pop((kind, slot)).wait()
        buf = stages[kind].at[slot]
        cin = pltpu.make_async_copy(src, buf, sem_in.at[idx])
        cin.start()
        cin.wait()
        cout = pltpu.make_async_copy(buf, dst, sem_out.at[idx])
        cout.start()
        busy[(kind, slot)] = cout
    for cp in busy.values():
        cp.wait()


def _allreduce_small(pack):
    rows = pack.shape[0]

    def body(p_ref, o_ref, buf, send, recv):
        x, y, c = _mesh_pos()
        me = 4 * x + 2 * y + c
        sends = []
        for r in range(1, 8):
            to = (x if not (r & 4) else 1 - x, y if not (r & 2) else 1 - y, c if not (r & 1) else 1 - c)
            cp = pltpu.make_async_remote_copy(src_ref=p_ref, dst_ref=buf.at[me], send_sem=send.at[r - 1],
                                              recv_sem=recv.at[r - 1], device_id=to, device_id_type=MESH)
            cp.start()
            sends.append(cp)
        buf[me] = p_ref[...]
        for r in range(1, 8):
            frm = (4 * x + 2 * y + c) ^ r
            pltpu.make_async_remote_copy(src_ref=p_ref, dst_ref=buf.at[frm], send_sem=send.at[r - 1],
                                         recv_sem=recv.at[r - 1], device_id=(x, y, c), device_id_type=MESH).wait_recv()
        acc = buf[0]
        for d in range(1, 8):
            acc = acc + buf[d]
        o_ref[...] = acc
        for cp in sends:
            cp.wait_send()

    vm = pl.BlockSpec(memory_space=pltpu.VMEM)
    return pl.pallas_call(
        body, name="allreduce_small", in_specs=[vm], out_specs=vm, out_shape=SDS((rows, 128), F32),
        scratch_shapes=[pltpu.VMEM((8, rows, 128), F32), pltpu.SemaphoreType.DMA((7,)), pltpu.SemaphoreType.DMA((7,))],
    )(pack)


def _rmsnorm_fwd(x, g):
    t = x.shape[0]
    tm = min(512, t)

    def body(x_ref, g_ref, h_ref, ht_ref):
        xv = x_ref[...]
        r = lax.rsqrt(jnp.mean(xv * xv, axis=-1, keepdims=True) + EPS)
        h = xv * r * g_ref[...]
        h_ref[...] = h.astype(BF16)
        ht_ref[...] = h.T.astype(BF16)

    return pl.pallas_call(
        body, name="rmsnorm_fwd", grid=(t // tm,),
        in_specs=[pl.BlockSpec((tm, D), lambda i: (i, 0)), pl.BlockSpec((1, D), lambda i: (0, 0))],
        out_specs=[pl.BlockSpec((tm, D), lambda i: (i, 0)), pl.BlockSpec((D, tm), lambda i: (0, i))],
        out_shape=[SDS((t, D), BF16), SDS((D, t), BF16)],
        compiler_params=_cp("parallel", vmem=VMEM_BIG),
    )(x, g)


FWD_SEGS = ((0, 8), (8, 3), (11, 2), (13, 4))


def _in_proj(h, w_full, rider):
    t = h.shape[0]
    tm = min(2048, t)

    def body(a_ref, b_ref, *o_refs):
        j = pl.program_id(1)
        for o_ref, (off, nblk) in zip(o_refs, FWD_SEGS):
            @pl.when((j >= off) & (j < off + nblk))
            def _():
                o_ref[...] = _dot(a_ref[...], b_ref[...]).astype(BF16)

    def out(seg):
        off, nblk = seg
        return pl.BlockSpec((tm, CB), lambda i, j: (i, jnp.clip(j - off, 0, nblk - 1)))

    res, got = _call(
        body, rider, name="in_proj", grid=(t // tm, N_IN // CB),
        in_specs=[pl.BlockSpec((tm, D), lambda i, j: (i, 0)), pl.BlockSpec((D, CB), lambda i, j: (0, j))],
        out_specs=[out(s) for s in FWD_SEGS], out_shape=[SDS((t, s[1] * CB), BF16) for s in FWD_SEGS],
        args=(h, w_full), vmem=VMEM_BIG)
    return res, got


def _conv_fwd(u_conv, conv_w):
    t = u_conv.shape[0]
    tm = min(256, t)
    hb = tm // 16

    def body(v_ref, b_ref, c_ref, z_ref, hv_ref, hc_ref, w_ref, y_ref):
        i = pl.program_id(0)
        cv = c_ref[...].astype(F32) * v_ref[...].astype(F32)
        halo = hc_ref[...].astype(F32) * hv_ref[...].astype(F32)
        halo = jnp.where(i > 0, halo, 0.0)
        row = lax.broadcasted_iota(jnp.int32, (tm, 1), 0)
        s1 = jnp.where(row == 0, halo[15:16], pltpu.roll(cv, 1, 0))
        s2 = jnp.where(row == 0, halo[14:15], jnp.where(row == 1, halo[15:16], pltpu.roll(cv, 2, 0)))
        conv = w_ref[0:1, :] * s2 + w_ref[1:2, :] * s1 + w_ref[2:3, :] * cv
        z = z_ref[...].astype(F32)
        y_ref[...] = (b_ref[...].astype(F32) * conv * (z * _sigmoid(z))).astype(BF16)

    def col(k):
        return pl.BlockSpec((tm, D), lambda i: (i, k))

    def halo(k):
        return pl.BlockSpec((16, D), lambda i: (jnp.maximum(i * hb - 1, 0), k))

    return pl.pallas_call(
        body, name="conv_fwd", grid=(t // tm,),
        in_specs=[col(0), col(1), col(2), col(3), halo(0), halo(2), pl.BlockSpec((3, D), lambda i: (0, 0))],
        out_specs=pl.BlockSpec((tm, D), lambda i: (i, 0)), out_shape=SDS((t, D), BF16),
        compiler_params=_cp("parallel", vmem=VMEM_BIG),
    )(u_conv, u_conv, u_conv, u_conv, u_conv, u_conv, conv_w)


KVX = 4 * N_KV * 128


def _iota2(shape):
    return lax.broadcasted_iota(jnp.int32, shape, 0), lax.broadcasted_iota(jnp.int32, shape, 1)


def _head_sum(v):
    r, c = _iota2((128, 128))
    ones = ((r >> 6) == (c >> 6)).astype(BF16)
    hi = v.astype(BF16)
    lo = (v - hi.astype(F32)).astype(BF16)
    return jnp.concatenate([_dot(hi[:, g:g + 128], ones) + _dot(lo[:, g:g + 128], ones)
                            for g in range(0, v.shape[1], 128)], axis=1)


def _expand_mats():
    r, c = _iota2((N_KV * HEAD, N_KV * 128))
    base = ((r >> 6) << 7) + (r & 63)
    return (c == base).astype(BF16), (c == base + 64).astype(BF16)


def _fold_mat():
    r, c = _iota2((N_KV * 128, N_KV * HEAD))
    return (((r >> 7) == (c >> 6)) & ((r & 63) == (c & 63))).astype(BF16)


def _qkv_prep(u_qkv, qg_s, kg_t):
    t = u_qkv.shape[0]
    tm = min(512, t)

    def body(u_ref, qg_ref, kg_ref, qs_ref, kvx_ref):
        q = u_ref[:, 0:D].astype(F32)
        rq = lax.rsqrt(_head_sum(q * q) * (1.0 / HEAD) + EPS)
        qs_ref[...] = (q * rq * qg_ref[...]).astype(BF16)
        k = u_ref[:, D:D + 256].astype(F32)
        rk = lax.rsqrt(_head_sum(k * k) * (1.0 / HEAD) + EPS)
        kn = (k * rk * kg_ref[...]).astype(BF16)
        v = u_ref[:, D + 256:D + 512]
        e_lo, e_hi = _expand_mats()
        kvx_ref[:, 0:512] = _dot(kn, e_lo).astype(BF16)
        kvx_ref[:, 512:1024] = _dot(kn, e_hi).astype(BF16)
        kvx_ref[:, 1024:1536] = _dot(v, e_lo).astype(BF16)
        kvx_ref[:, 1536:2048] = _dot(v, e_hi).astype(BF16)

    return pl.pallas_call(
        body, name="qkv_prep", grid=(t // tm,),
        in_specs=[pl.BlockSpec((tm, 1536), lambda i: (i, 0)), pl.BlockSpec((1, D), lambda i: (0, 0)),
                  pl.BlockSpec((1, 256), lambda i: (0, 0))],
        out_specs=[pl.BlockSpec((tm, D), lambda i: (i, 0)), pl.BlockSpec((tm, KVX), lambda i: (i, 0))],
        out_shape=[SDS((t, D), BF16), SDS((t, KVX), BF16)], compiler_params=_cp("parallel", vmem=VMEM_BIG),
    )(u_qkv, qg_s, kg_t)


def _band_bias():
    j, r = _iota2((2 * BLK, 2 * BLK))
    diff = (r & (BLK - 1)) - j + BLK
    band = (diff >= 0) & (diff < BLK)
    return jnp.stack([jnp.where(band & (j >= BLK), 0.0, NEG), jnp.where(band, 0.0, NEG)]).astype(F32)


def _pair_rows(ref_or_val, hk):
    return jnp.concatenate([ref_or_val[:, 256 * hk:256 * hk + 128], ref_or_val[:, 256 * hk + 128:256 * hk + 256]], axis=0)


def _sink_row(sink_ref, hk, half):
    return jnp.concatenate([jnp.full((1, BLK), sink_ref[0, GROUP * hk + half], F32),
                            jnp.full((1, BLK), sink_ref[0, GROUP * hk + 2 + half], F32)], axis=1)


def _kv_operands(kvb, hk, half):
    return (kvb[:, 512 * half + 128 * hk:512 * half + 128 * hk + 128],
            kvb[:, 1024 + 512 * half + 128 * hk:1024 + 512 * half + 128 * hk + 128])


def _attn_fwd(qs, kvx, u_za, sinks, bias, rider):
    t = qs.shape[0]
    nb = t // BLK

    def body(q_ref, kc_ref, kp_ref, za_ref, sink_ref, bias_ref, o_ref, lse_ref):
        kvb = jnp.concatenate([kp_ref[...], kc_ref[...]], axis=0)
        bias_v = bias_ref[...]
        key0 = lax.broadcasted_iota(jnp.int32, (2 * BLK, 1), 0) == 0
        ones = jnp.ones((2 * BLK, 128), BF16)
        cols = []
        for hk in range(N_KV):
            qpp = _pair_rows(q_ref, hk)
            opp = None
            for half in range(2):
                kx, vx = _kv_operands(kvb, hk, half)
                s = _dot_nt(kx, qpp) + bias_v
                sink = _sink_row(sink_ref, hk, half)
                m = jnp.maximum(jnp.max(s, axis=0, keepdims=True), sink)
                p = jnp.exp(s - m)
                es = jnp.exp(sink - m)
                lse_ref[0, 2 * hk + half:2 * hk + half + 1, :] = m + jnp.log(jnp.sum(p, axis=0, keepdims=True) + es)
                pe = jnp.where(key0, es, p).astype(BF16)
                rhs = jnp.concatenate([jnp.where(key0, jnp.zeros_like(vx), vx), ones], axis=1)
                nd = _dot_tn(pe, rhs)
                o = nd[:, :128] * (1.0 / nd[:, 128:])
                opp = o if opp is None else opp + o
            cols += [opp[:BLK], opp[BLK:]]
        za = za_ref[...].astype(F32)
        o_ref[...] = (jnp.concatenate(cols, axis=1) * (za * _sigmoid(za))).astype(BF16)

    prev = lambda n: jnp.maximum(n - 1, 0)
    (o, lse), got = _call(
        body, rider, name="attn_fwd", grid=(nb,),
        in_specs=[pl.BlockSpec((BLK, D), lambda n: (n, 0)),
                  pl.BlockSpec((BLK, KVX), lambda n: (n, 0)), pl.BlockSpec((BLK, KVX), lambda n: (prev(n), 0)),
                  pl.BlockSpec((BLK, D), lambda n: (n, 0)), pl.BlockSpec(memory_space=pltpu.SMEM),
                  pl.BlockSpec((None, 2 * BLK, 2 * BLK), lambda n: (jnp.minimum(n, 1), 0, 0))],
        out_specs=[pl.BlockSpec((BLK, D), lambda n: (n, 0)), pl.BlockSpec((1, 8, 2 * BLK), lambda n: (n, 0, 0))],
        out_shape=[SDS((t, D), BF16), SDS((nb, 8, 2 * BLK), F32)],
        args=(qs, kvx, kvx, u_za, sinks, bias), vmem=VMEM_BIG)
    return o, lse, got


def _out_proj_fwd(x, y_c, o, u_gl, gate_b, w_sm, rider):
    t = x.shape[0]
    tm = min(512, t)

    def body(x_ref, yc_ref, o_ref, gla_ref, glb_ref, gb_ref, wco_ref, wao_ref, wout_ref,
             xn_ref, ya_ref, yb_ref, mg_ref):
        ya = _dot(yc_ref[...], wco_ref[...])
        yb = _dot(o_ref[...], wao_ref[...])
        gb = gb_ref[...]
        ga_ = _sigmoid(gla_ref[...].astype(F32) + gb[:, :D])
        gb_ = _sigmoid(glb_ref[...].astype(F32) + gb[:, D:])
        merged = (ga_ * ya + gb_ * yb).astype(BF16)
        ya_ref[...] = ya.astype(BF16)
        yb_ref[...] = yb.astype(BF16)
        mg_ref[...] = merged
        xn_ref[...] = x_ref[...] + _dot(merged, wout_ref[...])

    row = pl.BlockSpec((tm, D), lambda i: (i, 0))
    wspec = lambda a: pl.BlockSpec((None, D, D), lambda i: (a, 0, 0))
    return _call(
        body, rider, name="out_proj_fwd", grid=(t // tm,),
        in_specs=[row, row, row, pl.BlockSpec((tm, D), lambda i: (i, 0)), pl.BlockSpec((tm, D), lambda i: (i, 1)),
                  pl.BlockSpec((1, 2 * D), lambda i: (0, 0)), wspec(0), wspec(1), wspec(2)],
        out_specs=[row, row, row, row],
        out_shape=[SDS((t, D), F32), SDS((t, D), BF16), SDS((t, D), BF16), SDS((t, D), BF16)],
        args=(x, y_c, o, u_gl, u_gl, gate_b, w_sm, w_sm, w_sm), vmem=VMEM_BIG)


def _loss_head(y, tgt):
    t = y.shape[0]
    tm = min(512, t)

    def body(y_ref, t_ref, dy_ref, acc_ref):
        @pl.when(pl.program_id(0) == 0)
        def _():
            acc_ref[...] = jnp.zeros_like(acc_ref)
        err = y_ref[...] - t_ref[...]
        dy_ref[...] = err * (1.0 / D)
        sq = _fold8(err * err)
        tot = sq[:, 0:128]
        for k in range(1, D // 128):
            tot = tot + sq[:, 128 * k:128 * (k + 1)]
        acc_ref[...] += tot

    row = pl.BlockSpec((tm, D), lambda i: (i, 0))
    return pl.pallas_call(
        body, name="loss_head", grid=(t // tm,), in_specs=[row, row],
        out_specs=[row, pl.BlockSpec((8, 128), lambda i: (0, 0))],
        out_shape=[SDS((t, D), F32), SDS((8, 128), F32)], compiler_params=_cp("arbitrary"),
    )(y, tgt)


def _out_proj_bwd(dout, y_a, y_b, u_gl, gate_b, w_sm, rider):
    t = dout.shape[0]
    tm = min(512, t)

    def body(do_ref, ya_ref, yb_ref, gla_ref, glb_ref, gb_ref, wco_ref, wao_ref, wout_ref,
             dya_ref, dyb_ref, dgl_ref, dyc_ref, dob_ref, dgb_ref):
        @pl.when(pl.program_id(0) == 0)
        def _():
            dgb_ref[...] = jnp.zeros_like(dgb_ref)
        dm = _dot_nt(do_ref[...].astype(BF16), wout_ref[...])
        gb = gb_ref[...]
        ga_ = _sigmoid(gla_ref[...].astype(F32) + gb[:, :D])
        gb_ = _sigmoid(glb_ref[...].astype(F32) + gb[:, D:])
        dya = (ga_ * dm).astype(BF16)
        dyb = (gb_ * dm).astype(BF16)
        dgla = ya_ref[...].astype(F32) * dm * (ga_ * (1.0 - ga_))
        dglb = yb_ref[...].astype(F32) * dm * (gb_ * (1.0 - gb_))
        dya_ref[...] = dya
        dyb_ref[...] = dyb
        dgl_ref[:, :D] = dgla.astype(BF16)
        dgl_ref[:, D:] = dglb.astype(BF16)
        dgb_ref[:, :D] += _fold8(dgla)
        dgb_ref[:, D:] += _fold8(dglb)
        dyc_ref[...] = _dot_nt(dya, wco_ref[...]).astype(BF16)
        dob_ref[...] = _dot_nt(dyb, wao_ref[...]).astype(BF16)

    row = pl.BlockSpec((tm, D), lambda i: (i, 0))
    wspec = lambda a: pl.BlockSpec((None, D, D), lambda i: (a, 0, 0))
    return _call(
        body, rider, name="out_proj_bwd", grid=(t // tm,),
        in_specs=[row, row, row, pl.BlockSpec((tm, D), lambda i: (i, 0)), pl.BlockSpec((tm, D), lambda i: (i, 1)),
                  pl.BlockSpec((1, 2 * D), lambda i: (0, 0)), wspec(0), wspec(1), wspec(2)],
        out_specs=[row, row, pl.BlockSpec((tm, 2 * D), lambda i: (i, 0)), row, row,
                   pl.BlockSpec((8, 2 * D), lambda i: (0, 0))],
        out_shape=[SDS((t, D), BF16), SDS((t, D), BF16), SDS((t, 2 * D), BF16), SDS((t, D), BF16), SDS((t, D), BF16),
                   SDS((8, 2 * D), F32)],
        args=(dout, y_a, y_b, u_gl, u_gl, gate_b, w_sm, w_sm, w_sm), vmem=VMEM_BIG)


def _small_wgrads(y_c, d_ya, o, d_yb, merged, dout):
    t = y_c.shape[0]
    tk = min(512, t)

    def body(yc_ref, dya_ref, o_ref, dyb_ref, mg_ref, do_ref, g_ref):
        @pl.when(pl.program_id(0) == 0)
        def _():
            g_ref[...] = jnp.zeros_like(g_ref)
        g_ref[0] += _dot_tn(yc_ref[...], dya_ref[...])
        g_ref[1] += _dot_tn(o_ref[...], dyb_ref[...])
        g_ref[2] += _dot_tn(mg_ref[...], do_ref[...].astype(BF16))

    row = pl.BlockSpec((tk, D), lambda k: (k, 0))
    return pl.pallas_call(
        body, name="small_wgrads", grid=(t // tk,), in_specs=[row] * 6,
        out_specs=pl.BlockSpec((3, D, D), lambda k: (0, 0, 0)), out_shape=SDS((3, D, D), F32),
        compiler_params=_cp("arbitrary", vmem=VMEM_BIG),
    )(y_c, d_ya, o, d_yb, merged, dout)


def _conv_bwd(d_yc, u_conv, conv_w, rider):
    t = d_yc.shape[0]
    tm = min(256, t)
    hb = tm // 16
    last_halo = t // 16 - 1
    n_steps = t // tm

    def body(dy_ref, v_ref, b_ref, c_ref, z_ref, hv_ref, hc_ref, ndy_ref, nb_ref, nz_ref, w_ref, du_ref, dw_ref):
        i = pl.program_id(0)

        @pl.when(i == 0)
        def _():
            dw_ref[...] = jnp.zeros_like(dw_ref)
        v, c = v_ref[...].astype(F32), c_ref[...].astype(F32)
        b, z = b_ref[...].astype(F32), z_ref[...].astype(F32)
        cv = c * v
        halo = jnp.where(i > 0, hc_ref[...].astype(F32) * hv_ref[...].astype(F32), 0.0)
        row = lax.broadcasted_iota(jnp.int32, (tm, 1), 0)
        s1 = jnp.where(row == 0, halo[15:16], pltpu.roll(cv, 1, 0))
        s2 = jnp.where(row == 0, halo[14:15], jnp.where(row == 1, halo[15:16], pltpu.roll(cv, 2, 0)))
        w0, w1, w2 = w_ref[0:1, :], w_ref[1:2, :], w_ref[2:3, :]
        conv = w0 * s2 + w1 * s1 + w2 * cv
        sig = _sigmoid(z)
        sz = z * sig
        dsz = sig * (1.0 + z * (1.0 - sig))
        dy = dy_ref[...].astype(F32)
        dconv = dy * b * sz
        nz = nz_ref[...].astype(F32)
        nxt = ndy_ref[...].astype(F32) * nb_ref[...].astype(F32) * (nz * _sigmoid(nz))
        nxt = jnp.where(i < n_steps - 1, nxt, 0.0)
        a1 = jnp.where(row == tm - 1, nxt[0:1], pltpu.roll(dconv, tm - 1, 0))
        a2 = jnp.where(row == tm - 2, nxt[0:1], jnp.where(row == tm - 1, nxt[1:2], pltpu.roll(dconv, tm - 2, 0)))
        dcv = w2 * dconv + w1 * a1 + w0 * a2
        du_ref[:, 0:D] = (dcv * c).astype(BF16)
        du_ref[:, D:2 * D] = (dy * conv * sz).astype(BF16)
        du_ref[:, 2 * D:3 * D] = (dcv * v).astype(BF16)
        du_ref[:, 3 * D:4 * D] = (dy * b * conv * dsz).astype(BF16)
        r8 = lax.broadcasted_iota(jnp.int32, (8, 1), 0)
        dw_ref[...] += jnp.where(r8 == 0, jnp.sum(dconv * s2, axis=0, keepdims=True),
                                 jnp.where(r8 == 1, jnp.sum(dconv * s1, axis=0, keepdims=True),
                                           jnp.where(r8 == 2, jnp.sum(dconv * cv, axis=0, keepdims=True), 0.0)))

    def col(k):
        return pl.BlockSpec((tm, D), lambda i: (i, k))

    def halo(k):
        return pl.BlockSpec((16, D), lambda i: (jnp.maximum(i * hb - 1, 0), k))

    def nxt(k):
        return pl.BlockSpec((16, D), lambda i: (jnp.minimum((i + 1) * hb, last_halo), k))

    return _call(
        body, rider, name="conv_bwd", grid=(t // tm,),
        in_specs=[col(0), col(0), col(1), col(2), col(3), halo(0), halo(2), nxt(0), nxt(1), nxt(3),
                  pl.BlockSpec((3, D), lambda i: (0, 0))],
        out_specs=[pl.BlockSpec((tm, 4 * D), lambda i: (i, 0)), pl.BlockSpec((8, D), lambda i: (0, 0))],
        out_shape=[SDS((t, 4 * D), BF16), SDS((8, D), F32)],
        args=(d_yc, u_conv, u_conv, u_conv, u_conv, u_conv, u_conv, d_yc, u_conv, u_conv, conv_w), vmem=VMEM_BIG)


def _attn_bwd(d_o, qs, kvx, u_za, lse, sinks, bias, rider):
    t = d_o.shape[0]
    nb = t // BLK

    def body(q_ref, kc_ref, kp_ref, za_ref, do_ref, lse_ref, sink_ref, bias_ref,
             dq_ref, dkv_ref, dza_ref, dsk_ref, carry_ref):
        n = pl.program_id(0)

        @pl.when(n == 0)
        def _():
            carry_ref[...] = jnp.zeros_like(carry_ref)
            dsk_ref[...] = jnp.zeros_like(dsk_ref)

        live = n < nb
        kvb = jnp.concatenate([kp_ref[...], kc_ref[...]], axis=0)
        bias_v = bias_ref[...]
        za = za_ref[...].astype(F32)
        sig = _sigmoid(za)
        dsa = sig * (1.0 + za * (1.0 - sig))
        do = jnp.where(live, do_ref[...].astype(F32), 0.0)
        dattn = (do * (za * sig)).astype(BF16)
        lo_lanes = lax.broadcasted_iota(jnp.int32, (1, 128), 1) < HEAD
        dq_cols, attn_cols, dk_cols, dv_cols, dsk_rows = [], [], [], [], []
        for hk in range(N_KV):
            qpp = _pair_rows(q_ref, hk)
            dapp = _pair_rows(dattn, hk)
            probs, dss, xk, xv = [], [], [], []
            for half in range(2):
                kx, vx = _kv_operands(kvb, hk, half)
                lse = lse_ref[0, 2 * hk + half:2 * hk + half + 1, :]
                prob = jnp.exp(_dot_nt(kx, qpp) + bias_v - lse)
                psink = jnp.exp(_sink_row(sink_ref, hk, half) - lse)
                tdp = prob * _dot_nt(vx, dapp)
                drow = jnp.sum(tdp, axis=0, keepdims=True)
                ds = (tdp - prob * drow).astype(BF16)
                prob_b = prob.astype(BF16)
                xk.append(_dot(ds, qpp))
                xv.append(_dot(prob_b, dapp))
                probs.append(prob_b)
                dss.append(ds)
                dsk_rows.append(-psink * drow)
            kcat = jnp.concatenate([kvb[:, 128 * hk:128 * hk + 128], kvb[:, 512 + 128 * hk:512 + 128 * hk + 128]], axis=0)
            vcat = jnp.concatenate([kvb[:, 1024 + 128 * hk:1024 + 128 * hk + 128],
                                    kvb[:, 1536 + 128 * hk:1536 + 128 * hk + 128]], axis=0)
            app = _dot_tn(jnp.concatenate(probs, axis=0), vcat)
            dqpp = _dot_tn(jnp.concatenate(dss, axis=0), kcat)
            dq_cols += [dqpp[:BLK], dqpp[BLK:]]
            attn_cols += [app[:BLK], app[BLK:]]
            dk_cols.append(jnp.where(lo_lanes, xk[0], xk[1]))
            dv_cols.append(jnp.where(lo_lanes, xv[0], xv[1]))

        @pl.when(live)
        def _():
            dq_ref[...] = jnp.concatenate(dq_cols, axis=1).astype(BF16)
            dza_ref[...] = (do * jnp.concatenate(attn_cols, axis=1) * dsa).astype(BF16)

        band = jnp.concatenate(dk_cols + dv_cols, axis=1)
        dkv_ref[...] = (band[:BLK] + carry_ref[...]).astype(BF16)
        carry_ref[...] = band[BLK:]
        dsk_ref[...] += jnp.broadcast_to(jnp.concatenate(dsk_rows, axis=1), (8, 2 * N_KV * 2 * BLK))

    cur = lambda n: jnp.minimum(n, nb - 1)
    prev = lambda n: jnp.maximum(n - 1, 0)
    return _call(
        body, rider, name="attn_bwd", grid=(nb + 1,),
        in_specs=[pl.BlockSpec((BLK, D), lambda n: (cur(n), 0)),
                  pl.BlockSpec((BLK, KVX), lambda n: (cur(n), 0)), pl.BlockSpec((BLK, KVX), lambda n: (prev(n), 0)),
                  pl.BlockSpec((BLK, D), lambda n: (cur(n), 0)), pl.BlockSpec((BLK, D), lambda n: (cur(n), 0)),
                  pl.BlockSpec((1, 8, 2 * BLK), lambda n: (cur(n), 0, 0)), pl.BlockSpec(memory_space=pltpu.SMEM),
                  pl.BlockSpec((None, 2 * BLK, 2 * BLK), lambda n: (jnp.minimum(n, 1), 0, 0))],
        out_specs=[pl.BlockSpec((BLK, D), lambda n: (cur(n), 0)), pl.BlockSpec((BLK, D), lambda n: (prev(n), 0)),
                   pl.BlockSpec((BLK, D), lambda n: (cur(n), 0)), pl.BlockSpec((8, 2 * D), lambda n: (0, 0))],
        out_shape=[SDS((t, D), BF16), SDS((t, D), BF16), SDS((t, D), BF16), SDS((8, 2 * D), F32)],
        scratch_shapes=[pltpu.VMEM((BLK, D), F32)],
        args=(qs, kvx, kvx, u_za, d_o, lse, sinks, bias), vmem=VMEM_BIG)


def _qkv_post(u_qkv, dqs, dkv, dza, qg_s, kg_t, rider):
    t = u_qkv.shape[0]
    tm = min(512, t)

    def norm_bwd(x, dy, g):
        r = lax.rsqrt(_head_sum(x * x) * (1.0 / HEAD) + EPS)
        xhat = x * r
        dxh = dy * g
        return r * (dxh - xhat * (_head_sum(dxh * xhat) * (1.0 / HEAD))), _fold8(dy * xhat)

    def body(u_ref, dq_ref, dkv_ref, dza_ref, qg_ref, kg_ref, du_ref, dqg_ref, dkg_ref):
        @pl.when(pl.program_id(0) == 0)
        def _():
            dqg_ref[...] = jnp.zeros_like(dqg_ref)
            dkg_ref[...] = jnp.zeros_like(dkg_ref)
        dq, gq = norm_bwd(u_ref[:, 0:D].astype(F32), dq_ref[...].astype(F32), qg_ref[...])
        fold = _fold_mat()
        dk, gk = norm_bwd(u_ref[:, D:D + 256].astype(F32), _dot(dkv_ref[:, 0:512], fold), kg_ref[...])
        du_ref[:, 0:D] = dq.astype(BF16)
        du_ref[:, D:D + 256] = dk.astype(BF16)
        du_ref[:, D + 256:D + 512] = _dot(dkv_ref[:, 512:1024], fold).astype(BF16)
        du_ref[:, D + 512:2 * D + 512] = dza_ref[...]
        dqg_ref[...] += gq
        dkg_ref[...] += gk

    row = pl.BlockSpec((tm, D), lambda i: (i, 0))
    return _call(
        body, rider, name="qkv_post", grid=(t // tm,),
        in_specs=[pl.BlockSpec((tm, 1536), lambda i: (i, 0)), row, row, row,
                  pl.BlockSpec((1, D), lambda i: (0, 0)), pl.BlockSpec((1, 256), lambda i: (0, 0))],
        out_specs=[pl.BlockSpec((tm, 2560), lambda i: (i, 0)), pl.BlockSpec((8, D), lambda i: (0, 0)),
                   pl.BlockSpec((8, 256), lambda i: (0, 0))],
        out_shape=[SDS((t, 2560), BF16), SDS((8, D), F32), SDS((8, 256), F32)],
        args=(u_qkv, dqs, dkv, dza, qg_s, kg_t), vmem=VMEM_BIG)


N_GRAN = N_IN // CB
DU_COLS = ((0, 4096), (4096, 6656), (6656, N_IN))


def _du_granule(j):
    return jnp.clip(j, 0, 7), jnp.clip(j - 8, 0, 4), jnp.clip(j - 13, 0, 3)


def _du_select(j, refs, fn):
    for ref, lo, hi in zip(refs, (0, 8, 13), (8, 13, 17)):
        @pl.when((j >= lo) & (j < hi))
        def _():
            fn(ref)


def _in_proj_bwd(du, w_full, rider):
    t = du[0].shape[0]
    tn = min(512, t)

    def body(a0, a1, a2, w_hbm, dht_ref, w_ref, sem):
        @pl.when(pl.program_id(0) == 0)
        def _():
            cp = pltpu.make_async_copy(w_hbm, w_ref, sem)
            cp.start()
            cp.wait()
        acc = None
        for a_ref, (lo, hi) in zip((a0, a1, a2), DU_COLS):
            part = _dot_nt(w_ref[:, lo:hi], a_ref[...])
            acc = part if acc is None else acc + part
        dht_ref[...] = acc

    (dht,), got = _call(
        body, rider, name="in_proj_bwd", grid=(t // tn,),
        in_specs=[pl.BlockSpec((tn, hi - lo), lambda i: (i, 0)) for lo, hi in DU_COLS] + [ANY],
        out_specs=[pl.BlockSpec((D, tn), lambda i: (0, i))], out_shape=[SDS((D, t), F32)],
        scratch_shapes=[pltpu.VMEM((D, N_IN), BF16), pltpu.SemaphoreType.DMA(())], args=(*du, w_full), vmem=VMEM_BIG)
    return dht, got


def _rmsnorm_bwd(dht, x, g, dout):
    t = x.shape[0]
    tm = min(256, t)

    def body(dht_ref, x_ref, g_ref, do_ref, dx_ref, dg_ref):
        @pl.when(pl.program_id(0) == 0)
        def _():
            dg_ref[...] = jnp.zeros_like(dg_ref)
        dh = dht_ref[...].T
        xv = x_ref[...]
        r = lax.rsqrt(jnp.mean(xv * xv, axis=-1, keepdims=True) + EPS)
        xhat = xv * r
        dg_ref[...] += _fold8(dh * xhat)
        dxh = dh * g_ref[...]
        dx_ref[...] = do_ref[...] + r * (dxh - xhat * jnp.mean(dxh * xhat, axis=-1, keepdims=True))

    row = pl.BlockSpec((tm, D), lambda i: (i, 0))
    return pl.pallas_call(
        body, name="rmsnorm_bwd", grid=(t // tm,),
        in_specs=[pl.BlockSpec((D, tm), lambda i: (0, i)), row, pl.BlockSpec((1, D), lambda i: (0, 0)), row],
        out_specs=[row, pl.BlockSpec((8, D), lambda i: (0, 0))],
        out_shape=[SDS((t, D), F32), SDS((8, D), F32)], compiler_params=_cp("arbitrary", vmem=VMEM_BIG),
    )(dht, x, g, dout)


def _in_proj_wgrad(ht, du):
    t = ht.shape[1]
    tk = min(4096, t)
    n_k = t // tk

    def body(h_ref, b0, b1, b2, g_ref):
        j, k = pl.program_id(0), pl.program_id(1)

        if n_k > 1:
            @pl.when(k == 0)
            def _():
                g_ref[...] = jnp.zeros_like(g_ref)

        def add(b_ref):
            if n_k > 1:
                g_ref[...] += _dot(h_ref[...], b_ref[...])
            else:
                g_ref[...] = _dot(h_ref[...], b_ref[...])
        _du_select(j, (b0, b1, b2), add)

    seg = lambda q: pl.BlockSpec((tk, CB), lambda j, k: (k, _du_granule(j)[q]))
    return pl.pallas_call(
        body, name="in_proj_wgrad", grid=(N_GRAN, t // tk),
        in_specs=[pl.BlockSpec((D, tk), lambda j, k: (0, k)), seg(0), seg(1), seg(2)],
        out_specs=pl.BlockSpec((D, CB), lambda j, k: (0, j)), out_shape=SDS((D, N_IN), F32),
        compiler_params=_cp("parallel", "arbitrary", vmem=VMEM_BIG),
    )(ht, *du)


def _swap_rider(g_in, g_sm):
    def copies(ins, outs, send, recv):
        x, y, c = _mesh_pos()
        cps = []
        for src, dst in zip(ins, outs):
            half = src.at[1 - c] if len(src.shape) == 3 else src.at[:, :, 1 - c]
            cps.append(_rcopy(half, dst, send, recv, len(cps), (x, y, 1 - c)))
        return cps

    arrays = [g for g in (g_in, g_sm) if g is not None]
    shapes = [SDS((512, N_IN), F32) if len(g.shape) == 3 else SDS((3, 4, 128, D), F32) for g in arrays]
    return _Rider(arrays, shapes, len(arrays), copies)


def _add_halves_in(c_idx, g_in, r_in):
    def body(c_ref, a_ref, b_ref, f_ref, h_ref):
        s = a_ref[...] + b_ref[...]
        f_ref[...] = s
        h_ref[...] = s.astype(BF16)

    blk = pl.BlockSpec((128, N_IN), lambda i, c: (i, 0))
    return pl.pallas_call(
        body, name="add_halves_in",
        grid_spec=pltpu.PrefetchScalarGridSpec(
            num_scalar_prefetch=1, grid=(4,),
            in_specs=[pl.BlockSpec((None, 128, N_IN), lambda i, c: (c[0], i, 0)), blk], out_specs=[blk, blk]),
        out_shape=[SDS((512, N_IN), F32), SDS((512, N_IN), BF16)], compiler_params=_cp("parallel", vmem=VMEM_BIG),
    )(c_idx, g_in, r_in)


def _add_halves_sm(c_idx, g_sm, r_sm):
    def body(c_ref, a_ref, b_ref, f_ref, h_ref):
        s = a_ref[...] + b_ref[...]
        f_ref[...] = s
        h_ref[...] = s.astype(BF16)

    blk = pl.BlockSpec((1, 4, 128, D), lambda a, c: (a, 0, 0, 0))
    return pl.pallas_call(
        body, name="add_halves_sm",
        grid_spec=pltpu.PrefetchScalarGridSpec(
            num_scalar_prefetch=1, grid=(3,),
            in_specs=[pl.BlockSpec((1, 4, None, 128, D), lambda a, c: (a, 0, c[0], 0, 0)), blk], out_specs=[blk, blk]),
        out_shape=[SDS((3, 4, 128, D), F32), SDS((3, 4, 128, D), BF16)], compiler_params=_cp("parallel"),
    )(c_idx, g_sm, r_sm)


def _scatter_rider(h_in, h_sm):
    def copies(ins, outs, send, recv):
        x, y, c = _mesh_pos()
        cps = []
        for src, dst in zip(ins, outs):
            for k, chip in enumerate(_other_chips(x, y)):
                their = 2 * chip[0] + chip[1]
                part = src.at[:, pl.ds(pl.multiple_of(their * SH_IN, 128), SH_IN)] if len(src.shape) == 2 else src.at[:, their]
                cps.append(_rcopy(part, dst.at[k], send, recv, len(cps), (*chip, c)))
        return cps

    arrays = [h for h in (h_in, h_sm) if h is not None]
    shapes = [SDS((3, 512, SH_IN), BF16) if len(h.shape) == 2 else SDS((3, 3, 128, D), BF16) for h in arrays]
    return _Rider(arrays, shapes, 3 * len(arrays), copies)


def _ride_alone(rider, name):
    return _hosted_call(None, rider, name=name, grid=(), in_specs=[], out_specs=[], out_shape=[], args=())[1]


def _final_sum_in(chip_idx, f_in, r_in):
    def body(j_ref, a_ref, r_ref, o_ref):
        o_ref[...] = a_ref[...] + r_ref[0].astype(F32) + r_ref[1].astype(F32) + r_ref[2].astype(F32)

    return pl.pallas_call(
        body, name="final_sum_in",
        grid_spec=pltpu.PrefetchScalarGridSpec(
            num_scalar_prefetch=1, grid=(4,),
            in_specs=[pl.BlockSpec((128, SH_IN), lambda i, j: (i, j[0])), pl.BlockSpec((3, 128, SH_IN), lambda i, j: (0, i, 0))],
            out_specs=pl.BlockSpec((128, SH_IN), lambda i, j: (i, 0))),
        out_shape=SDS((512, SH_IN), F32), compiler_params=_cp("parallel"),
    )(chip_idx, f_in, r_in)


def _final_sum_sm(chip_idx, f_sm, r_sm):
    def body(j_ref, a_ref, r_ref, o_ref):
        o_ref[...] = a_ref[...] + r_ref[0].astype(F32) + r_ref[1].astype(F32) + r_ref[2].astype(F32)

    return pl.pallas_call(
        body, name="final_sum_sm",
        grid_spec=pltpu.PrefetchScalarGridSpec(
            num_scalar_prefetch=1, grid=(3,),
            in_specs=[pl.BlockSpec((1, None, 128, D), lambda a, j: (a, j[0], 0, 0)),
                      pl.BlockSpec((3, 1, 128, D), lambda a, j: (0, a, 0, 0))],
            out_specs=pl.BlockSpec((1, 128, D), lambda a, j: (a, 0, 0))),
        out_shape=SDS((3, 128, D), F32), compiler_params=_cp("parallel"),
    )(chip_idx, f_sm, r_sm)


def _join_halves(t_in, t_sm):
    n_cp = N_LAYERS * 4

    def body(*refs):
        ins, outs = refs[:2 * N_LAYERS], refs[2 * N_LAYERS:2 * N_LAYERS + 4]
        send, recv, loc_in, loc_out, stage_in, stage_sm = refs[2 * N_LAYERS + 4:]
        x, y, c = _mesh_pos()
        cps, own = [], []
        for l in range(N_LAYERS):
            for a in range(4):
                s = 4 * l + a
                if a == 0:
                    src = ins[2 * l]
                    dst = outs[0].at[l, pl.ds(pl.multiple_of(c * 512, 512), 512), :]
                else:
                    src = ins[2 * l + 1].at[a - 1]
                    dst = outs[a].at[l, pl.ds(pl.multiple_of(c * 128, 128), 128), :]
                own.append((src, dst, min(a, 1)))
                cp = pltpu.make_async_remote_copy(src_ref=src, dst_ref=dst, send_sem=send.at[s], recv_sem=recv.at[s],
                                                  device_id=(x, y, 1 - c), device_id_type=MESH)
                cp.start()
                cps.append(cp)
        _staged_copies(own, (stage_in, stage_sm), loc_in, loc_out)
        for l in range(N_LAYERS):
            for a in range(4):
                s = 4 * l + a
                if a == 0:
                    got = outs[0].at[l, pl.ds(pl.multiple_of((1 - c) * 512, 512), 512), :]
                else:
                    got = outs[a].at[l, pl.ds(pl.multiple_of((1 - c) * 128, 128), 128), :]
                pltpu.make_async_remote_copy(src_ref=got, dst_ref=got, send_sem=send.at[s], recv_sem=recv.at[s],
                                             device_id=(x, y, 1 - c), device_id_type=MESH).wait_recv()
        for cp in cps:
            cp.wait_send()

    args = []
    for l in range(N_LAYERS):
        args += [t_in[l], t_sm[l]]
    sm = SDS((N_LAYERS, SH_ROW, D), F32)
    return pl.pallas_call(
        body, name="join_halves", in_specs=[ANY] * (2 * N_LAYERS), out_specs=[ANY] * 4,
        out_shape=[SDS((N_LAYERS, D, SH_IN), F32), sm, sm, sm],
        scratch_shapes=[pltpu.SemaphoreType.DMA((n_cp,))] * 4
        + [pltpu.VMEM((2, 512, SH_IN), F32), pltpu.VMEM((2, 128, D), F32)],
        compiler_params=_cp(vmem=VMEM_BIG),
    )(*args)


def _adam_math(w, g, m, v):
    m = ADAM_B1 * m + (1.0 - ADAM_B1) * g
    v = ADAM_B2 * v + (1.0 - ADAM_B2) * (g * g)
    m_hat = m / (1.0 - ADAM_B1 ** ADAM_STEP)
    v_hat = v / (1.0 - ADAM_B2 ** ADAM_STEP)
    delta = -ADAM_LR * (m_hat / (jnp.sqrt(v_hat) + ADAM_EPS) + ADAM_WD * w)
    return delta, m, v


def _adamw_big(w, g, m, v, name):
    rows, cols = w.shape
    tr = 128

    def body(w_ref, g_ref, m_ref, v_ref, d_ref, nm_ref, nv_ref):
        d_ref[...], nm_ref[...], nv_ref[...] = _adam_math(w_ref[...], g_ref[...], m_ref[...], v_ref[...])

    blk = pl.BlockSpec((tr, cols), lambda i: (i, 0))
    return pl.pallas_call(
        body, name=name, grid=(rows // tr,), in_specs=[blk] * 4, out_specs=[blk] * 3,
        out_shape=[SDS((rows, cols), F32)] * 3, compiler_params=_cp("parallel", vmem=VMEM_BIG),
    )(w, g, m, v)


def _adamw_small(ws, gs, ms, vs):
    n = len(ws)

    def body(*refs):
        for k in range(n):
            w_ref, g_ref, m_ref, v_ref = (refs[q * n + k] for q in range(4))
            d, nm, nv = _adam_math(w_ref[...], g_ref[...], m_ref[...], v_ref[...])
            refs[4 * n + k][...] = d
            refs[5 * n + k][...] = nm
            refs[6 * n + k][...] = nv

    vm = pl.BlockSpec(memory_space=pltpu.VMEM)
    shapes = [SDS(w.shape, F32) for w in ws]
    res = pl.pallas_call(
        body, name="adamw_small", in_specs=[vm] * (4 * n), out_specs=[vm] * (3 * n), out_shape=shapes * 3,
    )(*ws, *gs, *ms, *vs)
    return res[:n], res[n:2 * n], res[2 * n:]


def _pad_rows(a, rows):
    flat = a.reshape(-1)
    return jnp.pad(flat, (0, rows * 128 - flat.shape[0])).reshape(rows, 128)


def kernel(x, norm_g, w_in, conv_w, q_norm_g, k_norm_g, sinks, w_conv_out, w_attn_out, gate_b, w_out, loss_target, m_norm_g, m_w_in, m_conv_w, m_q_norm_g, m_k_norm_g, m_sinks, m_w_conv_out, m_w_attn_out, m_gate_b, m_w_out, v_norm_g, v_w_in, v_conv_w, v_q_norm_g, v_k_norm_g, v_sinks, v_w_conv_out, v_w_attn_out, v_gate_b, v_w_out):
    xi, yi, ci = _mesh_pos()
    chip = 2 * xi + yi
    c_idx = jnp.reshape(ci, (1,)).astype(jnp.int32)
    chip_idx = jnp.reshape(chip, (1,)).astype(jnp.int32)
    t = x.shape[1]
    xs = [x.reshape(t, D)]
    tgt = loss_target.reshape(t, D)

    full_w = [[_cast_w_in(chip_idx, w_in, l), _cast_w_small(chip_idx, w_conv_out, w_attn_out, w_out, l)]
              for l in range(N_LAYERS)]
    full_w[0][0] = _ride_alone(_gather_rider(full_w[0][:1], "A"), "gather_first_ici")[0]
    full_w[0][0] = _ride_alone(_gather_rider(full_w[0][:1], "B"), "gather_first_d2d")[0]
    placed = lax.dynamic_update_slice(jnp.zeros((N_LAYERS, 3, D), F32),
                                      jnp.where(ci == 0, conv_w, 0.0), (0, 0, chip * SH_ROW))
    conv_full = _allreduce_small(placed.reshape(96, 128)).reshape(N_LAYERS, 3, D)

    qg_s = jnp.tile(q_norm_g, (1, N_Q)) * SCALE
    kg_t = jnp.tile(k_norm_g, (1, N_KV))
    bias = _band_bias()
    saved = []
    for l in range(N_LAYERS):
        nxt = full_w[l + 1] if l + 1 < N_LAYERS else None
        h, ht = _rmsnorm_fwd(xs[l], norm_g[l:l + 1])
        (u_conv, u_qkv, u_za, u_gl), got = _in_proj(h, full_w[l][0], _gather_rider(nxt[:1], "A") if nxt else None)
        if nxt:
            nxt[0] = got[0]
        y_c = _conv_fwd(u_conv, conv_full[l])
        qs, kvx = _qkv_prep(u_qkv, qg_s[l:l + 1], kg_t[l:l + 1])
        small_w = ([full_w[0][1]] if l == 0 else []) + ([nxt[1]] if nxt else [])
        o, lse, got = _attn_fwd(qs, kvx, u_za, sinks[l:l + 1], bias, _gather_rider(small_w, "A") if small_w else None)
        if nxt:
            nxt[1] = got[-1]
        if l == 0:
            full_w[0][1] = _ride_alone(_gather_rider(got[:1], "B"), "gather_first_small_d2d")[0]
        (x_next, y_a, y_b, merged), got = _out_proj_fwd(xs[l], y_c, o, u_gl, gate_b[l:l + 1], full_w[l][1],
                                                        _gather_rider(nxt, "B") if nxt else None)
        if nxt:
            nxt[0], nxt[1] = got
        xs.append(x_next)
        saved.append((ht, u_conv, u_qkv, u_za, u_gl, y_c, o, y_a, y_b, merged, qs, kvx, lse))

    dout, sq = _loss_head(xs[N_LAYERS], tgt)
    loss = lax.psum(jnp.sum(sq) * (0.5 / D), ("x", "y", "c"))

    small, t_in, t_sm = [None] * N_LAYERS, [None] * N_LAYERS, [None] * N_LAYERS
    grads = None
    halves = None

    def add_halves(g, got):
        f_in, h_in = _add_halves_in(c_idx, g[0], got[0])
        f_sm, h_sm = _add_halves_sm(c_idx, g[1], got[1])
        return f_in, h_in, f_sm, h_sm

    def final_sums(hv, parts):
        return _final_sum_in(chip_idx, hv[0], parts[0]), _final_sum_sm(chip_idx, hv[2], parts[1])

    for l in reversed(range(N_LAYERS)):
        w_full, w_sm = full_w[l]
        last = l == 0
        ht, u_conv, u_qkv, u_za, u_gl, y_c, o, y_a, y_b, merged, qs, kvx, lse = saved[l]
        (d_ya, d_yb, du_gl, d_yc, d_o, dgb), got = _out_proj_bwd(dout, y_a, y_b, u_gl, gate_b[l:l + 1], w_sm,
                                                                 _swap_rider(*grads) if grads else None)
        if grads:
            halves = add_halves(grads, got)
        g_sm = _small_wgrads(y_c, d_ya, o, d_yb, merged, dout).reshape(3, 4, 2, 128, D)
        (du_conv, dcw), got = _conv_bwd(d_yc, u_conv, conv_full[l], _swap_rider(None, g_sm) if last else None)
        if last:
            f_sm0, h_sm0 = _add_halves_sm(c_idx, g_sm, got[0])
        (dqs, dkv, dza, dsk), got = _attn_bwd(d_o, qs, kvx, u_za, lse, sinks[l:l + 1], bias,
                                              _scatter_rider(halves[1], halves[3]) if halves else None)
        if halves:
            t_in[l + 1], t_sm[l + 1] = final_sums(halves, got)
        dsk = jnp.sum(dsk[0].reshape(N_KV, 2, 2, BLK), axis=-1).transpose(0, 2, 1).reshape(N_Q)
        (du_attn, dqg, dkg), got = _qkv_post(u_qkv, dqs, dkv, dza, qg_s[l:l + 1], kg_t[l:l + 1],
                                             _scatter_rider(None, h_sm0) if last else None)
        if last:
            t_sm[0] = _final_sum_sm(chip_idx, f_sm0, got[0])
        du = (du_conv, du_attn, du_gl)
        grads = (_in_proj_wgrad(ht, du).reshape(2, 512, N_IN), g_sm)
        if last:
            f_in0, h_in0 = _add_halves_in(c_idx, grads[0], _ride_alone(_swap_rider(grads[0], None), "swap_last")[0])
        dh, got = _in_proj_bwd(du, w_full, _scatter_rider(h_in0, None) if last else None)
        if last:
            t_in[0] = _final_sum_in(chip_idx, f_in0, got[0])
        dout, dng = _rmsnorm_bwd(dh, xs[l], norm_g[l:l + 1], dout)
        small[l] = (jnp.sum(dng, axis=0), SCALE * jnp.sum(dqg.reshape(8 * N_Q, HEAD), axis=0),
                    jnp.sum(dkg.reshape(8 * N_KV, HEAD), axis=0), dsk, jnp.sum(dgb, axis=0), dcw[:3])
    grad_x = dout.reshape(1, t, D)

    stack = lambda k: jnp.stack([small[l][k] for l in range(N_LAYERS)])
    pack = jnp.concatenate([_pad_rows(stack(0), 32), _pad_rows(stack(1), 8), _pad_rows(stack(2), 8),
                            _pad_rows(stack(3), 8), _pad_rows(stack(4), 64), _pad_rows(stack(5), 96)], axis=0)
    red = _allreduce_small(pack)
    g_norm_g = red[0:32].reshape(N_LAYERS, D)
    g_q_norm_g = red[32:40].reshape(-1)[:N_LAYERS * HEAD].reshape(N_LAYERS, HEAD)
    g_k_norm_g = red[40:48].reshape(-1)[:N_LAYERS * HEAD].reshape(N_LAYERS, HEAD)
    g_sinks = red[48:56].reshape(-1)[:N_LAYERS * N_Q].reshape(N_LAYERS, N_Q)
    g_gate_b = red[56:120].reshape(N_LAYERS, 2 * D)
    g_conv_full = red[120:216].reshape(N_LAYERS, 3, D)
    g_conv_w = lax.dynamic_slice(g_conv_full, (0, 0, chip * SH_ROW), (N_LAYERS, 3, SH_ROW))

    g_w_in, g_w_co, g_w_ao, g_w_out = _join_halves(t_in, t_sm)

    r_in = N_LAYERS * D
    d_in, nm_in, nv_in = (a.reshape(N_LAYERS, D, SH_IN) for a in _adamw_big(
        w_in.reshape(r_in, SH_IN), g_w_in.reshape(r_in, SH_IN), m_w_in.reshape(r_in, SH_IN),
        v_w_in.reshape(r_in, SH_IN), "adamw_w_in"))
    r_sm = N_LAYERS * SH_ROW
    big = {}
    for nm, w, g, m, v in (("co", w_conv_out, g_w_co, m_w_conv_out, v_w_conv_out),
                           ("ao", w_attn_out, g_w_ao, m_w_attn_out, v_w_attn_out),
                           ("out", w_out, g_w_out, m_w_out, v_w_out)):
        big[nm] = tuple(a.reshape(N_LAYERS, SH_ROW, D) for a in _adamw_big(
            w.reshape(r_sm, D), g.reshape(r_sm, D), m.reshape(r_sm, D), v.reshape(r_sm, D), "adamw_w_small"))
    sm_w = [norm_g, conv_w, q_norm_g, k_norm_g, sinks, gate_b]
    sm_g = [g_norm_g, g_conv_w, g_q_norm_g, g_k_norm_g, g_sinks, g_gate_b]
    sm_m = [m_norm_g, m_conv_w, m_q_norm_g, m_k_norm_g, m_sinks, m_gate_b]
    sm_v = [v_norm_g, v_conv_w, v_q_norm_g, v_k_norm_g, v_sinks, v_gate_b]
    sd, snm, snv = _adamw_small(sm_w, sm_g, sm_m, sm_v)

    def order(norm, w_in_, conv, qn, kn, sk, co, ao, gb, wo):
        return [norm, w_in_, conv, qn, kn, sk, co, ao, gb, wo]

    grads = order(g_norm_g, g_w_in, g_conv_w, g_q_norm_g, g_k_norm_g, g_sinks, g_w_co, g_w_ao, g_gate_b, g_w_out)
    deltas = order(sd[0], d_in, sd[1], sd[2], sd[3], sd[4], big["co"][0], big["ao"][0], sd[5], big["out"][0])
    new_m = order(snm[0], nm_in, snm[1], snm[2], snm[3], snm[4], big["co"][1], big["ao"][1], snm[5], big["out"][1])
    new_v = order(snv[0], nv_in, snv[1], snv[2], snv[3], snv[4], big["co"][2], big["ao"][2], snv[5], big["out"][2])
    return (loss, grad_x, *grads, *deltas, *new_m, *new_v)
```

```python
import functools

import jax
import jax.numpy as jnp
from jax import lax
from jax.experimental import pallas as pl
from jax.experimental.pallas import tpu as pltpu

F32, BF16 = jnp.float32, jnp.bfloat16
SDS = jax.ShapeDtypeStruct
MESH = pl.DeviceIdType.MESH
ANY = pl.BlockSpec(memory_space=pl.ANY)

D = 1024
N_IN = 8704
N_LAYERS = 4
N_Q, N_KV, HEAD = 16, 4, 64
GROUP = N_Q // N_KV
BLK = 128
EPS = 1e-6
NEG = -1e30
SCALE = HEAD ** -0.5
SH_IN = N_IN // 4
SH_ROW = D // 4
CB = 512
SEG_CONV, SEG_Q, SEG_KV, SEG_ZA, SEG_GL = (0, 8), (8, 2), (10, 1), (11, 2), (13, 4)
VMEM_BIG = 56 * 1024 * 1024

ADAM_LR, ADAM_B1, ADAM_B2, ADAM_EPS, ADAM_WD, ADAM_STEP = 0.001, 0.9, 0.999, 1e-08, 0.01, 10


def _cp(*sem, vmem=None):
    return pltpu.CompilerParams(dimension_semantics=sem if sem else None, vmem_limit_bytes=vmem)


def _sigmoid(z):
    return 1.0 / (1.0 + jnp.exp(-z))


def _dot(a, b):
    return jnp.dot(a, b, preferred_element_type=F32)


def _dot_nt(a, b):
    return lax.dot_general(a, b, (((1,), (1,)), ((), ())), preferred_element_type=F32)


def _dot_tn(a, b):
    return lax.dot_general(a, b, (((0,), (0,)), ((), ())), preferred_element_type=F32)


def _rms(xh):
    r = lax.rsqrt(jnp.mean(xh * xh, axis=-1, keepdims=True) + EPS)
    return xh * r, r


def _fold8(v):
    return jnp.sum(v.reshape(v.shape[0] // 8, 8, v.shape[1]), axis=0)


def _cast_w_in(chip_idx, w, layer):
    def body(j_ref, i_ref, o_ref):
        o_ref[...] = i_ref[...].astype(BF16)

    return pl.pallas_call(
        body, name="cast_w_in",
        grid_spec=pltpu.PrefetchScalarGridSpec(
            num_scalar_prefetch=1, grid=(2,),
            in_specs=[pl.BlockSpec((None, 512, SH_IN), lambda i, j: (layer, i, 0))],
            out_specs=pl.BlockSpec((512, SH_IN), lambda i, j: (i, j[0]))),
        out_shape=SDS((D, N_IN), BF16), compiler_params=_cp("parallel"),
    )(chip_idx, w)


def _cast_w_small(chip_idx, a, b, c, layer):
    def body(j_ref, a_ref, b_ref, c_ref, o_ref):
        o_ref[0] = a_ref[...].astype(BF16)
        o_ref[1] = b_ref[...].astype(BF16)
        o_ref[2] = c_ref[...].astype(BF16)

    spec = pl.BlockSpec((None, SH_ROW, D), lambda i, j: (layer, 0, 0))
    return pl.pallas_call(
        body, name="cast_w_small",
        grid_spec=pltpu.PrefetchScalarGridSpec(
            num_scalar_prefetch=1, grid=(1,), in_specs=[spec, spec, spec],
            out_specs=pl.BlockSpec((3, SH_ROW, D), lambda i, j: (0, j[0], 0))),
        out_shape=SDS((3, D, D), BF16), compiler_params=_cp("parallel"),
    )(chip_idx, a, b, c)


def _mesh_pos():
    return lax.axis_index("x"), lax.axis_index("y"), lax.axis_index("c")


def _other_chips(x, y):
    return [(1 - x, y), (x, 1 - y), (1 - x, 1 - y)]


class _Rider:
    def __init__(self, ins, out_shape, n, copies, aliases=()):
        self.ins, self.out_shape, self.n, self.copies, self.aliases = list(ins), list(out_shape), n, copies, aliases


def _merge_riders(*riders):
    riders = [r for r in riders if r is not None]
    if len(riders) < 2:
        return riders[0] if riders else None

    def copies(ins, outs, send, recv, base=0):
        cps, i0, o0 = [], 0, 0
        for r in riders:
            cps += r.copies(ins[i0:i0 + len(r.ins)], outs[o0:o0 + len(r.out_shape)], send, recv, base + len(cps))
            i0, o0 = i0 + len(r.ins), o0 + len(r.out_shape)
        return cps

    aliases, i0, o0 = [], 0, 0
    for r in riders:
        aliases += [(i0 + i, o0 + o) for i, o in r.aliases]
        i0, o0 = i0 + len(r.ins), o0 + len(r.out_shape)
    return _Rider(sum((r.ins for r in riders), []), sum((r.out_shape for r in riders), []),
                  sum(r.n for r in riders), copies, tuple(aliases))


def _rcopy(src, dst, send, recv, k, to):
    return pltpu.make_async_remote_copy(src_ref=src, dst_ref=dst, send_sem=send.at[k], recv_sem=recv.at[k],
                                        device_id=to, device_id_type=MESH)


def _hosted_call(body, rider, *, name, grid, in_specs, out_specs, out_shape, args, scratch_shapes=(), vmem=None):
    n_in, n_out, n_scr = len(in_specs), len(out_specs), len(scratch_shapes)
    r_in, r_out = len(rider.ins), len(rider.out_shape)

    def full_body(*refs):
        host_in, rid_in = refs[:n_in], refs[n_in:n_in + r_in]
        o0 = n_in + r_in
        host_out, rid_out = refs[o0:o0 + n_out], refs[o0 + n_out:o0 + n_out + r_out]
        s0 = o0 + n_out + r_out
        host_scr, (send, recv) = refs[s0:s0 + n_scr], refs[s0 + n_scr:]
        if body is None:
            cps = rider.copies(rid_in, rid_out, send, recv)
            for cp in cps:
                cp.start()
            for cp in cps:
                cp.wait()
            return
        ids = [pl.program_id(a) for a in range(len(grid))]
        first = functools.reduce(lambda p, q: p & q, [i == 0 for i in ids])
        last = functools.reduce(lambda p, q: p & q, [i == g - 1 for i, g in zip(ids, grid)])

        @pl.when(first)
        def _():
            for cp in rider.copies(rid_in, rid_out, send, recv):
                cp.start()

        body(*host_in, *host_out, *host_scr)

        @pl.when(last)
        def _():
            for cp in rider.copies(rid_in, rid_out, send, recv):
                cp.wait()

    res = pl.pallas_call(
        full_body, name=name, grid=grid if body is not None else (),
        in_specs=list(in_specs) + [ANY] * r_in, out_specs=list(out_specs) + [ANY] * r_out,
        out_shape=list(out_shape) + rider.out_shape,
        scratch_shapes=list(scratch_shapes) + [pltpu.SemaphoreType.DMA((rider.n,))] * 2,
        input_output_aliases={n_in + i: n_out + o for i, o in rider.aliases},
        compiler_params=_cp(*(("arbitrary",) * len(grid) if body is not None else ()), vmem=vmem),
    )(*args, *rider.ins)
    return res[:n_out], res[n_out:]


def _call(body, rider, **kw):
    if rider is not None:
        return _hosted_call(body, rider, **kw)
    res = pl.pallas_call(
        body, name=kw["name"], grid=kw["grid"], in_specs=list(kw["in_specs"]), out_specs=list(kw["out_specs"]),
        out_shape=list(kw["out_shape"]), scratch_shapes=list(kw.get("scratch_shapes", ())),
        compiler_params=_cp(*(("arbitrary",) * len(kw["grid"])), vmem=kw.get("vmem")),
    )(*kw["args"])
    return res, []


def _gather_rider(arrays, stage):
    def copies(ins, outs, send, recv, base=0):
        x, y, c = _mesh_pos()
        cps = []
        for full in outs:
            for k, chip in enumerate(_other_chips(x, y)):
                whose = 2 * x + y if stage == "A" else 2 * chip[0] + chip[1]
                if len(full.shape) == 2:
                    reg = full.at[pl.ds(pl.multiple_of(c * 512, 512), 512), pl.ds(pl.multiple_of(whose * SH_IN, 128), SH_IN)]
                else:
                    reg = full.at[:, pl.ds(pl.multiple_of(whose * SH_ROW + c * 128, 128), 128), :]
                to = (*chip, c) if stage == "A" else (x, y, 1 - c)
                cps.append(_rcopy(reg, reg, send, recv, base + len(cps), to))
        return cps

    return _Rider(arrays, [SDS(v.shape, v.dtype) for v in arrays], 3 * len(arrays), copies,
                  aliases=tuple((i, i) for i in range(len(arrays))))


def _staged_copies(copies, stages, sem_in, sem_out):
    busy, count = {}, {}
    for idx, (src, dst, kind) in enumerate(copies):
        slot = count.get(kind, 0) % 2
        count[kind] = count.get(kind, 0) + 1
        if (kind, slot) in busy:
            busy.pop((kind, slot)).wait()
        buf = stages[kind].at[slot]
        cin = pltpu.make_async_copy(src, buf, sem_in.at[idx])
        cin.start()
        cin.wait()
        cout = pltpu.make_async_copy(buf, dst, sem_out.at[idx])
        cout.start()
        busy[(kind, slot)] = cout
    for cp in busy.values():
        cp.wait()


def _allreduce_small(pack):
    rows = pack.shape[0]

    def body(p_ref, o_ref, buf, send, recv):
        x, y, c = _mesh_pos()
        me = 4 * x + 2 * y + c
        sends = []
        for r in range(1, 8):
            to = (x if not (r & 4) else 1 - x, y if not (r & 2) else 1 - y, c if not (r & 1) else 1 - c)
            cp = pltpu.make_async_remote_copy(src_ref=p_ref, dst_ref=buf.at[me], send_sem=send.at[r - 1],
                                              recv_sem=recv.at[r - 1], device_id=to, device_id_type=MESH)
            cp.start()
            sends.append(cp)
        buf[me] = p_ref[...]
        for r in range(1, 8):
            frm = (4 * x + 2 * y + c) ^ r
            pltpu.make_async_remote_copy(src_ref=p_ref, dst_ref=buf.at[frm], send_sem=send.at[r - 1],
                                         recv_sem=recv.at[r - 1], device_id=(x, y, c), device_id_type=MESH).wait_recv()
        acc = buf[0]
        for d in range(1, 8):
            acc = acc + buf[d]
        o_ref[...] = acc
        for cp in sends:
            cp.wait_send()

    vm = pl.BlockSpec(memory_space=pltpu.VMEM)
    return pl.pallas_call(
        body, name="allreduce_small", in_specs=[vm], out_specs=vm, out_shape=SDS((rows, 128), F32),
        scratch_shapes=[pltpu.VMEM((8, rows, 128), F32), pltpu.SemaphoreType.DMA((7,)), pltpu.SemaphoreType.DMA((7,))],
    )(pack)


def _rmsnorm_fwd(x, g):
    t = x.shape[0]
    tm = min(512, t)

    def body(x_ref, g_ref, h_ref, ht_ref):
        xv = x_ref[...]
        r = lax.rsqrt(jnp.mean(xv * xv, axis=-1, keepdims=True) + EPS)
        h = xv * r * g_ref[...]
        h_ref[...] = h.astype(BF16)
        ht_ref[...] = h.T.astype(BF16)

    return pl.pallas_call(
        body, name="rmsnorm_fwd", grid=(t // tm,),
        in_specs=[pl.BlockSpec((tm, D), lambda i: (i, 0)), pl.BlockSpec((1, D), lambda i: (0, 0))],
        out_specs=[pl.BlockSpec((tm, D), lambda i: (i, 0)), pl.BlockSpec((D, tm), lambda i: (0, i))],
        out_shape=[SDS((t, D), BF16), SDS((D, t), BF16)],
        compiler_params=_cp("parallel", vmem=VMEM_BIG),
    )(x, g)


FWD_SEGS = ((0, 8), (8, 3), (11, 2), (13, 4))


def _in_proj(h, w_full, rider):
    t = h.shape[0]
    tm = min(2048, t)

    def body(a_ref, b_ref, *o_refs):
        j = pl.program_id(1)
        for o_ref, (off, nblk) in zip(o_refs, FWD_SEGS):
            @pl.when((j >= off) & (j < off + nblk))
            def _():
                o_ref[...] = _dot(a_ref[...], b_ref[...]).astype(BF16)

    def out(seg):
        off, nblk = seg
        return pl.BlockSpec((tm, CB), lambda i, j: (i, jnp.clip(j - off, 0, nblk - 1)))

    res, got = _call(
        body, rider, name="in_proj", grid=(t // tm, N_IN // CB),
        in_specs=[pl.BlockSpec((tm, D), lambda i, j: (i, 0)), pl.BlockSpec((D, CB), lambda i, j: (0, j))],
        out_specs=[out(s) for s in FWD_SEGS], out_shape=[SDS((t, s[1] * CB), BF16) for s in FWD_SEGS],
        args=(h, w_full), vmem=VMEM_BIG)
    return res, got


def _conv_fwd(u_conv, conv_w):
    t = u_conv.shape[0]
    tm = min(256, t)
    hb = tm // 16

    def body(v_ref, b_ref, c_ref, z_ref, hv_ref, hc_ref, w_ref, y_ref):
        i = pl.program_id(0)
        cv = c_ref[...].astype(F32) * v_ref[...].astype(F32)
        halo = hc_ref[...].astype(F32) * hv_ref[...].astype(F32)
        halo = jnp.where(i > 0, halo, 0.0)
        row = lax.broadcasted_iota(jnp.int32, (tm, 1), 0)
        s1 = jnp.where(row == 0, halo[15:16], pltpu.roll(cv, 1, 0))
        s2 = jnp.where(row == 0, halo[14:15], jnp.where(row == 1, halo[15:16], pltpu.roll(cv, 2, 0)))
        conv = w_ref[0:1, :] * s2 + w_ref[1:2, :] * s1 + w_ref[2:3, :] * cv
        z = z_ref[...].astype(F32)
        y_ref[...] = (b_ref[...].astype(F32) * conv * (z * _sigmoid(z))).astype(BF16)

    def col(k):
        return pl.BlockSpec((tm, D), lambda i: (i, k))

    def halo(k):
        return pl.BlockSpec((16, D), lambda i: (jnp.maximum(i * hb - 1, 0), k))

    return pl.pallas_call(
        body, name="conv_fwd", grid=(t // tm,),
        in_specs=[col(0), col(1), col(2), col(3), halo(0), halo(2), pl.BlockSpec((3, D), lambda i: (0, 0))],
        out_specs=pl.BlockSpec((tm, D), lambda i: (i, 0)), out_shape=SDS((t, D), BF16),
        compiler_params=_cp("parallel", vmem=VMEM_BIG),
    )(u_conv, u_conv, u_conv, u_conv, u_conv, u_conv, conv_w)


KVX = 4 * N_KV * 128


def _iota2(shape):
    return lax.broadcasted_iota(jnp.int32, shape, 0), lax.broadcasted_iota(jnp.int32, shape, 1)


def _head_sum(v):
    r, c = _iota2((128, 128))
    ones = ((r >> 6) == (c >> 6)).astype(BF16)
    hi = v.astype(BF16)
    lo = (v - hi.astype(F32)).astype(BF16)
    return jnp.concatenate([_dot(hi[:, g:g + 128], ones) + _dot(lo[:, g:g + 128], ones)
                            for g in range(0, v.shape[1], 128)], axis=1)


def _expand_mats():
    r, c = _iota2((N_KV * HEAD, N_KV * 128))
    base = ((r >> 6) << 7) + (r & 63)
    return (c == base).astype(BF16), (c == base + 64).astype(BF16)


def _fold_mat():
    r, c = _iota2((N_KV * 128, N_KV * HEAD))
    return (((r >> 7) == (c >> 6)) & ((r & 63) == (c & 63))).astype(BF16)


def _qkv_prep(u_qkv, qg_s, kg_t, rider):
    t = u_qkv.shape[0]
    tm = min(512, t)

    def body(u_ref, qg_ref, kg_ref, qs_ref, kvx_ref):
        q = u_ref[:, 0:D].astype(F32)
        rq = lax.rsqrt(_head_sum(q * q) * (1.0 / HEAD) + EPS)
        qs_ref[...] = (q * rq * qg_ref[...]).astype(BF16)
        k = u_ref[:, D:D + 256].astype(F32)
        rk = lax.rsqrt(_head_sum(k * k) * (1.0 / HEAD) + EPS)
        kn = (k * rk * kg_ref[...]).astype(BF16)
        v = u_ref[:, D + 256:D + 512]
        e_lo, e_hi = _expand_mats()
        kvx_ref[:, 0:512] = _dot(kn, e_lo).astype(BF16)
        kvx_ref[:, 512:1024] = _dot(kn, e_hi).astype(BF16)
        kvx_ref[:, 1024:1536] = _dot(v, e_lo).astype(BF16)
        kvx_ref[:, 1536:2048] = _dot(v, e_hi).astype(BF16)

    return _call(
        body, rider, name="qkv_prep", grid=(t // tm,),
        in_specs=[pl.BlockSpec((tm, 1536), lambda i: (i, 0)), pl.BlockSpec((1, D), lambda i: (0, 0)),
                  pl.BlockSpec((1, 256), lambda i: (0, 0))],
        out_specs=[pl.BlockSpec((tm, D), lambda i: (i, 0)), pl.BlockSpec((tm, KVX), lambda i: (i, 0))],
        out_shape=[SDS((t, D), BF16), SDS((t, KVX), BF16)], args=(u_qkv, qg_s, kg_t), vmem=VMEM_BIG)


def _band_bias():
    j, r = _iota2((2 * BLK, 2 * BLK))
    diff = (r & (BLK - 1)) - j + BLK
    band = (diff >= 0) & (diff < BLK)
    return jnp.stack([jnp.where(band & (j >= BLK), 0.0, NEG), jnp.where(band, 0.0, NEG)]).astype(F32)


def _pair_rows(ref_or_val, hk):
    return jnp.concatenate([ref_or_val[:, 256 * hk:256 * hk + 128], ref_or_val[:, 256 * hk + 128:256 * hk + 256]], axis=0)


def _sink_row(sink_ref, hk, half):
    return jnp.concatenate([jnp.full((1, BLK), sink_ref[0, GROUP * hk + half], F32),
                            jnp.full((1, BLK), sink_ref[0, GROUP * hk + 2 + half], F32)], axis=1)


def _kv_operands(kvb, hk, half):
    return (kvb[:, 512 * half + 128 * hk:512 * half + 128 * hk + 128],
            kvb[:, 1024 + 512 * half + 128 * hk:1024 + 512 * half + 128 * hk + 128])


def _attn_fwd(qs, kvx, u_za, sinks, bias, rider):
    t = qs.shape[0]
    nb = t // BLK

    def body(q_ref, kc_ref, kp_ref, za_ref, sink_ref, bias_ref, o_ref, lse_ref):
        kvb = jnp.concatenate([kp_ref[...], kc_ref[...]], axis=0)
        bias_v = bias_ref[...]
        key0 = lax.broadcasted_iota(jnp.int32, (2 * BLK, 1), 0) == 0
        ones = jnp.ones((2 * BLK, 128), BF16)
        cols = []
        for hk in range(N_KV):
            qpp = _pair_rows(q_ref, hk)
            opp = None
            for half in range(2):
                kx, vx = _kv_operands(kvb, hk, half)
                s = _dot_nt(kx, qpp) + bias_v
                sink = _sink_row(sink_ref, hk, half)
                m = jnp.maximum(jnp.max(s, axis=0, keepdims=True), sink)
                p = jnp.exp(s - m)
                es = jnp.exp(sink - m)
                lse_ref[0, 2 * hk + half:2 * hk + half + 1, :] = m + jnp.log(jnp.sum(p, axis=0, keepdims=True) + es)
                pe = jnp.where(key0, es, p).astype(BF16)
                rhs = jnp.concatenate([jnp.where(key0, jnp.zeros_like(vx), vx), ones], axis=1)
                nd = _dot_tn(pe, rhs)
                o = nd[:, :128] * (1.0 / nd[:, 128:])
                opp = o if opp is None else opp + o
            cols += [opp[:BLK], opp[BLK:]]
        za = za_ref[...].astype(F32)
        o_ref[...] = (jnp.concatenate(cols, axis=1) * (za * _sigmoid(za))).astype(BF16)

    prev = lambda n: jnp.maximum(n - 1, 0)
    (o, lse), got = _call(
        body, rider, name="attn_fwd", grid=(nb,),
        in_specs=[pl.BlockSpec((BLK, D), lambda n: (n, 0)),
                  pl.BlockSpec((BLK, KVX), lambda n: (n, 0)), pl.BlockSpec((BLK, KVX), lambda n: (prev(n), 0)),
                  pl.BlockSpec((BLK, D), lambda n: (n, 0)), pl.BlockSpec(memory_space=pltpu.SMEM),
                  pl.BlockSpec((None, 2 * BLK, 2 * BLK), lambda n: (jnp.minimum(n, 1), 0, 0))],
        out_specs=[pl.BlockSpec((BLK, D), lambda n: (n, 0)), pl.BlockSpec((1, 8, 2 * BLK), lambda n: (n, 0, 0))],
        out_shape=[SDS((t, D), BF16), SDS((nb, 8, 2 * BLK), F32)],
        args=(qs, kvx, kvx, u_za, sinks, bias), vmem=VMEM_BIG)
    return o, lse, got


def _out_proj_fwd(x, y_c, o, u_gl, gate_b, w_sm, rider):
    t = x.shape[0]
    tm = min(512, t)

    def body(x_ref, yc_ref, o_ref, gla_ref, glb_ref, gb_ref, wco_ref, wao_ref, wout_ref,
             xn_ref, ya_ref, yb_ref, mg_ref):
        ya = _dot(yc_ref[...], wco_ref[...])
        yb = _dot(o_ref[...], wao_ref[...])
        gb = gb_ref[...]
        ga_ = _sigmoid(gla_ref[...].astype(F32) + gb[:, :D])
        gb_ = _sigmoid(glb_ref[...].astype(F32) + gb[:, D:])
        merged = (ga_ * ya + gb_ * yb).astype(BF16)
        ya_ref[...] = ya.astype(BF16)
        yb_ref[...] = yb.astype(BF16)
        mg_ref[...] = merged
        xn_ref[...] = x_ref[...] + _dot(merged, wout_ref[...])

    row = pl.BlockSpec((tm, D), lambda i: (i, 0))
    wspec = lambda a: pl.BlockSpec((None, D, D), lambda i: (a, 0, 0))
    return _call(
        body, rider, name="out_proj_fwd", grid=(t // tm,),
        in_specs=[row, row, row, pl.BlockSpec((tm, D), lambda i: (i, 0)), pl.BlockSpec((tm, D), lambda i: (i, 1)),
                  pl.BlockSpec((1, 2 * D), lambda i: (0, 0)), wspec(0), wspec(1), wspec(2)],
        out_specs=[row, row, row, row],
        out_shape=[SDS((t, D), F32), SDS((t, D), BF16), SDS((t, D), BF16), SDS((t, D), BF16)],
        args=(x, y_c, o, u_gl, u_gl, gate_b, w_sm, w_sm, w_sm), vmem=VMEM_BIG)


def _loss_head(y, tgt):
    t = y.shape[0]
    tm = min(512, t)

    def body(y_ref, t_ref, dy_ref, acc_ref):
        @pl.when(pl.program_id(0) == 0)
        def _():
            acc_ref[...] = jnp.zeros_like(acc_ref)
        err = y_ref[...] - t_ref[...]
        dy_ref[...] = err * (1.0 / D)
        sq = _fold8(err * err)
        tot = sq[:, 0:128]
        for k in range(1, D // 128):
            tot = tot + sq[:, 128 * k:128 * (k + 1)]
        acc_ref[...] += tot

    row = pl.BlockSpec((tm, D), lambda i: (i, 0))
    return pl.pallas_call(
        body, name="loss_head", grid=(t // tm,), in_specs=[row, row],
        out_specs=[row, pl.BlockSpec((8, 128), lambda i: (0, 0))],
        out_shape=[SDS((t, D), F32), SDS((8, 128), F32)], compiler_params=_cp("arbitrary"),
    )(y, tgt)


def _out_proj_bwd(dout, y_a, y_b, u_gl, gate_b, w_sm, rider):
    t = dout.shape[0]
    tm = min(512, t)

    def body(do_ref, ya_ref, yb_ref, gla_ref, glb_ref, gb_ref, wco_ref, wao_ref, wout_ref,
             dya_ref, dyb_ref, dgl_ref, dyc_ref, dob_ref, dgb_ref):
        @pl.when(pl.program_id(0) == 0)
        def _():
            dgb_ref[...] = jnp.zeros_like(dgb_ref)
        dm = _dot_nt(do_ref[...].astype(BF16), wout_ref[...])
        gb = gb_ref[...]
        ga_ = _sigmoid(gla_ref[...].astype(F32) + gb[:, :D])
        gb_ = _sigmoid(glb_ref[...].astype(F32) + gb[:, D:])
        dya = (ga_ * dm).astype(BF16)
        dyb = (gb_ * dm).astype(BF16)
        dgla = ya_ref[...].astype(F32) * dm * (ga_ * (1.0 - ga_))
        dglb = yb_ref[...].astype(F32) * dm * (gb_ * (1.0 - gb_))
        dya_ref[...] = dya
        dyb_ref[...] = dyb
        dgl_ref[:, :D] = dgla.astype(BF16)
        dgl_ref[:, D:] = dglb.astype(BF16)
        dgb_ref[:, :D] += _fold8(dgla)
        dgb_ref[:, D:] += _fold8(dglb)
        dyc_ref[...] = _dot_nt(dya, wco_ref[...]).astype(BF16)
        dob_ref[...] = _dot_nt(dyb, wao_ref[...]).astype(BF16)

    row = pl.BlockSpec((tm, D), lambda i: (i, 0))
    wspec = lambda a: pl.BlockSpec((None, D, D), lambda i: (a, 0, 0))
    return _call(
        body, rider, name="out_proj_bwd", grid=(t // tm,),
        in_specs=[row, row, row, pl.BlockSpec((tm, D), lambda i: (i, 0)), pl.BlockSpec((tm, D), lambda i: (i, 1)),
                  pl.BlockSpec((1, 2 * D), lambda i: (0, 0)), wspec(0), wspec(1), wspec(2)],
        out_specs=[row, row, pl.BlockSpec((tm, 2 * D), lambda i: (i, 0)), row, row,
                   pl.BlockSpec((8, 2 * D), lambda i: (0, 0))],
        out_shape=[SDS((t, D), BF16), SDS((t, D), BF16), SDS((t, 2 * D), BF16), SDS((t, D), BF16), SDS((t, D), BF16),
                   SDS((8, 2 * D), F32)],
        args=(dout, y_a, y_b, u_gl, u_gl, gate_b, w_sm, w_sm, w_sm), vmem=VMEM_BIG)


def _small_wgrads(y_c, d_ya, o, d_yb, merged, dout):
    t = y_c.shape[0]
    tk = min(512, t)

    def body(yc_ref, dya_ref, o_ref, dyb_ref, mg_ref, do_ref, g_ref):
        @pl.when(pl.program_id(0) == 0)
        def _():
            g_ref[...] = jnp.zeros_like(g_ref)
        g_ref[0] += _dot_tn(yc_ref[...], dya_ref[...])
        g_ref[1] += _dot_tn(o_ref[...], dyb_ref[...])
        g_ref[2] += _dot_tn(mg_ref[...], do_ref[...].astype(BF16))

    row = pl.BlockSpec((tk, D), lambda k: (k, 0))
    return pl.pallas_call(
        body, name="small_wgrads", grid=(t // tk,), in_specs=[row] * 6,
        out_specs=pl.BlockSpec((3, D, D), lambda k: (0, 0, 0)), out_shape=SDS((3, D, D), F32),
        compiler_params=_cp("arbitrary", vmem=VMEM_BIG),
    )(y_c, d_ya, o, d_yb, merged, dout)


def _conv_bwd(d_yc, u_conv, conv_w, rider):
    t = d_yc.shape[0]
    tm = min(256, t)
    hb = tm // 16
    last_halo = t // 16 - 1
    n_steps = t // tm

    def body(dy_ref, v_ref, b_ref, c_ref, z_ref, hv_ref, hc_ref, ndy_ref, nb_ref, nz_ref, w_ref, du_ref, dw_ref):
        i = pl.program_id(0)

        @pl.when(i == 0)
        def _():
            dw_ref[...] = jnp.zeros_like(dw_ref)
        v, c = v_ref[...].astype(F32), c_ref[...].astype(F32)
        b, z = b_ref[...].astype(F32), z_ref[...].astype(F32)
        cv = c * v
        halo = jnp.where(i > 0, hc_ref[...].astype(F32) * hv_ref[...].astype(F32), 0.0)
        row = lax.broadcasted_iota(jnp.int32, (tm, 1), 0)
        s1 = jnp.where(row == 0, halo[15:16], pltpu.roll(cv, 1, 0))
        s2 = jnp.where(row == 0, halo[14:15], jnp.where(row == 1, halo[15:16], pltpu.roll(cv, 2, 0)))
        w0, w1, w2 = w_ref[0:1, :], w_ref[1:2, :], w_ref[2:3, :]
        conv = w0 * s2 + w1 * s1 + w2 * cv
        sig = _sigmoid(z)
        sz = z * sig
        dsz = sig * (1.0 + z * (1.0 - sig))
        dy = dy_ref[...].astype(F32)
        dconv = dy * b * sz
        nz = nz_ref[...].astype(F32)
        nxt = ndy_ref[...].astype(F32) * nb_ref[...].astype(F32) * (nz * _sigmoid(nz))
        nxt = jnp.where(i < n_steps - 1, nxt, 0.0)
        a1 = jnp.where(row == tm - 1, nxt[0:1], pltpu.roll(dconv, tm - 1, 0))
        a2 = jnp.where(row == tm - 2, nxt[0:1], jnp.where(row == tm - 1, nxt[1:2], pltpu.roll(dconv, tm - 2, 0)))
        dcv = w2 * dconv + w1 * a1 + w0 * a2
        du_ref[:, 0:D] = (dcv * c).astype(BF16)
        du_ref[:, D:2 * D] = (dy * conv * sz).astype(BF16)
        du_ref[:, 2 * D:3 * D] = (dcv * v).astype(BF16)
        du_ref[:, 3 * D:4 * D] = (dy * b * conv * dsz).astype(BF16)
        r8 = lax.broadcasted_iota(jnp.int32, (8, 1), 0)
        dw_ref[...] += jnp.where(r8 == 0, jnp.sum(dconv * s2, axis=0, keepdims=True),
                                 jnp.where(r8 == 1, jnp.sum(dconv * s1, axis=0, keepdims=True),
                                           jnp.where(r8 == 2, jnp.sum(dconv * cv, axis=0, keepdims=True), 0.0)))

    def col(k):
        return pl.BlockSpec((tm, D), lambda i: (i, k))

    def halo(k):
        return pl.BlockSpec((16, D), lambda i: (jnp.maximum(i * hb - 1, 0), k))

    def nxt(k):
        return pl.BlockSpec((16, D), lambda i: (jnp.minimum((i + 1) * hb, last_halo), k))

    return _call(
        body, rider, name="conv_bwd", grid=(t // tm,),
        in_specs=[col(0), col(0), col(1), col(2), col(3), halo(0), halo(2), nxt(0), nxt(1), nxt(3),
                  pl.BlockSpec((3, D), lambda i: (0, 0))],
        out_specs=[pl.BlockSpec((tm, 4 * D), lambda i: (i, 0)), pl.BlockSpec((8, D), lambda i: (0, 0))],
        out_shape=[SDS((t, 4 * D), BF16), SDS((8, D), F32)],
        args=(d_yc, u_conv, u_conv, u_conv, u_conv, u_conv, u_conv, d_yc, u_conv, u_conv, conv_w), vmem=VMEM_BIG)


def _attn_bwd(d_o, qs, kvx, u_za, lse, sinks, bias, rider):
    t = d_o.shape[0]
    nb = t // BLK

    def body(q_ref, kc_ref, kp_ref, za_ref, do_ref, lse_ref, sink_ref, bias_ref,
             dq_ref, dkv_ref, dza_ref, dsk_ref, carry_ref):
        n = pl.program_id(0)

        @pl.when(n == 0)
        def _():
            carry_ref[...] = jnp.zeros_like(carry_ref)
            dsk_ref[...] = jnp.zeros_like(dsk_ref)

        live = n < nb
        kvb = jnp.concatenate([kp_ref[...], kc_ref[...]], axis=0)
        bias_v = bias_ref[...]
        za = za_ref[...].astype(F32)
        sig = _sigmoid(za)
        dsa = sig * (1.0 + za * (1.0 - sig))
        do = jnp.where(live, do_ref[...].astype(F32), 0.0)
        dattn = (do * (za * sig)).astype(BF16)
        lo_lanes = lax.broadcasted_iota(jnp.int32, (1, 128), 1) < HEAD
        dq_cols, attn_cols, dk_cols, dv_cols, dsk_rows = [], [], [], [], []
        for hk in range(N_KV):
            qpp = _pair_rows(q_ref, hk)
            dapp = _pair_rows(dattn, hk)
            probs, dss, xk, xv = [], [], [], []
            for half in range(2):
                kx, vx = _kv_operands(kvb, hk, half)
                lse = lse_ref[0, 2 * hk + half:2 * hk + half + 1, :]
                prob = jnp.exp(_dot_nt(kx, qpp) + bias_v - lse)
                psink = jnp.exp(_sink_row(sink_ref, hk, half) - lse)
                tdp = prob * _dot_nt(vx, dapp)
                drow = jnp.sum(tdp, axis=0, keepdims=True)
                ds = (tdp - prob * drow).astype(BF16)
                prob_b = prob.astype(BF16)
                xk.append(_dot(ds, qpp))
                xv.append(_dot(prob_b, dapp))
                probs.append(prob_b)
                dss.append(ds)
                dsk_rows.append(-psink * drow)
            kcat = jnp.concatenate([kvb[:, 128 * hk:128 * hk + 128], kvb[:, 512 + 128 * hk:512 + 128 * hk + 128]], axis=0)
            vcat = jnp.concatenate([kvb[:, 1024 + 128 * hk:1024 + 128 * hk + 128],
                                    kvb[:, 1536 + 128 * hk:1536 + 128 * hk + 128]], axis=0)
            app = _dot_tn(jnp.concatenate(probs, axis=0), vcat)
            dqpp = _dot_tn(jnp.concatenate(dss, axis=0), kcat)
            dq_cols += [dqpp[:BLK], dqpp[BLK:]]
            attn_cols += [app[:BLK], app[BLK:]]
            dk_cols.append(jnp.where(lo_lanes, xk[0], xk[1]))
            dv_cols.append(jnp.where(lo_lanes, xv[0], xv[1]))

        @pl.when(live)
        def _():
            dq_ref[...] = jnp.concatenate(dq_cols, axis=1).astype(BF16)
            dza_ref[...] = (do * jnp.concatenate(attn_cols, axis=1) * dsa).astype(BF16)

        band = jnp.concatenate(dk_cols + dv_cols, axis=1)
        dkv_ref[...] = (band[:BLK] + carry_ref[...]).astype(BF16)
        carry_ref[...] = band[BLK:]
        dsk_ref[...] += jnp.broadcast_to(jnp.concatenate(dsk_rows, axis=1), (8, 2 * N_KV * 2 * BLK))

    cur = lambda n: jnp.minimum(n, nb - 1)
    prev = lambda n: jnp.maximum(n - 1, 0)
    return _call(
        body, rider, name="attn_bwd", grid=(nb + 1,),
        in_specs=[pl.BlockSpec((BLK, D), lambda n: (cur(n), 0)),
                  pl.BlockSpec((BLK, KVX), lambda n: (cur(n), 0)), pl.BlockSpec((BLK, KVX), lambda n: (prev(n), 0)),
                  pl.BlockSpec((BLK, D), lambda n: (cur(n), 0)), pl.BlockSpec((BLK, D), lambda n: (cur(n), 0)),
                  pl.BlockSpec((1, 8, 2 * BLK), lambda n: (cur(n), 0, 0)), pl.BlockSpec(memory_space=pltpu.SMEM),
                  pl.BlockSpec((None, 2 * BLK, 2 * BLK), lambda n: (jnp.minimum(n, 1), 0, 0))],
        out_specs=[pl.BlockSpec((BLK, D), lambda n: (cur(n), 0)), pl.BlockSpec((BLK, D), lambda n: (prev(n), 0)),
                   pl.BlockSpec((BLK, D), lambda n: (cur(n), 0)), pl.BlockSpec((8, 2 * D), lambda n: (0, 0))],
        out_shape=[SDS((t, D), BF16), SDS((t, D), BF16), SDS((t, D), BF16), SDS((8, 2 * D), F32)],
        scratch_shapes=[pltpu.VMEM((BLK, D), F32)],
        args=(qs, kvx, kvx, u_za, d_o, lse, sinks, bias), vmem=VMEM_BIG)


def _qkv_post(u_qkv, dqs, dkv, dza, qg_s, kg_t, rider):
    t = u_qkv.shape[0]
    tm = min(512, t)

    def norm_bwd(x, dy, g):
        r = lax.rsqrt(_head_sum(x * x) * (1.0 / HEAD) + EPS)
        xhat = x * r
        dxh = dy * g
        return r * (dxh - xhat * (_head_sum(dxh * xhat) * (1.0 / HEAD))), _fold8(dy * xhat)

    def body(u_ref, dq_ref, dkv_ref, dza_ref, qg_ref, kg_ref, du_ref, dqg_ref, dkg_ref):
        @pl.when(pl.program_id(0) == 0)
        def _():
            dqg_ref[...] = jnp.zeros_like(dqg_ref)
            dkg_ref[...] = jnp.zeros_like(dkg_ref)
        dq, gq = norm_bwd(u_ref[:, 0:D].astype(F32), dq_ref[...].astype(F32), qg_ref[...])
        fold = _fold_mat()
        dk, gk = norm_bwd(u_ref[:, D:D + 256].astype(F32), _dot(dkv_ref[:, 0:512], fold), kg_ref[...])
        du_ref[:, 0:D] = dq.astype(BF16)
        du_ref[:, D:D + 256] = dk.astype(BF16)
        du_ref[:, D + 256:D + 512] = _dot(dkv_ref[:, 512:1024], fold).astype(BF16)
        du_ref[:, D + 512:2 * D + 512] = dza_ref[...]
        dqg_ref[...] += gq
        dkg_ref[...] += gk

    row = pl.BlockSpec((tm, D), lambda i: (i, 0))
    return _call(
        body, rider, name="qkv_post", grid=(t // tm,),
        in_specs=[pl.BlockSpec((tm, 1536), lambda i: (i, 0)), row, row, row,
                  pl.BlockSpec((1, D), lambda i: (0, 0)), pl.BlockSpec((1, 256), lambda i: (0, 0))],
        out_specs=[pl.BlockSpec((tm, 2560), lambda i: (i, 0)), pl.BlockSpec((8, D), lambda i: (0, 0)),
                   pl.BlockSpec((8, 256), lambda i: (0, 0))],
        out_shape=[SDS((t, 2560), BF16), SDS((8, D), F32), SDS((8, 256), F32)],
        args=(u_qkv, dqs, dkv, dza, qg_s, kg_t), vmem=VMEM_BIG)


N_GRAN = N_IN // CB
DU_COLS = ((0, 4096), (4096, 6656), (6656, N_IN))


def _du_granule(j):
    return jnp.clip(j, 0, 7), jnp.clip(j - 8, 0, 4), jnp.clip(j - 13, 0, 3)


def _du_select(j, refs, fn):
    for ref, lo, hi in zip(refs, (0, 8, 13), (8, 13, 17)):
        @pl.when((j >= lo) & (j < hi))
        def _():
            fn(ref)


def _in_proj_bwd(du, w_full, rider):
    t = du[0].shape[0]
    tn = min(512, t)

    def body(a0, a1, a2, w_hbm, dht_ref, w_ref, sem):
        @pl.when(pl.program_id(0) == 0)
        def _():
            cp = pltpu.make_async_copy(w_hbm, w_ref, sem)
            cp.start()
            cp.wait()
        acc = None
        for a_ref, (lo, hi) in zip((a0, a1, a2), DU_COLS):
            part = _dot_nt(w_ref[:, lo:hi], a_ref[...])
            acc = part if acc is None else acc + part
        dht_ref[...] = acc

    (dht,), got = _call(
        body, rider, name="in_proj_bwd", grid=(t // tn,),
        in_specs=[pl.BlockSpec((tn, hi - lo), lambda i: (i, 0)) for lo, hi in DU_COLS] + [ANY],
        out_specs=[pl.BlockSpec((D, tn), lambda i: (0, i))], out_shape=[SDS((D, t), F32)],
        scratch_shapes=[pltpu.VMEM((D, N_IN), BF16), pltpu.SemaphoreType.DMA(())], args=(*du, w_full), vmem=VMEM_BIG)
    return dht, got


def _rmsnorm_bwd(dht, x, g, dout):
    t = x.shape[0]
    tm = min(256, t)

    def body(dht_ref, x_ref, g_ref, do_ref, dx_ref, dg_ref):
        @pl.when(pl.program_id(0) == 0)
        def _():
            dg_ref[...] = jnp.zeros_like(dg_ref)
        dh = dht_ref[...].T
        xv = x_ref[...]
        r = lax.rsqrt(jnp.mean(xv * xv, axis=-1, keepdims=True) + EPS)
        xhat = xv * r
        dg_ref[...] += _fold8(dh * xhat)
        dxh = dh * g_ref[...]
        dx_ref[...] = do_ref[...] + r * (dxh - xhat * jnp.mean(dxh * xhat, axis=-1, keepdims=True))

    row = pl.BlockSpec((tm, D), lambda i: (i, 0))
    return pl.pallas_call(
        body, name="rmsnorm_bwd", grid=(t // tm,),
        in_specs=[pl.BlockSpec((D, tm), lambda i: (0, i)), row, pl.BlockSpec((1, D), lambda i: (0, 0)), row],
        out_specs=[row, pl.BlockSpec((8, D), lambda i: (0, 0))],
        out_shape=[SDS((t, D), F32), SDS((8, D), F32)], compiler_params=_cp("arbitrary", vmem=VMEM_BIG),
    )(dht, x, g, dout)


def _in_proj_wgrad(ht, du):
    t = ht.shape[1]
    tk = min(4096, t)
    n_k = t // tk

    def body(h_ref, b0, b1, b2, g_ref):
        j, k = pl.program_id(0), pl.program_id(1)

        if n_k > 1:
            @pl.when(k == 0)
            def _():
                g_ref[...] = jnp.zeros_like(g_ref)

        def add(b_ref):
            if n_k > 1:
                g_ref[...] += _dot(h_ref[...], b_ref[...])
            else:
                g_ref[...] = _dot(h_ref[...], b_ref[...])
        _du_select(j, (b0, b1, b2), add)

    seg = lambda q: pl.BlockSpec((tk, CB), lambda j, k: (k, _du_granule(j)[q]))
    return pl.pallas_call(
        body, name="in_proj_wgrad", grid=(N_GRAN, t // tk),
        in_specs=[pl.BlockSpec((D, tk), lambda j, k: (0, k)), seg(0), seg(1), seg(2)],
        out_specs=pl.BlockSpec((D, CB), lambda j, k: (0, j)), out_shape=SDS((D, N_IN), F32),
        compiler_params=_cp("parallel", "arbitrary", vmem=VMEM_BIG),
    )(ht, *du)


def _swap_rider(g_in, g_sm):
    def copies(ins, outs, send, recv, base=0):
        x, y, c = _mesh_pos()
        cps = []
        for src, dst in zip(ins, outs):
            half = src.at[1 - c] if len(src.shape) == 3 else src.at[:, :, 1 - c]
            cps.append(_rcopy(half, dst, send, recv, base + len(cps), (x, y, 1 - c)))
        return cps

    arrays = [g for g in (g_in, g_sm) if g is not None]
    shapes = [SDS((512, N_IN), F32) if len(g.shape) == 3 else SDS((3, 4, 128, D), F32) for g in arrays]
    return _Rider(arrays, shapes, len(arrays), copies)


def _add_halves_in(c_idx, g_in, r_in):
    def body(c_ref, a_ref, b_ref, f_ref, h_ref):
        s = a_ref[...] + b_ref[...]
        f_ref[...] = s
        h_ref[...] = s.astype(BF16)

    blk = pl.BlockSpec((128, N_IN), lambda i, c: (i, 0))
    return pl.pallas_call(
        body, name="add_halves_in",
        grid_spec=pltpu.PrefetchScalarGridSpec(
            num_scalar_prefetch=1, grid=(4,),
            in_specs=[pl.BlockSpec((None, 128, N_IN), lambda i, c: (c[0], i, 0)), blk], out_specs=[blk, blk]),
        out_shape=[SDS((512, N_IN), F32), SDS((512, N_IN), BF16)], compiler_params=_cp("parallel", vmem=VMEM_BIG),
    )(c_idx, g_in, r_in)


def _add_halves_sm(c_idx, g_sm, r_sm):
    def body(c_ref, a_ref, b_ref, f_ref, h_ref):
        s = a_ref[...] + b_ref[...]
        f_ref[...] = s
        h_ref[...] = s.astype(BF16)

    blk = pl.BlockSpec((1, 4, 128, D), lambda a, c: (a, 0, 0, 0))
    return pl.pallas_call(
        body, name="add_halves_sm",
        grid_spec=pltpu.PrefetchScalarGridSpec(
            num_scalar_prefetch=1, grid=(3,),
            in_specs=[pl.BlockSpec((1, 4, None, 128, D), lambda a, c: (a, 0, c[0], 0, 0)), blk], out_specs=[blk, blk]),
        out_shape=[SDS((3, 4, 128, D), F32), SDS((3, 4, 128, D), BF16)], compiler_params=_cp("parallel"),
    )(c_idx, g_sm, r_sm)


def _scatter_rider(h_in, h_sm):
    def copies(ins, outs, send, recv, base=0):
        x, y, c = _mesh_pos()
        cps = []
        for src, dst in zip(ins, outs):
            for k, chip in enumerate(_other_chips(x, y)):
                their = 2 * chip[0] + chip[1]
                part = src.at[:, pl.ds(pl.multiple_of(their * SH_IN, 128), SH_IN)] if len(src.shape) == 2 else src.at[:, their]
                cps.append(_rcopy(part, dst.at[k], send, recv, base + len(cps), (*chip, c)))
        return cps

    arrays = [h for h in (h_in, h_sm) if h is not None]
    shapes = [SDS((3, 512, SH_IN), BF16) if len(h.shape) == 2 else SDS((3, 3, 128, D), BF16) for h in arrays]
    return _Rider(arrays, shapes, 3 * len(arrays), copies)


def _ride_alone(rider, name):
    return _hosted_call(None, rider, name=name, grid=(), in_specs=[], out_specs=[], out_shape=[], args=())[1]


def _final_sum_in(chip_idx, f_in, r_in):
    def body(j_ref, a_ref, r_ref, o_ref):
        o_ref[...] = a_ref[...] + r_ref[0].astype(F32) + r_ref[1].astype(F32) + r_ref[2].astype(F32)

    return pl.pallas_call(
        body, name="final_sum_in",
        grid_spec=pltpu.PrefetchScalarGridSpec(
            num_scalar_prefetch=1, grid=(4,),
            in_specs=[pl.BlockSpec((128, SH_IN), lambda i, j: (i, j[0])), pl.BlockSpec((3, 128, SH_IN), lambda i, j: (0, i, 0))],
            out_specs=pl.BlockSpec((128, SH_IN), lambda i, j: (i, 0))),
        out_shape=SDS((512, SH_IN), F32), compiler_params=_cp("parallel"),
    )(chip_idx, f_in, r_in)


def _final_sum_sm(chip_idx, f_sm, r_sm):
    def body(j_ref, a_ref, r_ref, o_ref):
        o_ref[...] = a_ref[...] + r_ref[0].astype(F32) + r_ref[1].astype(F32) + r_ref[2].astype(F32)

    return pl.pallas_call(
        body, name="final_sum_sm",
        grid_spec=pltpu.PrefetchScalarGridSpec(
            num_scalar_prefetch=1, grid=(3,),
            in_specs=[pl.BlockSpec((1, None, 128, D), lambda a, j: (a, j[0], 0, 0)),
                      pl.BlockSpec((3, 1, 128, D), lambda a, j: (0, a, 0, 0))],
            out_specs=pl.BlockSpec((1, 128, D), lambda a, j: (a, 0, 0))),
        out_shape=SDS((3, 128, D), F32), compiler_params=_cp("parallel"),
    )(chip_idx, f_sm, r_sm)


def _join_halves(t_in, t_sm):
    n_cp = N_LAYERS * 4

    def body(*refs):
        ins, outs = refs[:2 * N_LAYERS], refs[2 * N_LAYERS:2 * N_LAYERS + 4]
        send, recv, loc_in, loc_out, stage_in, stage_sm = refs[2 * N_LAYERS + 4:]
        x, y, c = _mesh_pos()
        cps, own = [], []
        for l in range(N_LAYERS):
            for a in range(4):
                s = 4 * l + a
                if a == 0:
                    src = ins[2 * l]
                    dst = outs[0].at[l, pl.ds(pl.multiple_of(c * 512, 512), 512), :]
                else:
                    src = ins[2 * l + 1].at[a - 1]
                    dst = outs[a].at[l, pl.ds(pl.multiple_of(c * 128, 128), 128), :]
                own.append((src, dst, min(a, 1)))
                cp = pltpu.make_async_remote_copy(src_ref=src, dst_ref=dst, send_sem=send.at[s], recv_sem=recv.at[s],
                                                  device_id=(x, y, 1 - c), device_id_type=MESH)
                cp.start()
                cps.append(cp)
        _staged_copies(own, (stage_in, stage_sm), loc_in, loc_out)
        for l in range(N_LAYERS):
            for a in range(4):
                s = 4 * l + a
                if a == 0:
                    got = outs[0].at[l, pl.ds(pl.multiple_of((1 - c) * 512, 512), 512), :]
                else:
                    got = outs[a].at[l, pl.ds(pl.multiple_of((1 - c) * 128, 128), 128), :]
                pltpu.make_async_remote_copy(src_ref=got, dst_ref=got, send_sem=send.at[s], recv_sem=recv.at[s],
                                             device_id=(x, y, 1 - c), device_id_type=MESH).wait_recv()
        for cp in cps:
            cp.wait_send()

    args = []
    for l in range(N_LAYERS):
        args += [t_in[l], t_sm[l]]
    sm = SDS((N_LAYERS, SH_ROW, D), F32)
    return pl.pallas_call(
        body, name="join_halves", in_specs=[ANY] * (2 * N_LAYERS), out_specs=[ANY] * 4,
        out_shape=[SDS((N_LAYERS, D, SH_IN), F32), sm, sm, sm],
        scratch_shapes=[pltpu.SemaphoreType.DMA((n_cp,))] * 4
        + [pltpu.VMEM((2, 512, SH_IN), F32), pltpu.VMEM((2, 128, D), F32)],
        compiler_params=_cp(vmem=VMEM_BIG),
    )(*args)


def _adam_math(w, g, m, v):
    m = ADAM_B1 * m + (1.0 - ADAM_B1) * g
    v = ADAM_B2 * v + (1.0 - ADAM_B2) * (g * g)
    m_hat = m / (1.0 - ADAM_B1 ** ADAM_STEP)
    v_hat = v / (1.0 - ADAM_B2 ** ADAM_STEP)
    delta = -ADAM_LR * (m_hat / (jnp.sqrt(v_hat) + ADAM_EPS) + ADAM_WD * w)
    return delta, m, v


def _adamw_big(w, g, m, v, name):
    rows, cols = w.shape
    tr = 128

    def body(w_ref, g_ref, m_ref, v_ref, d_ref, nm_ref, nv_ref):
        d_ref[...], nm_ref[...], nv_ref[...] = _adam_math(w_ref[...], g_ref[...], m_ref[...], v_ref[...])

    blk = pl.BlockSpec((tr, cols), lambda i: (i, 0))
    return pl.pallas_call(
        body, name=name, grid=(rows // tr,), in_specs=[blk] * 4, out_specs=[blk] * 3,
        out_shape=[SDS((rows, cols), F32)] * 3, compiler_params=_cp("parallel", vmem=VMEM_BIG),
    )(w, g, m, v)


def _adamw_small(ws, gs, ms, vs):
    n = len(ws)

    def body(*refs):
        for k in range(n):
            w_ref, g_ref, m_ref, v_ref = (refs[q * n + k] for q in range(4))
            d, nm, nv = _adam_math(w_ref[...], g_ref[...], m_ref[...], v_ref[...])
            refs[4 * n + k][...] = d
            refs[5 * n + k][...] = nm
            refs[6 * n + k][...] = nv

    vm = pl.BlockSpec(memory_space=pltpu.VMEM)
    shapes = [SDS(w.shape, F32) for w in ws]
    res = pl.pallas_call(
        body, name="adamw_small", in_specs=[vm] * (4 * n), out_specs=[vm] * (3 * n), out_shape=shapes * 3,
    )(*ws, *gs, *ms, *vs)
    return res[:n], res[n:2 * n], res[2 * n:]


def _pad_rows(a, rows):
    flat = a.reshape(-1)
    return jnp.pad(flat, (0, rows * 128 - flat.shape[0])).reshape(rows, 128)


def kernel(x, norm_g, w_in, conv_w, q_norm_g, k_norm_g, sinks, w_conv_out, w_attn_out, gate_b, w_out, loss_target, m_norm_g, m_w_in, m_conv_w, m_q_norm_g, m_k_norm_g, m_sinks, m_w_conv_out, m_w_attn_out, m_gate_b, m_w_out, v_norm_g, v_w_in, v_conv_w, v_q_norm_g, v_k_norm_g, v_sinks, v_w_conv_out, v_w_attn_out, v_gate_b, v_w_out):
    xi, yi, ci = _mesh_pos()
    chip = 2 * xi + yi
    c_idx = jnp.reshape(ci, (1,)).astype(jnp.int32)
    chip_idx = jnp.reshape(chip, (1,)).astype(jnp.int32)
    t = x.shape[1]
    xs = [x.reshape(t, D)]
    tgt = loss_target.reshape(t, D)

    full_w = [[_cast_w_in(chip_idx, w_in, l), _cast_w_small(chip_idx, w_conv_out, w_attn_out, w_out, l)]
              for l in range(N_LAYERS)]
    full_w[0][0] = _ride_alone(_gather_rider(full_w[0][:1], "A"), "gather_first_ici")[0]
    full_w[0][0] = _ride_alone(_gather_rider(full_w[0][:1], "B"), "gather_first_d2d")[0]
    placed = lax.dynamic_update_slice(jnp.zeros((N_LAYERS, 3, D), F32),
                                      jnp.where(ci == 0, conv_w, 0.0), (0, 0, chip * SH_ROW))
    conv_full = _allreduce_small(placed.reshape(96, 128)).reshape(N_LAYERS, 3, D)

    qg_s = jnp.tile(q_norm_g, (1, N_Q)) * SCALE
    kg_t = jnp.tile(k_norm_g, (1, N_KV))
    bias = _band_bias()
    saved = []
    for l in range(N_LAYERS):
        nxt = full_w[l + 1] if l + 1 < N_LAYERS else None
        h, ht = _rmsnorm_fwd(xs[l], norm_g[l:l + 1])
        (u_conv, u_qkv, u_za, u_gl), got = _in_proj(h, full_w[l][0], _gather_rider(nxt[:1], "A") if nxt else None)
        if nxt:
            nxt[0] = got[0]
        y_c = _conv_fwd(u_conv, conv_full[l])
        (qs, kvx), got = _qkv_prep(u_qkv, qg_s[l:l + 1], kg_t[l:l + 1],
                                   _gather_rider(full_w[0][1:], "A") if l == 0 else None)
        if l == 0:
            full_w[0][1] = _ride_alone(_gather_rider(got, "B"), "gather_first_small_d2d")[0]
        o, lse, got = _attn_fwd(qs, kvx, u_za, sinks[l:l + 1], bias, _gather_rider(nxt[1:], "A") if nxt else None)
        if nxt:
            nxt[1] = got[0]
        (x_next, y_a, y_b, merged), got = _out_proj_fwd(xs[l], y_c, o, u_gl, gate_b[l:l + 1], full_w[l][1],
                                                        _gather_rider(nxt, "B") if nxt else None)
        if nxt:
            nxt[0], nxt[1] = got
        xs.append(x_next)
        saved.append((ht, u_conv, u_qkv, u_za, u_gl, y_c, o, y_a, y_b, merged, qs, kvx, lse))

    dout, sq = _loss_head(xs[N_LAYERS], tgt)
    loss = lax.psum(jnp.sum(sq) * (0.5 / D), ("x", "y", "c"))

    small, t_in, t_sm = [None] * N_LAYERS, [None] * N_LAYERS, [None] * N_LAYERS
    grads = None
    halves = None

    def add_halves(g, got):
        f_in, h_in = _add_halves_in(c_idx, g[0], got[0])
        f_sm, h_sm = _add_halves_sm(c_idx, g[1], got[1])
        return f_in, h_in, f_sm, h_sm

    for l in reversed(range(N_LAYERS)):
        w_full, w_sm = full_w[l]
        last = l == 0
        ht, u_conv, u_qkv, u_za, u_gl, y_c, o, y_a, y_b, merged, qs, kvx, lse = saved[l]
        (d_ya, d_yb, du_gl, d_yc, d_o, dgb), got = _out_proj_bwd(dout, y_a, y_b, u_gl, gate_b[l:l + 1], w_sm,
                                                                 _swap_rider(*grads) if grads else None)
        if grads:
            halves = add_halves(grads, got)
        g_sm = _small_wgrads(y_c, d_ya, o, d_yb, merged, dout).reshape(3, 4, 2, 128, D)
        (du_conv, dcw), got = _conv_bwd(d_yc, u_conv, conv_full[l], _merge_riders(
            _scatter_rider(None, halves[3]) if halves else None, _swap_rider(None, g_sm) if last else None))
        if halves:
            t_sm[l + 1] = _final_sum_sm(chip_idx, halves[2], got[0])
        if last:
            f_sm0, h_sm0 = _add_halves_sm(c_idx, g_sm, got[-1])
        (dqs, dkv, dza, dsk), got = _attn_bwd(d_o, qs, kvx, u_za, lse, sinks[l:l + 1], bias,
                                              _scatter_rider(halves[1], None) if halves else None)
        if halves:
            t_in[l + 1] = _final_sum_in(chip_idx, halves[0], got[0])
        dsk = jnp.sum(dsk[0].reshape(N_KV, 2, 2, BLK), axis=-1).transpose(0, 2, 1).reshape(N_Q)
        (du_attn, dqg, dkg), got = _qkv_post(u_qkv, dqs, dkv, dza, qg_s[l:l + 1], kg_t[l:l + 1],
                                             _scatter_rider(None, h_sm0) if last else None)
        if last:
            t_sm[0] = _final_sum_sm(chip_idx, f_sm0, got[0])
        du = (du_conv, du_attn, du_gl)
        grads = (_in_proj_wgrad(ht, du).reshape(2, 512, N_IN), g_sm)
        if last:
            f_in0, h_in0 = _add_halves_in(c_idx, grads[0], _ride_alone(_swap_rider(grads[0], None), "swap_last")[0])
        dh, got = _in_proj_bwd(du, w_full, _scatter_rider(h_in0, None) if last else None)
        if last:
            t_in[0] = _final_sum_in(chip_idx, f_in0, got[0])
        dout, dng = _rmsnorm_bwd(dh, xs[l], norm_g[l:l + 1], dout)
        small[l] = (jnp.sum(dng, axis=0), SCALE * jnp.sum(dqg.reshape(8 * N_Q, HEAD), axis=0),
                    jnp.sum(dkg.reshape(8 * N_KV, HEAD), axis=0), dsk, jnp.sum(dgb, axis=0), dcw[:3])
    grad_x = dout.reshape(1, t, D)

    stack = lambda k: jnp.stack([small[l][k] for l in range(N_LAYERS)])
    pack = jnp.concatenate([_pad_rows(stack(0), 32), _pad_rows(stack(1), 8), _pad_rows(stack(2), 8),
                            _pad_rows(stack(3), 8), _pad_rows(stack(4), 64), _pad_rows(stack(5), 96)], axis=0)
    red = _allreduce_small(pack)
    g_norm_g = red[0:32].reshape(N_LAYERS, D)
    g_q_norm_g = red[32:40].reshape(-1)[:N_LAYERS * HEAD].reshape(N_LAYERS, HEAD)
    g_k_norm_g = red[40:48].reshape(-1)[:N_LAYERS * HEAD].reshape(N_LAYERS, HEAD)
    g_sinks = red[48:56].reshape(-1)[:N_LAYERS * N_Q].reshape(N_LAYERS, N_Q)
    g_gate_b = red[56:120].reshape(N_LAYERS, 2 * D)
    g_conv_full = red[120:216].reshape(N_LAYERS, 3, D)
    g_conv_w = lax.dynamic_slice(g_conv_full, (0, 0, chip * SH_ROW), (N_LAYERS, 3, SH_ROW))

    g_w_in, g_w_co, g_w_ao, g_w_out = _join_halves(t_in, t_sm)

    r_in = N_LAYERS * D
    d_in, nm_in, nv_in = (a.reshape(N_LAYERS, D, SH_IN) for a in _adamw_big(
        w_in.reshape(r_in, SH_IN), g_w_in.reshape(r_in, SH_IN), m_w_in.reshape(r_in, SH_IN),
        v_w_in.reshape(r_in, SH_IN), "adamw_w_in"))
    r_sm = N_LAYERS * SH_ROW
    big = {}
    for nm, w, g, m, v in (("co", w_conv_out, g_w_co, m_w_conv_out, v_w_conv_out),
                           ("ao", w_attn_out, g_w_ao, m_w_attn_out, v_w_attn_out),
                           ("out", w_out, g_w_out, m_w_out, v_w_out)):
        big[nm] = tuple(a.reshape(N_LAYERS, SH_ROW, D) for a in _adamw_big(
            w.reshape(r_sm, D), g.reshape(r_sm, D), m.reshape(r_sm, D), v.reshape(r_sm, D), "adamw_w_small"))
    sm_w = [norm_g, conv_w, q_norm_g, k_norm_g, sinks, gate_b]
    sm_g = [g_norm_g, g_conv_w, g_q_norm_g, g_k_norm_g, g_sinks, g_gate_b]
    sm_m = [m_norm_g, m_conv_w, m_q_norm_g, m_k_norm_g, m_sinks, m_gate_b]
    sm_v = [v_norm_g, v_conv_w, v_q_norm_g, v_k_norm_g, v_sinks, v_gate_b]
    sd, snm, snv = _adamw_small(sm_w, sm_g, sm_m, sm_v)

    def order(norm, w_in_, conv, qn, kn, sk, co, ao, gb, wo):
        return [norm, w_in_, conv, qn, kn, sk, co, ao, gb, wo]

    grads = order(g_norm_g, g_w_in, g_conv_w, g_q_norm_g, g_k_norm_g, g_sinks, g_w_co, g_w_ao, g_gate_b, g_w_out)
    deltas = order(sd[0], d_in, sd[1], sd[2], sd[3], sd[4], big["co"][0], big["ao"][0], sd[5], big["out"][0])
    new_m = order(snm[0], nm_in, snm[1], snm[2], snm[3], snm[4], big["co"][1], big["ao"][1], snm[5], big["out"][1])
    new_v = order(snv[0], nv_in, snv[1], snv[2], snv[3], snv[4], big["co"][2], big["ao"][2], snv[5], big["out"][2])
    return (loss, grad_x, *grads, *deltas, *new_m, *new_v)
```

```python
import functools

import jax
import jax.numpy as jnp
from jax import lax
from jax.experimental import pallas as pl
from jax.experimental.pallas import tpu as pltpu

F32, BF16 = jnp.float32, jnp.bfloat16
SDS = jax.ShapeDtypeStruct
MESH = pl.DeviceIdType.MESH
ANY = pl.BlockSpec(memory_space=pl.ANY)

D = 1024
N_IN = 8704
N_LAYERS = 4
N_Q, N_KV, HEAD = 16, 4, 64
GROUP = N_Q // N_KV
BLK = 128
EPS = 1e-6
NEG = -1e30
SCALE = HEAD ** -0.5
SH_IN = N_IN // 4
SH_ROW = D // 4
CB = 512
SEG_CONV, SEG_Q, SEG_KV, SEG_ZA, SEG_GL = (0, 8), (8, 2), (10, 1), (11, 2), (13, 4)
VMEM_BIG = 56 * 1024 * 1024

ADAM_LR, ADAM_B1, ADAM_B2, ADAM_EPS, ADAM_WD, ADAM_STEP = 0.001, 0.9, 0.999, 1e-08, 0.01, 10


def _cp(*sem, vmem=None):
    return pltpu.CompilerParams(dimension_semantics=sem if sem else None, vmem_limit_bytes=vmem)


def _sigmoid(z):
    return 1.0 / (1.0 + jnp.exp(-z))


def _dot(a, b):
    return jnp.dot(a, b, preferred_element_type=F32)


def _dot_nt(a, b):
    return lax.dot_general(a, b, (((1,), (1,)), ((), ())), preferred_element_type=F32)


def _dot_tn(a, b):
    return lax.dot_general(a, b, (((0,), (0,)), ((), ())), preferred_element_type=F32)


def _rms(xh):
    r = lax.rsqrt(jnp.mean(xh * xh, axis=-1, keepdims=True) + EPS)
    return xh * r, r


def _fold8(v):
    return jnp.sum(v.reshape(v.shape[0] // 8, 8, v.shape[1]), axis=0)


def _cast_w_in(chip_idx, w, layer):
    def body(j_ref, i_ref, o_ref):
        o_ref[...] = i_ref[...].astype(BF16)

    return pl.pallas_call(
        body, name="cast_w_in",
        grid_spec=pltpu.PrefetchScalarGridSpec(
            num_scalar_prefetch=1, grid=(2,),
            in_specs=[pl.BlockSpec((None, 512, SH_IN), lambda i, j: (layer, i, 0))],
            out_specs=pl.BlockSpec((512, SH_IN), lambda i, j: (i, j[0]))),
        out_shape=SDS((D, N_IN), BF16), compiler_params=_cp("parallel"),
    )(chip_idx, w)


def _cast_w_small(chip_idx, a, b, c, layer):
    def body(j_ref, a_ref, b_ref, c_ref, o_ref):
        o_ref[0] = a_ref[...].astype(BF16)
        o_ref[1] = b_ref[...].astype(BF16)
        o_ref[2] = c_ref[...].astype(BF16)

    spec = pl.BlockSpec((None, SH_ROW, D), lambda i, j: (layer, 0, 0))
    return pl.pallas_call(
        body, name="cast_w_small",
        grid_spec=pltpu.PrefetchScalarGridSpec(
            num_scalar_prefetch=1, grid=(1,), in_specs=[spec, spec, spec],
            out_specs=pl.BlockSpec((3, SH_ROW, D), lambda i, j: (0, j[0], 0))),
        out_shape=SDS((3, D, D), BF16), compiler_params=_cp("parallel"),
    )(chip_idx, a, b, c)


def _mesh_pos():
    return lax.axis_index("x"), lax.axis_index("y"), lax.axis_index("c")


def _other_chips(x, y):
    return [(1 - x, y), (x, 1 - y), (1 - x, 1 - y)]


class _Rider:
    def __init__(self, ins, out_shape, n, copies, aliases=()):
        self.ins, self.out_shape, self.n, self.copies, self.aliases = list(ins), list(out_shape), n, copies, aliases


def _merge_riders(*riders):
    riders = [r for r in riders if r is not None]
    if len(riders) < 2:
        return riders[0] if riders else None

    def copies(ins, outs, send, recv, base=0):
        cps, i0, o0 = [], 0, 0
        for r in riders:
            cps += r.copies(ins[i0:i0 + len(r.ins)], outs[o0:o0 + len(r.out_shape)], send, recv, base + len(cps))
            i0, o0 = i0 + len(r.ins), o0 + len(r.out_shape)
        return cps

    aliases, i0, o0 = [], 0, 0
    for r in riders:
        aliases += [(i0 + i, o0 + o) for i, o in r.aliases]
        i0, o0 = i0 + len(r.ins), o0 + len(r.out_shape)
    return _Rider(sum((r.ins for r in riders), []), sum((r.out_shape for r in riders), []),
                  sum(r.n for r in riders), copies, tuple(aliases))


def _rcopy(src, dst, send, recv, k, to):
    return pltpu.make_async_remote_copy(src_ref=src, dst_ref=dst, send_sem=send.at[k], recv_sem=recv.at[k],
                                        device_id=to, device_id_type=MESH)


def _hosted_call(body, rider, *, name, grid, in_specs, out_specs, out_shape, args, scratch_shapes=(), vmem=None):
    n_in, n_out, n_scr = len(in_specs), len(out_specs), len(scratch_shapes)
    r_in, r_out = len(rider.ins), len(rider.out_shape)

    def full_body(*refs):
        host_in, rid_in = refs[:n_in], refs[n_in:n_in + r_in]
        o0 = n_in + r_in
        host_out, rid_out = refs[o0:o0 + n_out], refs[o0 + n_out:o0 + n_out + r_out]
        s0 = o0 + n_out + r_out
        host_scr, (send, recv) = refs[s0:s0 + n_scr], refs[s0 + n_scr:]
        if body is None:
            cps = rider.copies(rid_in, rid_out, send, recv)
            for cp in cps:
                cp.start()
            for cp in cps:
                cp.wait()
            return
        ids = [pl.program_id(a) for a in range(len(grid))]
        first = functools.reduce(lambda p, q: p & q, [i == 0 for i in ids])
        last = functools.reduce(lambda p, q: p & q, [i == g - 1 for i, g in zip(ids, grid)])

        @pl.when(first)
        def _():
            for cp in rider.copies(rid_in, rid_out, send, recv):
                cp.start()

        body(*host_in, *host_out, *host_scr)

        @pl.when(last)
        def _():
            for cp in rider.copies(rid_in, rid_out, send, recv):
                cp.wait()

    res = pl.pallas_call(
        full_body, name=name, grid=grid if body is not None else (),
        in_specs=list(in_specs) + [ANY] * r_in, out_specs=list(out_specs) + [ANY] * r_out,
        out_shape=list(out_shape) + rider.out_shape,
        scratch_shapes=list(scratch_shapes) + [pltpu.SemaphoreType.DMA((rider.n,))] * 2,
        input_output_aliases={n_in + i: n_out + o for i, o in rider.aliases},
        compiler_params=_cp(*(("arbitrary",) * len(grid) if body is not None else ()), vmem=vmem),
    )(*args, *rider.ins)
    return res[:n_out], res[n_out:]


def _call(body, rider, **kw):
    if rider is not None:
        return _hosted_call(body, rider, **kw)
    res = pl.pallas_call(
        body, name=kw["name"], grid=kw["grid"], in_specs=list(kw["in_specs"]), out_specs=list(kw["out_specs"]),
        out_shape=list(kw["out_shape"]), scratch_shapes=list(kw.get("scratch_shapes", ())),
        compiler_params=_cp(*(("arbitrary",) * len(kw["grid"])), vmem=kw.get("vmem")),
    )(*kw["args"])
    return res, []


def _gather_rider(arrays, stage):
    def copies(ins, outs, send, recv, base=0):
        x, y, c = _mesh_pos()
        cps = []
        for full in outs:
            for k, chip in enumerate(_other_chips(x, y)):
                whose = 2 * x + y if stage == "A" else 2 * chip[0] + chip[1]
                if len(full.shape) == 2:
                    reg = full.at[pl.ds(pl.multiple_of(c * 512, 512), 512), pl.ds(pl.multiple_of(whose * SH_IN, 128), SH_IN)]
                else:
                    reg = full.at[:, pl.ds(pl.multiple_of(whose * SH_ROW + c * 128, 128), 128), :]
                to = (*chip, c) if stage == "A" else (x, y, 1 - c)
                cps.append(_rcopy(reg, reg, send, recv, base + len(cps), to))
        return cps

    return _Rider(arrays, [SDS(v.shape, v.dtype) for v in arrays], 3 * len(arrays), copies,
                  aliases=tuple((i, i) for i in range(len(arrays))))


def _staged_copies(copies, stages, sem_in, sem_out):
    busy, count = {}, {}
    for idx, (src, dst, kind) in enumerate(copies):
        slot = count.get(kind, 0) % 2
        count[kind] = count.get(kind, 0) + 1
        if (kind, slot) in busy:
            busy.pop((kind, slot)).wait()
        buf = stages[kind].at[slot]
        cin = pltpu.make_async_copy(src, buf, sem_in.at[idx])
        cin.start()
        cin.wait()
        cout = pltpu.make_async_copy(buf, dst, sem_out.at[idx])
        cout.start()
        busy[(kind, slot)] = cout
    for cp in busy.values():
        cp.wait()


def _allreduce_small(pack):
    rows = pack.shape[0]

    def body(p_ref, o_ref, buf, send, recv):
        x, y, c = _mesh_pos()
        me = 4 * x + 2 * y + c
        sends = []
        for r in range(1, 8):
            to = (x if not (r & 4) else 1 - x, y if not (r & 2) else 1 - y, c if not (r & 1) else 1 - c)
            cp = pltpu.make_async_remote_copy(src_ref=p_ref, dst_ref=buf.at[me], send_sem=send.at[r - 1],
                                              recv_sem=recv.at[r - 1], device_id=to, device_id_type=MESH)
            cp.start()
            sends.append(cp)
        buf[me] = p_ref[...]
        for r in range(1, 8):
            frm = (4 * x + 2 * y + c) ^ r
            pltpu.make_async_remote_copy(src_ref=p_ref, dst_ref=buf.at[frm], send_sem=send.at[r - 1],
                                         recv_sem=recv.at[r - 1], device_id=(x, y, c), device_id_type=MESH).wait_recv()
        acc = buf[0]
        for d in range(1, 8):
            acc = acc + buf[d]
        o_ref[...] = acc
        for cp in sends:
            cp.wait_send()

    vm = pl.BlockSpec(memory_space=pltpu.VMEM)
    return pl.pallas_call(
        body, name="allreduce_small", in_specs=[vm], out_specs=vm, out_shape=SDS((rows, 128), F32),
        scratch_shapes=[pltpu.VMEM((8, rows, 128), F32), pltpu.SemaphoreType.DMA((7,)), pltpu.SemaphoreType.DMA((7,))],
    )(pack)


def _rmsnorm_fwd(x, g):
    t = x.shape[0]
    tm = min(512, t)

    def body(x_ref, g_ref, h_ref, ht_ref):
        xv = x_ref[...]
        r = lax.rsqrt(jnp.mean(xv * xv, axis=-1, keepdims=True) + EPS)
        h = xv * r * g_ref[...]
        h_ref[...] = h.astype(BF16)
        ht_ref[...] = h.T.astype(BF16)

    return pl.pallas_call(
        body, name="rmsnorm_fwd", grid=(t // tm,),
        in_specs=[pl.BlockSpec((tm, D), lambda i: (i, 0)), pl.BlockSpec((1, D), lambda i: (0, 0))],
        out_specs=[pl.BlockSpec((tm, D), lambda i: (i, 0)), pl.BlockSpec((D, tm), lambda i: (0, i))],
        out_shape=[SDS((t, D), BF16), SDS((D, t), BF16)],
        compiler_params=_cp("parallel", vmem=VMEM_BIG),
    )(x, g)


FWD_SEGS = ((0, 8), (8, 3), (11, 2), (13, 4))


def _in_proj(h, w_full, rider):
    t = h.shape[0]
    tm = min(2048, t)

    def body(a_ref, b_ref, *o_refs):
        j = pl.program_id(1)
        for o_ref, (off, nblk) in zip(o_refs, FWD_SEGS):
            @pl.when((j >= off) & (j < off + nblk))
            def _():
                o_ref[...] = _dot(a_ref[...], b_ref[...]).astype(BF16)

    def out(seg):
        off, nblk = seg
        return pl.BlockSpec((tm, CB), lambda i, j: (i, jnp.clip(j - off, 0, nblk - 1)))

    res, got = _call(
        body, rider, name="in_proj", grid=(t // tm, N_IN // CB),
        in_specs=[pl.BlockSpec((tm, D), lambda i, j: (i, 0)), pl.BlockSpec((D, CB), lambda i, j: (0, j))],
        out_specs=[out(s) for s in FWD_SEGS], out_shape=[SDS((t, s[1] * CB), BF16) for s in FWD_SEGS],
        args=(h, w_full), vmem=VMEM_BIG)
    return res, got


def _conv_fwd(u_conv, conv_w):
    t = u_conv.shape[0]
    tm = min(256, t)
    hb = tm // 16

    def body(v_ref, b_ref, c_ref, z_ref, hv_ref, hc_ref, w_ref, y_ref):
        i = pl.program_id(0)
        cv = c_ref[...].astype(F32) * v_ref[...].astype(F32)
        halo = hc_ref[...].astype(F32) * hv_ref[...].astype(F32)
        halo = jnp.where(i > 0, halo, 0.0)
        row = lax.broadcasted_iota(jnp.int32, (tm, 1), 0)
        s1 = jnp.where(row == 0, halo[15:16], pltpu.roll(cv, 1, 0))
        s2 = jnp.where(row == 0, halo[14:15], jnp.where(row == 1, halo[15:16], pltpu.roll(cv, 2, 0)))
        conv = w_ref[0:1, :] * s2 + w_ref[1:2, :] * s1 + w_ref[2:3, :] * cv
        z = z_ref[...].astype(F32)
        y_ref[...] = (b_ref[...].astype(F32) * conv * (z * _sigmoid(z))).astype(BF16)

    def col(k):
        return pl.BlockSpec((tm, D), lambda i: (i, k))

    def halo(k):
        return pl.BlockSpec((16, D), lambda i: (jnp.maximum(i * hb - 1, 0), k))

    return pl.pallas_call(
        body, name="conv_fwd", grid=(t // tm,),
        in_specs=[col(0), col(1), col(2), col(3), halo(0), halo(2), pl.BlockSpec((3, D), lambda i: (0, 0))],
        out_specs=pl.BlockSpec((tm, D), lambda i: (i, 0)), out_shape=SDS((t, D), BF16),
        compiler_params=_cp("parallel", vmem=VMEM_BIG),
    )(u_conv, u_conv, u_conv, u_conv, u_conv, u_conv, conv_w)


KVX = 4 * N_KV * 128


def _iota2(shape):
    return lax.broadcasted_iota(jnp.int32, shape, 0), lax.broadcasted_iota(jnp.int32, shape, 1)


def _head_sum(v):
    r, c = _iota2((128, 128))
    ones = ((r >> 6) == (c >> 6)).astype(BF16)
    hi = v.astype(BF16)
    lo = (v - hi.astype(F32)).astype(BF16)
    return jnp.concatenate([_dot(hi[:, g:g + 128], ones) + _dot(lo[:, g:g + 128], ones)
                            for g in range(0, v.shape[1], 128)], axis=1)


def _expand_mats():
    r, c = _iota2((N_KV * HEAD, N_KV * 128))
    base = ((r >> 6) << 7) + (r & 63)
    return (c == base).astype(BF16), (c == base + 64).astype(BF16)


def _fold_mat():
    r, c = _iota2((N_KV * 128, N_KV * HEAD))
    return (((r >> 7) == (c >> 6)) & ((r & 63) == (c & 63))).astype(BF16)


def _qkv_prep(u_qkv, qg_s, kg_t, rider):
    t = u_qkv.shape[0]
    tm = min(512, t)

    def body(u_ref, qg_ref, kg_ref, qs_ref, kvx_ref):
        q = u_ref[:, 0:D].astype(F32)
        rq = lax.rsqrt(_head_sum(q * q) * (1.0 / HEAD) + EPS)
        qs_ref[...] = (q * rq * qg_ref[...]).astype(BF16)
        k = u_ref[:, D:D + 256].astype(F32)
        rk = lax.rsqrt(_head_sum(k * k) * (1.0 / HEAD) + EPS)
        kn = (k * rk * kg_ref[...]).astype(BF16)
        v = u_ref[:, D + 256:D + 512]
        e_lo, e_hi = _expand_mats()
        kvx_ref[:, 0:512] = _dot(kn, e_lo).astype(BF16)
        kvx_ref[:, 512:1024] = _dot(kn, e_hi).astype(BF16)
        kvx_ref[:, 1024:1536] = _dot(v, e_lo).astype(BF16)
        kvx_ref[:, 1536:2048] = _dot(v, e_hi).astype(BF16)

    return _call(
        body, rider, name="qkv_prep", grid=(t // tm,),
        in_specs=[pl.BlockSpec((tm, 1536), lambda i: (i, 0)), pl.BlockSpec((1, D), lambda i: (0, 0)),
                  pl.BlockSpec((1, 256), lambda i: (0, 0))],
        out_specs=[pl.BlockSpec((tm, D), lambda i: (i, 0)), pl.BlockSpec((tm, KVX), lambda i: (i, 0))],
        out_shape=[SDS((t, D), BF16), SDS((t, KVX), BF16)], args=(u_qkv, qg_s, kg_t), vmem=VMEM_BIG)


def _band_bias():
    j, r = _iota2((2 * BLK, 2 * BLK))
    diff = (r & (BLK - 1)) - j + BLK
    band = (diff >= 0) & (diff < BLK)
    return jnp.stack([jnp.where(band & (j >= BLK), 0.0, NEG), jnp.where(band, 0.0, NEG)]).astype(F32)


def _pair_rows(ref_or_val, hk):
    return jnp.concatenate([ref_or_val[:, 256 * hk:256 * hk + 128], ref_or_val[:, 256 * hk + 128:256 * hk + 256]], axis=0)


def _sink_row(sink_ref, hk, half):
    return jnp.concatenate([jnp.full((1, BLK), sink_ref[0, GROUP * hk + half], F32),
                            jnp.full((1, BLK), sink_ref[0, GROUP * hk + 2 + half], F32)], axis=1)


def _kv_operands(kvb, hk, half):
    return (kvb[:, 512 * half + 128 * hk:512 * half + 128 * hk + 128],
            kvb[:, 1024 + 512 * half + 128 * hk:1024 + 512 * half + 128 * hk + 128])


def _attn_fwd(qs, kvx, u_za, sinks, bias, rider):
    t = qs.shape[0]
    nb = t // BLK

    def body(q_ref, kc_ref, kp_ref, za_ref, sink_ref, bias_ref, o_ref, lse_ref):
        kvb = jnp.concatenate([kp_ref[...], kc_ref[...]], axis=0)
        bias_v = bias_ref[...]
        key0 = lax.broadcasted_iota(jnp.int32, (2 * BLK, 1), 0) == 0
        ones = jnp.ones((2 * BLK, 128), BF16)
        cols = []
        for hk in range(N_KV):
            qpp = _pair_rows(q_ref, hk)
            opp = None
            for half in range(2):
                kx, vx = _kv_operands(kvb, hk, half)
                s = _dot_nt(kx, qpp) + bias_v
                sink = _sink_row(sink_ref, hk, half)
                m = jnp.maximum(jnp.max(s, axis=0, keepdims=True), sink)
                p = jnp.exp(s - m)
                es = jnp.exp(sink - m)
                lse_ref[0, 2 * hk + half:2 * hk + half + 1, :] = m + jnp.log(jnp.sum(p, axis=0, keepdims=True) + es)
                pe = jnp.where(key0, es, p).astype(BF16)
                rhs = jnp.concatenate([jnp.where(key0, jnp.zeros_like(vx), vx), ones], axis=1)
                nd = _dot_tn(pe, rhs)
                o = nd[:, :128] * (1.0 / nd[:, 128:])
                opp = o if opp is None else opp + o
            cols += [opp[:BLK], opp[BLK:]]
        za = za_ref[...].astype(F32)
        o_ref[...] = (jnp.concatenate(cols, axis=1) * (za * _sigmoid(za))).astype(BF16)

    prev = lambda n: jnp.maximum(n - 1, 0)
    (o, lse), got = _call(
        body, rider, name="attn_fwd", grid=(nb,),
        in_specs=[pl.BlockSpec((BLK, D), lambda n: (n, 0)),
                  pl.BlockSpec((BLK, KVX), lambda n: (n, 0)), pl.BlockSpec((BLK, KVX), lambda n: (prev(n), 0)),
                  pl.BlockSpec((BLK, D), lambda n: (n, 0)), pl.BlockSpec(memory_space=pltpu.SMEM),
                  pl.BlockSpec((None, 2 * BLK, 2 * BLK), lambda n: (jnp.minimum(n, 1), 0, 0))],
        out_specs=[pl.BlockSpec((BLK, D), lambda n: (n, 0)), pl.BlockSpec((1, 8, 2 * BLK), lambda n: (n, 0, 0))],
        out_shape=[SDS((t, D), BF16), SDS((nb, 8, 2 * BLK), F32)],
        args=(qs, kvx, kvx, u_za, sinks, bias), vmem=VMEM_BIG)
    return o, lse, got


def _out_proj_fwd(x, y_c, o, u_gl, gate_b, w_sm, rider):
    t = x.shape[0]
    tm = min(512, t)

    def body(x_ref, yc_ref, o_ref, gla_ref, glb_ref, gb_ref, wco_ref, wao_ref, wout_ref,
             xn_ref, ya_ref, yb_ref, mg_ref):
        ya = _dot(yc_ref[...], wco_ref[...])
        yb = _dot(o_ref[...], wao_ref[...])
        gb = gb_ref[...]
        ga_ = _sigmoid(gla_ref[...].astype(F32) + gb[:, :D])
        gb_ = _sigmoid(glb_ref[...].astype(F32) + gb[:, D:])
        merged = (ga_ * ya + gb_ * yb).astype(BF16)
        ya_ref[...] = ya.astype(BF16)
        yb_ref[...] = yb.astype(BF16)
        mg_ref[...] = merged
        xn_ref[...] = x_ref[...] + _dot(merged, wout_ref[...])

    row = pl.BlockSpec((tm, D), lambda i: (i, 0))
    wspec = lambda a: pl.BlockSpec((None, D, D), lambda i: (a, 0, 0))
    return _call(
        body, rider, name="out_proj_fwd", grid=(t // tm,),
        in_specs=[row, row, row, pl.BlockSpec((tm, D), lambda i: (i, 0)), pl.BlockSpec((tm, D), lambda i: (i, 1)),
                  pl.BlockSpec((1, 2 * D), lambda i: (0, 0)), wspec(0), wspec(1), wspec(2)],
        out_specs=[row, row, row, row],
        out_shape=[SDS((t, D), F32), SDS((t, D), BF16), SDS((t, D), BF16), SDS((t, D), BF16)],
        args=(x, y_c, o, u_gl, u_gl, gate_b, w_sm, w_sm, w_sm), vmem=VMEM_BIG)


def _loss_head(y, tgt):
    t = y.shape[0]
    tm = min(512, t)

    def body(y_ref, t_ref, dy_ref, acc_ref):
        @pl.when(pl.program_id(0) == 0)
        def _():
            acc_ref[...] = jnp.zeros_like(acc_ref)
        err = y_ref[...] - t_ref[...]
        dy_ref[...] = err * (1.0 / D)
        sq = _fold8(err * err)
        tot = sq[:, 0:128]
        for k in range(1, D // 128):
            tot = tot + sq[:, 128 * k:128 * (k + 1)]
        acc_ref[...] += tot

    row = pl.BlockSpec((tm, D), lambda i: (i, 0))
    return pl.pallas_call(
        body, name="loss_head", grid=(t // tm,), in_specs=[row, row],
        out_specs=[row, pl.BlockSpec((8, 128), lambda i: (0, 0))],
        out_shape=[SDS((t, D), F32), SDS((8, 128), F32)], compiler_params=_cp("arbitrary"),
    )(y, tgt)


def _out_proj_bwd(dout, y_a, y_b, u_gl, gate_b, w_sm, rider):
    t = dout.shape[0]
    tm = min(512, t)

    def body(do_ref, ya_ref, yb_ref, gla_ref, glb_ref, gb_ref, wco_ref, wao_ref, wout_ref,
             dya_ref, dyb_ref, dgl_ref, dyc_ref, dob_ref, dgb_ref):
        @pl.when(pl.program_id(0) == 0)
        def _():
            dgb_ref[...] = jnp.zeros_like(dgb_ref)
        dm = _dot_nt(do_ref[...].astype(BF16), wout_ref[...])
        gb = gb_ref[...]
        ga_ = _sigmoid(gla_ref[...].astype(F32) + gb[:, :D])
        gb_ = _sigmoid(glb_ref[...].astype(F32) + gb[:, D:])
        dya = (ga_ * dm).astype(BF16)
        dyb = (gb_ * dm).astype(BF16)
        dgla = ya_ref[...].astype(F32) * dm * (ga_ * (1.0 - ga_))
        dglb = yb_ref[...].astype(F32) * dm * (gb_ * (1.0 - gb_))
        dya_ref[...] = dya
        dyb_ref[...] = dyb
        dgl_ref[:, :D] = dgla.astype(BF16)
        dgl_ref[:, D:] = dglb.astype(BF16)
        dgb_ref[:, :D] += _fold8(dgla)
        dgb_ref[:, D:] += _fold8(dglb)
        dyc_ref[...] = _dot_nt(dya, wco_ref[...]).astype(BF16)
        dob_ref[...] = _dot_nt(dyb, wao_ref[...]).astype(BF16)

    row = pl.BlockSpec((tm, D), lambda i: (i, 0))
    wspec = lambda a: pl.BlockSpec((None, D, D), lambda i: (a, 0, 0))
    return _call(
        body, rider, name="out_proj_bwd", grid=(t // tm,),
        in_specs=[row, row, row, pl.BlockSpec((tm, D), lambda i: (i, 0)), pl.BlockSpec((tm, D), lambda i: (i, 1)),
                  pl.BlockSpec((1, 2 * D), lambda i: (0, 0)), wspec(0), wspec(1), wspec(2)],
        out_specs=[row, row, pl.BlockSpec((tm, 2 * D), lambda i: (i, 0)), row, row,
                   pl.BlockSpec((8, 2 * D), lambda i: (0, 0))],
        out_shape=[SDS((t, D), BF16), SDS((t, D), BF16), SDS((t, 2 * D), BF16), SDS((t, D), BF16), SDS((t, D), BF16),
                   SDS((8, 2 * D), F32)],
        args=(dout, y_a, y_b, u_gl, u_gl, gate_b, w_sm, w_sm, w_sm), vmem=VMEM_BIG)


def _small_wgrads(y_c, d_ya, o, d_yb, merged, dout):
    t = y_c.shape[0]
    tk = min(512, t)

    def body(yc_ref, dya_ref, o_ref, dyb_ref, mg_ref, do_ref, g_ref):
        @pl.when(pl.program_id(0) == 0)
        def _():
            g_ref[...] = jnp.zeros_like(g_ref)
        g_ref[0] += _dot_tn(yc_ref[...], dya_ref[...])
        g_ref[1] += _dot_tn(o_ref[...], dyb_ref[...])
        g_ref[2] += _dot_tn(mg_ref[...], do_ref[...].astype(BF16))

    row = pl.BlockSpec((tk, D), lambda k: (k, 0))
    return pl.pallas_call(
        body, name="small_wgrads", grid=(t // tk,), in_specs=[row] * 6,
        out_specs=pl.BlockSpec((3, D, D), lambda k: (0, 0, 0)), out_shape=SDS((3, D, D), F32),
        compiler_params=_cp("arbitrary", vmem=VMEM_BIG),
    )(y_c, d_ya, o, d_yb, merged, dout)


def _conv_bwd(d_yc, u_conv, conv_w, rider):
    t = d_yc.shape[0]
    tm = min(256, t)
    hb = tm // 16
    last_halo = t // 16 - 1
    n_steps = t // tm

    def body(dy_ref, v_ref, b_ref, c_ref, z_ref, hv_ref, hc_ref, ndy_ref, nb_ref, nz_ref, w_ref, du_ref, dw_ref):
        i = pl.program_id(0)

        @pl.when(i == 0)
        def _():
            dw_ref[...] = jnp.zeros_like(dw_ref)
        v, c = v_ref[...].astype(F32), c_ref[...].astype(F32)
        b, z = b_ref[...].astype(F32), z_ref[...].astype(F32)
        cv = c * v
        halo = jnp.where(i > 0, hc_ref[...].astype(F32) * hv_ref[...].astype(F32), 0.0)
        row = lax.broadcasted_iota(jnp.int32, (tm, 1), 0)
        s1 = jnp.where(row == 0, halo[15:16], pltpu.roll(cv, 1, 0))
        s2 = jnp.where(row == 0, halo[14:15], jnp.where(row == 1, halo[15:16], pltpu.roll(cv, 2, 0)))
        w0, w1, w2 = w_ref[0:1, :], w_ref[1:2, :], w_ref[2:3, :]
        conv = w0 * s2 + w1 * s1 + w2 * cv
        sig = _sigmoid(z)
        sz = z * sig
        dsz = sig * (1.0 + z * (1.0 - sig))
        dy = dy_ref[...].astype(F32)
        dconv = dy * b * sz
        nz = nz_ref[...].astype(F32)
        nxt = ndy_ref[...].astype(F32) * nb_ref[...].astype(F32) * (nz * _sigmoid(nz))
        nxt = jnp.where(i < n_steps - 1, nxt, 0.0)
        a1 = jnp.where(row == tm - 1, nxt[0:1], pltpu.roll(dconv, tm - 1, 0))
        a2 = jnp.where(row == tm - 2, nxt[0:1], jnp.where(row == tm - 1, nxt[1:2], pltpu.roll(dconv, tm - 2, 0)))
        dcv = w2 * dconv + w1 * a1 + w0 * a2
        du_ref[:, 0:D] = (dcv * c).astype(BF16)
        du_ref[:, D:2 * D] = (dy * conv * sz).astype(BF16)
        du_ref[:, 2 * D:3 * D] = (dcv * v).astype(BF16)
        du_ref[:, 3 * D:4 * D] = (dy * b * conv * dsz).astype(BF16)
        r8 = lax.broadcasted_iota(jnp.int32, (8, 1), 0)
        dw_ref[...] += jnp.where(r8 == 0, jnp.sum(dconv * s2, axis=0, keepdims=True),
                                 jnp.where(r8 == 1, jnp.sum(dconv * s1, axis=0, keepdims=True),
                                           jnp.where(r8 == 2, jnp.sum(dconv * cv, axis=0, keepdims=True), 0.0)))

    def col(k):
        return pl.BlockSpec((tm, D), lambda i: (i, k))

    def halo(k):
        return pl.BlockSpec((16, D), lambda i: (jnp.maximum(i * hb - 1, 0), k))

    def nxt(k):
        return pl.BlockSpec((16, D), lambda i: (jnp.minimum((i + 1) * hb, last_halo), k))

    return _call(
        body, rider, name="conv_bwd", grid=(t // tm,),
        in_specs=[col(0), col(0), col(1), col(2), col(3), halo(0), halo(2), nxt(0), nxt(1), nxt(3),
                  pl.BlockSpec((3, D), lambda i: (0, 0))],
        out_specs=[pl.BlockSpec((tm, 4 * D), lambda i: (i, 0)), pl.BlockSpec((8, D), lambda i: (0, 0))],
        out_shape=[SDS((t, 4 * D), BF16), SDS((8, D), F32)],
        args=(d_yc, u_conv, u_conv, u_conv, u_conv, u_conv, u_conv, d_yc, u_conv, u_conv, conv_w), vmem=VMEM_BIG)


def _attn_bwd(d_o, qs, kvx, u_za, lse, sinks, bias, rider):
    t = d_o.shape[0]
    nb = t // BLK

    def body(q_ref, kc_ref, kp_ref, za_ref, do_ref, lse_ref, sink_ref, bias_ref,
             dq_ref, dkv_ref, dza_ref, dsk_ref, carry_ref):
        n = pl.program_id(0)

        @pl.when(n == 0)
        def _():
            carry_ref[...] = jnp.zeros_like(carry_ref)
            dsk_ref[...] = jnp.zeros_like(dsk_ref)

        live = n < nb
        kvb = jnp.concatenate([kp_ref[...], kc_ref[...]], axis=0)
        bias_v = bias_ref[...]
        za = za_ref[...].astype(F32)
        sig = _sigmoid(za)
        dsa = sig * (1.0 + za * (1.0 - sig))
        do = jnp.where(live, do_ref[...].astype(F32), 0.0)
        dattn = (do * (za * sig)).astype(BF16)
        lo_lanes = lax.broadcasted_iota(jnp.int32, (1, 128), 1) < HEAD
        dq_cols, attn_cols, dk_cols, dv_cols, dsk_rows = [], [], [], [], []
        for hk in range(N_KV):
            qpp = _pair_rows(q_ref, hk)
            dapp = _pair_rows(dattn, hk)
            probs, dss, xk, xv = [], [], [], []
            for half in range(2):
                kx, vx = _kv_operands(kvb, hk, half)
                lse = lse_ref[0, 2 * hk + half:2 * hk + half + 1, :]
                prob = jnp.exp(_dot_nt(kx, qpp) + bias_v - lse)
                psink = jnp.exp(_sink_row(sink_ref, hk, half) - lse)
                tdp = prob * _dot_nt(vx, dapp)
                drow = jnp.sum(tdp, axis=0, keepdims=True)
                ds = (tdp - prob * drow).astype(BF16)
                prob_b = prob.astype(BF16)
                xk.append(_dot(ds, qpp))
                xv.append(_dot(prob_b, dapp))
                probs.append(prob_b)
                dss.append(ds)
                dsk_rows.append(-psink * drow)
            kcat = jnp.concatenate([kvb[:, 128 * hk:128 * hk + 128], kvb[:, 512 + 128 * hk:512 + 128 * hk + 128]], axis=0)
            vcat = jnp.concatenate([kvb[:, 1024 + 128 * hk:1024 + 128 * hk + 128],
                                    kvb[:, 1536 + 128 * hk:1536 + 128 * hk + 128]], axis=0)
            app = _dot_tn(jnp.concatenate(probs, axis=0), vcat)
            dqpp = _dot_tn(jnp.concatenate(dss, axis=0), kcat)
            dq_cols += [dqpp[:BLK], dqpp[BLK:]]
            attn_cols += [app[:BLK], app[BLK:]]
            dk_cols.append(jnp.where(lo_lanes, xk[0], xk[1]))
            dv_cols.append(jnp.where(lo_lanes, xv[0], xv[1]))

        @pl.when(live)
        def _():
            dq_ref[...] = jnp.concatenate(dq_cols, axis=1).astype(BF16)
            dza_ref[...] = (do * jnp.concatenate(attn_cols, axis=1) * dsa).astype(BF16)

        band = jnp.concatenate(dk_cols + dv_cols, axis=1)
        dkv_ref[...] = (band[:BLK] + carry_ref[...]).astype(BF16)
        carry_ref[...] = band[BLK:]
        dsk_ref[...] += jnp.broadcast_to(jnp.concatenate(dsk_rows, axis=1), (8, 2 * N_KV * 2 * BLK))

    cur = lambda n: jnp.minimum(n, nb - 1)
    prev = lambda n: jnp.maximum(n - 1, 0)
    return _call(
        body, rider, name="attn_bwd", grid=(nb + 1,),
        in_specs=[pl.BlockSpec((BLK, D), lambda n: (cur(n), 0)),
                  pl.BlockSpec((BLK, KVX), lambda n: (cur(n), 0)), pl.BlockSpec((BLK, KVX), lambda n: (prev(n), 0)),
                  pl.BlockSpec((BLK, D), lambda n: (cur(n), 0)), pl.BlockSpec((BLK, D), lambda n: (cur(n), 0)),
                  pl.BlockSpec((1, 8, 2 * BLK), lambda n: (cur(n), 0, 0)), pl.BlockSpec(memory_space=pltpu.SMEM),
                  pl.BlockSpec((None, 2 * BLK, 2 * BLK), lambda n: (jnp.minimum(n, 1), 0, 0))],
        out_specs=[pl.BlockSpec((BLK, D), lambda n: (cur(n), 0)), pl.BlockSpec((BLK, D), lambda n: (prev(n), 0)),
                   pl.BlockSpec((BLK, D), lambda n: (cur(n), 0)), pl.BlockSpec((8, 2 * D), lambda n: (0, 0))],
        out_shape=[SDS((t, D), BF16), SDS((t, D), BF16), SDS((t, D), BF16), SDS((8, 2 * D), F32)],
        scratch_shapes=[pltpu.VMEM((BLK, D), F32)],
        args=(qs, kvx, kvx, u_za, d_o, lse, sinks, bias), vmem=VMEM_BIG)


def _qkv_post(u_qkv, dqs, dkv, dza, qg_s, kg_t, rider):
    t = u_qkv.shape[0]
    tm = min(512, t)

    def norm_bwd(x, dy, g):
        r = lax.rsqrt(_head_sum(x * x) * (1.0 / HEAD) + EPS)
        xhat = x * r
        dxh = dy * g
        return r * (dxh - xhat * (_head_sum(dxh * xhat) * (1.0 / HEAD))), _fold8(dy * xhat)

    def body(u_ref, dq_ref, dkv_ref, dza_ref, qg_ref, kg_ref, du_ref, dqg_ref, dkg_ref):
        @pl.when(pl.program_id(0) == 0)
        def _():
            dqg_ref[...] = jnp.zeros_like(dqg_ref)
            dkg_ref[...] = jnp.zeros_like(dkg_ref)
        dq, gq = norm_bwd(u_ref[:, 0:D].astype(F32), dq_ref[...].astype(F32), qg_ref[...])
        fold = _fold_mat()
        dk, gk = norm_bwd(u_ref[:, D:D + 256].astype(F32), _dot(dkv_ref[:, 0:512], fold), kg_ref[...])
        du_ref[:, 0:D] = dq.astype(BF16)
        du_ref[:, D:D + 256] = dk.astype(BF16)
        du_ref[:, D + 256:D + 512] = _dot(dkv_ref[:, 512:1024], fold).astype(BF16)
        du_ref[:, D + 512:2 * D + 512] = dza_ref[...]
        dqg_ref[...] += gq
        dkg_ref[...] += gk

    row = pl.BlockSpec((tm, D), lambda i: (i, 0))
    return _call(
        body, rider, name="qkv_post", grid=(t // tm,),
        in_specs=[pl.BlockSpec((tm, 1536), lambda i: (i, 0)), row, row, row,
                  pl.BlockSpec((1, D), lambda i: (0, 0)), pl.BlockSpec((1, 256), lambda i: (0, 0))],
        out_specs=[pl.BlockSpec((tm, 2560), lambda i: (i, 0)), pl.BlockSpec((8, D), lambda i: (0, 0)),
                   pl.BlockSpec((8, 256), lambda i: (0, 0))],
        out_shape=[SDS((t, 2560), BF16), SDS((8, D), F32), SDS((8, 256), F32)],
        args=(u_qkv, dqs, dkv, dza, qg_s, kg_t), vmem=VMEM_BIG)


N_GRAN = N_IN // CB
DU_COLS = ((0, 4096), (4096, 6656), (6656, N_IN))


def _du_granule(j):
    return jnp.clip(j, 0, 7), jnp.clip(j - 8, 0, 4), jnp.clip(j - 13, 0, 3)


def _du_select(j, refs, fn):
    for ref, lo, hi in zip(refs, (0, 8, 13), (8, 13, 17)):
        @pl.when((j >= lo) & (j < hi))
        def _():
            fn(ref)


def _in_proj_bwd(du, w_full, rider):
    t = du[0].shape[0]
    tn = min(512, t)

    def body(a0, a1, a2, w_hbm, dht_ref, w_ref, sem):
        @pl.when(pl.program_id(0) == 0)
        def _():
            cp = pltpu.make_async_copy(w_hbm, w_ref, sem)
            cp.start()
            cp.wait()
        acc = None
        for a_ref, (lo, hi) in zip((a0, a1, a2), DU_COLS):
            part = _dot_nt(w_ref[:, lo:hi], a_ref[...])
            acc = part if acc is None else acc + part
        dht_ref[...] = acc

    (dht,), got = _call(
        body, rider, name="in_proj_bwd", grid=(t // tn,),
        in_specs=[pl.BlockSpec((tn, hi - lo), lambda i: (i, 0)) for lo, hi in DU_COLS] + [ANY],
        out_specs=[pl.BlockSpec((D, tn), lambda i: (0, i))], out_shape=[SDS((D, t), F32)],
        scratch_shapes=[pltpu.VMEM((D, N_IN), BF16), pltpu.SemaphoreType.DMA(())], args=(*du, w_full), vmem=VMEM_BIG)
    return dht, got


def _rmsnorm_bwd(dht, x, g, dout):
    t = x.shape[0]
    tm = min(256, t)

    def body(dht_ref, x_ref, g_ref, do_ref, dx_ref, dg_ref):
        @pl.when(pl.program_id(0) == 0)
        def _():
            dg_ref[...] = jnp.zeros_like(dg_ref)
        dh = dht_ref[...].T
        xv = x_ref[...]
        r = lax.rsqrt(jnp.mean(xv * xv, axis=-1, keepdims=True) + EPS)
        xhat = xv * r
        dg_ref[...] += _fold8(dh * xhat)
        dxh = dh * g_ref[...]
        dx_ref[...] = do_ref[...] + r * (dxh - xhat * jnp.mean(dxh * xhat, axis=-1, keepdims=True))

    row = pl.BlockSpec((tm, D), lambda i: (i, 0))
    return pl.pallas_call(
        body, name="rmsnorm_bwd", grid=(t // tm,),
        in_specs=[pl.BlockSpec((D, tm), lambda i: (0, i)), row, pl.BlockSpec((1, D), lambda i: (0, 0)), row],
        out_specs=[row, pl.BlockSpec((8, D), lambda i: (0, 0))],
        out_shape=[SDS((t, D), F32), SDS((8, D), F32)], compiler_params=_cp("arbitrary", vmem=VMEM_BIG),
    )(dht, x, g, dout)


def _in_proj_wgrad(ht, du):
    t = ht.shape[1]
    tk = min(4096, t)
    n_k = t // tk

    def body(h_ref, b0, b1, b2, g_ref):
        j, k = pl.program_id(0), pl.program_id(1)

        if n_k > 1:
            @pl.when(k == 0)
            def _():
                g_ref[...] = jnp.zeros_like(g_ref)

        def add(b_ref):
            if n_k > 1:
                g_ref[...] += _dot(h_ref[...], b_ref[...])
            else:
                g_ref[...] = _dot(h_ref[...], b_ref[...])
        _du_select(j, (b0, b1, b2), add)

    seg = lambda q: pl.BlockSpec((tk, CB), lambda j, k: (k, _du_granule(j)[q]))
    return pl.pallas_call(
        body, name="in_proj_wgrad", grid=(N_GRAN, t // tk),
        in_specs=[pl.BlockSpec((D, tk), lambda j, k: (0, k)), seg(0), seg(1), seg(2)],
        out_specs=pl.BlockSpec((D, CB), lambda j, k: (0, j)), out_shape=SDS((D, N_IN), F32),
        compiler_params=_cp("parallel", "arbitrary", vmem=VMEM_BIG),
    )(ht, *du)


def _swap_rider(g_in, g_sm):
    def copies(ins, outs, send, recv, base=0):
        x, y, c = _mesh_pos()
        cps = []
        for src, dst in zip(ins, outs):
            half = src.at[1 - c] if len(src.shape) == 3 else src.at[:, :, 1 - c]
            cps.append(_rcopy(half, dst, send, recv, base + len(cps), (x, y, 1 - c)))
        return cps

    arrays = [g for g in (g_in, g_sm) if g is not None]
    shapes = [SDS((512, N_IN), F32) if len(g.shape) == 3 else SDS((3, 4, 128, D), F32) for g in arrays]
    return _Rider(arrays, shapes, len(arrays), copies)


def _add_halves_in(cc_idx, g_in, r_in):
    def body(cc_ref, a_ref, b_ref, f_ref, h_ref):
        s = a_ref[...] + b_ref[...]
        h_ref[...] = s.astype(BF16)

        @pl.when(pl.program_id(1) == cc_ref[1])
        def _():
            f_ref[...] = s

    blk = pl.BlockSpec((128, SH_IN), lambda i, j, cc: (i, j))
    return pl.pallas_call(
        body, name="add_halves_in",
        grid_spec=pltpu.PrefetchScalarGridSpec(
            num_scalar_prefetch=1, grid=(4, 4),
            in_specs=[pl.BlockSpec((None, 128, SH_IN), lambda i, j, cc: (cc[0], i, j)), blk],
            out_specs=[pl.BlockSpec((128, SH_IN), lambda i, j, cc: (i, 0)), blk]),
        out_shape=[SDS((512, SH_IN), F32), SDS((512, N_IN), BF16)], compiler_params=_cp("arbitrary", "arbitrary"),
    )(cc_idx, g_in, r_in)


def _add_halves_sm(c_idx, g_sm, r_sm):
    def body(c_ref, a_ref, b_ref, f_ref, h_ref):
        s = a_ref[...] + b_ref[...]
        f_ref[...] = s
        h_ref[...] = s.astype(BF16)

    blk = pl.BlockSpec((1, 4, 128, D), lambda a, c: (a, 0, 0, 0))
    return pl.pallas_call(
        body, name="add_halves_sm",
        grid_spec=pltpu.PrefetchScalarGridSpec(
            num_scalar_prefetch=1, grid=(3,),
            in_specs=[pl.BlockSpec((1, 4, None, 128, D), lambda a, c: (a, 0, c[0], 0, 0)), blk], out_specs=[blk, blk]),
        out_shape=[SDS((3, 4, 128, D), F32), SDS((3, 4, 128, D), BF16)], compiler_params=_cp("parallel"),
    )(c_idx, g_sm, r_sm)


def _scatter_rider(h_in, h_sm):
    def copies(ins, outs, send, recv, base=0):
        x, y, c = _mesh_pos()
        cps = []
        for src, dst in zip(ins, outs):
            for k, chip in enumerate(_other_chips(x, y)):
                their = 2 * chip[0] + chip[1]
                part = src.at[:, pl.ds(pl.multiple_of(their * SH_IN, 128), SH_IN)] if len(src.shape) == 2 else src.at[:, their]
                cps.append(_rcopy(part, dst.at[k], send, recv, base + len(cps), (*chip, c)))
        return cps

    arrays = [h for h in (h_in, h_sm) if h is not None]
    shapes = [SDS((3, 512, SH_IN), BF16) if len(h.shape) == 2 else SDS((3, 3, 128, D), BF16) for h in arrays]
    return _Rider(arrays, shapes, 3 * len(arrays), copies)


def _ride_alone(rider, name):
    return _hosted_call(None, rider, name=name, grid=(), in_specs=[], out_specs=[], out_shape=[], args=())[1]


def _final_sum_in(f_in, r_in):
    def body(a_ref, r_ref, o_ref):
        o_ref[...] = a_ref[...] + r_ref[0].astype(F32) + r_ref[1].astype(F32) + r_ref[2].astype(F32)

    return pl.pallas_call(
        body, name="final_sum_in", grid=(4,),
        in_specs=[pl.BlockSpec((128, SH_IN), lambda i: (i, 0)), pl.BlockSpec((3, 128, SH_IN), lambda i: (0, i, 0))],
        out_specs=pl.BlockSpec((128, SH_IN), lambda i: (i, 0)),
        out_shape=SDS((512, SH_IN), F32), compiler_params=_cp("parallel"),
    )(f_in, r_in)


def _final_sum_sm(chip_idx, f_sm, r_sm):
    def body(j_ref, a_ref, r_ref, o_ref):
        o_ref[...] = a_ref[...] + r_ref[0].astype(F32) + r_ref[1].astype(F32) + r_ref[2].astype(F32)

    return pl.pallas_call(
        body, name="final_sum_sm",
        grid_spec=pltpu.PrefetchScalarGridSpec(
            num_scalar_prefetch=1, grid=(3,),
            in_specs=[pl.BlockSpec((1, None, 128, D), lambda a, j: (a, j[0], 0, 0)),
                      pl.BlockSpec((3, 1, 128, D), lambda a, j: (0, a, 0, 0))],
            out_specs=pl.BlockSpec((1, 128, D), lambda a, j: (a, 0, 0))),
        out_shape=SDS((3, 128, D), F32), compiler_params=_cp("parallel"),
    )(chip_idx, f_sm, r_sm)


def _join_halves(t_in, t_sm):
    n_cp = N_LAYERS * 4

    def body(*refs):
        ins, outs = refs[:2 * N_LAYERS], refs[2 * N_LAYERS:2 * N_LAYERS + 4]
        send, recv, loc_in, loc_out, stage_in, stage_sm = refs[2 * N_LAYERS + 4:]
        x, y, c = _mesh_pos()
        cps, own = [], []
        for l in range(N_LAYERS):
            for a in range(4):
                s = 4 * l + a
                if a == 0:
                    src = ins[2 * l]
                    dst = outs[0].at[l, pl.ds(pl.multiple_of(c * 512, 512), 512), :]
                else:
                    src = ins[2 * l + 1].at[a - 1]
                    dst = outs[a].at[l, pl.ds(pl.multiple_of(c * 128, 128), 128), :]
                own.append((src, dst, min(a, 1)))
                cp = pltpu.make_async_remote_copy(src_ref=src, dst_ref=dst, send_sem=send.at[s], recv_sem=recv.at[s],
                                                  device_id=(x, y, 1 - c), device_id_type=MESH)
                cp.start()
                cps.append(cp)
        _staged_copies(own, (stage_in, stage_sm), loc_in, loc_out)
        for l in range(N_LAYERS):
            for a in range(4):
                s = 4 * l + a
                if a == 0:
                    got = outs[0].at[l, pl.ds(pl.multiple_of((1 - c) * 512, 512), 512), :]
                else:
                    got = outs[a].at[l, pl.ds(pl.multiple_of((1 - c) * 128, 128), 128), :]
                pltpu.make_async_remote_copy(src_ref=got, dst_ref=got, send_sem=send.at[s], recv_sem=recv.at[s],
                                             device_id=(x, y, 1 - c), device_id_type=MESH).wait_recv()
        for cp in cps:
            cp.wait_send()

    args = []
    for l in range(N_LAYERS):
        args += [t_in[l], t_sm[l]]
    sm = SDS((N_LAYERS, SH_ROW, D), F32)
    return pl.pallas_call(
        body, name="join_halves", in_specs=[ANY] * (2 * N_LAYERS), out_specs=[ANY] * 4,
        out_shape=[SDS((N_LAYERS, D, SH_IN), F32), sm, sm, sm],
        scratch_shapes=[pltpu.SemaphoreType.DMA((n_cp,))] * 4
        + [pltpu.VMEM((2, 512, SH_IN), F32), pltpu.VMEM((2, 128, D), F32)],
        compiler_params=_cp(vmem=VMEM_BIG),
    )(*args)


def _adam_math(w, g, m, v):
    m = ADAM_B1 * m + (1.0 - ADAM_B1) * g
    v = ADAM_B2 * v + (1.0 - ADAM_B2) * (g * g)
    m_hat = m / (1.0 - ADAM_B1 ** ADAM_STEP)
    v_hat = v / (1.0 - ADAM_B2 ** ADAM_STEP)
    delta = -ADAM_LR * (m_hat / (jnp.sqrt(v_hat) + ADAM_EPS) + ADAM_WD * w)
    return delta, m, v


def _adamw_big(w, g, m, v, name):
    rows, cols = w.shape
    tr = 128

    def body(w_ref, g_ref, m_ref, v_ref, go_ref, d_ref, nm_ref, nv_ref):
        g = g_ref[...]
        go_ref[...] = g
        d_ref[...], nm_ref[...], nv_ref[...] = _adam_math(w_ref[...], g, m_ref[...], v_ref[...])

    blk = pl.BlockSpec((tr, cols), lambda i: (i, 0))
    return pl.pallas_call(
        body, name=name, grid=(rows // tr,), in_specs=[blk] * 4, out_specs=[blk] * 4,
        out_shape=[SDS((rows, cols), F32)] * 4, compiler_params=_cp("parallel", vmem=VMEM_BIG),
    )(w, g, m, v)


def _adamw_small(ws, gs, ms, vs):
    n = len(ws)

    def body(*refs):
        for k in range(n):
            w_ref, g_ref, m_ref, v_ref = (refs[q * n + k] for q in range(4))
            d, nm, nv = _adam_math(w_ref[...], g_ref[...], m_ref[...], v_ref[...])
            refs[4 * n + k][...] = d
            refs[5 * n + k][...] = nm
            refs[6 * n + k][...] = nv

    vm = pl.BlockSpec(memory_space=pltpu.VMEM)
    shapes = [SDS(w.shape, F32) for w in ws]
    res = pl.pallas_call(
        body, name="adamw_small", in_specs=[vm] * (4 * n), out_specs=[vm] * (3 * n), out_shape=shapes * 3,
    )(*ws, *gs, *ms, *vs)
    return res[:n], res[n:2 * n], res[2 * n:]


def _pad_rows(a, rows):
    flat = a.reshape(-1)
    return jnp.pad(flat, (0, rows * 128 - flat.shape[0])).reshape(rows, 128)


def kernel(x, norm_g, w_in, conv_w, q_norm_g, k_norm_g, sinks, w_conv_out, w_attn_out, gate_b, w_out, loss_target, m_norm_g, m_w_in, m_conv_w, m_q_norm_g, m_k_norm_g, m_sinks, m_w_conv_out, m_w_attn_out, m_gate_b, m_w_out, v_norm_g, v_w_in, v_conv_w, v_q_norm_g, v_k_norm_g, v_sinks, v_w_conv_out, v_w_attn_out, v_gate_b, v_w_out):
    xi, yi, ci = _mesh_pos()
    chip = 2 * xi + yi
    c_idx = jnp.reshape(ci, (1,)).astype(jnp.int32)
    chip_idx = jnp.reshape(chip, (1,)).astype(jnp.int32)
    cc_idx = jnp.stack([ci, chip]).astype(jnp.int32)
    t = x.shape[1]
    xs = [x.reshape(t, D)]
    tgt = loss_target.reshape(t, D)

    full_w = [[_cast_w_in(chip_idx, w_in, l), _cast_w_small(chip_idx, w_conv_out, w_attn_out, w_out, l)]
              for l in range(N_LAYERS)]
    full_w[0][0] = _ride_alone(_gather_rider(full_w[0][:1], "A"), "gather_first_ici")[0]
    full_w[0][0] = _ride_alone(_gather_rider(full_w[0][:1], "B"), "gather_first_d2d")[0]
    placed = lax.dynamic_update_slice(jnp.zeros((N_LAYERS, 3, D), F32),
                                      jnp.where(ci == 0, conv_w, 0.0), (0, 0, chip * SH_ROW))
    conv_full = _allreduce_small(placed.reshape(96, 128)).reshape(N_LAYERS, 3, D)

    qg_s = jnp.tile(q_norm_g, (1, N_Q)) * SCALE
    kg_t = jnp.tile(k_norm_g, (1, N_KV))
    bias = _band_bias()
    saved = []
    for l in range(N_LAYERS):
        nxt = full_w[l + 1] if l + 1 < N_LAYERS else None
        h, ht = _rmsnorm_fwd(xs[l], norm_g[l:l + 1])
        (u_conv, u_qkv, u_za, u_gl), got = _in_proj(h, full_w[l][0], _gather_rider(nxt[:1], "A") if nxt else None)
        if nxt:
            nxt[0] = got[0]
        y_c = _conv_fwd(u_conv, conv_full[l])
        (qs, kvx), got = _qkv_prep(u_qkv, qg_s[l:l + 1], kg_t[l:l + 1],
                                   _gather_rider(full_w[0][1:], "A") if l == 0 else None)
        if l == 0:
            full_w[0][1] = _ride_alone(_gather_rider(got, "B"), "gather_first_small_d2d")[0]
        o, lse, got = _attn_fwd(qs, kvx, u_za, sinks[l:l + 1], bias, _merge_riders(
            _gather_rider(nxt[1:], "A"), _gather_rider(nxt[:1], "B")) if nxt else None)
        if nxt:
            nxt[1], nxt[0] = got
        (x_next, y_a, y_b, merged), got = _out_proj_fwd(xs[l], y_c, o, u_gl, gate_b[l:l + 1], full_w[l][1],
                                                        _gather_rider(nxt[1:], "B") if nxt else None)
        if nxt:
            nxt[1] = got[0]
        xs.append(x_next)
        saved.append((ht, u_conv, u_qkv, u_za, u_gl, y_c, o, y_a, y_b, merged, qs, kvx, lse))

    dout, sq = _loss_head(xs[N_LAYERS], tgt)

    small, t_in, t_sm = [None] * N_LAYERS, [None] * N_LAYERS, [None] * N_LAYERS
    halves = None

    for l in reversed(range(N_LAYERS)):
        w_full, w_sm = full_w[l]
        last = l == 0
        ht, u_conv, u_qkv, u_za, u_gl, y_c, o, y_a, y_b, merged, qs, kvx, lse = saved[l]
        (d_ya, d_yb, du_gl, d_yc, d_o, dgb), _ = _out_proj_bwd(dout, y_a, y_b, u_gl, gate_b[l:l + 1], w_sm, None)
        g_sm = _small_wgrads(y_c, d_ya, o, d_yb, merged, dout).reshape(3, 4, 2, 128, D)
        (du_conv, dcw), got = _conv_bwd(d_yc, u_conv, conv_full[l], _merge_riders(
            _scatter_rider(None, halves[3]) if halves else None, _swap_rider(None, g_sm) if last else None))
        if halves:
            t_sm[l + 1] = _final_sum_sm(chip_idx, halves[2], got[0])
        if last:
            f_sm0, h_sm0 = _add_halves_sm(c_idx, g_sm, got[-1])
        (dqs, dkv, dza, dsk), got = _attn_bwd(d_o, qs, kvx, u_za, lse, sinks[l:l + 1], bias,
                                              _scatter_rider(halves[1], None) if halves else None)
        if halves:
            t_in[l + 1] = _final_sum_in(halves[0], got[0])
        dsk = jnp.sum(dsk[0].reshape(N_KV, 2, 2, BLK), axis=-1).transpose(0, 2, 1).reshape(N_Q)
        (du_attn, dqg, dkg), got = _qkv_post(u_qkv, dqs, dkv, dza, qg_s[l:l + 1], kg_t[l:l + 1],
                                             _scatter_rider(None, h_sm0) if last else None)
        if last:
            t_sm[0] = _final_sum_sm(chip_idx, f_sm0, got[0])
        du = (du_conv, du_attn, du_gl)
        g_in = _in_proj_wgrad(ht, du).reshape(2, 512, N_IN)
        if last:
            f_in0, h_in0 = _add_halves_in(cc_idx, g_in, _ride_alone(_swap_rider(g_in, None), "swap_last")[0])
        dht, got = _in_proj_bwd(du, w_full, _scatter_rider(h_in0, None) if last else _swap_rider(g_in, g_sm))
        if last:
            t_in[0] = _final_sum_in(f_in0, got[0])
        else:
            halves = _add_halves_in(cc_idx, g_in, got[0]) + _add_halves_sm(c_idx, g_sm, got[1])
        dout, dng = _rmsnorm_bwd(dht, xs[l], norm_g[l:l + 1], dout)
        small[l] = (jnp.sum(dng, axis=0), SCALE * jnp.sum(dqg.reshape(8 * N_Q, HEAD), axis=0),
                    jnp.sum(dkg.reshape(8 * N_KV, HEAD), axis=0), dsk, jnp.sum(dgb, axis=0), dcw[:3])
    grad_x = dout.reshape(1, t, D)

    stack = lambda k: jnp.stack([small[l][k] for l in range(N_LAYERS)])
    pack = jnp.concatenate([_pad_rows(stack(0), 32), _pad_rows(stack(1), 8), _pad_rows(stack(2), 8),
                            _pad_rows(stack(3), 8), _pad_rows(stack(4), 64), _pad_rows(stack(5), 96),
                            _pad_rows(jnp.sum(sq) * (0.5 / D), 8)], axis=0)
    red = _allreduce_small(pack)
    loss = red[216, 0]
    g_norm_g = red[0:32].reshape(N_LAYERS, D)
    g_q_norm_g = red[32:40].reshape(-1)[:N_LAYERS * HEAD].reshape(N_LAYERS, HEAD)
    g_k_norm_g = red[40:48].reshape(-1)[:N_LAYERS * HEAD].reshape(N_LAYERS, HEAD)
    g_sinks = red[48:56].reshape(-1)[:N_LAYERS * N_Q].reshape(N_LAYERS, N_Q)
    g_gate_b = red[56:120].reshape(N_LAYERS, 2 * D)
    g_conv_full = red[120:216].reshape(N_LAYERS, 3, D)
    g_conv_w = lax.dynamic_slice(g_conv_full, (0, 0, chip * SH_ROW), (N_LAYERS, 3, SH_ROW))

    g_w_in, g_w_co, g_w_ao, g_w_out = _join_halves(t_in, t_sm)

    r_in = N_LAYERS * D
    g_w_in, d_in, nm_in, nv_in = (a.reshape(N_LAYERS, D, SH_IN) for a in _adamw_big(
        w_in.reshape(r_in, SH_IN), g_w_in.reshape(r_in, SH_IN), m_w_in.reshape(r_in, SH_IN),
        v_w_in.reshape(r_in, SH_IN), "adamw_w_in"))
    r_sm = N_LAYERS * SH_ROW
    big = {}
    for nm, w, g, m, v in (("co", w_conv_out, g_w_co, m_w_conv_out, v_w_conv_out),
                           ("ao", w_attn_out, g_w_ao, m_w_attn_out, v_w_attn_out),
                           ("out", w_out, g_w_out, m_w_out, v_w_out)):
        big[nm] = tuple(a.reshape(N_LAYERS, SH_ROW, D) for a in _adamw_big(
            w.reshape(r_sm, D), g.reshape(r_sm, D), m.reshape(r_sm, D), v.reshape(r_sm, D), "adamw_w_small"))
    g_w_co, g_w_ao, g_w_out = big["co"][0], big["ao"][0], big["out"][0]
    sm_w = [norm_g, conv_w, q_norm_g, k_norm_g, sinks, gate_b]
    sm_g = [g_norm_g, g_conv_w, g_q_norm_g, g_k_norm_g, g_sinks, g_gate_b]
    sm_m = [m_norm_g, m_conv_w, m_q_norm_g, m_k_norm_g, m_sinks, m_gate_b]
    sm_v = [v_norm_g, v_conv_w, v_q_norm_g, v_k_norm_g, v_sinks, v_gate_b]
    sd, snm, snv = _adamw_small(sm_w, sm_g, sm_m, sm_v)

    def order(norm, w_in_, conv, qn, kn, sk, co, ao, gb, wo):
        return [norm, w_in_, conv, qn, kn, sk, co, ao, gb, wo]

    grads = order(g_norm_g, g_w_in, g_conv_w, g_q_norm_g, g_k_norm_g, g_sinks, g_w_co, g_w_ao, g_gate_b, g_w_out)
    deltas = order(sd[0], d_in, sd[1], sd[2], sd[3], sd[4], big["co"][1], big["ao"][1], sd[5], big["out"][1])
    new_m = order(snm[0], nm_in, snm[1], snm[2], snm[3], snm[4], big["co"][2], big["ao"][2], snm[5], big["out"][2])
    new_v = order(snv[0], nv_in, snv[1], snv[2], snv[3], snv[4], big["co"][3], big["ao"][3], snv[5], big["out"][3])
    return (loss, grad_x, *grads, *deltas, *new_m, *new_v)
```

```python
import functools

import jax
import jax.numpy as jnp
from jax import lax
from jax.experimental import pallas as pl
from jax.experimental.pallas import tpu as pltpu

F32, BF16 = jnp.float32, jnp.bfloat16
SDS = jax.ShapeDtypeStruct
MESH = pl.DeviceIdType.MESH
ANY = pl.BlockSpec(memory_space=pl.ANY)

D = 1024
N_IN = 8704
N_LAYERS = 4
N_Q, N_KV, HEAD = 16, 4, 64
GROUP = N_Q // N_KV
BLK = 128
EPS = 1e-6
NEG = -1e30
SCALE = HEAD ** -0.5
SH_IN = N_IN // 4
SH_ROW = D // 4
CB = 512
SEG_CONV, SEG_Q, SEG_KV, SEG_ZA, SEG_GL = (0, 8), (8, 2), (10, 1), (11, 2), (13, 4)
VMEM_BIG = 56 * 1024 * 1024

ADAM_LR, ADAM_B1, ADAM_B2, ADAM_EPS, ADAM_WD, ADAM_STEP = 0.001, 0.9, 0.999, 1e-08, 0.01, 10


def _cp(*sem, vmem=None):
    return pltpu.CompilerParams(dimension_semantics=sem if sem else None, vmem_limit_bytes=vmem)


def _sigmoid(z):
    return 1.0 / (1.0 + jnp.exp(-z))


def _dot(a, b):
    return jnp.dot(a, b, preferred_element_type=F32)


def _dot_nt(a, b):
    return lax.dot_general(a, b, (((1,), (1,)), ((), ())), preferred_element_type=F32)


def _dot_tn(a, b):
    return lax.dot_general(a, b, (((0,), (0,)), ((), ())), preferred_element_type=F32)


def _rms(xh):
    r = lax.rsqrt(jnp.mean(xh * xh, axis=-1, keepdims=True) + EPS)
    return xh * r, r


def _fold8(v):
    return jnp.sum(v.reshape(v.shape[0] // 8, 8, v.shape[1]), axis=0)


def _cast_w_in(chip_idx, w, layer):
    def body(j_ref, i_ref, o_ref):
        o_ref[...] = i_ref[...].astype(BF16)

    return pl.pallas_call(
        body, name="cast_w_in",
        grid_spec=pltpu.PrefetchScalarGridSpec(
            num_scalar_prefetch=1, grid=(2,),
            in_specs=[pl.BlockSpec((None, 512, SH_IN), lambda i, j: (layer, i, 0))],
            out_specs=pl.BlockSpec((512, SH_IN), lambda i, j: (i, j[0]))),
        out_shape=SDS((D, N_IN), BF16), compiler_params=_cp("parallel"),
    )(chip_idx, w)


def _cast_w_small(chip_idx, a, b, c, layer):
    def body(j_ref, a_ref, b_ref, c_ref, o_ref):
        o_ref[0] = a_ref[...].astype(BF16)
        o_ref[1] = b_ref[...].astype(BF16)
        o_ref[2] = c_ref[...].astype(BF16)

    spec = pl.BlockSpec((None, SH_ROW, D), lambda i, j: (layer, 0, 0))
    return pl.pallas_call(
        body, name="cast_w_small",
        grid_spec=pltpu.PrefetchScalarGridSpec(
            num_scalar_prefetch=1, grid=(1,), in_specs=[spec, spec, spec],
            out_specs=pl.BlockSpec((3, SH_ROW, D), lambda i, j: (0, j[0], 0))),
        out_shape=SDS((3, D, D), BF16), compiler_params=_cp("parallel"),
    )(chip_idx, a, b, c)


def _mesh_pos():
    return lax.axis_index("x"), lax.axis_index("y"), lax.axis_index("c")


def _other_chips(x, y):
    return [(1 - x, y), (x, 1 - y), (1 - x, 1 - y)]


class _Rider:
    def __init__(self, ins, out_shape, n, copies, aliases=()):
        self.ins, self.out_shape, self.n, self.copies, self.aliases = list(ins), list(out_shape), n, copies, aliases


def _merge_riders(*riders):
    riders = [r for r in riders if r is not None]
    if len(riders) < 2:
        return riders[0] if riders else None

    def copies(ins, outs, send, recv, base=0):
        cps, i0, o0 = [], 0, 0
        for r in riders:
            cps += r.copies(ins[i0:i0 + len(r.ins)], outs[o0:o0 + len(r.out_shape)], send, recv, base + len(cps))
            i0, o0 = i0 + len(r.ins), o0 + len(r.out_shape)
        return cps

    aliases, i0, o0 = [], 0, 0
    for r in riders:
        aliases += [(i0 + i, o0 + o) for i, o in r.aliases]
        i0, o0 = i0 + len(r.ins), o0 + len(r.out_shape)
    return _Rider(sum((r.ins for r in riders), []), sum((r.out_shape for r in riders), []),
                  sum(r.n for r in riders), copies, tuple(aliases))


def _rcopy(src, dst, send, recv, k, to):
    return pltpu.make_async_remote_copy(src_ref=src, dst_ref=dst, send_sem=send.at[k], recv_sem=recv.at[k],
                                        device_id=to, device_id_type=MESH)


def _hosted_call(body, rider, *, name, grid, in_specs, out_specs, out_shape, args, scratch_shapes=(), vmem=None):
    n_in, n_out, n_scr = len(in_specs), len(out_specs), len(scratch_shapes)
    r_in, r_out = len(rider.ins), len(rider.out_shape)

    def full_body(*refs):
        host_in, rid_in = refs[:n_in], refs[n_in:n_in + r_in]
        o0 = n_in + r_in
        host_out, rid_out = refs[o0:o0 + n_out], refs[o0 + n_out:o0 + n_out + r_out]
        s0 = o0 + n_out + r_out
        host_scr, (send, recv) = refs[s0:s0 + n_scr], refs[s0 + n_scr:]
        if body is None:
            cps = rider.copies(rid_in, rid_out, send, recv)
            for cp in cps:
                cp.start()
            for cp in cps:
                cp.wait()
            return
        ids = [pl.program_id(a) for a in range(len(grid))]
        first = functools.reduce(lambda p, q: p & q, [i == 0 for i in ids])
        last = functools.reduce(lambda p, q: p & q, [i == g - 1 for i, g in zip(ids, grid)])

        @pl.when(first)
        def _():
            for cp in rider.copies(rid_in, rid_out, send, recv):
                cp.start()

        body(*host_in, *host_out, *host_scr)

        @pl.when(last)
        def _():
            for cp in rider.copies(rid_in, rid_out, send, recv):
                cp.wait()

    res = pl.pallas_call(
        full_body, name=name, grid=grid if body is not None else (),
        in_specs=list(in_specs) + [ANY] * r_in, out_specs=list(out_specs) + [ANY] * r_out,
        out_shape=list(out_shape) + rider.out_shape,
        scratch_shapes=list(scratch_shapes) + [pltpu.SemaphoreType.DMA((rider.n,))] * 2,
        input_output_aliases={n_in + i: n_out + o for i, o in rider.aliases},
        compiler_params=_cp(*(("arbitrary",) * len(grid) if body is not None else ()), vmem=vmem),
    )(*args, *rider.ins)
    return res[:n_out], res[n_out:]


def _call(body, rider, **kw):
    if rider is not None:
        return _hosted_call(body, rider, **kw)
    res = pl.pallas_call(
        body, name=kw["name"], grid=kw["grid"], in_specs=list(kw["in_specs"]), out_specs=list(kw["out_specs"]),
        out_shape=list(kw["out_shape"]), scratch_shapes=list(kw.get("scratch_shapes", ())),
        compiler_params=_cp(*(("arbitrary",) * len(kw["grid"])), vmem=kw.get("vmem")),
    )(*kw["args"])
    return res, []


def _gather_rider(arrays, stage):
    def copies(ins, outs, send, recv, base=0):
        x, y, c = _mesh_pos()
        cps = []
        for full in outs:
            for k, chip in enumerate(_other_chips(x, y)):
                whose = 2 * x + y if stage == "A" else 2 * chip[0] + chip[1]
                if len(full.shape) == 2:
                    reg = full.at[pl.ds(pl.multiple_of(c * 512, 512), 512), pl.ds(pl.multiple_of(whose * SH_IN, 128), SH_IN)]
                else:
                    reg = full.at[:, pl.ds(pl.multiple_of(whose * SH_ROW + c * 128, 128), 128), :]
                to = (*chip, c) if stage == "A" else (x, y, 1 - c)
                cps.append(_rcopy(reg, reg, send, recv, base + len(cps), to))
        return cps

    return _Rider(arrays, [SDS(v.shape, v.dtype) for v in arrays], 3 * len(arrays), copies,
                  aliases=tuple((i, i) for i in range(len(arrays))))


def _staged_copies(copies, stages, sem_in, sem_out):
    busy, count = {}, {}
    for idx, (src, dst, kind) in enumerate(copies):
        slot = count.get(kind, 0) % 2
        count[kind] = count.get(kind, 0) + 1
        if (kind, slot) in busy:
            busy.pop((kind, slot)).wait()
        buf = stages[kind].at[slot]
        cin = pltpu.make_async_copy(src, buf, sem_in.at[idx])
        cin.start()
        cin.wait()
        cout = pltpu.make_async_copy(buf, dst, sem_out.at[idx])
        cout.start()
        busy[(kind, slot)] = cout
    for cp in busy.values():
        cp.wait()


def _allreduce_small(pack):
    rows = pack.shape[0]

    def body(p_ref, o_ref, buf, send, recv):
        x, y, c = _mesh_pos()
        me = 4 * x + 2 * y + c
        sends = []
        for r in range(1, 8):
            to = (x if not (r & 4) else 1 - x, y if not (r & 2) else 1 - y, c if not (r & 1) else 1 - c)
            cp = pltpu.make_async_remote_copy(src_ref=p_ref, dst_ref=buf.at[me], send_sem=send.at[r - 1],
                                              recv_sem=recv.at[r - 1], device_id=to, device_id_type=MESH)
            cp.start()
            sends.append(cp)
        buf[me] = p_ref[...]
        for r in range(1, 8):
            frm = (4 * x + 2 * y + c) ^ r
            pltpu.make_async_remote_copy(src_ref=p_ref, dst_ref=buf.at[frm], send_sem=send.at[r - 1],
                                         recv_sem=recv.at[r - 1], device_id=(x, y, c), device_id_type=MESH).wait_recv()
        acc = buf[0]
        for d in range(1, 8):
            acc = acc + buf[d]
        o_ref[...] = acc
        for cp in sends:
            cp.wait_send()

    vm = pl.BlockSpec(memory_space=pltpu.VMEM)
    return pl.pallas_call(
        body, name="allreduce_small", in_specs=[vm], out_specs=vm, out_shape=SDS((rows, 128), F32),
        scratch_shapes=[pltpu.VMEM((8, rows, 128), F32), pltpu.SemaphoreType.DMA((7,)), pltpu.SemaphoreType.DMA((7,))],
    )(pack)


def _rmsnorm_fwd(x, g):
    t = x.shape[0]
    tm = min(512, t)

    def body(x_ref, g_ref, h_ref, ht_ref):
        xv = x_ref[...]
        r = lax.rsqrt(jnp.mean(xv * xv, axis=-1, keepdims=True) + EPS)
        h = xv * r * g_ref[...]
        h_ref[...] = h.astype(BF16)
        ht_ref[...] = h.T.astype(BF16)

    return pl.pallas_call(
        body, name="rmsnorm_fwd", grid=(t // tm,),
        in_specs=[pl.BlockSpec((tm, D), lambda i: (i, 0)), pl.BlockSpec((1, D), lambda i: (0, 0))],
        out_specs=[pl.BlockSpec((tm, D), lambda i: (i, 0)), pl.BlockSpec((D, tm), lambda i: (0, i))],
        out_shape=[SDS((t, D), BF16), SDS((D, t), BF16)],
        compiler_params=_cp("parallel", vmem=VMEM_BIG),
    )(x, g)


FWD_SEGS = ((0, 8), (8, 3), (11, 2), (13, 4))


def _in_proj(h, w_full, rider):
    t = h.shape[0]
    tm = min(2048, t)

    def body(a_ref, b_ref, *o_refs):
        j = pl.program_id(1)
        for o_ref, (off, nblk) in zip(o_refs, FWD_SEGS):
            @pl.when((j >= off) & (j < off + nblk))
            def _():
                o_ref[...] = _dot(a_ref[...], b_ref[...]).astype(BF16)

    def out(seg):
        off, nblk = seg
        return pl.BlockSpec((tm, CB), lambda i, j: (i, jnp.clip(j - off, 0, nblk - 1)))

    res, got = _call(
        body, rider, name="in_proj", grid=(t // tm, N_IN // CB),
        in_specs=[pl.BlockSpec((tm, D), lambda i, j: (i, 0)), pl.BlockSpec((D, CB), lambda i, j: (0, j))],
        out_specs=[out(s) for s in FWD_SEGS], out_shape=[SDS((t, s[1] * CB), BF16) for s in FWD_SEGS],
        args=(h, w_full), vmem=VMEM_BIG)
    return res, got


def _conv_fwd(u_conv, conv_w):
    t = u_conv.shape[0]
    tm = min(256, t)
    hb = tm // 16

    def body(v_ref, b_ref, c_ref, z_ref, hv_ref, hc_ref, w_ref, y_ref):
        i = pl.program_id(0)
        cv = c_ref[...].astype(F32) * v_ref[...].astype(F32)
        halo = hc_ref[...].astype(F32) * hv_ref[...].astype(F32)
        halo = jnp.where(i > 0, halo, 0.0)
        row = lax.broadcasted_iota(jnp.int32, (tm, 1), 0)
        s1 = jnp.where(row == 0, halo[15:16], pltpu.roll(cv, 1, 0))
        s2 = jnp.where(row == 0, halo[14:15], jnp.where(row == 1, halo[15:16], pltpu.roll(cv, 2, 0)))
        conv = w_ref[0:1, :] * s2 + w_ref[1:2, :] * s1 + w_ref[2:3, :] * cv
        z = z_ref[...].astype(F32)
        y_ref[...] = (b_ref[...].astype(F32) * conv * (z * _sigmoid(z))).astype(BF16)

    def col(k):
        return pl.BlockSpec((tm, D), lambda i: (i, k))

    def halo(k):
        return pl.BlockSpec((16, D), lambda i: (jnp.maximum(i * hb - 1, 0), k))

    return pl.pallas_call(
        body, name="conv_fwd", grid=(t // tm,),
        in_specs=[col(0), col(1), col(2), col(3), halo(0), halo(2), pl.BlockSpec((3, D), lambda i: (0, 0))],
        out_specs=pl.BlockSpec((tm, D), lambda i: (i, 0)), out_shape=SDS((t, D), BF16),
        compiler_params=_cp("parallel", vmem=VMEM_BIG),
    )(u_conv, u_conv, u_conv, u_conv, u_conv, u_conv, conv_w)


KVX = 4 * N_KV * 128


def _iota2(shape):
    return lax.broadcasted_iota(jnp.int32, shape, 0), lax.broadcasted_iota(jnp.int32, shape, 1)


def _head_sum(v):
    r, c = _iota2((128, 128))
    ones = ((r >> 6) == (c >> 6)).astype(BF16)
    hi = v.astype(BF16)
    lo = (v - hi.astype(F32)).astype(BF16)
    return jnp.concatenate([_dot(hi[:, g:g + 128], ones) + _dot(lo[:, g:g + 128], ones)
                            for g in range(0, v.shape[1], 128)], axis=1)


def _expand_mats():
    r, c = _iota2((N_KV * HEAD, N_KV * 128))
    base = ((r >> 6) << 7) + (r & 63)
    return (c == base).astype(BF16), (c == base + 64).astype(BF16)


def _fold_mat():
    r, c = _iota2((N_KV * 128, N_KV * HEAD))
    return (((r >> 7) == (c >> 6)) & ((r & 63) == (c & 63))).astype(BF16)


def _qkv_prep(u_qkv, qg_s, kg_t, rider):
    t = u_qkv.shape[0]
    tm = min(512, t)

    def body(u_ref, qg_ref, kg_ref, qs_ref, kvx_ref):
        q = u_ref[:, 0:D].astype(F32)
        rq = lax.rsqrt(_head_sum(q * q) * (1.0 / HEAD) + EPS)
        qs_ref[...] = (q * rq * qg_ref[...]).astype(BF16)
        k = u_ref[:, D:D + 256].astype(F32)
        rk = lax.rsqrt(_head_sum(k * k) * (1.0 / HEAD) + EPS)
        kn = (k * rk * kg_ref[...]).astype(BF16)
        v = u_ref[:, D + 256:D + 512]
        e_lo, e_hi = _expand_mats()
        kvx_ref[:, 0:512] = _dot(kn, e_lo).astype(BF16)
        kvx_ref[:, 512:1024] = _dot(kn, e_hi).astype(BF16)
        kvx_ref[:, 1024:1536] = _dot(v, e_lo).astype(BF16)
        kvx_ref[:, 1536:2048] = _dot(v, e_hi).astype(BF16)

    return _call(
        body, rider, name="qkv_prep", grid=(t // tm,),
        in_specs=[pl.BlockSpec((tm, 1536), lambda i: (i, 0)), pl.BlockSpec((1, D), lambda i: (0, 0)),
                  pl.BlockSpec((1, 256), lambda i: (0, 0))],
        out_specs=[pl.BlockSpec((tm, D), lambda i: (i, 0)), pl.BlockSpec((tm, KVX), lambda i: (i, 0))],
        out_shape=[SDS((t, D), BF16), SDS((t, KVX), BF16)], args=(u_qkv, qg_s, kg_t), vmem=VMEM_BIG)


def _band_bias():
    j, r = _iota2((2 * BLK, 2 * BLK))
    diff = (r & (BLK - 1)) - j + BLK
    band = (diff >= 0) & (diff < BLK)
    return jnp.stack([jnp.where(band & (j >= BLK), 0.0, NEG), jnp.where(band, 0.0, NEG)]).astype(F32)


def _pair_rows(ref_or_val, hk):
    return jnp.concatenate([ref_or_val[:, 256 * hk:256 * hk + 128], ref_or_val[:, 256 * hk + 128:256 * hk + 256]], axis=0)


def _sink_row(sink_ref, hk, half):
    return jnp.concatenate([jnp.full((1, BLK), sink_ref[0, GROUP * hk + half], F32),
                            jnp.full((1, BLK), sink_ref[0, GROUP * hk + 2 + half], F32)], axis=1)


def _kv_operands(kvb, hk, half):
    return (kvb[:, 512 * half + 128 * hk:512 * half + 128 * hk + 128],
            kvb[:, 1024 + 512 * half + 128 * hk:1024 + 512 * half + 128 * hk + 128])


def _attn_fwd(qs, kvx, u_za, sinks, bias, rider):
    t = qs.shape[0]
    nb = t // BLK

    def body(q_ref, kc_ref, kp_ref, za_ref, sink_ref, bias_ref, o_ref, lse_ref):
        kvb = jnp.concatenate([kp_ref[...], kc_ref[...]], axis=0)
        bias_v = bias_ref[...]
        key0 = lax.broadcasted_iota(jnp.int32, (2 * BLK, 1), 0) == 0
        ones = jnp.ones((2 * BLK, 128), BF16)
        cols = []
        for hk in range(N_KV):
            qpp = _pair_rows(q_ref, hk)
            opp = None
            for half in range(2):
                kx, vx = _kv_operands(kvb, hk, half)
                s = _dot_nt(kx, qpp) + bias_v
                sink = _sink_row(sink_ref, hk, half)
                m = jnp.maximum(jnp.max(s, axis=0, keepdims=True), sink)
                p = jnp.exp(s - m)
                es = jnp.exp(sink - m)
                lse_ref[0, 2 * hk + half:2 * hk + half + 1, :] = m + jnp.log(jnp.sum(p, axis=0, keepdims=True) + es)
                pe = jnp.where(key0, es, p).astype(BF16)
                rhs = jnp.concatenate([jnp.where(key0, jnp.zeros_like(vx), vx), ones], axis=1)
                nd = _dot_tn(pe, rhs)
                o = nd[:, :128] * (1.0 / nd[:, 128:])
                opp = o if opp is None else opp + o
            cols += [opp[:BLK], opp[BLK:]]
        za = za_ref[...].astype(F32)
        o_ref[...] = (jnp.concatenate(cols, axis=1) * (za * _sigmoid(za))).astype(BF16)

    prev = lambda n: jnp.maximum(n - 1, 0)
    (o, lse), got = _call(
        body, rider, name="attn_fwd", grid=(nb,),
        in_specs=[pl.BlockSpec((BLK, D), lambda n: (n, 0)),
                  pl.BlockSpec((BLK, KVX), lambda n: (n, 0)), pl.BlockSpec((BLK, KVX), lambda n: (prev(n), 0)),
                  pl.BlockSpec((BLK, D), lambda n: (n, 0)), pl.BlockSpec(memory_space=pltpu.SMEM),
                  pl.BlockSpec((None, 2 * BLK, 2 * BLK), lambda n: (jnp.minimum(n, 1), 0, 0))],
        out_specs=[pl.BlockSpec((BLK, D), lambda n: (n, 0)), pl.BlockSpec((1, 8, 2 * BLK), lambda n: (n, 0, 0))],
        out_shape=[SDS((t, D), BF16), SDS((nb, 8, 2 * BLK), F32)],
        args=(qs, kvx, kvx, u_za, sinks, bias), vmem=VMEM_BIG)
    return o, lse, got


def _out_proj_fwd(x, y_c, o, u_gl, gate_b, w_sm, rider):
    t = x.shape[0]
    tm = min(512, t)

    def body(x_ref, yc_ref, o_ref, gla_ref, glb_ref, gb_ref, wco_ref, wao_ref, wout_ref,
             xn_ref, ya_ref, yb_ref, mg_ref):
        ya = _dot(yc_ref[...], wco_ref[...])
        yb = _dot(o_ref[...], wao_ref[...])
        gb = gb_ref[...]
        ga_ = _sigmoid(gla_ref[...].astype(F32) + gb[:, :D])
        gb_ = _sigmoid(glb_ref[...].astype(F32) + gb[:, D:])
        merged = (ga_ * ya + gb_ * yb).astype(BF16)
        ya_ref[...] = ya.astype(BF16)
        yb_ref[...] = yb.astype(BF16)
        mg_ref[...] = merged
        xn_ref[...] = x_ref[...] + _dot(merged, wout_ref[...])

    row = pl.BlockSpec((tm, D), lambda i: (i, 0))
    wspec = lambda a: pl.BlockSpec((None, D, D), lambda i: (a, 0, 0))
    return _call(
        body, rider, name="out_proj_fwd", grid=(t // tm,),
        in_specs=[row, row, row, pl.BlockSpec((tm, D), lambda i: (i, 0)), pl.BlockSpec((tm, D), lambda i: (i, 1)),
                  pl.BlockSpec((1, 2 * D), lambda i: (0, 0)), wspec(0), wspec(1), wspec(2)],
        out_specs=[row, row, row, row],
        out_shape=[SDS((t, D), F32), SDS((t, D), BF16), SDS((t, D), BF16), SDS((t, D), BF16)],
        args=(x, y_c, o, u_gl, u_gl, gate_b, w_sm, w_sm, w_sm), vmem=VMEM_BIG)


def _loss_head(y, tgt):
    t = y.shape[0]
    tm = min(512, t)

    def body(y_ref, t_ref, dy_ref, acc_ref):
        @pl.when(pl.program_id(0) == 0)
        def _():
            acc_ref[...] = jnp.zeros_like(acc_ref)
        err = y_ref[...] - t_ref[...]
        dy_ref[...] = err * (1.0 / D)
        sq = _fold8(err * err)
        tot = sq[:, 0:128]
        for k in range(1, D // 128):
            tot = tot + sq[:, 128 * k:128 * (k + 1)]
        acc_ref[...] += tot

    row = pl.BlockSpec((tm, D), lambda i: (i, 0))
    return pl.pallas_call(
        body, name="loss_head", grid=(t // tm,), in_specs=[row, row],
        out_specs=[row, pl.BlockSpec((8, 128), lambda i: (0, 0))],
        out_shape=[SDS((t, D), F32), SDS((8, 128), F32)], compiler_params=_cp("arbitrary"),
    )(y, tgt)


def _out_proj_bwd(dout, y_a, y_b, u_gl, gate_b, w_sm, rider):
    t = dout.shape[0]
    tm = min(512, t)

    def body(do_ref, ya_ref, yb_ref, gla_ref, glb_ref, gb_ref, wco_ref, wao_ref, wout_ref,
             dya_ref, dyb_ref, dgl_ref, dyc_ref, dob_ref, dgb_ref):
        @pl.when(pl.program_id(0) == 0)
        def _():
            dgb_ref[...] = jnp.zeros_like(dgb_ref)
        dm = _dot_nt(do_ref[...].astype(BF16), wout_ref[...])
        gb = gb_ref[...]
        ga_ = _sigmoid(gla_ref[...].astype(F32) + gb[:, :D])
        gb_ = _sigmoid(glb_ref[...].astype(F32) + gb[:, D:])
        dya = (ga_ * dm).astype(BF16)
        dyb = (gb_ * dm).astype(BF16)
        dgla = ya_ref[...].astype(F32) * dm * (ga_ * (1.0 - ga_))
        dglb = yb_ref[...].astype(F32) * dm * (gb_ * (1.0 - gb_))
        dya_ref[...] = dya
        dyb_ref[...] = dyb
        dgl_ref[:, :D] = dgla.astype(BF16)
        dgl_ref[:, D:] = dglb.astype(BF16)
        dgb_ref[:, :D] += _fold8(dgla)
        dgb_ref[:, D:] += _fold8(dglb)
        dyc_ref[...] = _dot_nt(dya, wco_ref[...]).astype(BF16)
        dob_ref[...] = _dot_nt(dyb, wao_ref[...]).astype(BF16)

    row = pl.BlockSpec((tm, D), lambda i: (i, 0))
    wspec = lambda a: pl.BlockSpec((None, D, D), lambda i: (a, 0, 0))
    return _call(
        body, rider, name="out_proj_bwd", grid=(t // tm,),
        in_specs=[row, row, row, pl.BlockSpec((tm, D), lambda i: (i, 0)), pl.BlockSpec((tm, D), lambda i: (i, 1)),
                  pl.BlockSpec((1, 2 * D), lambda i: (0, 0)), wspec(0), wspec(1), wspec(2)],
        out_specs=[row, row, pl.BlockSpec((tm, 2 * D), lambda i: (i, 0)), row, row,
                   pl.BlockSpec((8, 2 * D), lambda i: (0, 0))],
        out_shape=[SDS((t, D), BF16), SDS((t, D), BF16), SDS((t, 2 * D), BF16), SDS((t, D), BF16), SDS((t, D), BF16),
                   SDS((8, 2 * D), F32)],
        args=(dout, y_a, y_b, u_gl, u_gl, gate_b, w_sm, w_sm, w_sm), vmem=VMEM_BIG)


def _small_wgrads(y_c, d_ya, o, d_yb, merged, dout):
    t = y_c.shape[0]
    tk = min(512, t)

    def body(yc_ref, dya_ref, o_ref, dyb_ref, mg_ref, do_ref, g_ref):
        @pl.when(pl.program_id(0) == 0)
        def _():
            g_ref[...] = jnp.zeros_like(g_ref)
        g_ref[0] += _dot_tn(yc_ref[...], dya_ref[...])
        g_ref[1] += _dot_tn(o_ref[...], dyb_ref[...])
        g_ref[2] += _dot_tn(mg_ref[...], do_ref[...].astype(BF16))

    row = pl.BlockSpec((tk, D), lambda k: (k, 0))
    return pl.pallas_call(
        body, name="small_wgrads", grid=(t // tk,), in_specs=[row] * 6,
        out_specs=pl.BlockSpec((3, D, D), lambda k: (0, 0, 0)), out_shape=SDS((3, D, D), F32),
        compiler_params=_cp("arbitrary", vmem=VMEM_BIG),
    )(y_c, d_ya, o, d_yb, merged, dout)


def _conv_bwd(d_yc, u_conv, conv_w, rider):
    t = d_yc.shape[0]
    tm = min(256, t)
    hb = tm // 16
    last_halo = t // 16 - 1
    n_steps = t // tm

    def body(dy_ref, v_ref, b_ref, c_ref, z_ref, hv_ref, hc_ref, ndy_ref, nb_ref, nz_ref, w_ref, du_ref, dw_ref):
        i = pl.program_id(0)

        @pl.when(i == 0)
        def _():
            dw_ref[...] = jnp.zeros_like(dw_ref)
        v, c = v_ref[...].astype(F32), c_ref[...].astype(F32)
        b, z = b_ref[...].astype(F32), z_ref[...].astype(F32)
        cv = c * v
        halo = jnp.where(i > 0, hc_ref[...].astype(F32) * hv_ref[...].astype(F32), 0.0)
        row = lax.broadcasted_iota(jnp.int32, (tm, 1), 0)
        s1 = jnp.where(row == 0, halo[15:16], pltpu.roll(cv, 1, 0))
        s2 = jnp.where(row == 0, halo[14:15], jnp.where(row == 1, halo[15:16], pltpu.roll(cv, 2, 0)))
        w0, w1, w2 = w_ref[0:1, :], w_ref[1:2, :], w_ref[2:3, :]
        conv = w0 * s2 + w1 * s1 + w2 * cv
        sig = _sigmoid(z)
        sz = z * sig
        dsz = sig * (1.0 + z * (1.0 - sig))
        dy = dy_ref[...].astype(F32)
        dconv = dy * b * sz
        nz = nz_ref[...].astype(F32)
        nxt = ndy_ref[...].astype(F32) * nb_ref[...].astype(F32) * (nz * _sigmoid(nz))
        nxt = jnp.where(i < n_steps - 1, nxt, 0.0)
        a1 = jnp.where(row == tm - 1, nxt[0:1], pltpu.roll(dconv, tm - 1, 0))
        a2 = jnp.where(row == tm - 2, nxt[0:1], jnp.where(row == tm - 1, nxt[1:2], pltpu.roll(dconv, tm - 2, 0)))
        dcv = w2 * dconv + w1 * a1 + w0 * a2
        du_ref[:, 0:D] = (dcv * c).astype(BF16)
        du_ref[:, D:2 * D] = (dy * conv * sz).astype(BF16)
        du_ref[:, 2 * D:3 * D] = (dcv * v).astype(BF16)
        du_ref[:, 3 * D:4 * D] = (dy * b * conv * dsz).astype(BF16)
        r8 = lax.broadcasted_iota(jnp.int32, (8, 1), 0)
        dw_ref[...] += jnp.where(r8 == 0, jnp.sum(dconv * s2, axis=0, keepdims=True),
                                 jnp.where(r8 == 1, jnp.sum(dconv * s1, axis=0, keepdims=True),
                                           jnp.where(r8 == 2, jnp.sum(dconv * cv, axis=0, keepdims=True), 0.0)))

    def col(k):
        return pl.BlockSpec((tm, D), lambda i: (i, k))

    def halo(k):
        return pl.BlockSpec((16, D), lambda i: (jnp.maximum(i * hb - 1, 0), k))

    def nxt(k):
        return pl.BlockSpec((16, D), lambda i: (jnp.minimum((i + 1) * hb, last_halo), k))

    return _call(
        body, rider, name="conv_bwd", grid=(t // tm,),
        in_specs=[col(0), col(0), col(1), col(2), col(3), halo(0), halo(2), nxt(0), nxt(1), nxt(3),
                  pl.BlockSpec((3, D), lambda i: (0, 0))],
        out_specs=[pl.BlockSpec((tm, 4 * D), lambda i: (i, 0)), pl.BlockSpec((8, D), lambda i: (0, 0))],
        out_shape=[SDS((t, 4 * D), BF16), SDS((8, D), F32)],
        args=(d_yc, u_conv, u_conv, u_conv, u_conv, u_conv, u_conv, d_yc, u_conv, u_conv, conv_w), vmem=VMEM_BIG)


def _attn_bwd(d_o, qs, kvx, u_za, lse, sinks, bias, rider):
    t = d_o.shape[0]
    nb = t // BLK

    def body(q_ref, kc_ref, kp_ref, za_ref, do_ref, lse_ref, sink_ref, bias_ref,
             dq_ref, dkv_ref, dza_ref, dsk_ref, carry_ref):
        n = pl.program_id(0)

        @pl.when(n == 0)
        def _():
            carry_ref[...] = jnp.zeros_like(carry_ref)
            dsk_ref[...] = jnp.zeros_like(dsk_ref)

        live = n < nb
        kvb = jnp.concatenate([kp_ref[...], kc_ref[...]], axis=0)
        bias_v = bias_ref[...]
        za = za_ref[...].astype(F32)
        sig = _sigmoid(za)
        dsa = sig * (1.0 + za * (1.0 - sig))
        do = jnp.where(live, do_ref[...].astype(F32), 0.0)
        dattn = (do * (za * sig)).astype(BF16)
        lo_lanes = lax.broadcasted_iota(jnp.int32, (1, 128), 1) < HEAD
        dq_cols, attn_cols, dk_cols, dv_cols, dsk_rows = [], [], [], [], []
        for hk in range(N_KV):
            qpp = _pair_rows(q_ref, hk)
            dapp = _pair_rows(dattn, hk)
            probs, dss, xk, xv = [], [], [], []
            for half in range(2):
                kx, vx = _kv_operands(kvb, hk, half)
                lse = lse_ref[0, 2 * hk + half:2 * hk + half + 1, :]
                prob = jnp.exp(_dot_nt(kx, qpp) + bias_v - lse)
                psink = jnp.exp(_sink_row(sink_ref, hk, half) - lse)
                tdp = prob * _dot_nt(vx, dapp)
                drow = jnp.sum(tdp, axis=0, keepdims=True)
                ds = (tdp - prob * drow).astype(BF16)
                prob_b = prob.astype(BF16)
                xk.append(_dot(ds, qpp))
                xv.append(_dot(prob_b, dapp))
                probs.append(prob_b)
                dss.append(ds)
                dsk_rows.append(-psink * drow)
            kcat = jnp.concatenate([kvb[:, 128 * hk:128 * hk + 128], kvb[:, 512 + 128 * hk:512 + 128 * hk + 128]], axis=0)
            vcat = jnp.concatenate([kvb[:, 1024 + 128 * hk:1024 + 128 * hk + 128],
                                    kvb[:, 1536 + 128 * hk:1536 + 128 * hk + 128]], axis=0)
            app = _dot_tn(jnp.concatenate(probs, axis=0), vcat)
            dqpp = _dot_tn(jnp.concatenate(dss, axis=0), kcat)
            dq_cols += [dqpp[:BLK], dqpp[BLK:]]
            attn_cols += [app[:BLK], app[BLK:]]
            dk_cols.append(jnp.where(lo_lanes, xk[0], xk[1]))
            dv_cols.append(jnp.where(lo_lanes, xv[0], xv[1]))

        @pl.when(live)
        def _():
            dq_ref[...] = jnp.concatenate(dq_cols, axis=1).astype(BF16)
            dza_ref[...] = (do * jnp.concatenate(attn_cols, axis=1) * dsa).astype(BF16)

        band = jnp.concatenate(dk_cols + dv_cols, axis=1)
        dkv_ref[...] = (band[:BLK] + carry_ref[...]).astype(BF16)
        carry_ref[...] = band[BLK:]
        dsk_ref[...] += jnp.broadcast_to(jnp.concatenate(dsk_rows, axis=1), (8, 2 * N_KV * 2 * BLK))

    cur = lambda n: jnp.minimum(n, nb - 1)
    prev = lambda n: jnp.maximum(n - 1, 0)
    return _call(
        body, rider, name="attn_bwd", grid=(nb + 1,),
        in_specs=[pl.BlockSpec((BLK, D), lambda n: (cur(n), 0)),
                  pl.BlockSpec((BLK, KVX), lambda n: (cur(n), 0)), pl.BlockSpec((BLK, KVX), lambda n: (prev(n), 0)),
                  pl.BlockSpec((BLK, D), lambda n: (cur(n), 0)), pl.BlockSpec((BLK, D), lambda n: (cur(n), 0)),
                  pl.BlockSpec((1, 8, 2 * BLK), lambda n: (cur(n), 0, 0)), pl.BlockSpec(memory_space=pltpu.SMEM),
                  pl.BlockSpec((None, 2 * BLK, 2 * BLK), lambda n: (jnp.minimum(n, 1), 0, 0))],
        out_specs=[pl.BlockSpec((BLK, D), lambda n: (cur(n), 0)), pl.BlockSpec((BLK, D), lambda n: (prev(n), 0)),
                   pl.BlockSpec((BLK, D), lambda n: (cur(n), 0)), pl.BlockSpec((8, 2 * D), lambda n: (0, 0))],
        out_shape=[SDS((t, D), BF16), SDS((t, D), BF16), SDS((t, D), BF16), SDS((8, 2 * D), F32)],
        scratch_shapes=[pltpu.VMEM((BLK, D), F32)],
        args=(qs, kvx, kvx, u_za, d_o, lse, sinks, bias), vmem=VMEM_BIG)


def _qkv_post(u_qkv, dqs, dkv, dza, qg_s, kg_t, rider):
    t = u_qkv.shape[0]
    tm = min(512, t)

    def norm_bwd(x, dy, g):
        r = lax.rsqrt(_head_sum(x * x) * (1.0 / HEAD) + EPS)
        xhat = x * r
        dxh = dy * g
        return r * (dxh - xhat * (_head_sum(dxh * xhat) * (1.0 / HEAD))), _fold8(dy * xhat)

    def body(u_ref, dq_ref, dkv_ref, dza_ref, qg_ref, kg_ref, du_ref, dqg_ref, dkg_ref):
        @pl.when(pl.program_id(0) == 0)
        def _():
            dqg_ref[...] = jnp.zeros_like(dqg_ref)
            dkg_ref[...] = jnp.zeros_like(dkg_ref)
        dq, gq = norm_bwd(u_ref[:, 0:D].astype(F32), dq_ref[...].astype(F32), qg_ref[...])
        fold = _fold_mat()
        dk, gk = norm_bwd(u_ref[:, D:D + 256].astype(F32), _dot(dkv_ref[:, 0:512], fold), kg_ref[...])
        du_ref[:, 0:D] = dq.astype(BF16)
        du_ref[:, D:D + 256] = dk.astype(BF16)
        du_ref[:, D + 256:D + 512] = _dot(dkv_ref[:, 512:1024], fold).astype(BF16)
        du_ref[:, D + 512:2 * D + 512] = dza_ref[...]
        dqg_ref[...] += gq
        dkg_ref[...] += gk

    row = pl.BlockSpec((tm, D), lambda i: (i, 0))
    return _call(
        body, rider, name="qkv_post", grid=(t // tm,),
        in_specs=[pl.BlockSpec((tm, 1536), lambda i: (i, 0)), row, row, row,
                  pl.BlockSpec((1, D), lambda i: (0, 0)), pl.BlockSpec((1, 256), lambda i: (0, 0))],
        out_specs=[pl.BlockSpec((tm, 2560), lambda i: (i, 0)), pl.BlockSpec((8, D), lambda i: (0, 0)),
                   pl.BlockSpec((8, 256), lambda i: (0, 0))],
        out_shape=[SDS((t, 2560), BF16), SDS((8, D), F32), SDS((8, 256), F32)],
        args=(u_qkv, dqs, dkv, dza, qg_s, kg_t), vmem=VMEM_BIG)


N_GRAN = N_IN // CB
DU_COLS = ((0, 4096), (4096, 6656), (6656, N_IN))


def _du_granule(j):
    return jnp.clip(j, 0, 7), jnp.clip(j - 8, 0, 4), jnp.clip(j - 13, 0, 3)


def _du_select(j, refs, fn):
    for ref, lo, hi in zip(refs, (0, 8, 13), (8, 13, 17)):
        @pl.when((j >= lo) & (j < hi))
        def _():
            fn(ref)


def _in_proj_bwd(du, w_full, x, g, dout, rider):
    t = du[0].shape[0]
    tn = min(256, t)

    def body(a0, a1, a2, w_hbm, x_ref, g_ref, do_ref, dx_ref, dg_ref, w_ref, sem):
        @pl.when(pl.program_id(0) == 0)
        def _():
            cp = pltpu.make_async_copy(w_hbm, w_ref, sem)
            cp.start()
            dg_ref[...] = jnp.zeros_like(dg_ref)
            cp.wait()
        acc = None
        for a_ref, (lo, hi) in zip((a0, a1, a2), DU_COLS):
            part = _dot_nt(w_ref[:, lo:hi], a_ref[...])
            acc = part if acc is None else acc + part
        dh = acc.T
        xv = x_ref[...]
        r = lax.rsqrt(jnp.mean(xv * xv, axis=-1, keepdims=True) + EPS)
        xhat = xv * r
        dg_ref[...] += _fold8(dh * xhat)
        dxh = dh * g_ref[...]
        dx_ref[...] = do_ref[...] + r * (dxh - xhat * jnp.mean(dxh * xhat, axis=-1, keepdims=True))

    row = pl.BlockSpec((tn, D), lambda i: (i, 0))
    return _call(
        body, rider, name="in_proj_bwd", grid=(t // tn,),
        in_specs=[pl.BlockSpec((tn, hi - lo), lambda i: (i, 0)) for lo, hi in DU_COLS]
        + [ANY, row, pl.BlockSpec((1, D), lambda i: (0, 0)), row],
        out_specs=[row, pl.BlockSpec((8, D), lambda i: (0, 0))], out_shape=[SDS((t, D), F32), SDS((8, D), F32)],
        scratch_shapes=[pltpu.VMEM((D, N_IN), BF16), pltpu.SemaphoreType.DMA(())],
        args=(*du, w_full, x, g, dout), vmem=VMEM_BIG)


def _in_proj_wgrad(ht, du):
    t = ht.shape[1]
    tk = min(4096, t)
    n_k = t // tk

    def body(h_ref, b0, b1, b2, g_ref):
        j, k = pl.program_id(0), pl.program_id(1)

        if n_k > 1:
            @pl.when(k == 0)
            def _():
                g_ref[...] = jnp.zeros_like(g_ref)

        def add(b_ref):
            if n_k > 1:
                g_ref[...] += _dot(h_ref[...], b_ref[...])
            else:
                g_ref[...] = _dot(h_ref[...], b_ref[...])
        _du_select(j, (b0, b1, b2), add)

    seg = lambda q: pl.BlockSpec((tk, CB), lambda j, k: (k, _du_granule(j)[q]))
    return pl.pallas_call(
        body, name="in_proj_wgrad", grid=(N_GRAN, t // tk),
        in_specs=[pl.BlockSpec((D, tk), lambda j, k: (0, k)), seg(0), seg(1), seg(2)],
        out_specs=pl.BlockSpec((D, CB), lambda j, k: (0, j)), out_shape=SDS((D, N_IN), F32),
        compiler_params=_cp("parallel", "arbitrary", vmem=VMEM_BIG),
    )(ht, *du)


def _swap_rider(g_in, g_sm):
    def copies(ins, outs, send, recv, base=0):
        x, y, c = _mesh_pos()
        cps = []
        for src, dst in zip(ins, outs):
            half = src.at[1 - c] if len(src.shape) == 3 else src.at[:, :, 1 - c]
            cps.append(_rcopy(half, dst, send, recv, base + len(cps), (x, y, 1 - c)))
        return cps

    arrays = [g for g in (g_in, g_sm) if g is not None]
    shapes = [SDS((512, N_IN), F32) if len(g.shape) == 3 else SDS((3, 4, 128, D), F32) for g in arrays]
    return _Rider(arrays, shapes, len(arrays), copies)


def _add_halves_in(cc_idx, g_in, r_in):
    def body(cc_ref, a_ref, b_ref, f_ref, h_ref):
        s = a_ref[...] + b_ref[...]
        h_ref[...] = s.astype(BF16)

        @pl.when(pl.program_id(1) == cc_ref[1])
        def _():
            f_ref[...] = s

    blk = pl.BlockSpec((128, SH_IN), lambda i, j, cc: (i, j))
    return pl.pallas_call(
        body, name="add_halves_in",
        grid_spec=pltpu.PrefetchScalarGridSpec(
            num_scalar_prefetch=1, grid=(4, 4),
            in_specs=[pl.BlockSpec((None, 128, SH_IN), lambda i, j, cc: (cc[0], i, j)), blk],
            out_specs=[pl.BlockSpec((128, SH_IN), lambda i, j, cc: (i, 0)), blk]),
        out_shape=[SDS((512, SH_IN), F32), SDS((512, N_IN), BF16)], compiler_params=_cp("arbitrary", "arbitrary"),
    )(cc_idx, g_in, r_in)


def _add_halves_sm(c_idx, g_sm, r_sm):
    def body(c_ref, a_ref, b_ref, f_ref, h_ref):
        s = a_ref[...] + b_ref[...]
        f_ref[...] = s
        h_ref[...] = s.astype(BF16)

    blk = pl.BlockSpec((1, 4, 128, D), lambda a, c: (a, 0, 0, 0))
    return pl.pallas_call(
        body, name="add_halves_sm",
        grid_spec=pltpu.PrefetchScalarGridSpec(
            num_scalar_prefetch=1, grid=(3,),
            in_specs=[pl.BlockSpec((1, 4, None, 128, D), lambda a, c: (a, 0, c[0], 0, 0)), blk], out_specs=[blk, blk]),
        out_shape=[SDS((3, 4, 128, D), F32), SDS((3, 4, 128, D), BF16)], compiler_params=_cp("parallel"),
    )(c_idx, g_sm, r_sm)


def _scatter_rider(h_in, h_sm):
    def copies(ins, outs, send, recv, base=0):
        x, y, c = _mesh_pos()
        cps = []
        for src, dst in zip(ins, outs):
            for k, chip in enumerate(_other_chips(x, y)):
                their = 2 * chip[0] + chip[1]
                part = src.at[:, pl.ds(pl.multiple_of(their * SH_IN, 128), SH_IN)] if len(src.shape) == 2 else src.at[:, their]
                cps.append(_rcopy(part, dst.at[k], send, recv, base + len(cps), (*chip, c)))
        return cps

    arrays = [h for h in (h_in, h_sm) if h is not None]
    shapes = [SDS((3, 512, SH_IN), BF16) if len(h.shape) == 2 else SDS((3, 3, 128, D), BF16) for h in arrays]
    return _Rider(arrays, shapes, 3 * len(arrays), copies)


def _ride_alone(rider, name):
    return _hosted_call(None, rider, name=name, grid=(), in_specs=[], out_specs=[], out_shape=[], args=())[1]


def _final_sum_in(f_in, r_in):
    def body(a_ref, r_ref, o_ref):
        o_ref[...] = a_ref[...] + r_ref[0].astype(F32) + r_ref[1].astype(F32) + r_ref[2].astype(F32)

    return pl.pallas_call(
        body, name="final_sum_in", grid=(4,),
        in_specs=[pl.BlockSpec((128, SH_IN), lambda i: (i, 0)), pl.BlockSpec((3, 128, SH_IN), lambda i: (0, i, 0))],
        out_specs=pl.BlockSpec((128, SH_IN), lambda i: (i, 0)),
        out_shape=SDS((512, SH_IN), F32), compiler_params=_cp("parallel"),
    )(f_in, r_in)


def _final_sum_sm(chip_idx, f_sm, r_sm):
    def body(j_ref, a_ref, r_ref, o_ref):
        o_ref[...] = a_ref[...] + r_ref[0].astype(F32) + r_ref[1].astype(F32) + r_ref[2].astype(F32)

    return pl.pallas_call(
        body, name="final_sum_sm",
        grid_spec=pltpu.PrefetchScalarGridSpec(
            num_scalar_prefetch=1, grid=(3,),
            in_specs=[pl.BlockSpec((1, None, 128, D), lambda a, j: (a, j[0], 0, 0)),
                      pl.BlockSpec((3, 1, 128, D), lambda a, j: (0, a, 0, 0))],
            out_specs=pl.BlockSpec((1, 128, D), lambda a, j: (a, 0, 0))),
        out_shape=SDS((3, 128, D), F32), compiler_params=_cp("parallel"),
    )(chip_idx, f_sm, r_sm)


def _join_halves(t_in, t_sm):
    n_cp = N_LAYERS * 4

    def body(*refs):
        ins, outs = refs[:2 * N_LAYERS], refs[2 * N_LAYERS:2 * N_LAYERS + 4]
        send, recv, loc_in, loc_out, stage_in, stage_sm = refs[2 * N_LAYERS + 4:]
        x, y, c = _mesh_pos()
        cps, own = [], []
        for l in range(N_LAYERS):
            for a in range(4):
                s = 4 * l + a
                if a == 0:
                    src = ins[2 * l]
                    dst = outs[0].at[l, pl.ds(pl.multiple_of(c * 512, 512), 512), :]
                else:
                    src = ins[2 * l + 1].at[a - 1]
                    dst = outs[a].at[l, pl.ds(pl.multiple_of(c * 128, 128), 128), :]
                own.append((src, dst, min(a, 1)))
                cp = pltpu.make_async_remote_copy(src_ref=src, dst_ref=dst, send_sem=send.at[s], recv_sem=recv.at[s],
                                                  device_id=(x, y, 1 - c), device_id_type=MESH)
                cp.start()
                cps.append(cp)
        _staged_copies(own, (stage_in, stage_sm), loc_in, loc_out)
        for l in range(N_LAYERS):
            for a in range(4):
                s = 4 * l + a
                if a == 0:
                    got = outs[0].at[l, pl.ds(pl.multiple_of((1 - c) * 512, 512), 512), :]
                else:
                    got = outs[a].at[l, pl.ds(pl.multiple_of((1 - c) * 128, 128), 128), :]
                pltpu.make_async_remote_copy(src_ref=got, dst_ref=got, send_sem=send.at[s], recv_sem=recv.at[s],
                                             device_id=(x, y, 1 - c), device_id_type=MESH).wait_recv()
        for cp in cps:
            cp.wait_send()

    args = []
    for l in range(N_LAYERS):
        args += [t_in[l], t_sm[l]]
    sm = SDS((N_LAYERS, SH_ROW, D), F32)
    return pl.pallas_call(
        body, name="join_halves", in_specs=[ANY] * (2 * N_LAYERS), out_specs=[ANY] * 4,
        out_shape=[SDS((N_LAYERS, D, SH_IN), F32), sm, sm, sm],
        scratch_shapes=[pltpu.SemaphoreType.DMA((n_cp,))] * 4
        + [pltpu.VMEM((2, 512, SH_IN), F32), pltpu.VMEM((2, 128, D), F32)],
        compiler_params=_cp(vmem=VMEM_BIG),
    )(*args)


def _adam_math(w, g, m, v):
    m = ADAM_B1 * m + (1.0 - ADAM_B1) * g
    v = ADAM_B2 * v + (1.0 - ADAM_B2) * (g * g)
    m_hat = m / (1.0 - ADAM_B1 ** ADAM_STEP)
    v_hat = v / (1.0 - ADAM_B2 ** ADAM_STEP)
    delta = -ADAM_LR * (m_hat / (jnp.sqrt(v_hat) + ADAM_EPS) + ADAM_WD * w)
    return delta, m, v


def _adamw_big(w, g, m, v, name):
    rows, cols = w.shape
    tr = 128

    def body(w_ref, g_ref, m_ref, v_ref, go_ref, d_ref, nm_ref, nv_ref):
        g = g_ref[...]
        go_ref[...] = g
        d_ref[...], nm_ref[...], nv_ref[...] = _adam_math(w_ref[...], g, m_ref[...], v_ref[...])

    blk = pl.BlockSpec((tr, cols), lambda i: (i, 0))
    return pl.pallas_call(
        body, name=name, grid=(rows // tr,), in_specs=[blk] * 4, out_specs=[blk] * 4,
        out_shape=[SDS((rows, cols), F32)] * 4, compiler_params=_cp("parallel", vmem=VMEM_BIG),
    )(w, g, m, v)


def _adamw_small(ws, gs, ms, vs):
    n = len(ws)

    def body(*refs):
        for k in range(n):
            w_ref, g_ref, m_ref, v_ref = (refs[q * n + k] for q in range(4))
            d, nm, nv = _adam_math(w_ref[...], g_ref[...], m_ref[...], v_ref[...])
            refs[4 * n + k][...] = d
            refs[5 * n + k][...] = nm
            refs[6 * n + k][...] = nv

    vm = pl.BlockSpec(memory_space=pltpu.VMEM)
    shapes = [SDS(w.shape, F32) for w in ws]
    res = pl.pallas_call(
        body, name="adamw_small", in_specs=[vm] * (4 * n), out_specs=[vm] * (3 * n), out_shape=shapes * 3,
    )(*ws, *gs, *ms, *vs)
    return res[:n], res[n:2 * n], res[2 * n:]


def _pad_rows(a, rows):
    flat = a.reshape(-1)
    return jnp.pad(flat, (0, rows * 128 - flat.shape[0])).reshape(rows, 128)


def kernel(x, norm_g, w_in, conv_w, q_norm_g, k_norm_g, sinks, w_conv_out, w_attn_out, gate_b, w_out, loss_target, m_norm_g, m_w_in, m_conv_w, m_q_norm_g, m_k_norm_g, m_sinks, m_w_conv_out, m_w_attn_out, m_gate_b, m_w_out, v_norm_g, v_w_in, v_conv_w, v_q_norm_g, v_k_norm_g, v_sinks, v_w_conv_out, v_w_attn_out, v_gate_b, v_w_out):
    xi, yi, ci = _mesh_pos()
    chip = 2 * xi + yi
    c_idx = jnp.reshape(ci, (1,)).astype(jnp.int32)
    chip_idx = jnp.reshape(chip, (1,)).astype(jnp.int32)
    cc_idx = jnp.stack([ci, chip]).astype(jnp.int32)
    t = x.shape[1]
    xs = [x.reshape(t, D)]
    tgt = loss_target.reshape(t, D)

    full_w = [[_cast_w_in(chip_idx, w_in, l), _cast_w_small(chip_idx, w_conv_out, w_attn_out, w_out, l)]
              for l in range(N_LAYERS)]
    full_w[0][0] = _ride_alone(_gather_rider(full_w[0][:1], "A"), "gather_first_ici")[0]
    full_w[0][0] = _ride_alone(_gather_rider(full_w[0][:1], "B"), "gather_first_d2d")[0]
    placed = lax.dynamic_update_slice(jnp.zeros((N_LAYERS, 3, D), F32),
                                      jnp.where(ci == 0, conv_w, 0.0), (0, 0, chip * SH_ROW))
    conv_full = _allreduce_small(placed.reshape(96, 128)).reshape(N_LAYERS, 3, D)

    qg_s = jnp.tile(q_norm_g, (1, N_Q)) * SCALE
    kg_t = jnp.tile(k_norm_g, (1, N_KV))
    bias = _band_bias()
    saved = []
    for l in range(N_LAYERS):
        nxt = full_w[l + 1] if l + 1 < N_LAYERS else None
        h, ht = _rmsnorm_fwd(xs[l], norm_g[l:l + 1])
        (u_conv, u_qkv, u_za, u_gl), got = _in_proj(h, full_w[l][0], _gather_rider(nxt[:1], "A") if nxt else None)
        if nxt:
            nxt[0] = got[0]
        y_c = _conv_fwd(u_conv, conv_full[l])
        (qs, kvx), got = _qkv_prep(u_qkv, qg_s[l:l + 1], kg_t[l:l + 1],
                                   _gather_rider(full_w[0][1:], "A") if l == 0 else None)
        if l == 0:
            full_w[0][1] = _ride_alone(_gather_rider(got, "B"), "gather_first_small_d2d")[0]
        o, lse, got = _attn_fwd(qs, kvx, u_za, sinks[l:l + 1], bias, _merge_riders(
            _gather_rider(nxt[1:], "A"), _gather_rider(nxt[:1], "B")) if nxt else None)
        if nxt:
            nxt[1], nxt[0] = got
        (x_next, y_a, y_b, merged), got = _out_proj_fwd(xs[l], y_c, o, u_gl, gate_b[l:l + 1], full_w[l][1],
                                                        _gather_rider(nxt[1:], "B") if nxt else None)
        if nxt:
            nxt[1] = got[0]
        xs.append(x_next)
        saved.append((ht, u_conv, u_qkv, u_za, u_gl, y_c, o, y_a, y_b, merged, qs, kvx, lse))

    dout, sq = _loss_head(xs[N_LAYERS], tgt)

    small, t_in, t_sm = [None] * N_LAYERS, [None] * N_LAYERS, [None] * N_LAYERS
    halves = None

    for l in reversed(range(N_LAYERS)):
        w_full, w_sm = full_w[l]
        last = l == 0
        ht, u_conv, u_qkv, u_za, u_gl, y_c, o, y_a, y_b, merged, qs, kvx, lse = saved[l]
        (d_ya, d_yb, du_gl, d_yc, d_o, dgb), _ = _out_proj_bwd(dout, y_a, y_b, u_gl, gate_b[l:l + 1], w_sm, None)
        g_sm = _small_wgrads(y_c, d_ya, o, d_yb, merged, dout).reshape(3, 4, 2, 128, D)
        (du_conv, dcw), got = _conv_bwd(d_yc, u_conv, conv_full[l], _merge_riders(
            _scatter_rider(None, halves[3]) if halves else None, _swap_rider(None, g_sm) if last else None))
        if halves:
            t_sm[l + 1] = _final_sum_sm(chip_idx, halves[2], got[0])
        if last:
            f_sm0, h_sm0 = _add_halves_sm(c_idx, g_sm, got[-1])
        (dqs, dkv, dza, dsk), got = _attn_bwd(d_o, qs, kvx, u_za, lse, sinks[l:l + 1], bias,
                                              _scatter_rider(halves[1], None) if halves else None)
        if halves:
            t_in[l + 1] = _final_sum_in(halves[0], got[0])
        dsk = jnp.sum(dsk[0].reshape(N_KV, 2, 2, BLK), axis=-1).transpose(0, 2, 1).reshape(N_Q)
        (du_attn, dqg, dkg), got = _qkv_post(u_qkv, dqs, dkv, dza, qg_s[l:l + 1], kg_t[l:l + 1],
                                             _scatter_rider(None, h_sm0) if last else None)
        if last:
            t_sm[0] = _final_sum_sm(chip_idx, f_sm0, got[0])
        du = (du_conv, du_attn, du_gl)
        g_in = _in_proj_wgrad(ht, du).reshape(2, 512, N_IN)
        if last:
            f_in0, h_in0 = _add_halves_in(cc_idx, g_in, _ride_alone(_swap_rider(g_in, None), "swap_last")[0])
        (dout, dng), got = _in_proj_bwd(du, w_full, xs[l], norm_g[l:l + 1], dout,
                                        _scatter_rider(h_in0, None) if last else _swap_rider(g_in, g_sm))
        if last:
            t_in[0] = _final_sum_in(f_in0, got[0])
        else:
            halves = _add_halves_in(cc_idx, g_in, got[0]) + _add_halves_sm(c_idx, g_sm, got[1])
        small[l] = (jnp.sum(dng, axis=0), SCALE * jnp.sum(dqg.reshape(8 * N_Q, HEAD), axis=0),
                    jnp.sum(dkg.reshape(8 * N_KV, HEAD), axis=0), dsk, jnp.sum(dgb, axis=0), dcw[:3])
    grad_x = dout.reshape(1, t, D)

    stack = lambda k: jnp.stack([small[l][k] for l in range(N_LAYERS)])
    pack = jnp.concatenate([_pad_rows(stack(0), 32), _pad_rows(stack(1), 8), _pad_rows(stack(2), 8),
                            _pad_rows(stack(3), 8), _pad_rows(stack(4), 64), _pad_rows(stack(5), 96),
                            _pad_rows(jnp.sum(sq) * (0.5 / D), 8)], axis=0)
    red = _allreduce_small(pack)
    loss = red[216, 0]
    g_norm_g = red[0:32].reshape(N_LAYERS, D)
    g_q_norm_g = red[32:40].reshape(-1)[:N_LAYERS * HEAD].reshape(N_LAYERS, HEAD)
    g_k_norm_g = red[40:48].reshape(-1)[:N_LAYERS * HEAD].reshape(N_LAYERS, HEAD)
    g_sinks = red[48:56].reshape(-1)[:N_LAYERS * N_Q].reshape(N_LAYERS, N_Q)
    g_gate_b = red[56:120].reshape(N_LAYERS, 2 * D)
    g_conv_full = red[120:216].reshape(N_LAYERS, 3, D)
    g_conv_w = lax.dynamic_slice(g_conv_full, (0, 0, chip * SH_ROW), (N_LAYERS, 3, SH_ROW))

    g_w_in, g_w_co, g_w_ao, g_w_out = _join_halves(t_in, t_sm)

    r_in = N_LAYERS * D
    g_w_in, d_in, nm_in, nv_in = (a.reshape(N_LAYERS, D, SH_IN) for a in _adamw_big(
        w_in.reshape(r_in, SH_IN), g_w_in.reshape(r_in, SH_IN), m_w_in.reshape(r_in, SH_IN),
        v_w_in.reshape(r_in, SH_IN), "adamw_w_in"))
    r_sm = N_LAYERS * SH_ROW
    big = {}
    for nm, w, g, m, v in (("co", w_conv_out, g_w_co, m_w_conv_out, v_w_conv_out),
                           ("ao", w_attn_out, g_w_ao, m_w_attn_out, v_w_attn_out),
                           ("out", w_out, g_w_out, m_w_out, v_w_out)):
        big[nm] = tuple(a.reshape(N_LAYERS, SH_ROW, D) for a in _adamw_big(
            w.reshape(r_sm, D), g.reshape(r_sm, D), m.reshape(r_sm, D), v.reshape(r_sm, D), "adamw_w_small"))
    g_w_co, g_w_ao, g_w_out = big["co"][0], big["ao"][0], big["out"][0]
    sm_w = [norm_g, conv_w, q_norm_g, k_norm_g, sinks, gate_b]
    sm_g = [g_norm_g, g_conv_w, g_q_norm_g, g_k_norm_g, g_sinks, g_gate_b]
    sm_m = [m_norm_g, m_conv_w, m_q_norm_g, m_k_norm_g, m_sinks, m_gate_b]
    sm_v = [v_norm_g, v_conv_w, v_q_norm_g, v_k_norm_g, v_sinks, v_gate_b]
    sd, snm, snv = _adamw_small(sm_w, sm_g, sm_m, sm_v)

    def order(norm, w_in_, conv, qn, kn, sk, co, ao, gb, wo):
        return [norm, w_in_, conv, qn, kn, sk, co, ao, gb, wo]

    grads = order(g_norm_g, g_w_in, g_conv_w, g_q_norm_g, g_k_norm_g, g_sinks, g_w_co, g_w_ao, g_gate_b, g_w_out)
    deltas = order(sd[0], d_in, sd[1], sd[2], sd[3], sd[4], big["co"][1], big["ao"][1], sd[5], big["out"][1])
    new_m = order(snm[0], nm_in, snm[1], snm[2], snm[3], snm[4], big["co"][2], big["ao"][2], snm[5], big["out"][2])
    new_v = order(snv[0], nv_in, snv[1], snv[2], snv[3], snv[4], big["co"][3], big["ao"][3], snv[5], big["out"][3])
    return (loss, grad_x, *grads, *deltas, *new_m, *new_v)
```

```python
import functools

import jax
import jax.numpy as jnp
from jax import lax
from jax.experimental import pallas as pl
from jax.experimental.pallas import tpu as pltpu

F32, BF16 = jnp.float32, jnp.bfloat16
SDS = jax.ShapeDtypeStruct
MESH = pl.DeviceIdType.MESH
ANY = pl.BlockSpec(memory_space=pl.ANY)

D = 1024
N_IN = 8704
N_LAYERS = 4
N_Q, N_KV, HEAD = 16, 4, 64
GROUP = N_Q // N_KV
BLK = 128
EPS = 1e-6
NEG = -1e30
SCALE = HEAD ** -0.5
SH_IN = N_IN // 4
SH_ROW = D // 4
CB = 512
SEG_CONV, SEG_Q, SEG_KV, SEG_ZA, SEG_GL = (0, 8), (8, 2), (10, 1), (11, 2), (13, 4)
VMEM_BIG = 56 * 1024 * 1024

ADAM_LR, ADAM_B1, ADAM_B2, ADAM_EPS, ADAM_WD, ADAM_STEP = 0.001, 0.9, 0.999, 1e-08, 0.01, 10


def _cp(*sem, vmem=None):
    return pltpu.CompilerParams(dimension_semantics=sem if sem else None, vmem_limit_bytes=vmem)


def _sigmoid(z):
    return 1.0 / (1.0 + jnp.exp(-z))


def _dot(a, b):
    return jnp.dot(a, b, preferred_element_type=F32)


def _dot_nt(a, b):
    return lax.dot_general(a, b, (((1,), (1,)), ((), ())), preferred_element_type=F32)


def _dot_tn(a, b):
    return lax.dot_general(a, b, (((0,), (0,)), ((), ())), preferred_element_type=F32)


def _rms(xh):
    r = lax.rsqrt(jnp.mean(xh * xh, axis=-1, keepdims=True) + EPS)
    return xh * r, r


def _fold8(v):
    return jnp.sum(v.reshape(v.shape[0] // 8, 8, v.shape[1]), axis=0)


def _cast_w_in(chip_idx, w, layer):
    def body(j_ref, i_ref, o_ref):
        o_ref[...] = i_ref[...].astype(BF16)

    return pl.pallas_call(
        body, name="cast_w_in",
        grid_spec=pltpu.PrefetchScalarGridSpec(
            num_scalar_prefetch=1, grid=(2,),
            in_specs=[pl.BlockSpec((None, 512, SH_IN), lambda i, j: (layer, i, 0))],
            out_specs=pl.BlockSpec((512, SH_IN), lambda i, j: (i, j[0]))),
        out_shape=SDS((D, N_IN), BF16), compiler_params=_cp("parallel"),
    )(chip_idx, w)


def _cast_w_small(chip_idx, a, b, c, layer):
    def body(j_ref, a_ref, b_ref, c_ref, o_ref):
        o_ref[0] = a_ref[...].astype(BF16)
        o_ref[1] = b_ref[...].astype(BF16)
        o_ref[2] = c_ref[...].astype(BF16)

    spec = pl.BlockSpec((None, SH_ROW, D), lambda i, j: (layer, 0, 0))
    return pl.pallas_call(
        body, name="cast_w_small",
        grid_spec=pltpu.PrefetchScalarGridSpec(
            num_scalar_prefetch=1, grid=(1,), in_specs=[spec, spec, spec],
            out_specs=pl.BlockSpec((3, SH_ROW, D), lambda i, j: (0, j[0], 0))),
        out_shape=SDS((3, D, D), BF16), compiler_params=_cp("parallel"),
    )(chip_idx, a, b, c)


def _mesh_pos():
    return lax.axis_index("x"), lax.axis_index("y"), lax.axis_index("c")


def _other_chips(x, y):
    return [(1 - x, y), (x, 1 - y), (1 - x, 1 - y)]


class _Rider:
    def __init__(self, ins, out_shape, n, copies, aliases=()):
        self.ins, self.out_shape, self.n, self.copies, self.aliases = list(ins), list(out_shape), n, copies, aliases


def _merge_riders(*riders):
    riders = [r for r in riders if r is not None]
    if len(riders) < 2:
        return riders[0] if riders else None

    def copies(ins, outs, send, recv, base=0):
        cps, i0, o0 = [], 0, 0
        for r in riders:
            cps += r.copies(ins[i0:i0 + len(r.ins)], outs[o0:o0 + len(r.out_shape)], send, recv, base + len(cps))
            i0, o0 = i0 + len(r.ins), o0 + len(r.out_shape)
        return cps

    aliases, i0, o0 = [], 0, 0
    for r in riders:
        aliases += [(i0 + i, o0 + o) for i, o in r.aliases]
        i0, o0 = i0 + len(r.ins), o0 + len(r.out_shape)
    return _Rider(sum((r.ins for r in riders), []), sum((r.out_shape for r in riders), []),
                  sum(r.n for r in riders), copies, tuple(aliases))


def _rcopy(src, dst, send, recv, k, to):
    return pltpu.make_async_remote_copy(src_ref=src, dst_ref=dst, send_sem=send.at[k], recv_sem=recv.at[k],
                                        device_id=to, device_id_type=MESH)


def _hosted_call(body, rider, *, name, grid, in_specs, out_specs, out_shape, args, scratch_shapes=(), vmem=None):
    n_in, n_out, n_scr = len(in_specs), len(out_specs), len(scratch_shapes)
    r_in, r_out = len(rider.ins), len(rider.out_shape)

    def full_body(*refs):
        host_in, rid_in = refs[:n_in], refs[n_in:n_in + r_in]
        o0 = n_in + r_in
        host_out, rid_out = refs[o0:o0 + n_out], refs[o0 + n_out:o0 + n_out + r_out]
        s0 = o0 + n_out + r_out
        host_scr, (send, recv) = refs[s0:s0 + n_scr], refs[s0 + n_scr:]
        if body is None:
            cps = rider.copies(rid_in, rid_out, send, recv)
            for cp in cps:
                cp.start()
            for cp in cps:
                cp.wait()
            return
        ids = [pl.program_id(a) for a in range(len(grid))]
        first = functools.reduce(lambda p, q: p & q, [i == 0 for i in ids])
        last = functools.reduce(lambda p, q: p & q, [i == g - 1 for i, g in zip(ids, grid)])

        @pl.when(first)
        def _():
            for cp in rider.copies(rid_in, rid_out, send, recv):
                cp.start()

        body(*host_in, *host_out, *host_scr)

        @pl.when(last)
        def _():
            for cp in rider.copies(rid_in, rid_out, send, recv):
                cp.wait()

    res = pl.pallas_call(
        full_body, name=name, grid=grid if body is not None else (),
        in_specs=list(in_specs) + [ANY] * r_in, out_specs=list(out_specs) + [ANY] * r_out,
        out_shape=list(out_shape) + rider.out_shape,
        scratch_shapes=list(scratch_shapes) + [pltpu.SemaphoreType.DMA((rider.n,))] * 2,
        input_output_aliases={n_in + i: n_out + o for i, o in rider.aliases},
        compiler_params=_cp(*(("arbitrary",) * len(grid) if body is not None else ()), vmem=vmem),
    )(*args, *rider.ins)
    return res[:n_out], res[n_out:]


def _call(body, rider, **kw):
    if rider is not None:
        return _hosted_call(body, rider, **kw)
    res = pl.pallas_call(
        body, name=kw["name"], grid=kw["grid"], in_specs=list(kw["in_specs"]), out_specs=list(kw["out_specs"]),
        out_shape=list(kw["out_shape"]), scratch_shapes=list(kw.get("scratch_shapes", ())),
        compiler_params=_cp(*(("arbitrary",) * len(kw["grid"])), vmem=kw.get("vmem")),
    )(*kw["args"])
    return res, []


def _gather_rider(arrays, stage):
    def copies(ins, outs, send, recv, base=0):
        x, y, c = _mesh_pos()
        cps = []
        for full in outs:
            for k, chip in enumerate(_other_chips(x, y)):
                whose = 2 * x + y if stage == "A" else 2 * chip[0] + chip[1]
                if len(full.shape) == 2:
                    reg = full.at[pl.ds(pl.multiple_of(c * 512, 512), 512), pl.ds(pl.multiple_of(whose * SH_IN, 128), SH_IN)]
                else:
                    reg = full.at[:, pl.ds(pl.multiple_of(whose * SH_ROW + c * 128, 128), 128), :]
                to = (*chip, c) if stage == "A" else (x, y, 1 - c)
                cps.append(_rcopy(reg, reg, send, recv, base + len(cps), to))
        return cps

    return _Rider(arrays, [SDS(v.shape, v.dtype) for v in arrays], 3 * len(arrays), copies,
                  aliases=tuple((i, i) for i in range(len(arrays))))


def _staged_copies(copies, stages, sem_in, sem_out):
    busy, count = {}, {}
    for idx, (src, dst, kind) in enumerate(copies):
        slot = count.get(kind, 0) % 2
        count[kind] = count.get(kind, 0) + 1
        if (kind, slot) in busy:
            busy.pop((kind, slot)).wait()
        buf = stages[kind].at[slot]
        cin = pltpu.make_async_copy(src, buf, sem_in.at[idx])
        cin.start()
        cin.wait()
        cout = pltpu.make_async_copy(buf, dst, sem_out.at[idx])
        cout.start()
        busy[(kind, slot)] = cout
    for cp in busy.values():
        cp.wait()


def _allreduce_small(pack):
    rows = pack.shape[0]

    def body(p_ref, o_ref, buf, send, recv):
        x, y, c = _mesh_pos()
        me = 4 * x + 2 * y + c
        sends = []
        for r in range(1, 8):
            to = (x if not (r & 4) else 1 - x, y if not (r & 2) else 1 - y, c if not (r & 1) else 1 - c)
            cp = pltpu.make_async_remote_copy(src_ref=p_ref, dst_ref=buf.at[me], send_sem=send.at[r - 1],
                                              recv_sem=recv.at[r - 1], device_id=to, device_id_type=MESH)
            cp.start()
            sends.append(cp)
        buf[me] = p_ref[...]
        for r in range(1, 8):
            frm = (4 * x + 2 * y + c) ^ r
            pltpu.make_async_remote_copy(src_ref=p_ref, dst_ref=buf.at[frm], send_sem=send.at[r - 1],
                                         recv_sem=recv.at[r - 1], device_id=(x, y, c), device_id_type=MESH).wait_recv()
        acc = buf[0]
        for d in range(1, 8):
            acc = acc + buf[d]
        o_ref[...] = acc
        for cp in sends:
            cp.wait_send()

    vm = pl.BlockSpec(memory_space=pltpu.VMEM)
    return pl.pallas_call(
        body, name="allreduce_small", in_specs=[vm], out_specs=vm, out_shape=SDS((rows, 128), F32),
        scratch_shapes=[pltpu.VMEM((8, rows, 128), F32), pltpu.SemaphoreType.DMA((7,)), pltpu.SemaphoreType.DMA((7,))],
    )(pack)


def _rmsnorm_fwd(x, g):
    t = x.shape[0]
    tm = min(512, t)

    def body(x_ref, g_ref, h_ref, ht_ref):
        xv = x_ref[...]
        r = lax.rsqrt(jnp.mean(xv * xv, axis=-1, keepdims=True) + EPS)
        h = xv * r * g_ref[...]
        h_ref[...] = h.astype(BF16)
        ht_ref[...] = h.T.astype(BF16)

    return pl.pallas_call(
        body, name="rmsnorm_fwd", grid=(t // tm,),
        in_specs=[pl.BlockSpec((tm, D), lambda i: (i, 0)), pl.BlockSpec((1, D), lambda i: (0, 0))],
        out_specs=[pl.BlockSpec((tm, D), lambda i: (i, 0)), pl.BlockSpec((D, tm), lambda i: (0, i))],
        out_shape=[SDS((t, D), BF16), SDS((D, t), BF16)],
        compiler_params=_cp("parallel", vmem=VMEM_BIG),
    )(x, g)


FWD_SEGS = ((0, 8), (8, 3), (11, 2), (13, 4))


def _in_proj(h, w_full, rider):
    t = h.shape[0]
    tm = min(2048, t)

    def body(a_ref, b_ref, *o_refs):
        j = pl.program_id(1)
        for o_ref, (off, nblk) in zip(o_refs, FWD_SEGS):
            @pl.when((j >= off) & (j < off + nblk))
            def _():
                o_ref[...] = _dot(a_ref[...], b_ref[...]).astype(BF16)

    def out(seg):
        off, nblk = seg
        return pl.BlockSpec((tm, CB), lambda i, j: (i, jnp.clip(j - off, 0, nblk - 1)))

    res, got = _call(
        body, rider, name="in_proj", grid=(t // tm, N_IN // CB),
        in_specs=[pl.BlockSpec((tm, D), lambda i, j: (i, 0)), pl.BlockSpec((D, CB), lambda i, j: (0, j))],
        out_specs=[out(s) for s in FWD_SEGS], out_shape=[SDS((t, s[1] * CB), BF16) for s in FWD_SEGS],
        args=(h, w_full), vmem=VMEM_BIG)
    return res, got


def _conv_fwd(u_conv, conv_w):
    t = u_conv.shape[0]
    tm = min(256, t)
    hb = tm // 16

    def body(v_ref, b_ref, c_ref, z_ref, hv_ref, hc_ref, w_ref, y_ref):
        i = pl.program_id(0)
        cv = c_ref[...].astype(F32) * v_ref[...].astype(F32)
        halo = hc_ref[...].astype(F32) * hv_ref[...].astype(F32)
        halo = jnp.where(i > 0, halo, 0.0)
        row = lax.broadcasted_iota(jnp.int32, (tm, 1), 0)
        s1 = jnp.where(row == 0, halo[15:16], pltpu.roll(cv, 1, 0))
        s2 = jnp.where(row == 0, halo[14:15], jnp.where(row == 1, halo[15:16], pltpu.roll(cv, 2, 0)))
        conv = w_ref[0:1, :] * s2 + w_ref[1:2, :] * s1 + w_ref[2:3, :] * cv
        z = z_ref[...].astype(F32)
        y_ref[...] = (b_ref[...].astype(F32) * conv * (z * _sigmoid(z))).astype(BF16)

    def col(k):
        return pl.BlockSpec((tm, D), lambda i: (i, k))

    def halo(k):
        return pl.BlockSpec((16, D), lambda i: (jnp.maximum(i * hb - 1, 0), k))

    return pl.pallas_call(
        body, name="conv_fwd", grid=(t // tm,),
        in_specs=[col(0), col(1), col(2), col(3), halo(0), halo(2), pl.BlockSpec((3, D), lambda i: (0, 0))],
        out_specs=pl.BlockSpec((tm, D), lambda i: (i, 0)), out_shape=SDS((t, D), BF16),
        compiler_params=_cp("parallel", vmem=VMEM_BIG),
    )(u_conv, u_conv, u_conv, u_conv, u_conv, u_conv, conv_w)


KVX = 4 * N_KV * 128


def _iota2(shape):
    return lax.broadcasted_iota(jnp.int32, shape, 0), lax.broadcasted_iota(jnp.int32, shape, 1)


def _head_sum(v):
    r, c = _iota2((128, 128))
    ones = ((r >> 6) == (c >> 6)).astype(BF16)
    hi = v.astype(BF16)
    lo = (v - hi.astype(F32)).astype(BF16)
    return jnp.concatenate([_dot(hi[:, g:g + 128], ones) + _dot(lo[:, g:g + 128], ones)
                            for g in range(0, v.shape[1], 128)], axis=1)


def _expand_mats():
    r, c = _iota2((N_KV * HEAD, N_KV * 128))
    base = ((r >> 6) << 7) + (r & 63)
    return (c == base).astype(BF16), (c == base + 64).astype(BF16)


def _fold_mat():
    r, c = _iota2((N_KV * 128, N_KV * HEAD))
    return (((r >> 7) == (c >> 6)) & ((r & 63) == (c & 63))).astype(BF16)


def _qkv_prep(u_qkv, qg_s, kg_t, rider):
    t = u_qkv.shape[0]
    tm = min(512, t)

    def body(u_ref, qg_ref, kg_ref, qs_ref, kvx_ref):
        q = u_ref[:, 0:D].astype(F32)
        rq = lax.rsqrt(_head_sum(q * q) * (1.0 / HEAD) + EPS)
        qs_ref[...] = (q * rq * qg_ref[...]).astype(BF16)
        k = u_ref[:, D:D + 256].astype(F32)
        rk = lax.rsqrt(_head_sum(k * k) * (1.0 / HEAD) + EPS)
        kn = (k * rk * kg_ref[...]).astype(BF16)
        v = u_ref[:, D + 256:D + 512]
        e_lo, e_hi = _expand_mats()
        kvx_ref[:, 0:512] = _dot(kn, e_lo).astype(BF16)
        kvx_ref[:, 512:1024] = _dot(kn, e_hi).astype(BF16)
        kvx_ref[:, 1024:1536] = _dot(v, e_lo).astype(BF16)
        kvx_ref[:, 1536:2048] = _dot(v, e_hi).astype(BF16)

    return _call(
        body, rider, name="qkv_prep", grid=(t // tm,),
        in_specs=[pl.BlockSpec((tm, 1536), lambda i: (i, 0)), pl.BlockSpec((1, D), lambda i: (0, 0)),
                  pl.BlockSpec((1, 256), lambda i: (0, 0))],
        out_specs=[pl.BlockSpec((tm, D), lambda i: (i, 0)), pl.BlockSpec((tm, KVX), lambda i: (i, 0))],
        out_shape=[SDS((t, D), BF16), SDS((t, KVX), BF16)], args=(u_qkv, qg_s, kg_t), vmem=VMEM_BIG)


def _band_bias():
    j, r = _iota2((2 * BLK, 2 * BLK))
    diff = (r & (BLK - 1)) - j + BLK
    band = (diff >= 0) & (diff < BLK)
    return jnp.stack([jnp.where(band & (j >= BLK), 0.0, NEG), jnp.where(band, 0.0, NEG)]).astype(F32)


def _pair_rows(ref_or_val, hk):
    return jnp.concatenate([ref_or_val[:, 256 * hk:256 * hk + 128], ref_or_val[:, 256 * hk + 128:256 * hk + 256]], axis=0)


def _sink_row(sink_ref, hk, half):
    return jnp.concatenate([jnp.full((1, BLK), sink_ref[0, GROUP * hk + half], F32),
                            jnp.full((1, BLK), sink_ref[0, GROUP * hk + 2 + half], F32)], axis=1)


def _kv_operands(kvb, hk, half):
    return (kvb[:, 512 * half + 128 * hk:512 * half + 128 * hk + 128],
            kvb[:, 1024 + 512 * half + 128 * hk:1024 + 512 * half + 128 * hk + 128])


def _attn_fwd(qs, kvx, u_za, sinks, bias, rider):
    t = qs.shape[0]
    nb = t // BLK

    def body(q_ref, kc_ref, kp_ref, za_ref, sink_ref, bias_ref, o_ref, lse_ref):
        kvb = jnp.concatenate([kp_ref[...], kc_ref[...]], axis=0)
        bias_v = bias_ref[...]
        key0 = lax.broadcasted_iota(jnp.int32, (2 * BLK, 1), 0) == 0
        ones = jnp.ones((2 * BLK, 128), BF16)
        cols = []
        for hk in range(N_KV):
            qpp = _pair_rows(q_ref, hk)
            opp = None
            for half in range(2):
                kx, vx = _kv_operands(kvb, hk, half)
                s = _dot_nt(kx, qpp) + bias_v
                sink = _sink_row(sink_ref, hk, half)
                m = jnp.maximum(jnp.max(s, axis=0, keepdims=True), sink)
                p = jnp.exp(s - m)
                es = jnp.exp(sink - m)
                lse_ref[0, 2 * hk + half:2 * hk + half + 1, :] = m + jnp.log(jnp.sum(p, axis=0, keepdims=True) + es)
                pe = jnp.where(key0, es, p).astype(BF16)
                rhs = jnp.concatenate([jnp.where(key0, jnp.zeros_like(vx), vx), ones], axis=1)
                nd = _dot_tn(pe, rhs)
                o = nd[:, :128] * (1.0 / nd[:, 128:])
                opp = o if opp is None else opp + o
            cols += [opp[:BLK], opp[BLK:]]
        za = za_ref[...].astype(F32)
        o_ref[...] = (jnp.concatenate(cols, axis=1) * (za * _sigmoid(za))).astype(BF16)

    prev = lambda n: jnp.maximum(n - 1, 0)
    (o, lse), got = _call(
        body, rider, name="attn_fwd", grid=(nb,),
        in_specs=[pl.BlockSpec((BLK, D), lambda n: (n, 0)),
                  pl.BlockSpec((BLK, KVX), lambda n: (n, 0)), pl.BlockSpec((BLK, KVX), lambda n: (prev(n), 0)),
                  pl.BlockSpec((BLK, D), lambda n: (n, 0)), pl.BlockSpec(memory_space=pltpu.SMEM),
                  pl.BlockSpec((None, 2 * BLK, 2 * BLK), lambda n: (jnp.minimum(n, 1), 0, 0))],
        out_specs=[pl.BlockSpec((BLK, D), lambda n: (n, 0)), pl.BlockSpec((1, 8, 2 * BLK), lambda n: (n, 0, 0))],
        out_shape=[SDS((t, D), BF16), SDS((nb, 8, 2 * BLK), F32)],
        args=(qs, kvx, kvx, u_za, sinks, bias), vmem=VMEM_BIG)
    return o, lse, got


def _out_proj_fwd(x, y_c, o, u_gl, gate_b, w_sm, rider):
    t = x.shape[0]
    tm = min(512, t)

    def body(x_ref, yc_ref, o_ref, gla_ref, glb_ref, gb_ref, wco_ref, wao_ref, wout_ref,
             xn_ref, ya_ref, yb_ref, mg_ref):
        ya = _dot(yc_ref[...], wco_ref[...])
        yb = _dot(o_ref[...], wao_ref[...])
        gb = gb_ref[...]
        ga_ = _sigmoid(gla_ref[...].astype(F32) + gb[:, :D])
        gb_ = _sigmoid(glb_ref[...].astype(F32) + gb[:, D:])
        merged = (ga_ * ya + gb_ * yb).astype(BF16)
        ya_ref[...] = ya.astype(BF16)
        yb_ref[...] = yb.astype(BF16)
        mg_ref[...] = merged
        xn_ref[...] = x_ref[...] + _dot(merged, wout_ref[...])

    row = pl.BlockSpec((tm, D), lambda i: (i, 0))
    wspec = lambda a: pl.BlockSpec((None, D, D), lambda i: (a, 0, 0))
    return _call(
        body, rider, name="out_proj_fwd", grid=(t // tm,),
        in_specs=[row, row, row, pl.BlockSpec((tm, D), lambda i: (i, 0)), pl.BlockSpec((tm, D), lambda i: (i, 1)),
                  pl.BlockSpec((1, 2 * D), lambda i: (0, 0)), wspec(0), wspec(1), wspec(2)],
        out_specs=[row, row, row, row],
        out_shape=[SDS((t, D), F32), SDS((t, D), BF16), SDS((t, D), BF16), SDS((t, D), BF16)],
        args=(x, y_c, o, u_gl, u_gl, gate_b, w_sm, w_sm, w_sm), vmem=VMEM_BIG)


def _loss_head(y, tgt):
    t = y.shape[0]
    tm = min(512, t)

    def body(y_ref, t_ref, dy_ref, acc_ref):
        @pl.when(pl.program_id(0) == 0)
        def _():
            acc_ref[...] = jnp.zeros_like(acc_ref)
        err = y_ref[...] - t_ref[...]
        dy_ref[...] = err * (1.0 / D)
        sq = _fold8(err * err)
        tot = sq[:, 0:128]
        for k in range(1, D // 128):
            tot = tot + sq[:, 128 * k:128 * (k + 1)]
        acc_ref[...] += tot

    row = pl.BlockSpec((tm, D), lambda i: (i, 0))
    return pl.pallas_call(
        body, name="loss_head", grid=(t // tm,), in_specs=[row, row],
        out_specs=[row, pl.BlockSpec((8, 128), lambda i: (0, 0))],
        out_shape=[SDS((t, D), F32), SDS((8, 128), F32)], compiler_params=_cp("arbitrary"),
    )(y, tgt)


def _out_proj_bwd(dout, y_a, y_b, u_gl, gate_b, w_sm, rider):
    t = dout.shape[0]
    tm = min(512, t)

    def body(do_ref, ya_ref, yb_ref, gla_ref, glb_ref, gb_ref, wco_ref, wao_ref, wout_ref,
             dya_ref, dyb_ref, dgl_ref, dyc_ref, dob_ref, dgb_ref):
        @pl.when(pl.program_id(0) == 0)
        def _():
            dgb_ref[...] = jnp.zeros_like(dgb_ref)
        dm = _dot_nt(do_ref[...].astype(BF16), wout_ref[...])
        gb = gb_ref[...]
        ga_ = _sigmoid(gla_ref[...].astype(F32) + gb[:, :D])
        gb_ = _sigmoid(glb_ref[...].astype(F32) + gb[:, D:])
        dya = (ga_ * dm).astype(BF16)
        dyb = (gb_ * dm).astype(BF16)
        dgla = ya_ref[...].astype(F32) * dm * (ga_ * (1.0 - ga_))
        dglb = yb_ref[...].astype(F32) * dm * (gb_ * (1.0 - gb_))
        dya_ref[...] = dya
        dyb_ref[...] = dyb
        dgl_ref[:, :D] = dgla.astype(BF16)
        dgl_ref[:, D:] = dglb.astype(BF16)
        dgb_ref[:, :D] += _fold8(dgla)
        dgb_ref[:, D:] += _fold8(dglb)
        dyc_ref[...] = _dot_nt(dya, wco_ref[...]).astype(BF16)
        dob_ref[...] = _dot_nt(dyb, wao_ref[...]).astype(BF16)

    row = pl.BlockSpec((tm, D), lambda i: (i, 0))
    wspec = lambda a: pl.BlockSpec((None, D, D), lambda i: (a, 0, 0))
    return _call(
        body, rider, name="out_proj_bwd", grid=(t // tm,),
        in_specs=[row, row, row, pl.BlockSpec((tm, D), lambda i: (i, 0)), pl.BlockSpec((tm, D), lambda i: (i, 1)),
                  pl.BlockSpec((1, 2 * D), lambda i: (0, 0)), wspec(0), wspec(1), wspec(2)],
        out_specs=[row, row, pl.BlockSpec((tm, 2 * D), lambda i: (i, 0)), row, row,
                   pl.BlockSpec((8, 2 * D), lambda i: (0, 0))],
        out_shape=[SDS((t, D), BF16), SDS((t, D), BF16), SDS((t, 2 * D), BF16), SDS((t, D), BF16), SDS((t, D), BF16),
                   SDS((8, 2 * D), F32)],
        args=(dout, y_a, y_b, u_gl, u_gl, gate_b, w_sm, w_sm, w_sm), vmem=VMEM_BIG)


def _small_wgrads(y_c, d_ya, o, d_yb, merged, dout):
    t = y_c.shape[0]
    tk = min(512, t)

    def body(yc_ref, dya_ref, o_ref, dyb_ref, mg_ref, do_ref, g_ref):
        @pl.when(pl.program_id(0) == 0)
        def _():
            g_ref[...] = jnp.zeros_like(g_ref)
        g_ref[0] += _dot_tn(yc_ref[...], dya_ref[...])
        g_ref[1] += _dot_tn(o_ref[...], dyb_ref[...])
        g_ref[2] += _dot_tn(mg_ref[...], do_ref[...].astype(BF16))

    row = pl.BlockSpec((tk, D), lambda k: (k, 0))
    return pl.pallas_call(
        body, name="small_wgrads", grid=(t // tk,), in_specs=[row] * 6,
        out_specs=pl.BlockSpec((3, D, D), lambda k: (0, 0, 0)), out_shape=SDS((3, D, D), F32),
        compiler_params=_cp("arbitrary", vmem=VMEM_BIG),
    )(y_c, d_ya, o, d_yb, merged, dout)


def _conv_bwd(d_yc, u_conv, conv_w, rider):
    t = d_yc.shape[0]
    tm = min(256, t)
    hb = tm // 16
    last_halo = t // 16 - 1
    n_steps = t // tm

    def body(dy_ref, v_ref, b_ref, c_ref, z_ref, hv_ref, hc_ref, ndy_ref, nb_ref, nz_ref, w_ref, du_ref, dw_ref):
        i = pl.program_id(0)

        @pl.when(i == 0)
        def _():
            dw_ref[...] = jnp.zeros_like(dw_ref)
        v, c = v_ref[...].astype(F32), c_ref[...].astype(F32)
        b, z = b_ref[...].astype(F32), z_ref[...].astype(F32)
        cv = c * v
        halo = jnp.where(i > 0, hc_ref[...].astype(F32) * hv_ref[...].astype(F32), 0.0)
        row = lax.broadcasted_iota(jnp.int32, (tm, 1), 0)
        s1 = jnp.where(row == 0, halo[15:16], pltpu.roll(cv, 1, 0))
        s2 = jnp.where(row == 0, halo[14:15], jnp.where(row == 1, halo[15:16], pltpu.roll(cv, 2, 0)))
        w0, w1, w2 = w_ref[0:1, :], w_ref[1:2, :], w_ref[2:3, :]
        conv = w0 * s2 + w1 * s1 + w2 * cv
        sig = _sigmoid(z)
        sz = z * sig
        dsz = sig * (1.0 + z * (1.0 - sig))
        dy = dy_ref[...].astype(F32)
        dconv = dy * b * sz
        nz = nz_ref[...].astype(F32)
        nxt = ndy_ref[...].astype(F32) * nb_ref[...].astype(F32) * (nz * _sigmoid(nz))
        nxt = jnp.where(i < n_steps - 1, nxt, 0.0)
        a1 = jnp.where(row == tm - 1, nxt[0:1], pltpu.roll(dconv, tm - 1, 0))
        a2 = jnp.where(row == tm - 2, nxt[0:1], jnp.where(row == tm - 1, nxt[1:2], pltpu.roll(dconv, tm - 2, 0)))
        dcv = w2 * dconv + w1 * a1 + w0 * a2
        du_ref[:, 0:D] = (dcv * c).astype(BF16)
        du_ref[:, D:2 * D] = (dy * conv * sz).astype(BF16)
        du_ref[:, 2 * D:3 * D] = (dcv * v).astype(BF16)
        du_ref[:, 3 * D:4 * D] = (dy * b * conv * dsz).astype(BF16)
        r8 = lax.broadcasted_iota(jnp.int32, (8, 1), 0)
        dw_ref[...] += jnp.where(r8 == 0, jnp.sum(dconv * s2, axis=0, keepdims=True),
                                 jnp.where(r8 == 1, jnp.sum(dconv * s1, axis=0, keepdims=True),
                                           jnp.where(r8 == 2, jnp.sum(dconv * cv, axis=0, keepdims=True), 0.0)))

    def col(k):
        return pl.BlockSpec((tm, D), lambda i: (i, k))

    def halo(k):
        return pl.BlockSpec((16, D), lambda i: (jnp.maximum(i * hb - 1, 0), k))

    def nxt(k):
        return pl.BlockSpec((16, D), lambda i: (jnp.minimum((i + 1) * hb, last_halo), k))

    return _call(
        body, rider, name="conv_bwd", grid=(t // tm,),
        in_specs=[col(0), col(0), col(1), col(2), col(3), halo(0), halo(2), nxt(0), nxt(1), nxt(3),
                  pl.BlockSpec((3, D), lambda i: (0, 0))],
        out_specs=[pl.BlockSpec((tm, 4 * D), lambda i: (i, 0)), pl.BlockSpec((8, D), lambda i: (0, 0))],
        out_shape=[SDS((t, 4 * D), BF16), SDS((8, D), F32)],
        args=(d_yc, u_conv, u_conv, u_conv, u_conv, u_conv, u_conv, d_yc, u_conv, u_conv, conv_w), vmem=VMEM_BIG)


def _attn_bwd(d_o, qs, kvx, u_za, lse, sinks, bias, rider):
    t = d_o.shape[0]
    nb = t // BLK

    def body(q_ref, kc_ref, kp_ref, za_ref, do_ref, lse_ref, sink_ref, bias_ref,
             dq_ref, dkv_ref, dza_ref, dsk_ref, carry_ref):
        n = pl.program_id(0)

        @pl.when(n == 0)
        def _():
            carry_ref[...] = jnp.zeros_like(carry_ref)
            dsk_ref[...] = jnp.zeros_like(dsk_ref)

        live = n < nb
        kvb = jnp.concatenate([kp_ref[...], kc_ref[...]], axis=0)
        bias_v = bias_ref[...]
        za = za_ref[...].astype(F32)
        sig = _sigmoid(za)
        dsa = sig * (1.0 + za * (1.0 - sig))
        do = jnp.where(live, do_ref[...].astype(F32), 0.0)
        dattn = (do * (za * sig)).astype(BF16)
        lo_lanes = lax.broadcasted_iota(jnp.int32, (1, 128), 1) < HEAD
        dq_cols, attn_cols, dk_cols, dv_cols, dsk_rows = [], [], [], [], []
        for hk in range(N_KV):
            qpp = _pair_rows(q_ref, hk)
            dapp = _pair_rows(dattn, hk)
            probs, dss, xk, xv = [], [], [], []
            for half in range(2):
                kx, vx = _kv_operands(kvb, hk, half)
                lse = lse_ref[0, 2 * hk + half:2 * hk + half + 1, :]
                prob = jnp.exp(_dot_nt(kx, qpp) + bias_v - lse)
                psink = jnp.exp(_sink_row(sink_ref, hk, half) - lse)
                tdp = prob * _dot_nt(vx, dapp)
                drow = jnp.sum(tdp, axis=0, keepdims=True)
                ds = (tdp - prob * drow).astype(BF16)
                prob_b = prob.astype(BF16)
                xk.append(_dot(ds, qpp))
                xv.append(_dot(prob_b, dapp))
                probs.append(prob_b)
                dss.append(ds)
                dsk_rows.append(-psink * drow)
            kcat = jnp.concatenate([kvb[:, 128 * hk:128 * hk + 128], kvb[:, 512 + 128 * hk:512 + 128 * hk + 128]], axis=0)
            vcat = jnp.concatenate([kvb[:, 1024 + 128 * hk:1024 + 128 * hk + 128],
                                    kvb[:, 1536 + 128 * hk:1536 + 128 * hk + 128]], axis=0)
            app = _dot_tn(jnp.concatenate(probs, axis=0), vcat)
            dqpp = _dot_tn(jnp.concatenate(dss, axis=0), kcat)
            dq_cols += [dqpp[:BLK], dqpp[BLK:]]
            attn_cols += [app[:BLK], app[BLK:]]
            dk_cols.append(jnp.where(lo_lanes, xk[0], xk[1]))
            dv_cols.append(jnp.where(lo_lanes, xv[0], xv[1]))

        @pl.when(live)
        def _():
            dq_ref[...] = jnp.concatenate(dq_cols, axis=1).astype(BF16)
            dza_ref[...] = (do * jnp.concatenate(attn_cols, axis=1) * dsa).astype(BF16)

        band = jnp.concatenate(dk_cols + dv_cols, axis=1)
        dkv_ref[...] = (band[:BLK] + carry_ref[...]).astype(BF16)
        carry_ref[...] = band[BLK:]
        dsk_ref[...] += jnp.broadcast_to(jnp.concatenate(dsk_rows, axis=1), (8, 2 * N_KV * 2 * BLK))

    cur = lambda n: jnp.minimum(n, nb - 1)
    prev = lambda n: jnp.maximum(n - 1, 0)
    return _call(
        body, rider, name="attn_bwd", grid=(nb + 1,),
        in_specs=[pl.BlockSpec((BLK, D), lambda n: (cur(n), 0)),
                  pl.BlockSpec((BLK, KVX), lambda n: (cur(n), 0)), pl.BlockSpec((BLK, KVX), lambda n: (prev(n), 0)),
                  pl.BlockSpec((BLK, D), lambda n: (cur(n), 0)), pl.BlockSpec((BLK, D), lambda n: (cur(n), 0)),
                  pl.BlockSpec((1, 8, 2 * BLK), lambda n: (cur(n), 0, 0)), pl.BlockSpec(memory_space=pltpu.SMEM),
                  pl.BlockSpec((None, 2 * BLK, 2 * BLK), lambda n: (jnp.minimum(n, 1), 0, 0))],
        out_specs=[pl.BlockSpec((BLK, D), lambda n: (cur(n), 0)), pl.BlockSpec((BLK, D), lambda n: (prev(n), 0)),
                   pl.BlockSpec((BLK, D), lambda n: (cur(n), 0)), pl.BlockSpec((8, 2 * D), lambda n: (0, 0))],
        out_shape=[SDS((t, D), BF16), SDS((t, D), BF16), SDS((t, D), BF16), SDS((8, 2 * D), F32)],
        scratch_shapes=[pltpu.VMEM((BLK, D), F32)],
        args=(qs, kvx, kvx, u_za, d_o, lse, sinks, bias), vmem=VMEM_BIG)


def _qkv_post(u_qkv, dqs, dkv, dza, qg_s, kg_t, rider):
    t = u_qkv.shape[0]
    tm = min(512, t)

    def norm_bwd(x, dy, g):
        r = lax.rsqrt(_head_sum(x * x) * (1.0 / HEAD) + EPS)
        xhat = x * r
        dxh = dy * g
        return r * (dxh - xhat * (_head_sum(dxh * xhat) * (1.0 / HEAD))), _fold8(dy * xhat)

    def body(u_ref, dq_ref, dkv_ref, dza_ref, qg_ref, kg_ref, du_ref, dqg_ref, dkg_ref):
        @pl.when(pl.program_id(0) == 0)
        def _():
            dqg_ref[...] = jnp.zeros_like(dqg_ref)
            dkg_ref[...] = jnp.zeros_like(dkg_ref)
        dq, gq = norm_bwd(u_ref[:, 0:D].astype(F32), dq_ref[...].astype(F32), qg_ref[...])
        fold = _fold_mat()
        dk, gk = norm_bwd(u_ref[:, D:D + 256].astype(F32), _dot(dkv_ref[:, 0:512], fold), kg_ref[...])
        du_ref[:, 0:D] = dq.astype(BF16)
        du_ref[:, D:D + 256] = dk.astype(BF16)
        du_ref[:, D + 256:D + 512] = _dot(dkv_ref[:, 512:1024], fold).astype(BF16)
        du_ref[:, D + 512:2 * D + 512] = dza_ref[...]
        dqg_ref[...] += gq
        dkg_ref[...] += gk

    row = pl.BlockSpec((tm, D), lambda i: (i, 0))
    return _call(
        body, rider, name="qkv_post", grid=(t // tm,),
        in_specs=[pl.BlockSpec((tm, 1536), lambda i: (i, 0)), row, row, row,
                  pl.BlockSpec((1, D), lambda i: (0, 0)), pl.BlockSpec((1, 256), lambda i: (0, 0))],
        out_specs=[pl.BlockSpec((tm, 2560), lambda i: (i, 0)), pl.BlockSpec((8, D), lambda i: (0, 0)),
                   pl.BlockSpec((8, 256), lambda i: (0, 0))],
        out_shape=[SDS((t, 2560), BF16), SDS((8, D), F32), SDS((8, 256), F32)],
        args=(u_qkv, dqs, dkv, dza, qg_s, kg_t), vmem=VMEM_BIG)


N_GRAN = N_IN // CB
DU_COLS = ((0, 4096), (4096, 6656), (6656, N_IN))


def _du_granule(j):
    return jnp.clip(j, 0, 7), jnp.clip(j - 8, 0, 4), jnp.clip(j - 13, 0, 3)


def _du_select(j, refs, fn):
    for ref, lo, hi in zip(refs, (0, 8, 13), (8, 13, 17)):
        @pl.when((j >= lo) & (j < hi))
        def _():
            fn(ref)


def _in_proj_bwd(du, w_full, x, g, dout, rider):
    t = du[0].shape[0]
    tn = min(256, t)

    def body(a0, a1, a2, w_hbm, x_ref, g_ref, do_ref, dx_ref, dg_ref, w_ref, sem):
        @pl.when(pl.program_id(0) == 0)
        def _():
            cp = pltpu.make_async_copy(w_hbm, w_ref, sem)
            cp.start()
            dg_ref[...] = jnp.zeros_like(dg_ref)
            cp.wait()
        acc = None
        for a_ref, (lo, hi) in zip((a0, a1, a2), DU_COLS):
            part = _dot_nt(w_ref[:, lo:hi], a_ref[...])
            acc = part if acc is None else acc + part
        dh = acc.T
        xv = x_ref[...]
        r = lax.rsqrt(jnp.mean(xv * xv, axis=-1, keepdims=True) + EPS)
        xhat = xv * r
        dg_ref[...] += _fold8(dh * xhat)
        dxh = dh * g_ref[...]
        dx_ref[...] = do_ref[...] + r * (dxh - xhat * jnp.mean(dxh * xhat, axis=-1, keepdims=True))

    row = pl.BlockSpec((tn, D), lambda i: (i, 0))
    return _call(
        body, rider, name="in_proj_bwd", grid=(t // tn,),
        in_specs=[pl.BlockSpec((tn, hi - lo), lambda i: (i, 0)) for lo, hi in DU_COLS]
        + [ANY, row, pl.BlockSpec((1, D), lambda i: (0, 0)), row],
        out_specs=[row, pl.BlockSpec((8, D), lambda i: (0, 0))], out_shape=[SDS((t, D), F32), SDS((8, D), F32)],
        scratch_shapes=[pltpu.VMEM((D, N_IN), BF16), pltpu.SemaphoreType.DMA(())],
        args=(*du, w_full, x, g, dout), vmem=VMEM_BIG)


def _in_proj_wgrad(ht, du, rider):
    t = ht.shape[1]
    tk = min(4096, t)
    n_k = t // tk

    def body(h_ref, b0, b1, b2, g_ref):
        j, k = pl.program_id(0), pl.program_id(1)

        if n_k > 1:
            @pl.when(k == 0)
            def _():
                g_ref[...] = jnp.zeros_like(g_ref)

        def add(b_ref):
            if n_k > 1:
                g_ref[...] += _dot(h_ref[...], b_ref[...])
            else:
                g_ref[...] = _dot(h_ref[...], b_ref[...])
        _du_select(j, (b0, b1, b2), add)

    seg = lambda q: pl.BlockSpec((tk, CB), lambda j, k: (k, _du_granule(j)[q]))
    (g,), got = _call(
        body, rider, name="in_proj_wgrad", grid=(N_GRAN, t // tk),
        in_specs=[pl.BlockSpec((D, tk), lambda j, k: (0, k)), seg(0), seg(1), seg(2)],
        out_specs=[pl.BlockSpec((D, CB), lambda j, k: (0, j))], out_shape=[SDS((D, N_IN), F32)],
        args=(ht, *du), vmem=VMEM_BIG)
    return g, got


def _swap_rider(g_in, g_sm):
    def copies(ins, outs, send, recv, base=0):
        x, y, c = _mesh_pos()
        cps = []
        for src, dst in zip(ins, outs):
            half = src.at[1 - c] if len(src.shape) == 3 else src.at[:, :, 1 - c]
            cps.append(_rcopy(half, dst, send, recv, base + len(cps), (x, y, 1 - c)))
        return cps

    arrays = [g for g in (g_in, g_sm) if g is not None]
    shapes = [SDS((512, N_IN), F32) if len(g.shape) == 3 else SDS((3, 4, 128, D), F32) for g in arrays]
    return _Rider(arrays, shapes, len(arrays), copies)


def _add_halves_in(cc_idx, g_in, r_in):
    def body(cc_ref, a_ref, b_ref, f_ref, h_ref):
        s = a_ref[...] + b_ref[...]
        h_ref[...] = s.astype(BF16)

        @pl.when(pl.program_id(1) == cc_ref[1])
        def _():
            f_ref[...] = s

    blk = pl.BlockSpec((128, SH_IN), lambda i, j, cc: (i, j))
    return pl.pallas_call(
        body, name="add_halves_in",
        grid_spec=pltpu.PrefetchScalarGridSpec(
            num_scalar_prefetch=1, grid=(4, 4),
            in_specs=[pl.BlockSpec((None, 128, SH_IN), lambda i, j, cc: (cc[0], i, j)), blk],
            out_specs=[pl.BlockSpec((128, SH_IN), lambda i, j, cc: (i, 0)), blk]),
        out_shape=[SDS((512, SH_IN), F32), SDS((512, N_IN), BF16)], compiler_params=_cp("arbitrary", "arbitrary"),
    )(cc_idx, g_in, r_in)


def _add_halves_sm(c_idx, g_sm, r_sm):
    def body(c_ref, a_ref, b_ref, f_ref, h_ref):
        s = a_ref[...] + b_ref[...]
        f_ref[...] = s
        h_ref[...] = s.astype(BF16)

    blk = pl.BlockSpec((1, 4, 128, D), lambda a, c: (a, 0, 0, 0))
    return pl.pallas_call(
        body, name="add_halves_sm",
        grid_spec=pltpu.PrefetchScalarGridSpec(
            num_scalar_prefetch=1, grid=(3,),
            in_specs=[pl.BlockSpec((1, 4, None, 128, D), lambda a, c: (a, 0, c[0], 0, 0)), blk], out_specs=[blk, blk]),
        out_shape=[SDS((3, 4, 128, D), F32), SDS((3, 4, 128, D), BF16)], compiler_params=_cp("parallel"),
    )(c_idx, g_sm, r_sm)


def _scatter_rider(h_in, h_sm):
    def copies(ins, outs, send, recv, base=0):
        x, y, c = _mesh_pos()
        cps = []
        for src, dst in zip(ins, outs):
            for k, chip in enumerate(_other_chips(x, y)):
                their = 2 * chip[0] + chip[1]
                part = src.at[:, pl.ds(pl.multiple_of(their * SH_IN, 128), SH_IN)] if len(src.shape) == 2 else src.at[:, their]
                cps.append(_rcopy(part, dst.at[k], send, recv, base + len(cps), (*chip, c)))
        return cps

    arrays = [h for h in (h_in, h_sm) if h is not None]
    shapes = [SDS((3, 512, SH_IN), BF16) if len(h.shape) == 2 else SDS((3, 3, 128, D), BF16) for h in arrays]
    return _Rider(arrays, shapes, 3 * len(arrays), copies)


def _ride_alone(rider, name):
    return _hosted_call(None, rider, name=name, grid=(), in_specs=[], out_specs=[], out_shape=[], args=())[1]


def _final_sum_in(f_in, r_in):
    def body(a_ref, r_ref, o_ref):
        o_ref[...] = a_ref[...] + r_ref[0].astype(F32) + r_ref[1].astype(F32) + r_ref[2].astype(F32)

    return pl.pallas_call(
        body, name="final_sum_in", grid=(4,),
        in_specs=[pl.BlockSpec((128, SH_IN), lambda i: (i, 0)), pl.BlockSpec((3, 128, SH_IN), lambda i: (0, i, 0))],
        out_specs=pl.BlockSpec((128, SH_IN), lambda i: (i, 0)),
        out_shape=SDS((512, SH_IN), F32), compiler_params=_cp("parallel"),
    )(f_in, r_in)


def _final_sum_sm(chip_idx, f_sm, r_sm):
    def body(j_ref, a_ref, r_ref, o_ref):
        o_ref[...] = a_ref[...] + r_ref[0].astype(F32) + r_ref[1].astype(F32) + r_ref[2].astype(F32)

    return pl.pallas_call(
        body, name="final_sum_sm",
        grid_spec=pltpu.PrefetchScalarGridSpec(
            num_scalar_prefetch=1, grid=(3,),
            in_specs=[pl.BlockSpec((1, None, 128, D), lambda a, j: (a, j[0], 0, 0)),
                      pl.BlockSpec((3, 1, 128, D), lambda a, j: (0, a, 0, 0))],
            out_specs=pl.BlockSpec((1, 128, D), lambda a, j: (a, 0, 0))),
        out_shape=SDS((3, 128, D), F32), compiler_params=_cp("parallel"),
    )(chip_idx, f_sm, r_sm)


def _join_halves(t_in, t_sm):
    n_cp = N_LAYERS * 4
    args, plan = [], []
    for l in range(N_LAYERS):
        if t_in[l] is not None:
            plan.append((l, 0, len(args)))
            args.append(t_in[l])
        plan += [(l, a, len(args)) for a in (1, 2, 3)]
        args.append(t_sm[l])

    def body(*refs):
        ins, outs = refs[:len(args)], refs[len(args):len(args) + 4]
        send, recv, loc_in, loc_out, stage_in, stage_sm = refs[len(args) + 4:]
        x, y, c = _mesh_pos()
        cps, own = [], []

        def place(l, a, half):
            rows = 512 if a == 0 else 128
            return outs[a].at[l, pl.ds(pl.multiple_of(half * rows, rows), rows), :]

        for s, (l, a, k) in enumerate(plan):
            src = ins[k] if a == 0 else ins[k].at[a - 1]
            own.append((src, place(l, a, c), min(a, 1)))
            cp = pltpu.make_async_remote_copy(src_ref=src, dst_ref=place(l, a, c), send_sem=send.at[s],
                                              recv_sem=recv.at[s], device_id=(x, y, 1 - c), device_id_type=MESH)
            cp.start()
            cps.append(cp)
        _staged_copies(own, (stage_in, stage_sm), loc_in, loc_out)
        for s, (l, a, k) in enumerate(plan):
            got = place(l, a, 1 - c)
            pltpu.make_async_remote_copy(src_ref=got, dst_ref=got, send_sem=send.at[s], recv_sem=recv.at[s],
                                         device_id=(x, y, 1 - c), device_id_type=MESH).wait_recv()
        for cp in cps:
            cp.wait_send()

    sm = SDS((N_LAYERS, SH_ROW, D), F32)
    return pl.pallas_call(
        body, name="join_halves", in_specs=[ANY] * len(args), out_specs=[ANY] * 4,
        out_shape=[SDS((N_LAYERS, D, SH_IN), F32), sm, sm, sm],
        scratch_shapes=[pltpu.SemaphoreType.DMA((n_cp,))] * 4
        + [pltpu.VMEM((2, 512, SH_IN), F32), pltpu.VMEM((2, 128, D), F32)],
        compiler_params=_cp(vmem=VMEM_BIG),
    )(*args)


def _adam_math(w, g, m, v):
    m = ADAM_B1 * m + (1.0 - ADAM_B1) * g
    v = ADAM_B2 * v + (1.0 - ADAM_B2) * (g * g)
    m_hat = m / (1.0 - ADAM_B1 ** ADAM_STEP)
    v_hat = v / (1.0 - ADAM_B2 ** ADAM_STEP)
    delta = -ADAM_LR * (m_hat / (jnp.sqrt(v_hat) + ADAM_EPS) + ADAM_WD * w)
    return delta, m, v


def _adamw_big(w, g, m, v, name):
    rows, cols = w.shape
    tr = 128

    def body(w_ref, g_ref, m_ref, v_ref, go_ref, d_ref, nm_ref, nv_ref):
        g = g_ref[...]
        go_ref[...] = g
        d_ref[...], nm_ref[...], nv_ref[...] = _adam_math(w_ref[...], g, m_ref[...], v_ref[...])

    blk = pl.BlockSpec((tr, cols), lambda i: (i, 0))
    return pl.pallas_call(
        body, name=name, grid=(rows // tr,), in_specs=[blk] * 4, out_specs=[blk] * 4,
        out_shape=[SDS((rows, cols), F32)] * 4, compiler_params=_cp("parallel", vmem=VMEM_BIG),
    )(w, g, m, v)


def _adamw_w_in(c_idx, layer, half, w, m, v, g, chain):
    def rows(i, c):
        return layer, (4 * c[0] if half else 0) + i, 0

    slab = pl.BlockSpec((None, 128, SH_IN), rows)
    gspec = pl.BlockSpec((128, SH_IN), lambda i, c: (i, 0)) if half else slab

    def body(c_ref, w_ref, m_ref, v_ref, g_ref, *rest):
        go_ref, d_ref, nm_ref, nv_ref = rest[-4:]
        g = g_ref[...]
        go_ref[...] = g
        d_ref[...], nm_ref[...], nv_ref[...] = _adam_math(w_ref[...], g, m_ref[...], v_ref[...])

    n_chain = 0 if chain is None else 4
    return pl.pallas_call(
        body, name="adamw_w_in",
        grid_spec=pltpu.PrefetchScalarGridSpec(
            num_scalar_prefetch=1, grid=(4 if half else 8,),
            in_specs=[slab] * 3 + [gspec] + [ANY] * n_chain, out_specs=[slab] * 4),
        out_shape=[SDS((N_LAYERS, D, SH_IN), F32)] * 4,
        input_output_aliases={5 + k: k for k in range(n_chain)},
        compiler_params=_cp("parallel", vmem=VMEM_BIG),
    )(c_idx, w, m, v, g, *(chain or ()))


def _exchange_rider(chain, layer):
    def copies(ins, outs, send, recv, base=0):
        x, y, c = _mesh_pos()
        cps = []
        for k, arr in enumerate(outs):
            reg = arr.at[layer, pl.ds(pl.multiple_of(c * 512, 512), 512), :]
            cps.append(_rcopy(reg, reg, send, recv, base + k, (x, y, 1 - c)))
        return cps

    return _Rider(list(chain), [SDS(a.shape, a.dtype) for a in chain], 4, copies, aliases=tuple((i, i) for i in range(4)))


def _adamw_small(ws, gs, ms, vs):
    n = len(ws)

    def body(*refs):
        for k in range(n):
            w_ref, g_ref, m_ref, v_ref = (refs[q * n + k] for q in range(4))
            d, nm, nv = _adam_math(w_ref[...], g_ref[...], m_ref[...], v_ref[...])
            refs[4 * n + k][...] = d
            refs[5 * n + k][...] = nm
            refs[6 * n + k][...] = nv

    vm = pl.BlockSpec(memory_space=pltpu.VMEM)
    shapes = [SDS(w.shape, F32) for w in ws]
    res = pl.pallas_call(
        body, name="adamw_small", in_specs=[vm] * (4 * n), out_specs=[vm] * (3 * n), out_shape=shapes * 3,
    )(*ws, *gs, *ms, *vs)
    return res[:n], res[n:2 * n], res[2 * n:]


def _pad_rows(a, rows):
    flat = a.reshape(-1)
    return jnp.pad(flat, (0, rows * 128 - flat.shape[0])).reshape(rows, 128)


def kernel(x, norm_g, w_in, conv_w, q_norm_g, k_norm_g, sinks, w_conv_out, w_attn_out, gate_b, w_out, loss_target, m_norm_g, m_w_in, m_conv_w, m_q_norm_g, m_k_norm_g, m_sinks, m_w_conv_out, m_w_attn_out, m_gate_b, m_w_out, v_norm_g, v_w_in, v_conv_w, v_q_norm_g, v_k_norm_g, v_sinks, v_w_conv_out, v_w_attn_out, v_gate_b, v_w_out):
    xi, yi, ci = _mesh_pos()
    chip = 2 * xi + yi
    c_idx = jnp.reshape(ci, (1,)).astype(jnp.int32)
    chip_idx = jnp.reshape(chip, (1,)).astype(jnp.int32)
    cc_idx = jnp.stack([ci, chip]).astype(jnp.int32)
    t = x.shape[1]
    xs = [x.reshape(t, D)]
    tgt = loss_target.reshape(t, D)

    full_w = [[_cast_w_in(chip_idx, w_in, l), _cast_w_small(chip_idx, w_conv_out, w_attn_out, w_out, l)]
              for l in range(N_LAYERS)]
    full_w[0][0] = _ride_alone(_gather_rider(full_w[0][:1], "A"), "gather_first_ici")[0]
    full_w[0][0] = _ride_alone(_gather_rider(full_w[0][:1], "B"), "gather_first_d2d")[0]
    placed = lax.dynamic_update_slice(jnp.zeros((N_LAYERS, 3, D), F32),
                                      jnp.where(ci == 0, conv_w, 0.0), (0, 0, chip * SH_ROW))
    conv_full = _allreduce_small(placed.reshape(96, 128)).reshape(N_LAYERS, 3, D)

    qg_s = jnp.tile(q_norm_g, (1, N_Q)) * SCALE
    kg_t = jnp.tile(k_norm_g, (1, N_KV))
    bias = _band_bias()
    saved = []
    for l in range(N_LAYERS):
        nxt = full_w[l + 1] if l + 1 < N_LAYERS else None
        h, ht = _rmsnorm_fwd(xs[l], norm_g[l:l + 1])
        (u_conv, u_qkv, u_za, u_gl), got = _in_proj(h, full_w[l][0], _gather_rider(nxt[:1], "A") if nxt else None)
        if nxt:
            nxt[0] = got[0]
        y_c = _conv_fwd(u_conv, conv_full[l])
        (qs, kvx), got = _qkv_prep(u_qkv, qg_s[l:l + 1], kg_t[l:l + 1],
                                   _gather_rider(full_w[0][1:], "A") if l == 0 else None)
        if l == 0:
            full_w[0][1] = _ride_alone(_gather_rider(got, "B"), "gather_first_small_d2d")[0]
        o, lse, got = _attn_fwd(qs, kvx, u_za, sinks[l:l + 1], bias, _merge_riders(
            _gather_rider(nxt[1:], "A"), _gather_rider(nxt[:1], "B")) if nxt else None)
        if nxt:
            nxt[1], nxt[0] = got
        (x_next, y_a, y_b, merged), got = _out_proj_fwd(xs[l], y_c, o, u_gl, gate_b[l:l + 1], full_w[l][1],
                                                        _gather_rider(nxt[1:], "B") if nxt else None)
        if nxt:
            nxt[1] = got[0]
        xs.append(x_next)
        saved.append((ht, u_conv, u_qkv, u_za, u_gl, y_c, o, y_a, y_b, merged, qs, kvx, lse))

    dout, sq = _loss_head(xs[N_LAYERS], tgt)

    small, t_in, t_sm = [None] * N_LAYERS, [None] * N_LAYERS, [None] * N_LAYERS
    halves = None
    chain = None

    for l in reversed(range(N_LAYERS)):
        w_full, w_sm = full_w[l]
        last = l == 0
        ht, u_conv, u_qkv, u_za, u_gl, y_c, o, y_a, y_b, merged, qs, kvx, lse = saved[l]
        (d_ya, d_yb, du_gl, d_yc, d_o, dgb), _ = _out_proj_bwd(dout, y_a, y_b, u_gl, gate_b[l:l + 1], w_sm, None)
        g_sm = _small_wgrads(y_c, d_ya, o, d_yb, merged, dout).reshape(3, 4, 2, 128, D)
        (du_conv, dcw), got = _conv_bwd(d_yc, u_conv, conv_full[l], _merge_riders(
            _scatter_rider(None, halves[3]) if halves else None, _swap_rider(None, g_sm) if last else None))
        if halves:
            t_sm[l + 1] = _final_sum_sm(chip_idx, halves[2], got[0])
        if last:
            f_sm0, h_sm0 = _add_halves_sm(c_idx, g_sm, got[-1])
        (dqs, dkv, dza, dsk), got = _attn_bwd(d_o, qs, kvx, u_za, lse, sinks[l:l + 1], bias,
                                              _scatter_rider(halves[1], None) if halves else None)
        if halves:
            chain = _adamw_w_in(c_idx, l + 1, True, w_in, m_w_in, v_w_in, _final_sum_in(halves[0], got[0]), chain)
        dsk = jnp.sum(dsk[0].reshape(N_KV, 2, 2, BLK), axis=-1).transpose(0, 2, 1).reshape(N_Q)
        (du_attn, dqg, dkg), got = _qkv_post(u_qkv, dqs, dkv, dza, qg_s[l:l + 1], kg_t[l:l + 1],
                                             _scatter_rider(None, h_sm0) if last else None)
        if last:
            t_sm[0] = _final_sum_sm(chip_idx, f_sm0, got[0])
        du = (du_conv, du_attn, du_gl)
        g_in, got = _in_proj_wgrad(ht, du, _exchange_rider(chain, l + 1) if halves else None)
        if halves:
            chain = got
        g_in = g_in.reshape(2, 512, N_IN)
        if last:
            f_in0, h_in0 = _add_halves_in(cc_idx, g_in, _ride_alone(_swap_rider(g_in, None), "swap_last")[0])
        (dout, dng), got = _in_proj_bwd(du, w_full, xs[l], norm_g[l:l + 1], dout,
                                        _scatter_rider(h_in0, None) if last else _swap_rider(g_in, g_sm))
        if last:
            t_in[0] = _final_sum_in(f_in0, got[0])
        else:
            halves = _add_halves_in(cc_idx, g_in, got[0]) + _add_halves_sm(c_idx, g_sm, got[1])
        small[l] = (jnp.sum(dng, axis=0), SCALE * jnp.sum(dqg.reshape(8 * N_Q, HEAD), axis=0),
                    jnp.sum(dkg.reshape(8 * N_KV, HEAD), axis=0), dsk, jnp.sum(dgb, axis=0), dcw[:3])
    grad_x = dout.reshape(1, t, D)

    stack = lambda k: jnp.stack([small[l][k] for l in range(N_LAYERS)])
    pack = jnp.concatenate([_pad_rows(stack(0), 32), _pad_rows(stack(1), 8), _pad_rows(stack(2), 8),
                            _pad_rows(stack(3), 8), _pad_rows(stack(4), 64), _pad_rows(stack(5), 96),
                            _pad_rows(jnp.sum(sq) * (0.5 / D), 8)], axis=0)
    red = _allreduce_small(pack)
    loss = red[216, 0]
    g_norm_g = red[0:32].reshape(N_LAYERS, D)
    g_q_norm_g = red[32:40].reshape(-1)[:N_LAYERS * HEAD].reshape(N_LAYERS, HEAD)
    g_k_norm_g = red[40:48].reshape(-1)[:N_LAYERS * HEAD].reshape(N_LAYERS, HEAD)
    g_sinks = red[48:56].reshape(-1)[:N_LAYERS * N_Q].reshape(N_LAYERS, N_Q)
    g_gate_b = red[56:120].reshape(N_LAYERS, 2 * D)
    g_conv_full = red[120:216].reshape(N_LAYERS, 3, D)
    g_conv_w = lax.dynamic_slice(g_conv_full, (0, 0, chip * SH_ROW), (N_LAYERS, 3, SH_ROW))

    g_w_in, g_w_co, g_w_ao, g_w_out = _join_halves(t_in, t_sm)

    g_w_in, d_in, nm_in, nv_in = _adamw_w_in(c_idx, 0, False, w_in, m_w_in, v_w_in, g_w_in, chain)
    r_sm = N_LAYERS * SH_ROW
    big = {}
    for nm, w, g, m, v in (("co", w_conv_out, g_w_co, m_w_conv_out, v_w_conv_out),
                           ("ao", w_attn_out, g_w_ao, m_w_attn_out, v_w_attn_out),
                           ("out", w_out, g_w_out, m_w_out, v_w_out)):
        big[nm] = tuple(a.reshape(N_LAYERS, SH_ROW, D) for a in _adamw_big(
            w.reshape(r_sm, D), g.reshape(r_sm, D), m.reshape(r_sm, D), v.reshape(r_sm, D), "adamw_w_small"))
    g_w_co, g_w_ao, g_w_out = big["co"][0], big["ao"][0], big["out"][0]
    sm_w = [norm_g, conv_w, q_norm_g, k_norm_g, sinks, gate_b]
    sm_g = [g_norm_g, g_conv_w, g_q_norm_g, g_k_norm_g, g_sinks, g_gate_b]
    sm_m = [m_norm_g, m_conv_w, m_q_norm_g, m_k_norm_g, m_sinks, m_gate_b]
    sm_v = [v_norm_g, v_conv_w, v_q_norm_g, v_k_norm_g, v_sinks, v_gate_b]
    sd, snm, snv = _adamw_small(sm_w, sm_g, sm_m, sm_v)

    def order(norm, w_in_, conv, qn, kn, sk, co, ao, gb, wo):
        return [norm, w_in_, conv, qn, kn, sk, co, ao, gb, wo]

    grads = order(g_norm_g, g_w_in, g_conv_w, g_q_norm_g, g_k_norm_g, g_sinks, g_w_co, g_w_ao, g_gate_b, g_w_out)
    deltas = order(sd[0], d_in, sd[1], sd[2], sd[3], sd[4], big["co"][1], big["ao"][1], sd[5], big["out"][1])
    new_m = order(snm[0], nm_in, snm[1], snm[2], snm[3], snm[4], big["co"][2], big["ao"][2], snm[5], big["out"][2])
    new_v = order(snv[0], nv_in, snv[1], snv[2], snv[3], snv[4], big["co"][3], big["ao"][3], snv[5], big["out"][3])
    return (loss, grad_x, *grads, *deltas, *new_m, *new_v)
```

```python
import functools

import jax
import jax.numpy as jnp
from jax import lax
from jax.experimental import pallas as pl
from jax.experimental.pallas import tpu as pltpu

F32, BF16 = jnp.float32, jnp.bfloat16
SDS = jax.ShapeDtypeStruct
MESH = pl.DeviceIdType.MESH
ANY = pl.BlockSpec(memory_space=pl.ANY)

D = 1024
N_IN = 8704
N_LAYERS = 4
N_Q, N_KV, HEAD = 16, 4, 64
GROUP = N_Q // N_KV
BLK = 128
EPS = 1e-6
NEG = -1e30
SCALE = HEAD ** -0.5
SH_IN = N_IN // 4
SH_ROW = D // 4
CB = 512
SEG_CONV, SEG_Q, SEG_KV, SEG_ZA, SEG_GL = (0, 8), (8, 2), (10, 1), (11, 2), (13, 4)
VMEM_BIG = 56 * 1024 * 1024

ADAM_LR, ADAM_B1, ADAM_B2, ADAM_EPS, ADAM_WD, ADAM_STEP = 0.001, 0.9, 0.999, 1e-08, 0.01, 10


def _cp(*sem, vmem=None):
    return pltpu.CompilerParams(dimension_semantics=sem if sem else None, vmem_limit_bytes=vmem)


def _sigmoid(z):
    return 1.0 / (1.0 + jnp.exp(-z))


def _dot(a, b):
    return jnp.dot(a, b, preferred_element_type=F32)


def _dot_nt(a, b):
    return lax.dot_general(a, b, (((1,), (1,)), ((), ())), preferred_element_type=F32)


def _dot_tn(a, b):
    return lax.dot_general(a, b, (((0,), (0,)), ((), ())), preferred_element_type=F32)


def _rms(xh):
    r = lax.rsqrt(jnp.mean(xh * xh, axis=-1, keepdims=True) + EPS)
    return xh * r, r


def _fold8(v):
    return jnp.sum(v.reshape(v.shape[0] // 8, 8, v.shape[1]), axis=0)


def _cast_w_in(chip_idx, w, layer):
    def body(j_ref, i_ref, o_ref):
        o_ref[...] = i_ref[...].astype(BF16)

    return pl.pallas_call(
        body, name="cast_w_in",
        grid_spec=pltpu.PrefetchScalarGridSpec(
            num_scalar_prefetch=1, grid=(2,),
            in_specs=[pl.BlockSpec((None, 512, SH_IN), lambda i, j: (layer, i, 0))],
            out_specs=pl.BlockSpec((512, SH_IN), lambda i, j: (i, j[0]))),
        out_shape=SDS((D, N_IN), BF16), compiler_params=_cp("parallel"),
    )(chip_idx, w)


def _cast_w_small(chip_idx, a, b, c, layer):
    def body(j_ref, a_ref, b_ref, c_ref, o_ref):
        o_ref[0] = a_ref[...].astype(BF16)
        o_ref[1] = b_ref[...].astype(BF16)
        o_ref[2] = c_ref[...].astype(BF16)

    spec = pl.BlockSpec((None, SH_ROW, D), lambda i, j: (layer, 0, 0))
    return pl.pallas_call(
        body, name="cast_w_small",
        grid_spec=pltpu.PrefetchScalarGridSpec(
            num_scalar_prefetch=1, grid=(1,), in_specs=[spec, spec, spec],
            out_specs=pl.BlockSpec((3, SH_ROW, D), lambda i, j: (0, j[0], 0))),
        out_shape=SDS((3, D, D), BF16), compiler_params=_cp("parallel"),
    )(chip_idx, a, b, c)


def _mesh_pos():
    return lax.axis_index("x"), lax.axis_index("y"), lax.axis_index("c")


def _other_chips(x, y):
    return [(1 - x, y), (x, 1 - y), (1 - x, 1 - y)]


class _Rider:
    def __init__(self, ins, out_shape, n, copies, aliases=()):
        self.ins, self.out_shape, self.n, self.copies, self.aliases = list(ins), list(out_shape), n, copies, aliases


def _merge_riders(*riders):
    riders = [r for r in riders if r is not None]
    if len(riders) < 2:
        return riders[0] if riders else None

    def copies(ins, outs, send, recv, base=0):
        cps, i0, o0 = [], 0, 0
        for r in riders:
            cps += r.copies(ins[i0:i0 + len(r.ins)], outs[o0:o0 + len(r.out_shape)], send, recv, base + len(cps))
            i0, o0 = i0 + len(r.ins), o0 + len(r.out_shape)
        return cps

    aliases, i0, o0 = [], 0, 0
    for r in riders:
        aliases += [(i0 + i, o0 + o) for i, o in r.aliases]
        i0, o0 = i0 + len(r.ins), o0 + len(r.out_shape)
    return _Rider(sum((r.ins for r in riders), []), sum((r.out_shape for r in riders), []),
                  sum(r.n for r in riders), copies, tuple(aliases))


def _rcopy(src, dst, send, recv, k, to):
    return pltpu.make_async_remote_copy(src_ref=src, dst_ref=dst, send_sem=send.at[k], recv_sem=recv.at[k],
                                        device_id=to, device_id_type=MESH)


def _hosted_call(body, rider, *, name, grid, in_specs, out_specs, out_shape, args, scratch_shapes=(), vmem=None):
    n_in, n_out, n_scr = len(in_specs), len(out_specs), len(scratch_shapes)
    r_in, r_out = len(rider.ins), len(rider.out_shape)

    def full_body(*refs):
        host_in, rid_in = refs[:n_in], refs[n_in:n_in + r_in]
        o0 = n_in + r_in
        host_out, rid_out = refs[o0:o0 + n_out], refs[o0 + n_out:o0 + n_out + r_out]
        s0 = o0 + n_out + r_out
        host_scr, (send, recv) = refs[s0:s0 + n_scr], refs[s0 + n_scr:]
        if body is None:
            cps = rider.copies(rid_in, rid_out, send, recv)
            for cp in cps:
                cp.start()
            for cp in cps:
                cp.wait()
            return
        ids = [pl.program_id(a) for a in range(len(grid))]
        first = functools.reduce(lambda p, q: p & q, [i == 0 for i in ids])
        last = functools.reduce(lambda p, q: p & q, [i == g - 1 for i, g in zip(ids, grid)])

        @pl.when(first)
        def _():
            for cp in rider.copies(rid_in, rid_out, send, recv):
                cp.start()

        body(*host_in, *host_out, *host_scr)

        @pl.when(last)
        def _():
            for cp in rider.copies(rid_in, rid_out, send, recv):
                cp.wait()

    res = pl.pallas_call(
        full_body, name=name, grid=grid if body is not None else (),
        in_specs=list(in_specs) + [ANY] * r_in, out_specs=list(out_specs) + [ANY] * r_out,
        out_shape=list(out_shape) + rider.out_shape,
        scratch_shapes=list(scratch_shapes) + [pltpu.SemaphoreType.DMA((rider.n,))] * 2,
        input_output_aliases={n_in + i: n_out + o for i, o in rider.aliases},
        compiler_params=_cp(*(("arbitrary",) * len(grid) if body is not None else ()), vmem=vmem),
    )(*args, *rider.ins)
    return res[:n_out], res[n_out:]


def _call(body, rider, **kw):
    if rider is not None:
        return _hosted_call(body, rider, **kw)
    res = pl.pallas_call(
        body, name=kw["name"], grid=kw["grid"], in_specs=list(kw["in_specs"]), out_specs=list(kw["out_specs"]),
        out_shape=list(kw["out_shape"]), scratch_shapes=list(kw.get("scratch_shapes", ())),
        compiler_params=_cp(*(("arbitrary",) * len(kw["grid"])), vmem=kw.get("vmem")),
    )(*kw["args"])
    return res, []


def _gather_rider(arrays, stage):
    def copies(ins, outs, send, recv, base=0):
        x, y, c = _mesh_pos()
        cps = []
        for full in outs:
            for k, chip in enumerate(_other_chips(x, y)):
                whose = 2 * x + y if stage == "A" else 2 * chip[0] + chip[1]
                if len(full.shape) == 2:
                    rows, cols = full.shape[0] // 2, full.shape[1] // 4
                    reg = full.at[pl.ds(pl.multiple_of(c * rows, rows), rows), pl.ds(pl.multiple_of(whose * cols, 128), cols)]
                else:
                    reg = full.at[:, pl.ds(pl.multiple_of(whose * SH_ROW + c * 128, 128), 128), :]
                to = (*chip, c) if stage == "A" else (x, y, 1 - c)
                cps.append(_rcopy(reg, reg, send, recv, base + len(cps), to))
        return cps

    return _Rider(arrays, [SDS(v.shape, v.dtype) for v in arrays], 3 * len(arrays), copies,
                  aliases=tuple((i, i) for i in range(len(arrays))))


def _staged_copies(copies, stages, sem_in, sem_out):
    busy, count = {}, {}
    for idx, (src, dst, kind) in enumerate(copies):
        slot = count.get(kind, 0) % 2
        count[kind] = count.get(kind, 0) + 1
        if (kind, slot) in busy:
            busy.pop((kind, slot)).wait()
        buf = stages[kind].at[slot]
        cin = pltpu.make_async_copy(src, buf, sem_in.at[idx])
        cin.start()
        cin.wait()
        cout = pltpu.make_async_copy(buf, dst, sem_out.at[idx])
        cout.start()
        busy[(kind, slot)] = cout
    for cp in busy.values():
        cp.wait()


def _allreduce_small(pack):
    rows = pack.shape[0]

    def body(p_ref, o_ref, buf, send, recv):
        x, y, c = _mesh_pos()
        me = 4 * x + 2 * y + c
        sends = []
        for r in range(1, 8):
            to = (x if not (r & 4) else 1 - x, y if not (r & 2) else 1 - y, c if not (r & 1) else 1 - c)
            cp = pltpu.make_async_remote_copy(src_ref=p_ref, dst_ref=buf.at[me], send_sem=send.at[r - 1],
                                              recv_sem=recv.at[r - 1], device_id=to, device_id_type=MESH)
            cp.start()
            sends.append(cp)
        buf[me] = p_ref[...]
        for r in range(1, 8):
            frm = (4 * x + 2 * y + c) ^ r
            pltpu.make_async_remote_copy(src_ref=p_ref, dst_ref=buf.at[frm], send_sem=send.at[r - 1],
                                         recv_sem=recv.at[r - 1], device_id=(x, y, c), device_id_type=MESH).wait_recv()
        acc = buf[0]
        for d in range(1, 8):
            acc = acc + buf[d]
        o_ref[...] = acc
        for cp in sends:
            cp.wait_send()

    vm = pl.BlockSpec(memory_space=pltpu.VMEM)
    return pl.pallas_call(
        body, name="allreduce_small", in_specs=[vm], out_specs=vm, out_shape=SDS((rows, 128), F32),
        scratch_shapes=[pltpu.VMEM((8, rows, 128), F32), pltpu.SemaphoreType.DMA((7,)), pltpu.SemaphoreType.DMA((7,))],
    )(pack)


def _rmsnorm_fwd(x, g, rider):
    t = x.shape[0]
    tm = min(512, t)

    def body(x_ref, g_ref, h_ref, ht_ref):
        xv = x_ref[...]
        r = lax.rsqrt(jnp.mean(xv * xv, axis=-1, keepdims=True) + EPS)
        h = xv * r * g_ref[...]
        h_ref[...] = h.astype(BF16)
        ht_ref[...] = h.T.astype(BF16)

    return _call(
        body, rider, name="rmsnorm_fwd", grid=(t // tm,),
        in_specs=[pl.BlockSpec((tm, D), lambda i: (i, 0)), pl.BlockSpec((1, D), lambda i: (0, 0))],
        out_specs=[pl.BlockSpec((tm, D), lambda i: (i, 0)), pl.BlockSpec((D, tm), lambda i: (0, i))],
        out_shape=[SDS((t, D), BF16), SDS((D, t), BF16)], args=(x, g), vmem=VMEM_BIG)


FWD_SEGS = ((0, 8), (8, 3), (11, 2), (13, 4))


def _in_proj(h, w_full, rider):
    t = h.shape[0]
    tm = min(2048, t)

    def body(a_ref, b_ref, *o_refs):
        j = pl.program_id(1)
        for o_ref, (off, nblk) in zip(o_refs, FWD_SEGS):
            @pl.when((j >= off) & (j < off + nblk))
            def _():
                o_ref[...] = _dot(a_ref[...], b_ref[...]).astype(BF16)

    def out(seg):
        off, nblk = seg
        return pl.BlockSpec((tm, CB), lambda i, j: (i, jnp.clip(j - off, 0, nblk - 1)))

    res, got = _call(
        body, rider, name="in_proj", grid=(t // tm, N_IN // CB),
        in_specs=[pl.BlockSpec((tm, D), lambda i, j: (i, 0)), pl.BlockSpec((D, CB), lambda i, j: (0, j))],
        out_specs=[out(s) for s in FWD_SEGS], out_shape=[SDS((t, s[1] * CB), BF16) for s in FWD_SEGS],
        args=(h, w_full), vmem=VMEM_BIG)
    return res, got


def _conv_fwd(u_conv, conv_w, rider):
    t = u_conv.shape[0]
    tm = min(256, t)
    hb = tm // 16

    def body(v_ref, b_ref, c_ref, z_ref, hv_ref, hc_ref, w_ref, y_ref):
        i = pl.program_id(0)
        cv = c_ref[...].astype(F32) * v_ref[...].astype(F32)
        halo = hc_ref[...].astype(F32) * hv_ref[...].astype(F32)
        halo = jnp.where(i > 0, halo, 0.0)
        row = lax.broadcasted_iota(jnp.int32, (tm, 1), 0)
        s1 = jnp.where(row == 0, halo[15:16], pltpu.roll(cv, 1, 0))
        s2 = jnp.where(row == 0, halo[14:15], jnp.where(row == 1, halo[15:16], pltpu.roll(cv, 2, 0)))
        conv = w_ref[0:1, :] * s2 + w_ref[1:2, :] * s1 + w_ref[2:3, :] * cv
        z = z_ref[...].astype(F32)
        y_ref[...] = (b_ref[...].astype(F32) * conv * (z * _sigmoid(z))).astype(BF16)

    def col(k):
        return pl.BlockSpec((tm, D), lambda i: (i, k))

    def halo(k):
        return pl.BlockSpec((16, D), lambda i: (jnp.maximum(i * hb - 1, 0), k))

    (y_c,), got = _call(
        body, rider, name="conv_fwd", grid=(t // tm,),
        in_specs=[col(0), col(1), col(2), col(3), halo(0), halo(2), pl.BlockSpec((3, D), lambda i: (0, 0))],
        out_specs=[pl.BlockSpec((tm, D), lambda i: (i, 0))], out_shape=[SDS((t, D), BF16)],
        args=(u_conv, u_conv, u_conv, u_conv, u_conv, u_conv, conv_w), vmem=VMEM_BIG)
    return y_c, got


KVX = 4 * N_KV * 128


def _iota2(shape):
    return lax.broadcasted_iota(jnp.int32, shape, 0), lax.broadcasted_iota(jnp.int32, shape, 1)


def _head_sum(v):
    r, c = _iota2((128, 128))
    ones = ((r >> 6) == (c >> 6)).astype(BF16)
    hi = v.astype(BF16)
    lo = (v - hi.astype(F32)).astype(BF16)
    return jnp.concatenate([_dot(hi[:, g:g + 128], ones) + _dot(lo[:, g:g + 128], ones)
                            for g in range(0, v.shape[1], 128)], axis=1)


def _expand_mats():
    r, c = _iota2((N_KV * HEAD, N_KV * 128))
    base = ((r >> 6) << 7) + (r & 63)
    return (c == base).astype(BF16), (c == base + 64).astype(BF16)


def _fold_mat():
    r, c = _iota2((N_KV * 128, N_KV * HEAD))
    return (((r >> 7) == (c >> 6)) & ((r & 63) == (c & 63))).astype(BF16)


def _qkv_prep(u_qkv, qg_s, kg_t, rider):
    t = u_qkv.shape[0]
    tm = min(512, t)

    def body(u_ref, qg_ref, kg_ref, qs_ref, kvx_ref):
        q = u_ref[:, 0:D].astype(F32)
        rq = lax.rsqrt(_head_sum(q * q) * (1.0 / HEAD) + EPS)
        qs_ref[...] = (q * rq * qg_ref[...]).astype(BF16)
        k = u_ref[:, D:D + 256].astype(F32)
        rk = lax.rsqrt(_head_sum(k * k) * (1.0 / HEAD) + EPS)
        kn = (k * rk * kg_ref[...]).astype(BF16)
        v = u_ref[:, D + 256:D + 512]
        e_lo, e_hi = _expand_mats()
        kvx_ref[:, 0:512] = _dot(kn, e_lo).astype(BF16)
        kvx_ref[:, 512:1024] = _dot(kn, e_hi).astype(BF16)
        kvx_ref[:, 1024:1536] = _dot(v, e_lo).astype(BF16)
        kvx_ref[:, 1536:2048] = _dot(v, e_hi).astype(BF16)

    return _call(
        body, rider, name="qkv_prep", grid=(t // tm,),
        in_specs=[pl.BlockSpec((tm, 1536), lambda i: (i, 0)), pl.BlockSpec((1, D), lambda i: (0, 0)),
                  pl.BlockSpec((1, 256), lambda i: (0, 0))],
        out_specs=[pl.BlockSpec((tm, D), lambda i: (i, 0)), pl.BlockSpec((tm, KVX), lambda i: (i, 0))],
        out_shape=[SDS((t, D), BF16), SDS((t, KVX), BF16)], args=(u_qkv, qg_s, kg_t), vmem=VMEM_BIG)


def _band_bias():
    j, r = _iota2((2 * BLK, 2 * BLK))
    diff = (r & (BLK - 1)) - j + BLK
    band = (diff >= 0) & (diff < BLK)
    return jnp.stack([jnp.where(band & (j >= BLK), 0.0, NEG), jnp.where(band, 0.0, NEG)]).astype(F32)


def _pair_rows(ref_or_val, hk):
    return jnp.concatenate([ref_or_val[:, 256 * hk:256 * hk + 128], ref_or_val[:, 256 * hk + 128:256 * hk + 256]], axis=0)


def _sink_row(sink_ref, hk, half):
    return jnp.concatenate([jnp.full((1, BLK), sink_ref[0, GROUP * hk + half], F32),
                            jnp.full((1, BLK), sink_ref[0, GROUP * hk + 2 + half], F32)], axis=1)


def _kv_operands(kvb, hk, half):
    return (kvb[:, 512 * half + 128 * hk:512 * half + 128 * hk + 128],
            kvb[:, 1024 + 512 * half + 128 * hk:1024 + 512 * half + 128 * hk + 128])


def _attn_fwd(qs, kvx, u_za, sinks, bias, rider):
    t = qs.shape[0]
    nb = t // BLK

    def body(q_ref, kc_ref, kp_ref, za_ref, sink_ref, bias_ref, o_ref, lse_ref):
        kvb = jnp.concatenate([kp_ref[...], kc_ref[...]], axis=0)
        bias_v = bias_ref[...]
        key0 = lax.broadcasted_iota(jnp.int32, (2 * BLK, 1), 0) == 0
        ones = jnp.ones((2 * BLK, 128), BF16)
        cols = []
        for hk in range(N_KV):
            qpp = _pair_rows(q_ref, hk)
            opp = None
            for half in range(2):
                kx, vx = _kv_operands(kvb, hk, half)
                s = _dot_nt(kx, qpp) + bias_v
                sink = _sink_row(sink_ref, hk, half)
                m = jnp.maximum(jnp.max(s, axis=0, keepdims=True), sink)
                p = jnp.exp(s - m)
                es = jnp.exp(sink - m)
                lse_ref[0, 2 * hk + half:2 * hk + half + 1, :] = m + jnp.log(jnp.sum(p, axis=0, keepdims=True) + es)
                pe = jnp.where(key0, es, p).astype(BF16)
                rhs = jnp.concatenate([jnp.where(key0, jnp.zeros_like(vx), vx), ones], axis=1)
                nd = _dot_tn(pe, rhs)
                o = nd[:, :128] * (1.0 / nd[:, 128:])
                opp = o if opp is None else opp + o
            cols += [opp[:BLK], opp[BLK:]]
        za = za_ref[...].astype(F32)
        o_ref[...] = (jnp.concatenate(cols, axis=1) * (za * _sigmoid(za))).astype(BF16)

    prev = lambda n: jnp.maximum(n - 1, 0)
    (o, lse), got = _call(
        body, rider, name="attn_fwd", grid=(nb,),
        in_specs=[pl.BlockSpec((BLK, D), lambda n: (n, 0)),
                  pl.BlockSpec((BLK, KVX), lambda n: (n, 0)), pl.BlockSpec((BLK, KVX), lambda n: (prev(n), 0)),
                  pl.BlockSpec((BLK, D), lambda n: (n, 0)), pl.BlockSpec(memory_space=pltpu.SMEM),
                  pl.BlockSpec((None, 2 * BLK, 2 * BLK), lambda n: (jnp.minimum(n, 1), 0, 0))],
        out_specs=[pl.BlockSpec((BLK, D), lambda n: (n, 0)), pl.BlockSpec((1, 8, 2 * BLK), lambda n: (n, 0, 0))],
        out_shape=[SDS((t, D), BF16), SDS((nb, 8, 2 * BLK), F32)],
        args=(qs, kvx, kvx, u_za, sinks, bias), vmem=VMEM_BIG)
    return o, lse, got


def _out_proj_fwd(x, y_c, o, u_gl, gate_b, w_sm, rider):
    t = x.shape[0]
    tm = min(512, t)

    def body(x_ref, yc_ref, o_ref, gla_ref, glb_ref, gb_ref, wco_ref, wao_ref, wout_ref,
             xn_ref, ya_ref, yb_ref, mg_ref):
        ya = _dot(yc_ref[...], wco_ref[...])
        yb = _dot(o_ref[...], wao_ref[...])
        gb = gb_ref[...]
        ga_ = _sigmoid(gla_ref[...].astype(F32) + gb[:, :D])
        gb_ = _sigmoid(glb_ref[...].astype(F32) + gb[:, D:])
        merged = (ga_ * ya + gb_ * yb).astype(BF16)
        ya_ref[...] = ya.astype(BF16)
        yb_ref[...] = yb.astype(BF16)
        mg_ref[...] = merged
        xn_ref[...] = x_ref[...] + _dot(merged, wout_ref[...])

    row = pl.BlockSpec((tm, D), lambda i: (i, 0))
    wspec = lambda a: pl.BlockSpec((None, D, D), lambda i: (a, 0, 0))
    return _call(
        body, rider, name="out_proj_fwd", grid=(t // tm,),
        in_specs=[row, row, row, pl.BlockSpec((tm, D), lambda i: (i, 0)), pl.BlockSpec((tm, D), lambda i: (i, 1)),
                  pl.BlockSpec((1, 2 * D), lambda i: (0, 0)), wspec(0), wspec(1), wspec(2)],
        out_specs=[row, row, row, row],
        out_shape=[SDS((t, D), F32), SDS((t, D), BF16), SDS((t, D), BF16), SDS((t, D), BF16)],
        args=(x, y_c, o, u_gl, u_gl, gate_b, w_sm, w_sm, w_sm), vmem=VMEM_BIG)


def _loss_head(y, tgt):
    t = y.shape[0]
    tm = min(512, t)

    def body(y_ref, t_ref, dy_ref, acc_ref):
        @pl.when(pl.program_id(0) == 0)
        def _():
            acc_ref[...] = jnp.zeros_like(acc_ref)
        err = y_ref[...] - t_ref[...]
        dy_ref[...] = err * (1.0 / D)
        sq = _fold8(err * err)
        tot = sq[:, 0:128]
        for k in range(1, D // 128):
            tot = tot + sq[:, 128 * k:128 * (k + 1)]
        acc_ref[...] += tot

    row = pl.BlockSpec((tm, D), lambda i: (i, 0))
    return pl.pallas_call(
        body, name="loss_head", grid=(t // tm,), in_specs=[row, row],
        out_specs=[row, pl.BlockSpec((8, 128), lambda i: (0, 0))],
        out_shape=[SDS((t, D), F32), SDS((8, 128), F32)], compiler_params=_cp("arbitrary"),
    )(y, tgt)


def _out_proj_bwd(dout, y_a, y_b, u_gl, gate_b, w_sm, rider):
    t = dout.shape[0]
    tm = min(512, t)

    def body(do_ref, ya_ref, yb_ref, gla_ref, glb_ref, gb_ref, wco_ref, wao_ref, wout_ref,
             dya_ref, dyb_ref, dgl_ref, dyc_ref, dob_ref, dgb_ref):
        @pl.when(pl.program_id(0) == 0)
        def _():
            dgb_ref[...] = jnp.zeros_like(dgb_ref)
        dm = _dot_nt(do_ref[...].astype(BF16), wout_ref[...])
        gb = gb_ref[...]
        ga_ = _sigmoid(gla_ref[...].astype(F32) + gb[:, :D])
        gb_ = _sigmoid(glb_ref[...].astype(F32) + gb[:, D:])
        dya = (ga_ * dm).astype(BF16)
        dyb = (gb_ * dm).astype(BF16)
        dgla = ya_ref[...].astype(F32) * dm * (ga_ * (1.0 - ga_))
        dglb = yb_ref[...].astype(F32) * dm * (gb_ * (1.0 - gb_))
        dya_ref[...] = dya
        dyb_ref[...] = dyb
        dgl_ref[:, :D] = dgla.astype(BF16)
        dgl_ref[:, D:] = dglb.astype(BF16)
        dgb_ref[:, :D] += _fold8(dgla)
        dgb_ref[:, D:] += _fold8(dglb)
        dyc_ref[...] = _dot_nt(dya, wco_ref[...]).astype(BF16)
        dob_ref[...] = _dot_nt(dyb, wao_ref[...]).astype(BF16)

    row = pl.BlockSpec((tm, D), lambda i: (i, 0))
    wspec = lambda a: pl.BlockSpec((None, D, D), lambda i: (a, 0, 0))
    return _call(
        body, rider, name="out_proj_bwd", grid=(t // tm,),
        in_specs=[row, row, row, pl.BlockSpec((tm, D), lambda i: (i, 0)), pl.BlockSpec((tm, D), lambda i: (i, 1)),
                  pl.BlockSpec((1, 2 * D), lambda i: (0, 0)), wspec(0), wspec(1), wspec(2)],
        out_specs=[row, row, pl.BlockSpec((tm, 2 * D), lambda i: (i, 0)), row, row,
                   pl.BlockSpec((8, 2 * D), lambda i: (0, 0))],
        out_shape=[SDS((t, D), BF16), SDS((t, D), BF16), SDS((t, 2 * D), BF16), SDS((t, D), BF16), SDS((t, D), BF16),
                   SDS((8, 2 * D), F32)],
        args=(dout, y_a, y_b, u_gl, u_gl, gate_b, w_sm, w_sm, w_sm), vmem=VMEM_BIG)


def _small_wgrads(y_c, d_ya, o, d_yb, merged, dout):
    t = y_c.shape[0]
    tk = min(512, t)

    def body(yc_ref, dya_ref, o_ref, dyb_ref, mg_ref, do_ref, g_ref):
        @pl.when(pl.program_id(0) == 0)
        def _():
            g_ref[...] = jnp.zeros_like(g_ref)
        g_ref[0] += _dot_tn(yc_ref[...], dya_ref[...])
        g_ref[1] += _dot_tn(o_ref[...], dyb_ref[...])
        g_ref[2] += _dot_tn(mg_ref[...], do_ref[...].astype(BF16))

    row = pl.BlockSpec((tk, D), lambda k: (k, 0))
    return pl.pallas_call(
        body, name="small_wgrads", grid=(t // tk,), in_specs=[row] * 6,
        out_specs=pl.BlockSpec((3, D, D), lambda k: (0, 0, 0)), out_shape=SDS((3, D, D), F32),
        compiler_params=_cp("arbitrary", vmem=VMEM_BIG),
    )(y_c, d_ya, o, d_yb, merged, dout)


def _conv_bwd(d_yc, u_conv, conv_w, rider):
    t = d_yc.shape[0]
    tm = min(256, t)
    hb = tm // 16
    last_halo = t // 16 - 1
    n_steps = t // tm

    def body(dy_ref, v_ref, b_ref, c_ref, z_ref, hv_ref, hc_ref, ndy_ref, nb_ref, nz_ref, w_ref, du_ref, dw_ref):
        i = pl.program_id(0)

        @pl.when(i == 0)
        def _():
            dw_ref[...] = jnp.zeros_like(dw_ref)
        v, c = v_ref[...].astype(F32), c_ref[...].astype(F32)
        b, z = b_ref[...].astype(F32), z_ref[...].astype(F32)
        cv = c * v
        halo = jnp.where(i > 0, hc_ref[...].astype(F32) * hv_ref[...].astype(F32), 0.0)
        row = lax.broadcasted_iota(jnp.int32, (tm, 1), 0)
        s1 = jnp.where(row == 0, halo[15:16], pltpu.roll(cv, 1, 0))
        s2 = jnp.where(row == 0, halo[14:15], jnp.where(row == 1, halo[15:16], pltpu.roll(cv, 2, 0)))
        w0, w1, w2 = w_ref[0:1, :], w_ref[1:2, :], w_ref[2:3, :]
        conv = w0 * s2 + w1 * s1 + w2 * cv
        sig = _sigmoid(z)
        sz = z * sig
        dsz = sig * (1.0 + z * (1.0 - sig))
        dy = dy_ref[...].astype(F32)
        dconv = dy * b * sz
        nz = nz_ref[...].astype(F32)
        nxt = ndy_ref[...].astype(F32) * nb_ref[...].astype(F32) * (nz * _sigmoid(nz))
        nxt = jnp.where(i < n_steps - 1, nxt, 0.0)
        a1 = jnp.where(row == tm - 1, nxt[0:1], pltpu.roll(dconv, tm - 1, 0))
        a2 = jnp.where(row == tm - 2, nxt[0:1], jnp.where(row == tm - 1, nxt[1:2], pltpu.roll(dconv, tm - 2, 0)))
        dcv = w2 * dconv + w1 * a1 + w0 * a2
        du_ref[:, 0:D] = (dcv * c).astype(BF16)
        du_ref[:, D:2 * D] = (dy * conv * sz).astype(BF16)
        du_ref[:, 2 * D:3 * D] = (dcv * v).astype(BF16)
        du_ref[:, 3 * D:4 * D] = (dy * b * conv * dsz).astype(BF16)
        r8 = lax.broadcasted_iota(jnp.int32, (8, 1), 0)
        dw_ref[...] += jnp.where(r8 == 0, jnp.sum(dconv * s2, axis=0, keepdims=True),
                                 jnp.where(r8 == 1, jnp.sum(dconv * s1, axis=0, keepdims=True),
                                           jnp.where(r8 == 2, jnp.sum(dconv * cv, axis=0, keepdims=True), 0.0)))

    def col(k):
        return pl.BlockSpec((tm, D), lambda i: (i, k))

    def halo(k):
        return pl.BlockSpec((16, D), lambda i: (jnp.maximum(i * hb - 1, 0), k))

    def nxt(k):
        return pl.BlockSpec((16, D), lambda i: (jnp.minimum((i + 1) * hb, last_halo), k))

    return _call(
        body, rider, name="conv_bwd", grid=(t // tm,),
        in_specs=[col(0), col(0), col(1), col(2), col(3), halo(0), halo(2), nxt(0), nxt(1), nxt(3),
                  pl.BlockSpec((3, D), lambda i: (0, 0))],
        out_specs=[pl.BlockSpec((tm, 4 * D), lambda i: (i, 0)), pl.BlockSpec((8, D), lambda i: (0, 0))],
        out_shape=[SDS((t, 4 * D), BF16), SDS((8, D), F32)],
        args=(d_yc, u_conv, u_conv, u_conv, u_conv, u_conv, u_conv, d_yc, u_conv, u_conv, conv_w), vmem=VMEM_BIG)


def _attn_bwd(d_o, qs, kvx, u_za, lse, sinks, bias, rider):
    t = d_o.shape[0]
    nb = t // BLK

    def body(q_ref, kc_ref, kp_ref, za_ref, do_ref, lse_ref, sink_ref, bias_ref,
             dq_ref, dkv_ref, dza_ref, dsk_ref, carry_ref):
        n = pl.program_id(0)

        @pl.when(n == 0)
        def _():
            carry_ref[...] = jnp.zeros_like(carry_ref)
            dsk_ref[...] = jnp.zeros_like(dsk_ref)

        live = n < nb
        kvb = jnp.concatenate([kp_ref[...], kc_ref[...]], axis=0)
        bias_v = bias_ref[...]
        za = za_ref[...].astype(F32)
        sig = _sigmoid(za)
        dsa = sig * (1.0 + za * (1.0 - sig))
        do = jnp.where(live, do_ref[...].astype(F32), 0.0)
        dattn = (do * (za * sig)).astype(BF16)
        lo_lanes = lax.broadcasted_iota(jnp.int32, (1, 128), 1) < HEAD
        dq_cols, attn_cols, dk_cols, dv_cols, dsk_rows = [], [], [], [], []
        for hk in range(N_KV):
            qpp = _pair_rows(q_ref, hk)
            dapp = _pair_rows(dattn, hk)
            probs, dss, xk, xv = [], [], [], []
            for half in range(2):
                kx, vx = _kv_operands(kvb, hk, half)
                lse = lse_ref[0, 2 * hk + half:2 * hk + half + 1, :]
                prob = jnp.exp(_dot_nt(kx, qpp) + bias_v - lse)
                psink = jnp.exp(_sink_row(sink_ref, hk, half) - lse)
                tdp = prob * _dot_nt(vx, dapp)
                drow = jnp.sum(tdp, axis=0, keepdims=True)
                ds = (tdp - prob * drow).astype(BF16)
                prob_b = prob.astype(BF16)
                xk.append(_dot(ds, qpp))
                xv.append(_dot(prob_b, dapp))
                probs.append(prob_b)
                dss.append(ds)
                dsk_rows.append(-psink * drow)
            kcat = jnp.concatenate([kvb[:, 128 * hk:128 * hk + 128], kvb[:, 512 + 128 * hk:512 + 128 * hk + 128]], axis=0)
            vcat = jnp.concatenate([kvb[:, 1024 + 128 * hk:1024 + 128 * hk + 128],
                                    kvb[:, 1536 + 128 * hk:1536 + 128 * hk + 128]], axis=0)
            app = _dot_tn(jnp.concatenate(probs, axis=0), vcat)
            dqpp = _dot_tn(jnp.concatenate(dss, axis=0), kcat)
            dq_cols += [dqpp[:BLK], dqpp[BLK:]]
            attn_cols += [app[:BLK], app[BLK:]]
            dk_cols.append(jnp.where(lo_lanes, xk[0], xk[1]))
            dv_cols.append(jnp.where(lo_lanes, xv[0], xv[1]))

        @pl.when(live)
        def _():
            dq_ref[...] = jnp.concatenate(dq_cols, axis=1).astype(BF16)
            dza_ref[...] = (do * jnp.concatenate(attn_cols, axis=1) * dsa).astype(BF16)

        band = jnp.concatenate(dk_cols + dv_cols, axis=1)
        dkv_ref[...] = (band[:BLK] + carry_ref[...]).astype(BF16)
        carry_ref[...] = band[BLK:]
        dsk_ref[...] += jnp.broadcast_to(jnp.concatenate(dsk_rows, axis=1), (8, 2 * N_KV * 2 * BLK))

    cur = lambda n: jnp.minimum(n, nb - 1)
    prev = lambda n: jnp.maximum(n - 1, 0)
    return _call(
        body, rider, name="attn_bwd", grid=(nb + 1,),
        in_specs=[pl.BlockSpec((BLK, D), lambda n: (cur(n), 0)),
                  pl.BlockSpec((BLK, KVX), lambda n: (cur(n), 0)), pl.BlockSpec((BLK, KVX), lambda n: (prev(n), 0)),
                  pl.BlockSpec((BLK, D), lambda n: (cur(n), 0)), pl.BlockSpec((BLK, D), lambda n: (cur(n), 0)),
                  pl.BlockSpec((1, 8, 2 * BLK), lambda n: (cur(n), 0, 0)), pl.BlockSpec(memory_space=pltpu.SMEM),
                  pl.BlockSpec((None, 2 * BLK, 2 * BLK), lambda n: (jnp.minimum(n, 1), 0, 0))],
        out_specs=[pl.BlockSpec((BLK, D), lambda n: (cur(n), 0)), pl.BlockSpec((BLK, D), lambda n: (prev(n), 0)),
                   pl.BlockSpec((BLK, D), lambda n: (cur(n), 0)), pl.BlockSpec((8, 2 * D), lambda n: (0, 0))],
        out_shape=[SDS((t, D), BF16), SDS((t, D), BF16), SDS((t, D), BF16), SDS((8, 2 * D), F32)],
        scratch_shapes=[pltpu.VMEM((BLK, D), F32)],
        args=(qs, kvx, kvx, u_za, d_o, lse, sinks, bias), vmem=VMEM_BIG)


def _qkv_post(u_qkv, dqs, dkv, dza, qg_s, kg_t, rider):
    t = u_qkv.shape[0]
    tm = min(512, t)

    def norm_bwd(x, dy, g):
        r = lax.rsqrt(_head_sum(x * x) * (1.0 / HEAD) + EPS)
        xhat = x * r
        dxh = dy * g
        return r * (dxh - xhat * (_head_sum(dxh * xhat) * (1.0 / HEAD))), _fold8(dy * xhat)

    def body(u_ref, dq_ref, dkv_ref, dza_ref, qg_ref, kg_ref, du_ref, dqg_ref, dkg_ref):
        @pl.when(pl.program_id(0) == 0)
        def _():
            dqg_ref[...] = jnp.zeros_like(dqg_ref)
            dkg_ref[...] = jnp.zeros_like(dkg_ref)
        dq, gq = norm_bwd(u_ref[:, 0:D].astype(F32), dq_ref[...].astype(F32), qg_ref[...])
        fold = _fold_mat()
        dk, gk = norm_bwd(u_ref[:, D:D + 256].astype(F32), _dot(dkv_ref[:, 0:512], fold), kg_ref[...])
        du_ref[:, 0:D] = dq.astype(BF16)
        du_ref[:, D:D + 256] = dk.astype(BF16)
        du_ref[:, D + 256:D + 512] = _dot(dkv_ref[:, 512:1024], fold).astype(BF16)
        du_ref[:, D + 512:2 * D + 512] = dza_ref[...]
        dqg_ref[...] += gq
        dkg_ref[...] += gk

    row = pl.BlockSpec((tm, D), lambda i: (i, 0))
    return _call(
        body, rider, name="qkv_post", grid=(t // tm,),
        in_specs=[pl.BlockSpec((tm, 1536), lambda i: (i, 0)), row, row, row,
                  pl.BlockSpec((1, D), lambda i: (0, 0)), pl.BlockSpec((1, 256), lambda i: (0, 0))],
        out_specs=[pl.BlockSpec((tm, 2560), lambda i: (i, 0)), pl.BlockSpec((8, D), lambda i: (0, 0)),
                   pl.BlockSpec((8, 256), lambda i: (0, 0))],
        out_shape=[SDS((t, 2560), BF16), SDS((8, D), F32), SDS((8, 256), F32)],
        args=(u_qkv, dqs, dkv, dza, qg_s, kg_t), vmem=VMEM_BIG)


N_GRAN = N_IN // CB
DU_COLS = ((0, 4096), (4096, 6656), (6656, N_IN))


def _du_granule(j):
    return jnp.clip(j, 0, 7), jnp.clip(j - 8, 0, 4), jnp.clip(j - 13, 0, 3)


def _du_select(j, refs, fn):
    for ref, lo, hi in zip(refs, (0, 8, 13), (8, 13, 17)):
        @pl.when((j >= lo) & (j < hi))
        def _():
            fn(ref)


def _in_proj_bwd(du, w_full, x, g, dout, rider):
    t = du[0].shape[0]
    tn = min(256, t)

    def body(a0, a1, a2, w_hbm, x_ref, g_ref, do_ref, dx_ref, dg_ref, w_ref, sem):
        @pl.when(pl.program_id(0) == 0)
        def _():
            cp = pltpu.make_async_copy(w_hbm, w_ref, sem)
            cp.start()
            dg_ref[...] = jnp.zeros_like(dg_ref)
            cp.wait()
        acc = None
        for a_ref, (lo, hi) in zip((a0, a1, a2), DU_COLS):
            part = _dot_nt(w_ref[:, lo:hi], a_ref[...])
            acc = part if acc is None else acc + part
        dh = acc.T
        xv = x_ref[...]
        r = lax.rsqrt(jnp.mean(xv * xv, axis=-1, keepdims=True) + EPS)
        xhat = xv * r
        dg_ref[...] += _fold8(dh * xhat)
        dxh = dh * g_ref[...]
        dx_ref[...] = do_ref[...] + r * (dxh - xhat * jnp.mean(dxh * xhat, axis=-1, keepdims=True))

    row = pl.BlockSpec((tn, D), lambda i: (i, 0))
    return _call(
        body, rider, name="in_proj_bwd", grid=(t // tn,),
        in_specs=[pl.BlockSpec((tn, hi - lo), lambda i: (i, 0)) for lo, hi in DU_COLS]
        + [ANY, row, pl.BlockSpec((1, D), lambda i: (0, 0)), row],
        out_specs=[row, pl.BlockSpec((8, D), lambda i: (0, 0))], out_shape=[SDS((t, D), F32), SDS((8, D), F32)],
        scratch_shapes=[pltpu.VMEM((D, N_IN), BF16), pltpu.SemaphoreType.DMA(())],
        args=(*du, w_full, x, g, dout), vmem=VMEM_BIG)


def _in_proj_wgrad(ht, du, rider):
    t = ht.shape[1]
    tk = min(4096, t)
    n_k = t // tk

    def body(h_ref, b0, b1, b2, g_ref):
        j, k = pl.program_id(0), pl.program_id(1)

        if n_k > 1:
            @pl.when(k == 0)
            def _():
                g_ref[...] = jnp.zeros_like(g_ref)

        def add(b_ref):
            if n_k > 1:
                g_ref[...] += _dot(h_ref[...], b_ref[...])
            else:
                g_ref[...] = _dot(h_ref[...], b_ref[...])
        _du_select(j, (b0, b1, b2), add)

    seg = lambda q: pl.BlockSpec((tk, CB), lambda j, k: (k, _du_granule(j)[q]))
    (g,), got = _call(
        body, rider, name="in_proj_wgrad", grid=(N_GRAN, t // tk),
        in_specs=[pl.BlockSpec((D, tk), lambda j, k: (0, k)), seg(0), seg(1), seg(2)],
        out_specs=[pl.BlockSpec((D, CB), lambda j, k: (0, j))], out_shape=[SDS((D, N_IN), F32)],
        args=(ht, *du), vmem=VMEM_BIG)
    return g, got


def _swap_rider(g_in, g_sm):
    def copies(ins, outs, send, recv, base=0):
        x, y, c = _mesh_pos()
        cps = []
        for src, dst in zip(ins, outs):
            half = src.at[1 - c] if len(src.shape) == 3 else src.at[:, :, 1 - c]
            cps.append(_rcopy(half, dst, send, recv, base + len(cps), (x, y, 1 - c)))
        return cps

    arrays = [g for g in (g_in, g_sm) if g is not None]
    shapes = [SDS((512, N_IN), F32) if len(g.shape) == 3 else SDS((3, 4, 128, D), F32) for g in arrays]
    return _Rider(arrays, shapes, len(arrays), copies)


def _add_halves_in(cc_idx, g_in, r_in):
    def body(cc_ref, a_ref, b_ref, f_ref, h_ref):
        s = a_ref[...] + b_ref[...]
        h_ref[...] = s.astype(BF16)

        @pl.when(pl.program_id(1) == cc_ref[1])
        def _():
            f_ref[...] = s

    blk = pl.BlockSpec((128, SH_IN), lambda i, j, cc: (i, j))
    return pl.pallas_call(
        body, name="add_halves_in",
        grid_spec=pltpu.PrefetchScalarGridSpec(
            num_scalar_prefetch=1, grid=(4, 4),
            in_specs=[pl.BlockSpec((None, 128, SH_IN), lambda i, j, cc: (cc[0], i, j)), blk],
            out_specs=[pl.BlockSpec((128, SH_IN), lambda i, j, cc: (i, 0)), blk]),
        out_shape=[SDS((512, SH_IN), F32), SDS((512, N_IN), BF16)], compiler_params=_cp("arbitrary", "arbitrary"),
    )(cc_idx, g_in, r_in)


def _add_halves_sm(c_idx, g_sm, r_sm):
    def body(c_ref, a_ref, b_ref, f_ref, h_ref):
        s = a_ref[...] + b_ref[...]
        f_ref[...] = s
        h_ref[...] = s.astype(BF16)

    blk = pl.BlockSpec((1, 4, 128, D), lambda a, c: (a, 0, 0, 0))
    return pl.pallas_call(
        body, name="add_halves_sm",
        grid_spec=pltpu.PrefetchScalarGridSpec(
            num_scalar_prefetch=1, grid=(3,),
            in_specs=[pl.BlockSpec((1, 4, None, 128, D), lambda a, c: (a, 0, c[0], 0, 0)), blk], out_specs=[blk, blk]),
        out_shape=[SDS((3, 4, 128, D), F32), SDS((3, 4, 128, D), BF16)], compiler_params=_cp("parallel"),
    )(c_idx, g_sm, r_sm)


def _scatter_rider(h_in, h_sm):
    def copies(ins, outs, send, recv, base=0):
        x, y, c = _mesh_pos()
        cps = []
        for src, dst in zip(ins, outs):
            for k, chip in enumerate(_other_chips(x, y)):
                their = 2 * chip[0] + chip[1]
                part = src.at[:, pl.ds(pl.multiple_of(their * SH_IN, 128), SH_IN)] if len(src.shape) == 2 else src.at[:, their]
                cps.append(_rcopy(part, dst.at[k], send, recv, base + len(cps), (*chip, c)))
        return cps

    arrays = [h for h in (h_in, h_sm) if h is not None]
    shapes = [SDS((3, 512, SH_IN), BF16) if len(h.shape) == 2 else SDS((3, 3, 128, D), BF16) for h in arrays]
    return _Rider(arrays, shapes, 3 * len(arrays), copies)


def _ride_alone(rider, name):
    return _hosted_call(None, rider, name=name, grid=(), in_specs=[], out_specs=[], out_shape=[], args=())[1]


def _final_sum_in(f_in, r_in):
    def body(a_ref, r_ref, o_ref):
        o_ref[...] = a_ref[...] + r_ref[0].astype(F32) + r_ref[1].astype(F32) + r_ref[2].astype(F32)

    return pl.pallas_call(
        body, name="final_sum_in", grid=(4,),
        in_specs=[pl.BlockSpec((128, SH_IN), lambda i: (i, 0)), pl.BlockSpec((3, 128, SH_IN), lambda i: (0, i, 0))],
        out_specs=pl.BlockSpec((128, SH_IN), lambda i: (i, 0)),
        out_shape=SDS((512, SH_IN), F32), compiler_params=_cp("parallel"),
    )(f_in, r_in)


def _final_sum_sm(chip_idx, f_sm, r_sm):
    def body(j_ref, a_ref, r_ref, o_ref):
        o_ref[...] = a_ref[...] + r_ref[0].astype(F32) + r_ref[1].astype(F32) + r_ref[2].astype(F32)

    return pl.pallas_call(
        body, name="final_sum_sm",
        grid_spec=pltpu.PrefetchScalarGridSpec(
            num_scalar_prefetch=1, grid=(3,),
            in_specs=[pl.BlockSpec((1, None, 128, D), lambda a, j: (a, j[0], 0, 0)),
                      pl.BlockSpec((3, 1, 128, D), lambda a, j: (0, a, 0, 0))],
            out_specs=pl.BlockSpec((1, 128, D), lambda a, j: (a, 0, 0))),
        out_shape=SDS((3, 128, D), F32), compiler_params=_cp("parallel"),
    )(chip_idx, f_sm, r_sm)


def _join_halves(t_in, t_sm):
    n_cp = N_LAYERS * 4
    args, plan = [], []
    for l in range(N_LAYERS):
        if t_in[l] is not None:
            plan.append((l, 0, len(args)))
            args.append(t_in[l])
        plan += [(l, a, len(args)) for a in (1, 2, 3)]
        args.append(t_sm[l])

    def body(*refs):
        ins, outs = refs[:len(args)], refs[len(args):len(args) + 4]
        send, recv, loc_in, loc_out, stage_in, stage_sm = refs[len(args) + 4:]
        x, y, c = _mesh_pos()
        cps, own = [], []

        def place(l, a, half):
            rows = 512 if a == 0 else 128
            return outs[a].at[l, pl.ds(pl.multiple_of(half * rows, rows), rows), :]

        for s, (l, a, k) in enumerate(plan):
            src = ins[k] if a == 0 else ins[k].at[a - 1]
            own.append((src, place(l, a, c), min(a, 1)))
            cp = pltpu.make_async_remote_copy(src_ref=src, dst_ref=place(l, a, c), send_sem=send.at[s],
                                              recv_sem=recv.at[s], device_id=(x, y, 1 - c), device_id_type=MESH)
            cp.start()
            cps.append(cp)
        _staged_copies(own, (stage_in, stage_sm), loc_in, loc_out)
        for s, (l, a, k) in enumerate(plan):
            got = place(l, a, 1 - c)
            pltpu.make_async_remote_copy(src_ref=got, dst_ref=got, send_sem=send.at[s], recv_sem=recv.at[s],
                                         device_id=(x, y, 1 - c), device_id_type=MESH).wait_recv()
        for cp in cps:
            cp.wait_send()

    sm = SDS((N_LAYERS, SH_ROW, D), F32)
    return pl.pallas_call(
        body, name="join_halves", in_specs=[ANY] * len(args), out_specs=[ANY] * 4,
        out_shape=[SDS((N_LAYERS, D, SH_IN), F32), sm, sm, sm],
        scratch_shapes=[pltpu.SemaphoreType.DMA((n_cp,))] * 4
        + [pltpu.VMEM((2, 512, SH_IN), F32), pltpu.VMEM((2, 128, D), F32)],
        compiler_params=_cp(vmem=VMEM_BIG),
    )(*args)


def _adam_math(w, g, m, v):
    m = ADAM_B1 * m + (1.0 - ADAM_B1) * g
    v = ADAM_B2 * v + (1.0 - ADAM_B2) * (g * g)
    m_hat = m / (1.0 - ADAM_B1 ** ADAM_STEP)
    v_hat = v / (1.0 - ADAM_B2 ** ADAM_STEP)
    delta = -ADAM_LR * (m_hat / (jnp.sqrt(v_hat) + ADAM_EPS) + ADAM_WD * w)
    return delta, m, v


def _adamw_big(w, g, m, v, name):
    rows, cols = w.shape
    tr = 128

    def body(w_ref, g_ref, m_ref, v_ref, go_ref, d_ref, nm_ref, nv_ref):
        g = g_ref[...]
        go_ref[...] = g
        d_ref[...], nm_ref[...], nv_ref[...] = _adam_math(w_ref[...], g, m_ref[...], v_ref[...])

    blk = pl.BlockSpec((tr, cols), lambda i: (i, 0))
    return pl.pallas_call(
        body, name=name, grid=(rows // tr,), in_specs=[blk] * 4, out_specs=[blk] * 4,
        out_shape=[SDS((rows, cols), F32)] * 4, compiler_params=_cp("parallel", vmem=VMEM_BIG),
    )(w, g, m, v)


def _adamw_small(ws, gs, ms, vs):
    n = len(ws)

    def body(*refs):
        for k in range(n):
            w_ref, g_ref, m_ref, v_ref = (refs[q * n + k] for q in range(4))
            d, nm, nv = _adam_math(w_ref[...], g_ref[...], m_ref[...], v_ref[...])
            refs[4 * n + k][...] = d
            refs[5 * n + k][...] = nm
            refs[6 * n + k][...] = nv

    vm = pl.BlockSpec(memory_space=pltpu.VMEM)
    shapes = [SDS(w.shape, F32) for w in ws]
    res = pl.pallas_call(
        body, name="adamw_small", in_specs=[vm] * (4 * n), out_specs=[vm] * (3 * n), out_shape=shapes * 3,
    )(*ws, *gs, *ms, *vs)
    return res[:n], res[n:2 * n], res[2 * n:]


def _pad_rows(a, rows):
    flat = a.reshape(-1)
    return jnp.pad(flat, (0, rows * 128 - flat.shape[0])).reshape(rows, 128)


def kernel(x, norm_g, w_in, conv_w, q_norm_g, k_norm_g, sinks, w_conv_out, w_attn_out, gate_b, w_out, loss_target, m_norm_g, m_w_in, m_conv_w, m_q_norm_g, m_k_norm_g, m_sinks, m_w_conv_out, m_w_attn_out, m_gate_b, m_w_out, v_norm_g, v_w_in, v_conv_w, v_q_norm_g, v_k_norm_g, v_sinks, v_w_conv_out, v_w_attn_out, v_gate_b, v_w_out):
    xi, yi, ci = _mesh_pos()
    chip = 2 * xi + yi
    c_idx = jnp.reshape(ci, (1,)).astype(jnp.int32)
    chip_idx = jnp.reshape(chip, (1,)).astype(jnp.int32)
    cc_idx = jnp.stack([ci, chip]).astype(jnp.int32)
    t = x.shape[1]
    xs = [x.reshape(t, D)]
    tgt = loss_target.reshape(t, D)

    full_w = [[_cast_w_in(chip_idx, w_in, l), _cast_w_small(chip_idx, w_conv_out, w_attn_out, w_out, l)]
              for l in range(N_LAYERS)]
    conv16 = lax.dynamic_update_slice(jnp.zeros((16, D), F32), jnp.pad(conv_w.reshape(3 * N_LAYERS, SH_ROW), ((0, 4), (0, 0))),
                                      (0, chip * SH_ROW))
    qg_s = jnp.tile(q_norm_g, (1, N_Q)) * SCALE
    kg_t = jnp.tile(k_norm_g, (1, N_KV))
    bias = _band_bias()
    saved = []
    for l in range(N_LAYERS):
        nxt = full_w[l + 1] if l + 1 < N_LAYERS else None
        (h, ht), got = _rmsnorm_fwd(xs[l], norm_g[l:l + 1], _gather_rider([full_w[0][0], conv16], "A") if l == 0 else None)
        if l == 0:
            full_w[0][0], conv16 = _ride_alone(_gather_rider(got, "B"), "gather_first_d2d")
            conv_full = conv16[:3 * N_LAYERS].reshape(N_LAYERS, 3, D)
        (u_conv, u_qkv, u_za, u_gl), got = _in_proj(h, full_w[l][0], _gather_rider(nxt[:1], "A") if nxt else None)
        if nxt:
            nxt[0] = got[0]
        (qs, kvx), got = _qkv_prep(u_qkv, qg_s[l:l + 1], kg_t[l:l + 1],
                                   _gather_rider(full_w[0][1:], "A") if l == 0 else None)
        y_c, got = _conv_fwd(u_conv, conv_full[l], _gather_rider(got, "B") if l == 0 else None)
        if l == 0:
            full_w[0][1] = got[0]
        o, lse, got = _attn_fwd(qs, kvx, u_za, sinks[l:l + 1], bias, _merge_riders(
            _gather_rider(nxt[1:], "A"), _gather_rider(nxt[:1], "B")) if nxt else None)
        if nxt:
            nxt[1], nxt[0] = got
        (x_next, y_a, y_b, merged), got = _out_proj_fwd(xs[l], y_c, o, u_gl, gate_b[l:l + 1], full_w[l][1],
                                                        _gather_rider(nxt[1:], "B") if nxt else None)
        if nxt:
            nxt[1] = got[0]
        xs.append(x_next)
        saved.append((ht, u_conv, u_qkv, u_za, u_gl, y_c, o, y_a, y_b, merged, qs, kvx, lse))

    dout, sq = _loss_head(xs[N_LAYERS], tgt)

    small, t_in, t_sm = [None] * N_LAYERS, [None] * N_LAYERS, [None] * N_LAYERS
    halves = None

    for l in reversed(range(N_LAYERS)):
        w_full, w_sm = full_w[l]
        last = l == 0
        ht, u_conv, u_qkv, u_za, u_gl, y_c, o, y_a, y_b, merged, qs, kvx, lse = saved[l]
        (d_ya, d_yb, du_gl, d_yc, d_o, dgb), _ = _out_proj_bwd(dout, y_a, y_b, u_gl, gate_b[l:l + 1], w_sm, None)
        g_sm = _small_wgrads(y_c, d_ya, o, d_yb, merged, dout).reshape(3, 4, 2, 128, D)
        (du_conv, dcw), got = _conv_bwd(d_yc, u_conv, conv_full[l], _merge_riders(
            _scatter_rider(None, halves[3]) if halves else None, _swap_rider(None, g_sm) if last else None))
        if halves:
            t_sm[l + 1] = _final_sum_sm(chip_idx, halves[2], got[0])
        if last:
            f_sm0, h_sm0 = _add_halves_sm(c_idx, g_sm, got[-1])
        (dqs, dkv, dza, dsk), got = _attn_bwd(d_o, qs, kvx, u_za, lse, sinks[l:l + 1], bias,
                                              _scatter_rider(halves[1], None) if halves else None)
        if halves:
            t_in[l + 1] = _final_sum_in(halves[0], got[0])
        dsk = jnp.sum(dsk[0].reshape(N_KV, 2, 2, BLK), axis=-1).transpose(0, 2, 1).reshape(N_Q)
        (du_attn, dqg, dkg), got = _qkv_post(u_qkv, dqs, dkv, dza, qg_s[l:l + 1], kg_t[l:l + 1],
                                             _scatter_rider(None, h_sm0) if last else None)
        if last:
            t_sm[0] = _final_sum_sm(chip_idx, f_sm0, got[0])
        du = (du_conv, du_attn, du_gl)
        g_in = _in_proj_wgrad(ht, du, None)[0].reshape(2, 512, N_IN)
        if last:
            f_in0, h_in0 = _add_halves_in(cc_idx, g_in, _ride_alone(_swap_rider(g_in, None), "swap_last")[0])
        (dout, dng), got = _in_proj_bwd(du, w_full, xs[l], norm_g[l:l + 1], dout,
                                        _scatter_rider(h_in0, None) if last else _swap_rider(g_in, g_sm))
        if last:
            t_in[0] = _final_sum_in(f_in0, got[0])
        else:
            halves = _add_halves_in(cc_idx, g_in, got[0]) + _add_halves_sm(c_idx, g_sm, got[1])
        small[l] = (jnp.sum(dng, axis=0), SCALE * jnp.sum(dqg.reshape(8 * N_Q, HEAD), axis=0),
                    jnp.sum(dkg.reshape(8 * N_KV, HEAD), axis=0), dsk, jnp.sum(dgb, axis=0), dcw[:3])
    grad_x = dout.reshape(1, t, D)

    stack = lambda k: jnp.stack([small[l][k] for l in range(N_LAYERS)])
    pack = jnp.concatenate([_pad_rows(stack(0), 32), _pad_rows(stack(1), 8), _pad_rows(stack(2), 8),
                            _pad_rows(stack(3), 8), _pad_rows(stack(4), 64), _pad_rows(stack(5), 96),
                            _pad_rows(jnp.sum(sq) * (0.5 / D), 8)], axis=0)
    red = _allreduce_small(pack)
    loss = red[216, 0]
    g_norm_g = red[0:32].reshape(N_LAYERS, D)
    g_q_norm_g = red[32:40].reshape(-1)[:N_LAYERS * HEAD].reshape(N_LAYERS, HEAD)
    g_k_norm_g = red[40:48].reshape(-1)[:N_LAYERS * HEAD].reshape(N_LAYERS, HEAD)
    g_sinks = red[48:56].reshape(-1)[:N_LAYERS * N_Q].reshape(N_LAYERS, N_Q)
    g_gate_b = red[56:120].reshape(N_LAYERS, 2 * D)
    g_conv_full = red[120:216].reshape(N_LAYERS, 3, D)
    g_conv_w = lax.dynamic_slice(g_conv_full, (0, 0, chip * SH_ROW), (N_LAYERS, 3, SH_ROW))

    g_w_in, g_w_co, g_w_ao, g_w_out = _join_halves(t_in, t_sm)

    r_in = N_LAYERS * D
    g_w_in, d_in, nm_in, nv_in = (a.reshape(N_LAYERS, D, SH_IN) for a in _adamw_big(
        w_in.reshape(r_in, SH_IN), g_w_in.reshape(r_in, SH_IN), m_w_in.reshape(r_in, SH_IN),
        v_w_in.reshape(r_in, SH_IN), "adamw_w_in"))
    r_sm = N_LAYERS * SH_ROW
    big = {}
    for nm, w, g, m, v in (("co", w_conv_out, g_w_co, m_w_conv_out, v_w_conv_out),
                           ("ao", w_attn_out, g_w_ao, m_w_attn_out, v_w_attn_out),
                           ("out", w_out, g_w_out, m_w_out, v_w_out)):
        big[nm] = tuple(a.reshape(N_LAYERS, SH_ROW, D) for a in _adamw_big(
            w.reshape(r_sm, D), g.reshape(r_sm, D), m.reshape(r_sm, D), v.reshape(r_sm, D), "adamw_w_small"))
    g_w_co, g_w_ao, g_w_out = big["co"][0], big["ao"][0], big["out"][0]
    sm_w = [norm_g, conv_w, q_norm_g, k_norm_g, sinks, gate_b]
    sm_g = [g_norm_g, g_conv_w, g_q_norm_g, g_k_norm_g, g_sinks, g_gate_b]
    sm_m = [m_norm_g, m_conv_w, m_q_norm_g, m_k_norm_g, m_sinks, m_gate_b]
    sm_v = [v_norm_g, v_conv_w, v_q_norm_g, v_k_norm_g, v_sinks, v_gate_b]
    sd, snm, snv = _adamw_small(sm_w, sm_g, sm_m, sm_v)

    def order(norm, w_in_, conv, qn, kn, sk, co, ao, gb, wo):
        return [norm, w_in_, conv, qn, kn, sk, co, ao, gb, wo]

    grads = order(g_norm_g, g_w_in, g_conv_w, g_q_norm_g, g_k_norm_g, g_sinks, g_w_co, g_w_ao, g_gate_b, g_w_out)
    deltas = order(sd[0], d_in, sd[1], sd[2], sd[3], sd[4], big["co"][1], big["ao"][1], sd[5], big["out"][1])
    new_m = order(snm[0], nm_in, snm[1], snm[2], snm[3], snm[4], big["co"][2], big["ao"][2], snm[5], big["out"][2])
    new_v = order(snv[0], nv_in, snv[1], snv[2], snv[3], snv[4], big["co"][3], big["ao"][3], snv[5], big["out"][3])
    return (loss, grad_x, *grads, *deltas, *new_m, *new_v)
```

```python
import functools

import jax
import jax.numpy as jnp
from jax import lax
from jax.experimental import pallas as pl
from jax.experimental.pallas import tpu as pltpu

F32, BF16 = jnp.float32, jnp.bfloat16
SDS = jax.ShapeDtypeStruct
MESH = pl.DeviceIdType.MESH
ANY = pl.BlockSpec(memory_space=pl.ANY)

D = 1024
N_IN = 8704
N_LAYERS = 4
N_Q, N_KV, HEAD = 16, 4, 64
GROUP = N_Q // N_KV
BLK = 128
EPS = 1e-6
NEG = -1e30
SCALE = HEAD ** -0.5
SH_IN = N_IN // 4
SH_ROW = D // 4
CB = 512
SEG_CONV, SEG_Q, SEG_KV, SEG_ZA, SEG_GL = (0, 8), (8, 2), (10, 1), (11, 2), (13, 4)
VMEM_BIG = 56 * 1024 * 1024

ADAM_LR, ADAM_B1, ADAM_B2, ADAM_EPS, ADAM_WD, ADAM_STEP = 0.001, 0.9, 0.999, 1e-08, 0.01, 10


def _cp(*sem, vmem=None):
    return pltpu.CompilerParams(dimension_semantics=sem if sem else None, vmem_limit_bytes=vmem)


def _sigmoid(z):
    return 1.0 / (1.0 + jnp.exp(-z))


def _dot(a, b):
    return jnp.dot(a, b, preferred_element_type=F32)


def _dot_nt(a, b):
    return lax.dot_general(a, b, (((1,), (1,)), ((), ())), preferred_element_type=F32)


def _dot_tn(a, b):
    return lax.dot_general(a, b, (((0,), (0,)), ((), ())), preferred_element_type=F32)


def _rms(xh):
    r = lax.rsqrt(jnp.mean(xh * xh, axis=-1, keepdims=True) + EPS)
    return xh * r, r


def _fold8(v):
    return jnp.sum(v.reshape(v.shape[0] // 8, 8, v.shape[1]), axis=0)


def _cast_w_in(chip_idx, w, layer):
    def body(j_ref, i_ref, o_ref):
        o_ref[...] = i_ref[...].astype(BF16)

    return pl.pallas_call(
        body, name="cast_w_in",
        grid_spec=pltpu.PrefetchScalarGridSpec(
            num_scalar_prefetch=1, grid=(2,),
            in_specs=[pl.BlockSpec((None, 512, SH_IN), lambda i, j: (layer, i, 0))],
            out_specs=pl.BlockSpec((512, SH_IN), lambda i, j: (i, j[0]))),
        out_shape=SDS((D, N_IN), BF16), compiler_params=_cp("parallel"),
    )(chip_idx, w)


def _cast_w_small(chip_idx, a, b, c, layer):
    def body(j_ref, a_ref, b_ref, c_ref, o_ref):
        o_ref[0] = a_ref[...].astype(BF16)
        o_ref[1] = b_ref[...].astype(BF16)
        o_ref[2] = c_ref[...].astype(BF16)

    spec = pl.BlockSpec((None, SH_ROW, D), lambda i, j: (layer, 0, 0))
    return pl.pallas_call(
        body, name="cast_w_small",
        grid_spec=pltpu.PrefetchScalarGridSpec(
            num_scalar_prefetch=1, grid=(1,), in_specs=[spec, spec, spec],
            out_specs=pl.BlockSpec((3, SH_ROW, D), lambda i, j: (0, j[0], 0))),
        out_shape=SDS((3, D, D), BF16), compiler_params=_cp("parallel"),
    )(chip_idx, a, b, c)


def _mesh_pos():
    return lax.axis_index("x"), lax.axis_index("y"), lax.axis_index("c")


def _other_chips(x, y):
    return [(1 - x, y), (x, 1 - y), (1 - x, 1 - y)]


class _Rider:
    def __init__(self, ins, out_shape, n, copies, aliases=()):
        self.ins, self.out_shape, self.n, self.copies, self.aliases = list(ins), list(out_shape), n, copies, aliases


def _merge_riders(*riders):
    riders = [r for r in riders if r is not None]
    if len(riders) < 2:
        return riders[0] if riders else None

    def copies(ins, outs, send, recv, base=0):
        cps, i0, o0 = [], 0, 0
        for r in riders:
            cps += r.copies(ins[i0:i0 + len(r.ins)], outs[o0:o0 + len(r.out_shape)], send, recv, base + len(cps))
            i0, o0 = i0 + len(r.ins), o0 + len(r.out_shape)
        return cps

    aliases, i0, o0 = [], 0, 0
    for r in riders:
        aliases += [(i0 + i, o0 + o) for i, o in r.aliases]
        i0, o0 = i0 + len(r.ins), o0 + len(r.out_shape)
    return _Rider(sum((r.ins for r in riders), []), sum((r.out_shape for r in riders), []),
                  sum(r.n for r in riders), copies, tuple(aliases))


def _rcopy(src, dst, send, recv, k, to):
    return pltpu.make_async_remote_copy(src_ref=src, dst_ref=dst, send_sem=send.at[k], recv_sem=recv.at[k],
                                        device_id=to, device_id_type=MESH)


def _hosted_call(body, rider, *, name, grid, in_specs, out_specs, out_shape, args, scratch_shapes=(), vmem=None):
    n_in, n_out, n_scr = len(in_specs), len(out_specs), len(scratch_shapes)
    r_in, r_out = len(rider.ins), len(rider.out_shape)

    def full_body(*refs):
        host_in, rid_in = refs[:n_in], refs[n_in:n_in + r_in]
        o0 = n_in + r_in
        host_out, rid_out = refs[o0:o0 + n_out], refs[o0 + n_out:o0 + n_out + r_out]
        s0 = o0 + n_out + r_out
        host_scr, (send, recv) = refs[s0:s0 + n_scr], refs[s0 + n_scr:]
        if body is None:
            cps = rider.copies(rid_in, rid_out, send, recv)
            for cp in cps:
                cp.start()
            for cp in cps:
                cp.wait()
            return
        ids = [pl.program_id(a) for a in range(len(grid))]
        first = functools.reduce(lambda p, q: p & q, [i == 0 for i in ids])
        last = functools.reduce(lambda p, q: p & q, [i == g - 1 for i, g in zip(ids, grid)])

        @pl.when(first)
        def _():
            for cp in rider.copies(rid_in, rid_out, send, recv):
                cp.start()

        body(*host_in, *host_out, *host_scr)

        @pl.when(last)
        def _():
            for cp in rider.copies(rid_in, rid_out, send, recv):
                cp.wait()

    res = pl.pallas_call(
        full_body, name=name, grid=grid if body is not None else (),
        in_specs=list(in_specs) + [ANY] * r_in, out_specs=list(out_specs) + [ANY] * r_out,
        out_shape=list(out_shape) + rider.out_shape,
        scratch_shapes=list(scratch_shapes) + [pltpu.SemaphoreType.DMA((rider.n,))] * 2,
        input_output_aliases={n_in + i: n_out + o for i, o in rider.aliases},
        compiler_params=_cp(*(("arbitrary",) * len(grid) if body is not None else ()), vmem=vmem),
    )(*args, *rider.ins)
    return res[:n_out], res[n_out:]


def _call(body, rider, **kw):
    if rider is not None:
        return _hosted_call(body, rider, **kw)
    res = pl.pallas_call(
        body, name=kw["name"], grid=kw["grid"], in_specs=list(kw["in_specs"]), out_specs=list(kw["out_specs"]),
        out_shape=list(kw["out_shape"]), scratch_shapes=list(kw.get("scratch_shapes", ())),
        compiler_params=_cp(*(("arbitrary",) * len(kw["grid"])), vmem=kw.get("vmem")),
    )(*kw["args"])
    return res, []


def _gather_rider(arrays, stage):
    def region(full, whose, c, sub):
        if len(full.shape) == 2:
            rows, cols = full.shape[0] // 2, full.shape[1] // 4
            first, n = (c * rows, rows) if sub is None else (c * rows + sub * (rows // 2), rows // 2)
            return full.at[pl.ds(pl.multiple_of(first, n), n), pl.ds(pl.multiple_of(whose * cols, 128), cols)]
        first, n = (whose * SH_ROW + c * 128, 128) if sub is None else (whose * SH_ROW + c * 128 + sub * 64, 64)
        return full.at[:, pl.ds(pl.multiple_of(first, n), n), :]

    def copies(ins, outs, send, recv, base=0):
        x, y, c = _mesh_pos()
        nbr_x, nbr_y = (1 - x, y), (x, 1 - y)
        cps = []
        for full in outs:
            if stage == "N":
                plan = [(region(full, 2 * x + y, c, None), (*nbr_x, c)), (region(full, 2 * x + y, c, None), (*nbr_y, c))]
            elif stage == "F":
                plan = [(region(full, 2 * nbr_x[0] + nbr_x[1], c, 0), (*nbr_y, c)),
                        (region(full, 2 * nbr_y[0] + nbr_y[1], c, 1), (*nbr_x, c))]
            else:
                plan = [(region(full, 2 * chip[0] + chip[1], c, None), (x, y, 1 - c)) for chip in _other_chips(x, y)]
            for reg, to in plan:
                cps.append(_rcopy(reg, reg, send, recv, base + len(cps), to))
        return cps

    per_array = 3 if stage == "B" else 2
    return _Rider(arrays, [SDS(v.shape, v.dtype) for v in arrays], per_array * len(arrays), copies,
                  aliases=tuple((i, i) for i in range(len(arrays))))


def _staged_copies(copies, stages, sem_in, sem_out):
    busy, count = {}, {}
    for idx, (src, dst, kind) in enumerate(copies):
        slot = count.get(kind, 0) % 2
        count[kind] = count.get(kind, 0) + 1
        if (kind, slot) in busy:
            busy.pop((kind, slot)).wait()
        buf = stages[kind].at[slot]
        cin = pltpu.make_async_copy(src, buf, sem_in.at[idx])
        cin.start()
        cin.wait()
        cout = pltpu.make_async_copy(buf, dst, sem_out.at[idx])
        cout.start()
        busy[(kind, slot)] = cout
    for cp in busy.values():
        cp.wait()


def _allreduce_small(pack):
    rows = pack.shape[0]

    def body(p_ref, o_ref, buf, send, recv):
        x, y, c = _mesh_pos()
        me = 4 * x + 2 * y + c
        sends = []
        for r in range(1, 8):
            to = (x if not (r & 4) else 1 - x, y if not (r & 2) else 1 - y, c if not (r & 1) else 1 - c)
            cp = pltpu.make_async_remote_copy(src_ref=p_ref, dst_ref=buf.at[me], send_sem=send.at[r - 1],
                                              recv_sem=recv.at[r - 1], device_id=to, device_id_type=MESH)
            cp.start()
            sends.append(cp)
        buf[me] = p_ref[...]
        for r in range(1, 8):
            frm = (4 * x + 2 * y + c) ^ r
            pltpu.make_async_remote_copy(src_ref=p_ref, dst_ref=buf.at[frm], send_sem=send.at[r - 1],
                                         recv_sem=recv.at[r - 1], device_id=(x, y, c), device_id_type=MESH).wait_recv()
        acc = buf[0]
        for d in range(1, 8):
            acc = acc + buf[d]
        o_ref[...] = acc
        for cp in sends:
            cp.wait_send()

    vm = pl.BlockSpec(memory_space=pltpu.VMEM)
    return pl.pallas_call(
        body, name="allreduce_small", in_specs=[vm], out_specs=vm, out_shape=SDS((rows, 128), F32),
        scratch_shapes=[pltpu.VMEM((8, rows, 128), F32), pltpu.SemaphoreType.DMA((7,)), pltpu.SemaphoreType.DMA((7,))],
    )(pack)


def _rmsnorm_fwd(x, g, rider):
    t = x.shape[0]
    tm = min(512, t)

    def body(x_ref, g_ref, h_ref, ht_ref):
        xv = x_ref[...]
        r = lax.rsqrt(jnp.mean(xv * xv, axis=-1, keepdims=True) + EPS)
        h = xv * r * g_ref[...]
        h_ref[...] = h.astype(BF16)
        ht_ref[...] = h.T.astype(BF16)

    return _call(
        body, rider, name="rmsnorm_fwd", grid=(t // tm,),
        in_specs=[pl.BlockSpec((tm, D), lambda i: (i, 0)), pl.BlockSpec((1, D), lambda i: (0, 0))],
        out_specs=[pl.BlockSpec((tm, D), lambda i: (i, 0)), pl.BlockSpec((D, tm), lambda i: (0, i))],
        out_shape=[SDS((t, D), BF16), SDS((D, t), BF16)], args=(x, g), vmem=VMEM_BIG)


FWD_SEGS = ((0, 8), (8, 3), (11, 2), (13, 4))


def _in_proj(h, w_full, rider):
    t = h.shape[0]
    tm = min(2048, t)

    def body(a_ref, b_ref, *o_refs):
        j = pl.program_id(1)
        for o_ref, (off, nblk) in zip(o_refs, FWD_SEGS):
            @pl.when((j >= off) & (j < off + nblk))
            def _():
                o_ref[...] = _dot(a_ref[...], b_ref[...]).astype(BF16)

    def out(seg):
        off, nblk = seg
        return pl.BlockSpec((tm, CB), lambda i, j: (i, jnp.clip(j - off, 0, nblk - 1)))

    res, got = _call(
        body, rider, name="in_proj", grid=(t // tm, N_IN // CB),
        in_specs=[pl.BlockSpec((tm, D), lambda i, j: (i, 0)), pl.BlockSpec((D, CB), lambda i, j: (0, j))],
        out_specs=[out(s) for s in FWD_SEGS], out_shape=[SDS((t, s[1] * CB), BF16) for s in FWD_SEGS],
        args=(h, w_full), vmem=VMEM_BIG)
    return res, got


def _conv_fwd(u_conv, conv_w, rider):
    t = u_conv.shape[0]
    tm = min(256, t)
    hb = tm // 16

    def body(v_ref, b_ref, c_ref, z_ref, hv_ref, hc_ref, w_ref, y_ref):
        i = pl.program_id(0)
        cv = c_ref[...].astype(F32) * v_ref[...].astype(F32)
        halo = hc_ref[...].astype(F32) * hv_ref[...].astype(F32)
        halo = jnp.where(i > 0, halo, 0.0)
        row = lax.broadcasted_iota(jnp.int32, (tm, 1), 0)
        s1 = jnp.where(row == 0, halo[15:16], pltpu.roll(cv, 1, 0))
        s2 = jnp.where(row == 0, halo[14:15], jnp.where(row == 1, halo[15:16], pltpu.roll(cv, 2, 0)))
        conv = w_ref[0:1, :] * s2 + w_ref[1:2, :] * s1 + w_ref[2:3, :] * cv
        z = z_ref[...].astype(F32)
        y_ref[...] = (b_ref[...].astype(F32) * conv * (z * _sigmoid(z))).astype(BF16)

    def col(k):
        return pl.BlockSpec((tm, D), lambda i: (i, k))

    def halo(k):
        return pl.BlockSpec((16, D), lambda i: (jnp.maximum(i * hb - 1, 0), k))

    (y_c,), got = _call(
        body, rider, name="conv_fwd", grid=(t // tm,),
        in_specs=[col(0), col(1), col(2), col(3), halo(0), halo(2), pl.BlockSpec((3, D), lambda i: (0, 0))],
        out_specs=[pl.BlockSpec((tm, D), lambda i: (i, 0))], out_shape=[SDS((t, D), BF16)],
        args=(u_conv, u_conv, u_conv, u_conv, u_conv, u_conv, conv_w), vmem=VMEM_BIG)
    return y_c, got


KVX = 4 * N_KV * 128


def _iota2(shape):
    return lax.broadcasted_iota(jnp.int32, shape, 0), lax.broadcasted_iota(jnp.int32, shape, 1)


def _head_sum(v):
    r, c = _iota2((128, 128))
    ones = ((r >> 6) == (c >> 6)).astype(BF16)
    hi = v.astype(BF16)
    lo = (v - hi.astype(F32)).astype(BF16)
    return jnp.concatenate([_dot(hi[:, g:g + 128], ones) + _dot(lo[:, g:g + 128], ones)
                            for g in range(0, v.shape[1], 128)], axis=1)


def _expand_mats():
    r, c = _iota2((N_KV * HEAD, N_KV * 128))
    base = ((r >> 6) << 7) + (r & 63)
    return (c == base).astype(BF16), (c == base + 64).astype(BF16)


def _fold_mat():
    r, c = _iota2((N_KV * 128, N_KV * HEAD))
    return (((r >> 7) == (c >> 6)) & ((r & 63) == (c & 63))).astype(BF16)


def _qkv_prep(u_qkv, qg_s, kg_t, rider):
    t = u_qkv.shape[0]
    tm = min(512, t)

    def body(u_ref, qg_ref, kg_ref, qs_ref, kvx_ref):
        q = u_ref[:, 0:D].astype(F32)
        rq = lax.rsqrt(_head_sum(q * q) * (1.0 / HEAD) + EPS)
        qs_ref[...] = (q * rq * qg_ref[...]).astype(BF16)
        k = u_ref[:, D:D + 256].astype(F32)
        rk = lax.rsqrt(_head_sum(k * k) * (1.0 / HEAD) + EPS)
        kn = (k * rk * kg_ref[...]).astype(BF16)
        v = u_ref[:, D + 256:D + 512]
        e_lo, e_hi = _expand_mats()
        kvx_ref[:, 0:512] = _dot(kn, e_lo).astype(BF16)
        kvx_ref[:, 512:1024] = _dot(kn, e_hi).astype(BF16)
        kvx_ref[:, 1024:1536] = _dot(v, e_lo).astype(BF16)
        kvx_ref[:, 1536:2048] = _dot(v, e_hi).astype(BF16)

    return _call(
        body, rider, name="qkv_prep", grid=(t // tm,),
        in_specs=[pl.BlockSpec((tm, 1536), lambda i: (i, 0)), pl.BlockSpec((1, D), lambda i: (0, 0)),
                  pl.BlockSpec((1, 256), lambda i: (0, 0))],
        out_specs=[pl.BlockSpec((tm, D), lambda i: (i, 0)), pl.BlockSpec((tm, KVX), lambda i: (i, 0))],
        out_shape=[SDS((t, D), BF16), SDS((t, KVX), BF16)], args=(u_qkv, qg_s, kg_t), vmem=VMEM_BIG)


def _band_bias():
    j, r = _iota2((2 * BLK, 2 * BLK))
    diff = (r & (BLK - 1)) - j + BLK
    band = (diff >= 0) & (diff < BLK)
    return jnp.stack([jnp.where(band & (j >= BLK), 0.0, NEG), jnp.where(band, 0.0, NEG)]).astype(F32)


def _pair_rows(ref_or_val, hk):
    return jnp.concatenate([ref_or_val[:, 256 * hk:256 * hk + 128], ref_or_val[:, 256 * hk + 128:256 * hk + 256]], axis=0)


def _sink_row(sink_ref, hk, half):
    return jnp.concatenate([jnp.full((1, BLK), sink_ref[0, GROUP * hk + half], F32),
                            jnp.full((1, BLK), sink_ref[0, GROUP * hk + 2 + half], F32)], axis=1)


def _kv_operands(kvb, hk, half):
    return (kvb[:, 512 * half + 128 * hk:512 * half + 128 * hk + 128],
            kvb[:, 1024 + 512 * half + 128 * hk:1024 + 512 * half + 128 * hk + 128])


def _attn_fwd(qs, kvx, u_za, sinks, bias, rider):
    t = qs.shape[0]
    nb = t // BLK

    def body(q_ref, kc_ref, kp_ref, za_ref, sink_ref, bias_ref, o_ref, lse_ref):
        kvb = jnp.concatenate([kp_ref[...], kc_ref[...]], axis=0)
        bias_v = bias_ref[...]
        key0 = lax.broadcasted_iota(jnp.int32, (2 * BLK, 1), 0) == 0
        ones = jnp.ones((2 * BLK, 128), BF16)
        cols = []
        for hk in range(N_KV):
            qpp = _pair_rows(q_ref, hk)
            opp = None
            for half in range(2):
                kx, vx = _kv_operands(kvb, hk, half)
                s = _dot_nt(kx, qpp) + bias_v
                sink = _sink_row(sink_ref, hk, half)
                m = jnp.maximum(jnp.max(s, axis=0, keepdims=True), sink)
                p = jnp.exp(s - m)
                es = jnp.exp(sink - m)
                lse_ref[0, 2 * hk + half:2 * hk + half + 1, :] = m + jnp.log(jnp.sum(p, axis=0, keepdims=True) + es)
                pe = jnp.where(key0, es, p).astype(BF16)
                rhs = jnp.concatenate([jnp.where(key0, jnp.zeros_like(vx), vx), ones], axis=1)
                nd = _dot_tn(pe, rhs)
                o = nd[:, :128] * (1.0 / nd[:, 128:])
                opp = o if opp is None else opp + o
            cols += [opp[:BLK], opp[BLK:]]
        za = za_ref[...].astype(F32)
        o_ref[...] = (jnp.concatenate(cols, axis=1) * (za * _sigmoid(za))).astype(BF16)

    prev = lambda n: jnp.maximum(n - 1, 0)
    (o, lse), got = _call(
        body, rider, name="attn_fwd", grid=(nb,),
        in_specs=[pl.BlockSpec((BLK, D), lambda n: (n, 0)),
                  pl.BlockSpec((BLK, KVX), lambda n: (n, 0)), pl.BlockSpec((BLK, KVX), lambda n: (prev(n), 0)),
                  pl.BlockSpec((BLK, D), lambda n: (n, 0)), pl.BlockSpec(memory_space=pltpu.SMEM),
                  pl.BlockSpec((None, 2 * BLK, 2 * BLK), lambda n: (jnp.minimum(n, 1), 0, 0))],
        out_specs=[pl.BlockSpec((BLK, D), lambda n: (n, 0)), pl.BlockSpec((1, 8, 2 * BLK), lambda n: (n, 0, 0))],
        out_shape=[SDS((t, D), BF16), SDS((nb, 8, 2 * BLK), F32)],
        args=(qs, kvx, kvx, u_za, sinks, bias), vmem=VMEM_BIG)
    return o, lse, got


def _out_proj_fwd(x, y_c, o, u_gl, gate_b, w_sm, rider):
    t = x.shape[0]
    tm = min(512, t)

    def body(x_ref, yc_ref, o_ref, gla_ref, glb_ref, gb_ref, wco_ref, wao_ref, wout_ref,
             xn_ref, ya_ref, yb_ref, mg_ref):
        ya = _dot(yc_ref[...], wco_ref[...])
        yb = _dot(o_ref[...], wao_ref[...])
        gb = gb_ref[...]
        ga_ = _sigmoid(gla_ref[...].astype(F32) + gb[:, :D])
        gb_ = _sigmoid(glb_ref[...].astype(F32) + gb[:, D:])
        merged = (ga_ * ya + gb_ * yb).astype(BF16)
        ya_ref[...] = ya.astype(BF16)
        yb_ref[...] = yb.astype(BF16)
        mg_ref[...] = merged
        xn_ref[...] = x_ref[...] + _dot(merged, wout_ref[...])

    row = pl.BlockSpec((tm, D), lambda i: (i, 0))
    wspec = lambda a: pl.BlockSpec((None, D, D), lambda i: (a, 0, 0))
    return _call(
        body, rider, name="out_proj_fwd", grid=(t // tm,),
        in_specs=[row, row, row, pl.BlockSpec((tm, D), lambda i: (i, 0)), pl.BlockSpec((tm, D), lambda i: (i, 1)),
                  pl.BlockSpec((1, 2 * D), lambda i: (0, 0)), wspec(0), wspec(1), wspec(2)],
        out_specs=[row, row, row, row],
        out_shape=[SDS((t, D), F32), SDS((t, D), BF16), SDS((t, D), BF16), SDS((t, D), BF16)],
        args=(x, y_c, o, u_gl, u_gl, gate_b, w_sm, w_sm, w_sm), vmem=VMEM_BIG)


def _loss_head(y, tgt):
    t = y.shape[0]
    tm = min(512, t)

    def body(y_ref, t_ref, dy_ref, acc_ref):
        @pl.when(pl.program_id(0) == 0)
        def _():
            acc_ref[...] = jnp.zeros_like(acc_ref)
        err = y_ref[...] - t_ref[...]
        dy_ref[...] = err * (1.0 / D)
        sq = _fold8(err * err)
        tot = sq[:, 0:128]
        for k in range(1, D // 128):
            tot = tot + sq[:, 128 * k:128 * (k + 1)]
        acc_ref[...] += tot

    row = pl.BlockSpec((tm, D), lambda i: (i, 0))
    return pl.pallas_call(
        body, name="loss_head", grid=(t // tm,), in_specs=[row, row],
        out_specs=[row, pl.BlockSpec((8, 128), lambda i: (0, 0))],
        out_shape=[SDS((t, D), F32), SDS((8, 128), F32)], compiler_params=_cp("arbitrary"),
    )(y, tgt)


def _out_proj_bwd(dout, y_a, y_b, u_gl, gate_b, w_sm, rider):
    t = dout.shape[0]
    tm = min(512, t)

    def body(do_ref, ya_ref, yb_ref, gla_ref, glb_ref, gb_ref, wco_ref, wao_ref, wout_ref,
             dya_ref, dyb_ref, dgl_ref, dyc_ref, dob_ref, dgb_ref):
        @pl.when(pl.program_id(0) == 0)
        def _():
            dgb_ref[...] = jnp.zeros_like(dgb_ref)
        dm = _dot_nt(do_ref[...].astype(BF16), wout_ref[...])
        gb = gb_ref[...]
        ga_ = _sigmoid(gla_ref[...].astype(F32) + gb[:, :D])
        gb_ = _sigmoid(glb_ref[...].astype(F32) + gb[:, D:])
        dya = (ga_ * dm).astype(BF16)
        dyb = (gb_ * dm).astype(BF16)
        dgla = ya_ref[...].astype(F32) * dm * (ga_ * (1.0 - ga_))
        dglb = yb_ref[...].astype(F32) * dm * (gb_ * (1.0 - gb_))
        dya_ref[...] = dya
        dyb_ref[...] = dyb
        dgl_ref[:, :D] = dgla.astype(BF16)
        dgl_ref[:, D:] = dglb.astype(BF16)
        dgb_ref[:, :D] += _fold8(dgla)
        dgb_ref[:, D:] += _fold8(dglb)
        dyc_ref[...] = _dot_nt(dya, wco_ref[...]).astype(BF16)
        dob_ref[...] = _dot_nt(dyb, wao_ref[...]).astype(BF16)

    row = pl.BlockSpec((tm, D), lambda i: (i, 0))
    wspec = lambda a: pl.BlockSpec((None, D, D), lambda i: (a, 0, 0))
    return _call(
        body, rider, name="out_proj_bwd", grid=(t // tm,),
        in_specs=[row, row, row, pl.BlockSpec((tm, D), lambda i: (i, 0)), pl.BlockSpec((tm, D), lambda i: (i, 1)),
                  pl.BlockSpec((1, 2 * D), lambda i: (0, 0)), wspec(0), wspec(1), wspec(2)],
        out_specs=[row, row, pl.BlockSpec((tm, 2 * D), lambda i: (i, 0)), row, row,
                   pl.BlockSpec((8, 2 * D), lambda i: (0, 0))],
        out_shape=[SDS((t, D), BF16), SDS((t, D), BF16), SDS((t, 2 * D), BF16), SDS((t, D), BF16), SDS((t, D), BF16),
                   SDS((8, 2 * D), F32)],
        args=(dout, y_a, y_b, u_gl, u_gl, gate_b, w_sm, w_sm, w_sm), vmem=VMEM_BIG)


def _small_wgrads(y_c, d_ya, o, d_yb, merged, dout):
    t = y_c.shape[0]
    tk = min(512, t)

    def body(yc_ref, dya_ref, o_ref, dyb_ref, mg_ref, do_ref, g_ref):
        @pl.when(pl.program_id(0) == 0)
        def _():
            g_ref[...] = jnp.zeros_like(g_ref)
        g_ref[0] += _dot_tn(yc_ref[...], dya_ref[...])
        g_ref[1] += _dot_tn(o_ref[...], dyb_ref[...])
        g_ref[2] += _dot_tn(mg_ref[...], do_ref[...].astype(BF16))

    row = pl.BlockSpec((tk, D), lambda k: (k, 0))
    return pl.pallas_call(
        body, name="small_wgrads", grid=(t // tk,), in_specs=[row] * 6,
        out_specs=pl.BlockSpec((3, D, D), lambda k: (0, 0, 0)), out_shape=SDS((3, D, D), F32),
        compiler_params=_cp("arbitrary", vmem=VMEM_BIG),
    )(y_c, d_ya, o, d_yb, merged, dout)


def _conv_bwd(d_yc, u_conv, conv_w, rider):
    t = d_yc.shape[0]
    tm = min(256, t)
    hb = tm // 16
    last_halo = t // 16 - 1
    n_steps = t // tm

    def body(dy_ref, v_ref, b_ref, c_ref, z_ref, hv_ref, hc_ref, ndy_ref, nb_ref, nz_ref, w_ref, du_ref, dw_ref):
        i = pl.program_id(0)

        @pl.when(i == 0)
        def _():
            dw_ref[...] = jnp.zeros_like(dw_ref)
        v, c = v_ref[...].astype(F32), c_ref[...].astype(F32)
        b, z = b_ref[...].astype(F32), z_ref[...].astype(F32)
        cv = c * v
        halo = jnp.where(i > 0, hc_ref[...].astype(F32) * hv_ref[...].astype(F32), 0.0)
        row = lax.broadcasted_iota(jnp.int32, (tm, 1), 0)
        s1 = jnp.where(row == 0, halo[15:16], pltpu.roll(cv, 1, 0))
        s2 = jnp.where(row == 0, halo[14:15], jnp.where(row == 1, halo[15:16], pltpu.roll(cv, 2, 0)))
        w0, w1, w2 = w_ref[0:1, :], w_ref[1:2, :], w_ref[2:3, :]
        conv = w0 * s2 + w1 * s1 + w2 * cv
        sig = _sigmoid(z)
        sz = z * sig
        dsz = sig * (1.0 + z * (1.0 - sig))
        dy = dy_ref[...].astype(F32)
        dconv = dy * b * sz
        nz = nz_ref[...].astype(F32)
        nxt = ndy_ref[...].astype(F32) * nb_ref[...].astype(F32) * (nz * _sigmoid(nz))
        nxt = jnp.where(i < n_steps - 1, nxt, 0.0)
        a1 = jnp.where(row == tm - 1, nxt[0:1], pltpu.roll(dconv, tm - 1, 0))
        a2 = jnp.where(row == tm - 2, nxt[0:1], jnp.where(row == tm - 1, nxt[1:2], pltpu.roll(dconv, tm - 2, 0)))
        dcv = w2 * dconv + w1 * a1 + w0 * a2
        du_ref[:, 0:D] = (dcv * c).astype(BF16)
        du_ref[:, D:2 * D] = (dy * conv * sz).astype(BF16)
        du_ref[:, 2 * D:3 * D] = (dcv * v).astype(BF16)
        du_ref[:, 3 * D:4 * D] = (dy * b * conv * dsz).astype(BF16)
        r8 = lax.broadcasted_iota(jnp.int32, (8, 1), 0)
        dw_ref[...] += jnp.where(r8 == 0, jnp.sum(dconv * s2, axis=0, keepdims=True),
                                 jnp.where(r8 == 1, jnp.sum(dconv * s1, axis=0, keepdims=True),
                                           jnp.where(r8 == 2, jnp.sum(dconv * cv, axis=0, keepdims=True), 0.0)))

    def col(k):
        return pl.BlockSpec((tm, D), lambda i: (i, k))

    def halo(k):
        return pl.BlockSpec((16, D), lambda i: (jnp.maximum(i * hb - 1, 0), k))

    def nxt(k):
        return pl.BlockSpec((16, D), lambda i: (jnp.minimum((i + 1) * hb, last_halo), k))

    return _call(
        body, rider, name="conv_bwd", grid=(t // tm,),
        in_specs=[col(0), col(0), col(1), col(2), col(3), halo(0), halo(2), nxt(0), nxt(1), nxt(3),
                  pl.BlockSpec((3, D), lambda i: (0, 0))],
        out_specs=[pl.BlockSpec((tm, 4 * D), lambda i: (i, 0)), pl.BlockSpec((8, D), lambda i: (0, 0))],
        out_shape=[SDS((t, 4 * D), BF16), SDS((8, D), F32)],
        args=(d_yc, u_conv, u_conv, u_conv, u_conv, u_conv, u_conv, d_yc, u_conv, u_conv, conv_w), vmem=VMEM_BIG)


def _attn_bwd(d_o, qs, kvx, u_za, lse, sinks, bias, rider):
    t = d_o.shape[0]
    nb = t // BLK

    def body(q_ref, kc_ref, kp_ref, za_ref, do_ref, lse_ref, sink_ref, bias_ref,
             dq_ref, dkv_ref, dza_ref, dsk_ref, carry_ref):
        n = pl.program_id(0)

        @pl.when(n == 0)
        def _():
            carry_ref[...] = jnp.zeros_like(carry_ref)
            dsk_ref[...] = jnp.zeros_like(dsk_ref)

        live = n < nb
        kvb = jnp.concatenate([kp_ref[...], kc_ref[...]], axis=0)
        bias_v = bias_ref[...]
        za = za_ref[...].astype(F32)
        sig = _sigmoid(za)
        dsa = sig * (1.0 + za * (1.0 - sig))
        do = jnp.where(live, do_ref[...].astype(F32), 0.0)
        dattn = (do * (za * sig)).astype(BF16)
        lo_lanes = lax.broadcasted_iota(jnp.int32, (1, 128), 1) < HEAD
        dq_cols, attn_cols, dk_cols, dv_cols, dsk_rows = [], [], [], [], []
        for hk in range(N_KV):
            qpp = _pair_rows(q_ref, hk)
            dapp = _pair_rows(dattn, hk)
            probs, dss, xk, xv = [], [], [], []
            for half in range(2):
                kx, vx = _kv_operands(kvb, hk, half)
                lse = lse_ref[0, 2 * hk + half:2 * hk + half + 1, :]
                prob = jnp.exp(_dot_nt(kx, qpp) + bias_v - lse)
                psink = jnp.exp(_sink_row(sink_ref, hk, half) - lse)
                tdp = prob * _dot_nt(vx, dapp)
                drow = jnp.sum(tdp, axis=0, keepdims=True)
                ds = (tdp - prob * drow).astype(BF16)
                prob_b = prob.astype(BF16)
                xk.append(_dot(ds, qpp))
                xv.append(_dot(prob_b, dapp))
                probs.append(prob_b)
                dss.append(ds)
                dsk_rows.append(-psink * drow)
            kcat = jnp.concatenate([kvb[:, 128 * hk:128 * hk + 128], kvb[:, 512 + 128 * hk:512 + 128 * hk + 128]], axis=0)
            vcat = jnp.concatenate([kvb[:, 1024 + 128 * hk:1024 + 128 * hk + 128],
                                    kvb[:, 1536 + 128 * hk:1536 + 128 * hk + 128]], axis=0)
            app = _dot_tn(jnp.concatenate(probs, axis=0), vcat)
            dqpp = _dot_tn(jnp.concatenate(dss, axis=0), kcat)
            dq_cols += [dqpp[:BLK], dqpp[BLK:]]
            attn_cols += [app[:BLK], app[BLK:]]
            dk_cols.append(jnp.where(lo_lanes, xk[0], xk[1]))
            dv_cols.append(jnp.where(lo_lanes, xv[0], xv[1]))

        @pl.when(live)
        def _():
            dq_ref[...] = jnp.concatenate(dq_cols, axis=1).astype(BF16)
            dza_ref[...] = (do * jnp.concatenate(attn_cols, axis=1) * dsa).astype(BF16)

        band = jnp.concatenate(dk_cols + dv_cols, axis=1)
        dkv_ref[...] = (band[:BLK] + carry_ref[...]).astype(BF16)
        carry_ref[...] = band[BLK:]
        dsk_ref[...] += jnp.broadcast_to(jnp.concatenate(dsk_rows, axis=1), (8, 2 * N_KV * 2 * BLK))

    cur = lambda n: jnp.minimum(n, nb - 1)
    prev = lambda n: jnp.maximum(n - 1, 0)
    return _call(
        body, rider, name="attn_bwd", grid=(nb + 1,),
        in_specs=[pl.BlockSpec((BLK, D), lambda n: (cur(n), 0)),
                  pl.BlockSpec((BLK, KVX), lambda n: (cur(n), 0)), pl.BlockSpec((BLK, KVX), lambda n: (prev(n), 0)),
                  pl.BlockSpec((BLK, D), lambda n: (cur(n), 0)), pl.BlockSpec((BLK, D), lambda n: (cur(n), 0)),
                  pl.BlockSpec((1, 8, 2 * BLK), lambda n: (cur(n), 0, 0)), pl.BlockSpec(memory_space=pltpu.SMEM),
                  pl.BlockSpec((None, 2 * BLK, 2 * BLK), lambda n: (jnp.minimum(n, 1), 0, 0))],
        out_specs=[pl.BlockSpec((BLK, D), lambda n: (cur(n), 0)), pl.BlockSpec((BLK, D), lambda n: (prev(n), 0)),
                   pl.BlockSpec((BLK, D), lambda n: (cur(n), 0)), pl.BlockSpec((8, 2 * D), lambda n: (0, 0))],
        out_shape=[SDS((t, D), BF16), SDS((t, D), BF16), SDS((t, D), BF16), SDS((8, 2 * D), F32)],
        scratch_shapes=[pltpu.VMEM((BLK, D), F32)],
        args=(qs, kvx, kvx, u_za, d_o, lse, sinks, bias), vmem=VMEM_BIG)


def _qkv_post(u_qkv, dqs, dkv, dza, qg_s, kg_t, rider):
    t = u_qkv.shape[0]
    tm = min(512, t)

    def norm_bwd(x, dy, g):
        r = lax.rsqrt(_head_sum(x * x) * (1.0 / HEAD) + EPS)
        xhat = x * r
        dxh = dy * g
        return r * (dxh - xhat * (_head_sum(dxh * xhat) * (1.0 / HEAD))), _fold8(dy * xhat)

    def body(u_ref, dq_ref, dkv_ref, dza_ref, qg_ref, kg_ref, du_ref, dqg_ref, dkg_ref):
        @pl.when(pl.program_id(0) == 0)
        def _():
            dqg_ref[...] = jnp.zeros_like(dqg_ref)
            dkg_ref[...] = jnp.zeros_like(dkg_ref)
        dq, gq = norm_bwd(u_ref[:, 0:D].astype(F32), dq_ref[...].astype(F32), qg_ref[...])
        fold = _fold_mat()
        dk, gk = norm_bwd(u_ref[:, D:D + 256].astype(F32), _dot(dkv_ref[:, 0:512], fold), kg_ref[...])
        du_ref[:, 0:D] = dq.astype(BF16)
        du_ref[:, D:D + 256] = dk.astype(BF16)
        du_ref[:, D + 256:D + 512] = _dot(dkv_ref[:, 512:1024], fold).astype(BF16)
        du_ref[:, D + 512:2 * D + 512] = dza_ref[...]
        dqg_ref[...] += gq
        dkg_ref[...] += gk

    row = pl.BlockSpec((tm, D), lambda i: (i, 0))
    return _call(
        body, rider, name="qkv_post", grid=(t // tm,),
        in_specs=[pl.BlockSpec((tm, 1536), lambda i: (i, 0)), row, row, row,
                  pl.BlockSpec((1, D), lambda i: (0, 0)), pl.BlockSpec((1, 256), lambda i: (0, 0))],
        out_specs=[pl.BlockSpec((tm, 2560), lambda i: (i, 0)), pl.BlockSpec((8, D), lambda i: (0, 0)),
                   pl.BlockSpec((8, 256), lambda i: (0, 0))],
        out_shape=[SDS((t, 2560), BF16), SDS((8, D), F32), SDS((8, 256), F32)],
        args=(u_qkv, dqs, dkv, dza, qg_s, kg_t), vmem=VMEM_BIG)


N_GRAN = N_IN // CB
DU_COLS = ((0, 4096), (4096, 6656), (6656, N_IN))


def _du_granule(j):
    return jnp.clip(j, 0, 7), jnp.clip(j - 8, 0, 4), jnp.clip(j - 13, 0, 3)


def _du_select(j, refs, fn):
    for ref, lo, hi in zip(refs, (0, 8, 13), (8, 13, 17)):
        @pl.when((j >= lo) & (j < hi))
        def _():
            fn(ref)


def _in_proj_bwd(du, w_full, x, g, dout, rider):
    t = du[0].shape[0]
    tn = min(256, t)

    def body(a0, a1, a2, w_hbm, x_ref, g_ref, do_ref, dx_ref, dg_ref, w_ref, sem):
        @pl.when(pl.program_id(0) == 0)
        def _():
            cp = pltpu.make_async_copy(w_hbm, w_ref, sem)
            cp.start()
            dg_ref[...] = jnp.zeros_like(dg_ref)
            cp.wait()
        acc = None
        for a_ref, (lo, hi) in zip((a0, a1, a2), DU_COLS):
            part = _dot_nt(w_ref[:, lo:hi], a_ref[...])
            acc = part if acc is None else acc + part
        dh = acc.T
        xv = x_ref[...]
        r = lax.rsqrt(jnp.mean(xv * xv, axis=-1, keepdims=True) + EPS)
        xhat = xv * r
        dg_ref[...] += _fold8(dh * xhat)
        dxh = dh * g_ref[...]
        dx_ref[...] = do_ref[...] + r * (dxh - xhat * jnp.mean(dxh * xhat, axis=-1, keepdims=True))

    row = pl.BlockSpec((tn, D), lambda i: (i, 0))
    return _call(
        body, rider, name="in_proj_bwd", grid=(t // tn,),
        in_specs=[pl.BlockSpec((tn, hi - lo), lambda i: (i, 0)) for lo, hi in DU_COLS]
        + [ANY, row, pl.BlockSpec((1, D), lambda i: (0, 0)), row],
        out_specs=[row, pl.BlockSpec((8, D), lambda i: (0, 0))], out_shape=[SDS((t, D), F32), SDS((8, D), F32)],
        scratch_shapes=[pltpu.VMEM((D, N_IN), BF16), pltpu.SemaphoreType.DMA(())],
        args=(*du, w_full, x, g, dout), vmem=VMEM_BIG)


def _in_proj_wgrad(ht, du, rider):
    t = ht.shape[1]
    tk = min(4096, t)
    n_k = t // tk

    def body(h_ref, b0, b1, b2, g_ref):
        j, k = pl.program_id(0), pl.program_id(1)

        if n_k > 1:
            @pl.when(k == 0)
            def _():
                g_ref[...] = jnp.zeros_like(g_ref)

        def add(b_ref):
            if n_k > 1:
                g_ref[...] += _dot(h_ref[...], b_ref[...])
            else:
                g_ref[...] = _dot(h_ref[...], b_ref[...])
        _du_select(j, (b0, b1, b2), add)

    seg = lambda q: pl.BlockSpec((tk, CB), lambda j, k: (k, _du_granule(j)[q]))
    (g,), got = _call(
        body, rider, name="in_proj_wgrad", grid=(N_GRAN, t // tk),
        in_specs=[pl.BlockSpec((D, tk), lambda j, k: (0, k)), seg(0), seg(1), seg(2)],
        out_specs=[pl.BlockSpec((D, CB), lambda j, k: (0, j))], out_shape=[SDS((D, N_IN), F32)],
        args=(ht, *du), vmem=VMEM_BIG)
    return g, got


def _swap_rider(g_in, g_sm):
    def copies(ins, outs, send, recv, base=0):
        x, y, c = _mesh_pos()
        cps = []
        for src, dst in zip(ins, outs):
            half = src.at[1 - c] if len(src.shape) == 3 else src.at[:, :, 1 - c]
            cps.append(_rcopy(half, dst, send, recv, base + len(cps), (x, y, 1 - c)))
        return cps

    arrays = [g for g in (g_in, g_sm) if g is not None]
    shapes = [SDS((512, N_IN), F32) if len(g.shape) == 3 else SDS((3, 4, 128, D), F32) for g in arrays]
    return _Rider(arrays, shapes, len(arrays), copies)


def _add_halves_in(cc_idx, g_in, r_in):
    def body(cc_ref, a_ref, b_ref, f_ref, h_ref):
        s = a_ref[...] + b_ref[...]
        h_ref[...] = s.astype(BF16)

        @pl.when(pl.program_id(1) == cc_ref[1])
        def _():
            f_ref[...] = s

    blk = pl.BlockSpec((128, SH_IN), lambda i, j, cc: (i, j))
    return pl.pallas_call(
        body, name="add_halves_in",
        grid_spec=pltpu.PrefetchScalarGridSpec(
            num_scalar_prefetch=1, grid=(4, 4),
            in_specs=[pl.BlockSpec((None, 128, SH_IN), lambda i, j, cc: (cc[0], i, j)), blk],
            out_specs=[pl.BlockSpec((128, SH_IN), lambda i, j, cc: (i, 0)), blk]),
        out_shape=[SDS((512, SH_IN), F32), SDS((512, N_IN), BF16)], compiler_params=_cp("arbitrary", "arbitrary"),
    )(cc_idx, g_in, r_in)


def _add_halves_sm(c_idx, g_sm, r_sm):
    def body(c_ref, a_ref, b_ref, f_ref, h_ref):
        s = a_ref[...] + b_ref[...]
        f_ref[...] = s
        h_ref[...] = s.astype(BF16)

    blk = pl.BlockSpec((1, 4, 128, D), lambda a, c: (a, 0, 0, 0))
    return pl.pallas_call(
        body, name="add_halves_sm",
        grid_spec=pltpu.PrefetchScalarGridSpec(
            num_scalar_prefetch=1, grid=(3,),
            in_specs=[pl.BlockSpec((1, 4, None, 128, D), lambda a, c: (a, 0, c[0], 0, 0)), blk], out_specs=[blk, blk]),
        out_shape=[SDS((3, 4, 128, D), F32), SDS((3, 4, 128, D), BF16)], compiler_params=_cp("parallel"),
    )(c_idx, g_sm, r_sm)


def _scatter_rider(h_in, h_sm):
    def copies(ins, outs, send, recv, base=0):
        x, y, c = _mesh_pos()
        cps = []
        for src, dst in zip(ins, outs):
            for k, chip in enumerate(_other_chips(x, y)):
                their = 2 * chip[0] + chip[1]
                part = src.at[:, pl.ds(pl.multiple_of(their * SH_IN, 128), SH_IN)] if len(src.shape) == 2 else src.at[:, their]
                cps.append(_rcopy(part, dst.at[k], send, recv, base + len(cps), (*chip, c)))
        return cps

    arrays = [h for h in (h_in, h_sm) if h is not None]
    shapes = [SDS((3, 512, SH_IN), BF16) if len(h.shape) == 2 else SDS((3, 3, 128, D), BF16) for h in arrays]
    return _Rider(arrays, shapes, 3 * len(arrays), copies)


def _ride_alone(rider, name):
    return _hosted_call(None, rider, name=name, grid=(), in_specs=[], out_specs=[], out_shape=[], args=())[1]


def _final_sum_in(f_in, r_in):
    def body(a_ref, r_ref, o_ref):
        o_ref[...] = a_ref[...] + r_ref[0].astype(F32) + r_ref[1].astype(F32) + r_ref[2].astype(F32)

    return pl.pallas_call(
        body, name="final_sum_in", grid=(4,),
        in_specs=[pl.BlockSpec((128, SH_IN), lambda i: (i, 0)), pl.BlockSpec((3, 128, SH_IN), lambda i: (0, i, 0))],
        out_specs=pl.BlockSpec((128, SH_IN), lambda i: (i, 0)),
        out_shape=SDS((512, SH_IN), F32), compiler_params=_cp("parallel"),
    )(f_in, r_in)


def _final_sum_sm(chip_idx, f_sm, r_sm):
    def body(j_ref, a_ref, r_ref, o_ref):
        o_ref[...] = a_ref[...] + r_ref[0].astype(F32) + r_ref[1].astype(F32) + r_ref[2].astype(F32)

    return pl.pallas_call(
        body, name="final_sum_sm",
        grid_spec=pltpu.PrefetchScalarGridSpec(
            num_scalar_prefetch=1, grid=(3,),
            in_specs=[pl.BlockSpec((1, None, 128, D), lambda a, j: (a, j[0], 0, 0)),
                      pl.BlockSpec((3, 1, 128, D), lambda a, j: (0, a, 0, 0))],
            out_specs=pl.BlockSpec((1, 128, D), lambda a, j: (a, 0, 0))),
        out_shape=SDS((3, 128, D), F32), compiler_params=_cp("parallel"),
    )(chip_idx, f_sm, r_sm)


def _join_halves(t_in, t_sm):
    n_cp = N_LAYERS * 4
    args, plan = [], []
    for l in range(N_LAYERS):
        if t_in[l] is not None:
            plan.append((l, 0, len(args)))
            args.append(t_in[l])
        plan += [(l, a, len(args)) for a in (1, 2, 3)]
        args.append(t_sm[l])

    def body(*refs):
        ins, outs = refs[:len(args)], refs[len(args):len(args) + 4]
        send, recv, loc_in, loc_out, stage_in, stage_sm = refs[len(args) + 4:]
        x, y, c = _mesh_pos()
        cps, own = [], []

        def place(l, a, half):
            rows = 512 if a == 0 else 128
            return outs[a].at[l, pl.ds(pl.multiple_of(half * rows, rows), rows), :]

        for s, (l, a, k) in enumerate(plan):
            src = ins[k] if a == 0 else ins[k].at[a - 1]
            own.append((src, place(l, a, c), min(a, 1)))
            cp = pltpu.make_async_remote_copy(src_ref=src, dst_ref=place(l, a, c), send_sem=send.at[s],
                                              recv_sem=recv.at[s], device_id=(x, y, 1 - c), device_id_type=MESH)
            cp.start()
            cps.append(cp)
        _staged_copies(own, (stage_in, stage_sm), loc_in, loc_out)
        for s, (l, a, k) in enumerate(plan):
            got = place(l, a, 1 - c)
            pltpu.make_async_remote_copy(src_ref=got, dst_ref=got, send_sem=send.at[s], recv_sem=recv.at[s],
                                         device_id=(x, y, 1 - c), device_id_type=MESH).wait_recv()
        for cp in cps:
            cp.wait_send()

    sm = SDS((N_LAYERS, SH_ROW, D), F32)
    return pl.pallas_call(
        body, name="join_halves", in_specs=[ANY] * len(args), out_specs=[ANY] * 4,
        out_shape=[SDS((N_LAYERS, D, SH_IN), F32), sm, sm, sm],
        scratch_shapes=[pltpu.SemaphoreType.DMA((n_cp,))] * 4
        + [pltpu.VMEM((2, 512, SH_IN), F32), pltpu.VMEM((2, 128, D), F32)],
        compiler_params=_cp(vmem=VMEM_BIG),
    )(*args)


def _adam_math(w, g, m, v):
    m = ADAM_B1 * m + (1.0 - ADAM_B1) * g
    v = ADAM_B2 * v + (1.0 - ADAM_B2) * (g * g)
    m_hat = m / (1.0 - ADAM_B1 ** ADAM_STEP)
    v_hat = v / (1.0 - ADAM_B2 ** ADAM_STEP)
    delta = -ADAM_LR * (m_hat / (jnp.sqrt(v_hat) + ADAM_EPS) + ADAM_WD * w)
    return delta, m, v


def _adamw_big(w, g, m, v, name):
    rows, cols = w.shape
    tr = 128

    def body(w_ref, g_ref, m_ref, v_ref, go_ref, d_ref, nm_ref, nv_ref):
        g = g_ref[...]
        go_ref[...] = g
        d_ref[...], nm_ref[...], nv_ref[...] = _adam_math(w_ref[...], g, m_ref[...], v_ref[...])

    blk = pl.BlockSpec((tr, cols), lambda i: (i, 0))
    return pl.pallas_call(
        body, name=name, grid=(rows // tr,), in_specs=[blk] * 4, out_specs=[blk] * 4,
        out_shape=[SDS((rows, cols), F32)] * 4, compiler_params=_cp("parallel", vmem=VMEM_BIG),
    )(w, g, m, v)


def _adamw_small(ws, gs, ms, vs):
    n = len(ws)

    def body(*refs):
        for k in range(n):
            w_ref, g_ref, m_ref, v_ref = (refs[q * n + k] for q in range(4))
            d, nm, nv = _adam_math(w_ref[...], g_ref[...], m_ref[...], v_ref[...])
            refs[4 * n + k][...] = d
            refs[5 * n + k][...] = nm
            refs[6 * n + k][...] = nv

    vm = pl.BlockSpec(memory_space=pltpu.VMEM)
    shapes = [SDS(w.shape, F32) for w in ws]
    res = pl.pallas_call(
        body, name="adamw_small", in_specs=[vm] * (4 * n), out_specs=[vm] * (3 * n), out_shape=shapes * 3,
    )(*ws, *gs, *ms, *vs)
    return res[:n], res[n:2 * n], res[2 * n:]


def _pad_rows(a, rows):
    flat = a.reshape(-1)
    return jnp.pad(flat, (0, rows * 128 - flat.shape[0])).reshape(rows, 128)


def kernel(x, norm_g, w_in, conv_w, q_norm_g, k_norm_g, sinks, w_conv_out, w_attn_out, gate_b, w_out, loss_target, m_norm_g, m_w_in, m_conv_w, m_q_norm_g, m_k_norm_g, m_sinks, m_w_conv_out, m_w_attn_out, m_gate_b, m_w_out, v_norm_g, v_w_in, v_conv_w, v_q_norm_g, v_k_norm_g, v_sinks, v_w_conv_out, v_w_attn_out, v_gate_b, v_w_out):
    xi, yi, ci = _mesh_pos()
    chip = 2 * xi + yi
    c_idx = jnp.reshape(ci, (1,)).astype(jnp.int32)
    chip_idx = jnp.reshape(chip, (1,)).astype(jnp.int32)
    cc_idx = jnp.stack([ci, chip]).astype(jnp.int32)
    t = x.shape[1]
    xs = [x.reshape(t, D)]
    tgt = loss_target.reshape(t, D)

    full_w = [[_cast_w_in(chip_idx, w_in, l), _cast_w_small(chip_idx, w_conv_out, w_attn_out, w_out, l)]
              for l in range(N_LAYERS)]
    conv32 = lax.dynamic_update_slice(jnp.zeros((32, D), F32), jnp.pad(conv_w.reshape(3 * N_LAYERS, SH_ROW), ((0, 20), (0, 0))),
                                      (0, chip * SH_ROW))
    qg_s = jnp.tile(q_norm_g, (1, N_Q)) * SCALE
    kg_t = jnp.tile(k_norm_g, (1, N_KV))
    bias = _band_bias()
    saved = []
    for l in range(N_LAYERS):
        nxt = full_w[l + 1] if l + 1 < N_LAYERS else None
        (h, ht), got = _rmsnorm_fwd(xs[l], norm_g[l:l + 1], _gather_rider([full_w[0][0], conv32], "N") if l == 0 else None)
        if l == 0:
            got = _ride_alone(_gather_rider(got, "F"), "gather_first_forward")
            full_w[0][0], conv32 = _ride_alone(_gather_rider(got, "B"), "gather_first_d2d")
            conv_full = conv32[:3 * N_LAYERS].reshape(N_LAYERS, 3, D)
        (u_conv, u_qkv, u_za, u_gl), got = _in_proj(h, full_w[l][0], _gather_rider(nxt, "N") if nxt else None)
        if nxt:
            nxt[0], nxt[1] = got
        (qs, kvx), got = _qkv_prep(u_qkv, qg_s[l:l + 1], kg_t[l:l + 1],
                                   _gather_rider(full_w[0][1:], "N") if l == 0 else None)
        y_c, got = _conv_fwd(u_conv, conv_full[l], _gather_rider(got, "F") if l == 0 else None)
        o, lse, got = _attn_fwd(qs, kvx, u_za, sinks[l:l + 1], bias, _merge_riders(
            _gather_rider(nxt, "F") if nxt else None, _gather_rider(got, "B") if l == 0 else None))
        if nxt:
            nxt[0], nxt[1] = got[:2]
        if l == 0:
            full_w[0][1] = got[-1]
        (x_next, y_a, y_b, merged), got = _out_proj_fwd(xs[l], y_c, o, u_gl, gate_b[l:l + 1], full_w[l][1],
                                                        _gather_rider(nxt, "B") if nxt else None)
        if nxt:
            nxt[0], nxt[1] = got
        xs.append(x_next)
        saved.append((ht, u_conv, u_qkv, u_za, u_gl, y_c, o, y_a, y_b, merged, qs, kvx, lse))

    dout, sq = _loss_head(xs[N_LAYERS], tgt)

    small, t_in, t_sm = [None] * N_LAYERS, [None] * N_LAYERS, [None] * N_LAYERS
    halves = None

    for l in reversed(range(N_LAYERS)):
        w_full, w_sm = full_w[l]
        last = l == 0
        ht, u_conv, u_qkv, u_za, u_gl, y_c, o, y_a, y_b, merged, qs, kvx, lse = saved[l]
        (d_ya, d_yb, du_gl, d_yc, d_o, dgb), _ = _out_proj_bwd(dout, y_a, y_b, u_gl, gate_b[l:l + 1], w_sm, None)
        g_sm = _small_wgrads(y_c, d_ya, o, d_yb, merged, dout).reshape(3, 4, 2, 128, D)
        (du_conv, dcw), got = _conv_bwd(d_yc, u_conv, conv_full[l], _merge_riders(
            _scatter_rider(None, halves[3]) if halves else None, _swap_rider(None, g_sm) if last else None))
        if halves:
            t_sm[l + 1] = _final_sum_sm(chip_idx, halves[2], got[0])
        if last:
            f_sm0, h_sm0 = _add_halves_sm(c_idx, g_sm, got[-1])
        (dqs, dkv, dza, dsk), got = _attn_bwd(d_o, qs, kvx, u_za, lse, sinks[l:l + 1], bias,
                                              _scatter_rider(halves[1], None) if halves else None)
        if halves:
            t_in[l + 1] = _final_sum_in(halves[0], got[0])
        dsk = jnp.sum(dsk[0].reshape(N_KV, 2, 2, BLK), axis=-1).transpose(0, 2, 1).reshape(N_Q)
        (du_attn, dqg, dkg), got = _qkv_post(u_qkv, dqs, dkv, dza, qg_s[l:l + 1], kg_t[l:l + 1],
                                             _scatter_rider(None, h_sm0) if last else None)
        if last:
            t_sm[0] = _final_sum_sm(chip_idx, f_sm0, got[0])
        du = (du_conv, du_attn, du_gl)
        g_in = _in_proj_wgrad(ht, du, None)[0].reshape(2, 512, N_IN)
        if last:
            f_in0, h_in0 = _add_halves_in(cc_idx, g_in, _ride_alone(_swap_rider(g_in, None), "swap_last")[0])
        (dout, dng), got = _in_proj_bwd(du, w_full, xs[l], norm_g[l:l + 1], dout,
                                        _scatter_rider(h_in0, None) if last else _swap_rider(g_in, g_sm))
        if last:
            t_in[0] = _final_sum_in(f_in0, got[0])
        else:
            halves = _add_halves_in(cc_idx, g_in, got[0]) + _add_halves_sm(c_idx, g_sm, got[1])
        small[l] = (jnp.sum(dng, axis=0), SCALE * jnp.sum(dqg.reshape(8 * N_Q, HEAD), axis=0),
                    jnp.sum(dkg.reshape(8 * N_KV, HEAD), axis=0), dsk, jnp.sum(dgb, axis=0), dcw[:3])
    grad_x = dout.reshape(1, t, D)

    stack = lambda k: jnp.stack([small[l][k] for l in range(N_LAYERS)])
    pack = jnp.concatenate([_pad_rows(stack(0), 32), _pad_rows(stack(1), 8), _pad_rows(stack(2), 8),
                            _pad_rows(stack(3), 8), _pad_rows(stack(4), 64), _pad_rows(stack(5), 96),
                            _pad_rows(jnp.sum(sq) * (0.5 / D), 8)], axis=0)
    red = _allreduce_small(pack)
    loss = red[216, 0]
    g_norm_g = red[0:32].reshape(N_LAYERS, D)
    g_q_norm_g = red[32:40].reshape(-1)[:N_LAYERS * HEAD].reshape(N_LAYERS, HEAD)
    g_k_norm_g = red[40:48].reshape(-1)[:N_LAYERS * HEAD].reshape(N_LAYERS, HEAD)
    g_sinks = red[48:56].reshape(-1)[:N_LAYERS * N_Q].reshape(N_LAYERS, N_Q)
    g_gate_b = red[56:120].reshape(N_LAYERS, 2 * D)
    g_conv_full = red[120:216].reshape(N_LAYERS, 3, D)
    g_conv_w = lax.dynamic_slice(g_conv_full, (0, 0, chip * SH_ROW), (N_LAYERS, 3, SH_ROW))

    g_w_in, g_w_co, g_w_ao, g_w_out = _join_halves(t_in, t_sm)

    r_in = N_LAYERS * D
    g_w_in, d_in, nm_in, nv_in = (a.reshape(N_LAYERS, D, SH_IN) for a in _adamw_big(
        w_in.reshape(r_in, SH_IN), g_w_in.reshape(r_in, SH_IN), m_w_in.reshape(r_in, SH_IN),
        v_w_in.reshape(r_in, SH_IN), "adamw_w_in"))
    r_sm = N_LAYERS * SH_ROW
    big = {}
    for nm, w, g, m, v in (("co", w_conv_out, g_w_co, m_w_conv_out, v_w_conv_out),
                           ("ao", w_attn_out, g_w_ao, m_w_attn_out, v_w_attn_out),
                           ("out", w_out, g_w_out, m_w_out, v_w_out)):
        big[nm] = tuple(a.reshape(N_LAYERS, SH_ROW, D) for a in _adamw_big(
            w.reshape(r_sm, D), g.reshape(r_sm, D), m.reshape(r_sm, D), v.reshape(r_sm, D), "adamw_w_small"))
    g_w_co, g_w_ao, g_w_out = big["co"][0], big["ao"][0], big["out"][0]
    sm_w = [norm_g, conv_w, q_norm_g, k_norm_g, sinks, gate_b]
    sm_g = [g_norm_g, g_conv_w, g_q_norm_g, g_k_norm_g, g_sinks, g_gate_b]
    sm_m = [m_norm_g, m_conv_w, m_q_norm_g, m_k_norm_g, m_sinks, m_gate_b]
    sm_v = [v_norm_g, v_conv_w, v_q_norm_g, v_k_norm_g, v_sinks, v_gate_b]
    sd, snm, snv = _adamw_small(sm_w, sm_g, sm_m, sm_v)

    def order(norm, w_in_, conv, qn, kn, sk, co, ao, gb, wo):
        return [norm, w_in_, conv, qn, kn, sk, co, ao, gb, wo]

    grads = order(g_norm_g, g_w_in, g_conv_w, g_q_norm_g, g_k_norm_g, g_sinks, g_w_co, g_w_ao, g_gate_b, g_w_out)
    deltas = order(sd[0], d_in, sd[1], sd[2], sd[3], sd[4], big["co"][1], big["ao"][1], sd[5], big["out"][1])
    new_m = order(snm[0], nm_in, snm[1], snm[2], snm[3], snm[4], big["co"][2], big["ao"][2], snm[5], big["out"][2])
    new_v = order(snv[0], nv_in, snv[1], snv[2], snv[3], snv[4], big["co"][3], big["ao"][3], snv[5], big["out"][3])
    return (loss, grad_x, *grads, *deltas, *new_m, *new_v)
```

```python
import functools

import jax
import jax.numpy as jnp
from jax import lax
from jax.experimental import pallas as pl
from jax.experimental.pallas import tpu as pltpu

F32, BF16 = jnp.float32, jnp.bfloat16
SDS = jax.ShapeDtypeStruct
MESH = pl.DeviceIdType.MESH
ANY = pl.BlockSpec(memory_space=pl.ANY)

D = 1024
N_IN = 8704
N_LAYERS = 4
N_Q, N_KV, HEAD = 16, 4, 64
GROUP = N_Q // N_KV
BLK = 128
EPS = 1e-6
NEG = -1e30
SCALE = HEAD ** -0.5
SH_IN = N_IN // 4
SH_ROW = D // 4
CB = 512
SEG_CONV, SEG_Q, SEG_KV, SEG_ZA, SEG_GL = (0, 8), (8, 2), (10, 1), (11, 2), (13, 4)
VMEM_BIG = 56 * 1024 * 1024

ADAM_LR, ADAM_B1, ADAM_B2, ADAM_EPS, ADAM_WD, ADAM_STEP = 0.001, 0.9, 0.999, 1e-08, 0.01, 10


def _cp(*sem, vmem=None):
    return pltpu.CompilerParams(dimension_semantics=sem if sem else None, vmem_limit_bytes=vmem)


def _sigmoid(z):
    return 1.0 / (1.0 + jnp.exp(-z))


def _dot(a, b):
    return jnp.dot(a, b, preferred_element_type=F32)


def _dot_nt(a, b):
    return lax.dot_general(a, b, (((1,), (1,)), ((), ())), preferred_element_type=F32)


def _dot_tn(a, b):
    return lax.dot_general(a, b, (((0,), (0,)), ((), ())), preferred_element_type=F32)


def _rms(xh):
    r = lax.rsqrt(jnp.mean(xh * xh, axis=-1, keepdims=True) + EPS)
    return xh * r, r


def _fold8(v):
    return jnp.sum(v.reshape(v.shape[0] // 8, 8, v.shape[1]), axis=0)


def _cast_w_in(chip_idx, w, layer):
    def body(j_ref, i_ref, o_ref):
        o_ref[...] = i_ref[...].astype(BF16)

    return pl.pallas_call(
        body, name="cast_w_in",
        grid_spec=pltpu.PrefetchScalarGridSpec(
            num_scalar_prefetch=1, grid=(2,),
            in_specs=[pl.BlockSpec((None, 512, SH_IN), lambda i, j: (layer, i, 0))],
            out_specs=pl.BlockSpec((512, SH_IN), lambda i, j: (i, j[0]))),
        out_shape=SDS((D, N_IN), BF16), compiler_params=_cp("parallel"),
    )(chip_idx, w)


def _cast_w_small(chip_idx, a, b, c, layer):
    def body(j_ref, a_ref, b_ref, c_ref, o_ref):
        o_ref[0] = a_ref[...].astype(BF16)
        o_ref[1] = b_ref[...].astype(BF16)
        o_ref[2] = c_ref[...].astype(BF16)

    spec = pl.BlockSpec((None, SH_ROW, D), lambda i, j: (layer, 0, 0))
    return pl.pallas_call(
        body, name="cast_w_small",
        grid_spec=pltpu.PrefetchScalarGridSpec(
            num_scalar_prefetch=1, grid=(1,), in_specs=[spec, spec, spec],
            out_specs=pl.BlockSpec((3, SH_ROW, D), lambda i, j: (0, j[0], 0))),
        out_shape=SDS((3, D, D), BF16), compiler_params=_cp("parallel"),
    )(chip_idx, a, b, c)


def _mesh_pos():
    return lax.axis_index("x"), lax.axis_index("y"), lax.axis_index("c")


def _other_chips(x, y):
    return [(1 - x, y), (x, 1 - y), (1 - x, 1 - y)]


class _Rider:
    def __init__(self, ins, out_shape, n, copies, aliases=()):
        self.ins, self.out_shape, self.n, self.copies, self.aliases = list(ins), list(out_shape), n, copies, aliases


def _merge_riders(*riders):
    riders = [r for r in riders if r is not None]
    if len(riders) < 2:
        return riders[0] if riders else None

    def copies(ins, outs, send, recv, base=0):
        cps, i0, o0 = [], 0, 0
        for r in riders:
            cps += r.copies(ins[i0:i0 + len(r.ins)], outs[o0:o0 + len(r.out_shape)], send, recv, base + len(cps))
            i0, o0 = i0 + len(r.ins), o0 + len(r.out_shape)
        return cps

    aliases, i0, o0 = [], 0, 0
    for r in riders:
        aliases += [(i0 + i, o0 + o) for i, o in r.aliases]
        i0, o0 = i0 + len(r.ins), o0 + len(r.out_shape)
    return _Rider(sum((r.ins for r in riders), []), sum((r.out_shape for r in riders), []),
                  sum(r.n for r in riders), copies, tuple(aliases))


def _rcopy(src, dst, send, recv, k, to):
    return pltpu.make_async_remote_copy(src_ref=src, dst_ref=dst, send_sem=send.at[k], recv_sem=recv.at[k],
                                        device_id=to, device_id_type=MESH)


def _hosted_call(body, rider, *, name, grid, in_specs, out_specs, out_shape, args, scratch_shapes=(), vmem=None):
    n_in, n_out, n_scr = len(in_specs), len(out_specs), len(scratch_shapes)
    r_in, r_out = len(rider.ins), len(rider.out_shape)

    def full_body(*refs):
        host_in, rid_in = refs[:n_in], refs[n_in:n_in + r_in]
        o0 = n_in + r_in
        host_out, rid_out = refs[o0:o0 + n_out], refs[o0 + n_out:o0 + n_out + r_out]
        s0 = o0 + n_out + r_out
        host_scr, (send, recv) = refs[s0:s0 + n_scr], refs[s0 + n_scr:]
        if body is None:
            cps = rider.copies(rid_in, rid_out, send, recv)
            for cp in cps:
                cp.start()
            for cp in cps:
                cp.wait()
            return
        ids = [pl.program_id(a) for a in range(len(grid))]
        first = functools.reduce(lambda p, q: p & q, [i == 0 for i in ids])
        last = functools.reduce(lambda p, q: p & q, [i == g - 1 for i, g in zip(ids, grid)])

        @pl.when(first)
        def _():
            for cp in rider.copies(rid_in, rid_out, send, recv):
                cp.start()

        body(*host_in, *host_out, *host_scr)

        @pl.when(last)
        def _():
            for cp in rider.copies(rid_in, rid_out, send, recv):
                cp.wait()

    res = pl.pallas_call(
        full_body, name=name, grid=grid if body is not None else (),
        in_specs=list(in_specs) + [ANY] * r_in, out_specs=list(out_specs) + [ANY] * r_out,
        out_shape=list(out_shape) + rider.out_shape,
        scratch_shapes=list(scratch_shapes) + [pltpu.SemaphoreType.DMA((rider.n,))] * 2,
        input_output_aliases={n_in + i: n_out + o for i, o in rider.aliases},
        compiler_params=_cp(*(("arbitrary",) * len(grid) if body is not None else ()), vmem=vmem),
    )(*args, *rider.ins)
    return res[:n_out], res[n_out:]


def _call(body, rider, **kw):
    if rider is not None:
        return _hosted_call(body, rider, **kw)
    res = pl.pallas_call(
        body, name=kw["name"], grid=kw["grid"], in_specs=list(kw["in_specs"]), out_specs=list(kw["out_specs"]),
        out_shape=list(kw["out_shape"]), scratch_shapes=list(kw.get("scratch_shapes", ())),
        compiler_params=_cp(*(("arbitrary",) * len(kw["grid"])), vmem=kw.get("vmem")),
    )(*kw["args"])
    return res, []


def _gather_rider(arrays, stage):
    def region(full, whose, c, sub):
        if len(full.shape) == 2:
            rows, cols = full.shape[0] // 2, full.shape[1] // 4
            first, n = (c * rows, rows) if sub is None else (c * rows + sub * (rows // 2), rows // 2)
            return full.at[pl.ds(pl.multiple_of(first, n), n), pl.ds(pl.multiple_of(whose * cols, 128), cols)]
        first, n = (whose * SH_ROW + c * 128, 128) if sub is None else (whose * SH_ROW + c * 128 + sub * 64, 64)
        return full.at[:, pl.ds(pl.multiple_of(first, n), n), :]

    def copies(ins, outs, send, recv, base=0):
        x, y, c = _mesh_pos()
        nbr_x, nbr_y = (1 - x, y), (x, 1 - y)
        cps = []
        for full in outs:
            plan = []
            if stage == "N":
                plan = [(region(full, 2 * x + y, c, None), (*nbr_x, c)), (region(full, 2 * x + y, c, None), (*nbr_y, c))]
            if stage in ("F", "FB1"):
                plan = [(region(full, 2 * nbr_x[0] + nbr_x[1], c, 0), (*nbr_y, c)),
                        (region(full, 2 * nbr_y[0] + nbr_y[1], c, 1), (*nbr_x, c))]
            if stage in ("B", "FB1", "B2"):
                chips = {"B": _other_chips(x, y), "FB1": [nbr_x, nbr_y], "B2": [(1 - x, 1 - y)]}[stage]
                plan += [(region(full, 2 * chip[0] + chip[1], c, None), (x, y, 1 - c)) for chip in chips]
            for reg, to in plan:
                cps.append(_rcopy(reg, reg, send, recv, base + len(cps), to))
        return cps

    per_array = {"N": 2, "F": 2, "B": 3, "FB1": 4, "B2": 1}[stage]
    return _Rider(arrays, [SDS(v.shape, v.dtype) for v in arrays], per_array * len(arrays), copies,
                  aliases=tuple((i, i) for i in range(len(arrays))))


def _staged_copies(copies, stages, sem_in, sem_out):
    busy, count = {}, {}
    for idx, (src, dst, kind) in enumerate(copies):
        slot = count.get(kind, 0) % 2
        count[kind] = count.get(kind, 0) + 1
        if (kind, slot) in busy:
            busy.pop((kind, slot)).wait()
        buf = stages[kind].at[slot]
        cin = pltpu.make_async_copy(src, buf, sem_in.at[idx])
        cin.start()
        cin.wait()
        cout = pltpu.make_async_copy(buf, dst, sem_out.at[idx])
        cout.start()
        busy[(kind, slot)] = cout
    for cp in busy.values():
        cp.wait()


def _allreduce_small(pack):
    rows = pack.shape[0]

    def body(p_ref, o_ref, buf, send, recv):
        x, y, c = _mesh_pos()
        me = 4 * x + 2 * y + c
        sends = []
        for r in range(1, 8):
            to = (x if not (r & 4) else 1 - x, y if not (r & 2) else 1 - y, c if not (r & 1) else 1 - c)
            cp = pltpu.make_async_remote_copy(src_ref=p_ref, dst_ref=buf.at[me], send_sem=send.at[r - 1],
                                              recv_sem=recv.at[r - 1], device_id=to, device_id_type=MESH)
            cp.start()
            sends.append(cp)
        buf[me] = p_ref[...]
        for r in range(1, 8):
            frm = (4 * x + 2 * y + c) ^ r
            pltpu.make_async_remote_copy(src_ref=p_ref, dst_ref=buf.at[frm], send_sem=send.at[r - 1],
                                         recv_sem=recv.at[r - 1], device_id=(x, y, c), device_id_type=MESH).wait_recv()
        acc = buf[0]
        for d in range(1, 8):
            acc = acc + buf[d]
        o_ref[...] = acc
        for cp in sends:
            cp.wait_send()

    vm = pl.BlockSpec(memory_space=pltpu.VMEM)
    return pl.pallas_call(
        body, name="allreduce_small", in_specs=[vm], out_specs=vm, out_shape=SDS((rows, 128), F32),
        scratch_shapes=[pltpu.VMEM((8, rows, 128), F32), pltpu.SemaphoreType.DMA((7,)), pltpu.SemaphoreType.DMA((7,))],
    )(pack)


def _rmsnorm_fwd(x, g, rider):
    t = x.shape[0]
    tm = min(512, t)

    def body(x_ref, g_ref, h_ref, ht_ref):
        xv = x_ref[...]
        r = lax.rsqrt(jnp.mean(xv * xv, axis=-1, keepdims=True) + EPS)
        h = xv * r * g_ref[...]
        h_ref[...] = h.astype(BF16)
        ht_ref[...] = h.T.astype(BF16)

    return _call(
        body, rider, name="rmsnorm_fwd", grid=(t // tm,),
        in_specs=[pl.BlockSpec((tm, D), lambda i: (i, 0)), pl.BlockSpec((1, D), lambda i: (0, 0))],
        out_specs=[pl.BlockSpec((tm, D), lambda i: (i, 0)), pl.BlockSpec((D, tm), lambda i: (0, i))],
        out_shape=[SDS((t, D), BF16), SDS((D, t), BF16)], args=(x, g), vmem=VMEM_BIG)


FWD_SEGS = ((0, 8), (8, 3), (11, 2), (13, 4))


def _in_proj(h, w_full, rider):
    t = h.shape[0]
    tm = min(2048, t)

    def body(a_ref, b_ref, *o_refs):
        j = pl.program_id(1)
        for o_ref, (off, nblk) in zip(o_refs, FWD_SEGS):
            @pl.when((j >= off) & (j < off + nblk))
            def _():
                o_ref[...] = _dot(a_ref[...], b_ref[...]).astype(BF16)

    def out(seg):
        off, nblk = seg
        return pl.BlockSpec((tm, CB), lambda i, j: (i, jnp.clip(j - off, 0, nblk - 1)))

    res, got = _call(
        body, rider, name="in_proj", grid=(t // tm, N_IN // CB),
        in_specs=[pl.BlockSpec((tm, D), lambda i, j: (i, 0)), pl.BlockSpec((D, CB), lambda i, j: (0, j))],
        out_specs=[out(s) for s in FWD_SEGS], out_shape=[SDS((t, s[1] * CB), BF16) for s in FWD_SEGS],
        args=(h, w_full), vmem=VMEM_BIG)
    return res, got


def _conv_fwd(u_conv, conv_w, rider):
    t = u_conv.shape[0]
    tm = min(256, t)
    hb = tm // 16

    def body(v_ref, b_ref, c_ref, z_ref, hv_ref, hc_ref, w_ref, y_ref):
        i = pl.program_id(0)
        cv = c_ref[...].astype(F32) * v_ref[...].astype(F32)
        halo = hc_ref[...].astype(F32) * hv_ref[...].astype(F32)
        halo = jnp.where(i > 0, halo, 0.0)
        row = lax.broadcasted_iota(jnp.int32, (tm, 1), 0)
        s1 = jnp.where(row == 0, halo[15:16], pltpu.roll(cv, 1, 0))
        s2 = jnp.where(row == 0, halo[14:15], jnp.where(row == 1, halo[15:16], pltpu.roll(cv, 2, 0)))
        conv = w_ref[0:1, :] * s2 + w_ref[1:2, :] * s1 + w_ref[2:3, :] * cv
        z = z_ref[...].astype(F32)
        y_ref[...] = (b_ref[...].astype(F32) * conv * (z * _sigmoid(z))).astype(BF16)

    def col(k):
        return pl.BlockSpec((tm, D), lambda i: (i, k))

    def halo(k):
        return pl.BlockSpec((16, D), lambda i: (jnp.maximum(i * hb - 1, 0), k))

    (y_c,), got = _call(
        body, rider, name="conv_fwd", grid=(t // tm,),
        in_specs=[col(0), col(1), col(2), col(3), halo(0), halo(2), pl.BlockSpec((3, D), lambda i: (0, 0))],
        out_specs=[pl.BlockSpec((tm, D), lambda i: (i, 0))], out_shape=[SDS((t, D), BF16)],
        args=(u_conv, u_conv, u_conv, u_conv, u_conv, u_conv, conv_w), vmem=VMEM_BIG)
    return y_c, got


KVX = 4 * N_KV * 128


def _iota2(shape):
    return lax.broadcasted_iota(jnp.int32, shape, 0), lax.broadcasted_iota(jnp.int32, shape, 1)


def _head_sum(v):
    r, c = _iota2((128, 128))
    ones = ((r >> 6) == (c >> 6)).astype(BF16)
    hi = v.astype(BF16)
    lo = (v - hi.astype(F32)).astype(BF16)
    return jnp.concatenate([_dot(hi[:, g:g + 128], ones) + _dot(lo[:, g:g + 128], ones)
                            for g in range(0, v.shape[1], 128)], axis=1)


def _expand_mats():
    r, c = _iota2((N_KV * HEAD, N_KV * 128))
    base = ((r >> 6) << 7) + (r & 63)
    return (c == base).astype(BF16), (c == base + 64).astype(BF16)


def _fold_mat():
    r, c = _iota2((N_KV * 128, N_KV * HEAD))
    return (((r >> 7) == (c >> 6)) & ((r & 63) == (c & 63))).astype(BF16)


def _qkv_prep(u_qkv, qg_s, kg_t, rider):
    t = u_qkv.shape[0]
    tm = min(512, t)

    def body(u_ref, qg_ref, kg_ref, qs_ref, kvx_ref):
        q = u_ref[:, 0:D].astype(F32)
        rq = lax.rsqrt(_head_sum(q * q) * (1.0 / HEAD) + EPS)
        qs_ref[...] = (q * rq * qg_ref[...]).astype(BF16)
        k = u_ref[:, D:D + 256].astype(F32)
        rk = lax.rsqrt(_head_sum(k * k) * (1.0 / HEAD) + EPS)
        kn = (k * rk * kg_ref[...]).astype(BF16)
        v = u_ref[:, D + 256:D + 512]
        e_lo, e_hi = _expand_mats()
        kvx_ref[:, 0:512] = _dot(kn, e_lo).astype(BF16)
        kvx_ref[:, 512:1024] = _dot(kn, e_hi).astype(BF16)
        kvx_ref[:, 1024:1536] = _dot(v, e_lo).astype(BF16)
        kvx_ref[:, 1536:2048] = _dot(v, e_hi).astype(BF16)

    return _call(
        body, rider, name="qkv_prep", grid=(t // tm,),
        in_specs=[pl.BlockSpec((tm, 1536), lambda i: (i, 0)), pl.BlockSpec((1, D), lambda i: (0, 0)),
                  pl.BlockSpec((1, 256), lambda i: (0, 0))],
        out_specs=[pl.BlockSpec((tm, D), lambda i: (i, 0)), pl.BlockSpec((tm, KVX), lambda i: (i, 0))],
        out_shape=[SDS((t, D), BF16), SDS((t, KVX), BF16)], args=(u_qkv, qg_s, kg_t), vmem=VMEM_BIG)


def _band_bias():
    j, r = _iota2((2 * BLK, 2 * BLK))
    diff = (r & (BLK - 1)) - j + BLK
    band = (diff >= 0) & (diff < BLK)
    return jnp.stack([jnp.where(band & (j >= BLK), 0.0, NEG), jnp.where(band, 0.0, NEG)]).astype(F32)


def _pair_rows(ref_or_val, hk):
    return jnp.concatenate([ref_or_val[:, 256 * hk:256 * hk + 128], ref_or_val[:, 256 * hk + 128:256 * hk + 256]], axis=0)


def _sink_row(sink_ref, hk, half):
    return jnp.concatenate([jnp.full((1, BLK), sink_ref[0, GROUP * hk + half], F32),
                            jnp.full((1, BLK), sink_ref[0, GROUP * hk + 2 + half], F32)], axis=1)


def _kv_operands(kvb, hk, half):
    return (kvb[:, 512 * half + 128 * hk:512 * half + 128 * hk + 128],
            kvb[:, 1024 + 512 * half + 128 * hk:1024 + 512 * half + 128 * hk + 128])


def _attn_fwd(qs, kvx, u_za, sinks, bias, rider):
    t = qs.shape[0]
    nb = t // BLK

    def body(q_ref, kc_ref, kp_ref, za_ref, sink_ref, bias_ref, o_ref, lse_ref):
        kvb = jnp.concatenate([kp_ref[...], kc_ref[...]], axis=0)
        bias_v = bias_ref[...]
        key0 = lax.broadcasted_iota(jnp.int32, (2 * BLK, 1), 0) == 0
        ones = jnp.ones((2 * BLK, 128), BF16)
        cols = []
        for hk in range(N_KV):
            qpp = _pair_rows(q_ref, hk)
            opp = None
            for half in range(2):
                kx, vx = _kv_operands(kvb, hk, half)
                s = _dot_nt(kx, qpp) + bias_v
                sink = _sink_row(sink_ref, hk, half)
                m = jnp.maximum(jnp.max(s, axis=0, keepdims=True), sink)
                p = jnp.exp(s - m)
                es = jnp.exp(sink - m)
                lse_ref[0, 2 * hk + half:2 * hk + half + 1, :] = m + jnp.log(jnp.sum(p, axis=0, keepdims=True) + es)
                pe = jnp.where(key0, es, p).astype(BF16)
                rhs = jnp.concatenate([jnp.where(key0, jnp.zeros_like(vx), vx), ones], axis=1)
                nd = _dot_tn(pe, rhs)
                o = nd[:, :128] * (1.0 / nd[:, 128:])
                opp = o if opp is None else opp + o
            cols += [opp[:BLK], opp[BLK:]]
        za = za_ref[...].astype(F32)
        o_ref[...] = (jnp.concatenate(cols, axis=1) * (za * _sigmoid(za))).astype(BF16)

    prev = lambda n: jnp.maximum(n - 1, 0)
    (o, lse), got = _call(
        body, rider, name="attn_fwd", grid=(nb,),
        in_specs=[pl.BlockSpec((BLK, D), lambda n: (n, 0)),
                  pl.BlockSpec((BLK, KVX), lambda n: (n, 0)), pl.BlockSpec((BLK, KVX), lambda n: (prev(n), 0)),
                  pl.BlockSpec((BLK, D), lambda n: (n, 0)), pl.BlockSpec(memory_space=pltpu.SMEM),
                  pl.BlockSpec((None, 2 * BLK, 2 * BLK), lambda n: (jnp.minimum(n, 1), 0, 0))],
        out_specs=[pl.BlockSpec((BLK, D), lambda n: (n, 0)), pl.BlockSpec((1, 8, 2 * BLK), lambda n: (n, 0, 0))],
        out_shape=[SDS((t, D), BF16), SDS((nb, 8, 2 * BLK), F32)],
        args=(qs, kvx, kvx, u_za, sinks, bias), vmem=VMEM_BIG)
    return o, lse, got


def _out_proj_fwd(x, y_c, o, u_gl, gate_b, w_sm, rider):
    t = x.shape[0]
    tm = min(512, t)

    def body(x_ref, yc_ref, o_ref, gla_ref, glb_ref, gb_ref, wco_ref, wao_ref, wout_ref,
             xn_ref, ya_ref, yb_ref, mg_ref):
        ya = _dot(yc_ref[...], wco_ref[...])
        yb = _dot(o_ref[...], wao_ref[...])
        gb = gb_ref[...]
        ga_ = _sigmoid(gla_ref[...].astype(F32) + gb[:, :D])
        gb_ = _sigmoid(glb_ref[...].astype(F32) + gb[:, D:])
        merged = (ga_ * ya + gb_ * yb).astype(BF16)
        ya_ref[...] = ya.astype(BF16)
        yb_ref[...] = yb.astype(BF16)
        mg_ref[...] = merged
        xn_ref[...] = x_ref[...] + _dot(merged, wout_ref[...])

    row = pl.BlockSpec((tm, D), lambda i: (i, 0))
    wspec = lambda a: pl.BlockSpec((None, D, D), lambda i: (a, 0, 0))
    return _call(
        body, rider, name="out_proj_fwd", grid=(t // tm,),
        in_specs=[row, row, row, pl.BlockSpec((tm, D), lambda i: (i, 0)), pl.BlockSpec((tm, D), lambda i: (i, 1)),
                  pl.BlockSpec((1, 2 * D), lambda i: (0, 0)), wspec(0), wspec(1), wspec(2)],
        out_specs=[row, row, row, row],
        out_shape=[SDS((t, D), F32), SDS((t, D), BF16), SDS((t, D), BF16), SDS((t, D), BF16)],
        args=(x, y_c, o, u_gl, u_gl, gate_b, w_sm, w_sm, w_sm), vmem=VMEM_BIG)


def _loss_head(y, tgt):
    t = y.shape[0]
    tm = min(512, t)

    def body(y_ref, t_ref, dy_ref, acc_ref):
        @pl.when(pl.program_id(0) == 0)
        def _():
            acc_ref[...] = jnp.zeros_like(acc_ref)
        err = y_ref[...] - t_ref[...]
        dy_ref[...] = err * (1.0 / D)
        sq = _fold8(err * err)
        tot = sq[:, 0:128]
        for k in range(1, D // 128):
            tot = tot + sq[:, 128 * k:128 * (k + 1)]
        acc_ref[...] += tot

    row = pl.BlockSpec((tm, D), lambda i: (i, 0))
    return pl.pallas_call(
        body, name="loss_head", grid=(t // tm,), in_specs=[row, row],
        out_specs=[row, pl.BlockSpec((8, 128), lambda i: (0, 0))],
        out_shape=[SDS((t, D), F32), SDS((8, 128), F32)], compiler_params=_cp("arbitrary"),
    )(y, tgt)


def _out_proj_bwd(dout, y_a, y_b, u_gl, gate_b, w_sm, rider):
    t = dout.shape[0]
    tm = min(512, t)

    def body(do_ref, ya_ref, yb_ref, gla_ref, glb_ref, gb_ref, wco_ref, wao_ref, wout_ref,
             dya_ref, dyb_ref, dgl_ref, dyc_ref, dob_ref, dgb_ref):
        @pl.when(pl.program_id(0) == 0)
        def _():
            dgb_ref[...] = jnp.zeros_like(dgb_ref)
        dm = _dot_nt(do_ref[...].astype(BF16), wout_ref[...])
        gb = gb_ref[...]
        ga_ = _sigmoid(gla_ref[...].astype(F32) + gb[:, :D])
        gb_ = _sigmoid(glb_ref[...].astype(F32) + gb[:, D:])
        dya = (ga_ * dm).astype(BF16)
        dyb = (gb_ * dm).astype(BF16)
        dgla = ya_ref[...].astype(F32) * dm * (ga_ * (1.0 - ga_))
        dglb = yb_ref[...].astype(F32) * dm * (gb_ * (1.0 - gb_))
        dya_ref[...] = dya
        dyb_ref[...] = dyb
        dgl_ref[:, :D] = dgla.astype(BF16)
        dgl_ref[:, D:] = dglb.astype(BF16)
        dgb_ref[:, :D] += _fold8(dgla)
        dgb_ref[:, D:] += _fold8(dglb)
        dyc_ref[...] = _dot_nt(dya, wco_ref[...]).astype(BF16)
        dob_ref[...] = _dot_nt(dyb, wao_ref[...]).astype(BF16)

    row = pl.BlockSpec((tm, D), lambda i: (i, 0))
    wspec = lambda a: pl.BlockSpec((None, D, D), lambda i: (a, 0, 0))
    return _call(
        body, rider, name="out_proj_bwd", grid=(t // tm,),
        in_specs=[row, row, row, pl.BlockSpec((tm, D), lambda i: (i, 0)), pl.BlockSpec((tm, D), lambda i: (i, 1)),
                  pl.BlockSpec((1, 2 * D), lambda i: (0, 0)), wspec(0), wspec(1), wspec(2)],
        out_specs=[row, row, pl.BlockSpec((tm, 2 * D), lambda i: (i, 0)), row, row,
                   pl.BlockSpec((8, 2 * D), lambda i: (0, 0))],
        out_shape=[SDS((t, D), BF16), SDS((t, D), BF16), SDS((t, 2 * D), BF16), SDS((t, D), BF16), SDS((t, D), BF16),
                   SDS((8, 2 * D), F32)],
        args=(dout, y_a, y_b, u_gl, u_gl, gate_b, w_sm, w_sm, w_sm), vmem=VMEM_BIG)


def _small_wgrads(y_c, d_ya, o, d_yb, merged, dout):
    t = y_c.shape[0]
    tk = min(512, t)

    def body(yc_ref, dya_ref, o_ref, dyb_ref, mg_ref, do_ref, g_ref):
        @pl.when(pl.program_id(0) == 0)
        def _():
            g_ref[...] = jnp.zeros_like(g_ref)
        g_ref[0] += _dot_tn(yc_ref[...], dya_ref[...])
        g_ref[1] += _dot_tn(o_ref[...], dyb_ref[...])
        g_ref[2] += _dot_tn(mg_ref[...], do_ref[...].astype(BF16))

    row = pl.BlockSpec((tk, D), lambda k: (k, 0))
    return pl.pallas_call(
        body, name="small_wgrads", grid=(t // tk,), in_specs=[row] * 6,
        out_specs=pl.BlockSpec((3, D, D), lambda k: (0, 0, 0)), out_shape=SDS((3, D, D), F32),
        compiler_params=_cp("arbitrary", vmem=VMEM_BIG),
    )(y_c, d_ya, o, d_yb, merged, dout)


def _conv_bwd(d_yc, u_conv, conv_w, rider):
    t = d_yc.shape[0]
    tm = min(256, t)
    hb = tm // 16
    last_halo = t // 16 - 1
    n_steps = t // tm

    def body(dy_ref, v_ref, b_ref, c_ref, z_ref, hv_ref, hc_ref, ndy_ref, nb_ref, nz_ref, w_ref, du_ref, dw_ref):
        i = pl.program_id(0)

        @pl.when(i == 0)
        def _():
            dw_ref[...] = jnp.zeros_like(dw_ref)
        v, c = v_ref[...].astype(F32), c_ref[...].astype(F32)
        b, z = b_ref[...].astype(F32), z_ref[...].astype(F32)
        cv = c * v
        halo = jnp.where(i > 0, hc_ref[...].astype(F32) * hv_ref[...].astype(F32), 0.0)
        row = lax.broadcasted_iota(jnp.int32, (tm, 1), 0)
        s1 = jnp.where(row == 0, halo[15:16], pltpu.roll(cv, 1, 0))
        s2 = jnp.where(row == 0, halo[14:15], jnp.where(row == 1, halo[15:16], pltpu.roll(cv, 2, 0)))
        w0, w1, w2 = w_ref[0:1, :], w_ref[1:2, :], w_ref[2:3, :]
        conv = w0 * s2 + w1 * s1 + w2 * cv
        sig = _sigmoid(z)
        sz = z * sig
        dsz = sig * (1.0 + z * (1.0 - sig))
        dy = dy_ref[...].astype(F32)
        dconv = dy * b * sz
        nz = nz_ref[...].astype(F32)
        nxt = ndy_ref[...].astype(F32) * nb_ref[...].astype(F32) * (nz * _sigmoid(nz))
        nxt = jnp.where(i < n_steps - 1, nxt, 0.0)
        a1 = jnp.where(row == tm - 1, nxt[0:1], pltpu.roll(dconv, tm - 1, 0))
        a2 = jnp.where(row == tm - 2, nxt[0:1], jnp.where(row == tm - 1, nxt[1:2], pltpu.roll(dconv, tm - 2, 0)))
        dcv = w2 * dconv + w1 * a1 + w0 * a2
        du_ref[:, 0:D] = (dcv * c).astype(BF16)
        du_ref[:, D:2 * D] = (dy * conv * sz).astype(BF16)
        du_ref[:, 2 * D:3 * D] = (dcv * v).astype(BF16)
        du_ref[:, 3 * D:4 * D] = (dy * b * conv * dsz).astype(BF16)
        r8 = lax.broadcasted_iota(jnp.int32, (8, 1), 0)
        dw_ref[...] += jnp.where(r8 == 0, jnp.sum(dconv * s2, axis=0, keepdims=True),
                                 jnp.where(r8 == 1, jnp.sum(dconv * s1, axis=0, keepdims=True),
                                           jnp.where(r8 == 2, jnp.sum(dconv * cv, axis=0, keepdims=True), 0.0)))

    def col(k):
        return pl.BlockSpec((tm, D), lambda i: (i, k))

    def halo(k):
        return pl.BlockSpec((16, D), lambda i: (jnp.maximum(i * hb - 1, 0), k))

    def nxt(k):
        return pl.BlockSpec((16, D), lambda i: (jnp.minimum((i + 1) * hb, last_halo), k))

    return _call(
        body, rider, name="conv_bwd", grid=(t // tm,),
        in_specs=[col(0), col(0), col(1), col(2), col(3), halo(0), halo(2), nxt(0), nxt(1), nxt(3),
                  pl.BlockSpec((3, D), lambda i: (0, 0))],
        out_specs=[pl.BlockSpec((tm, 4 * D), lambda i: (i, 0)), pl.BlockSpec((8, D), lambda i: (0, 0))],
        out_shape=[SDS((t, 4 * D), BF16), SDS((8, D), F32)],
        args=(d_yc, u_conv, u_conv, u_conv, u_conv, u_conv, u_conv, d_yc, u_conv, u_conv, conv_w), vmem=VMEM_BIG)


def _attn_bwd(d_o, qs, kvx, u_za, lse, sinks, bias, rider):
    t = d_o.shape[0]
    nb = t // BLK

    def body(q_ref, kc_ref, kp_ref, za_ref, do_ref, lse_ref, sink_ref, bias_ref,
             dq_ref, dkv_ref, dza_ref, dsk_ref, carry_ref):
        n = pl.program_id(0)

        @pl.when(n == 0)
        def _():
            carry_ref[...] = jnp.zeros_like(carry_ref)
            dsk_ref[...] = jnp.zeros_like(dsk_ref)

        live = n < nb
        kvb = jnp.concatenate([kp_ref[...], kc_ref[...]], axis=0)
        bias_v = bias_ref[...]
        za = za_ref[...].astype(F32)
        sig = _sigmoid(za)
        dsa = sig * (1.0 + za * (1.0 - sig))
        do = jnp.where(live, do_ref[...].astype(F32), 0.0)
        dattn = (do * (za * sig)).astype(BF16)
        lo_lanes = lax.broadcasted_iota(jnp.int32, (1, 128), 1) < HEAD
        dq_cols, attn_cols, dk_cols, dv_cols, dsk_rows = [], [], [], [], []
        for hk in range(N_KV):
            qpp = _pair_rows(q_ref, hk)
            dapp = _pair_rows(dattn, hk)
            probs, dss, xk, xv = [], [], [], []
            for half in range(2):
                kx, vx = _kv_operands(kvb, hk, half)
                lse = lse_ref[0, 2 * hk + half:2 * hk + half + 1, :]
                prob = jnp.exp(_dot_nt(kx, qpp) + bias_v - lse)
                psink = jnp.exp(_sink_row(sink_ref, hk, half) - lse)
                tdp = prob * _dot_nt(vx, dapp)
                drow = jnp.sum(tdp, axis=0, keepdims=True)
                ds = (tdp - prob * drow).astype(BF16)
                prob_b = prob.astype(BF16)
                xk.append(_dot(ds, qpp))
                xv.append(_dot(prob_b, dapp))
                probs.append(prob_b)
                dss.append(ds)
                dsk_rows.append(-psink * drow)
            kcat = jnp.concatenate([kvb[:, 128 * hk:128 * hk + 128], kvb[:, 512 + 128 * hk:512 + 128 * hk + 128]], axis=0)
            vcat = jnp.concatenate([kvb[:, 1024 + 128 * hk:1024 + 128 * hk + 128],
                                    kvb[:, 1536 + 128 * hk:1536 + 128 * hk + 128]], axis=0)
            app = _dot_tn(jnp.concatenate(probs, axis=0), vcat)
            dqpp = _dot_tn(jnp.concatenate(dss, axis=0), kcat)
            dq_cols += [dqpp[:BLK], dqpp[BLK:]]
            attn_cols += [app[:BLK], app[BLK:]]
            dk_cols.append(jnp.where(lo_lanes, xk[0], xk[1]))
            dv_cols.append(jnp.where(lo_lanes, xv[0], xv[1]))

        @pl.when(live)
        def _():
            dq_ref[...] = jnp.concatenate(dq_cols, axis=1).astype(BF16)
            dza_ref[...] = (do * jnp.concatenate(attn_cols, axis=1) * dsa).astype(BF16)

        band = jnp.concatenate(dk_cols + dv_cols, axis=1)
        dkv_ref[...] = (band[:BLK] + carry_ref[...]).astype(BF16)
        carry_ref[...] = band[BLK:]
        dsk_ref[...] += jnp.broadcast_to(jnp.concatenate(dsk_rows, axis=1), (8, 2 * N_KV * 2 * BLK))

    cur = lambda n: jnp.minimum(n, nb - 1)
    prev = lambda n: jnp.maximum(n - 1, 0)
    return _call(
        body, rider, name="attn_bwd", grid=(nb + 1,),
        in_specs=[pl.BlockSpec((BLK, D), lambda n: (cur(n), 0)),
                  pl.BlockSpec((BLK, KVX), lambda n: (cur(n), 0)), pl.BlockSpec((BLK, KVX), lambda n: (prev(n), 0)),
                  pl.BlockSpec((BLK, D), lambda n: (cur(n), 0)), pl.BlockSpec((BLK, D), lambda n: (cur(n), 0)),
                  pl.BlockSpec((1, 8, 2 * BLK), lambda n: (cur(n), 0, 0)), pl.BlockSpec(memory_space=pltpu.SMEM),
                  pl.BlockSpec((None, 2 * BLK, 2 * BLK), lambda n: (jnp.minimum(n, 1), 0, 0))],
        out_specs=[pl.BlockSpec((BLK, D), lambda n: (cur(n), 0)), pl.BlockSpec((BLK, D), lambda n: (prev(n), 0)),
                   pl.BlockSpec((BLK, D), lambda n: (cur(n), 0)), pl.BlockSpec((8, 2 * D), lambda n: (0, 0))],
        out_shape=[SDS((t, D), BF16), SDS((t, D), BF16), SDS((t, D), BF16), SDS((8, 2 * D), F32)],
        scratch_shapes=[pltpu.VMEM((BLK, D), F32)],
        args=(qs, kvx, kvx, u_za, d_o, lse, sinks, bias), vmem=VMEM_BIG)


def _qkv_post(u_qkv, dqs, dkv, dza, qg_s, kg_t, rider):
    t = u_qkv.shape[0]
    tm = min(512, t)

    def norm_bwd(x, dy, g):
        r = lax.rsqrt(_head_sum(x * x) * (1.0 / HEAD) + EPS)
        xhat = x * r
        dxh = dy * g
        return r * (dxh - xhat * (_head_sum(dxh * xhat) * (1.0 / HEAD))), _fold8(dy * xhat)

    def body(u_ref, dq_ref, dkv_ref, dza_ref, qg_ref, kg_ref, du_ref, dqg_ref, dkg_ref):
        @pl.when(pl.program_id(0) == 0)
        def _():
            dqg_ref[...] = jnp.zeros_like(dqg_ref)
            dkg_ref[...] = jnp.zeros_like(dkg_ref)
        dq, gq = norm_bwd(u_ref[:, 0:D].astype(F32), dq_ref[...].astype(F32), qg_ref[...])
        fold = _fold_mat()
        dk, gk = norm_bwd(u_ref[:, D:D + 256].astype(F32), _dot(dkv_ref[:, 0:512], fold), kg_ref[...])
        du_ref[:, 0:D] = dq.astype(BF16)
        du_ref[:, D:D + 256] = dk.astype(BF16)
        du_ref[:, D + 256:D + 512] = _dot(dkv_ref[:, 512:1024], fold).astype(BF16)
        du_ref[:, D + 512:2 * D + 512] = dza_ref[...]
        dqg_ref[...] += gq
        dkg_ref[...] += gk

    row = pl.BlockSpec((tm, D), lambda i: (i, 0))
    return _call(
        body, rider, name="qkv_post", grid=(t // tm,),
        in_specs=[pl.BlockSpec((tm, 1536), lambda i: (i, 0)), row, row, row,
                  pl.BlockSpec((1, D), lambda i: (0, 0)), pl.BlockSpec((1, 256), lambda i: (0, 0))],
        out_specs=[pl.BlockSpec((tm, 2560), lambda i: (i, 0)), pl.BlockSpec((8, D), lambda i: (0, 0)),
                   pl.BlockSpec((8, 256), lambda i: (0, 0))],
        out_shape=[SDS((t, 2560), BF16), SDS((8, D), F32), SDS((8, 256), F32)],
        args=(u_qkv, dqs, dkv, dza, qg_s, kg_t), vmem=VMEM_BIG)


N_GRAN = N_IN // CB
DU_COLS = ((0, 4096), (4096, 6656), (6656, N_IN))


def _du_granule(j):
    return jnp.clip(j, 0, 7), jnp.clip(j - 8, 0, 4), jnp.clip(j - 13, 0, 3)


def _du_select(j, refs, fn):
    for ref, lo, hi in zip(refs, (0, 8, 13), (8, 13, 17)):
        @pl.when((j >= lo) & (j < hi))
        def _():
            fn(ref)


def _in_proj_bwd(du, w_full, x, g, dout, rider):
    t = du[0].shape[0]
    tn = min(256, t)

    def body(a0, a1, a2, w_hbm, x_ref, g_ref, do_ref, dx_ref, dg_ref, w_ref, sem):
        @pl.when(pl.program_id(0) == 0)
        def _():
            cp = pltpu.make_async_copy(w_hbm, w_ref, sem)
            cp.start()
            dg_ref[...] = jnp.zeros_like(dg_ref)
            cp.wait()
        acc = None
        for a_ref, (lo, hi) in zip((a0, a1, a2), DU_COLS):
            part = _dot_nt(w_ref[:, lo:hi], a_ref[...])
            acc = part if acc is None else acc + part
        dh = acc.T
        xv = x_ref[...]
        r = lax.rsqrt(jnp.mean(xv * xv, axis=-1, keepdims=True) + EPS)
        xhat = xv * r
        dg_ref[...] += _fold8(dh * xhat)
        dxh = dh * g_ref[...]
        dx_ref[...] = do_ref[...] + r * (dxh - xhat * jnp.mean(dxh * xhat, axis=-1, keepdims=True))

    row = pl.BlockSpec((tn, D), lambda i: (i, 0))
    return _call(
        body, rider, name="in_proj_bwd", grid=(t // tn,),
        in_specs=[pl.BlockSpec((tn, hi - lo), lambda i: (i, 0)) for lo, hi in DU_COLS]
        + [ANY, row, pl.BlockSpec((1, D), lambda i: (0, 0)), row],
        out_specs=[row, pl.BlockSpec((8, D), lambda i: (0, 0))], out_shape=[SDS((t, D), F32), SDS((8, D), F32)],
        scratch_shapes=[pltpu.VMEM((D, N_IN), BF16), pltpu.SemaphoreType.DMA(())],
        args=(*du, w_full, x, g, dout), vmem=VMEM_BIG)


def _in_proj_wgrad(ht, du, rider):
    t = ht.shape[1]
    tk = min(4096, t)
    n_k = t // tk

    def body(h_ref, b0, b1, b2, g_ref):
        j, k = pl.program_id(0), pl.program_id(1)

        if n_k > 1:
            @pl.when(k == 0)
            def _():
                g_ref[...] = jnp.zeros_like(g_ref)

        def add(b_ref):
            if n_k > 1:
                g_ref[...] += _dot(h_ref[...], b_ref[...])
            else:
                g_ref[...] = _dot(h_ref[...], b_ref[...])
        _du_select(j, (b0, b1, b2), add)

    seg = lambda q: pl.BlockSpec((tk, CB), lambda j, k: (k, _du_granule(j)[q]))
    (g,), got = _call(
        body, rider, name="in_proj_wgrad", grid=(N_GRAN, t // tk),
        in_specs=[pl.BlockSpec((D, tk), lambda j, k: (0, k)), seg(0), seg(1), seg(2)],
        out_specs=[pl.BlockSpec((D, CB), lambda j, k: (0, j))], out_shape=[SDS((D, N_IN), F32)],
        args=(ht, *du), vmem=VMEM_BIG)
    return g, got


def _swap_rider(g_in, g_sm):
    def copies(ins, outs, send, recv, base=0):
        x, y, c = _mesh_pos()
        cps = []
        for src, dst in zip(ins, outs):
            half = src.at[1 - c] if len(src.shape) == 3 else src.at[:, :, 1 - c]
            cps.append(_rcopy(half, dst, send, recv, base + len(cps), (x, y, 1 - c)))
        return cps

    arrays = [g for g in (g_in, g_sm) if g is not None]
    shapes = [SDS((512, N_IN), F32) if len(g.shape) == 3 else SDS((3, 4, 128, D), F32) for g in arrays]
    return _Rider(arrays, shapes, len(arrays), copies)


def _add_halves_in(cc_idx, g_in, r_in):
    def body(cc_ref, a_ref, b_ref, f_ref, h_ref):
        s = a_ref[...] + b_ref[...]
        h_ref[...] = s.astype(BF16)

        @pl.when(pl.program_id(1) == cc_ref[1])
        def _():
            f_ref[...] = s

    blk = pl.BlockSpec((128, SH_IN), lambda i, j, cc: (i, j))
    return pl.pallas_call(
        body, name="add_halves_in",
        grid_spec=pltpu.PrefetchScalarGridSpec(
            num_scalar_prefetch=1, grid=(4, 4),
            in_specs=[pl.BlockSpec((None, 128, SH_IN), lambda i, j, cc: (cc[0], i, j)), blk],
            out_specs=[pl.BlockSpec((128, SH_IN), lambda i, j, cc: (i, 0)), blk]),
        out_shape=[SDS((512, SH_IN), F32), SDS((512, N_IN), BF16)], compiler_params=_cp("arbitrary", "arbitrary"),
    )(cc_idx, g_in, r_in)


def _add_halves_sm(c_idx, g_sm, r_sm):
    def body(c_ref, a_ref, b_ref, f_ref, h_ref):
        s = a_ref[...] + b_ref[...]
        f_ref[...] = s
        h_ref[...] = s.astype(BF16)

    blk = pl.BlockSpec((1, 4, 128, D), lambda a, c: (a, 0, 0, 0))
    return pl.pallas_call(
        body, name="add_halves_sm",
        grid_spec=pltpu.PrefetchScalarGridSpec(
            num_scalar_prefetch=1, grid=(3,),
            in_specs=[pl.BlockSpec((1, 4, None, 128, D), lambda a, c: (a, 0, c[0], 0, 0)), blk], out_specs=[blk, blk]),
        out_shape=[SDS((3, 4, 128, D), F32), SDS((3, 4, 128, D), BF16)], compiler_params=_cp("parallel"),
    )(c_idx, g_sm, r_sm)


def _scatter_rider(h_in, h_sm):
    def copies(ins, outs, send, recv, base=0):
        x, y, c = _mesh_pos()
        cps = []
        for src, dst in zip(ins, outs):
            for k, chip in enumerate(_other_chips(x, y)):
                their = 2 * chip[0] + chip[1]
                part = src.at[:, pl.ds(pl.multiple_of(their * SH_IN, 128), SH_IN)] if len(src.shape) == 2 else src.at[:, their]
                cps.append(_rcopy(part, dst.at[k], send, recv, base + len(cps), (*chip, c)))
        return cps

    arrays = [h for h in (h_in, h_sm) if h is not None]
    shapes = [SDS((3, 512, SH_IN), BF16) if len(h.shape) == 2 else SDS((3, 3, 128, D), BF16) for h in arrays]
    return _Rider(arrays, shapes, 3 * len(arrays), copies)


def _ride_alone(rider, name):
    return _hosted_call(None, rider, name=name, grid=(), in_specs=[], out_specs=[], out_shape=[], args=())[1]


def _final_sum_in(f_in, r_in):
    def body(a_ref, r_ref, o_ref):
        o_ref[...] = a_ref[...] + r_ref[0].astype(F32) + r_ref[1].astype(F32) + r_ref[2].astype(F32)

    return pl.pallas_call(
        body, name="final_sum_in", grid=(4,),
        in_specs=[pl.BlockSpec((128, SH_IN), lambda i: (i, 0)), pl.BlockSpec((3, 128, SH_IN), lambda i: (0, i, 0))],
        out_specs=pl.BlockSpec((128, SH_IN), lambda i: (i, 0)),
        out_shape=SDS((512, SH_IN), F32), compiler_params=_cp("parallel"),
    )(f_in, r_in)


def _final_sum_sm(chip_idx, f_sm, r_sm):
    def body(j_ref, a_ref, r_ref, o_ref):
        o_ref[...] = a_ref[...] + r_ref[0].astype(F32) + r_ref[1].astype(F32) + r_ref[2].astype(F32)

    return pl.pallas_call(
        body, name="final_sum_sm",
        grid_spec=pltpu.PrefetchScalarGridSpec(
            num_scalar_prefetch=1, grid=(3,),
            in_specs=[pl.BlockSpec((1, None, 128, D), lambda a, j: (a, j[0], 0, 0)),
                      pl.BlockSpec((3, 1, 128, D), lambda a, j: (0, a, 0, 0))],
            out_specs=pl.BlockSpec((1, 128, D), lambda a, j: (a, 0, 0))),
        out_shape=SDS((3, 128, D), F32), compiler_params=_cp("parallel"),
    )(chip_idx, f_sm, r_sm)


def _join_halves(t_in, t_sm):
    n_cp = N_LAYERS * 4
    args, plan = [], []
    for l in range(N_LAYERS):
        if t_in[l] is not None:
            plan.append((l, 0, len(args)))
            args.append(t_in[l])
        plan += [(l, a, len(args)) for a in (1, 2, 3)]
        args.append(t_sm[l])

    def body(*refs):
        ins, outs = refs[:len(args)], refs[len(args):len(args) + 4]
        send, recv, loc_in, loc_out, stage_in, stage_sm = refs[len(args) + 4:]
        x, y, c = _mesh_pos()
        cps, own = [], []

        def place(l, a, half):
            rows = 512 if a == 0 else 128
            return outs[a].at[l, pl.ds(pl.multiple_of(half * rows, rows), rows), :]

        for s, (l, a, k) in enumerate(plan):
            src = ins[k] if a == 0 else ins[k].at[a - 1]
            own.append((src, place(l, a, c), min(a, 1)))
            cp = pltpu.make_async_remote_copy(src_ref=src, dst_ref=place(l, a, c), send_sem=send.at[s],
                                              recv_sem=recv.at[s], device_id=(x, y, 1 - c), device_id_type=MESH)
            cp.start()
            cps.append(cp)
        _staged_copies(own, (stage_in, stage_sm), loc_in, loc_out)
        for s, (l, a, k) in enumerate(plan):
            got = place(l, a, 1 - c)
            pltpu.make_async_remote_copy(src_ref=got, dst_ref=got, send_sem=send.at[s], recv_sem=recv.at[s],
                                         device_id=(x, y, 1 - c), device_id_type=MESH).wait_recv()
        for cp in cps:
            cp.wait_send()

    sm = SDS((N_LAYERS, SH_ROW, D), F32)
    return pl.pallas_call(
        body, name="join_halves", in_specs=[ANY] * len(args), out_specs=[ANY] * 4,
        out_shape=[SDS((N_LAYERS, D, SH_IN), F32), sm, sm, sm],
        scratch_shapes=[pltpu.SemaphoreType.DMA((n_cp,))] * 4
        + [pltpu.VMEM((2, 512, SH_IN), F32), pltpu.VMEM((2, 128, D), F32)],
        compiler_params=_cp(vmem=VMEM_BIG),
    )(*args)


def _adam_math(w, g, m, v):
    m = ADAM_B1 * m + (1.0 - ADAM_B1) * g
    v = ADAM_B2 * v + (1.0 - ADAM_B2) * (g * g)
    m_hat = m / (1.0 - ADAM_B1 ** ADAM_STEP)
    v_hat = v / (1.0 - ADAM_B2 ** ADAM_STEP)
    delta = -ADAM_LR * (m_hat / (jnp.sqrt(v_hat) + ADAM_EPS) + ADAM_WD * w)
    return delta, m, v


def _adamw_big(w, g, m, v, name):
    rows, cols = w.shape
    tr = 128

    def body(w_ref, g_ref, m_ref, v_ref, go_ref, d_ref, nm_ref, nv_ref):
        g = g_ref[...]
        go_ref[...] = g
        d_ref[...], nm_ref[...], nv_ref[...] = _adam_math(w_ref[...], g, m_ref[...], v_ref[...])

    blk = pl.BlockSpec((tr, cols), lambda i: (i, 0))
    return pl.pallas_call(
        body, name=name, grid=(rows // tr,), in_specs=[blk] * 4, out_specs=[blk] * 4,
        out_shape=[SDS((rows, cols), F32)] * 4, compiler_params=_cp("parallel", vmem=VMEM_BIG),
    )(w, g, m, v)


def _adamw_small(ws, gs, ms, vs):
    n = len(ws)

    def body(*refs):
        for k in range(n):
            w_ref, g_ref, m_ref, v_ref = (refs[q * n + k] for q in range(4))
            d, nm, nv = _adam_math(w_ref[...], g_ref[...], m_ref[...], v_ref[...])
            refs[4 * n + k][...] = d
            refs[5 * n + k][...] = nm
            refs[6 * n + k][...] = nv

    vm = pl.BlockSpec(memory_space=pltpu.VMEM)
    shapes = [SDS(w.shape, F32) for w in ws]
    res = pl.pallas_call(
        body, name="adamw_small", in_specs=[vm] * (4 * n), out_specs=[vm] * (3 * n), out_shape=shapes * 3,
    )(*ws, *gs, *ms, *vs)
    return res[:n], res[n:2 * n], res[2 * n:]


def _pad_rows(a, rows):
    flat = a.reshape(-1)
    return jnp.pad(flat, (0, rows * 128 - flat.shape[0])).reshape(rows, 128)


def kernel(x, norm_g, w_in, conv_w, q_norm_g, k_norm_g, sinks, w_conv_out, w_attn_out, gate_b, w_out, loss_target, m_norm_g, m_w_in, m_conv_w, m_q_norm_g, m_k_norm_g, m_sinks, m_w_conv_out, m_w_attn_out, m_gate_b, m_w_out, v_norm_g, v_w_in, v_conv_w, v_q_norm_g, v_k_norm_g, v_sinks, v_w_conv_out, v_w_attn_out, v_gate_b, v_w_out):
    xi, yi, ci = _mesh_pos()
    chip = 2 * xi + yi
    c_idx = jnp.reshape(ci, (1,)).astype(jnp.int32)
    chip_idx = jnp.reshape(chip, (1,)).astype(jnp.int32)
    cc_idx = jnp.stack([ci, chip]).astype(jnp.int32)
    t = x.shape[1]
    xs = [x.reshape(t, D)]
    tgt = loss_target.reshape(t, D)

    full_w = [[_cast_w_in(chip_idx, w_in, l), _cast_w_small(chip_idx, w_conv_out, w_attn_out, w_out, l)]
              for l in range(N_LAYERS)]
    conv32 = lax.dynamic_update_slice(jnp.zeros((32, D), F32), jnp.pad(conv_w.reshape(3 * N_LAYERS, SH_ROW), ((0, 20), (0, 0))),
                                      (0, chip * SH_ROW))
    qg_s = jnp.tile(q_norm_g, (1, N_Q)) * SCALE
    kg_t = jnp.tile(k_norm_g, (1, N_KV))
    bias = _band_bias()
    saved = []
    for l in range(N_LAYERS):
        nxt = full_w[l + 1] if l + 1 < N_LAYERS else None
        (h, ht), got = _rmsnorm_fwd(xs[l], norm_g[l:l + 1], _gather_rider([full_w[0][0], conv32], "N") if l == 0 else None)
        if l == 0:
            got = _ride_alone(_gather_rider(got, "F"), "gather_first_forward")
            full_w[0][0], conv32 = _ride_alone(_gather_rider(got, "B"), "gather_first_d2d")
            conv_full = conv32[:3 * N_LAYERS].reshape(N_LAYERS, 3, D)
        (u_conv, u_qkv, u_za, u_gl), got = _in_proj(h, full_w[l][0], _gather_rider(nxt, "N") if nxt else None)
        if nxt:
            nxt[0], nxt[1] = got
        (qs, kvx), got = _qkv_prep(u_qkv, qg_s[l:l + 1], kg_t[l:l + 1],
                                   _gather_rider(full_w[0][1:], "N") if l == 0 else None)
        y_c, got = _conv_fwd(u_conv, conv_full[l], _gather_rider(got, "F") if l == 0 else None)
        o, lse, got = _attn_fwd(qs, kvx, u_za, sinks[l:l + 1], bias, _merge_riders(
            _gather_rider(nxt, "FB1") if nxt else None, _gather_rider(got, "B") if l == 0 else None))
        if nxt:
            nxt[0], nxt[1] = got[:2]
        if l == 0:
            full_w[0][1] = got[-1]
        (x_next, y_a, y_b, merged), got = _out_proj_fwd(xs[l], y_c, o, u_gl, gate_b[l:l + 1], full_w[l][1],
                                                        _gather_rider(nxt, "B2") if nxt else None)
        if nxt:
            nxt[0], nxt[1] = got
        xs.append(x_next)
        saved.append((ht, u_conv, u_qkv, u_za, u_gl, y_c, o, y_a, y_b, merged, qs, kvx, lse))

    dout, sq = _loss_head(xs[N_LAYERS], tgt)

    small, t_in, t_sm = [None] * N_LAYERS, [None] * N_LAYERS, [None] * N_LAYERS
    halves = None

    for l in reversed(range(N_LAYERS)):
        w_full, w_sm = full_w[l]
        last = l == 0
        ht, u_conv, u_qkv, u_za, u_gl, y_c, o, y_a, y_b, merged, qs, kvx, lse = saved[l]
        (d_ya, d_yb, du_gl, d_yc, d_o, dgb), _ = _out_proj_bwd(dout, y_a, y_b, u_gl, gate_b[l:l + 1], w_sm, None)
        g_sm = _small_wgrads(y_c, d_ya, o, d_yb, merged, dout).reshape(3, 4, 2, 128, D)
        (du_conv, dcw), got = _conv_bwd(d_yc, u_conv, conv_full[l], _merge_riders(
            _scatter_rider(None, halves[3]) if halves else None, _swap_rider(None, g_sm) if last else None))
        if halves:
            t_sm[l + 1] = _final_sum_sm(chip_idx, halves[2], got[0])
        if last:
            f_sm0, h_sm0 = _add_halves_sm(c_idx, g_sm, got[-1])
        (dqs, dkv, dza, dsk), got = _attn_bwd(d_o, qs, kvx, u_za, lse, sinks[l:l + 1], bias,
                                              _scatter_rider(halves[1], None) if halves else None)
        if halves:
            t_in[l + 1] = _final_sum_in(halves[0], got[0])
        dsk = jnp.sum(dsk[0].reshape(N_KV, 2, 2, BLK), axis=-1).transpose(0, 2, 1).reshape(N_Q)
        (du_attn, dqg, dkg), _ = _qkv_post(u_qkv, dqs, dkv, dza, qg_s[l:l + 1], kg_t[l:l + 1], None)
        du = (du_conv, du_attn, du_gl)
        g_in, got = _in_proj_wgrad(ht, du, _scatter_rider(None, h_sm0) if last else None)
        g_in = g_in.reshape(2, 512, N_IN)
        if last:
            t_sm[0] = _final_sum_sm(chip_idx, f_sm0, got[0])
        if last:
            f_in0, h_in0 = _add_halves_in(cc_idx, g_in, _ride_alone(_swap_rider(g_in, None), "swap_last")[0])
        (dout, dng), got = _in_proj_bwd(du, w_full, xs[l], norm_g[l:l + 1], dout,
                                        _scatter_rider(h_in0, None) if last else _swap_rider(g_in, g_sm))
        if last:
            t_in[0] = _final_sum_in(f_in0, got[0])
        else:
            halves = _add_halves_in(cc_idx, g_in, got[0]) + _add_halves_sm(c_idx, g_sm, got[1])
        small[l] = (jnp.sum(dng, axis=0), SCALE * jnp.sum(dqg.reshape(8 * N_Q, HEAD), axis=0),
                    jnp.sum(dkg.reshape(8 * N_KV, HEAD), axis=0), dsk, jnp.sum(dgb, axis=0), dcw[:3])
    grad_x = dout.reshape(1, t, D)

    stack = lambda k: jnp.stack([small[l][k] for l in range(N_LAYERS)])
    pack = jnp.concatenate([_pad_rows(stack(0), 32), _pad_rows(stack(1), 8), _pad_rows(stack(2), 8),
                            _pad_rows(stack(3), 8), _pad_rows(stack(4), 64), _pad_rows(stack(5), 96),
                            _pad_rows(jnp.sum(sq) * (0.5 / D), 8)], axis=0)
    red = _allreduce_small(pack)
    loss = red[216, 0]
    g_norm_g = red[0:32].reshape(N_LAYERS, D)
    g_q_norm_g = red[32:40].reshape(-1)[:N_LAYERS * HEAD].reshape(N_LAYERS, HEAD)
    g_k_norm_g = red[40:48].reshape(-1)[:N_LAYERS * HEAD].reshape(N_LAYERS, HEAD)
    g_sinks = red[48:56].reshape(-1)[:N_LAYERS * N_Q].reshape(N_LAYERS, N_Q)
    g_gate_b = red[56:120].reshape(N_LAYERS, 2 * D)
    g_conv_full = red[120:216].reshape(N_LAYERS, 3, D)
    g_conv_w = lax.dynamic_slice(g_conv_full, (0, 0, chip * SH_ROW), (N_LAYERS, 3, SH_ROW))

    g_w_in, g_w_co, g_w_ao, g_w_out = _join_halves(t_in, t_sm)

    r_in = N_LAYERS * D
    g_w_in, d_in, nm_in, nv_in = (a.reshape(N_LAYERS, D, SH_IN) for a in _adamw_big(
        w_in.reshape(r_in, SH_IN), g_w_in.reshape(r_in, SH_IN), m_w_in.reshape(r_in, SH_IN),
        v_w_in.reshape(r_in, SH_IN), "adamw_w_in"))
    r_sm = N_LAYERS * SH_ROW
    big = {}
    for nm, w, g, m, v in (("co", w_conv_out, g_w_co, m_w_conv_out, v_w_conv_out),
                           ("ao", w_attn_out, g_w_ao, m_w_attn_out, v_w_attn_out),
                           ("out", w_out, g_w_out, m_w_out, v_w_out)):
        big[nm] = tuple(a.reshape(N_LAYERS, SH_ROW, D) for a in _adamw_big(
            w.reshape(r_sm, D), g.reshape(r_sm, D), m.reshape(r_sm, D), v.reshape(r_sm, D), "adamw_w_small"))
    g_w_co, g_w_ao, g_w_out = big["co"][0], big["ao"][0], big["out"][0]
    sm_w = [norm_g, conv_w, q_norm_g, k_norm_g, sinks, gate_b]
    sm_g = [g_norm_g, g_conv_w, g_q_norm_g, g_k_norm_g, g_sinks, g_gate_b]
    sm_m = [m_norm_g, m_conv_w, m_q_norm_g, m_k_norm_g, m_sinks, m_gate_b]
    sm_v = [v_norm_g, v_conv_w, v_q_norm_g, v_k_norm_g, v_sinks, v_gate_b]
    sd, snm, snv = _adamw_small(sm_w, sm_g, sm_m, sm_v)

    def order(norm, w_in_, conv, qn, kn, sk, co, ao, gb, wo):
        return [norm, w_in_, conv, qn, kn, sk, co, ao, gb, wo]

    grads = order(g_norm_g, g_w_in, g_conv_w, g_q_norm_g, g_k_norm_g, g_sinks, g_w_co, g_w_ao, g_gate_b, g_w_out)
    deltas = order(sd[0], d_in, sd[1], sd[2], sd[3], sd[4], big["co"][1], big["ao"][1], sd[5], big["out"][1])
    new_m = order(snm[0], nm_in, snm[1], snm[2], snm[3], snm[4], big["co"][2], big["ao"][2], snm[5], big["out"][2])
    new_v = order(snv[0], nv_in, snv[1], snv[2], snv[3], snv[4], big["co"][3], big["ao"][3], snv[5], big["out"][3])
    return (loss, grad_x, *grads, *deltas, *new_m, *new_v)
```

```python
import functools

import jax
import jax.numpy as jnp
from jax import lax
from jax.experimental import pallas as pl
from jax.experimental.pallas import tpu as pltpu

F32, BF16 = jnp.float32, jnp.bfloat16
SDS = jax.ShapeDtypeStruct
MESH = pl.DeviceIdType.MESH
ANY = pl.BlockSpec(memory_space=pl.ANY)

D = 1024
N_IN = 8704
N_LAYERS = 4
N_Q, N_KV, HEAD = 16, 4, 64
GROUP = N_Q // N_KV
BLK = 128
EPS = 1e-6
NEG = -1e30
SCALE = HEAD ** -0.5
SH_IN = N_IN // 4
SH_ROW = D // 4
CB = 512
SEG_CONV, SEG_Q, SEG_KV, SEG_ZA, SEG_GL = (0, 8), (8, 2), (10, 1), (11, 2), (13, 4)
VMEM_BIG = 56 * 1024 * 1024

ADAM_LR, ADAM_B1, ADAM_B2, ADAM_EPS, ADAM_WD, ADAM_STEP = 0.001, 0.9, 0.999, 1e-08, 0.01, 10


def _cp(*sem, vmem=None):
    return pltpu.CompilerParams(dimension_semantics=sem if sem else None, vmem_limit_bytes=vmem)


def _sigmoid(z):
    return 1.0 / (1.0 + jnp.exp(-z))


def _dot(a, b):
    return jnp.dot(a, b, preferred_element_type=F32)


def _dot_nt(a, b):
    return lax.dot_general(a, b, (((1,), (1,)), ((), ())), preferred_element_type=F32)


def _dot_tn(a, b):
    return lax.dot_general(a, b, (((0,), (0,)), ((), ())), preferred_element_type=F32)


def _rms(xh):
    r = lax.rsqrt(jnp.mean(xh * xh, axis=-1, keepdims=True) + EPS)
    return xh * r, r


def _fold8(v):
    return jnp.sum(v.reshape(v.shape[0] // 8, 8, v.shape[1]), axis=0)


def _cast_w_in(chip_idx, w, layer):
    def body(j_ref, i_ref, o_ref):
        o_ref[...] = i_ref[...].astype(BF16)

    return pl.pallas_call(
        body, name="cast_w_in",
        grid_spec=pltpu.PrefetchScalarGridSpec(
            num_scalar_prefetch=1, grid=(2,),
            in_specs=[pl.BlockSpec((None, 512, SH_IN), lambda i, j: (layer, i, 0))],
            out_specs=pl.BlockSpec((512, SH_IN), lambda i, j: (i, j[0]))),
        out_shape=SDS((D, N_IN), BF16), compiler_params=_cp("parallel"),
    )(chip_idx, w)


def _cast_w_small(chip_idx, a, b, c, layer):
    def body(j_ref, a_ref, b_ref, c_ref, o_ref):
        o_ref[0] = a_ref[...].astype(BF16)
        o_ref[1] = b_ref[...].astype(BF16)
        o_ref[2] = c_ref[...].astype(BF16)

    spec = pl.BlockSpec((None, SH_ROW, D), lambda i, j: (layer, 0, 0))
    return pl.pallas_call(
        body, name="cast_w_small",
        grid_spec=pltpu.PrefetchScalarGridSpec(
            num_scalar_prefetch=1, grid=(1,), in_specs=[spec, spec, spec],
            out_specs=pl.BlockSpec((3, SH_ROW, D), lambda i, j: (0, j[0], 0))),
        out_shape=SDS((3, D, D), BF16), compiler_params=_cp("parallel"),
    )(chip_idx, a, b, c)


def _mesh_pos():
    return lax.axis_index("x"), lax.axis_index("y"), lax.axis_index("c")


def _other_chips(x, y):
    return [(1 - x, y), (x, 1 - y), (1 - x, 1 - y)]


class _Rider:
    def __init__(self, ins, out_shape, n, copies, aliases=()):
        self.ins, self.out_shape, self.n, self.copies, self.aliases = list(ins), list(out_shape), n, copies, aliases


def _merge_riders(*riders):
    riders = [r for r in riders if r is not None]
    if len(riders) < 2:
        return riders[0] if riders else None

    def copies(ins, outs, send, recv, base=0):
        cps, i0, o0 = [], 0, 0
        for r in riders:
            cps += r.copies(ins[i0:i0 + len(r.ins)], outs[o0:o0 + len(r.out_shape)], send, recv, base + len(cps))
            i0, o0 = i0 + len(r.ins), o0 + len(r.out_shape)
        return cps

    aliases, i0, o0 = [], 0, 0
    for r in riders:
        aliases += [(i0 + i, o0 + o) for i, o in r.aliases]
        i0, o0 = i0 + len(r.ins), o0 + len(r.out_shape)
    return _Rider(sum((r.ins for r in riders), []), sum((r.out_shape for r in riders), []),
                  sum(r.n for r in riders), copies, tuple(aliases))


def _rcopy(src, dst, send, recv, k, to):
    return pltpu.make_async_remote_copy(src_ref=src, dst_ref=dst, send_sem=send.at[k], recv_sem=recv.at[k],
                                        device_id=to, device_id_type=MESH)


def _hosted_call(body, rider, *, name, grid, in_specs, out_specs, out_shape, args, scratch_shapes=(), vmem=None):
    n_in, n_out, n_scr = len(in_specs), len(out_specs), len(scratch_shapes)
    r_in, r_out = len(rider.ins), len(rider.out_shape)

    def full_body(*refs):
        host_in, rid_in = refs[:n_in], refs[n_in:n_in + r_in]
        o0 = n_in + r_in
        host_out, rid_out = refs[o0:o0 + n_out], refs[o0 + n_out:o0 + n_out + r_out]
        s0 = o0 + n_out + r_out
        host_scr, (send, recv) = refs[s0:s0 + n_scr], refs[s0 + n_scr:]
        if body is None:
            cps = rider.copies(rid_in, rid_out, send, recv)
            for cp in cps:
                cp.start()
            for cp in cps:
                cp.wait()
            return
        ids = [pl.program_id(a) for a in range(len(grid))]
        first = functools.reduce(lambda p, q: p & q, [i == 0 for i in ids])
        last = functools.reduce(lambda p, q: p & q, [i == g - 1 for i, g in zip(ids, grid)])

        @pl.when(first)
        def _():
            for cp in rider.copies(rid_in, rid_out, send, recv):
                cp.start()

        body(*host_in, *host_out, *host_scr)

        @pl.when(last)
        def _():
            for cp in rider.copies(rid_in, rid_out, send, recv):
                cp.wait()

    res = pl.pallas_call(
        full_body, name=name, grid=grid if body is not None else (),
        in_specs=list(in_specs) + [ANY] * r_in, out_specs=list(out_specs) + [ANY] * r_out,
        out_shape=list(out_shape) + rider.out_shape,
        scratch_shapes=list(scratch_shapes) + [pltpu.SemaphoreType.DMA((rider.n,))] * 2,
        input_output_aliases={n_in + i: n_out + o for i, o in rider.aliases},
        compiler_params=_cp(*(("arbitrary",) * len(grid) if body is not None else ()), vmem=vmem),
    )(*args, *rider.ins)
    return res[:n_out], res[n_out:]


def _call(body, rider, **kw):
    if rider is not None:
        return _hosted_call(body, rider, **kw)
    res = pl.pallas_call(
        body, name=kw["name"], grid=kw["grid"], in_specs=list(kw["in_specs"]), out_specs=list(kw["out_specs"]),
        out_shape=list(kw["out_shape"]), scratch_shapes=list(kw.get("scratch_shapes", ())),
        compiler_params=_cp(*(("arbitrary",) * len(kw["grid"])), vmem=kw.get("vmem")),
    )(*kw["args"])
    return res, []


def _gather_rider(arrays, stage):
    def region(full, whose, c, sub):
        if len(full.shape) == 2:
            rows, cols = full.shape[0] // 2, full.shape[1] // 4
            first, n = (c * rows, rows) if sub is None else (c * rows + sub * (rows // 2), rows // 2)
            return full.at[pl.ds(pl.multiple_of(first, n), n), pl.ds(pl.multiple_of(whose * cols, 128), cols)]
        first, n = (whose * SH_ROW + c * 128, 128) if sub is None else (whose * SH_ROW + c * 128 + sub * 64, 64)
        return full.at[:, pl.ds(pl.multiple_of(first, n), n), :]

    def copies(ins, outs, send, recv, base=0):
        x, y, c = _mesh_pos()
        nbr_x, nbr_y = (1 - x, y), (x, 1 - y)
        cps = []
        for full in outs:
            plan = []
            if stage == "N":
                plan = [(region(full, 2 * x + y, c, None), (*nbr_x, c)), (region(full, 2 * x + y, c, None), (*nbr_y, c))]
            if stage in ("F", "FB1"):
                plan = [(region(full, 2 * nbr_x[0] + nbr_x[1], c, 0), (*nbr_y, c)),
                        (region(full, 2 * nbr_y[0] + nbr_y[1], c, 1), (*nbr_x, c))]
            if stage in ("B", "FB1", "B2"):
                chips = {"B": _other_chips(x, y), "FB1": [nbr_x, nbr_y], "B2": [(1 - x, 1 - y)]}[stage]
                plan += [(region(full, 2 * chip[0] + chip[1], c, None), (x, y, 1 - c)) for chip in chips]
            for reg, to in plan:
                cps.append(_rcopy(reg, reg, send, recv, base + len(cps), to))
        return cps

    per_array = {"N": 2, "F": 2, "B": 3, "FB1": 4, "B2": 1}[stage]
    return _Rider(arrays, [SDS(v.shape, v.dtype) for v in arrays], per_array * len(arrays), copies,
                  aliases=tuple((i, i) for i in range(len(arrays))))


def _staged_copies(copies, stages, sem_in, sem_out):
    busy, count = {}, {}
    for idx, (src, dst, kind) in enumerate(copies):
        slot = count.get(kind, 0) % 2
        count[kind] = count.get(kind, 0) + 1
        if (kind, slot) in busy:
            busy.pop((kind, slot)).wait()
        buf = stages[kind].at[slot]
        cin = pltpu.make_async_copy(src, buf, sem_in.at[idx])
        cin.start()
        cin.wait()
        cout = pltpu.make_async_copy(buf, dst, sem_out.at[idx])
        cout.start()
        busy[(kind, slot)] = cout
    for cp in busy.values():
        cp.wait()


def _allreduce_small(pack):
    rows = pack.shape[0]

    def body(p_ref, o_ref, buf, send, recv):
        x, y, c = _mesh_pos()
        me = 4 * x + 2 * y + c
        sends = []
        for r in range(1, 8):
            to = (x if not (r & 4) else 1 - x, y if not (r & 2) else 1 - y, c if not (r & 1) else 1 - c)
            cp = pltpu.make_async_remote_copy(src_ref=p_ref, dst_ref=buf.at[me], send_sem=send.at[r - 1],
                                              recv_sem=recv.at[r - 1], device_id=to, device_id_type=MESH)
            cp.start()
            sends.append(cp)
        buf[me] = p_ref[...]
        for r in range(1, 8):
            frm = (4 * x + 2 * y + c) ^ r
            pltpu.make_async_remote_copy(src_ref=p_ref, dst_ref=buf.at[frm], send_sem=send.at[r - 1],
                                         recv_sem=recv.at[r - 1], device_id=(x, y, c), device_id_type=MESH).wait_recv()
        acc = buf[0]
        for d in range(1, 8):
            acc = acc + buf[d]
        o_ref[...] = acc
        for cp in sends:
            cp.wait_send()

    vm = pl.BlockSpec(memory_space=pltpu.VMEM)
    return pl.pallas_call(
        body, name="allreduce_small", in_specs=[vm], out_specs=vm, out_shape=SDS((rows, 128), F32),
        scratch_shapes=[pltpu.VMEM((8, rows, 128), F32), pltpu.SemaphoreType.DMA((7,)), pltpu.SemaphoreType.DMA((7,))],
    )(pack)


def _rmsnorm_fwd(x, g, rider):
    t = x.shape[0]
    tm = min(512, t)

    def body(x_ref, g_ref, h_ref, ht_ref):
        xv = x_ref[...]
        r = lax.rsqrt(jnp.mean(xv * xv, axis=-1, keepdims=True) + EPS)
        h = xv * r * g_ref[...]
        h_ref[...] = h.astype(BF16)
        ht_ref[...] = h.T.astype(BF16)

    return _call(
        body, rider, name="rmsnorm_fwd", grid=(t // tm,),
        in_specs=[pl.BlockSpec((tm, D), lambda i: (i, 0)), pl.BlockSpec((1, D), lambda i: (0, 0))],
        out_specs=[pl.BlockSpec((tm, D), lambda i: (i, 0)), pl.BlockSpec((D, tm), lambda i: (0, i))],
        out_shape=[SDS((t, D), BF16), SDS((D, t), BF16)], args=(x, g), vmem=VMEM_BIG)


FWD_SEGS = ((0, 8), (8, 3), (11, 2), (13, 4))


def _in_proj(h, w_full, rider):
    t = h.shape[0]
    tm = min(2048, t)

    def body(a_ref, b_ref, *o_refs):
        j = pl.program_id(1)
        for o_ref, (off, nblk) in zip(o_refs, FWD_SEGS):
            @pl.when((j >= off) & (j < off + nblk))
            def _():
                o_ref[...] = _dot(a_ref[...], b_ref[...]).astype(BF16)

    def out(seg):
        off, nblk = seg
        return pl.BlockSpec((tm, CB), lambda i, j: (i, jnp.clip(j - off, 0, nblk - 1)))

    res, got = _call(
        body, rider, name="in_proj", grid=(t // tm, N_IN // CB),
        in_specs=[pl.BlockSpec((tm, D), lambda i, j: (i, 0)), pl.BlockSpec((D, CB), lambda i, j: (0, j))],
        out_specs=[out(s) for s in FWD_SEGS], out_shape=[SDS((t, s[1] * CB), BF16) for s in FWD_SEGS],
        args=(h, w_full), vmem=VMEM_BIG)
    return res, got


def _conv_fwd(u_conv, conv_w, rider):
    t = u_conv.shape[0]
    tm = min(512, t)
    hb = tm // 16

    def body(v_ref, b_ref, c_ref, z_ref, hv_ref, hc_ref, w_ref, y_ref):
        i = pl.program_id(0)
        cv = c_ref[...].astype(F32) * v_ref[...].astype(F32)
        halo = hc_ref[...].astype(F32) * hv_ref[...].astype(F32)
        halo = jnp.where(i > 0, halo, 0.0)
        row = lax.broadcasted_iota(jnp.int32, (tm, 1), 0)
        s1 = jnp.where(row == 0, halo[15:16], pltpu.roll(cv, 1, 0))
        s2 = jnp.where(row == 0, halo[14:15], jnp.where(row == 1, halo[15:16], pltpu.roll(cv, 2, 0)))
        conv = w_ref[0:1, :] * s2 + w_ref[1:2, :] * s1 + w_ref[2:3, :] * cv
        z = z_ref[...].astype(F32)
        y_ref[...] = (b_ref[...].astype(F32) * conv * (z * _sigmoid(z))).astype(BF16)

    def col(k):
        return pl.BlockSpec((tm, D), lambda i: (i, k))

    def halo(k):
        return pl.BlockSpec((16, D), lambda i: (jnp.maximum(i * hb - 1, 0), k))

    (y_c,), got = _call(
        body, rider, name="conv_fwd", grid=(t // tm,),
        in_specs=[col(0), col(1), col(2), col(3), halo(0), halo(2), pl.BlockSpec((3, D), lambda i: (0, 0))],
        out_specs=[pl.BlockSpec((tm, D), lambda i: (i, 0))], out_shape=[SDS((t, D), BF16)],
        args=(u_conv, u_conv, u_conv, u_conv, u_conv, u_conv, conv_w), vmem=VMEM_BIG)
    return y_c, got


KVX = 4 * N_KV * 128


def _iota2(shape):
    return lax.broadcasted_iota(jnp.int32, shape, 0), lax.broadcasted_iota(jnp.int32, shape, 1)


def _head_sum(v):
    r, c = _iota2((128, 128))
    ones = ((r >> 6) == (c >> 6)).astype(BF16)
    hi = v.astype(BF16)
    lo = (v - hi.astype(F32)).astype(BF16)
    return jnp.concatenate([_dot(hi[:, g:g + 128], ones) + _dot(lo[:, g:g + 128], ones)
                            for g in range(0, v.shape[1], 128)], axis=1)


def _expand_mats():
    r, c = _iota2((N_KV * HEAD, N_KV * 128))
    base = ((r >> 6) << 7) + (r & 63)
    return (c == base).astype(BF16), (c == base + 64).astype(BF16)


def _fold_mat():
    r, c = _iota2((N_KV * 128, N_KV * HEAD))
    return (((r >> 7) == (c >> 6)) & ((r & 63) == (c & 63))).astype(BF16)


def _qkv_prep(u_qkv, qg_s, kg_t, rider):
    t = u_qkv.shape[0]
    tm = min(512, t)

    def body(u_ref, qg_ref, kg_ref, qs_ref, kvx_ref):
        q = u_ref[:, 0:D].astype(F32)
        rq = lax.rsqrt(_head_sum(q * q) * (1.0 / HEAD) + EPS)
        qs_ref[...] = (q * rq * qg_ref[...]).astype(BF16)
        k = u_ref[:, D:D + 256].astype(F32)
        rk = lax.rsqrt(_head_sum(k * k) * (1.0 / HEAD) + EPS)
        kn = (k * rk * kg_ref[...]).astype(BF16)
        v = u_ref[:, D + 256:D + 512]
        e_lo, e_hi = _expand_mats()
        kvx_ref[:, 0:512] = _dot(kn, e_lo).astype(BF16)
        kvx_ref[:, 512:1024] = _dot(kn, e_hi).astype(BF16)
        kvx_ref[:, 1024:1536] = _dot(v, e_lo).astype(BF16)
        kvx_ref[:, 1536:2048] = _dot(v, e_hi).astype(BF16)

    return _call(
        body, rider, name="qkv_prep", grid=(t // tm,),
        in_specs=[pl.BlockSpec((tm, 1536), lambda i: (i, 0)), pl.BlockSpec((1, D), lambda i: (0, 0)),
                  pl.BlockSpec((1, 256), lambda i: (0, 0))],
        out_specs=[pl.BlockSpec((tm, D), lambda i: (i, 0)), pl.BlockSpec((tm, KVX), lambda i: (i, 0))],
        out_shape=[SDS((t, D), BF16), SDS((t, KVX), BF16)], args=(u_qkv, qg_s, kg_t), vmem=VMEM_BIG)


def _band_bias():
    j, r = _iota2((2 * BLK, 2 * BLK))
    diff = (r & (BLK - 1)) - j + BLK
    band = (diff >= 0) & (diff < BLK)
    return jnp.stack([jnp.where(band & (j >= BLK), 0.0, NEG), jnp.where(band, 0.0, NEG)]).astype(F32)


def _pair_rows(ref_or_val, hk):
    return jnp.concatenate([ref_or_val[:, 256 * hk:256 * hk + 128], ref_or_val[:, 256 * hk + 128:256 * hk + 256]], axis=0)


def _sink_row(sink_ref, hk, half):
    return jnp.concatenate([jnp.full((1, BLK), sink_ref[0, GROUP * hk + half], F32),
                            jnp.full((1, BLK), sink_ref[0, GROUP * hk + 2 + half], F32)], axis=1)


def _kv_operands(kvb, hk, half):
    return (kvb[:, 512 * half + 128 * hk:512 * half + 128 * hk + 128],
            kvb[:, 1024 + 512 * half + 128 * hk:1024 + 512 * half + 128 * hk + 128])


def _attn_fwd(qs, kvx, u_za, sinks, bias, rider):
    t = qs.shape[0]
    nb = t // BLK

    def body(q_ref, kc_ref, kp_ref, za_ref, sink_ref, bias_ref, o_ref, lse_ref):
        kvb = jnp.concatenate([kp_ref[...], kc_ref[...]], axis=0)
        bias_v = bias_ref[...]
        key0 = lax.broadcasted_iota(jnp.int32, (2 * BLK, 1), 0) == 0
        ones = jnp.ones((2 * BLK, 128), BF16)
        cols = []
        for hk in range(N_KV):
            qpp = _pair_rows(q_ref, hk)
            opp = None
            for half in range(2):
                kx, vx = _kv_operands(kvb, hk, half)
                s = _dot_nt(kx, qpp) + bias_v
                sink = _sink_row(sink_ref, hk, half)
                m = jnp.maximum(jnp.max(s, axis=0, keepdims=True), sink)
                p = jnp.exp(s - m)
                es = jnp.exp(sink - m)
                lse_ref[0, 2 * hk + half:2 * hk + half + 1, :] = m + jnp.log(jnp.sum(p, axis=0, keepdims=True) + es)
                pe = jnp.where(key0, es, p).astype(BF16)
                rhs = jnp.concatenate([jnp.where(key0, jnp.zeros_like(vx), vx), ones], axis=1)
                nd = _dot_tn(pe, rhs)
                o = nd[:, :128] * (1.0 / nd[:, 128:])
                opp = o if opp is None else opp + o
            cols += [opp[:BLK], opp[BLK:]]
        za = za_ref[...].astype(F32)
        o_ref[...] = (jnp.concatenate(cols, axis=1) * (za * _sigmoid(za))).astype(BF16)

    prev = lambda n: jnp.maximum(n - 1, 0)
    (o, lse), got = _call(
        body, rider, name="attn_fwd", grid=(nb,),
        in_specs=[pl.BlockSpec((BLK, D), lambda n: (n, 0)),
                  pl.BlockSpec((BLK, KVX), lambda n: (n, 0)), pl.BlockSpec((BLK, KVX), lambda n: (prev(n), 0)),
                  pl.BlockSpec((BLK, D), lambda n: (n, 0)), pl.BlockSpec(memory_space=pltpu.SMEM),
                  pl.BlockSpec((None, 2 * BLK, 2 * BLK), lambda n: (jnp.minimum(n, 1), 0, 0))],
        out_specs=[pl.BlockSpec((BLK, D), lambda n: (n, 0)), pl.BlockSpec((1, 8, 2 * BLK), lambda n: (n, 0, 0))],
        out_shape=[SDS((t, D), BF16), SDS((nb, 8, 2 * BLK), F32)],
        args=(qs, kvx, kvx, u_za, sinks, bias), vmem=VMEM_BIG)
    return o, lse, got


def _out_proj_fwd(x, y_c, o, u_gl, gate_b, w_sm, rider):
    t = x.shape[0]
    tm = min(512, t)

    def body(x_ref, yc_ref, o_ref, gla_ref, glb_ref, gb_ref, wco_ref, wao_ref, wout_ref,
             xn_ref, ya_ref, yb_ref, mg_ref):
        ya = _dot(yc_ref[...], wco_ref[...])
        yb = _dot(o_ref[...], wao_ref[...])
        gb = gb_ref[...]
        ga_ = _sigmoid(gla_ref[...].astype(F32) + gb[:, :D])
        gb_ = _sigmoid(glb_ref[...].astype(F32) + gb[:, D:])
        merged = (ga_ * ya + gb_ * yb).astype(BF16)
        ya_ref[...] = ya.astype(BF16)
        yb_ref[...] = yb.astype(BF16)
        mg_ref[...] = merged
        xn_ref[...] = x_ref[...] + _dot(merged, wout_ref[...])

    row = pl.BlockSpec((tm, D), lambda i: (i, 0))
    wspec = lambda a: pl.BlockSpec((None, D, D), lambda i: (a, 0, 0))
    return _call(
        body, rider, name="out_proj_fwd", grid=(t // tm,),
        in_specs=[row, row, row, pl.BlockSpec((tm, D), lambda i: (i, 0)), pl.BlockSpec((tm, D), lambda i: (i, 1)),
                  pl.BlockSpec((1, 2 * D), lambda i: (0, 0)), wspec(0), wspec(1), wspec(2)],
        out_specs=[row, row, row, row],
        out_shape=[SDS((t, D), F32), SDS((t, D), BF16), SDS((t, D), BF16), SDS((t, D), BF16)],
        args=(x, y_c, o, u_gl, u_gl, gate_b, w_sm, w_sm, w_sm), vmem=VMEM_BIG)


def _loss_head(y, tgt):
    t = y.shape[0]
    tm = min(1024, t)

    def body(y_ref, t_ref, dy_ref, acc_ref):
        @pl.when(pl.program_id(0) == 0)
        def _():
            acc_ref[...] = jnp.zeros_like(acc_ref)
        err = y_ref[...] - t_ref[...]
        dy_ref[...] = err * (1.0 / D)
        sq = _fold8(err * err)
        tot = sq[:, 0:128]
        for k in range(1, D // 128):
            tot = tot + sq[:, 128 * k:128 * (k + 1)]
        acc_ref[...] += tot

    row = pl.BlockSpec((tm, D), lambda i: (i, 0))
    return pl.pallas_call(
        body, name="loss_head", grid=(t // tm,), in_specs=[row, row],
        out_specs=[row, pl.BlockSpec((8, 128), lambda i: (0, 0))],
        out_shape=[SDS((t, D), F32), SDS((8, 128), F32)], compiler_params=_cp("arbitrary"),
    )(y, tgt)


def _out_proj_bwd(dout, y_a, y_b, u_gl, gate_b, w_sm, rider):
    t = dout.shape[0]
    tm = min(512, t)

    def body(do_ref, ya_ref, yb_ref, gla_ref, glb_ref, gb_ref, wco_ref, wao_ref, wout_ref,
             dya_ref, dyb_ref, dgl_ref, dyc_ref, dob_ref, dgb_ref):
        @pl.when(pl.program_id(0) == 0)
        def _():
            dgb_ref[...] = jnp.zeros_like(dgb_ref)
        dm = _dot_nt(do_ref[...].astype(BF16), wout_ref[...])
        gb = gb_ref[...]
        ga_ = _sigmoid(gla_ref[...].astype(F32) + gb[:, :D])
        gb_ = _sigmoid(glb_ref[...].astype(F32) + gb[:, D:])
        dya = (ga_ * dm).astype(BF16)
        dyb = (gb_ * dm).astype(BF16)
        dgla = ya_ref[...].astype(F32) * dm * (ga_ * (1.0 - ga_))
        dglb = yb_ref[...].astype(F32) * dm * (gb_ * (1.0 - gb_))
        dya_ref[...] = dya
        dyb_ref[...] = dyb
        dgl_ref[:, :D] = dgla.astype(BF16)
        dgl_ref[:, D:] = dglb.astype(BF16)
        dgb_ref[:, :D] += _fold8(dgla)
        dgb_ref[:, D:] += _fold8(dglb)
        dyc_ref[...] = _dot_nt(dya, wco_ref[...]).astype(BF16)
        dob_ref[...] = _dot_nt(dyb, wao_ref[...]).astype(BF16)

    row = pl.BlockSpec((tm, D), lambda i: (i, 0))
    wspec = lambda a: pl.BlockSpec((None, D, D), lambda i: (a, 0, 0))
    return _call(
        body, rider, name="out_proj_bwd", grid=(t // tm,),
        in_specs=[row, row, row, pl.BlockSpec((tm, D), lambda i: (i, 0)), pl.BlockSpec((tm, D), lambda i: (i, 1)),
                  pl.BlockSpec((1, 2 * D), lambda i: (0, 0)), wspec(0), wspec(1), wspec(2)],
        out_specs=[row, row, pl.BlockSpec((tm, 2 * D), lambda i: (i, 0)), row, row,
                   pl.BlockSpec((8, 2 * D), lambda i: (0, 0))],
        out_shape=[SDS((t, D), BF16), SDS((t, D), BF16), SDS((t, 2 * D), BF16), SDS((t, D), BF16), SDS((t, D), BF16),
                   SDS((8, 2 * D), F32)],
        args=(dout, y_a, y_b, u_gl, u_gl, gate_b, w_sm, w_sm, w_sm), vmem=VMEM_BIG)


def _small_wgrads(y_c, d_ya, o, d_yb, merged, dout):
    t = y_c.shape[0]
    tk = min(512, t)

    def body(yc_ref, dya_ref, o_ref, dyb_ref, mg_ref, do_ref, g_ref):
        @pl.when(pl.program_id(0) == 0)
        def _():
            g_ref[...] = jnp.zeros_like(g_ref)
        g_ref[0] += _dot_tn(yc_ref[...], dya_ref[...])
        g_ref[1] += _dot_tn(o_ref[...], dyb_ref[...])
        g_ref[2] += _dot_tn(mg_ref[...], do_ref[...].astype(BF16))

    row = pl.BlockSpec((tk, D), lambda k: (k, 0))
    return pl.pallas_call(
        body, name="small_wgrads", grid=(t // tk,), in_specs=[row] * 6,
        out_specs=pl.BlockSpec((3, D, D), lambda k: (0, 0, 0)), out_shape=SDS((3, D, D), F32),
        compiler_params=_cp("arbitrary", vmem=VMEM_BIG),
    )(y_c, d_ya, o, d_yb, merged, dout)


def _conv_bwd(d_yc, u_conv, conv_w, rider):
    t = d_yc.shape[0]
    tm = min(512, t)
    hb = tm // 16
    last_halo = t // 16 - 1
    n_steps = t // tm

    def body(dy_ref, v_ref, b_ref, c_ref, z_ref, hv_ref, hc_ref, ndy_ref, nb_ref, nz_ref, w_ref, du_ref, dw_ref):
        i = pl.program_id(0)

        @pl.when(i == 0)
        def _():
            dw_ref[...] = jnp.zeros_like(dw_ref)
        v, c = v_ref[...].astype(F32), c_ref[...].astype(F32)
        b, z = b_ref[...].astype(F32), z_ref[...].astype(F32)
        cv = c * v
        halo = jnp.where(i > 0, hc_ref[...].astype(F32) * hv_ref[...].astype(F32), 0.0)
        row = lax.broadcasted_iota(jnp.int32, (tm, 1), 0)
        s1 = jnp.where(row == 0, halo[15:16], pltpu.roll(cv, 1, 0))
        s2 = jnp.where(row == 0, halo[14:15], jnp.where(row == 1, halo[15:16], pltpu.roll(cv, 2, 0)))
        w0, w1, w2 = w_ref[0:1, :], w_ref[1:2, :], w_ref[2:3, :]
        conv = w0 * s2 + w1 * s1 + w2 * cv
        sig = _sigmoid(z)
        sz = z * sig
        dsz = sig * (1.0 + z * (1.0 - sig))
        dy = dy_ref[...].astype(F32)
        dconv = dy * b * sz
        nz = nz_ref[...].astype(F32)
        nxt = ndy_ref[...].astype(F32) * nb_ref[...].astype(F32) * (nz * _sigmoid(nz))
        nxt = jnp.where(i < n_steps - 1, nxt, 0.0)
        a1 = jnp.where(row == tm - 1, nxt[0:1], pltpu.roll(dconv, tm - 1, 0))
        a2 = jnp.where(row == tm - 2, nxt[0:1], jnp.where(row == tm - 1, nxt[1:2], pltpu.roll(dconv, tm - 2, 0)))
        dcv = w2 * dconv + w1 * a1 + w0 * a2
        du_ref[:, 0:D] = (dcv * c).astype(BF16)
        du_ref[:, D:2 * D] = (dy * conv * sz).astype(BF16)
        du_ref[:, 2 * D:3 * D] = (dcv * v).astype(BF16)
        du_ref[:, 3 * D:4 * D] = (dy * b * conv * dsz).astype(BF16)
        r8 = lax.broadcasted_iota(jnp.int32, (8, 1), 0)
        dw_ref[...] += jnp.where(r8 == 0, jnp.sum(dconv * s2, axis=0, keepdims=True),
                                 jnp.where(r8 == 1, jnp.sum(dconv * s1, axis=0, keepdims=True),
                                           jnp.where(r8 == 2, jnp.sum(dconv * cv, axis=0, keepdims=True), 0.0)))

    def col(k):
        return pl.BlockSpec((tm, D), lambda i: (i, k))

    def halo(k):
        return pl.BlockSpec((16, D), lambda i: (jnp.maximum(i * hb - 1, 0), k))

    def nxt(k):
        return pl.BlockSpec((16, D), lambda i: (jnp.minimum((i + 1) * hb, last_halo), k))

    return _call(
        body, rider, name="conv_bwd", grid=(t // tm,),
        in_specs=[col(0), col(0), col(1), col(2), col(3), halo(0), halo(2), nxt(0), nxt(1), nxt(3),
                  pl.BlockSpec((3, D), lambda i: (0, 0))],
        out_specs=[pl.BlockSpec((tm, 4 * D), lambda i: (i, 0)), pl.BlockSpec((8, D), lambda i: (0, 0))],
        out_shape=[SDS((t, 4 * D), BF16), SDS((8, D), F32)],
        args=(d_yc, u_conv, u_conv, u_conv, u_conv, u_conv, u_conv, d_yc, u_conv, u_conv, conv_w), vmem=VMEM_BIG)


def _attn_bwd(d_o, qs, kvx, u_za, lse, sinks, bias, rider):
    t = d_o.shape[0]
    nb = t // BLK

    def body(q_ref, kc_ref, kp_ref, za_ref, do_ref, lse_ref, sink_ref, bias_ref,
             dq_ref, dkv_ref, dza_ref, dsk_ref, carry_ref):
        n = pl.program_id(0)

        @pl.when(n == 0)
        def _():
            carry_ref[...] = jnp.zeros_like(carry_ref)
            dsk_ref[...] = jnp.zeros_like(dsk_ref)

        live = n < nb
        kvb = jnp.concatenate([kp_ref[...], kc_ref[...]], axis=0)
        bias_v = bias_ref[...]
        za = za_ref[...].astype(F32)
        sig = _sigmoid(za)
        dsa = sig * (1.0 + za * (1.0 - sig))
        do = jnp.where(live, do_ref[...].astype(F32), 0.0)
        dattn = (do * (za * sig)).astype(BF16)
        lo_lanes = lax.broadcasted_iota(jnp.int32, (1, 128), 1) < HEAD
        dq_cols, attn_cols, dk_cols, dv_cols, dsk_rows = [], [], [], [], []
        for hk in range(N_KV):
            qpp = _pair_rows(q_ref, hk)
            dapp = _pair_rows(dattn, hk)
            probs, dss, xk, xv = [], [], [], []
            for half in range(2):
                kx, vx = _kv_operands(kvb, hk, half)
                lse = lse_ref[0, 2 * hk + half:2 * hk + half + 1, :]
                prob = jnp.exp(_dot_nt(kx, qpp) + bias_v - lse)
                psink = jnp.exp(_sink_row(sink_ref, hk, half) - lse)
                tdp = prob * _dot_nt(vx, dapp)
                drow = jnp.sum(tdp, axis=0, keepdims=True)
                ds = (tdp - prob * drow).astype(BF16)
                prob_b = prob.astype(BF16)
                xk.append(_dot(ds, qpp))
                xv.append(_dot(prob_b, dapp))
                probs.append(prob_b)
                dss.append(ds)
                dsk_rows.append(-psink * drow)
            kcat = jnp.concatenate([kvb[:, 128 * hk:128 * hk + 128], kvb[:, 512 + 128 * hk:512 + 128 * hk + 128]], axis=0)
            vcat = jnp.concatenate([kvb[:, 1024 + 128 * hk:1024 + 128 * hk + 128],
                                    kvb[:, 1536 + 128 * hk:1536 + 128 * hk + 128]], axis=0)
            app = _dot_tn(jnp.concatenate(probs, axis=0), vcat)
            dqpp = _dot_tn(jnp.concatenate(dss, axis=0), kcat)
            dq_cols += [dqpp[:BLK], dqpp[BLK:]]
            attn_cols += [app[:BLK], app[BLK:]]
            dk_cols.append(jnp.where(lo_lanes, xk[0], xk[1]))
            dv_cols.append(jnp.where(lo_lanes, xv[0], xv[1]))

        @pl.when(live)
        def _():
            dq_ref[...] = jnp.concatenate(dq_cols, axis=1).astype(BF16)
            dza_ref[...] = (do * jnp.concatenate(attn_cols, axis=1) * dsa).astype(BF16)

        band = jnp.concatenate(dk_cols + dv_cols, axis=1)
        dkv_ref[...] = (band[:BLK] + carry_ref[...]).astype(BF16)
        carry_ref[...] = band[BLK:]
        dsk_ref[...] += jnp.broadcast_to(jnp.concatenate(dsk_rows, axis=1), (8, 2 * N_KV * 2 * BLK))

    cur = lambda n: jnp.minimum(n, nb - 1)
    prev = lambda n: jnp.maximum(n - 1, 0)
    return _call(
        body, rider, name="attn_bwd", grid=(nb + 1,),
        in_specs=[pl.BlockSpec((BLK, D), lambda n: (cur(n), 0)),
                  pl.BlockSpec((BLK, KVX), lambda n: (cur(n), 0)), pl.BlockSpec((BLK, KVX), lambda n: (prev(n), 0)),
                  pl.BlockSpec((BLK, D), lambda n: (cur(n), 0)), pl.BlockSpec((BLK, D), lambda n: (cur(n), 0)),
                  pl.BlockSpec((1, 8, 2 * BLK), lambda n: (cur(n), 0, 0)), pl.BlockSpec(memory_space=pltpu.SMEM),
                  pl.BlockSpec((None, 2 * BLK, 2 * BLK), lambda n: (jnp.minimum(n, 1), 0, 0))],
        out_specs=[pl.BlockSpec((BLK, D), lambda n: (cur(n), 0)), pl.BlockSpec((BLK, D), lambda n: (prev(n), 0)),
                   pl.BlockSpec((BLK, D), lambda n: (cur(n), 0)), pl.BlockSpec((8, 2 * D), lambda n: (0, 0))],
        out_shape=[SDS((t, D), BF16), SDS((t, D), BF16), SDS((t, D), BF16), SDS((8, 2 * D), F32)],
        scratch_shapes=[pltpu.VMEM((BLK, D), F32)],
        args=(qs, kvx, kvx, u_za, d_o, lse, sinks, bias), vmem=VMEM_BIG)


def _qkv_post(u_qkv, dqs, dkv, dza, qg_s, kg_t, rider):
    t = u_qkv.shape[0]
    tm = min(512, t)

    def norm_bwd(x, dy, g):
        r = lax.rsqrt(_head_sum(x * x) * (1.0 / HEAD) + EPS)
        xhat = x * r
        dxh = dy * g
        return r * (dxh - xhat * (_head_sum(dxh * xhat) * (1.0 / HEAD))), _fold8(dy * xhat)

    def body(u_ref, dq_ref, dkv_ref, dza_ref, qg_ref, kg_ref, du_ref, dqg_ref, dkg_ref):
        @pl.when(pl.program_id(0) == 0)
        def _():
            dqg_ref[...] = jnp.zeros_like(dqg_ref)
            dkg_ref[...] = jnp.zeros_like(dkg_ref)
        dq, gq = norm_bwd(u_ref[:, 0:D].astype(F32), dq_ref[...].astype(F32), qg_ref[...])
        fold = _fold_mat()
        dk, gk = norm_bwd(u_ref[:, D:D + 256].astype(F32), _dot(dkv_ref[:, 0:512], fold), kg_ref[...])
        du_ref[:, 0:D] = dq.astype(BF16)
        du_ref[:, D:D + 256] = dk.astype(BF16)
        du_ref[:, D + 256:D + 512] = _dot(dkv_ref[:, 512:1024], fold).astype(BF16)
        du_ref[:, D + 512:2 * D + 512] = dza_ref[...]
        dqg_ref[...] += gq
        dkg_ref[...] += gk

    row = pl.BlockSpec((tm, D), lambda i: (i, 0))
    return _call(
        body, rider, name="qkv_post", grid=(t // tm,),
        in_specs=[pl.BlockSpec((tm, 1536), lambda i: (i, 0)), row, row, row,
                  pl.BlockSpec((1, D), lambda i: (0, 0)), pl.BlockSpec((1, 256), lambda i: (0, 0))],
        out_specs=[pl.BlockSpec((tm, 2560), lambda i: (i, 0)), pl.BlockSpec((8, D), lambda i: (0, 0)),
                   pl.BlockSpec((8, 256), lambda i: (0, 0))],
        out_shape=[SDS((t, 2560), BF16), SDS((8, D), F32), SDS((8, 256), F32)],
        args=(u_qkv, dqs, dkv, dza, qg_s, kg_t), vmem=VMEM_BIG)


N_GRAN = N_IN // CB
DU_COLS = ((0, 4096), (4096, 6656), (6656, N_IN))


def _du_granule(j):
    return jnp.clip(j, 0, 7), jnp.clip(j - 8, 0, 4), jnp.clip(j - 13, 0, 3)


def _du_select(j, refs, fn):
    for ref, lo, hi in zip(refs, (0, 8, 13), (8, 13, 17)):
        @pl.when((j >= lo) & (j < hi))
        def _():
            fn(ref)


def _in_proj_bwd(du, w_full, x, g, dout, rider):
    t = du[0].shape[0]
    tn = min(256, t)

    def body(a0, a1, a2, w_hbm, x_ref, g_ref, do_ref, dx_ref, dg_ref, w_ref, sem):
        @pl.when(pl.program_id(0) == 0)
        def _():
            cp = pltpu.make_async_copy(w_hbm, w_ref, sem)
            cp.start()
            dg_ref[...] = jnp.zeros_like(dg_ref)
            cp.wait()
        acc = None
        for a_ref, (lo, hi) in zip((a0, a1, a2), DU_COLS):
            part = _dot_nt(w_ref[:, lo:hi], a_ref[...])
            acc = part if acc is None else acc + part
        dh = acc.T
        xv = x_ref[...]
        r = lax.rsqrt(jnp.mean(xv * xv, axis=-1, keepdims=True) + EPS)
        xhat = xv * r
        dg_ref[...] += _fold8(dh * xhat)
        dxh = dh * g_ref[...]
        dx_ref[...] = do_ref[...] + r * (dxh - xhat * jnp.mean(dxh * xhat, axis=-1, keepdims=True))

    row = pl.BlockSpec((tn, D), lambda i: (i, 0))
    return _call(
        body, rider, name="in_proj_bwd", grid=(t // tn,),
        in_specs=[pl.BlockSpec((tn, hi - lo), lambda i: (i, 0)) for lo, hi in DU_COLS]
        + [ANY, row, pl.BlockSpec((1, D), lambda i: (0, 0)), row],
        out_specs=[row, pl.BlockSpec((8, D), lambda i: (0, 0))], out_shape=[SDS((t, D), F32), SDS((8, D), F32)],
        scratch_shapes=[pltpu.VMEM((D, N_IN), BF16), pltpu.SemaphoreType.DMA(())],
        args=(*du, w_full, x, g, dout), vmem=VMEM_BIG)


def _in_proj_wgrad(ht, du, rider):
    t = ht.shape[1]
    tk = min(4096, t)
    n_k = t // tk

    def body(h_ref, b0, b1, b2, g_ref):
        j, k = pl.program_id(0), pl.program_id(1)

        if n_k > 1:
            @pl.when(k == 0)
            def _():
                g_ref[...] = jnp.zeros_like(g_ref)

        def add(b_ref):
            if n_k > 1:
                g_ref[...] += _dot(h_ref[...], b_ref[...])
            else:
                g_ref[...] = _dot(h_ref[...], b_ref[...])
        _du_select(j, (b0, b1, b2), add)

    seg = lambda q: pl.BlockSpec((tk, CB), lambda j, k: (k, _du_granule(j)[q]))
    (g,), got = _call(
        body, rider, name="in_proj_wgrad", grid=(N_GRAN, t // tk),
        in_specs=[pl.BlockSpec((D, tk), lambda j, k: (0, k)), seg(0), seg(1), seg(2)],
        out_specs=[pl.BlockSpec((D, CB), lambda j, k: (0, j))], out_shape=[SDS((D, N_IN), F32)],
        args=(ht, *du), vmem=VMEM_BIG)
    return g, got


def _swap_rider(g_in, g_sm):
    def copies(ins, outs, send, recv, base=0):
        x, y, c = _mesh_pos()
        cps = []
        for src, dst in zip(ins, outs):
            half = src.at[1 - c] if len(src.shape) == 3 else src.at[:, :, 1 - c]
            cps.append(_rcopy(half, dst, send, recv, base + len(cps), (x, y, 1 - c)))
        return cps

    arrays = [g for g in (g_in, g_sm) if g is not None]
    shapes = [SDS((512, N_IN), F32) if len(g.shape) == 3 else SDS((3, 4, 128, D), F32) for g in arrays]
    return _Rider(arrays, shapes, len(arrays), copies)


def _add_halves_in(cc_idx, g_in, r_in):
    def body(cc_ref, a_ref, b_ref, f_ref, h_ref):
        s = a_ref[...] + b_ref[...]
        h_ref[...] = s.astype(BF16)

        @pl.when(pl.program_id(1) == cc_ref[1])
        def _():
            f_ref[...] = s

    blk = pl.BlockSpec((256, SH_IN), lambda i, j, cc: (i, j))
    return pl.pallas_call(
        body, name="add_halves_in",
        grid_spec=pltpu.PrefetchScalarGridSpec(
            num_scalar_prefetch=1, grid=(2, 4),
            in_specs=[pl.BlockSpec((None, 256, SH_IN), lambda i, j, cc: (cc[0], i, j)), blk],
            out_specs=[pl.BlockSpec((256, SH_IN), lambda i, j, cc: (i, 0)), blk]),
        out_shape=[SDS((512, SH_IN), F32), SDS((512, N_IN), BF16)],
        compiler_params=_cp("arbitrary", "arbitrary", vmem=VMEM_BIG),
    )(cc_idx, g_in, r_in)


def _add_halves_sm(c_idx, g_sm, r_sm):
    def body(c_ref, a_ref, b_ref, f_ref, h_ref):
        s = a_ref[...] + b_ref[...]
        f_ref[...] = s
        h_ref[...] = s.astype(BF16)

    blk = pl.BlockSpec((1, 4, 128, D), lambda a, c: (a, 0, 0, 0))
    return pl.pallas_call(
        body, name="add_halves_sm",
        grid_spec=pltpu.PrefetchScalarGridSpec(
            num_scalar_prefetch=1, grid=(3,),
            in_specs=[pl.BlockSpec((1, 4, None, 128, D), lambda a, c: (a, 0, c[0], 0, 0)), blk], out_specs=[blk, blk]),
        out_shape=[SDS((3, 4, 128, D), F32), SDS((3, 4, 128, D), BF16)], compiler_params=_cp("parallel"),
    )(c_idx, g_sm, r_sm)


def _scatter_rider(h_in, h_sm):
    def copies(ins, outs, send, recv, base=0):
        x, y, c = _mesh_pos()
        cps = []
        for src, dst in zip(ins, outs):
            for k, chip in enumerate(_other_chips(x, y)):
                their = 2 * chip[0] + chip[1]
                part = src.at[:, pl.ds(pl.multiple_of(their * SH_IN, 128), SH_IN)] if len(src.shape) == 2 else src.at[:, their]
                cps.append(_rcopy(part, dst.at[k], send, recv, base + len(cps), (*chip, c)))
        return cps

    arrays = [h for h in (h_in, h_sm) if h is not None]
    shapes = [SDS((3, 512, SH_IN), BF16) if len(h.shape) == 2 else SDS((3, 3, 128, D), BF16) for h in arrays]
    return _Rider(arrays, shapes, 3 * len(arrays), copies)


def _ride_alone(rider, name):
    return _hosted_call(None, rider, name=name, grid=(), in_specs=[], out_specs=[], out_shape=[], args=())[1]


def _final_sum_in(f_in, r_in):
    def body(a_ref, r_ref, o_ref):
        o_ref[...] = a_ref[...] + r_ref[0].astype(F32) + r_ref[1].astype(F32) + r_ref[2].astype(F32)

    return pl.pallas_call(
        body, name="final_sum_in", grid=(4,),
        in_specs=[pl.BlockSpec((128, SH_IN), lambda i: (i, 0)), pl.BlockSpec((3, 128, SH_IN), lambda i: (0, i, 0))],
        out_specs=pl.BlockSpec((128, SH_IN), lambda i: (i, 0)),
        out_shape=SDS((512, SH_IN), F32), compiler_params=_cp("parallel"),
    )(f_in, r_in)


def _final_sum_sm(chip_idx, f_sm, r_sm):
    def body(j_ref, a_ref, r_ref, o_ref):
        o_ref[...] = a_ref[...] + r_ref[0].astype(F32) + r_ref[1].astype(F32) + r_ref[2].astype(F32)

    return pl.pallas_call(
        body, name="final_sum_sm",
        grid_spec=pltpu.PrefetchScalarGridSpec(
            num_scalar_prefetch=1, grid=(3,),
            in_specs=[pl.BlockSpec((1, None, 128, D), lambda a, j: (a, j[0], 0, 0)),
                      pl.BlockSpec((3, 1, 128, D), lambda a, j: (0, a, 0, 0))],
            out_specs=pl.BlockSpec((1, 128, D), lambda a, j: (a, 0, 0))),
        out_shape=SDS((3, 128, D), F32), compiler_params=_cp("parallel"),
    )(chip_idx, f_sm, r_sm)


def _join_halves(t_in, t_sm):
    n_cp = N_LAYERS * 4
    args, plan = [], []
    for l in range(N_LAYERS):
        if t_in[l] is not None:
            plan.append((l, 0, len(args)))
            args.append(t_in[l])
        plan += [(l, a, len(args)) for a in (1, 2, 3)]
        args.append(t_sm[l])

    def body(*refs):
        ins, outs = refs[:len(args)], refs[len(args):len(args) + 4]
        send, recv, loc_in, loc_out, stage_in, stage_sm = refs[len(args) + 4:]
        x, y, c = _mesh_pos()
        cps, own = [], []

        def place(l, a, half):
            rows = 512 if a == 0 else 128
            return outs[a].at[l, pl.ds(pl.multiple_of(half * rows, rows), rows), :]

        for s, (l, a, k) in enumerate(plan):
            src = ins[k] if a == 0 else ins[k].at[a - 1]
            own.append((src, place(l, a, c), min(a, 1)))
            cp = pltpu.make_async_remote_copy(src_ref=src, dst_ref=place(l, a, c), send_sem=send.at[s],
                                              recv_sem=recv.at[s], device_id=(x, y, 1 - c), device_id_type=MESH)
            cp.start()
            cps.append(cp)
        _staged_copies(own, (stage_in, stage_sm), loc_in, loc_out)
        for s, (l, a, k) in enumerate(plan):
            got = place(l, a, 1 - c)
            pltpu.make_async_remote_copy(src_ref=got, dst_ref=got, send_sem=send.at[s], recv_sem=recv.at[s],
                                         device_id=(x, y, 1 - c), device_id_type=MESH).wait_recv()
        for cp in cps:
            cp.wait_send()

    sm = SDS((N_LAYERS, SH_ROW, D), F32)
    return pl.pallas_call(
        body, name="join_halves", in_specs=[ANY] * len(args), out_specs=[ANY] * 4,
        out_shape=[SDS((N_LAYERS, D, SH_IN), F32), sm, sm, sm],
        scratch_shapes=[pltpu.SemaphoreType.DMA((n_cp,))] * 4
        + [pltpu.VMEM((2, 512, SH_IN), F32), pltpu.VMEM((2, 128, D), F32)],
        compiler_params=_cp(vmem=VMEM_BIG),
    )(*args)


def _adam_math(w, g, m, v):
    m = ADAM_B1 * m + (1.0 - ADAM_B1) * g
    v = ADAM_B2 * v + (1.0 - ADAM_B2) * (g * g)
    m_hat = m / (1.0 - ADAM_B1 ** ADAM_STEP)
    v_hat = v / (1.0 - ADAM_B2 ** ADAM_STEP)
    delta = -ADAM_LR * (m_hat / (jnp.sqrt(v_hat) + ADAM_EPS) + ADAM_WD * w)
    return delta, m, v


def _adamw_big(w, g, m, v, name):
    rows, cols = w.shape
    tr = 256

    def body(w_ref, g_ref, m_ref, v_ref, go_ref, d_ref, nm_ref, nv_ref):
        g = g_ref[...]
        go_ref[...] = g
        d_ref[...], nm_ref[...], nv_ref[...] = _adam_math(w_ref[...], g, m_ref[...], v_ref[...])

    blk = pl.BlockSpec((tr, cols), lambda i: (i, 0))
    return pl.pallas_call(
        body, name=name, grid=(rows // tr,), in_specs=[blk] * 4, out_specs=[blk] * 4,
        out_shape=[SDS((rows, cols), F32)] * 4, compiler_params=_cp("parallel", vmem=VMEM_BIG),
    )(w, g, m, v)


def _adamw_small(ws, gs, ms, vs):
    n = len(ws)

    def body(*refs):
        for k in range(n):
            w_ref, g_ref, m_ref, v_ref = (refs[q * n + k] for q in range(4))
            d, nm, nv = _adam_math(w_ref[...], g_ref[...], m_ref[...], v_ref[...])
            refs[4 * n + k][...] = d
            refs[5 * n + k][...] = nm
            refs[6 * n + k][...] = nv

    vm = pl.BlockSpec(memory_space=pltpu.VMEM)
    shapes = [SDS(w.shape, F32) for w in ws]
    res = pl.pallas_call(
        body, name="adamw_small", in_specs=[vm] * (4 * n), out_specs=[vm] * (3 * n), out_shape=shapes * 3,
    )(*ws, *gs, *ms, *vs)
    return res[:n], res[n:2 * n], res[2 * n:]


def _pad_rows(a, rows):
    flat = a.reshape(-1)
    return jnp.pad(flat, (0, rows * 128 - flat.shape[0])).reshape(rows, 128)


def kernel(x, norm_g, w_in, conv_w, q_norm_g, k_norm_g, sinks, w_conv_out, w_attn_out, gate_b, w_out, loss_target, m_norm_g, m_w_in, m_conv_w, m_q_norm_g, m_k_norm_g, m_sinks, m_w_conv_out, m_w_attn_out, m_gate_b, m_w_out, v_norm_g, v_w_in, v_conv_w, v_q_norm_g, v_k_norm_g, v_sinks, v_w_conv_out, v_w_attn_out, v_gate_b, v_w_out):
    xi, yi, ci = _mesh_pos()
    chip = 2 * xi + yi
    c_idx = jnp.reshape(ci, (1,)).astype(jnp.int32)
    chip_idx = jnp.reshape(chip, (1,)).astype(jnp.int32)
    cc_idx = jnp.stack([ci, chip]).astype(jnp.int32)
    t = x.shape[1]
    xs = [x.reshape(t, D)]
    tgt = loss_target.reshape(t, D)

    full_w = [[_cast_w_in(chip_idx, w_in, l), _cast_w_small(chip_idx, w_conv_out, w_attn_out, w_out, l)]
              for l in range(N_LAYERS)]
    conv32 = lax.dynamic_update_slice(jnp.zeros((32, D), F32), jnp.pad(conv_w.reshape(3 * N_LAYERS, SH_ROW), ((0, 20), (0, 0))),
                                      (0, chip * SH_ROW))
    qg_s = jnp.tile(q_norm_g, (1, N_Q)) * SCALE
    kg_t = jnp.tile(k_norm_g, (1, N_KV))
    bias = _band_bias()
    saved = []
    for l in range(N_LAYERS):
        nxt = full_w[l + 1] if l + 1 < N_LAYERS else None
        (h, ht), got = _rmsnorm_fwd(xs[l], norm_g[l:l + 1], _gather_rider([full_w[0][0], conv32], "N") if l == 0 else None)
        if l == 0:
            got = _ride_alone(_gather_rider(got, "F"), "gather_first_forward")
            full_w[0][0], conv32 = _ride_alone(_gather_rider(got, "B"), "gather_first_d2d")
            conv_full = conv32[:3 * N_LAYERS].reshape(N_LAYERS, 3, D)
        (u_conv, u_qkv, u_za, u_gl), got = _in_proj(h, full_w[l][0], _gather_rider(nxt, "N") if nxt else None)
        if nxt:
            nxt[0], nxt[1] = got
        (qs, kvx), got = _qkv_prep(u_qkv, qg_s[l:l + 1], kg_t[l:l + 1],
                                   _gather_rider(full_w[0][1:], "N") if l == 0 else None)
        y_c, got = _conv_fwd(u_conv, conv_full[l], _gather_rider(got, "F") if l == 0 else None)
        o, lse, got = _attn_fwd(qs, kvx, u_za, sinks[l:l + 1], bias, _merge_riders(
            _gather_rider(nxt, "FB1") if nxt else None, _gather_rider(got, "B") if l == 0 else None))
        if nxt:
            nxt[0], nxt[1] = got[:2]
        if l == 0:
            full_w[0][1] = got[-1]
        (x_next, y_a, y_b, merged), got = _out_proj_fwd(xs[l], y_c, o, u_gl, gate_b[l:l + 1], full_w[l][1],
                                                        _gather_rider(nxt, "B2") if nxt else None)
        if nxt:
            nxt[0], nxt[1] = got
        xs.append(x_next)
        saved.append((ht, u_conv, u_qkv, u_za, u_gl, y_c, o, y_a, y_b, merged, qs, kvx, lse))

    dout, sq = _loss_head(xs[N_LAYERS], tgt)

    small, t_in, t_sm = [None] * N_LAYERS, [None] * N_LAYERS, [None] * N_LAYERS
    halves = None

    for l in reversed(range(N_LAYERS)):
        w_full, w_sm = full_w[l]
        last = l == 0
        ht, u_conv, u_qkv, u_za, u_gl, y_c, o, y_a, y_b, merged, qs, kvx, lse = saved[l]
        (d_ya, d_yb, du_gl, d_yc, d_o, dgb), _ = _out_proj_bwd(dout, y_a, y_b, u_gl, gate_b[l:l + 1], w_sm, None)
        g_sm = _small_wgrads(y_c, d_ya, o, d_yb, merged, dout).reshape(3, 4, 2, 128, D)
        (du_conv, dcw), got = _conv_bwd(d_yc, u_conv, conv_full[l], _merge_riders(
            _scatter_rider(None, halves[3]) if halves else None, _swap_rider(None, g_sm) if last else None))
        if halves:
            t_sm[l + 1] = _final_sum_sm(chip_idx, halves[2], got[0])
        if last:
            f_sm0, h_sm0 = _add_halves_sm(c_idx, g_sm, got[-1])
        (dqs, dkv, dza, dsk), got = _attn_bwd(d_o, qs, kvx, u_za, lse, sinks[l:l + 1], bias,
                                              _scatter_rider(halves[1], None) if halves else None)
        if halves:
            t_in[l + 1] = _final_sum_in(halves[0], got[0])
        dsk = jnp.sum(dsk[0].reshape(N_KV, 2, 2, BLK), axis=-1).transpose(0, 2, 1).reshape(N_Q)
        (du_attn, dqg, dkg), _ = _qkv_post(u_qkv, dqs, dkv, dza, qg_s[l:l + 1], kg_t[l:l + 1], None)
        du = (du_conv, du_attn, du_gl)
        g_in, got = _in_proj_wgrad(ht, du, _scatter_rider(None, h_sm0) if last else None)
        g_in = g_in.reshape(2, 512, N_IN)
        if last:
            t_sm[0] = _final_sum_sm(chip_idx, f_sm0, got[0])
        if last:
            f_in0, h_in0 = _add_halves_in(cc_idx, g_in, _ride_alone(_swap_rider(g_in, None), "swap_last")[0])
        (dout, dng), got = _in_proj_bwd(du, w_full, xs[l], norm_g[l:l + 1], dout,
                                        _scatter_rider(h_in0, None) if last else _swap_rider(g_in, g_sm))
        if last:
            t_in[0] = _final_sum_in(f_in0, got[0])
        else:
            halves = _add_halves_in(cc_idx, g_in, got[0]) + _add_halves_sm(c_idx, g_sm, got[1])
        small[l] = (jnp.sum(dng, axis=0), SCALE * jnp.sum(dqg.reshape(8 * N_Q, HEAD), axis=0),
                    jnp.sum(dkg.reshape(8 * N_KV, HEAD), axis=0), dsk, jnp.sum(dgb, axis=0), dcw[:3])
    grad_x = dout.reshape(1, t, D)

    stack = lambda k: jnp.stack([small[l][k] for l in range(N_LAYERS)])
    pack = jnp.concatenate([_pad_rows(stack(0), 32), _pad_rows(stack(1), 8), _pad_rows(stack(2), 8),
                            _pad_rows(stack(3), 8), _pad_rows(stack(4), 64), _pad_rows(stack(5), 96),
                            _pad_rows(jnp.sum(sq) * (0.5 / D), 8)], axis=0)
    red = _allreduce_small(pack)
    loss = red[216, 0]
    g_norm_g = red[0:32].reshape(N_LAYERS, D)
    g_q_norm_g = red[32:40].reshape(-1)[:N_LAYERS * HEAD].reshape(N_LAYERS, HEAD)
    g_k_norm_g = red[40:48].reshape(-1)[:N_LAYERS * HEAD].reshape(N_LAYERS, HEAD)
    g_sinks = red[48:56].reshape(-1)[:N_LAYERS * N_Q].reshape(N_LAYERS, N_Q)
    g_gate_b = red[56:120].reshape(N_LAYERS, 2 * D)
    g_conv_full = red[120:216].reshape(N_LAYERS, 3, D)
    g_conv_w = lax.dynamic_slice(g_conv_full, (0, 0, chip * SH_ROW), (N_LAYERS, 3, SH_ROW))

    g_w_in, g_w_co, g_w_ao, g_w_out = _join_halves(t_in, t_sm)

    r_in = N_LAYERS * D
    g_w_in, d_in, nm_in, nv_in = (a.reshape(N_LAYERS, D, SH_IN) for a in _adamw_big(
        w_in.reshape(r_in, SH_IN), g_w_in.reshape(r_in, SH_IN), m_w_in.reshape(r_in, SH_IN),
        v_w_in.reshape(r_in, SH_IN), "adamw_w_in"))
    r_sm = N_LAYERS * SH_ROW
    big = {}
    for nm, w, g, m, v in (("co", w_conv_out, g_w_co, m_w_conv_out, v_w_conv_out),
                           ("ao", w_attn_out, g_w_ao, m_w_attn_out, v_w_attn_out),
                           ("out", w_out, g_w_out, m_w_out, v_w_out)):
        big[nm] = tuple(a.reshape(N_LAYERS, SH_ROW, D) for a in _adamw_big(
            w.reshape(r_sm, D), g.reshape(r_sm, D), m.reshape(r_sm, D), v.reshape(r_sm, D), "adamw_w_small"))
    g_w_co, g_w_ao, g_w_out = big["co"][0], big["ao"][0], big["out"][0]
    sm_w = [norm_g, conv_w, q_norm_g, k_norm_g, sinks, gate_b]
    sm_g = [g_norm_g, g_conv_w, g_q_norm_g, g_k_norm_g, g_sinks, g_gate_b]
    sm_m = [m_norm_g, m_conv_w, m_q_norm_g, m_k_norm_g, m_sinks, m_gate_b]
    sm_v = [v_norm_g, v_conv_w, v_q_norm_g, v_k_norm_g, v_sinks, v_gate_b]
    sd, snm, snv = _adamw_small(sm_w, sm_g, sm_m, sm_v)

    def order(norm, w_in_, conv, qn, kn, sk, co, ao, gb, wo):
        return [norm, w_in_, conv, qn, kn, sk, co, ao, gb, wo]

    grads = order(g_norm_g, g_w_in, g_conv_w, g_q_norm_g, g_k_norm_g, g_sinks, g_w_co, g_w_ao, g_gate_b, g_w_out)
    deltas = order(sd[0], d_in, sd[1], sd[2], sd[3], sd[4], big["co"][1], big["ao"][1], sd[5], big["out"][1])
    new_m = order(snm[0], nm_in, snm[1], snm[2], snm[3], snm[4], big["co"][2], big["ao"][2], snm[5], big["out"][2])
    new_v = order(snv[0], nv_in, snv[1], snv[2], snv[3], snv[4], big["co"][3], big["ao"][3], snv[5], big["out"][3])
    return (loss, grad_x, *grads, *deltas, *new_m, *new_v)
```

```python
import functools

import jax
import jax.numpy as jnp
from jax import lax
from jax.experimental import pallas as pl
from jax.experimental.pallas import tpu as pltpu

F32, BF16 = jnp.float32, jnp.bfloat16
SDS = jax.ShapeDtypeStruct
MESH = pl.DeviceIdType.MESH
ANY = pl.BlockSpec(memory_space=pl.ANY)

D = 1024
N_IN = 8704
N_LAYERS = 4
N_Q, N_KV, HEAD = 16, 4, 64
GROUP = N_Q // N_KV
BLK = 128
EPS = 1e-6
NEG = -1e30
SCALE = HEAD ** -0.5
SH_IN = N_IN // 4
SH_ROW = D // 4
CB = 512
SEG_CONV, SEG_Q, SEG_KV, SEG_ZA, SEG_GL = (0, 8), (8, 2), (10, 1), (11, 2), (13, 4)
VMEM_BIG = 56 * 1024 * 1024

ADAM_LR, ADAM_B1, ADAM_B2, ADAM_EPS, ADAM_WD, ADAM_STEP = 0.001, 0.9, 0.999, 1e-08, 0.01, 10


def _cp(*sem, vmem=None):
    return pltpu.CompilerParams(dimension_semantics=sem if sem else None, vmem_limit_bytes=vmem)


def _sigmoid(z):
    return 1.0 / (1.0 + jnp.exp(-z))


def _dot(a, b):
    return jnp.dot(a, b, preferred_element_type=F32)


def _dot_nt(a, b):
    return lax.dot_general(a, b, (((1,), (1,)), ((), ())), preferred_element_type=F32)


def _dot_tn(a, b):
    return lax.dot_general(a, b, (((0,), (0,)), ((), ())), preferred_element_type=F32)


def _rms(xh):
    r = lax.rsqrt(jnp.mean(xh * xh, axis=-1, keepdims=True) + EPS)
    return xh * r, r


def _fold8(v):
    return jnp.sum(v.reshape(v.shape[0] // 8, 8, v.shape[1]), axis=0)


def _cast_w_in(chip_idx, w, layer):
    def body(j_ref, i_ref, o_ref):
        o_ref[...] = i_ref[...].astype(BF16)

    return pl.pallas_call(
        body, name="cast_w_in",
        grid_spec=pltpu.PrefetchScalarGridSpec(
            num_scalar_prefetch=1, grid=(2,),
            in_specs=[pl.BlockSpec((None, 512, SH_IN), lambda i, j: (layer, i, 0))],
            out_specs=pl.BlockSpec((512, SH_IN), lambda i, j: (i, j[0]))),
        out_shape=SDS((D, N_IN), BF16), compiler_params=_cp("parallel"),
    )(chip_idx, w)


def _cast_w_small(chip_idx, a, b, c, layer):
    def body(j_ref, a_ref, b_ref, c_ref, o_ref):
        o_ref[0] = a_ref[...].astype(BF16)
        o_ref[1] = b_ref[...].astype(BF16)
        o_ref[2] = c_ref[...].astype(BF16)

    spec = pl.BlockSpec((None, SH_ROW, D), lambda i, j: (layer, 0, 0))
    return pl.pallas_call(
        body, name="cast_w_small",
        grid_spec=pltpu.PrefetchScalarGridSpec(
            num_scalar_prefetch=1, grid=(1,), in_specs=[spec, spec, spec],
            out_specs=pl.BlockSpec((3, SH_ROW, D), lambda i, j: (0, j[0], 0))),
        out_shape=SDS((3, D, D), BF16), compiler_params=_cp("parallel"),
    )(chip_idx, a, b, c)


def _mesh_pos():
    return lax.axis_index("x"), lax.axis_index("y"), lax.axis_index("c")


def _other_chips(x, y):
    return [(1 - x, y), (x, 1 - y), (1 - x, 1 - y)]


class _Rider:
    def __init__(self, ins, out_shape, n, copies, aliases=()):
        self.ins, self.out_shape, self.n, self.copies, self.aliases = list(ins), list(out_shape), n, copies, aliases


def _merge_riders(*riders):
    riders = [r for r in riders if r is not None]
    if len(riders) < 2:
        return riders[0] if riders else None

    def copies(ins, outs, send, recv, base=0):
        cps, i0, o0 = [], 0, 0
        for r in riders:
            cps += r.copies(ins[i0:i0 + len(r.ins)], outs[o0:o0 + len(r.out_shape)], send, recv, base + len(cps))
            i0, o0 = i0 + len(r.ins), o0 + len(r.out_shape)
        return cps

    aliases, i0, o0 = [], 0, 0
    for r in riders:
        aliases += [(i0 + i, o0 + o) for i, o in r.aliases]
        i0, o0 = i0 + len(r.ins), o0 + len(r.out_shape)
    return _Rider(sum((r.ins for r in riders), []), sum((r.out_shape for r in riders), []),
                  sum(r.n for r in riders), copies, tuple(aliases))


def _rcopy(src, dst, send, recv, k, to):
    return pltpu.make_async_remote_copy(src_ref=src, dst_ref=dst, send_sem=send.at[k], recv_sem=recv.at[k],
                                        device_id=to, device_id_type=MESH)


def _hosted_call(body, rider, *, name, grid, in_specs, out_specs, out_shape, args, scratch_shapes=(), vmem=None):
    n_in, n_out, n_scr = len(in_specs), len(out_specs), len(scratch_shapes)
    r_in, r_out = len(rider.ins), len(rider.out_shape)

    def full_body(*refs):
        host_in, rid_in = refs[:n_in], refs[n_in:n_in + r_in]
        o0 = n_in + r_in
        host_out, rid_out = refs[o0:o0 + n_out], refs[o0 + n_out:o0 + n_out + r_out]
        s0 = o0 + n_out + r_out
        host_scr, (send, recv) = refs[s0:s0 + n_scr], refs[s0 + n_scr:]
        if body is None:
            cps = rider.copies(rid_in, rid_out, send, recv)
            for cp in cps:
                cp.start()
            for cp in cps:
                cp.wait()
            return
        ids = [pl.program_id(a) for a in range(len(grid))]
        first = functools.reduce(lambda p, q: p & q, [i == 0 for i in ids])
        last = functools.reduce(lambda p, q: p & q, [i == g - 1 for i, g in zip(ids, grid)])

        @pl.when(first)
        def _():
            for cp in rider.copies(rid_in, rid_out, send, recv):
                cp.start()

        body(*host_in, *host_out, *host_scr)

        @pl.when(last)
        def _():
            for cp in rider.copies(rid_in, rid_out, send, recv):
                cp.wait()

    res = pl.pallas_call(
        full_body, name=name, grid=grid if body is not None else (),
        in_specs=list(in_specs) + [ANY] * r_in, out_specs=list(out_specs) + [ANY] * r_out,
        out_shape=list(out_shape) + rider.out_shape,
        scratch_shapes=list(scratch_shapes) + [pltpu.SemaphoreType.DMA((rider.n,))] * 2,
        input_output_aliases={n_in + i: n_out + o for i, o in rider.aliases},
        compiler_params=_cp(*(("arbitrary",) * len(grid) if body is not None else ()), vmem=vmem),
    )(*args, *rider.ins)
    return res[:n_out], res[n_out:]


def _call(body, rider, **kw):
    if rider is not None:
        return _hosted_call(body, rider, **kw)
    res = pl.pallas_call(
        body, name=kw["name"], grid=kw["grid"], in_specs=list(kw["in_specs"]), out_specs=list(kw["out_specs"]),
        out_shape=list(kw["out_shape"]), scratch_shapes=list(kw.get("scratch_shapes", ())),
        compiler_params=_cp(*(("arbitrary",) * len(kw["grid"])), vmem=kw.get("vmem")),
    )(*kw["args"])
    return res, []


def _gather_rider(arrays, stage):
    def region(full, whose, c, sub):
        if len(full.shape) == 2:
            rows, cols = full.shape[0] // 2, full.shape[1] // 4
            first, n = (c * rows, rows) if sub is None else (c * rows + sub * (rows // 2), rows // 2)
            return full.at[pl.ds(pl.multiple_of(first, n), n), pl.ds(pl.multiple_of(whose * cols, 128), cols)]
        first, n = (whose * SH_ROW + c * 128, 128) if sub is None else (whose * SH_ROW + c * 128 + sub * 64, 64)
        return full.at[:, pl.ds(pl.multiple_of(first, n), n), :]

    def copies(ins, outs, send, recv, base=0):
        x, y, c = _mesh_pos()
        nbr_x, nbr_y = (1 - x, y), (x, 1 - y)
        cps = []
        for full in outs:
            plan = []
            if stage == "N":
                plan = [(region(full, 2 * x + y, c, None), (*nbr_x, c)), (region(full, 2 * x + y, c, None), (*nbr_y, c))]
            if stage in ("F", "FB1"):
                plan = [(region(full, 2 * nbr_x[0] + nbr_x[1], c, 0), (*nbr_y, c)),
                        (region(full, 2 * nbr_y[0] + nbr_y[1], c, 1), (*nbr_x, c))]
            if stage in ("B", "FB1", "B2"):
                chips = {"B": _other_chips(x, y), "FB1": [nbr_x, nbr_y], "B2": [(1 - x, 1 - y)]}[stage]
                plan += [(region(full, 2 * chip[0] + chip[1], c, None), (x, y, 1 - c)) for chip in chips]
            for reg, to in plan:
                cps.append(_rcopy(reg, reg, send, recv, base + len(cps), to))
        return cps

    per_array = {"N": 2, "F": 2, "B": 3, "FB1": 4, "B2": 1}[stage]
    return _Rider(arrays, [SDS(v.shape, v.dtype) for v in arrays], per_array * len(arrays), copies,
                  aliases=tuple((i, i) for i in range(len(arrays))))


def _staged_copies(copies, stages, sem_in, sem_out):
    busy, count = {}, {}
    for idx, (src, dst, kind) in enumerate(copies):
        slot = count.get(kind, 0) % 2
        count[kind] = count.get(kind, 0) + 1
        if (kind, slot) in busy:
            busy.pop((kind, slot)).wait()
        buf = stages[kind].at[slot]
        cin = pltpu.make_async_copy(src, buf, sem_in.at[idx])
        cin.start()
        cin.wait()
        cout = pltpu.make_async_copy(buf, dst, sem_out.at[idx])
        cout.start()
        busy[(kind, slot)] = cout
    for cp in busy.values():
        cp.wait()


def _allreduce_small(pack):
    rows = pack.shape[0]

    def body(p_ref, o_ref, buf, send, recv):
        x, y, c = _mesh_pos()
        me = 4 * x + 2 * y + c
        sends = []
        for r in range(1, 8):
            to = (x if not (r & 4) else 1 - x, y if not (r & 2) else 1 - y, c if not (r & 1) else 1 - c)
            cp = pltpu.make_async_remote_copy(src_ref=p_ref, dst_ref=buf.at[me], send_sem=send.at[r - 1],
                                              recv_sem=recv.at[r - 1], device_id=to, device_id_type=MESH)
            cp.start()
            sends.append(cp)
        buf[me] = p_ref[...]
        for r in range(1, 8):
            frm = (4 * x + 2 * y + c) ^ r
            pltpu.make_async_remote_copy(src_ref=p_ref, dst_ref=buf.at[frm], send_sem=send.at[r - 1],
                                         recv_sem=recv.at[r - 1], device_id=(x, y, c), device_id_type=MESH).wait_recv()
        acc = buf[0]
        for d in range(1, 8):
            acc = acc + buf[d]
        o_ref[...] = acc
        for cp in sends:
            cp.wait_send()

    vm = pl.BlockSpec(memory_space=pltpu.VMEM)
    return pl.pallas_call(
        body, name="allreduce_small", in_specs=[vm], out_specs=vm, out_shape=SDS((rows, 128), F32),
        scratch_shapes=[pltpu.VMEM((8, rows, 128), F32), pltpu.SemaphoreType.DMA((7,)), pltpu.SemaphoreType.DMA((7,))],
    )(pack)


def _rmsnorm_fwd(x, g, rider):
    t = x.shape[0]
    tm = min(512, t)

    def body(x_ref, g_ref, h_ref, ht_ref):
        xv = x_ref[...]
        r = lax.rsqrt(jnp.mean(xv * xv, axis=-1, keepdims=True) + EPS)
        h = xv * r * g_ref[...]
        h_ref[...] = h.astype(BF16)
        ht_ref[...] = h.T.astype(BF16)

    return _call(
        body, rider, name="rmsnorm_fwd", grid=(t // tm,),
        in_specs=[pl.BlockSpec((tm, D), lambda i: (i, 0)), pl.BlockSpec((1, D), lambda i: (0, 0))],
        out_specs=[pl.BlockSpec((tm, D), lambda i: (i, 0)), pl.BlockSpec((D, tm), lambda i: (0, i))],
        out_shape=[SDS((t, D), BF16), SDS((D, t), BF16)], args=(x, g), vmem=VMEM_BIG)


FWD_SEGS = ((0, 8), (8, 3), (11, 2), (13, 4))


def _in_proj(h, w_full, rider):
    t = h.shape[0]
    tm = min(2048, t)

    def body(a_ref, b_ref, *o_refs):
        j = pl.program_id(1)
        for o_ref, (off, nblk) in zip(o_refs, FWD_SEGS):
            @pl.when((j >= off) & (j < off + nblk))
            def _():
                o_ref[...] = _dot(a_ref[...], b_ref[...]).astype(BF16)

    def out(seg):
        off, nblk = seg
        return pl.BlockSpec((tm, CB), lambda i, j: (i, jnp.clip(j - off, 0, nblk - 1)))

    res, got = _call(
        body, rider, name="in_proj", grid=(t // tm, N_IN // CB),
        in_specs=[pl.BlockSpec((tm, D), lambda i, j: (i, 0)), pl.BlockSpec((D, CB), lambda i, j: (0, j))],
        out_specs=[out(s) for s in FWD_SEGS], out_shape=[SDS((t, s[1] * CB), BF16) for s in FWD_SEGS],
        args=(h, w_full), vmem=VMEM_BIG)
    return res, got


def _conv_fwd(u_conv, conv_w, rider):
    t = u_conv.shape[0]
    tm = min(512, t)
    hb = tm // 16

    def body(v_ref, b_ref, c_ref, z_ref, hv_ref, hc_ref, w_ref, y_ref):
        i = pl.program_id(0)
        cv = c_ref[...].astype(F32) * v_ref[...].astype(F32)
        halo = hc_ref[...].astype(F32) * hv_ref[...].astype(F32)
        halo = jnp.where(i > 0, halo, 0.0)
        row = lax.broadcasted_iota(jnp.int32, (tm, 1), 0)
        s1 = jnp.where(row == 0, halo[15:16], pltpu.roll(cv, 1, 0))
        s2 = jnp.where(row == 0, halo[14:15], jnp.where(row == 1, halo[15:16], pltpu.roll(cv, 2, 0)))
        conv = w_ref[0:1, :] * s2 + w_ref[1:2, :] * s1 + w_ref[2:3, :] * cv
        z = z_ref[...].astype(F32)
        y_ref[...] = (b_ref[...].astype(F32) * conv * (z * _sigmoid(z))).astype(BF16)

    def col(k):
        return pl.BlockSpec((tm, D), lambda i: (i, k))

    def halo(k):
        return pl.BlockSpec((16, D), lambda i: (jnp.maximum(i * hb - 1, 0), k))

    (y_c,), got = _call(
        body, rider, name="conv_fwd", grid=(t // tm,),
        in_specs=[col(0), col(1), col(2), col(3), halo(0), halo(2), pl.BlockSpec((3, D), lambda i: (0, 0))],
        out_specs=[pl.BlockSpec((tm, D), lambda i: (i, 0))], out_shape=[SDS((t, D), BF16)],
        args=(u_conv, u_conv, u_conv, u_conv, u_conv, u_conv, conv_w), vmem=VMEM_BIG)
    return y_c, got


KVX = 4 * N_KV * 128


def _iota2(shape):
    return lax.broadcasted_iota(jnp.int32, shape, 0), lax.broadcasted_iota(jnp.int32, shape, 1)


def _head_sum(v):
    r, c = _iota2((128, 128))
    ones = ((r >> 6) == (c >> 6)).astype(BF16)
    hi = v.astype(BF16)
    lo = (v - hi.astype(F32)).astype(BF16)
    return jnp.concatenate([_dot(hi[:, g:g + 128], ones) + _dot(lo[:, g:g + 128], ones)
                            for g in range(0, v.shape[1], 128)], axis=1)


def _expand_mats():
    r, c = _iota2((N_KV * HEAD, N_KV * 128))
    base = ((r >> 6) << 7) + (r & 63)
    return (c == base).astype(BF16), (c == base + 64).astype(BF16)


def _fold_mat():
    r, c = _iota2((N_KV * 128, N_KV * HEAD))
    return (((r >> 7) == (c >> 6)) & ((r & 63) == (c & 63))).astype(BF16)


def _qkv_prep(u_qkv, qg_s, kg_t, rider):
    t = u_qkv.shape[0]
    tm = min(512, t)

    def body(u_ref, qg_ref, kg_ref, qs_ref, kvx_ref):
        q = u_ref[:, 0:D].astype(F32)
        rq = lax.rsqrt(_head_sum(q * q) * (1.0 / HEAD) + EPS)
        qs_ref[...] = (q * rq * qg_ref[...]).astype(BF16)
        k = u_ref[:, D:D + 256].astype(F32)
        rk = lax.rsqrt(_head_sum(k * k) * (1.0 / HEAD) + EPS)
        kn = (k * rk * kg_ref[...]).astype(BF16)
        v = u_ref[:, D + 256:D + 512]
        e_lo, e_hi = _expand_mats()
        kvx_ref[:, 0:512] = _dot(kn, e_lo).astype(BF16)
        kvx_ref[:, 512:1024] = _dot(kn, e_hi).astype(BF16)
        kvx_ref[:, 1024:1536] = _dot(v, e_lo).astype(BF16)
        kvx_ref[:, 1536:2048] = _dot(v, e_hi).astype(BF16)

    return _call(
        body, rider, name="qkv_prep", grid=(t // tm,),
        in_specs=[pl.BlockSpec((tm, 1536), lambda i: (i, 0)), pl.BlockSpec((1, D), lambda i: (0, 0)),
                  pl.BlockSpec((1, 256), lambda i: (0, 0))],
        out_specs=[pl.BlockSpec((tm, D), lambda i: (i, 0)), pl.BlockSpec((tm, KVX), lambda i: (i, 0))],
        out_shape=[SDS((t, D), BF16), SDS((t, KVX), BF16)], args=(u_qkv, qg_s, kg_t), vmem=VMEM_BIG)


def _band_bias():
    j, r = _iota2((2 * BLK, 2 * BLK))
    diff = (r & (BLK - 1)) - j + BLK
    band = (diff >= 0) & (diff < BLK)
    return jnp.stack([jnp.where(band & (j >= BLK), 0.0, NEG), jnp.where(band, 0.0, NEG)]).astype(F32)


def _pair_rows(ref_or_val, hk):
    return jnp.concatenate([ref_or_val[:, 256 * hk:256 * hk + 128], ref_or_val[:, 256 * hk + 128:256 * hk + 256]], axis=0)


def _sink_row(sink_ref, hk, half):
    return jnp.concatenate([jnp.full((1, BLK), sink_ref[0, GROUP * hk + half], F32),
                            jnp.full((1, BLK), sink_ref[0, GROUP * hk + 2 + half], F32)], axis=1)


def _kv_operands(kvb, hk, half):
    return (kvb[:, 512 * half + 128 * hk:512 * half + 128 * hk + 128],
            kvb[:, 1024 + 512 * half + 128 * hk:1024 + 512 * half + 128 * hk + 128])


def _attn_fwd(qs, kvx, u_za, sinks, bias, rider):
    t = qs.shape[0]
    nb = t // BLK

    def body(q_ref, kc_ref, kp_ref, za_ref, sink_ref, bias_ref, o_ref, lse_ref):
        kvb = jnp.concatenate([kp_ref[...], kc_ref[...]], axis=0)
        bias_v = bias_ref[...]
        key0 = lax.broadcasted_iota(jnp.int32, (2 * BLK, 1), 0) == 0
        ones = jnp.ones((2 * BLK, 128), BF16)
        cols = []
        for hk in range(N_KV):
            qpp = _pair_rows(q_ref, hk)
            opp = None
            for half in range(2):
                kx, vx = _kv_operands(kvb, hk, half)
                s = _dot_nt(kx, qpp) + bias_v
                sink = _sink_row(sink_ref, hk, half)
                m = jnp.maximum(jnp.max(s, axis=0, keepdims=True), sink)
                p = jnp.exp(s - m)
                es = jnp.exp(sink - m)
                lse_ref[0, 2 * hk + half:2 * hk + half + 1, :] = m + jnp.log(jnp.sum(p, axis=0, keepdims=True) + es)
                pe = jnp.where(key0, es, p).astype(BF16)
                rhs = jnp.concatenate([jnp.where(key0, jnp.zeros_like(vx), vx), ones], axis=1)
                nd = _dot_tn(pe, rhs)
                o = nd[:, :128] * (1.0 / nd[:, 128:])
                opp = o if opp is None else opp + o
            cols += [opp[:BLK], opp[BLK:]]
        za = za_ref[...].astype(F32)
        o_ref[...] = (jnp.concatenate(cols, axis=1) * (za * _sigmoid(za))).astype(BF16)

    prev = lambda n: jnp.maximum(n - 1, 0)
    (o, lse), got = _call(
        body, rider, name="attn_fwd", grid=(nb,),
        in_specs=[pl.BlockSpec((BLK, D), lambda n: (n, 0)),
                  pl.BlockSpec((BLK, KVX), lambda n: (n, 0)), pl.BlockSpec((BLK, KVX), lambda n: (prev(n), 0)),
                  pl.BlockSpec((BLK, D), lambda n: (n, 0)), pl.BlockSpec(memory_space=pltpu.SMEM),
                  pl.BlockSpec((None, 2 * BLK, 2 * BLK), lambda n: (jnp.minimum(n, 1), 0, 0))],
        out_specs=[pl.BlockSpec((BLK, D), lambda n: (n, 0)), pl.BlockSpec((1, 8, 2 * BLK), lambda n: (n, 0, 0))],
        out_shape=[SDS((t, D), BF16), SDS((nb, 8, 2 * BLK), F32)],
        args=(qs, kvx, kvx, u_za, sinks, bias), vmem=VMEM_BIG)
    return o, lse, got


def _out_proj_fwd(x, y_c, o, u_gl, gate_b, w_sm, rider):
    t = x.shape[0]
    tm = min(512, t)

    def body(x_ref, yc_ref, o_ref, gla_ref, glb_ref, gb_ref, wco_ref, wao_ref, wout_ref,
             xn_ref, ya_ref, yb_ref, mg_ref):
        ya = _dot(yc_ref[...], wco_ref[...])
        yb = _dot(o_ref[...], wao_ref[...])
        gb = gb_ref[...]
        ga_ = _sigmoid(gla_ref[...].astype(F32) + gb[:, :D])
        gb_ = _sigmoid(glb_ref[...].astype(F32) + gb[:, D:])
        merged = (ga_ * ya + gb_ * yb).astype(BF16)
        ya_ref[...] = ya.astype(BF16)
        yb_ref[...] = yb.astype(BF16)
        mg_ref[...] = merged
        xn_ref[...] = x_ref[...] + _dot(merged, wout_ref[...])

    row = pl.BlockSpec((tm, D), lambda i: (i, 0))
    wspec = lambda a: pl.BlockSpec((None, D, D), lambda i: (a, 0, 0))
    return _call(
        body, rider, name="out_proj_fwd", grid=(t // tm,),
        in_specs=[row, row, row, pl.BlockSpec((tm, D), lambda i: (i, 0)), pl.BlockSpec((tm, D), lambda i: (i, 1)),
                  pl.BlockSpec((1, 2 * D), lambda i: (0, 0)), wspec(0), wspec(1), wspec(2)],
        out_specs=[row, row, row, row],
        out_shape=[SDS((t, D), F32), SDS((t, D), BF16), SDS((t, D), BF16), SDS((t, D), BF16)],
        args=(x, y_c, o, u_gl, u_gl, gate_b, w_sm, w_sm, w_sm), vmem=VMEM_BIG)


def _loss_head(y, tgt):
    t = y.shape[0]
    tm = min(1024, t)

    def body(y_ref, t_ref, dy_ref, acc_ref):
        @pl.when(pl.program_id(0) == 0)
        def _():
            acc_ref[...] = jnp.zeros_like(acc_ref)
        err = y_ref[...] - t_ref[...]
        dy_ref[...] = err * (1.0 / D)
        sq = _fold8(err * err)
        tot = sq[:, 0:128]
        for k in range(1, D // 128):
            tot = tot + sq[:, 128 * k:128 * (k + 1)]
        acc_ref[...] += tot

    row = pl.BlockSpec((tm, D), lambda i: (i, 0))
    return pl.pallas_call(
        body, name="loss_head", grid=(t // tm,), in_specs=[row, row],
        out_specs=[row, pl.BlockSpec((8, 128), lambda i: (0, 0))],
        out_shape=[SDS((t, D), F32), SDS((8, 128), F32)], compiler_params=_cp("arbitrary"),
    )(y, tgt)


def _out_proj_bwd(dout, y_a, y_b, u_gl, gate_b, w_sm, rider):
    t = dout.shape[0]
    tm = min(512, t)

    def body(do_ref, ya_ref, yb_ref, gla_ref, glb_ref, gb_ref, wco_ref, wao_ref, wout_ref,
             dya_ref, dyb_ref, dgl_ref, dyc_ref, dob_ref, dgb_ref):
        @pl.when(pl.program_id(0) == 0)
        def _():
            dgb_ref[...] = jnp.zeros_like(dgb_ref)
        dm = _dot_nt(wout_ref[...], do_ref[...].astype(BF16)).T
        gb = gb_ref[...]
        ga_ = _sigmoid(gla_ref[...].astype(F32) + gb[:, :D])
        gb_ = _sigmoid(glb_ref[...].astype(F32) + gb[:, D:])
        dya = (ga_ * dm).astype(BF16)
        dyb = (gb_ * dm).astype(BF16)
        dgla = ya_ref[...].astype(F32) * dm * (ga_ * (1.0 - ga_))
        dglb = yb_ref[...].astype(F32) * dm * (gb_ * (1.0 - gb_))
        dya_ref[...] = dya
        dyb_ref[...] = dyb
        dgl_ref[:, :D] = dgla.astype(BF16)
        dgl_ref[:, D:] = dglb.astype(BF16)
        dgb_ref[:, :D] += _fold8(dgla)
        dgb_ref[:, D:] += _fold8(dglb)
        dyc_ref[...] = _dot_nt(wco_ref[...], dya).T.astype(BF16)
        dob_ref[...] = _dot_nt(wao_ref[...], dyb).T.astype(BF16)

    row = pl.BlockSpec((tm, D), lambda i: (i, 0))
    wspec = lambda a: pl.BlockSpec((None, D, D), lambda i: (a, 0, 0))
    return _call(
        body, rider, name="out_proj_bwd", grid=(t // tm,),
        in_specs=[row, row, row, pl.BlockSpec((tm, D), lambda i: (i, 0)), pl.BlockSpec((tm, D), lambda i: (i, 1)),
                  pl.BlockSpec((1, 2 * D), lambda i: (0, 0)), wspec(0), wspec(1), wspec(2)],
        out_specs=[row, row, pl.BlockSpec((tm, 2 * D), lambda i: (i, 0)), row, row,
                   pl.BlockSpec((8, 2 * D), lambda i: (0, 0))],
        out_shape=[SDS((t, D), BF16), SDS((t, D), BF16), SDS((t, 2 * D), BF16), SDS((t, D), BF16), SDS((t, D), BF16),
                   SDS((8, 2 * D), F32)],
        args=(dout, y_a, y_b, u_gl, u_gl, gate_b, w_sm, w_sm, w_sm), vmem=VMEM_BIG)


def _small_wgrads(y_c, d_ya, o, d_yb, merged, dout):
    t = y_c.shape[0]
    tk = min(512, t)

    def body(yc_ref, dya_ref, o_ref, dyb_ref, mg_ref, do_ref, g_ref):
        @pl.when(pl.program_id(0) == 0)
        def _():
            g_ref[...] = jnp.zeros_like(g_ref)
        g_ref[0] += _dot_tn(yc_ref[...], dya_ref[...])
        g_ref[1] += _dot_tn(o_ref[...], dyb_ref[...])
        g_ref[2] += _dot_tn(mg_ref[...], do_ref[...].astype(BF16))

    row = pl.BlockSpec((tk, D), lambda k: (k, 0))
    return pl.pallas_call(
        body, name="small_wgrads", grid=(t // tk,), in_specs=[row] * 6,
        out_specs=pl.BlockSpec((3, D, D), lambda k: (0, 0, 0)), out_shape=SDS((3, D, D), F32),
        compiler_params=_cp("arbitrary", vmem=VMEM_BIG),
    )(y_c, d_ya, o, d_yb, merged, dout)


def _conv_bwd(d_yc, u_conv, conv_w, rider):
    t = d_yc.shape[0]
    tm = min(512, t)
    hb = tm // 16
    last_halo = t // 16 - 1
    n_steps = t // tm

    def body(dy_ref, v_ref, b_ref, c_ref, z_ref, hv_ref, hc_ref, ndy_ref, nb_ref, nz_ref, w_ref, du_ref, dw_ref):
        i = pl.program_id(0)

        @pl.when(i == 0)
        def _():
            dw_ref[...] = jnp.zeros_like(dw_ref)
        v, c = v_ref[...].astype(F32), c_ref[...].astype(F32)
        b, z = b_ref[...].astype(F32), z_ref[...].astype(F32)
        cv = c * v
        halo = jnp.where(i > 0, hc_ref[...].astype(F32) * hv_ref[...].astype(F32), 0.0)
        row = lax.broadcasted_iota(jnp.int32, (tm, 1), 0)
        s1 = jnp.where(row == 0, halo[15:16], pltpu.roll(cv, 1, 0))
        s2 = jnp.where(row == 0, halo[14:15], jnp.where(row == 1, halo[15:16], pltpu.roll(cv, 2, 0)))
        w0, w1, w2 = w_ref[0:1, :], w_ref[1:2, :], w_ref[2:3, :]
        conv = w0 * s2 + w1 * s1 + w2 * cv
        sig = _sigmoid(z)
        sz = z * sig
        dsz = sig * (1.0 + z * (1.0 - sig))
        dy = dy_ref[...].astype(F32)
        dconv = dy * b * sz
        nz = nz_ref[...].astype(F32)
        nxt = ndy_ref[...].astype(F32) * nb_ref[...].astype(F32) * (nz * _sigmoid(nz))
        nxt = jnp.where(i < n_steps - 1, nxt, 0.0)
        a1 = jnp.where(row == tm - 1, nxt[0:1], pltpu.roll(dconv, tm - 1, 0))
        a2 = jnp.where(row == tm - 2, nxt[0:1], jnp.where(row == tm - 1, nxt[1:2], pltpu.roll(dconv, tm - 2, 0)))
        dcv = w2 * dconv + w1 * a1 + w0 * a2
        du_ref[:, 0:D] = (dcv * c).astype(BF16)
        du_ref[:, D:2 * D] = (dy * conv * sz).astype(BF16)
        du_ref[:, 2 * D:3 * D] = (dcv * v).astype(BF16)
        du_ref[:, 3 * D:4 * D] = (dy * b * conv * dsz).astype(BF16)
        r8 = lax.broadcasted_iota(jnp.int32, (8, 1), 0)
        dw_ref[...] += jnp.where(r8 == 0, jnp.sum(dconv * s2, axis=0, keepdims=True),
                                 jnp.where(r8 == 1, jnp.sum(dconv * s1, axis=0, keepdims=True),
                                           jnp.where(r8 == 2, jnp.sum(dconv * cv, axis=0, keepdims=True), 0.0)))

    def col(k):
        return pl.BlockSpec((tm, D), lambda i: (i, k))

    def halo(k):
        return pl.BlockSpec((16, D), lambda i: (jnp.maximum(i * hb - 1, 0), k))

    def nxt(k):
        return pl.BlockSpec((16, D), lambda i: (jnp.minimum((i + 1) * hb, last_halo), k))

    return _call(
        body, rider, name="conv_bwd", grid=(t // tm,),
        in_specs=[col(0), col(0), col(1), col(2), col(3), halo(0), halo(2), nxt(0), nxt(1), nxt(3),
                  pl.BlockSpec((3, D), lambda i: (0, 0))],
        out_specs=[pl.BlockSpec((tm, 4 * D), lambda i: (i, 0)), pl.BlockSpec((8, D), lambda i: (0, 0))],
        out_shape=[SDS((t, 4 * D), BF16), SDS((8, D), F32)],
        args=(d_yc, u_conv, u_conv, u_conv, u_conv, u_conv, u_conv, d_yc, u_conv, u_conv, conv_w), vmem=VMEM_BIG)


def _attn_bwd(d_o, qs, kvx, u_za, lse, sinks, bias, rider):
    t = d_o.shape[0]
    nb = t // BLK

    def body(q_ref, kc_ref, kp_ref, za_ref, do_ref, lse_ref, sink_ref, bias_ref,
             dq_ref, dkv_ref, dza_ref, dsk_ref, carry_ref):
        n = pl.program_id(0)

        @pl.when(n == 0)
        def _():
            carry_ref[...] = jnp.zeros_like(carry_ref)
            dsk_ref[...] = jnp.zeros_like(dsk_ref)

        live = n < nb
        kvb = jnp.concatenate([kp_ref[...], kc_ref[...]], axis=0)
        bias_v = bias_ref[...]
        za = za_ref[...].astype(F32)
        sig = _sigmoid(za)
        dsa = sig * (1.0 + za * (1.0 - sig))
        do = jnp.where(live, do_ref[...].astype(F32), 0.0)
        dattn = (do * (za * sig)).astype(BF16)
        lo_lanes = lax.broadcasted_iota(jnp.int32, (1, 128), 1) < HEAD
        dq_cols, attn_cols, dk_cols, dv_cols, dsk_rows = [], [], [], [], []
        for hk in range(N_KV):
            qpp = _pair_rows(q_ref, hk)
            dapp = _pair_rows(dattn, hk)
            probs, dss, xk, xv = [], [], [], []
            for half in range(2):
                kx, vx = _kv_operands(kvb, hk, half)
                lse = lse_ref[0, 2 * hk + half:2 * hk + half + 1, :]
                prob = jnp.exp(_dot_nt(kx, qpp) + bias_v - lse)
                psink = jnp.exp(_sink_row(sink_ref, hk, half) - lse)
                tdp = prob * _dot_nt(vx, dapp)
                drow = jnp.sum(tdp, axis=0, keepdims=True)
                ds = (tdp - prob * drow).astype(BF16)
                prob_b = prob.astype(BF16)
                xk.append(_dot(ds, qpp))
                xv.append(_dot(prob_b, dapp))
                probs.append(prob_b)
                dss.append(ds)
                dsk_rows.append(-psink * drow)
            kcat = jnp.concatenate([kvb[:, 128 * hk:128 * hk + 128], kvb[:, 512 + 128 * hk:512 + 128 * hk + 128]], axis=0)
            vcat = jnp.concatenate([kvb[:, 1024 + 128 * hk:1024 + 128 * hk + 128],
                                    kvb[:, 1536 + 128 * hk:1536 + 128 * hk + 128]], axis=0)
            app = _dot_tn(jnp.concatenate(probs, axis=0), vcat)
            dqpp = _dot_tn(jnp.concatenate(dss, axis=0), kcat)
            dq_cols += [dqpp[:BLK], dqpp[BLK:]]
            attn_cols += [app[:BLK], app[BLK:]]
            dk_cols.append(jnp.where(lo_lanes, xk[0], xk[1]))
            dv_cols.append(jnp.where(lo_lanes, xv[0], xv[1]))

        @pl.when(live)
        def _():
            dq_ref[...] = jnp.concatenate(dq_cols, axis=1).astype(BF16)
            dza_ref[...] = (do * jnp.concatenate(attn_cols, axis=1) * dsa).astype(BF16)

        band = jnp.concatenate(dk_cols + dv_cols, axis=1)
        dkv_ref[...] = (band[:BLK] + carry_ref[...]).astype(BF16)
        carry_ref[...] = band[BLK:]
        dsk_ref[...] += jnp.broadcast_to(jnp.concatenate(dsk_rows, axis=1), (8, 2 * N_KV * 2 * BLK))

    cur = lambda n: jnp.minimum(n, nb - 1)
    prev = lambda n: jnp.maximum(n - 1, 0)
    return _call(
        body, rider, name="attn_bwd", grid=(nb + 1,),
        in_specs=[pl.BlockSpec((BLK, D), lambda n: (cur(n), 0)),
                  pl.BlockSpec((BLK, KVX), lambda n: (cur(n), 0)), pl.BlockSpec((BLK, KVX), lambda n: (prev(n), 0)),
                  pl.BlockSpec((BLK, D), lambda n: (cur(n), 0)), pl.BlockSpec((BLK, D), lambda n: (cur(n), 0)),
                  pl.BlockSpec((1, 8, 2 * BLK), lambda n: (cur(n), 0, 0)), pl.BlockSpec(memory_space=pltpu.SMEM),
                  pl.BlockSpec((None, 2 * BLK, 2 * BLK), lambda n: (jnp.minimum(n, 1), 0, 0))],
        out_specs=[pl.BlockSpec((BLK, D), lambda n: (cur(n), 0)), pl.BlockSpec((BLK, D), lambda n: (prev(n), 0)),
                   pl.BlockSpec((BLK, D), lambda n: (cur(n), 0)), pl.BlockSpec((8, 2 * D), lambda n: (0, 0))],
        out_shape=[SDS((t, D), BF16), SDS((t, D), BF16), SDS((t, D), BF16), SDS((8, 2 * D), F32)],
        scratch_shapes=[pltpu.VMEM((BLK, D), F32)],
        args=(qs, kvx, kvx, u_za, d_o, lse, sinks, bias), vmem=VMEM_BIG)


def _qkv_post(u_qkv, dqs, dkv, dza, qg_s, kg_t, rider):
    t = u_qkv.shape[0]
    tm = min(512, t)

    def norm_bwd(x, dy, g):
        r = lax.rsqrt(_head_sum(x * x) * (1.0 / HEAD) + EPS)
        xhat = x * r
        dxh = dy * g
        return r * (dxh - xhat * (_head_sum(dxh * xhat) * (1.0 / HEAD))), _fold8(dy * xhat)

    def body(u_ref, dq_ref, dkv_ref, dza_ref, qg_ref, kg_ref, du_ref, dqg_ref, dkg_ref):
        @pl.when(pl.program_id(0) == 0)
        def _():
            dqg_ref[...] = jnp.zeros_like(dqg_ref)
            dkg_ref[...] = jnp.zeros_like(dkg_ref)
        dq, gq = norm_bwd(u_ref[:, 0:D].astype(F32), dq_ref[...].astype(F32), qg_ref[...])
        fold = _fold_mat()
        dk, gk = norm_bwd(u_ref[:, D:D + 256].astype(F32), _dot(dkv_ref[:, 0:512], fold), kg_ref[...])
        du_ref[:, 0:D] = dq.astype(BF16)
        du_ref[:, D:D + 256] = dk.astype(BF16)
        du_ref[:, D + 256:D + 512] = _dot(dkv_ref[:, 512:1024], fold).astype(BF16)
        du_ref[:, D + 512:2 * D + 512] = dza_ref[...]
        dqg_ref[...] += gq
        dkg_ref[...] += gk

    row = pl.BlockSpec((tm, D), lambda i: (i, 0))
    return _call(
        body, rider, name="qkv_post", grid=(t // tm,),
        in_specs=[pl.BlockSpec((tm, 1536), lambda i: (i, 0)), row, row, row,
                  pl.BlockSpec((1, D), lambda i: (0, 0)), pl.BlockSpec((1, 256), lambda i: (0, 0))],
        out_specs=[pl.BlockSpec((tm, 2560), lambda i: (i, 0)), pl.BlockSpec((8, D), lambda i: (0, 0)),
                   pl.BlockSpec((8, 256), lambda i: (0, 0))],
        out_shape=[SDS((t, 2560), BF16), SDS((8, D), F32), SDS((8, 256), F32)],
        args=(u_qkv, dqs, dkv, dza, qg_s, kg_t), vmem=VMEM_BIG)


N_GRAN = N_IN // CB
DU_COLS = ((0, 4096), (4096, 6656), (6656, N_IN))


def _du_granule(j):
    return jnp.clip(j, 0, 7), jnp.clip(j - 8, 0, 4), jnp.clip(j - 13, 0, 3)


def _du_select(j, refs, fn):
    for ref, lo, hi in zip(refs, (0, 8, 13), (8, 13, 17)):
        @pl.when((j >= lo) & (j < hi))
        def _():
            fn(ref)


def _in_proj_bwd(du, w_full, x, g, dout, rider):
    t = du[0].shape[0]
    tn = min(256, t)

    def body(a0, a1, a2, w_hbm, x_ref, g_ref, do_ref, dx_ref, dg_ref, w_ref, sem):
        @pl.when(pl.program_id(0) == 0)
        def _():
            cp = pltpu.make_async_copy(w_hbm, w_ref, sem)
            cp.start()
            dg_ref[...] = jnp.zeros_like(dg_ref)
            cp.wait()
        acc = None
        for a_ref, (lo, hi) in zip((a0, a1, a2), DU_COLS):
            part = _dot_nt(w_ref[:, lo:hi], a_ref[...])
            acc = part if acc is None else acc + part
        dh = acc.T
        xv = x_ref[...]
        r = lax.rsqrt(jnp.mean(xv * xv, axis=-1, keepdims=True) + EPS)
        xhat = xv * r
        dg_ref[...] += _fold8(dh * xhat)
        dxh = dh * g_ref[...]
        dx_ref[...] = do_ref[...] + r * (dxh - xhat * jnp.mean(dxh * xhat, axis=-1, keepdims=True))

    row = pl.BlockSpec((tn, D), lambda i: (i, 0))
    return _call(
        body, rider, name="in_proj_bwd", grid=(t // tn,),
        in_specs=[pl.BlockSpec((tn, hi - lo), lambda i: (i, 0)) for lo, hi in DU_COLS]
        + [ANY, row, pl.BlockSpec((1, D), lambda i: (0, 0)), row],
        out_specs=[row, pl.BlockSpec((8, D), lambda i: (0, 0))], out_shape=[SDS((t, D), F32), SDS((8, D), F32)],
        scratch_shapes=[pltpu.VMEM((D, N_IN), BF16), pltpu.SemaphoreType.DMA(())],
        args=(*du, w_full, x, g, dout), vmem=VMEM_BIG)


def _in_proj_wgrad(ht, du, rider):
    t = ht.shape[1]
    tk = min(4096, t)
    n_k = t // tk

    def body(h_ref, b0, b1, b2, g_ref):
        j, k = pl.program_id(0), pl.program_id(1)

        if n_k > 1:
            @pl.when(k == 0)
            def _():
                g_ref[...] = jnp.zeros_like(g_ref)

        def add(b_ref):
            if n_k > 1:
                g_ref[...] += _dot(h_ref[...], b_ref[...])
            else:
                g_ref[...] = _dot(h_ref[...], b_ref[...])
        _du_select(j, (b0, b1, b2), add)

    seg = lambda q: pl.BlockSpec((tk, CB), lambda j, k: (k, _du_granule(j)[q]))
    (g,), got = _call(
        body, rider, name="in_proj_wgrad", grid=(N_GRAN, t // tk),
        in_specs=[pl.BlockSpec((D, tk), lambda j, k: (0, k)), seg(0), seg(1), seg(2)],
        out_specs=[pl.BlockSpec((D, CB), lambda j, k: (0, j))], out_shape=[SDS((D, N_IN), F32)],
        args=(ht, *du), vmem=VMEM_BIG)
    return g, got


def _swap_rider(g_in, g_sm):
    def copies(ins, outs, send, recv, base=0):
        x, y, c = _mesh_pos()
        cps = []
        for src, dst in zip(ins, outs):
            half = src.at[1 - c] if len(src.shape) == 3 else src.at[:, :, 1 - c]
            cps.append(_rcopy(half, dst, send, recv, base + len(cps), (x, y, 1 - c)))
        return cps

    arrays = [g for g in (g_in, g_sm) if g is not None]
    shapes = [SDS((512, N_IN), F32) if len(g.shape) == 3 else SDS((3, 4, 128, D), F32) for g in arrays]
    return _Rider(arrays, shapes, len(arrays), copies)


def _add_halves_in(cc_idx, g_in, r_in):
    def body(cc_ref, a_ref, b_ref, f_ref, h_ref):
        s = a_ref[...] + b_ref[...]
        h_ref[...] = s.astype(BF16)

        @pl.when(pl.program_id(1) == cc_ref[1])
        def _():
            f_ref[...] = s

    blk = pl.BlockSpec((256, SH_IN), lambda i, j, cc: (i, j))
    return pl.pallas_call(
        body, name="add_halves_in",
        grid_spec=pltpu.PrefetchScalarGridSpec(
            num_scalar_prefetch=1, grid=(2, 4),
            in_specs=[pl.BlockSpec((None, 256, SH_IN), lambda i, j, cc: (cc[0], i, j)), blk],
            out_specs=[pl.BlockSpec((256, SH_IN), lambda i, j, cc: (i, 0)), blk]),
        out_shape=[SDS((512, SH_IN), F32), SDS((512, N_IN), BF16)],
        compiler_params=_cp("arbitrary", "arbitrary", vmem=VMEM_BIG),
    )(cc_idx, g_in, r_in)


def _add_halves_sm(c_idx, g_sm, r_sm):
    def body(c_ref, a_ref, b_ref, f_ref, h_ref):
        s = a_ref[...] + b_ref[...]
        f_ref[...] = s
        h_ref[...] = s.astype(BF16)

    blk = pl.BlockSpec((1, 4, 128, D), lambda a, c: (a, 0, 0, 0))
    return pl.pallas_call(
        body, name="add_halves_sm",
        grid_spec=pltpu.PrefetchScalarGridSpec(
            num_scalar_prefetch=1, grid=(3,),
            in_specs=[pl.BlockSpec((1, 4, None, 128, D), lambda a, c: (a, 0, c[0], 0, 0)), blk], out_specs=[blk, blk]),
        out_shape=[SDS((3, 4, 128, D), F32), SDS((3, 4, 128, D), BF16)], compiler_params=_cp("parallel"),
    )(c_idx, g_sm, r_sm)


def _scatter_rider(h_in, h_sm):
    def copies(ins, outs, send, recv, base=0):
        x, y, c = _mesh_pos()
        cps = []
        for src, dst in zip(ins, outs):
            for k, chip in enumerate(_other_chips(x, y)):
                their = 2 * chip[0] + chip[1]
                part = src.at[:, pl.ds(pl.multiple_of(their * SH_IN, 128), SH_IN)] if len(src.shape) == 2 else src.at[:, their]
                cps.append(_rcopy(part, dst.at[k], send, recv, base + len(cps), (*chip, c)))
        return cps

    arrays = [h for h in (h_in, h_sm) if h is not None]
    shapes = [SDS((3, 512, SH_IN), BF16) if len(h.shape) == 2 else SDS((3, 3, 128, D), BF16) for h in arrays]
    return _Rider(arrays, shapes, 3 * len(arrays), copies)


def _ride_alone(rider, name):
    return _hosted_call(None, rider, name=name, grid=(), in_specs=[], out_specs=[], out_shape=[], args=())[1]


def _final_sum_in(f_in, r_in):
    def body(a_ref, r_ref, o_ref):
        o_ref[...] = a_ref[...] + r_ref[0].astype(F32) + r_ref[1].astype(F32) + r_ref[2].astype(F32)

    return pl.pallas_call(
        body, name="final_sum_in", grid=(4,),
        in_specs=[pl.BlockSpec((128, SH_IN), lambda i: (i, 0)), pl.BlockSpec((3, 128, SH_IN), lambda i: (0, i, 0))],
        out_specs=pl.BlockSpec((128, SH_IN), lambda i: (i, 0)),
        out_shape=SDS((512, SH_IN), F32), compiler_params=_cp("parallel"),
    )(f_in, r_in)


def _final_sum_sm(chip_idx, f_sm, r_sm):
    def body(j_ref, a_ref, r_ref, o_ref):
        o_ref[...] = a_ref[...] + r_ref[0].astype(F32) + r_ref[1].astype(F32) + r_ref[2].astype(F32)

    return pl.pallas_call(
        body, name="final_sum_sm",
        grid_spec=pltpu.PrefetchScalarGridSpec(
            num_scalar_prefetch=1, grid=(3,),
            in_specs=[pl.BlockSpec((1, None, 128, D), lambda a, j: (a, j[0], 0, 0)),
                      pl.BlockSpec((3, 1, 128, D), lambda a, j: (0, a, 0, 0))],
            out_specs=pl.BlockSpec((1, 128, D), lambda a, j: (a, 0, 0))),
        out_shape=SDS((3, 128, D), F32), compiler_params=_cp("parallel"),
    )(chip_idx, f_sm, r_sm)


def _join_halves(t_in, t_sm):
    n_cp = N_LAYERS * 4
    args, plan = [], []
    for l in range(N_LAYERS):
        if t_in[l] is not None:
            plan.append((l, 0, len(args)))
            args.append(t_in[l])
        plan += [(l, a, len(args)) for a in (1, 2, 3)]
        args.append(t_sm[l])

    def body(*refs):
        ins, outs = refs[:len(args)], refs[len(args):len(args) + 4]
        send, recv, loc_in, loc_out, stage_in, stage_sm = refs[len(args) + 4:]
        x, y, c = _mesh_pos()
        cps, own = [], []

        def place(l, a, half):
            rows = 512 if a == 0 else 128
            return outs[a].at[l, pl.ds(pl.multiple_of(half * rows, rows), rows), :]

        for s, (l, a, k) in enumerate(plan):
            src = ins[k] if a == 0 else ins[k].at[a - 1]
            own.append((src, place(l, a, c), min(a, 1)))
            cp = pltpu.make_async_remote_copy(src_ref=src, dst_ref=place(l, a, c), send_sem=send.at[s],
                                              recv_sem=recv.at[s], device_id=(x, y, 1 - c), device_id_type=MESH)
            cp.start()
            cps.append(cp)
        _staged_copies(own, (stage_in, stage_sm), loc_in, loc_out)
        for s, (l, a, k) in enumerate(plan):
            got = place(l, a, 1 - c)
            pltpu.make_async_remote_copy(src_ref=got, dst_ref=got, send_sem=send.at[s], recv_sem=recv.at[s],
                                         device_id=(x, y, 1 - c), device_id_type=MESH).wait_recv()
        for cp in cps:
            cp.wait_send()

    sm = SDS((N_LAYERS, SH_ROW, D), F32)
    return pl.pallas_call(
        body, name="join_halves", in_specs=[ANY] * len(args), out_specs=[ANY] * 4,
        out_shape=[SDS((N_LAYERS, D, SH_IN), F32), sm, sm, sm],
        scratch_shapes=[pltpu.SemaphoreType.DMA((n_cp,))] * 4
        + [pltpu.VMEM((2, 512, SH_IN), F32), pltpu.VMEM((2, 128, D), F32)],
        compiler_params=_cp(vmem=VMEM_BIG),
    )(*args)


def _adam_math(w, g, m, v):
    m = ADAM_B1 * m + (1.0 - ADAM_B1) * g
    v = ADAM_B2 * v + (1.0 - ADAM_B2) * (g * g)
    m_hat = m / (1.0 - ADAM_B1 ** ADAM_STEP)
    v_hat = v / (1.0 - ADAM_B2 ** ADAM_STEP)
    delta = -ADAM_LR * (m_hat / (jnp.sqrt(v_hat) + ADAM_EPS) + ADAM_WD * w)
    return delta, m, v


def _adamw_big(w, g, m, v, name):
    rows, cols = w.shape
    tr = 256

    def body(w_ref, g_ref, m_ref, v_ref, go_ref, d_ref, nm_ref, nv_ref):
        g = g_ref[...]
        go_ref[...] = g
        d_ref[...], nm_ref[...], nv_ref[...] = _adam_math(w_ref[...], g, m_ref[...], v_ref[...])

    blk = pl.BlockSpec((tr, cols), lambda i: (i, 0))
    return pl.pallas_call(
        body, name=name, grid=(rows // tr,), in_specs=[blk] * 4, out_specs=[blk] * 4,
        out_shape=[SDS((rows, cols), F32)] * 4, compiler_params=_cp("parallel", vmem=VMEM_BIG),
    )(w, g, m, v)


def _adamw_small(ws, gs, ms, vs):
    n = len(ws)

    def body(*refs):
        for k in range(n):
            w_ref, g_ref, m_ref, v_ref = (refs[q * n + k] for q in range(4))
            d, nm, nv = _adam_math(w_ref[...], g_ref[...], m_ref[...], v_ref[...])
            refs[4 * n + k][...] = d
            refs[5 * n + k][...] = nm
            refs[6 * n + k][...] = nv

    vm = pl.BlockSpec(memory_space=pltpu.VMEM)
    shapes = [SDS(w.shape, F32) for w in ws]
    res = pl.pallas_call(
        body, name="adamw_small", in_specs=[vm] * (4 * n), out_specs=[vm] * (3 * n), out_shape=shapes * 3,
    )(*ws, *gs, *ms, *vs)
    return res[:n], res[n:2 * n], res[2 * n:]


def _pad_rows(a, rows):
    flat = a.reshape(-1)
    return jnp.pad(flat, (0, rows * 128 - flat.shape[0])).reshape(rows, 128)


def kernel(x, norm_g, w_in, conv_w, q_norm_g, k_norm_g, sinks, w_conv_out, w_attn_out, gate_b, w_out, loss_target, m_norm_g, m_w_in, m_conv_w, m_q_norm_g, m_k_norm_g, m_sinks, m_w_conv_out, m_w_attn_out, m_gate_b, m_w_out, v_norm_g, v_w_in, v_conv_w, v_q_norm_g, v_k_norm_g, v_sinks, v_w_conv_out, v_w_attn_out, v_gate_b, v_w_out):
    xi, yi, ci = _mesh_pos()
    chip = 2 * xi + yi
    c_idx = jnp.reshape(ci, (1,)).astype(jnp.int32)
    chip_idx = jnp.reshape(chip, (1,)).astype(jnp.int32)
    cc_idx = jnp.stack([ci, chip]).astype(jnp.int32)
    t = x.shape[1]
    xs = [x.reshape(t, D)]
    tgt = loss_target.reshape(t, D)

    full_w = [[_cast_w_in(chip_idx, w_in, l), _cast_w_small(chip_idx, w_conv_out, w_attn_out, w_out, l)]
              for l in range(N_LAYERS)]
    conv32 = lax.dynamic_update_slice(jnp.zeros((32, D), F32), jnp.pad(conv_w.reshape(3 * N_LAYERS, SH_ROW), ((0, 20), (0, 0))),
                                      (0, chip * SH_ROW))
    qg_s = jnp.tile(q_norm_g, (1, N_Q)) * SCALE
    kg_t = jnp.tile(k_norm_g, (1, N_KV))
    bias = _band_bias()
    saved = []
    for l in range(N_LAYERS):
        nxt = full_w[l + 1] if l + 1 < N_LAYERS else None
        (h, ht), got = _rmsnorm_fwd(xs[l], norm_g[l:l + 1], _gather_rider([full_w[0][0], conv32], "N") if l == 0 else None)
        if l == 0:
            got = _ride_alone(_gather_rider(got, "F"), "gather_first_forward")
            full_w[0][0], conv32 = _ride_alone(_gather_rider(got, "B"), "gather_first_d2d")
            conv_full = conv32[:3 * N_LAYERS].reshape(N_LAYERS, 3, D)
        (u_conv, u_qkv, u_za, u_gl), got = _in_proj(h, full_w[l][0], _gather_rider(nxt, "N") if nxt else None)
        if nxt:
            nxt[0], nxt[1] = got
        (qs, kvx), got = _qkv_prep(u_qkv, qg_s[l:l + 1], kg_t[l:l + 1],
                                   _gather_rider(full_w[0][1:], "N") if l == 0 else None)
        y_c, got = _conv_fwd(u_conv, conv_full[l], _gather_rider(got, "F") if l == 0 else None)
        o, lse, got = _attn_fwd(qs, kvx, u_za, sinks[l:l + 1], bias, _merge_riders(
            _gather_rider(nxt, "FB1") if nxt else None, _gather_rider(got, "B") if l == 0 else None))
        if nxt:
            nxt[0], nxt[1] = got[:2]
        if l == 0:
            full_w[0][1] = got[-1]
        (x_next, y_a, y_b, merged), got = _out_proj_fwd(xs[l], y_c, o, u_gl, gate_b[l:l + 1], full_w[l][1],
                                                        _gather_rider(nxt, "B2") if nxt else None)
        if nxt:
            nxt[0], nxt[1] = got
        xs.append(x_next)
        saved.append((ht, u_conv, u_qkv, u_za, u_gl, y_c, o, y_a, y_b, merged, qs, kvx, lse))

    dout, sq = _loss_head(xs[N_LAYERS], tgt)

    small, t_in, t_sm = [None] * N_LAYERS, [None] * N_LAYERS, [None] * N_LAYERS
    halves = None

    for l in reversed(range(N_LAYERS)):
        w_full, w_sm = full_w[l]
        last = l == 0
        ht, u_conv, u_qkv, u_za, u_gl, y_c, o, y_a, y_b, merged, qs, kvx, lse = saved[l]
        (d_ya, d_yb, du_gl, d_yc, d_o, dgb), _ = _out_proj_bwd(dout, y_a, y_b, u_gl, gate_b[l:l + 1], w_sm, None)
        g_sm = _small_wgrads(y_c, d_ya, o, d_yb, merged, dout).reshape(3, 4, 2, 128, D)
        (du_conv, dcw), got = _conv_bwd(d_yc, u_conv, conv_full[l], _merge_riders(
            _scatter_rider(None, halves[3]) if halves else None, _swap_rider(None, g_sm) if last else None))
        if halves:
            t_sm[l + 1] = _final_sum_sm(chip_idx, halves[2], got[0])
        if last:
            f_sm0, h_sm0 = _add_halves_sm(c_idx, g_sm, got[-1])
        (dqs, dkv, dza, dsk), got = _attn_bwd(d_o, qs, kvx, u_za, lse, sinks[l:l + 1], bias,
                                              _scatter_rider(halves[1], None) if halves else None)
        if halves:
            t_in[l + 1] = _final_sum_in(halves[0], got[0])
        dsk = jnp.sum(dsk[0].reshape(N_KV, 2, 2, BLK), axis=-1).transpose(0, 2, 1).reshape(N_Q)
        (du_attn, dqg, dkg), _ = _qkv_post(u_qkv, dqs, dkv, dza, qg_s[l:l + 1], kg_t[l:l + 1], None)
        du = (du_conv, du_attn, du_gl)
        g_in, got = _in_proj_wgrad(ht, du, _scatter_rider(None, h_sm0) if last else None)
        g_in = g_in.reshape(2, 512, N_IN)
        if last:
            t_sm[0] = _final_sum_sm(chip_idx, f_sm0, got[0])
        if last:
            f_in0, h_in0 = _add_halves_in(cc_idx, g_in, _ride_alone(_swap_rider(g_in, None), "swap_last")[0])
        (dout, dng), got = _in_proj_bwd(du, w_full, xs[l], norm_g[l:l + 1], dout,
                                        _scatter_rider(h_in0, None) if last else _swap_rider(g_in, g_sm))
        if last:
            t_in[0] = _final_sum_in(f_in0, got[0])
        else:
            halves = _add_halves_in(cc_idx, g_in, got[0]) + _add_halves_sm(c_idx, g_sm, got[1])
        small[l] = (jnp.sum(dng, axis=0), SCALE * jnp.sum(dqg.reshape(8 * N_Q, HEAD), axis=0),
                    jnp.sum(dkg.reshape(8 * N_KV, HEAD), axis=0), dsk, jnp.sum(dgb, axis=0), dcw[:3])
    grad_x = dout.reshape(1, t, D)

    stack = lambda k: jnp.stack([small[l][k] for l in range(N_LAYERS)])
    pack = jnp.concatenate([_pad_rows(stack(0), 32), _pad_rows(stack(1), 8), _pad_rows(stack(2), 8),
                            _pad_rows(stack(3), 8), _pad_rows(stack(4), 64), _pad_rows(stack(5), 96),
                            _pad_rows(jnp.sum(sq) * (0.5 / D), 8)], axis=0)
    red = _allreduce_small(pack)
    loss = red[216, 0]
    g_norm_g = red[0:32].reshape(N_LAYERS, D)
    g_q_norm_g = red[32:40].reshape(-1)[:N_LAYERS * HEAD].reshape(N_LAYERS, HEAD)
    g_k_norm_g = red[40:48].reshape(-1)[:N_LAYERS * HEAD].reshape(N_LAYERS, HEAD)
    g_sinks = red[48:56].reshape(-1)[:N_LAYERS * N_Q].reshape(N_LAYERS, N_Q)
    g_gate_b = red[56:120].reshape(N_LAYERS, 2 * D)
    g_conv_full = red[120:216].reshape(N_LAYERS, 3, D)
    g_conv_w = lax.dynamic_slice(g_conv_full, (0, 0, chip * SH_ROW), (N_LAYERS, 3, SH_ROW))

    g_w_in, g_w_co, g_w_ao, g_w_out = _join_halves(t_in, t_sm)

    r_in = N_LAYERS * D
    g_w_in, d_in, nm_in, nv_in = (a.reshape(N_LAYERS, D, SH_IN) for a in _adamw_big(
        w_in.reshape(r_in, SH_IN), g_w_in.reshape(r_in, SH_IN), m_w_in.reshape(r_in, SH_IN),
        v_w_in.reshape(r_in, SH_IN), "adamw_w_in"))
    r_sm = N_LAYERS * SH_ROW
    big = {}
    for nm, w, g, m, v in (("co", w_conv_out, g_w_co, m_w_conv_out, v_w_conv_out),
                           ("ao", w_attn_out, g_w_ao, m_w_attn_out, v_w_attn_out),
                           ("out", w_out, g_w_out, m_w_out, v_w_out)):
        big[nm] = tuple(a.reshape(N_LAYERS, SH_ROW, D) for a in _adamw_big(
            w.reshape(r_sm, D), g.reshape(r_sm, D), m.reshape(r_sm, D), v.reshape(r_sm, D), "adamw_w_small"))
    g_w_co, g_w_ao, g_w_out = big["co"][0], big["ao"][0], big["out"][0]
    sm_w = [norm_g, conv_w, q_norm_g, k_norm_g, sinks, gate_b]
    sm_g = [g_norm_g, g_conv_w, g_q_norm_g, g_k_norm_g, g_sinks, g_gate_b]
    sm_m = [m_norm_g, m_conv_w, m_q_norm_g, m_k_norm_g, m_sinks, m_gate_b]
    sm_v = [v_norm_g, v_conv_w, v_q_norm_g, v_k_norm_g, v_sinks, v_gate_b]
    sd, snm, snv = _adamw_small(sm_w, sm_g, sm_m, sm_v)

    def order(norm, w_in_, conv, qn, kn, sk, co, ao, gb, wo):
        return [norm, w_in_, conv, qn, kn, sk, co, ao, gb, wo]

    grads = order(g_norm_g, g_w_in, g_conv_w, g_q_norm_g, g_k_norm_g, g_sinks, g_w_co, g_w_ao, g_gate_b, g_w_out)
    deltas = order(sd[0], d_in, sd[1], sd[2], sd[3], sd[4], big["co"][1], big["ao"][1], sd[5], big["out"][1])
    new_m = order(snm[0], nm_in, snm[1], snm[2], snm[3], snm[4], big["co"][2], big["ao"][2], snm[5], big["out"][2])
    new_v = order(snv[0], nv_in, snv[1], snv[2], snv[3], snv[4], big["co"][3], big["ao"][3], snv[5], big["out"][3])
    return (loss, grad_x, *grads, *deltas, *new_m, *new_v)
```

```python
import functools

import jax
import jax.numpy as jnp
from jax import lax
from jax.experimental import pallas as pl
from jax.experimental.pallas import tpu as pltpu

F32, BF16 = jnp.float32, jnp.bfloat16
SDS = jax.ShapeDtypeStruct
MESH = pl.DeviceIdType.MESH
ANY = pl.BlockSpec(memory_space=pl.ANY)

D = 1024
N_IN = 8704
N_LAYERS = 4
N_Q, N_KV, HEAD = 16, 4, 64
GROUP = N_Q // N_KV
BLK = 128
EPS = 1e-6
NEG = -1e30
SCALE = HEAD ** -0.5
SH_IN = N_IN // 4
SH_ROW = D // 4
CB = 512
SEG_CONV, SEG_Q, SEG_KV, SEG_ZA, SEG_GL = (0, 8), (8, 2), (10, 1), (11, 2), (13, 4)
VMEM_BIG = 56 * 1024 * 1024

ADAM_LR, ADAM_B1, ADAM_B2, ADAM_EPS, ADAM_WD, ADAM_STEP = 0.001, 0.9, 0.999, 1e-08, 0.01, 10


def _cp(*sem, vmem=None):
    return pltpu.CompilerParams(dimension_semantics=sem if sem else None, vmem_limit_bytes=vmem)


def _sigmoid(z):
    return 1.0 / (1.0 + jnp.exp(-z))


def _dot(a, b):
    return jnp.dot(a, b, preferred_element_type=F32)


def _dot_nt(a, b):
    return lax.dot_general(a, b, (((1,), (1,)), ((), ())), preferred_element_type=F32)


def _dot_tn(a, b):
    return lax.dot_general(a, b, (((0,), (0,)), ((), ())), preferred_element_type=F32)


def _rms(xh):
    r = lax.rsqrt(jnp.mean(xh * xh, axis=-1, keepdims=True) + EPS)
    return xh * r, r


def _fold8(v):
    return jnp.sum(v.reshape(v.shape[0] // 8, 8, v.shape[1]), axis=0)


def _cast_w_in(chip_idx, w, layer):
    def body(j_ref, i_ref, o_ref):
        o_ref[...] = i_ref[...].astype(BF16)

    return pl.pallas_call(
        body, name="cast_w_in",
        grid_spec=pltpu.PrefetchScalarGridSpec(
            num_scalar_prefetch=1, grid=(2,),
            in_specs=[pl.BlockSpec((None, 512, SH_IN), lambda i, j: (layer, i, 0))],
            out_specs=pl.BlockSpec((512, SH_IN), lambda i, j: (i, j[0]))),
        out_shape=SDS((D, N_IN), BF16), compiler_params=_cp("parallel"),
    )(chip_idx, w)


def _cast_w_small(chip_idx, a, b, c, layer):
    def body(j_ref, a_ref, b_ref, c_ref, o_ref):
        o_ref[0] = a_ref[...].astype(BF16)
        o_ref[1] = b_ref[...].astype(BF16)
        o_ref[2] = c_ref[...].astype(BF16)

    spec = pl.BlockSpec((None, SH_ROW, D), lambda i, j: (layer, 0, 0))
    return pl.pallas_call(
        body, name="cast_w_small",
        grid_spec=pltpu.PrefetchScalarGridSpec(
            num_scalar_prefetch=1, grid=(1,), in_specs=[spec, spec, spec],
            out_specs=pl.BlockSpec((3, SH_ROW, D), lambda i, j: (0, j[0], 0))),
        out_shape=SDS((3, D, D), BF16), compiler_params=_cp("parallel"),
    )(chip_idx, a, b, c)


def _mesh_pos():
    return lax.axis_index("x"), lax.axis_index("y"), lax.axis_index("c")


def _other_chips(x, y):
    return [(1 - x, y), (x, 1 - y), (1 - x, 1 - y)]


class _Rider:
    def __init__(self, ins, out_shape, n, copies, aliases=()):
        self.ins, self.out_shape, self.n, self.copies, self.aliases = list(ins), list(out_shape), n, copies, aliases


def _merge_riders(*riders):
    riders = [r for r in riders if r is not None]
    if len(riders) < 2:
        return riders[0] if riders else None

    def copies(ins, outs, send, recv, base=0):
        cps, i0, o0 = [], 0, 0
        for r in riders:
            cps += r.copies(ins[i0:i0 + len(r.ins)], outs[o0:o0 + len(r.out_shape)], send, recv, base + len(cps))
            i0, o0 = i0 + len(r.ins), o0 + len(r.out_shape)
        return cps

    aliases, i0, o0 = [], 0, 0
    for r in riders:
        aliases += [(i0 + i, o0 + o) for i, o in r.aliases]
        i0, o0 = i0 + len(r.ins), o0 + len(r.out_shape)
    return _Rider(sum((r.ins for r in riders), []), sum((r.out_shape for r in riders), []),
                  sum(r.n for r in riders), copies, tuple(aliases))


def _rcopy(src, dst, send, recv, k, to):
    return pltpu.make_async_remote_copy(src_ref=src, dst_ref=dst, send_sem=send.at[k], recv_sem=recv.at[k],
                                        device_id=to, device_id_type=MESH)


def _hosted_call(body, rider, *, name, grid, in_specs, out_specs, out_shape, args, scratch_shapes=(), vmem=None):
    n_in, n_out, n_scr = len(in_specs), len(out_specs), len(scratch_shapes)
    r_in, r_out = len(rider.ins), len(rider.out_shape)

    def full_body(*refs):
        host_in, rid_in = refs[:n_in], refs[n_in:n_in + r_in]
        o0 = n_in + r_in
        host_out, rid_out = refs[o0:o0 + n_out], refs[o0 + n_out:o0 + n_out + r_out]
        s0 = o0 + n_out + r_out
        host_scr, (send, recv) = refs[s0:s0 + n_scr], refs[s0 + n_scr:]
        if body is None:
            cps = rider.copies(rid_in, rid_out, send, recv)
            for cp in cps:
                cp.start()
            for cp in cps:
                cp.wait()
            return
        ids = [pl.program_id(a) for a in range(len(grid))]
        first = functools.reduce(lambda p, q: p & q, [i == 0 for i in ids])
        last = functools.reduce(lambda p, q: p & q, [i == g - 1 for i, g in zip(ids, grid)])

        @pl.when(first)
        def _():
            for cp in rider.copies(rid_in, rid_out, send, recv):
                cp.start()

        body(*host_in, *host_out, *host_scr)

        @pl.when(last)
        def _():
            for cp in rider.copies(rid_in, rid_out, send, recv):
                cp.wait()

    res = pl.pallas_call(
        full_body, name=name, grid=grid if body is not None else (),
        in_specs=list(in_specs) + [ANY] * r_in, out_specs=list(out_specs) + [ANY] * r_out,
        out_shape=list(out_shape) + rider.out_shape,
        scratch_shapes=list(scratch_shapes) + [pltpu.SemaphoreType.DMA((rider.n,))] * 2,
        input_output_aliases={n_in + i: n_out + o for i, o in rider.aliases},
        compiler_params=_cp(*(("arbitrary",) * len(grid) if body is not None else ()), vmem=vmem),
    )(*args, *rider.ins)
    return res[:n_out], res[n_out:]


def _call(body, rider, **kw):
    if rider is not None:
        return _hosted_call(body, rider, **kw)
    res = pl.pallas_call(
        body, name=kw["name"], grid=kw["grid"], in_specs=list(kw["in_specs"]), out_specs=list(kw["out_specs"]),
        out_shape=list(kw["out_shape"]), scratch_shapes=list(kw.get("scratch_shapes", ())),
        compiler_params=_cp(*(("arbitrary",) * len(kw["grid"])), vmem=kw.get("vmem")),
    )(*kw["args"])
    return res, []


def _gather_rider(arrays, stage):
    def region(full, whose, c, sub):
        if len(full.shape) == 2:
            rows, cols = full.shape[0] // 2, full.shape[1] // 4
            first, n = (c * rows, rows) if sub is None else (c * rows + sub * (rows // 2), rows // 2)
            return full.at[pl.ds(pl.multiple_of(first, n), n), pl.ds(pl.multiple_of(whose * cols, 128), cols)]
        first, n = (whose * SH_ROW + c * 128, 128) if sub is None else (whose * SH_ROW + c * 128 + sub * 64, 64)
        return full.at[:, pl.ds(pl.multiple_of(first, n), n), :]

    def copies(ins, outs, send, recv, base=0):
        x, y, c = _mesh_pos()
        nbr_x, nbr_y = (1 - x, y), (x, 1 - y)
        cps = []
        for full in outs:
            plan = []
            if stage == "N":
                plan = [(region(full, 2 * x + y, c, None), (*nbr_x, c)), (region(full, 2 * x + y, c, None), (*nbr_y, c))]
            if stage in ("F", "FB1"):
                plan = [(region(full, 2 * nbr_x[0] + nbr_x[1], c, 0), (*nbr_y, c)),
                        (region(full, 2 * nbr_y[0] + nbr_y[1], c, 1), (*nbr_x, c))]
            if stage in ("B", "FB1", "B2"):
                chips = {"B": _other_chips(x, y), "FB1": [nbr_x, nbr_y], "B2": [(1 - x, 1 - y)]}[stage]
                plan += [(region(full, 2 * chip[0] + chip[1], c, None), (x, y, 1 - c)) for chip in chips]
            for reg, to in plan:
                cps.append(_rcopy(reg, reg, send, recv, base + len(cps), to))
        return cps

    per_array = {"N": 2, "F": 2, "B": 3, "FB1": 4, "B2": 1}[stage]
    return _Rider(arrays, [SDS(v.shape, v.dtype) for v in arrays], per_array * len(arrays), copies,
                  aliases=tuple((i, i) for i in range(len(arrays))))


def _staged_copies(copies, stages, sem_in, sem_out):
    busy, count = {}, {}
    for idx, (src, dst, kind) in enumerate(copies):
        slot = count.get(kind, 0) % 2
        count[kind] = count.get(kind, 0) + 1
        if (kind, slot) in busy:
            busy.pop((kind, slot)).wait()
        buf = stages[kind].at[slot]
        cin = pltpu.make_async_copy(src, buf, sem_in.at[idx])
        cin.start()
        cin.wait()
        cout = pltpu.make_async_copy(buf, dst, sem_out.at[idx])
        cout.start()
        busy[(kind, slot)] = cout
    for cp in busy.values():
        cp.wait()


def _allreduce_small(pack):
    rows = pack.shape[0]

    def body(p_ref, o_ref, buf, send, recv):
        x, y, c = _mesh_pos()
        me = 4 * x + 2 * y + c
        sends = []
        for r in range(1, 8):
            to = (x if not (r & 4) else 1 - x, y if not (r & 2) else 1 - y, c if not (r & 1) else 1 - c)
            cp = pltpu.make_async_remote_copy(src_ref=p_ref, dst_ref=buf.at[me], send_sem=send.at[r - 1],
                                              recv_sem=recv.at[r - 1], device_id=to, device_id_type=MESH)
            cp.start()
            sends.append(cp)
        buf[me] = p_ref[...]
        for r in range(1, 8):
            frm = (4 * x + 2 * y + c) ^ r
            pltpu.make_async_remote_copy(src_ref=p_ref, dst_ref=buf.at[frm], send_sem=send.at[r - 1],
                                         recv_sem=recv.at[r - 1], device_id=(x, y, c), device_id_type=MESH).wait_recv()
        acc = buf[0]
        for d in range(1, 8):
            acc = acc + buf[d]
        o_ref[...] = acc
        for cp in sends:
            cp.wait_send()

    vm = pl.BlockSpec(memory_space=pltpu.VMEM)
    return pl.pallas_call(
        body, name="allreduce_small", in_specs=[vm], out_specs=vm, out_shape=SDS((rows, 128), F32),
        scratch_shapes=[pltpu.VMEM((8, rows, 128), F32), pltpu.SemaphoreType.DMA((7,)), pltpu.SemaphoreType.DMA((7,))],
    )(pack)


def _rmsnorm_fwd(x, g, rider):
    t = x.shape[0]
    tm = min(1024, t)

    def body(x_ref, g_ref, h_ref, ht_ref):
        xv = x_ref[...]
        r = lax.rsqrt(jnp.mean(xv * xv, axis=-1, keepdims=True) + EPS)
        h = xv * r * g_ref[...]
        h_ref[...] = h.astype(BF16)
        ht_ref[...] = h.T.astype(BF16)

    return _call(
        body, rider, name="rmsnorm_fwd", grid=(t // tm,),
        in_specs=[pl.BlockSpec((tm, D), lambda i: (i, 0)), pl.BlockSpec((1, D), lambda i: (0, 0))],
        out_specs=[pl.BlockSpec((tm, D), lambda i: (i, 0)), pl.BlockSpec((D, tm), lambda i: (0, i))],
        out_shape=[SDS((t, D), BF16), SDS((D, t), BF16)], args=(x, g), vmem=VMEM_BIG)


FWD_SEGS = ((0, 8), (8, 3), (11, 2), (13, 4))


def _in_proj(h, w_full, rider):
    t = h.shape[0]
    tm = min(2048, t)

    def body(a_ref, b_ref, *o_refs):
        j = pl.program_id(1)
        for o_ref, (off, nblk) in zip(o_refs, FWD_SEGS):
            @pl.when((j >= off) & (j < off + nblk))
            def _():
                o_ref[...] = _dot(a_ref[...], b_ref[...]).astype(BF16)

    def out(seg):
        off, nblk = seg
        return pl.BlockSpec((tm, CB), lambda i, j: (i, jnp.clip(j - off, 0, nblk - 1)))

    res, got = _call(
        body, rider, name="in_proj", grid=(t // tm, N_IN // CB),
        in_specs=[pl.BlockSpec((tm, D), lambda i, j: (i, 0)), pl.BlockSpec((D, CB), lambda i, j: (0, j))],
        out_specs=[out(s) for s in FWD_SEGS], out_shape=[SDS((t, s[1] * CB), BF16) for s in FWD_SEGS],
        args=(h, w_full), vmem=VMEM_BIG)
    return res, got


def _conv_fwd(u_conv, conv_w, rider):
    t = u_conv.shape[0]
    tm = min(512, t)
    hb = tm // 16

    def body(v_ref, b_ref, c_ref, z_ref, hv_ref, hc_ref, w_ref, y_ref):
        i = pl.program_id(0)
        cv = c_ref[...].astype(F32) * v_ref[...].astype(F32)
        halo = hc_ref[...].astype(F32) * hv_ref[...].astype(F32)
        halo = jnp.where(i > 0, halo, 0.0)
        row = lax.broadcasted_iota(jnp.int32, (tm, 1), 0)
        s1 = jnp.where(row == 0, halo[15:16], pltpu.roll(cv, 1, 0))
        s2 = jnp.where(row == 0, halo[14:15], jnp.where(row == 1, halo[15:16], pltpu.roll(cv, 2, 0)))
        conv = w_ref[0:1, :] * s2 + w_ref[1:2, :] * s1 + w_ref[2:3, :] * cv
        z = z_ref[...].astype(F32)
        y_ref[...] = (b_ref[...].astype(F32) * conv * (z * _sigmoid(z))).astype(BF16)

    def col(k):
        return pl.BlockSpec((tm, D), lambda i: (i, k))

    def halo(k):
        return pl.BlockSpec((16, D), lambda i: (jnp.maximum(i * hb - 1, 0), k))

    (y_c,), got = _call(
        body, rider, name="conv_fwd", grid=(t // tm,),
        in_specs=[col(0), col(1), col(2), col(3), halo(0), halo(2), pl.BlockSpec((3, D), lambda i: (0, 0))],
        out_specs=[pl.BlockSpec((tm, D), lambda i: (i, 0))], out_shape=[SDS((t, D), BF16)],
        args=(u_conv, u_conv, u_conv, u_conv, u_conv, u_conv, conv_w), vmem=VMEM_BIG)
    return y_c, got


KVX = 4 * N_KV * 128


def _iota2(shape):
    return lax.broadcasted_iota(jnp.int32, shape, 0), lax.broadcasted_iota(jnp.int32, shape, 1)


def _head_sum(v):
    r, c = _iota2((128, 128))
    ones = ((r >> 6) == (c >> 6)).astype(BF16)
    hi = v.astype(BF16)
    lo = (v - hi.astype(F32)).astype(BF16)
    return jnp.concatenate([_dot(hi[:, g:g + 128], ones) + _dot(lo[:, g:g + 128], ones)
                            for g in range(0, v.shape[1], 128)], axis=1)


def _expand_mats():
    r, c = _iota2((N_KV * HEAD, N_KV * 128))
    base = ((r >> 6) << 7) + (r & 63)
    return (c == base).astype(BF16), (c == base + 64).astype(BF16)


def _fold_mat():
    r, c = _iota2((N_KV * 128, N_KV * HEAD))
    return (((r >> 7) == (c >> 6)) & ((r & 63) == (c & 63))).astype(BF16)


def _qkv_prep(u_qkv, qg_s, kg_t, rider):
    t = u_qkv.shape[0]
    tm = min(1024, t)

    def body(u_ref, qg_ref, kg_ref, qs_ref, kvx_ref):
        q = u_ref[:, 0:D].astype(F32)
        rq = lax.rsqrt(_head_sum(q * q) * (1.0 / HEAD) + EPS)
        qs_ref[...] = (q * rq * qg_ref[...]).astype(BF16)
        k = u_ref[:, D:D + 256].astype(F32)
        rk = lax.rsqrt(_head_sum(k * k) * (1.0 / HEAD) + EPS)
        kn = (k * rk * kg_ref[...]).astype(BF16)
        v = u_ref[:, D + 256:D + 512]
        e_lo, e_hi = _expand_mats()
        kvx_ref[:, 0:512] = _dot(kn, e_lo).astype(BF16)
        kvx_ref[:, 512:1024] = _dot(kn, e_hi).astype(BF16)
        kvx_ref[:, 1024:1536] = _dot(v, e_lo).astype(BF16)
        kvx_ref[:, 1536:2048] = _dot(v, e_hi).astype(BF16)

    return _call(
        body, rider, name="qkv_prep", grid=(t // tm,),
        in_specs=[pl.BlockSpec((tm, 1536), lambda i: (i, 0)), pl.BlockSpec((1, D), lambda i: (0, 0)),
                  pl.BlockSpec((1, 256), lambda i: (0, 0))],
        out_specs=[pl.BlockSpec((tm, D), lambda i: (i, 0)), pl.BlockSpec((tm, KVX), lambda i: (i, 0))],
        out_shape=[SDS((t, D), BF16), SDS((t, KVX), BF16)], args=(u_qkv, qg_s, kg_t), vmem=VMEM_BIG)


def _band_bias():
    j, r = _iota2((2 * BLK, 2 * BLK))
    diff = (r & (BLK - 1)) - j + BLK
    band = (diff >= 0) & (diff < BLK)
    return jnp.stack([jnp.where(band & (j >= BLK), 0.0, NEG), jnp.where(band, 0.0, NEG)]).astype(F32)


def _pair_rows(ref_or_val, hk):
    return jnp.concatenate([ref_or_val[:, 256 * hk:256 * hk + 128], ref_or_val[:, 256 * hk + 128:256 * hk + 256]], axis=0)


def _sink_row(sink_ref, hk, half):
    return jnp.concatenate([jnp.full((1, BLK), sink_ref[0, GROUP * hk + half], F32),
                            jnp.full((1, BLK), sink_ref[0, GROUP * hk + 2 + half], F32)], axis=1)


def _kv_operands(kvb, hk, half):
    return (kvb[:, 512 * half + 128 * hk:512 * half + 128 * hk + 128],
            kvb[:, 1024 + 512 * half + 128 * hk:1024 + 512 * half + 128 * hk + 128])


def _attn_fwd(qs, kvx, u_za, sinks, bias, rider):
    t = qs.shape[0]
    nb = t // BLK

    def body(q_ref, kc_ref, kp_ref, za_ref, sink_ref, bias_ref, o_ref, lse_ref):
        kvb = jnp.concatenate([kp_ref[...], kc_ref[...]], axis=0)
        bias_v = bias_ref[...]
        key0 = lax.broadcasted_iota(jnp.int32, (2 * BLK, 1), 0) == 0
        ones = jnp.ones((2 * BLK, 128), BF16)
        cols = []
        for hk in range(N_KV):
            qpp = _pair_rows(q_ref, hk)
            opp = None
            for half in range(2):
                kx, vx = _kv_operands(kvb, hk, half)
                s = _dot_nt(kx, qpp) + bias_v
                sink = _sink_row(sink_ref, hk, half)
                m = jnp.maximum(jnp.max(s, axis=0, keepdims=True), sink)
                p = jnp.exp(s - m)
                es = jnp.exp(sink - m)
                lse_ref[0, 2 * hk + half:2 * hk + half + 1, :] = m + jnp.log(jnp.sum(p, axis=0, keepdims=True) + es)
                pe = jnp.where(key0, es, p).astype(BF16)
                rhs = jnp.concatenate([jnp.where(key0, jnp.zeros_like(vx), vx), ones], axis=1)
                nd = _dot_tn(pe, rhs)
                o = nd[:, :128] * (1.0 / nd[:, 128:])
                opp = o if opp is None else opp + o
            cols += [opp[:BLK], opp[BLK:]]
        za = za_ref[...].astype(F32)
        o_ref[...] = (jnp.concatenate(cols, axis=1) * (za * _sigmoid(za))).astype(BF16)

    prev = lambda n: jnp.maximum(n - 1, 0)
    (o, lse), got = _call(
        body, rider, name="attn_fwd", grid=(nb,),
        in_specs=[pl.BlockSpec((BLK, D), lambda n: (n, 0)),
                  pl.BlockSpec((BLK, KVX), lambda n: (n, 0)), pl.BlockSpec((BLK, KVX), lambda n: (prev(n), 0)),
                  pl.BlockSpec((BLK, D), lambda n: (n, 0)), pl.BlockSpec(memory_space=pltpu.SMEM),
                  pl.BlockSpec((None, 2 * BLK, 2 * BLK), lambda n: (jnp.minimum(n, 1), 0, 0))],
        out_specs=[pl.BlockSpec((BLK, D), lambda n: (n, 0)), pl.BlockSpec((1, 8, 2 * BLK), lambda n: (n, 0, 0))],
        out_shape=[SDS((t, D), BF16), SDS((nb, 8, 2 * BLK), F32)],
        args=(qs, kvx, kvx, u_za, sinks, bias), vmem=VMEM_BIG)
    return o, lse, got


def _out_proj_fwd(x, y_c, o, u_gl, gate_b, w_sm, rider):
    t = x.shape[0]
    tm = min(512, t)

    def body(x_ref, yc_ref, o_ref, gla_ref, glb_ref, gb_ref, wco_ref, wao_ref, wout_ref,
             xn_ref, ya_ref, yb_ref, mg_ref):
        ya = _dot(yc_ref[...], wco_ref[...])
        yb = _dot(o_ref[...], wao_ref[...])
        gb = gb_ref[...]
        ga_ = _sigmoid(gla_ref[...].astype(F32) + gb[:, :D])
        gb_ = _sigmoid(glb_ref[...].astype(F32) + gb[:, D:])
        merged = (ga_ * ya + gb_ * yb).astype(BF16)
        ya_ref[...] = ya.astype(BF16)
        yb_ref[...] = yb.astype(BF16)
        mg_ref[...] = merged
        xn_ref[...] = x_ref[...] + _dot(merged, wout_ref[...])

    row = pl.BlockSpec((tm, D), lambda i: (i, 0))
    wspec = lambda a: pl.BlockSpec((None, D, D), lambda i: (a, 0, 0))
    return _call(
        body, rider, name="out_proj_fwd", grid=(t // tm,),
        in_specs=[row, row, row, pl.BlockSpec((tm, D), lambda i: (i, 0)), pl.BlockSpec((tm, D), lambda i: (i, 1)),
                  pl.BlockSpec((1, 2 * D), lambda i: (0, 0)), wspec(0), wspec(1), wspec(2)],
        out_specs=[row, row, row, row],
        out_shape=[SDS((t, D), F32), SDS((t, D), BF16), SDS((t, D), BF16), SDS((t, D), BF16)],
        args=(x, y_c, o, u_gl, u_gl, gate_b, w_sm, w_sm, w_sm), vmem=VMEM_BIG)


def _loss_head(y, tgt):
    t = y.shape[0]
    tm = min(1024, t)

    def body(y_ref, t_ref, dy_ref, acc_ref):
        @pl.when(pl.program_id(0) == 0)
        def _():
            acc_ref[...] = jnp.zeros_like(acc_ref)
        err = y_ref[...] - t_ref[...]
        dy_ref[...] = err * (1.0 / D)
        sq = _fold8(err * err)
        tot = sq[:, 0:128]
        for k in range(1, D // 128):
            tot = tot + sq[:, 128 * k:128 * (k + 1)]
        acc_ref[...] += tot

    row = pl.BlockSpec((tm, D), lambda i: (i, 0))
    return pl.pallas_call(
        body, name="loss_head", grid=(t // tm,), in_specs=[row, row],
        out_specs=[row, pl.BlockSpec((8, 128), lambda i: (0, 0))],
        out_shape=[SDS((t, D), F32), SDS((8, 128), F32)], compiler_params=_cp("arbitrary"),
    )(y, tgt)


def _out_proj_bwd(dout, y_a, y_b, u_gl, gate_b, w_sm, rider):
    t = dout.shape[0]
    tm = min(512, t)

    def body(do_ref, ya_ref, yb_ref, gla_ref, glb_ref, gb_ref, wco_ref, wao_ref, wout_ref,
             dya_ref, dyb_ref, dgl_ref, dyc_ref, dob_ref, dgb_ref):
        @pl.when(pl.program_id(0) == 0)
        def _():
            dgb_ref[...] = jnp.zeros_like(dgb_ref)
        dm = _dot_nt(do_ref[...].astype(BF16), wout_ref[...])
        gb = gb_ref[...]
        ga_ = _sigmoid(gla_ref[...].astype(F32) + gb[:, :D])
        gb_ = _sigmoid(glb_ref[...].astype(F32) + gb[:, D:])
        dya = (ga_ * dm).astype(BF16)
        dyb = (gb_ * dm).astype(BF16)
        dgla = ya_ref[...].astype(F32) * dm * (ga_ * (1.0 - ga_))
        dglb = yb_ref[...].astype(F32) * dm * (gb_ * (1.0 - gb_))
        dya_ref[...] = dya
        dyb_ref[...] = dyb
        dgl_ref[:, :D] = dgla.astype(BF16)
        dgl_ref[:, D:] = dglb.astype(BF16)
        dgb_ref[:, :D] += _fold8(dgla)
        dgb_ref[:, D:] += _fold8(dglb)
        dyc_ref[...] = _dot_nt(dya, wco_ref[...]).astype(BF16)
        dob_ref[...] = _dot_nt(dyb, wao_ref[...]).astype(BF16)

    row = pl.BlockSpec((tm, D), lambda i: (i, 0))
    wspec = lambda a: pl.BlockSpec((None, D, D), lambda i: (a, 0, 0))
    return _call(
        body, rider, name="out_proj_bwd", grid=(t // tm,),
        in_specs=[row, row, row, pl.BlockSpec((tm, D), lambda i: (i, 0)), pl.BlockSpec((tm, D), lambda i: (i, 1)),
                  pl.BlockSpec((1, 2 * D), lambda i: (0, 0)), wspec(0), wspec(1), wspec(2)],
        out_specs=[row, row, pl.BlockSpec((tm, 2 * D), lambda i: (i, 0)), row, row,
                   pl.BlockSpec((8, 2 * D), lambda i: (0, 0))],
        out_shape=[SDS((t, D), BF16), SDS((t, D), BF16), SDS((t, 2 * D), BF16), SDS((t, D), BF16), SDS((t, D), BF16),
                   SDS((8, 2 * D), F32)],
        args=(dout, y_a, y_b, u_gl, u_gl, gate_b, w_sm, w_sm, w_sm), vmem=VMEM_BIG)


def _small_wgrads(y_c, d_ya, o, d_yb, merged, dout):
    t = y_c.shape[0]
    tk = min(512, t)

    def body(yc_ref, dya_ref, o_ref, dyb_ref, mg_ref, do_ref, g_ref):
        @pl.when(pl.program_id(0) == 0)
        def _():
            g_ref[...] = jnp.zeros_like(g_ref)
        g_ref[0] += _dot_tn(yc_ref[...], dya_ref[...])
        g_ref[1] += _dot_tn(o_ref[...], dyb_ref[...])
        g_ref[2] += _dot_tn(mg_ref[...], do_ref[...].astype(BF16))

    row = pl.BlockSpec((tk, D), lambda k: (k, 0))
    return pl.pallas_call(
        body, name="small_wgrads", grid=(t // tk,), in_specs=[row] * 6,
        out_specs=pl.BlockSpec((3, D, D), lambda k: (0, 0, 0)), out_shape=SDS((3, D, D), F32),
        compiler_params=_cp("arbitrary", vmem=VMEM_BIG),
    )(y_c, d_ya, o, d_yb, merged, dout)


def _conv_bwd(d_yc, u_conv, conv_w, rider):
    t = d_yc.shape[0]
    tm = min(512, t)
    hb = tm // 16
    last_halo = t // 16 - 1
    n_steps = t // tm

    def body(dy_ref, v_ref, b_ref, c_ref, z_ref, hv_ref, hc_ref, ndy_ref, nb_ref, nz_ref, w_ref, du_ref, dw_ref):
        i = pl.program_id(0)

        @pl.when(i == 0)
        def _():
            dw_ref[...] = jnp.zeros_like(dw_ref)
        v, c = v_ref[...].astype(F32), c_ref[...].astype(F32)
        b, z = b_ref[...].astype(F32), z_ref[...].astype(F32)
        cv = c * v
        halo = jnp.where(i > 0, hc_ref[...].astype(F32) * hv_ref[...].astype(F32), 0.0)
        row = lax.broadcasted_iota(jnp.int32, (tm, 1), 0)
        s1 = jnp.where(row == 0, halo[15:16], pltpu.roll(cv, 1, 0))
        s2 = jnp.where(row == 0, halo[14:15], jnp.where(row == 1, halo[15:16], pltpu.roll(cv, 2, 0)))
        w0, w1, w2 = w_ref[0:1, :], w_ref[1:2, :], w_ref[2:3, :]
        conv = w0 * s2 + w1 * s1 + w2 * cv
        sig = _sigmoid(z)
        sz = z * sig
        dsz = sig * (1.0 + z * (1.0 - sig))
        dy = dy_ref[...].astype(F32)
        dconv = dy * b * sz
        nz = nz_ref[...].astype(F32)
        nxt = ndy_ref[...].astype(F32) * nb_ref[...].astype(F32) * (nz * _sigmoid(nz))
        nxt = jnp.where(i < n_steps - 1, nxt, 0.0)
        a1 = jnp.where(row == tm - 1, nxt[0:1], pltpu.roll(dconv, tm - 1, 0))
        a2 = jnp.where(row == tm - 2, nxt[0:1], jnp.where(row == tm - 1, nxt[1:2], pltpu.roll(dconv, tm - 2, 0)))
        dcv = w2 * dconv + w1 * a1 + w0 * a2
        du_ref[:, 0:D] = (dcv * c).astype(BF16)
        du_ref[:, D:2 * D] = (dy * conv * sz).astype(BF16)
        du_ref[:, 2 * D:3 * D] = (dcv * v).astype(BF16)
        du_ref[:, 3 * D:4 * D] = (dy * b * conv * dsz).astype(BF16)
        r8 = lax.broadcasted_iota(jnp.int32, (8, 1), 0)
        dw_ref[...] += jnp.where(r8 == 0, jnp.sum(dconv * s2, axis=0, keepdims=True),
                                 jnp.where(r8 == 1, jnp.sum(dconv * s1, axis=0, keepdims=True),
                                           jnp.where(r8 == 2, jnp.sum(dconv * cv, axis=0, keepdims=True), 0.0)))

    def col(k):
        return pl.BlockSpec((tm, D), lambda i: (i, k))

    def halo(k):
        return pl.BlockSpec((16, D), lambda i: (jnp.maximum(i * hb - 1, 0), k))

    def nxt(k):
        return pl.BlockSpec((16, D), lambda i: (jnp.minimum((i + 1) * hb, last_halo), k))

    return _call(
        body, rider, name="conv_bwd", grid=(t // tm,),
        in_specs=[col(0), col(0), col(1), col(2), col(3), halo(0), halo(2), nxt(0), nxt(1), nxt(3),
                  pl.BlockSpec((3, D), lambda i: (0, 0))],
        out_specs=[pl.BlockSpec((tm, 4 * D), lambda i: (i, 0)), pl.BlockSpec((8, D), lambda i: (0, 0))],
        out_shape=[SDS((t, 4 * D), BF16), SDS((8, D), F32)],
        args=(d_yc, u_conv, u_conv, u_conv, u_conv, u_conv, u_conv, d_yc, u_conv, u_conv, conv_w), vmem=VMEM_BIG)


def _attn_bwd(d_o, qs, kvx, u_za, lse, sinks, bias, rider):
    t = d_o.shape[0]
    nb = t // BLK

    def body(q_ref, kc_ref, kp_ref, za_ref, do_ref, lse_ref, sink_ref, bias_ref,
             dq_ref, dkv_ref, dza_ref, dsk_ref, carry_ref):
        n = pl.program_id(0)

        @pl.when(n == 0)
        def _():
            carry_ref[...] = jnp.zeros_like(carry_ref)
            dsk_ref[...] = jnp.zeros_like(dsk_ref)

        live = n < nb
        kvb = jnp.concatenate([kp_ref[...], kc_ref[...]], axis=0)
        bias_v = bias_ref[...]
        za = za_ref[...].astype(F32)
        sig = _sigmoid(za)
        dsa = sig * (1.0 + za * (1.0 - sig))
        do = jnp.where(live, do_ref[...].astype(F32), 0.0)
        dattn = (do * (za * sig)).astype(BF16)
        lo_lanes = lax.broadcasted_iota(jnp.int32, (1, 128), 1) < HEAD
        dq_cols, attn_cols, dk_cols, dv_cols, dsk_rows = [], [], [], [], []
        for hk in range(N_KV):
            qpp = _pair_rows(q_ref, hk)
            dapp = _pair_rows(dattn, hk)
            probs, dss, xk, xv = [], [], [], []
            for half in range(2):
                kx, vx = _kv_operands(kvb, hk, half)
                lse = lse_ref[0, 2 * hk + half:2 * hk + half + 1, :]
                prob = jnp.exp(_dot_nt(kx, qpp) + bias_v - lse)
                psink = jnp.exp(_sink_row(sink_ref, hk, half) - lse)
                tdp = prob * _dot_nt(vx, dapp)
                drow = jnp.sum(tdp, axis=0, keepdims=True)
                ds = (tdp - prob * drow).astype(BF16)
                prob_b = prob.astype(BF16)
                xk.append(_dot(ds, qpp))
                xv.append(_dot(prob_b, dapp))
                probs.append(prob_b)
                dss.append(ds)
                dsk_rows.append(-psink * drow)
            kcat = jnp.concatenate([kvb[:, 128 * hk:128 * hk + 128], kvb[:, 512 + 128 * hk:512 + 128 * hk + 128]], axis=0)
            vcat = jnp.concatenate([kvb[:, 1024 + 128 * hk:1024 + 128 * hk + 128],
                                    kvb[:, 1536 + 128 * hk:1536 + 128 * hk + 128]], axis=0)
            app = _dot_tn(jnp.concatenate(probs, axis=0), vcat)
            dqpp = _dot_tn(jnp.concatenate(dss, axis=0), kcat)
            dq_cols += [dqpp[:BLK], dqpp[BLK:]]
            attn_cols += [app[:BLK], app[BLK:]]
            dk_cols.append(jnp.where(lo_lanes, xk[0], xk[1]))
            dv_cols.append(jnp.where(lo_lanes, xv[0], xv[1]))

        @pl.when(live)
        def _():
            dq_ref[...] = jnp.concatenate(dq_cols, axis=1).astype(BF16)
            dza_ref[...] = (do * jnp.concatenate(attn_cols, axis=1) * dsa).astype(BF16)

        band = jnp.concatenate(dk_cols + dv_cols, axis=1)
        dkv_ref[...] = (band[:BLK] + carry_ref[...]).astype(BF16)
        carry_ref[...] = band[BLK:]
        dsk_ref[...] += jnp.broadcast_to(jnp.concatenate(dsk_rows, axis=1), (8, 2 * N_KV * 2 * BLK))

    cur = lambda n: jnp.minimum(n, nb - 1)
    prev = lambda n: jnp.maximum(n - 1, 0)
    return _call(
        body, rider, name="attn_bwd", grid=(nb + 1,),
        in_specs=[pl.BlockSpec((BLK, D), lambda n: (cur(n), 0)),
                  pl.BlockSpec((BLK, KVX), lambda n: (cur(n), 0)), pl.BlockSpec((BLK, KVX), lambda n: (prev(n), 0)),
                  pl.BlockSpec((BLK, D), lambda n: (cur(n), 0)), pl.BlockSpec((BLK, D), lambda n: (cur(n), 0)),
                  pl.BlockSpec((1, 8, 2 * BLK), lambda n: (cur(n), 0, 0)), pl.BlockSpec(memory_space=pltpu.SMEM),
                  pl.BlockSpec((None, 2 * BLK, 2 * BLK), lambda n: (jnp.minimum(n, 1), 0, 0))],
        out_specs=[pl.BlockSpec((BLK, D), lambda n: (cur(n), 0)), pl.BlockSpec((BLK, D), lambda n: (prev(n), 0)),
                   pl.BlockSpec((BLK, D), lambda n: (cur(n), 0)), pl.BlockSpec((8, 2 * D), lambda n: (0, 0))],
        out_shape=[SDS((t, D), BF16), SDS((t, D), BF16), SDS((t, D), BF16), SDS((8, 2 * D), F32)],
        scratch_shapes=[pltpu.VMEM((BLK, D), F32)],
        args=(qs, kvx, kvx, u_za, d_o, lse, sinks, bias), vmem=VMEM_BIG)


def _qkv_post(u_qkv, dqs, dkv, dza, qg_s, kg_t, rider):
    t = u_qkv.shape[0]
    tm = min(512, t)

    def norm_bwd(x, dy, g):
        r = lax.rsqrt(_head_sum(x * x) * (1.0 / HEAD) + EPS)
        xhat = x * r
        dxh = dy * g
        return r * (dxh - xhat * (_head_sum(dxh * xhat) * (1.0 / HEAD))), _fold8(dy * xhat)

    def body(u_ref, dq_ref, dkv_ref, dza_ref, qg_ref, kg_ref, du_ref, dqg_ref, dkg_ref):
        @pl.when(pl.program_id(0) == 0)
        def _():
            dqg_ref[...] = jnp.zeros_like(dqg_ref)
            dkg_ref[...] = jnp.zeros_like(dkg_ref)
        dq, gq = norm_bwd(u_ref[:, 0:D].astype(F32), dq_ref[...].astype(F32), qg_ref[...])
        fold = _fold_mat()
        dk, gk = norm_bwd(u_ref[:, D:D + 256].astype(F32), _dot(dkv_ref[:, 0:512], fold), kg_ref[...])
        du_ref[:, 0:D] = dq.astype(BF16)
        du_ref[:, D:D + 256] = dk.astype(BF16)
        du_ref[:, D + 256:D + 512] = _dot(dkv_ref[:, 512:1024], fold).astype(BF16)
        du_ref[:, D + 512:2 * D + 512] = dza_ref[...]
        dqg_ref[...] += gq
        dkg_ref[...] += gk

    row = pl.BlockSpec((tm, D), lambda i: (i, 0))
    return _call(
        body, rider, name="qkv_post", grid=(t // tm,),
        in_specs=[pl.BlockSpec((tm, 1536), lambda i: (i, 0)), row, row, row,
                  pl.BlockSpec((1, D), lambda i: (0, 0)), pl.BlockSpec((1, 256), lambda i: (0, 0))],
        out_specs=[pl.BlockSpec((tm, 2560), lambda i: (i, 0)), pl.BlockSpec((8, D), lambda i: (0, 0)),
                   pl.BlockSpec((8, 256), lambda i: (0, 0))],
        out_shape=[SDS((t, 2560), BF16), SDS((8, D), F32), SDS((8, 256), F32)],
        args=(u_qkv, dqs, dkv, dza, qg_s, kg_t), vmem=VMEM_BIG)


N_GRAN = N_IN // CB
DU_COLS = ((0, 4096), (4096, 6656), (6656, N_IN))


def _du_granule(j):
    return jnp.clip(j, 0, 7), jnp.clip(j - 8, 0, 4), jnp.clip(j - 13, 0, 3)


def _du_select(j, refs, fn):
    for ref, lo, hi in zip(refs, (0, 8, 13), (8, 13, 17)):
        @pl.when((j >= lo) & (j < hi))
        def _():
            fn(ref)


def _in_proj_bwd(du, w_full, x, g, dout, rider):
    t = du[0].shape[0]
    tn = min(256, t)

    def body(a0, a1, a2, w_hbm, x_ref, g_ref, do_ref, dx_ref, dg_ref, w_ref, sem):
        @pl.when(pl.program_id(0) == 0)
        def _():
            cp = pltpu.make_async_copy(w_hbm, w_ref, sem)
            cp.start()
            dg_ref[...] = jnp.zeros_like(dg_ref)
            cp.wait()
        acc = None
        for a_ref, (lo, hi) in zip((a0, a1, a2), DU_COLS):
            part = _dot_nt(w_ref[:, lo:hi], a_ref[...])
            acc = part if acc is None else acc + part
        dh = acc.T
        xv = x_ref[...]
        r = lax.rsqrt(jnp.mean(xv * xv, axis=-1, keepdims=True) + EPS)
        xhat = xv * r
        dg_ref[...] += _fold8(dh * xhat)
        dxh = dh * g_ref[...]
        dx_ref[...] = do_ref[...] + r * (dxh - xhat * jnp.mean(dxh * xhat, axis=-1, keepdims=True))

    row = pl.BlockSpec((tn, D), lambda i: (i, 0))
    return _call(
        body, rider, name="in_proj_bwd", grid=(t // tn,),
        in_specs=[pl.BlockSpec((tn, hi - lo), lambda i: (i, 0)) for lo, hi in DU_COLS]
        + [ANY, row, pl.BlockSpec((1, D), lambda i: (0, 0)), row],
        out_specs=[row, pl.BlockSpec((8, D), lambda i: (0, 0))], out_shape=[SDS((t, D), F32), SDS((8, D), F32)],
        scratch_shapes=[pltpu.VMEM((D, N_IN), BF16), pltpu.SemaphoreType.DMA(())],
        args=(*du, w_full, x, g, dout), vmem=VMEM_BIG)


def _in_proj_wgrad(ht, du, rider):
    t = ht.shape[1]
    tk = min(4096, t)
    n_k = t // tk

    def body(h_ref, b0, b1, b2, g_ref):
        j, k = pl.program_id(0), pl.program_id(1)

        if n_k > 1:
            @pl.when(k == 0)
            def _():
                g_ref[...] = jnp.zeros_like(g_ref)

        def add(b_ref):
            if n_k > 1:
                g_ref[...] += _dot(h_ref[...], b_ref[...])
            else:
                g_ref[...] = _dot(h_ref[...], b_ref[...])
        _du_select(j, (b0, b1, b2), add)

    seg = lambda q: pl.BlockSpec((tk, CB), lambda j, k: (k, _du_granule(j)[q]))
    (g,), got = _call(
        body, rider, name="in_proj_wgrad", grid=(N_GRAN, t // tk),
        in_specs=[pl.BlockSpec((D, tk), lambda j, k: (0, k)), seg(0), seg(1), seg(2)],
        out_specs=[pl.BlockSpec((D, CB), lambda j, k: (0, j))], out_shape=[SDS((D, N_IN), F32)],
        args=(ht, *du), vmem=VMEM_BIG)
    return g, got


def _swap_rider(g_in, g_sm):
    def copies(ins, outs, send, recv, base=0):
        x, y, c = _mesh_pos()
        cps = []
        for src, dst in zip(ins, outs):
            half = src.at[1 - c] if len(src.shape) == 3 else src.at[:, :, 1 - c]
            cps.append(_rcopy(half, dst, send, recv, base + len(cps), (x, y, 1 - c)))
        return cps

    arrays = [g for g in (g_in, g_sm) if g is not None]
    shapes = [SDS((512, N_IN), F32) if len(g.shape) == 3 else SDS((3, 4, 128, D), F32) for g in arrays]
    return _Rider(arrays, shapes, len(arrays), copies)


def _add_halves_in(cc_idx, g_in, r_in):
    def body(cc_ref, a_ref, b_ref, f_ref, h_ref):
        s = a_ref[...] + b_ref[...]
        h_ref[...] = s.astype(BF16)

        @pl.when(pl.program_id(1) == cc_ref[1])
        def _():
            f_ref[...] = s

    blk = pl.BlockSpec((256, SH_IN), lambda i, j, cc: (i, j))
    return pl.pallas_call(
        body, name="add_halves_in",
        grid_spec=pltpu.PrefetchScalarGridSpec(
            num_scalar_prefetch=1, grid=(2, 4),
            in_specs=[pl.BlockSpec((None, 256, SH_IN), lambda i, j, cc: (cc[0], i, j)), blk],
            out_specs=[pl.BlockSpec((256, SH_IN), lambda i, j, cc: (i, 0)), blk]),
        out_shape=[SDS((512, SH_IN), F32), SDS((512, N_IN), BF16)],
        compiler_params=_cp("arbitrary", "arbitrary", vmem=VMEM_BIG),
    )(cc_idx, g_in, r_in)


def _add_halves_sm(c_idx, g_sm, r_sm):
    def body(c_ref, a_ref, b_ref, f_ref, h_ref):
        s = a_ref[...] + b_ref[...]
        f_ref[...] = s
        h_ref[...] = s.astype(BF16)

    blk = pl.BlockSpec((1, 4, 128, D), lambda a, c: (a, 0, 0, 0))
    return pl.pallas_call(
        body, name="add_halves_sm",
        grid_spec=pltpu.PrefetchScalarGridSpec(
            num_scalar_prefetch=1, grid=(3,),
            in_specs=[pl.BlockSpec((1, 4, None, 128, D), lambda a, c: (a, 0, c[0], 0, 0)), blk], out_specs=[blk, blk]),
        out_shape=[SDS((3, 4, 128, D), F32), SDS((3, 4, 128, D), BF16)], compiler_params=_cp("parallel"),
    )(c_idx, g_sm, r_sm)


def _scatter_rider(h_in, h_sm):
    def copies(ins, outs, send, recv, base=0):
        x, y, c = _mesh_pos()
        cps = []
        for src, dst in zip(ins, outs):
            for k, chip in enumerate(_other_chips(x, y)):
                their = 2 * chip[0] + chip[1]
                part = src.at[:, pl.ds(pl.multiple_of(their * SH_IN, 128), SH_IN)] if len(src.shape) == 2 else src.at[:, their]
                cps.append(_rcopy(part, dst.at[k], send, recv, base + len(cps), (*chip, c)))
        return cps

    arrays = [h for h in (h_in, h_sm) if h is not None]
    shapes = [SDS((3, 512, SH_IN), BF16) if len(h.shape) == 2 else SDS((3, 3, 128, D), BF16) for h in arrays]
    return _Rider(arrays, shapes, 3 * len(arrays), copies)


def _ride_alone(rider, name):
    return _hosted_call(None, rider, name=name, grid=(), in_specs=[], out_specs=[], out_shape=[], args=())[1]


def _final_sum_in(f_in, r_in):
    def body(a_ref, r_ref, o_ref):
        o_ref[...] = a_ref[...] + r_ref[0].astype(F32) + r_ref[1].astype(F32) + r_ref[2].astype(F32)

    return pl.pallas_call(
        body, name="final_sum_in", grid=(4,),
        in_specs=[pl.BlockSpec((128, SH_IN), lambda i: (i, 0)), pl.BlockSpec((3, 128, SH_IN), lambda i: (0, i, 0))],
        out_specs=pl.BlockSpec((128, SH_IN), lambda i: (i, 0)),
        out_shape=SDS((512, SH_IN), F32), compiler_params=_cp("parallel"),
    )(f_in, r_in)


def _final_sum_sm(chip_idx, f_sm, r_sm):
    def body(j_ref, a_ref, r_ref, o_ref):
        o_ref[...] = a_ref[...] + r_ref[0].astype(F32) + r_ref[1].astype(F32) + r_ref[2].astype(F32)

    return pl.pallas_call(
        body, name="final_sum_sm",
        grid_spec=pltpu.PrefetchScalarGridSpec(
            num_scalar_prefetch=1, grid=(3,),
            in_specs=[pl.BlockSpec((1, None, 128, D), lambda a, j: (a, j[0], 0, 0)),
                      pl.BlockSpec((3, 1, 128, D), lambda a, j: (0, a, 0, 0))],
            out_specs=pl.BlockSpec((1, 128, D), lambda a, j: (a, 0, 0))),
        out_shape=SDS((3, 128, D), F32), compiler_params=_cp("parallel"),
    )(chip_idx, f_sm, r_sm)


def _join_halves(t_in, t_sm):
    n_cp = N_LAYERS * 4
    args, plan = [], []
    for l in range(N_LAYERS):
        if t_in[l] is not None:
            plan.append((l, 0, len(args)))
            args.append(t_in[l])
        plan += [(l, a, len(args)) for a in (1, 2, 3)]
        args.append(t_sm[l])

    def body(*refs):
        ins, outs = refs[:len(args)], refs[len(args):len(args) + 4]
        send, recv, loc_in, loc_out, stage_in, stage_sm = refs[len(args) + 4:]
        x, y, c = _mesh_pos()
        cps, own = [], []

        def place(l, a, half):
            rows = 512 if a == 0 else 128
            return outs[a].at[l, pl.ds(pl.multiple_of(half * rows, rows), rows), :]

        for s, (l, a, k) in enumerate(plan):
            src = ins[k] if a == 0 else ins[k].at[a - 1]
            own.append((src, place(l, a, c), min(a, 1)))
            cp = pltpu.make_async_remote_copy(src_ref=src, dst_ref=place(l, a, c), send_sem=send.at[s],
                                              recv_sem=recv.at[s], device_id=(x, y, 1 - c), device_id_type=MESH)
            cp.start()
            cps.append(cp)
        _staged_copies(own, (stage_in, stage_sm), loc_in, loc_out)
        for s, (l, a, k) in enumerate(plan):
            got = place(l, a, 1 - c)
            pltpu.make_async_remote_copy(src_ref=got, dst_ref=got, send_sem=send.at[s], recv_sem=recv.at[s],
                                         device_id=(x, y, 1 - c), device_id_type=MESH).wait_recv()
        for cp in cps:
            cp.wait_send()

    sm = SDS((N_LAYERS, SH_ROW, D), F32)
    return pl.pallas_call(
        body, name="join_halves", in_specs=[ANY] * len(args), out_specs=[ANY] * 4,
        out_shape=[SDS((N_LAYERS, D, SH_IN), F32), sm, sm, sm],
        scratch_shapes=[pltpu.SemaphoreType.DMA((n_cp,))] * 4
        + [pltpu.VMEM((2, 512, SH_IN), F32), pltpu.VMEM((2, 128, D), F32)],
        compiler_params=_cp(vmem=VMEM_BIG),
    )(*args)


def _adam_math(w, g, m, v):
    m = ADAM_B1 * m + (1.0 - ADAM_B1) * g
    v = ADAM_B2 * v + (1.0 - ADAM_B2) * (g * g)
    m_hat = m / (1.0 - ADAM_B1 ** ADAM_STEP)
    v_hat = v / (1.0 - ADAM_B2 ** ADAM_STEP)
    delta = -ADAM_LR * (m_hat / (jnp.sqrt(v_hat) + ADAM_EPS) + ADAM_WD * w)
    return delta, m, v


def _adamw_big(w, g, m, v, name):
    rows, cols = w.shape
    tr = 256

    def body(w_ref, g_ref, m_ref, v_ref, go_ref, d_ref, nm_ref, nv_ref):
        g = g_ref[...]
        go_ref[...] = g
        d_ref[...], nm_ref[...], nv_ref[...] = _adam_math(w_ref[...], g, m_ref[...], v_ref[...])

    blk = pl.BlockSpec((tr, cols), lambda i: (i, 0))
    return pl.pallas_call(
        body, name=name, grid=(rows // tr,), in_specs=[blk] * 4, out_specs=[blk] * 4,
        out_shape=[SDS((rows, cols), F32)] * 4, compiler_params=_cp("parallel", vmem=VMEM_BIG),
    )(w, g, m, v)


def _adamw_small(ws, gs, ms, vs):
    n = len(ws)

    def body(*refs):
        for k in range(n):
            w_ref, g_ref, m_ref, v_ref = (refs[q * n + k] for q in range(4))
            d, nm, nv = _adam_math(w_ref[...], g_ref[...], m_ref[...], v_ref[...])
            refs[4 * n + k][...] = d
            refs[5 * n + k][...] = nm
            refs[6 * n + k][...] = nv

    vm = pl.BlockSpec(memory_space=pltpu.VMEM)
    shapes = [SDS(w.shape, F32) for w in ws]
    res = pl.pallas_call(
        body, name="adamw_small", in_specs=[vm] * (4 * n), out_specs=[vm] * (3 * n), out_shape=shapes * 3,
    )(*ws, *gs, *ms, *vs)
    return res[:n], res[n:2 * n], res[2 * n:]


def _pad_rows(a, rows):
    flat = a.reshape(-1)
    return jnp.pad(flat, (0, rows * 128 - flat.shape[0])).reshape(rows, 128)


def kernel(x, norm_g, w_in, conv_w, q_norm_g, k_norm_g, sinks, w_conv_out, w_attn_out, gate_b, w_out, loss_target, m_norm_g, m_w_in, m_conv_w, m_q_norm_g, m_k_norm_g, m_sinks, m_w_conv_out, m_w_attn_out, m_gate_b, m_w_out, v_norm_g, v_w_in, v_conv_w, v_q_norm_g, v_k_norm_g, v_sinks, v_w_conv_out, v_w_attn_out, v_gate_b, v_w_out):
    xi, yi, ci = _mesh_pos()
    chip = 2 * xi + yi
    c_idx = jnp.reshape(ci, (1,)).astype(jnp.int32)
    chip_idx = jnp.reshape(chip, (1,)).astype(jnp.int32)
    cc_idx = jnp.stack([ci, chip]).astype(jnp.int32)
    t = x.shape[1]
    xs = [x.reshape(t, D)]
    tgt = loss_target.reshape(t, D)

    full_w = [[_cast_w_in(chip_idx, w_in, l), _cast_w_small(chip_idx, w_conv_out, w_attn_out, w_out, l)]
              for l in range(N_LAYERS)]
    conv32 = lax.dynamic_update_slice(jnp.zeros((32, D), F32), jnp.pad(conv_w.reshape(3 * N_LAYERS, SH_ROW), ((0, 20), (0, 0))),
                                      (0, chip * SH_ROW))
    qg_s = jnp.tile(q_norm_g, (1, N_Q)) * SCALE
    kg_t = jnp.tile(k_norm_g, (1, N_KV))
    bias = _band_bias()
    saved = []
    for l in range(N_LAYERS):
        nxt = full_w[l + 1] if l + 1 < N_LAYERS else None
        (h, ht), got = _rmsnorm_fwd(xs[l], norm_g[l:l + 1], _gather_rider([full_w[0][0], conv32], "N") if l == 0 else None)
        if l == 0:
            got = _ride_alone(_gather_rider(got, "F"), "gather_first_forward")
            full_w[0][0], conv32 = _ride_alone(_gather_rider(got, "B"), "gather_first_d2d")
            conv_full = conv32[:3 * N_LAYERS].reshape(N_LAYERS, 3, D)
        (u_conv, u_qkv, u_za, u_gl), got = _in_proj(h, full_w[l][0], _gather_rider(nxt, "N") if nxt else None)
        if nxt:
            nxt[0], nxt[1] = got
        (qs, kvx), got = _qkv_prep(u_qkv, qg_s[l:l + 1], kg_t[l:l + 1],
                                   _gather_rider(full_w[0][1:], "N") if l == 0 else None)
        y_c, got = _conv_fwd(u_conv, conv_full[l], _gather_rider(got, "F") if l == 0 else None)
        o, lse, got = _attn_fwd(qs, kvx, u_za, sinks[l:l + 1], bias, _merge_riders(
            _gather_rider(nxt, "FB1") if nxt else None, _gather_rider(got, "B") if l == 0 else None))
        if nxt:
            nxt[0], nxt[1] = got[:2]
        if l == 0:
            full_w[0][1] = got[-1]
        (x_next, y_a, y_b, merged), got = _out_proj_fwd(xs[l], y_c, o, u_gl, gate_b[l:l + 1], full_w[l][1],
                                                        _gather_rider(nxt, "B2") if nxt else None)
        if nxt:
            nxt[0], nxt[1] = got
        xs.append(x_next)
        saved.append((ht, u_conv, u_qkv, u_za, u_gl, y_c, o, y_a, y_b, merged, qs, kvx, lse))

    dout, sq = _loss_head(xs[N_LAYERS], tgt)

    small, t_in, t_sm = [None] * N_LAYERS, [None] * N_LAYERS, [None] * N_LAYERS
    halves = None

    for l in reversed(range(N_LAYERS)):
        w_full, w_sm = full_w[l]
        last = l == 0
        ht, u_conv, u_qkv, u_za, u_gl, y_c, o, y_a, y_b, merged, qs, kvx, lse = saved[l]
        (d_ya, d_yb, du_gl, d_yc, d_o, dgb), _ = _out_proj_bwd(dout, y_a, y_b, u_gl, gate_b[l:l + 1], w_sm, None)
        g_sm = _small_wgrads(y_c, d_ya, o, d_yb, merged, dout).reshape(3, 4, 2, 128, D)
        (du_conv, dcw), got = _conv_bwd(d_yc, u_conv, conv_full[l], _swap_rider(None, g_sm) if last else None)
        if last:
            f_sm0, h_sm0 = _add_halves_sm(c_idx, g_sm, got[0])
        (dqs, dkv, dza, dsk), got = _attn_bwd(d_o, qs, kvx, u_za, lse, sinks[l:l + 1], bias,
                                              _scatter_rider(halves[1], None) if halves else None)
        if halves:
            t_in[l + 1] = _final_sum_in(halves[0], got[0])
        dsk = jnp.sum(dsk[0].reshape(N_KV, 2, 2, BLK), axis=-1).transpose(0, 2, 1).reshape(N_Q)
        (du_attn, dqg, dkg), _ = _qkv_post(u_qkv, dqs, dkv, dza, qg_s[l:l + 1], kg_t[l:l + 1], None)
        du = (du_conv, du_attn, du_gl)
        g_in, got = _in_proj_wgrad(ht, du, _merge_riders(
            _scatter_rider(None, halves[3]) if halves else None, _scatter_rider(None, h_sm0) if last else None))
        g_in = g_in.reshape(2, 512, N_IN)
        if halves:
            t_sm[l + 1] = _final_sum_sm(chip_idx, halves[2], got[0])
        if last:
            t_sm[0] = _final_sum_sm(chip_idx, f_sm0, got[-1])
        if last:
            f_in0, h_in0 = _add_halves_in(cc_idx, g_in, _ride_alone(_swap_rider(g_in, None), "swap_last")[0])
        (dout, dng), got = _in_proj_bwd(du, w_full, xs[l], norm_g[l:l + 1], dout,
                                        _scatter_rider(h_in0, None) if last else _swap_rider(g_in, g_sm))
        if last:
            t_in[0] = _final_sum_in(f_in0, got[0])
        else:
            halves = _add_halves_in(cc_idx, g_in, got[0]) + _add_halves_sm(c_idx, g_sm, got[1])
        small[l] = (jnp.sum(dng, axis=0), SCALE * jnp.sum(dqg.reshape(8 * N_Q, HEAD), axis=0),
                    jnp.sum(dkg.reshape(8 * N_KV, HEAD), axis=0), dsk, jnp.sum(dgb, axis=0), dcw[:3])
    grad_x = dout.reshape(1, t, D)

    stack = lambda k: jnp.stack([small[l][k] for l in range(N_LAYERS)])
    pack = jnp.concatenate([_pad_rows(stack(0), 32), _pad_rows(stack(1), 8), _pad_rows(stack(2), 8),
                            _pad_rows(stack(3), 8), _pad_rows(stack(4), 64), _pad_rows(stack(5), 96),
                            _pad_rows(jnp.sum(sq) * (0.5 / D), 8)], axis=0)
    red = _allreduce_small(pack)
    loss = red[216, 0]
    g_norm_g = red[0:32].reshape(N_LAYERS, D)
    g_q_norm_g = red[32:40].reshape(-1)[:N_LAYERS * HEAD].reshape(N_LAYERS, HEAD)
    g_k_norm_g = red[40:48].reshape(-1)[:N_LAYERS * HEAD].reshape(N_LAYERS, HEAD)
    g_sinks = red[48:56].reshape(-1)[:N_LAYERS * N_Q].reshape(N_LAYERS, N_Q)
    g_gate_b = red[56:120].reshape(N_LAYERS, 2 * D)
    g_conv_full = red[120:216].reshape(N_LAYERS, 3, D)
    g_conv_w = lax.dynamic_slice(g_conv_full, (0, 0, chip * SH_ROW), (N_LAYERS, 3, SH_ROW))

    g_w_in, g_w_co, g_w_ao, g_w_out = _join_halves(t_in, t_sm)

    r_in = N_LAYERS * D
    g_w_in, d_in, nm_in, nv_in = (a.reshape(N_LAYERS, D, SH_IN) for a in _adamw_big(
        w_in.reshape(r_in, SH_IN), g_w_in.reshape(r_in, SH_IN), m_w_in.reshape(r_in, SH_IN),
        v_w_in.reshape(r_in, SH_IN), "adamw_w_in"))
    r_sm = N_LAYERS * SH_ROW
    big = {}
    for nm, w, g, m, v in (("co", w_conv_out, g_w_co, m_w_conv_out, v_w_conv_out),
                           ("ao", w_attn_out, g_w_ao, m_w_attn_out, v_w_attn_out),
                           ("out", w_out, g_w_out, m_w_out, v_w_out)):
        big[nm] = tuple(a.reshape(N_LAYERS, SH_ROW, D) for a in _adamw_big(
            w.reshape(r_sm, D), g.reshape(r_sm, D), m.reshape(r_sm, D), v.reshape(r_sm, D), "adamw_w_small"))
    g_w_co, g_w_ao, g_w_out = big["co"][0], big["ao"][0], big["out"][0]
    sm_w = [norm_g, conv_w, q_norm_g, k_norm_g, sinks, gate_b]
    sm_g = [g_norm_g, g_conv_w, g_q_norm_g, g_k_norm_g, g_sinks, g_gate_b]
    sm_m = [m_norm_g, m_conv_w, m_q_norm_g, m_k_norm_g, m_sinks, m_gate_b]
    sm_v = [v_norm_g, v_conv_w, v_q_norm_g, v_k_norm_g, v_sinks, v_gate_b]
    sd, snm, snv = _adamw_small(sm_w, sm_g, sm_m, sm_v)

    def order(norm, w_in_, conv, qn, kn, sk, co, ao, gb, wo):
        return [norm, w_in_, conv, qn, kn, sk, co, ao, gb, wo]

    grads = order(g_norm_g, g_w_in, g_conv_w, g_q_norm_g, g_k_norm_g, g_sinks, g_w_co, g_w_ao, g_gate_b, g_w_out)
    deltas = order(sd[0], d_in, sd[1], sd[2], sd[3], sd[4], big["co"][1], big["ao"][1], sd[5], big["out"][1])
    new_m = order(snm[0], nm_in, snm[1], snm[2], snm[3], snm[4], big["co"][2], big["ao"][2], snm[5], big["out"][2])
    new_v = order(snv[0], nv_in, snv[1], snv[2], snv[3], snv[4], big["co"][3], big["ao"][3], snv[5], big["out"][3])
    return (loss, grad_x, *grads, *deltas, *new_m, *new_v)
```

```python
import functools

import jax
import jax.numpy as jnp
from jax import lax
from jax.experimental import pallas as pl
from jax.experimental.pallas import tpu as pltpu

F32, BF16 = jnp.float32, jnp.bfloat16
SDS = jax.ShapeDtypeStruct
MESH = pl.DeviceIdType.MESH
ANY = pl.BlockSpec(memory_space=pl.ANY)

D = 1024
N_IN = 8704
N_LAYERS = 4
N_Q, N_KV, HEAD = 16, 4, 64
GROUP = N_Q // N_KV
BLK = 128
EPS = 1e-6
NEG = -1e30
SCALE = HEAD ** -0.5
SH_IN = N_IN // 4
SH_ROW = D // 4
CB = 512
SEG_CONV, SEG_Q, SEG_KV, SEG_ZA, SEG_GL = (0, 8), (8, 2), (10, 1), (11, 2), (13, 4)
VMEM_BIG = 56 * 1024 * 1024

ADAM_LR, ADAM_B1, ADAM_B2, ADAM_EPS, ADAM_WD, ADAM_STEP = 0.001, 0.9, 0.999, 1e-08, 0.01, 10


def _cp(*sem, vmem=None):
    return pltpu.CompilerParams(dimension_semantics=sem if sem else None, vmem_limit_bytes=vmem)


def _sigmoid(z):
    return 1.0 / (1.0 + jnp.exp(-z))


def _dot(a, b):
    return jnp.dot(a, b, preferred_element_type=F32)


def _dot_nt(a, b):
    return lax.dot_general(a, b, (((1,), (1,)), ((), ())), preferred_element_type=F32)


def _dot_tn(a, b):
    return lax.dot_general(a, b, (((0,), (0,)), ((), ())), preferred_element_type=F32)


def _rms(xh):
    r = lax.rsqrt(jnp.mean(xh * xh, axis=-1, keepdims=True) + EPS)
    return xh * r, r


def _fold8(v):
    return jnp.sum(v.reshape(v.shape[0] // 8, 8, v.shape[1]), axis=0)


def _cast_w_in(chip_idx, w, layer):
    def body(j_ref, i_ref, o_ref):
        o_ref[...] = i_ref[...].astype(BF16)

    return pl.pallas_call(
        body, name="cast_w_in",
        grid_spec=pltpu.PrefetchScalarGridSpec(
            num_scalar_prefetch=1, grid=(2,),
            in_specs=[pl.BlockSpec((None, 512, SH_IN), lambda i, j: (layer, i, 0))],
            out_specs=pl.BlockSpec((512, SH_IN), lambda i, j: (i, j[0]))),
        out_shape=SDS((D, N_IN), BF16), compiler_params=_cp("parallel"),
    )(chip_idx, w)


def _cast_w_small(chip_idx, a, b, c, layer):
    def body(j_ref, a_ref, b_ref, c_ref, o_ref):
        o_ref[0] = a_ref[...].astype(BF16)
        o_ref[1] = b_ref[...].astype(BF16)
        o_ref[2] = c_ref[...].astype(BF16)

    spec = pl.BlockSpec((None, SH_ROW, D), lambda i, j: (layer, 0, 0))
    return pl.pallas_call(
        body, name="cast_w_small",
        grid_spec=pltpu.PrefetchScalarGridSpec(
            num_scalar_prefetch=1, grid=(1,), in_specs=[spec, spec, spec],
            out_specs=pl.BlockSpec((3, SH_ROW, D), lambda i, j: (0, j[0], 0))),
        out_shape=SDS((3, D, D), BF16), compiler_params=_cp("parallel"),
    )(chip_idx, a, b, c)


def _mesh_pos():
    return lax.axis_index("x"), lax.axis_index("y"), lax.axis_index("c")


def _other_chips(x, y):
    return [(1 - x, y), (x, 1 - y), (1 - x, 1 - y)]


class _Rider:
    def __init__(self, ins, out_shape, n, copies, aliases=()):
        self.ins, self.out_shape, self.n, self.copies, self.aliases = list(ins), list(out_shape), n, copies, aliases


def _merge_riders(*riders):
    riders = [r for r in riders if r is not None]
    if len(riders) < 2:
        return riders[0] if riders else None

    def copies(ins, outs, send, recv, base=0):
        cps, i0, o0 = [], 0, 0
        for r in riders:
            cps += r.copies(ins[i0:i0 + len(r.ins)], outs[o0:o0 + len(r.out_shape)], send, recv, base + len(cps))
            i0, o0 = i0 + len(r.ins), o0 + len(r.out_shape)
        return cps

    aliases, i0, o0 = [], 0, 0
    for r in riders:
        aliases += [(i0 + i, o0 + o) for i, o in r.aliases]
        i0, o0 = i0 + len(r.ins), o0 + len(r.out_shape)
    return _Rider(sum((r.ins for r in riders), []), sum((r.out_shape for r in riders), []),
                  sum(r.n for r in riders), copies, tuple(aliases))


def _rcopy(src, dst, send, recv, k, to):
    return pltpu.make_async_remote_copy(src_ref=src, dst_ref=dst, send_sem=send.at[k], recv_sem=recv.at[k],
                                        device_id=to, device_id_type=MESH)


def _hosted_call(body, rider, *, name, grid, in_specs, out_specs, out_shape, args, scratch_shapes=(), vmem=None):
    n_in, n_out, n_scr = len(in_specs), len(out_specs), len(scratch_shapes)
    r_in, r_out = len(rider.ins), len(rider.out_shape)

    def full_body(*refs):
        host_in, rid_in = refs[:n_in], refs[n_in:n_in + r_in]
        o0 = n_in + r_in
        host_out, rid_out = refs[o0:o0 + n_out], refs[o0 + n_out:o0 + n_out + r_out]
        s0 = o0 + n_out + r_out
        host_scr, (send, recv) = refs[s0:s0 + n_scr], refs[s0 + n_scr:]
        if body is None:
            cps = rider.copies(rid_in, rid_out, send, recv)
            for cp in cps:
                cp.start()
            for cp in cps:
                cp.wait()
            return
        ids = [pl.program_id(a) for a in range(len(grid))]
        first = functools.reduce(lambda p, q: p & q, [i == 0 for i in ids])
        last = functools.reduce(lambda p, q: p & q, [i == g - 1 for i, g in zip(ids, grid)])

        @pl.when(first)
        def _():
            for cp in rider.copies(rid_in, rid_out, send, recv):
                cp.start()

        body(*host_in, *host_out, *host_scr)

        @pl.when(last)
        def _():
            for cp in rider.copies(rid_in, rid_out, send, recv):
                cp.wait()

    res = pl.pallas_call(
        full_body, name=name, grid=grid if body is not None else (),
        in_specs=list(in_specs) + [ANY] * r_in, out_specs=list(out_specs) + [ANY] * r_out,
        out_shape=list(out_shape) + rider.out_shape,
        scratch_shapes=list(scratch_shapes) + [pltpu.SemaphoreType.DMA((rider.n,))] * 2,
        input_output_aliases={n_in + i: n_out + o for i, o in rider.aliases},
        compiler_params=_cp(*(("arbitrary",) * len(grid) if body is not None else ()), vmem=vmem),
    )(*args, *rider.ins)
    return res[:n_out], res[n_out:]


def _call(body, rider, **kw):
    if rider is not None:
        return _hosted_call(body, rider, **kw)
    res = pl.pallas_call(
        body, name=kw["name"], grid=kw["grid"], in_specs=list(kw["in_specs"]), out_specs=list(kw["out_specs"]),
        out_shape=list(kw["out_shape"]), scratch_shapes=list(kw.get("scratch_shapes", ())),
        compiler_params=_cp(*(("arbitrary",) * len(kw["grid"])), vmem=kw.get("vmem")),
    )(*kw["args"])
    return res, []


def _gather_rider(arrays, stage):
    def region(full, whose, c, sub):
        if len(full.shape) == 2:
            rows, cols = full.shape[0] // 2, full.shape[1] // 4
            first, n = (c * rows, rows) if sub is None else (c * rows + sub * (rows // 2), rows // 2)
            return full.at[pl.ds(pl.multiple_of(first, n), n), pl.ds(pl.multiple_of(whose * cols, 128), cols)]
        first, n = (whose * SH_ROW + c * 128, 128) if sub is None else (whose * SH_ROW + c * 128 + sub * 64, 64)
        return full.at[:, pl.ds(pl.multiple_of(first, n), n), :]

    def copies(ins, outs, send, recv, base=0):
        x, y, c = _mesh_pos()
        nbr_x, nbr_y = (1 - x, y), (x, 1 - y)
        cps = []
        for full in outs:
            plan = []
            if stage == "N":
                plan = [(region(full, 2 * x + y, c, None), (*nbr_x, c)), (region(full, 2 * x + y, c, None), (*nbr_y, c))]
            if stage in ("F", "FB1"):
                plan = [(region(full, 2 * nbr_x[0] + nbr_x[1], c, 0), (*nbr_y, c)),
                        (region(full, 2 * nbr_y[0] + nbr_y[1], c, 1), (*nbr_x, c))]
            if stage in ("B", "FB1", "B2"):
                chips = {"B": _other_chips(x, y), "FB1": [nbr_x, nbr_y], "B2": [(1 - x, 1 - y)]}[stage]
                plan += [(region(full, 2 * chip[0] + chip[1], c, None), (x, y, 1 - c)) for chip in chips]
            for reg, to in plan:
                cps.append(_rcopy(reg, reg, send, recv, base + len(cps), to))
        return cps

    per_array = {"N": 2, "F": 2, "B": 3, "FB1": 4, "B2": 1}[stage]
    return _Rider(arrays, [SDS(v.shape, v.dtype) for v in arrays], per_array * len(arrays), copies,
                  aliases=tuple((i, i) for i in range(len(arrays))))


def _staged_copies(copies, stages, sem_in, sem_out):
    busy, count = {}, {}
    for idx, (src, dst, kind) in enumerate(copies):
        slot = count.get(kind, 0) % 2
        count[kind] = count.get(kind, 0) + 1
        if (kind, slot) in busy:
            busy.pop((kind, slot)).wait()
        buf = stages[kind].at[slot]
        cin = pltpu.make_async_copy(src, buf, sem_in.at[idx])
        cin.start()
        cin.wait()
        cout = pltpu.make_async_copy(buf, dst, sem_out.at[idx])
        cout.start()
        busy[(kind, slot)] = cout
    for cp in busy.values():
        cp.wait()


def _allreduce_small(pack):
    rows = pack.shape[0]

    def body(p_ref, o_ref, buf, send, recv):
        x, y, c = _mesh_pos()
        me = 4 * x + 2 * y + c
        sends = []
        for r in range(1, 8):
            to = (x if not (r & 4) else 1 - x, y if not (r & 2) else 1 - y, c if not (r & 1) else 1 - c)
            cp = pltpu.make_async_remote_copy(src_ref=p_ref, dst_ref=buf.at[me], send_sem=send.at[r - 1],
                                              recv_sem=recv.at[r - 1], device_id=to, device_id_type=MESH)
            cp.start()
            sends.append(cp)
        buf[me] = p_ref[...]
        for r in range(1, 8):
            frm = (4 * x + 2 * y + c) ^ r
            pltpu.make_async_remote_copy(src_ref=p_ref, dst_ref=buf.at[frm], send_sem=send.at[r - 1],
                                         recv_sem=recv.at[r - 1], device_id=(x, y, c), device_id_type=MESH).wait_recv()
        acc = buf[0]
        for d in range(1, 8):
            acc = acc + buf[d]
        o_ref[...] = acc
        for cp in sends:
            cp.wait_send()

    vm = pl.BlockSpec(memory_space=pltpu.VMEM)
    return pl.pallas_call(
        body, name="allreduce_small", in_specs=[vm], out_specs=vm, out_shape=SDS((rows, 128), F32),
        scratch_shapes=[pltpu.VMEM((8, rows, 128), F32), pltpu.SemaphoreType.DMA((7,)), pltpu.SemaphoreType.DMA((7,))],
    )(pack)


def _rmsnorm_fwd(x, g, rider):
    t = x.shape[0]
    tm = min(1024, t)

    def body(x_ref, g_ref, h_ref, ht_ref):
        xv = x_ref[...]
        r = lax.rsqrt(jnp.mean(xv * xv, axis=-1, keepdims=True) + EPS)
        h = xv * r * g_ref[...]
        h_ref[...] = h.astype(BF16)
        ht_ref[...] = h.T.astype(BF16)

    return _call(
        body, rider, name="rmsnorm_fwd", grid=(t // tm,),
        in_specs=[pl.BlockSpec((tm, D), lambda i: (i, 0)), pl.BlockSpec((1, D), lambda i: (0, 0))],
        out_specs=[pl.BlockSpec((tm, D), lambda i: (i, 0)), pl.BlockSpec((D, tm), lambda i: (0, i))],
        out_shape=[SDS((t, D), BF16), SDS((D, t), BF16)], args=(x, g), vmem=VMEM_BIG)


FWD_SEGS = ((0, 8), (8, 3), (11, 2), (13, 4))


def _in_proj(h, w_full, rider):
    t = h.shape[0]
    tm = min(2048, t)

    def body(a_ref, b_ref, *o_refs):
        j = pl.program_id(1)
        for o_ref, (off, nblk) in zip(o_refs, FWD_SEGS):
            @pl.when((j >= off) & (j < off + nblk))
            def _():
                o_ref[...] = _dot(a_ref[...], b_ref[...]).astype(BF16)

    def out(seg):
        off, nblk = seg
        return pl.BlockSpec((tm, CB), lambda i, j: (i, jnp.clip(j - off, 0, nblk - 1)))

    res, got = _call(
        body, rider, name="in_proj", grid=(t // tm, N_IN // CB),
        in_specs=[pl.BlockSpec((tm, D), lambda i, j: (i, 0)), pl.BlockSpec((D, CB), lambda i, j: (0, j))],
        out_specs=[out(s) for s in FWD_SEGS], out_shape=[SDS((t, s[1] * CB), BF16) for s in FWD_SEGS],
        args=(h, w_full), vmem=VMEM_BIG)
    return res, got


def _conv_fwd(u_conv, conv_w, rider):
    t = u_conv.shape[0]
    tm = min(512, t)
    hb = tm // 16

    def body(v_ref, b_ref, c_ref, z_ref, hv_ref, hc_ref, w_ref, y_ref):
        i = pl.program_id(0)
        cv = c_ref[...].astype(F32) * v_ref[...].astype(F32)
        halo = hc_ref[...].astype(F32) * hv_ref[...].astype(F32)
        halo = jnp.where(i > 0, halo, 0.0)
        row = lax.broadcasted_iota(jnp.int32, (tm, 1), 0)
        s1 = jnp.where(row == 0, halo[15:16], pltpu.roll(cv, 1, 0))
        s2 = jnp.where(row == 0, halo[14:15], jnp.where(row == 1, halo[15:16], pltpu.roll(cv, 2, 0)))
        conv = w_ref[0:1, :] * s2 + w_ref[1:2, :] * s1 + w_ref[2:3, :] * cv
        z = z_ref[...].astype(F32)
        y_ref[...] = (b_ref[...].astype(F32) * conv * (z * _sigmoid(z))).astype(BF16)

    def col(k):
        return pl.BlockSpec((tm, D), lambda i: (i, k))

    def halo(k):
        return pl.BlockSpec((16, D), lambda i: (jnp.maximum(i * hb - 1, 0), k))

    (y_c,), got = _call(
        body, rider, name="conv_fwd", grid=(t // tm,),
        in_specs=[col(0), col(1), col(2), col(3), halo(0), halo(2), pl.BlockSpec((3, D), lambda i: (0, 0))],
        out_specs=[pl.BlockSpec((tm, D), lambda i: (i, 0))], out_shape=[SDS((t, D), BF16)],
        args=(u_conv, u_conv, u_conv, u_conv, u_conv, u_conv, conv_w), vmem=VMEM_BIG)
    return y_c, got


KVX = 4 * N_KV * 128


def _iota2(shape):
    return lax.broadcasted_iota(jnp.int32, shape, 0), lax.broadcasted_iota(jnp.int32, shape, 1)


def _head_sum(v):
    r, c = _iota2((128, 128))
    ones = ((r >> 6) == (c >> 6)).astype(BF16)
    hi = v.astype(BF16)
    lo = (v - hi.astype(F32)).astype(BF16)
    return jnp.concatenate([_dot(hi[:, g:g + 128], ones) + _dot(lo[:, g:g + 128], ones)
                            for g in range(0, v.shape[1], 128)], axis=1)


def _expand_mats():
    r, c = _iota2((N_KV * HEAD, N_KV * 128))
    base = ((r >> 6) << 7) + (r & 63)
    return (c == base).astype(BF16), (c == base + 64).astype(BF16)


def _fold_mat():
    r, c = _iota2((N_KV * 128, N_KV * HEAD))
    return (((r >> 7) == (c >> 6)) & ((r & 63) == (c & 63))).astype(BF16)


def _qkv_prep(u_qkv, qg_s, kg_t, rider):
    t = u_qkv.shape[0]
    tm = min(1024, t)

    def body(u_ref, qg_ref, kg_ref, qs_ref, kvx_ref):
        q = u_ref[:, 0:D].astype(F32)
        rq = lax.rsqrt(_head_sum(q * q) * (1.0 / HEAD) + EPS)
        qs_ref[...] = (q * rq * qg_ref[...]).astype(BF16)
        k = u_ref[:, D:D + 256].astype(F32)
        rk = lax.rsqrt(_head_sum(k * k) * (1.0 / HEAD) + EPS)
        kn = (k * rk * kg_ref[...]).astype(BF16)
        v = u_ref[:, D + 256:D + 512]
        e_lo, e_hi = _expand_mats()
        kvx_ref[:, 0:512] = _dot(kn, e_lo).astype(BF16)
        kvx_ref[:, 512:1024] = _dot(kn, e_hi).astype(BF16)
        kvx_ref[:, 1024:1536] = _dot(v, e_lo).astype(BF16)
        kvx_ref[:, 1536:2048] = _dot(v, e_hi).astype(BF16)

    return _call(
        body, rider, name="qkv_prep", grid=(t // tm,),
        in_specs=[pl.BlockSpec((tm, 1536), lambda i: (i, 0)), pl.BlockSpec((1, D), lambda i: (0, 0)),
                  pl.BlockSpec((1, 256), lambda i: (0, 0))],
        out_specs=[pl.BlockSpec((tm, D), lambda i: (i, 0)), pl.BlockSpec((tm, KVX), lambda i: (i, 0))],
        out_shape=[SDS((t, D), BF16), SDS((t, KVX), BF16)], args=(u_qkv, qg_s, kg_t), vmem=VMEM_BIG)


def _band_bias():
    j, r = _iota2((2 * BLK, 2 * BLK))
    diff = (r & (BLK - 1)) - j + BLK
    band = (diff >= 0) & (diff < BLK)
    return jnp.stack([jnp.where(band & (j >= BLK), 0.0, NEG), jnp.where(band, 0.0, NEG)]).astype(F32)


def _pair_rows(ref_or_val, hk):
    return jnp.concatenate([ref_or_val[:, 256 * hk:256 * hk + 128], ref_or_val[:, 256 * hk + 128:256 * hk + 256]], axis=0)


LOG2E, LN2 = 1.4426950408889634, 0.6931471805599453


def _sink_row(sink_ref, hk, half):
    return jnp.concatenate([jnp.full((1, BLK), sink_ref[0, GROUP * hk + half] * LOG2E, F32),
                            jnp.full((1, BLK), sink_ref[0, GROUP * hk + 2 + half] * LOG2E, F32)], axis=1)


def _kv_operands(kvb, hk, half):
    return (kvb[:, 512 * half + 128 * hk:512 * half + 128 * hk + 128],
            kvb[:, 1024 + 512 * half + 128 * hk:1024 + 512 * half + 128 * hk + 128])


def _attn_fwd(qs, kvx, u_za, sinks, bias, rider):
    t = qs.shape[0]
    nb = t // BLK

    def body(q_ref, kc_ref, kp_ref, za_ref, sink_ref, bias_ref, o_ref, lse_ref):
        kvb = jnp.concatenate([kp_ref[...], kc_ref[...]], axis=0)
        bias_v = bias_ref[...]
        key0 = lax.broadcasted_iota(jnp.int32, (2 * BLK, 1), 0) == 0
        ones = jnp.ones((2 * BLK, 128), BF16)
        cols = []
        for hk in range(N_KV):
            qpp = _pair_rows(q_ref, hk)
            opp = None
            for half in range(2):
                kx, vx = _kv_operands(kvb, hk, half)
                s = _dot_nt(kx, qpp) + bias_v
                sink = _sink_row(sink_ref, hk, half)
                m = jnp.maximum(jnp.max(s, axis=0, keepdims=True), sink)
                p = jnp.exp2(s - m)
                es = jnp.exp2(sink - m)
                lse_ref[0, 2 * hk + half:2 * hk + half + 1, :] = m + jnp.log(jnp.sum(p, axis=0, keepdims=True) + es) * LOG2E
                pe = jnp.where(key0, es, p).astype(BF16)
                rhs = jnp.concatenate([jnp.where(key0, jnp.zeros_like(vx), vx), ones], axis=1)
                nd = _dot_tn(pe, rhs)
                o = nd[:, :128] * (1.0 / nd[:, 128:])
                opp = o if opp is None else opp + o
            cols += [opp[:BLK], opp[BLK:]]
        za = za_ref[...].astype(F32)
        o_ref[...] = (jnp.concatenate(cols, axis=1) * (za * _sigmoid(za))).astype(BF16)

    prev = lambda n: jnp.maximum(n - 1, 0)
    (o, lse), got = _call(
        body, rider, name="attn_fwd", grid=(nb,),
        in_specs=[pl.BlockSpec((BLK, D), lambda n: (n, 0)),
                  pl.BlockSpec((BLK, KVX), lambda n: (n, 0)), pl.BlockSpec((BLK, KVX), lambda n: (prev(n), 0)),
                  pl.BlockSpec((BLK, D), lambda n: (n, 0)), pl.BlockSpec(memory_space=pltpu.SMEM),
                  pl.BlockSpec((None, 2 * BLK, 2 * BLK), lambda n: (jnp.minimum(n, 1), 0, 0))],
        out_specs=[pl.BlockSpec((BLK, D), lambda n: (n, 0)), pl.BlockSpec((1, 8, 2 * BLK), lambda n: (n, 0, 0))],
        out_shape=[SDS((t, D), BF16), SDS((nb, 8, 2 * BLK), F32)],
        args=(qs, kvx, kvx, u_za, sinks, bias), vmem=VMEM_BIG)
    return o, lse, got


def _out_proj_fwd(x, y_c, o, u_gl, gate_b, w_sm, rider):
    t = x.shape[0]
    tm = min(512, t)

    def body(x_ref, yc_ref, o_ref, gla_ref, glb_ref, gb_ref, wco_ref, wao_ref, wout_ref,
             xn_ref, ya_ref, yb_ref, mg_ref):
        ya = _dot(yc_ref[...], wco_ref[...])
        yb = _dot(o_ref[...], wao_ref[...])
        gb = gb_ref[...]
        ga_ = _sigmoid(gla_ref[...].astype(F32) + gb[:, :D])
        gb_ = _sigmoid(glb_ref[...].astype(F32) + gb[:, D:])
        merged = (ga_ * ya + gb_ * yb).astype(BF16)
        ya_ref[...] = ya.astype(BF16)
        yb_ref[...] = yb.astype(BF16)
        mg_ref[...] = merged
        xn_ref[...] = x_ref[...] + _dot(merged, wout_ref[...])

    row = pl.BlockSpec((tm, D), lambda i: (i, 0))
    wspec = lambda a: pl.BlockSpec((None, D, D), lambda i: (a, 0, 0))
    return _call(
        body, rider, name="out_proj_fwd", grid=(t // tm,),
        in_specs=[row, row, row, pl.BlockSpec((tm, D), lambda i: (i, 0)), pl.BlockSpec((tm, D), lambda i: (i, 1)),
                  pl.BlockSpec((1, 2 * D), lambda i: (0, 0)), wspec(0), wspec(1), wspec(2)],
        out_specs=[row, row, row, row],
        out_shape=[SDS((t, D), F32), SDS((t, D), BF16), SDS((t, D), BF16), SDS((t, D), BF16)],
        args=(x, y_c, o, u_gl, u_gl, gate_b, w_sm, w_sm, w_sm), vmem=VMEM_BIG)


def _loss_head(y, tgt):
    t = y.shape[0]
    tm = min(1024, t)

    def body(y_ref, t_ref, dy_ref, acc_ref):
        @pl.when(pl.program_id(0) == 0)
        def _():
            acc_ref[...] = jnp.zeros_like(acc_ref)
        err = y_ref[...] - t_ref[...]
        dy_ref[...] = err * (1.0 / D)
        sq = _fold8(err * err)
        tot = sq[:, 0:128]
        for k in range(1, D // 128):
            tot = tot + sq[:, 128 * k:128 * (k + 1)]
        acc_ref[...] += tot

    row = pl.BlockSpec((tm, D), lambda i: (i, 0))
    return pl.pallas_call(
        body, name="loss_head", grid=(t // tm,), in_specs=[row, row],
        out_specs=[row, pl.BlockSpec((8, 128), lambda i: (0, 0))],
        out_shape=[SDS((t, D), F32), SDS((8, 128), F32)], compiler_params=_cp("arbitrary"),
    )(y, tgt)


def _out_proj_bwd(dout, y_a, y_b, u_gl, gate_b, w_sm, rider):
    t = dout.shape[0]
    tm = min(512, t)

    def body(do_ref, ya_ref, yb_ref, gla_ref, glb_ref, gb_ref, wco_ref, wao_ref, wout_ref,
             dya_ref, dyb_ref, dgl_ref, dyc_ref, dob_ref, dgb_ref):
        @pl.when(pl.program_id(0) == 0)
        def _():
            dgb_ref[...] = jnp.zeros_like(dgb_ref)
        dm = _dot_nt(do_ref[...].astype(BF16), wout_ref[...])
        gb = gb_ref[...]
        ga_ = _sigmoid(gla_ref[...].astype(F32) + gb[:, :D])
        gb_ = _sigmoid(glb_ref[...].astype(F32) + gb[:, D:])
        dya = (ga_ * dm).astype(BF16)
        dyb = (gb_ * dm).astype(BF16)
        dgla = ya_ref[...].astype(F32) * dm * (ga_ * (1.0 - ga_))
        dglb = yb_ref[...].astype(F32) * dm * (gb_ * (1.0 - gb_))
        dya_ref[...] = dya
        dyb_ref[...] = dyb
        dgl_ref[:, :D] = dgla.astype(BF16)
        dgl_ref[:, D:] = dglb.astype(BF16)
        dgb_ref[:, :D] += _fold8(dgla)
        dgb_ref[:, D:] += _fold8(dglb)
        dyc_ref[...] = _dot_nt(dya, wco_ref[...]).astype(BF16)
        dob_ref[...] = _dot_nt(dyb, wao_ref[...]).astype(BF16)

    row = pl.BlockSpec((tm, D), lambda i: (i, 0))
    wspec = lambda a: pl.BlockSpec((None, D, D), lambda i: (a, 0, 0))
    return _call(
        body, rider, name="out_proj_bwd", grid=(t // tm,),
        in_specs=[row, row, row, pl.BlockSpec((tm, D), lambda i: (i, 0)), pl.BlockSpec((tm, D), lambda i: (i, 1)),
                  pl.BlockSpec((1, 2 * D), lambda i: (0, 0)), wspec(0), wspec(1), wspec(2)],
        out_specs=[row, row, pl.BlockSpec((tm, 2 * D), lambda i: (i, 0)), row, row,
                   pl.BlockSpec((8, 2 * D), lambda i: (0, 0))],
        out_shape=[SDS((t, D), BF16), SDS((t, D), BF16), SDS((t, 2 * D), BF16), SDS((t, D), BF16), SDS((t, D), BF16),
                   SDS((8, 2 * D), F32)],
        args=(dout, y_a, y_b, u_gl, u_gl, gate_b, w_sm, w_sm, w_sm), vmem=VMEM_BIG)


def _small_wgrads(y_c, d_ya, o, d_yb, merged, dout):
    t = y_c.shape[0]
    tk = min(512, t)

    def body(yc_ref, dya_ref, o_ref, dyb_ref, mg_ref, do_ref, g_ref):
        @pl.when(pl.program_id(0) == 0)
        def _():
            g_ref[...] = jnp.zeros_like(g_ref)
        g_ref[0] += _dot_tn(yc_ref[...], dya_ref[...])
        g_ref[1] += _dot_tn(o_ref[...], dyb_ref[...])
        g_ref[2] += _dot_tn(mg_ref[...], do_ref[...].astype(BF16))

    row = pl.BlockSpec((tk, D), lambda k: (k, 0))
    return pl.pallas_call(
        body, name="small_wgrads", grid=(t // tk,), in_specs=[row] * 6,
        out_specs=pl.BlockSpec((3, D, D), lambda k: (0, 0, 0)), out_shape=SDS((3, D, D), F32),
        compiler_params=_cp("arbitrary", vmem=VMEM_BIG),
    )(y_c, d_ya, o, d_yb, merged, dout)


def _conv_bwd(d_yc, u_conv, conv_w, rider):
    t = d_yc.shape[0]
    tm = min(512, t)
    hb = tm // 16
    last_halo = t // 16 - 1
    n_steps = t // tm

    def body(dy_ref, v_ref, b_ref, c_ref, z_ref, hv_ref, hc_ref, ndy_ref, nb_ref, nz_ref, w_ref, du_ref, dw_ref):
        i = pl.program_id(0)

        @pl.when(i == 0)
        def _():
            dw_ref[...] = jnp.zeros_like(dw_ref)
        v, c = v_ref[...].astype(F32), c_ref[...].astype(F32)
        b, z = b_ref[...].astype(F32), z_ref[...].astype(F32)
        cv = c * v
        halo = jnp.where(i > 0, hc_ref[...].astype(F32) * hv_ref[...].astype(F32), 0.0)
        row = lax.broadcasted_iota(jnp.int32, (tm, 1), 0)
        s1 = jnp.where(row == 0, halo[15:16], pltpu.roll(cv, 1, 0))
        s2 = jnp.where(row == 0, halo[14:15], jnp.where(row == 1, halo[15:16], pltpu.roll(cv, 2, 0)))
        w0, w1, w2 = w_ref[0:1, :], w_ref[1:2, :], w_ref[2:3, :]
        conv = w0 * s2 + w1 * s1 + w2 * cv
        sig = _sigmoid(z)
        sz = z * sig
        dsz = sig * (1.0 + z * (1.0 - sig))
        dy = dy_ref[...].astype(F32)
        dconv = dy * b * sz
        nz = nz_ref[...].astype(F32)
        nxt = ndy_ref[...].astype(F32) * nb_ref[...].astype(F32) * (nz * _sigmoid(nz))
        nxt = jnp.where(i < n_steps - 1, nxt, 0.0)
        a1 = jnp.where(row == tm - 1, nxt[0:1], pltpu.roll(dconv, tm - 1, 0))
        a2 = jnp.where(row == tm - 2, nxt[0:1], jnp.where(row == tm - 1, nxt[1:2], pltpu.roll(dconv, tm - 2, 0)))
        dcv = w2 * dconv + w1 * a1 + w0 * a2
        du_ref[:, 0:D] = (dcv * c).astype(BF16)
        du_ref[:, D:2 * D] = (dy * conv * sz).astype(BF16)
        du_ref[:, 2 * D:3 * D] = (dcv * v).astype(BF16)
        du_ref[:, 3 * D:4 * D] = (dy * b * conv * dsz).astype(BF16)
        r8 = lax.broadcasted_iota(jnp.int32, (8, 1), 0)
        dw_ref[...] += jnp.where(r8 == 0, jnp.sum(dconv * s2, axis=0, keepdims=True),
                                 jnp.where(r8 == 1, jnp.sum(dconv * s1, axis=0, keepdims=True),
                                           jnp.where(r8 == 2, jnp.sum(dconv * cv, axis=0, keepdims=True), 0.0)))

    def col(k):
        return pl.BlockSpec((tm, D), lambda i: (i, k))

    def halo(k):
        return pl.BlockSpec((16, D), lambda i: (jnp.maximum(i * hb - 1, 0), k))

    def nxt(k):
        return pl.BlockSpec((16, D), lambda i: (jnp.minimum((i + 1) * hb, last_halo), k))

    return _call(
        body, rider, name="conv_bwd", grid=(t // tm,),
        in_specs=[col(0), col(0), col(1), col(2), col(3), halo(0), halo(2), nxt(0), nxt(1), nxt(3),
                  pl.BlockSpec((3, D), lambda i: (0, 0))],
        out_specs=[pl.BlockSpec((tm, 4 * D), lambda i: (i, 0)), pl.BlockSpec((8, D), lambda i: (0, 0))],
        out_shape=[SDS((t, 4 * D), BF16), SDS((8, D), F32)],
        args=(d_yc, u_conv, u_conv, u_conv, u_conv, u_conv, u_conv, d_yc, u_conv, u_conv, conv_w), vmem=VMEM_BIG)


def _attn_bwd(d_o, qs, kvx, u_za, lse, sinks, bias, rider):
    t = d_o.shape[0]
    nb = t // BLK

    def body(q_ref, kc_ref, kp_ref, za_ref, do_ref, lse_ref, sink_ref, bias_ref,
             dq_ref, dkv_ref, dza_ref, dsk_ref, carry_ref):
        n = pl.program_id(0)

        @pl.when(n == 0)
        def _():
            carry_ref[...] = jnp.zeros_like(carry_ref)
            dsk_ref[...] = jnp.zeros_like(dsk_ref)

        live = n < nb
        kvb = jnp.concatenate([kp_ref[...], kc_ref[...]], axis=0)
        bias_v = bias_ref[...]
        za = za_ref[...].astype(F32)
        sig = _sigmoid(za)
        dsa = sig * (1.0 + za * (1.0 - sig))
        do = jnp.where(live, do_ref[...].astype(F32), 0.0)
        dattn_f = do * (za * sig)
        dattn = dattn_f.astype(BF16)
        dattn_ln2 = (dattn_f * LN2).astype(BF16)
        lo_lanes = lax.broadcasted_iota(jnp.int32, (1, 128), 1) < HEAD
        dq_cols, attn_cols, dk_cols, dv_cols, dsk_rows = [], [], [], [], []
        for hk in range(N_KV):
            qpp = _pair_rows(q_ref, hk)
            dapp = _pair_rows(dattn, hk)
            dapp_ln2 = _pair_rows(dattn_ln2, hk)
            probs, dss, xk, xv = [], [], [], []
            for half in range(2):
                kx, vx = _kv_operands(kvb, hk, half)
                lse = lse_ref[0, 2 * hk + half:2 * hk + half + 1, :]
                prob = jnp.exp2(_dot_nt(kx, qpp) + bias_v - lse)
                psink = jnp.exp2(_sink_row(sink_ref, hk, half) - lse)
                tdp = prob * _dot_nt(vx, dapp_ln2)
                drow = jnp.sum(tdp, axis=0, keepdims=True)
                ds = (tdp - prob * drow).astype(BF16)
                prob_b = prob.astype(BF16)
                xk.append(_dot(ds, qpp))
                xv.append(_dot(prob_b, dapp))
                probs.append(prob_b)
                dss.append(ds)
                dsk_rows.append(-psink * drow * LOG2E)
            kcat = jnp.concatenate([kvb[:, 128 * hk:128 * hk + 128], kvb[:, 512 + 128 * hk:512 + 128 * hk + 128]], axis=0)
            vcat = jnp.concatenate([kvb[:, 1024 + 128 * hk:1024 + 128 * hk + 128],
                                    kvb[:, 1536 + 128 * hk:1536 + 128 * hk + 128]], axis=0)
            app = _dot_tn(jnp.concatenate(probs, axis=0), vcat)
            dqpp = _dot_tn(jnp.concatenate(dss, axis=0), kcat)
            dq_cols += [dqpp[:BLK], dqpp[BLK:]]
            attn_cols += [app[:BLK], app[BLK:]]
            dk_cols.append(jnp.where(lo_lanes, xk[0], xk[1]))
            dv_cols.append(jnp.where(lo_lanes, xv[0], xv[1]))

        @pl.when(live)
        def _():
            dq_ref[...] = jnp.concatenate(dq_cols, axis=1).astype(BF16)
            dza_ref[...] = (do * jnp.concatenate(attn_cols, axis=1) * dsa).astype(BF16)

        band = jnp.concatenate(dk_cols + dv_cols, axis=1)
        dkv_ref[...] = (band[:BLK] + carry_ref[...]).astype(BF16)
        carry_ref[...] = band[BLK:]
        dsk_ref[...] += jnp.broadcast_to(jnp.concatenate(dsk_rows, axis=1), (8, 2 * N_KV * 2 * BLK))

    cur = lambda n: jnp.minimum(n, nb - 1)
    prev = lambda n: jnp.maximum(n - 1, 0)
    return _call(
        body, rider, name="attn_bwd", grid=(nb + 1,),
        in_specs=[pl.BlockSpec((BLK, D), lambda n: (cur(n), 0)),
                  pl.BlockSpec((BLK, KVX), lambda n: (cur(n), 0)), pl.BlockSpec((BLK, KVX), lambda n: (prev(n), 0)),
                  pl.BlockSpec((BLK, D), lambda n: (cur(n), 0)), pl.BlockSpec((BLK, D), lambda n: (cur(n), 0)),
                  pl.BlockSpec((1, 8, 2 * BLK), lambda n: (cur(n), 0, 0)), pl.BlockSpec(memory_space=pltpu.SMEM),
                  pl.BlockSpec((None, 2 * BLK, 2 * BLK), lambda n: (jnp.minimum(n, 1), 0, 0))],
        out_specs=[pl.BlockSpec((BLK, D), lambda n: (cur(n), 0)), pl.BlockSpec((BLK, D), lambda n: (prev(n), 0)),
                   pl.BlockSpec((BLK, D), lambda n: (cur(n), 0)), pl.BlockSpec((8, 2 * D), lambda n: (0, 0))],
        out_shape=[SDS((t, D), BF16), SDS((t, D), BF16), SDS((t, D), BF16), SDS((8, 2 * D), F32)],
        scratch_shapes=[pltpu.VMEM((BLK, D), F32)],
        args=(qs, kvx, kvx, u_za, d_o, lse, sinks, bias), vmem=VMEM_BIG)


def _qkv_post(u_qkv, dqs, dkv, dza, qg_s, kg_t, rider):
    t = u_qkv.shape[0]
    tm = min(512, t)

    def norm_bwd(x, dy, g):
        r = lax.rsqrt(_head_sum(x * x) * (1.0 / HEAD) + EPS)
        xhat = x * r
        dxh = dy * g
        return r * (dxh - xhat * (_head_sum(dxh * xhat) * (1.0 / HEAD))), _fold8(dy * xhat)

    def body(u_ref, dq_ref, dkv_ref, dza_ref, qg_ref, kg_ref, du_ref, dqg_ref, dkg_ref):
        @pl.when(pl.program_id(0) == 0)
        def _():
            dqg_ref[...] = jnp.zeros_like(dqg_ref)
            dkg_ref[...] = jnp.zeros_like(dkg_ref)
        dq, gq = norm_bwd(u_ref[:, 0:D].astype(F32), dq_ref[...].astype(F32), qg_ref[...])
        fold = _fold_mat()
        dk, gk = norm_bwd(u_ref[:, D:D + 256].astype(F32), _dot(dkv_ref[:, 0:512], fold), kg_ref[...])
        du_ref[:, 0:D] = dq.astype(BF16)
        du_ref[:, D:D + 256] = dk.astype(BF16)
        du_ref[:, D + 256:D + 512] = _dot(dkv_ref[:, 512:1024], fold).astype(BF16)
        du_ref[:, D + 512:2 * D + 512] = dza_ref[...]
        dqg_ref[...] += gq
        dkg_ref[...] += gk

    row = pl.BlockSpec((tm, D), lambda i: (i, 0))
    return _call(
        body, rider, name="qkv_post", grid=(t // tm,),
        in_specs=[pl.BlockSpec((tm, 1536), lambda i: (i, 0)), row, row, row,
                  pl.BlockSpec((1, D), lambda i: (0, 0)), pl.BlockSpec((1, 256), lambda i: (0, 0))],
        out_specs=[pl.BlockSpec((tm, 2560), lambda i: (i, 0)), pl.BlockSpec((8, D), lambda i: (0, 0)),
                   pl.BlockSpec((8, 256), lambda i: (0, 0))],
        out_shape=[SDS((t, 2560), BF16), SDS((8, D), F32), SDS((8, 256), F32)],
        args=(u_qkv, dqs, dkv, dza, qg_s, kg_t), vmem=VMEM_BIG)


N_GRAN = N_IN // CB
DU_COLS = ((0, 4096), (4096, 6656), (6656, N_IN))


def _du_granule(j):
    return jnp.clip(j, 0, 7), jnp.clip(j - 8, 0, 4), jnp.clip(j - 13, 0, 3)


def _du_select(j, refs, fn):
    for ref, lo, hi in zip(refs, (0, 8, 13), (8, 13, 17)):
        @pl.when((j >= lo) & (j < hi))
        def _():
            fn(ref)


def _in_proj_bwd(du, w_full, x, g, dout, rider):
    t = du[0].shape[0]
    tn = min(256, t)

    def body(a0, a1, a2, w_hbm, x_ref, g_ref, do_ref, dx_ref, dg_ref, w_ref, sem):
        @pl.when(pl.program_id(0) == 0)
        def _():
            cp = pltpu.make_async_copy(w_hbm, w_ref, sem)
            cp.start()
            dg_ref[...] = jnp.zeros_like(dg_ref)
            cp.wait()
        acc = None
        for a_ref, (lo, hi) in zip((a0, a1, a2), DU_COLS):
            part = _dot_nt(w_ref[:, lo:hi], a_ref[...])
            acc = part if acc is None else acc + part
        dh = acc.T
        xv = x_ref[...]
        r = lax.rsqrt(jnp.mean(xv * xv, axis=-1, keepdims=True) + EPS)
        xhat = xv * r
        dg_ref[...] += _fold8(dh * xhat)
        dxh = dh * g_ref[...]
        dx_ref[...] = do_ref[...] + r * (dxh - xhat * jnp.mean(dxh * xhat, axis=-1, keepdims=True))

    row = pl.BlockSpec((tn, D), lambda i: (i, 0))
    return _call(
        body, rider, name="in_proj_bwd", grid=(t // tn,),
        in_specs=[pl.BlockSpec((tn, hi - lo), lambda i: (i, 0)) for lo, hi in DU_COLS]
        + [ANY, row, pl.BlockSpec((1, D), lambda i: (0, 0)), row],
        out_specs=[row, pl.BlockSpec((8, D), lambda i: (0, 0))], out_shape=[SDS((t, D), F32), SDS((8, D), F32)],
        scratch_shapes=[pltpu.VMEM((D, N_IN), BF16), pltpu.SemaphoreType.DMA(())],
        args=(*du, w_full, x, g, dout), vmem=VMEM_BIG)


def _in_proj_wgrad(ht, du, rider):
    t = ht.shape[1]
    tk = min(4096, t)
    n_k = t // tk

    def body(h_ref, b0, b1, b2, g_ref):
        j, k = pl.program_id(0), pl.program_id(1)

        if n_k > 1:
            @pl.when(k == 0)
            def _():
                g_ref[...] = jnp.zeros_like(g_ref)

        def add(b_ref):
            if n_k > 1:
                g_ref[...] += _dot(h_ref[...], b_ref[...])
            else:
                g_ref[...] = _dot(h_ref[...], b_ref[...])
        _du_select(j, (b0, b1, b2), add)

    seg = lambda q: pl.BlockSpec((tk, CB), lambda j, k: (k, _du_granule(j)[q]))
    (g,), got = _call(
        body, rider, name="in_proj_wgrad", grid=(N_GRAN, t // tk),
        in_specs=[pl.BlockSpec((D, tk), lambda j, k: (0, k)), seg(0), seg(1), seg(2)],
        out_specs=[pl.BlockSpec((D, CB), lambda j, k: (0, j))], out_shape=[SDS((D, N_IN), F32)],
        args=(ht, *du), vmem=VMEM_BIG)
    return g, got


def _swap_rider(g_in, g_sm):
    def copies(ins, outs, send, recv, base=0):
        x, y, c = _mesh_pos()
        cps = []
        for src, dst in zip(ins, outs):
            half = src.at[1 - c] if len(src.shape) == 3 else src.at[:, :, 1 - c]
            cps.append(_rcopy(half, dst, send, recv, base + len(cps), (x, y, 1 - c)))
        return cps

    arrays = [g for g in (g_in, g_sm) if g is not None]
    shapes = [SDS((512, N_IN), F32) if len(g.shape) == 3 else SDS((3, 4, 128, D), F32) for g in arrays]
    return _Rider(arrays, shapes, len(arrays), copies)


def _add_halves_in(cc_idx, g_in, r_in):
    def body(cc_ref, a_ref, b_ref, f_ref, h_ref):
        s = a_ref[...] + b_ref[...]
        h_ref[...] = s.astype(BF16)

        @pl.when(pl.program_id(1) == cc_ref[1])
        def _():
            f_ref[...] = s

    blk = pl.BlockSpec((256, SH_IN), lambda i, j, cc: (i, j))
    return pl.pallas_call(
        body, name="add_halves_in",
        grid_spec=pltpu.PrefetchScalarGridSpec(
            num_scalar_prefetch=1, grid=(2, 4),
            in_specs=[pl.BlockSpec((None, 256, SH_IN), lambda i, j, cc: (cc[0], i, j)), blk],
            out_specs=[pl.BlockSpec((256, SH_IN), lambda i, j, cc: (i, 0)), blk]),
        out_shape=[SDS((512, SH_IN), F32), SDS((512, N_IN), BF16)],
        compiler_params=_cp("arbitrary", "arbitrary", vmem=VMEM_BIG),
    )(cc_idx, g_in, r_in)


def _add_halves_sm(c_idx, g_sm, r_sm):
    def body(c_ref, a_ref, b_ref, f_ref, h_ref):
        s = a_ref[...] + b_ref[...]
        f_ref[...] = s
        h_ref[...] = s.astype(BF16)

    blk = pl.BlockSpec((1, 4, 128, D), lambda a, c: (a, 0, 0, 0))
    return pl.pallas_call(
        body, name="add_halves_sm",
        grid_spec=pltpu.PrefetchScalarGridSpec(
            num_scalar_prefetch=1, grid=(3,),
            in_specs=[pl.BlockSpec((1, 4, None, 128, D), lambda a, c: (a, 0, c[0], 0, 0)), blk], out_specs=[blk, blk]),
        out_shape=[SDS((3, 4, 128, D), F32), SDS((3, 4, 128, D), BF16)], compiler_params=_cp("parallel"),
    )(c_idx, g_sm, r_sm)


def _scatter_rider(h_in, h_sm):
    def copies(ins, outs, send, recv, base=0):
        x, y, c = _mesh_pos()
        cps = []
        for src, dst in zip(ins, outs):
            for k, chip in enumerate(_other_chips(x, y)):
                their = 2 * chip[0] + chip[1]
                part = src.at[:, pl.ds(pl.multiple_of(their * SH_IN, 128), SH_IN)] if len(src.shape) == 2 else src.at[:, their]
                cps.append(_rcopy(part, dst.at[k], send, recv, base + len(cps), (*chip, c)))
        return cps

    arrays = [h for h in (h_in, h_sm) if h is not None]
    shapes = [SDS((3, 512, SH_IN), BF16) if len(h.shape) == 2 else SDS((3, 3, 128, D), BF16) for h in arrays]
    return _Rider(arrays, shapes, 3 * len(arrays), copies)


def _ride_alone(rider, name):
    return _hosted_call(None, rider, name=name, grid=(), in_specs=[], out_specs=[], out_shape=[], args=())[1]


def _final_sum_in(f_in, r_in):
    def body(a_ref, r_ref, o_ref):
        o_ref[...] = a_ref[...] + r_ref[0].astype(F32) + r_ref[1].astype(F32) + r_ref[2].astype(F32)

    return pl.pallas_call(
        body, name="final_sum_in", grid=(4,),
        in_specs=[pl.BlockSpec((128, SH_IN), lambda i: (i, 0)), pl.BlockSpec((3, 128, SH_IN), lambda i: (0, i, 0))],
        out_specs=pl.BlockSpec((128, SH_IN), lambda i: (i, 0)),
        out_shape=SDS((512, SH_IN), F32), compiler_params=_cp("parallel"),
    )(f_in, r_in)


def _final_sum_sm(chip_idx, f_sm, r_sm):
    def body(j_ref, a_ref, r_ref, o_ref):
        o_ref[...] = a_ref[...] + r_ref[0].astype(F32) + r_ref[1].astype(F32) + r_ref[2].astype(F32)

    return pl.pallas_call(
        body, name="final_sum_sm",
        grid_spec=pltpu.PrefetchScalarGridSpec(
            num_scalar_prefetch=1, grid=(3,),
            in_specs=[pl.BlockSpec((1, None, 128, D), lambda a, j: (a, j[0], 0, 0)),
                      pl.BlockSpec((3, 1, 128, D), lambda a, j: (0, a, 0, 0))],
            out_specs=pl.BlockSpec((1, 128, D), lambda a, j: (a, 0, 0))),
        out_shape=SDS((3, 128, D), F32), compiler_params=_cp("parallel"),
    )(chip_idx, f_sm, r_sm)


def _join_halves(t_in, t_sm):
    n_cp = N_LAYERS * 4
    args, plan = [], []
    for l in range(N_LAYERS):
        if t_in[l] is not None:
            plan.append((l, 0, len(args)))
            args.append(t_in[l])
        plan += [(l, a, len(args)) for a in (1, 2, 3)]
        args.append(t_sm[l])

    def body(*refs):
        ins, outs = refs[:len(args)], refs[len(args):len(args) + 4]
        send, recv, loc_in, loc_out, stage_in, stage_sm = refs[len(args) + 4:]
        x, y, c = _mesh_pos()
        cps, own = [], []

        def place(l, a, half):
            rows = 512 if a == 0 else 128
            return outs[a].at[l, pl.ds(pl.multiple_of(half * rows, rows), rows), :]

        for s, (l, a, k) in enumerate(plan):
            src = ins[k] if a == 0 else ins[k].at[a - 1]
            own.append((src, place(l, a, c), min(a, 1)))
            cp = pltpu.make_async_remote_copy(src_ref=src, dst_ref=place(l, a, c), send_sem=send.at[s],
                                              recv_sem=recv.at[s], device_id=(x, y, 1 - c), device_id_type=MESH)
            cp.start()
            cps.append(cp)
        _staged_copies(own, (stage_in, stage_sm), loc_in, loc_out)
        for s, (l, a, k) in enumerate(plan):
            got = place(l, a, 1 - c)
            pltpu.make_async_remote_copy(src_ref=got, dst_ref=got, send_sem=send.at[s], recv_sem=recv.at[s],
                                         device_id=(x, y, 1 - c), device_id_type=MESH).wait_recv()
        for cp in cps:
            cp.wait_send()

    sm = SDS((N_LAYERS, SH_ROW, D), F32)
    return pl.pallas_call(
        body, name="join_halves", in_specs=[ANY] * len(args), out_specs=[ANY] * 4,
        out_shape=[SDS((N_LAYERS, D, SH_IN), F32), sm, sm, sm],
        scratch_shapes=[pltpu.SemaphoreType.DMA((n_cp,))] * 4
        + [pltpu.VMEM((2, 512, SH_IN), F32), pltpu.VMEM((2, 128, D), F32)],
        compiler_params=_cp(vmem=VMEM_BIG),
    )(*args)


def _adam_math(w, g, m, v):
    m = ADAM_B1 * m + (1.0 - ADAM_B1) * g
    v = ADAM_B2 * v + (1.0 - ADAM_B2) * (g * g)
    m_hat = m / (1.0 - ADAM_B1 ** ADAM_STEP)
    v_hat = v / (1.0 - ADAM_B2 ** ADAM_STEP)
    delta = -ADAM_LR * (m_hat / (jnp.sqrt(v_hat) + ADAM_EPS) + ADAM_WD * w)
    return delta, m, v


def _adamw_big(w, g, m, v, name):
    rows, cols = w.shape
    tr = 256

    def body(w_ref, g_ref, m_ref, v_ref, go_ref, d_ref, nm_ref, nv_ref):
        g = g_ref[...]
        go_ref[...] = g
        d_ref[...], nm_ref[...], nv_ref[...] = _adam_math(w_ref[...], g, m_ref[...], v_ref[...])

    blk = pl.BlockSpec((tr, cols), lambda i: (i, 0))
    return pl.pallas_call(
        body, name=name, grid=(rows // tr,), in_specs=[blk] * 4, out_specs=[blk] * 4,
        out_shape=[SDS((rows, cols), F32)] * 4, compiler_params=_cp("parallel", vmem=VMEM_BIG),
    )(w, g, m, v)


def _adamw_small(ws, gs, ms, vs):
    n = len(ws)

    def body(*refs):
        for k in range(n):
            w_ref, g_ref, m_ref, v_ref = (refs[q * n + k] for q in range(4))
            d, nm, nv = _adam_math(w_ref[...], g_ref[...], m_ref[...], v_ref[...])
            refs[4 * n + k][...] = d
            refs[5 * n + k][...] = nm
            refs[6 * n + k][...] = nv

    vm = pl.BlockSpec(memory_space=pltpu.VMEM)
    shapes = [SDS(w.shape, F32) for w in ws]
    res = pl.pallas_call(
        body, name="adamw_small", in_specs=[vm] * (4 * n), out_specs=[vm] * (3 * n), out_shape=shapes * 3,
    )(*ws, *gs, *ms, *vs)
    return res[:n], res[n:2 * n], res[2 * n:]


def _pad_rows(a, rows):
    flat = a.reshape(-1)
    return jnp.pad(flat, (0, rows * 128 - flat.shape[0])).reshape(rows, 128)


def kernel(x, norm_g, w_in, conv_w, q_norm_g, k_norm_g, sinks, w_conv_out, w_attn_out, gate_b, w_out, loss_target, m_norm_g, m_w_in, m_conv_w, m_q_norm_g, m_k_norm_g, m_sinks, m_w_conv_out, m_w_attn_out, m_gate_b, m_w_out, v_norm_g, v_w_in, v_conv_w, v_q_norm_g, v_k_norm_g, v_sinks, v_w_conv_out, v_w_attn_out, v_gate_b, v_w_out):
    xi, yi, ci = _mesh_pos()
    chip = 2 * xi + yi
    c_idx = jnp.reshape(ci, (1,)).astype(jnp.int32)
    chip_idx = jnp.reshape(chip, (1,)).astype(jnp.int32)
    cc_idx = jnp.stack([ci, chip]).astype(jnp.int32)
    t = x.shape[1]
    xs = [x.reshape(t, D)]
    tgt = loss_target.reshape(t, D)

    full_w = [[_cast_w_in(chip_idx, w_in, l), _cast_w_small(chip_idx, w_conv_out, w_attn_out, w_out, l)]
              for l in range(N_LAYERS)]
    conv32 = lax.dynamic_update_slice(jnp.zeros((32, D), F32), jnp.pad(conv_w.reshape(3 * N_LAYERS, SH_ROW), ((0, 20), (0, 0))),
                                      (0, chip * SH_ROW))
    qg_s = jnp.tile(q_norm_g, (1, N_Q)) * (SCALE * LOG2E)
    kg_t = jnp.tile(k_norm_g, (1, N_KV))
    bias = _band_bias()
    saved = []
    for l in range(N_LAYERS):
        nxt = full_w[l + 1] if l + 1 < N_LAYERS else None
        (h, ht), got = _rmsnorm_fwd(xs[l], norm_g[l:l + 1], _gather_rider([full_w[0][0], conv32], "N") if l == 0 else None)
        if l == 0:
            got = _ride_alone(_gather_rider(got, "F"), "gather_first_forward")
            full_w[0][0], conv32 = _ride_alone(_gather_rider(got, "B"), "gather_first_d2d")
            conv_full = conv32[:3 * N_LAYERS].reshape(N_LAYERS, 3, D)
        (u_conv, u_qkv, u_za, u_gl), got = _in_proj(h, full_w[l][0], _gather_rider(nxt, "N") if nxt else None)
        if nxt:
            nxt[0], nxt[1] = got
        (qs, kvx), got = _qkv_prep(u_qkv, qg_s[l:l + 1], kg_t[l:l + 1],
                                   _gather_rider(full_w[0][1:], "N") if l == 0 else None)
        y_c, got = _conv_fwd(u_conv, conv_full[l], _gather_rider(got, "F") if l == 0 else None)
        o, lse, got = _attn_fwd(qs, kvx, u_za, sinks[l:l + 1], bias, _merge_riders(
            _gather_rider(nxt, "FB1") if nxt else None, _gather_rider(got, "B") if l == 0 else None))
        if nxt:
            nxt[0], nxt[1] = got[:2]
        if l == 0:
            full_w[0][1] = got[-1]
        (x_next, y_a, y_b, merged), got = _out_proj_fwd(xs[l], y_c, o, u_gl, gate_b[l:l + 1], full_w[l][1],
                                                        _gather_rider(nxt, "B2") if nxt else None)
        if nxt:
            nxt[0], nxt[1] = got
        xs.append(x_next)
        saved.append((ht, u_conv, u_qkv, u_za, u_gl, y_c, o, y_a, y_b, merged, qs, kvx, lse))

    dout, sq = _loss_head(xs[N_LAYERS], tgt)

    small, t_in, t_sm = [None] * N_LAYERS, [None] * N_LAYERS, [None] * N_LAYERS
    halves = None

    for l in reversed(range(N_LAYERS)):
        w_full, w_sm = full_w[l]
        last = l == 0
        ht, u_conv, u_qkv, u_za, u_gl, y_c, o, y_a, y_b, merged, qs, kvx, lse = saved[l]
        (d_ya, d_yb, du_gl, d_yc, d_o, dgb), _ = _out_proj_bwd(dout, y_a, y_b, u_gl, gate_b[l:l + 1], w_sm, None)
        g_sm = _small_wgrads(y_c, d_ya, o, d_yb, merged, dout).reshape(3, 4, 2, 128, D)
        (du_conv, dcw), got = _conv_bwd(d_yc, u_conv, conv_full[l], _swap_rider(None, g_sm) if last else None)
        if last:
            f_sm0, h_sm0 = _add_halves_sm(c_idx, g_sm, got[0])
        (dqs, dkv, dza, dsk), got = _attn_bwd(d_o, qs, kvx, u_za, lse, sinks[l:l + 1], bias,
                                              _scatter_rider(halves[1], None) if halves else None)
        if halves:
            t_in[l + 1] = _final_sum_in(halves[0], got[0])
        dsk = jnp.sum(dsk[0].reshape(N_KV, 2, 2, BLK), axis=-1).transpose(0, 2, 1).reshape(N_Q)
        (du_attn, dqg, dkg), _ = _qkv_post(u_qkv, dqs, dkv, dza, qg_s[l:l + 1], kg_t[l:l + 1], None)
        du = (du_conv, du_attn, du_gl)
        g_in, got = _in_proj_wgrad(ht, du, _merge_riders(
            _scatter_rider(None, halves[3]) if halves else None, _scatter_rider(None, h_sm0) if last else None))
        g_in = g_in.reshape(2, 512, N_IN)
        if halves:
            t_sm[l + 1] = _final_sum_sm(chip_idx, halves[2], got[0])
        if last:
            t_sm[0] = _final_sum_sm(chip_idx, f_sm0, got[-1])
        if last:
            f_in0, h_in0 = _add_halves_in(cc_idx, g_in, _ride_alone(_swap_rider(g_in, None), "swap_last")[0])
        (dout, dng), got = _in_proj_bwd(du, w_full, xs[l], norm_g[l:l + 1], dout,
                                        _scatter_rider(h_in0, None) if last else _swap_rider(g_in, g_sm))
        if last:
            t_in[0] = _final_sum_in(f_in0, got[0])
        else:
            halves = _add_halves_in(cc_idx, g_in, got[0]) + _add_halves_sm(c_idx, g_sm, got[1])
        small[l] = (jnp.sum(dng, axis=0), (SCALE * LOG2E) * jnp.sum(dqg.reshape(8 * N_Q, HEAD), axis=0),
                    jnp.sum(dkg.reshape(8 * N_KV, HEAD), axis=0), dsk, jnp.sum(dgb, axis=0), dcw[:3])
    grad_x = dout.reshape(1, t, D)

    stack = lambda k: jnp.stack([small[l][k] for l in range(N_LAYERS)])
    pack = jnp.concatenate([_pad_rows(stack(0), 32), _pad_rows(stack(1), 8), _pad_rows(stack(2), 8),
                            _pad_rows(stack(3), 8), _pad_rows(stack(4), 64), _pad_rows(stack(5), 96),
                            _pad_rows(jnp.sum(sq) * (0.5 / D), 8)], axis=0)
    red = _allreduce_small(pack)
    loss = red[216, 0]
    g_norm_g = red[0:32].reshape(N_LAYERS, D)
    g_q_norm_g = red[32:40].reshape(-1)[:N_LAYERS * HEAD].reshape(N_LAYERS, HEAD)
    g_k_norm_g = red[40:48].reshape(-1)[:N_LAYERS * HEAD].reshape(N_LAYERS, HEAD)
    g_sinks = red[48:56].reshape(-1)[:N_LAYERS * N_Q].reshape(N_LAYERS, N_Q)
    g_gate_b = red[56:120].reshape(N_LAYERS, 2 * D)
    g_conv_full = red[120:216].reshape(N_LAYERS, 3, D)
    g_conv_w = lax.dynamic_slice(g_conv_full, (0, 0, chip * SH_ROW), (N_LAYERS, 3, SH_ROW))

    g_w_in, g_w_co, g_w_ao, g_w_out = _join_halves(t_in, t_sm)

    r_in = N_LAYERS * D
    g_w_in, d_in, nm_in, nv_in = (a.reshape(N_LAYERS, D, SH_IN) for a in _adamw_big(
        w_in.reshape(r_in, SH_IN), g_w_in.reshape(r_in, SH_IN), m_w_in.reshape(r_in, SH_IN),
        v_w_in.reshape(r_in, SH_IN), "adamw_w_in"))
    r_sm = N_LAYERS * SH_ROW
    big = {}
    for nm, w, g, m, v in (("co", w_conv_out, g_w_co, m_w_conv_out, v_w_conv_out),
                           ("ao", w_attn_out, g_w_ao, m_w_attn_out, v_w_attn_out),
                           ("out", w_out, g_w_out, m_w_out, v_w_out)):
        big[nm] = tuple(a.reshape(N_LAYERS, SH_ROW, D) for a in _adamw_big(
            w.reshape(r_sm, D), g.reshape(r_sm, D), m.reshape(r_sm, D), v.reshape(r_sm, D), "adamw_w_small"))
    g_w_co, g_w_ao, g_w_out = big["co"][0], big["ao"][0], big["out"][0]
    sm_w = [norm_g, conv_w, q_norm_g, k_norm_g, sinks, gate_b]
    sm_g = [g_norm_g, g_conv_w, g_q_norm_g, g_k_norm_g, g_sinks, g_gate_b]
    sm_m = [m_norm_g, m_conv_w, m_q_norm_g, m_k_norm_g, m_sinks, m_gate_b]
    sm_v = [v_norm_g, v_conv_w, v_q_norm_g, v_k_norm_g, v_sinks, v_gate_b]
    sd, snm, snv = _adamw_small(sm_w, sm_g, sm_m, sm_v)

    def order(norm, w_in_, conv, qn, kn, sk, co, ao, gb, wo):
        return [norm, w_in_, conv, qn, kn, sk, co, ao, gb, wo]

    grads = order(g_norm_g, g_w_in, g_conv_w, g_q_norm_g, g_k_norm_g, g_sinks, g_w_co, g_w_ao, g_gate_b, g_w_out)
    deltas = order(sd[0], d_in, sd[1], sd[2], sd[3], sd[4], big["co"][1], big["ao"][1], sd[5], big["out"][1])
    new_m = order(snm[0], nm_in, snm[1], snm[2], snm[3], snm[4], big["co"][2], big["ao"][2], snm[5], big["out"][2])
    new_v = order(snv[0], nv_in, snv[1], snv[2], snv[3], snv[4], big["co"][3], big["ao"][3], snv[5], big["out"][3])
    return (loss, grad_x, *grads, *deltas, *new_m, *new_v)
```

```python
import functools

import jax
import jax.numpy as jnp
from jax import lax
from jax.experimental import pallas as pl
from jax.experimental.pallas import tpu as pltpu

F32, BF16 = jnp.float32, jnp.bfloat16
SDS = jax.ShapeDtypeStruct
MESH = pl.DeviceIdType.MESH
ANY = pl.BlockSpec(memory_space=pl.ANY)

D = 1024
N_IN = 8704
N_LAYERS = 4
N_Q, N_KV, HEAD = 16, 4, 64
GROUP = N_Q // N_KV
BLK = 128
EPS = 1e-6
NEG = -1e30
SCALE = HEAD ** -0.5
SH_IN = N_IN // 4
SH_ROW = D // 4
CB = 512
VMEM_BIG = 56 * 1024 * 1024

ADAM_LR, ADAM_B1, ADAM_B2, ADAM_EPS, ADAM_WD, ADAM_STEP = 0.001, 0.9, 0.999, 1e-08, 0.01, 10


def _cp(*sem, vmem=None):
    return pltpu.CompilerParams(dimension_semantics=sem if sem else None, vmem_limit_bytes=vmem)


def _sigmoid(z):
    return 1.0 / (1.0 + jnp.exp(-z))


def _dot(a, b):
    return jnp.dot(a, b, preferred_element_type=F32)


def _dot_nt(a, b):
    return lax.dot_general(a, b, (((1,), (1,)), ((), ())), preferred_element_type=F32)


def _dot_tn(a, b):
    return lax.dot_general(a, b, (((0,), (0,)), ((), ())), preferred_element_type=F32)


def _fold8(v):
    return jnp.sum(v.reshape(v.shape[0] // 8, 8, v.shape[1]), axis=0)


def _cast_w_in(chip_idx, w, layer):
    def body(j_ref, i_ref, o_ref):
        o_ref[...] = i_ref[...].astype(BF16)

    return pl.pallas_call(
        body, name="cast_w_in",
        grid_spec=pltpu.PrefetchScalarGridSpec(
            num_scalar_prefetch=1, grid=(2,),
            in_specs=[pl.BlockSpec((None, 512, SH_IN), lambda i, j: (layer, i, 0))],
            out_specs=pl.BlockSpec((512, SH_IN), lambda i, j: (i, j[0]))),
        out_shape=SDS((D, N_IN), BF16), compiler_params=_cp("parallel"),
    )(chip_idx, w)


def _cast_w_small(chip_idx, a, b, c, layer):
    def body(j_ref, a_ref, b_ref, c_ref, o_ref):
        o_ref[0] = a_ref[...].astype(BF16)
        o_ref[1] = b_ref[...].astype(BF16)
        o_ref[2] = c_ref[...].astype(BF16)

    spec = pl.BlockSpec((None, SH_ROW, D), lambda i, j: (layer, 0, 0))
    return pl.pallas_call(
        body, name="cast_w_small",
        grid_spec=pltpu.PrefetchScalarGridSpec(
            num_scalar_prefetch=1, grid=(1,), in_specs=[spec, spec, spec],
            out_specs=pl.BlockSpec((3, SH_ROW, D), lambda i, j: (0, j[0], 0))),
        out_shape=SDS((3, D, D), BF16), compiler_params=_cp("parallel"),
    )(chip_idx, a, b, c)


def _mesh_pos():
    return lax.axis_index("x"), lax.axis_index("y"), lax.axis_index("c")


def _other_chips(x, y):
    return [(1 - x, y), (x, 1 - y), (1 - x, 1 - y)]


class _Rider:
    def __init__(self, ins, out_shape, n, copies, aliases=()):
        self.ins, self.out_shape, self.n, self.copies, self.aliases = list(ins), list(out_shape), n, copies, aliases


def _merge_riders(*riders):
    riders = [r for r in riders if r is not None]
    if len(riders) < 2:
        return riders[0] if riders else None

    def copies(ins, outs, send, recv, base=0):
        cps, i0, o0 = [], 0, 0
        for r in riders:
            cps += r.copies(ins[i0:i0 + len(r.ins)], outs[o0:o0 + len(r.out_shape)], send, recv, base + len(cps))
            i0, o0 = i0 + len(r.ins), o0 + len(r.out_shape)
        return cps

    aliases, i0, o0 = [], 0, 0
    for r in riders:
        aliases += [(i0 + i, o0 + o) for i, o in r.aliases]
        i0, o0 = i0 + len(r.ins), o0 + len(r.out_shape)
    return _Rider(sum((r.ins for r in riders), []), sum((r.out_shape for r in riders), []),
                  sum(r.n for r in riders), copies, tuple(aliases))


def _rcopy(src, dst, send, recv, k, to):
    return pltpu.make_async_remote_copy(src_ref=src, dst_ref=dst, send_sem=send.at[k], recv_sem=recv.at[k],
                                        device_id=to, device_id_type=MESH)


def _hosted_call(body, rider, *, name, grid, in_specs, out_specs, out_shape, args, scratch_shapes=(), vmem=None):
    n_in, n_out, n_scr = len(in_specs), len(out_specs), len(scratch_shapes)
    r_in, r_out = len(rider.ins), len(rider.out_shape)

    def full_body(*refs):
        host_in, rid_in = refs[:n_in], refs[n_in:n_in + r_in]
        o0 = n_in + r_in
        host_out, rid_out = refs[o0:o0 + n_out], refs[o0 + n_out:o0 + n_out + r_out]
        s0 = o0 + n_out + r_out
        host_scr, (send, recv) = refs[s0:s0 + n_scr], refs[s0 + n_scr:]
        if body is None:
            cps = rider.copies(rid_in, rid_out, send, recv)
            for cp in cps:
                cp.start()
            for cp in cps:
                cp.wait()
            return
        ids = [pl.program_id(a) for a in range(len(grid))]
        first = functools.reduce(lambda p, q: p & q, [i == 0 for i in ids])
        last = functools.reduce(lambda p, q: p & q, [i == g - 1 for i, g in zip(ids, grid)])

        @pl.when(first)
        def _():
            for cp in rider.copies(rid_in, rid_out, send, recv):
                cp.start()

        body(*host_in, *host_out, *host_scr)

        @pl.when(last)
        def _():
            for cp in rider.copies(rid_in, rid_out, send, recv):
                cp.wait()

    res = pl.pallas_call(
        full_body, name=name, grid=grid if body is not None else (),
        in_specs=list(in_specs) + [ANY] * r_in, out_specs=list(out_specs) + [ANY] * r_out,
        out_shape=list(out_shape) + rider.out_shape,
        scratch_shapes=list(scratch_shapes) + [pltpu.SemaphoreType.DMA((rider.n,))] * 2,
        input_output_aliases={n_in + i: n_out + o for i, o in rider.aliases},
        compiler_params=_cp(*(("arbitrary",) * len(grid) if body is not None else ()), vmem=vmem),
    )(*args, *rider.ins)
    return res[:n_out], res[n_out:]


def _call(body, rider, **kw):
    if rider is not None:
        return _hosted_call(body, rider, **kw)
    res = pl.pallas_call(
        body, name=kw["name"], grid=kw["grid"], in_specs=list(kw["in_specs"]), out_specs=list(kw["out_specs"]),
        out_shape=list(kw["out_shape"]), scratch_shapes=list(kw.get("scratch_shapes", ())),
        compiler_params=_cp(*(("arbitrary",) * len(kw["grid"])), vmem=kw.get("vmem")),
    )(*kw["args"])
    return res, []


def _gather_rider(arrays, stage):
    def region(full, whose, c, sub):
        if len(full.shape) == 2:
            rows, cols = full.shape[0] // 2, full.shape[1] // 4
            first, n = (c * rows, rows) if sub is None else (c * rows + sub * (rows // 2), rows // 2)
            return full.at[pl.ds(pl.multiple_of(first, n), n), pl.ds(pl.multiple_of(whose * cols, 128), cols)]
        first, n = (whose * SH_ROW + c * 128, 128) if sub is None else (whose * SH_ROW + c * 128 + sub * 64, 64)
        return full.at[:, pl.ds(pl.multiple_of(first, n), n), :]

    def copies(ins, outs, send, recv, base=0):
        x, y, c = _mesh_pos()
        nbr_x, nbr_y = (1 - x, y), (x, 1 - y)
        cps = []
        for full in outs:
            plan = []
            if stage == "N":
                plan = [(region(full, 2 * x + y, c, None), (*nbr_x, c)), (region(full, 2 * x + y, c, None), (*nbr_y, c))]
            if stage in ("F", "FB1"):
                plan = [(region(full, 2 * nbr_x[0] + nbr_x[1], c, 0), (*nbr_y, c)),
                        (region(full, 2 * nbr_y[0] + nbr_y[1], c, 1), (*nbr_x, c))]
            if stage in ("B", "FB1", "B2"):
                chips = {"B": _other_chips(x, y), "FB1": [nbr_x, nbr_y], "B2": [(1 - x, 1 - y)]}[stage]
                plan += [(region(full, 2 * chip[0] + chip[1], c, None), (x, y, 1 - c)) for chip in chips]
            for reg, to in plan:
                cps.append(_rcopy(reg, reg, send, recv, base + len(cps), to))
        return cps

    per_array = {"N": 2, "F": 2, "B": 3, "FB1": 4, "B2": 1}[stage]
    return _Rider(arrays, [SDS(v.shape, v.dtype) for v in arrays], per_array * len(arrays), copies,
                  aliases=tuple((i, i) for i in range(len(arrays))))


def _staged_copies(copies, stages, sem_in, sem_out):
    busy, count = {}, {}
    for idx, (src, dst, kind) in enumerate(copies):
        slot = count.get(kind, 0) % 2
        count[kind] = count.get(kind, 0) + 1
        if (kind, slot) in busy:
            busy.pop((kind, slot)).wait()
        buf = stages[kind].at[slot]
        cin = pltpu.make_async_copy(src, buf, sem_in.at[idx])
        cin.start()
        cin.wait()
        cout = pltpu.make_async_copy(buf, dst, sem_out.at[idx])
        cout.start()
        busy[(kind, slot)] = cout
    for cp in busy.values():
        cp.wait()


def _allreduce_small(pack):
    rows = pack.shape[0]

    def body(p_ref, o_ref, buf, send, recv):
        x, y, c = _mesh_pos()
        me = 4 * x + 2 * y + c
        sends = []
        for r in range(1, 8):
            to = (x if not (r & 4) else 1 - x, y if not (r & 2) else 1 - y, c if not (r & 1) else 1 - c)
            cp = pltpu.make_async_remote_copy(src_ref=p_ref, dst_ref=buf.at[me], send_sem=send.at[r - 1],
                                              recv_sem=recv.at[r - 1], device_id=to, device_id_type=MESH)
            cp.start()
            sends.append(cp)
        buf[me] = p_ref[...]
        for r in range(1, 8):
            frm = (4 * x + 2 * y + c) ^ r
            pltpu.make_async_remote_copy(src_ref=p_ref, dst_ref=buf.at[frm], send_sem=send.at[r - 1],
                                         recv_sem=recv.at[r - 1], device_id=(x, y, c), device_id_type=MESH).wait_recv()
        acc = buf[0]
        for d in range(1, 8):
            acc = acc + buf[d]
        o_ref[...] = acc
        for cp in sends:
            cp.wait_send()

    vm = pl.BlockSpec(memory_space=pltpu.VMEM)
    return pl.pallas_call(
        body, name="allreduce_small", in_specs=[vm], out_specs=vm, out_shape=SDS((rows, 128), F32),
        scratch_shapes=[pltpu.VMEM((8, rows, 128), F32), pltpu.SemaphoreType.DMA((7,)), pltpu.SemaphoreType.DMA((7,))],
    )(pack)


def _rmsnorm_fwd(x, g, rider):
    t = x.shape[0]
    tm = min(1024, t)

    def body(x_ref, g_ref, h_ref, ht_ref):
        xv = x_ref[...]
        r = lax.rsqrt(jnp.mean(xv * xv, axis=-1, keepdims=True) + EPS)
        h = xv * r * g_ref[...]
        h_ref[...] = h.astype(BF16)
        ht_ref[...] = h.T.astype(BF16)

    return _call(
        body, rider, name="rmsnorm_fwd", grid=(t // tm,),
        in_specs=[pl.BlockSpec((tm, D), lambda i: (i, 0)), pl.BlockSpec((1, D), lambda i: (0, 0))],
        out_specs=[pl.BlockSpec((tm, D), lambda i: (i, 0)), pl.BlockSpec((D, tm), lambda i: (0, i))],
        out_shape=[SDS((t, D), BF16), SDS((D, t), BF16)], args=(x, g), vmem=VMEM_BIG)


FWD_SEGS = ((0, 8), (8, 3), (11, 2), (13, 4))


def _in_proj(h, w_full, rider):
    t = h.shape[0]
    tm = min(2048, t)

    def body(a_ref, b_ref, *o_refs):
        j = pl.program_id(1)
        for o_ref, (off, nblk) in zip(o_refs, FWD_SEGS):
            @pl.when((j >= off) & (j < off + nblk))
            def _():
                o_ref[...] = _dot(a_ref[...], b_ref[...]).astype(BF16)

    def out(seg):
        off, nblk = seg
        return pl.BlockSpec((tm, CB), lambda i, j: (i, jnp.clip(j - off, 0, nblk - 1)))

    res, got = _call(
        body, rider, name="in_proj", grid=(t // tm, N_IN // CB),
        in_specs=[pl.BlockSpec((tm, D), lambda i, j: (i, 0)), pl.BlockSpec((D, CB), lambda i, j: (0, j))],
        out_specs=[out(s) for s in FWD_SEGS], out_shape=[SDS((t, s[1] * CB), BF16) for s in FWD_SEGS],
        args=(h, w_full), vmem=VMEM_BIG)
    return res, got


def _conv_fwd(u_conv, conv_w, rider):
    t = u_conv.shape[0]
    tm = min(512, t)
    hb = tm // 16

    def body(v_ref, b_ref, c_ref, z_ref, hv_ref, hc_ref, w_ref, y_ref):
        i = pl.program_id(0)
        cv = c_ref[...].astype(F32) * v_ref[...].astype(F32)
        halo = hc_ref[...].astype(F32) * hv_ref[...].astype(F32)
        halo = jnp.where(i > 0, halo, 0.0)
        row = lax.broadcasted_iota(jnp.int32, (tm, 1), 0)
        s1 = jnp.where(row == 0, halo[15:16], pltpu.roll(cv, 1, 0))
        s2 = jnp.where(row == 0, halo[14:15], jnp.where(row == 1, halo[15:16], pltpu.roll(cv, 2, 0)))
        conv = w_ref[0:1, :] * s2 + w_ref[1:2, :] * s1 + w_ref[2:3, :] * cv
        z = z_ref[...].astype(F32)
        y_ref[...] = (b_ref[...].astype(F32) * conv * (z * _sigmoid(z))).astype(BF16)

    def col(k):
        return pl.BlockSpec((tm, D), lambda i: (i, k))

    def halo(k):
        return pl.BlockSpec((16, D), lambda i: (jnp.maximum(i * hb - 1, 0), k))

    (y_c,), got = _call(
        body, rider, name="conv_fwd", grid=(t // tm,),
        in_specs=[col(0), col(1), col(2), col(3), halo(0), halo(2), pl.BlockSpec((3, D), lambda i: (0, 0))],
        out_specs=[pl.BlockSpec((tm, D), lambda i: (i, 0))], out_shape=[SDS((t, D), BF16)],
        args=(u_conv, u_conv, u_conv, u_conv, u_conv, u_conv, conv_w), vmem=VMEM_BIG)
    return y_c, got


KVX = 4 * N_KV * 128


def _iota2(shape):
    return lax.broadcasted_iota(jnp.int32, shape, 0), lax.broadcasted_iota(jnp.int32, shape, 1)


def _head_sum(v):
    r, c = _iota2((128, 128))
    ones = ((r >> 6) == (c >> 6)).astype(BF16)
    hi = v.astype(BF16)
    lo = (v - hi.astype(F32)).astype(BF16)
    return jnp.concatenate([_dot(hi[:, g:g + 128], ones) + _dot(lo[:, g:g + 128], ones)
                            for g in range(0, v.shape[1], 128)], axis=1)


def _expand_mats():
    r, c = _iota2((N_KV * HEAD, N_KV * 128))
    base = ((r >> 6) << 7) + (r & 63)
    return (c == base).astype(BF16), (c == base + 64).astype(BF16)


def _fold_mat():
    r, c = _iota2((N_KV * 128, N_KV * HEAD))
    return (((r >> 7) == (c >> 6)) & ((r & 63) == (c & 63))).astype(BF16)


def _qkv_prep(u_qkv, qg_s, kg_t, rider):
    t = u_qkv.shape[0]
    tm = min(1024, t)

    def body(u_ref, qg_ref, kg_ref, qs_ref, kvx_ref):
        q = u_ref[:, 0:D].astype(F32)
        rq = lax.rsqrt(_head_sum(q * q) * (1.0 / HEAD) + EPS)
        qs_ref[...] = (q * rq * qg_ref[...]).astype(BF16)
        k = u_ref[:, D:D + 256].astype(F32)
        rk = lax.rsqrt(_head_sum(k * k) * (1.0 / HEAD) + EPS)
        kn = (k * rk * kg_ref[...]).astype(BF16)
        v = u_ref[:, D + 256:D + 512]
        e_lo, e_hi = _expand_mats()
        kvx_ref[:, 0:512] = _dot(kn, e_lo).astype(BF16)
        kvx_ref[:, 512:1024] = _dot(kn, e_hi).astype(BF16)
        kvx_ref[:, 1024:1536] = _dot(v, e_lo).astype(BF16)
        kvx_ref[:, 1536:2048] = _dot(v, e_hi).astype(BF16)

    return _call(
        body, rider, name="qkv_prep", grid=(t // tm,),
        in_specs=[pl.BlockSpec((tm, 1536), lambda i: (i, 0)), pl.BlockSpec((1, D), lambda i: (0, 0)),
                  pl.BlockSpec((1, 256), lambda i: (0, 0))],
        out_specs=[pl.BlockSpec((tm, D), lambda i: (i, 0)), pl.BlockSpec((tm, KVX), lambda i: (i, 0))],
        out_shape=[SDS((t, D), BF16), SDS((t, KVX), BF16)], args=(u_qkv, qg_s, kg_t), vmem=VMEM_BIG)


def _band_bias():
    j, r = _iota2((2 * BLK, 2 * BLK))
    diff = (r & (BLK - 1)) - j + BLK
    band = (diff >= 0) & (diff < BLK)
    return jnp.stack([jnp.where(band & (j >= BLK), 0.0, NEG), jnp.where(band, 0.0, NEG)]).astype(F32)


def _pair_rows(ref_or_val, hk):
    return jnp.concatenate([ref_or_val[:, 256 * hk:256 * hk + 128], ref_or_val[:, 256 * hk + 128:256 * hk + 256]], axis=0)


LOG2E, LN2 = 1.4426950408889634, 0.6931471805599453


def _sink_row(sink_ref, hk, half):
    return jnp.concatenate([jnp.full((1, BLK), sink_ref[0, GROUP * hk + half] * LOG2E, F32),
                            jnp.full((1, BLK), sink_ref[0, GROUP * hk + 2 + half] * LOG2E, F32)], axis=1)


def _kv_operands(kvb, hk, half):
    return (kvb[:, 512 * half + 128 * hk:512 * half + 128 * hk + 128],
            kvb[:, 1024 + 512 * half + 128 * hk:1024 + 512 * half + 128 * hk + 128])


def _attn_fwd(qs, kvx, u_za, sinks, bias, rider):
    t = qs.shape[0]
    nb = t // BLK

    def body(q_ref, kc_ref, kp_ref, za_ref, sink_ref, bias_ref, o_ref, lse_ref):
        kvb = jnp.concatenate([kp_ref[...], kc_ref[...]], axis=0)
        bias_v = bias_ref[...]
        key0 = lax.broadcasted_iota(jnp.int32, (2 * BLK, 1), 0) == 0
        ones = jnp.ones((2 * BLK, 128), BF16)
        cols = []
        for hk in range(N_KV):
            qpp = _pair_rows(q_ref, hk)
            opp = None
            for half in range(2):
                kx, vx = _kv_operands(kvb, hk, half)
                s = _dot_nt(kx, qpp) + bias_v
                sink = _sink_row(sink_ref, hk, half)
                m = jnp.maximum(jnp.max(s, axis=0, keepdims=True), sink)
                p = jnp.exp2(s - m)
                es = jnp.exp2(sink - m)
                lse_ref[0, 2 * hk + half:2 * hk + half + 1, :] = m + jnp.log(jnp.sum(p, axis=0, keepdims=True) + es) * LOG2E
                pe = jnp.where(key0, es, p).astype(BF16)
                rhs = jnp.concatenate([jnp.where(key0, jnp.zeros_like(vx), vx), ones], axis=1)
                nd = _dot_tn(pe, rhs)
                o = nd[:, :128] * (1.0 / nd[:, 128:])
                opp = o if opp is None else opp + o
            cols += [opp[:BLK], opp[BLK:]]
        za = za_ref[...].astype(F32)
        o_ref[...] = (jnp.concatenate(cols, axis=1) * (za * _sigmoid(za))).astype(BF16)

    prev = lambda n: jnp.maximum(n - 1, 0)
    (o, lse), got = _call(
        body, rider, name="attn_fwd", grid=(nb,),
        in_specs=[pl.BlockSpec((BLK, D), lambda n: (n, 0)),
                  pl.BlockSpec((BLK, KVX), lambda n: (n, 0)), pl.BlockSpec((BLK, KVX), lambda n: (prev(n), 0)),
                  pl.BlockSpec((BLK, D), lambda n: (n, 0)), pl.BlockSpec(memory_space=pltpu.SMEM),
                  pl.BlockSpec((None, 2 * BLK, 2 * BLK), lambda n: (jnp.minimum(n, 1), 0, 0))],
        out_specs=[pl.BlockSpec((BLK, D), lambda n: (n, 0)), pl.BlockSpec((1, 8, 2 * BLK), lambda n: (n, 0, 0))],
        out_shape=[SDS((t, D), BF16), SDS((nb, 8, 2 * BLK), F32)],
        args=(qs, kvx, kvx, u_za, sinks, bias), vmem=VMEM_BIG)
    return o, lse, got


def _out_proj_fwd(x, y_c, o, u_gl, gate_b, w_sm, rider):
    t = x.shape[0]
    tm = min(512, t)

    def body(x_ref, yc_ref, o_ref, gla_ref, glb_ref, gb_ref, wco_ref, wao_ref, wout_ref,
             xn_ref, ya_ref, yb_ref, mg_ref):
        ya = _dot(yc_ref[...], wco_ref[...])
        yb = _dot(o_ref[...], wao_ref[...])
        gb = gb_ref[...]
        ga_ = _sigmoid(gla_ref[...].astype(F32) + gb[:, :D])
        gb_ = _sigmoid(glb_ref[...].astype(F32) + gb[:, D:])
        merged = (ga_ * ya + gb_ * yb).astype(BF16)
        ya_ref[...] = ya.astype(BF16)
        yb_ref[...] = yb.astype(BF16)
        mg_ref[...] = merged
        xn_ref[...] = x_ref[...] + _dot(merged, wout_ref[...])

    row = pl.BlockSpec((tm, D), lambda i: (i, 0))
    wspec = lambda a: pl.BlockSpec((None, D, D), lambda i: (a, 0, 0))
    return _call(
        body, rider, name="out_proj_fwd", grid=(t // tm,),
        in_specs=[row, row, row, pl.BlockSpec((tm, D), lambda i: (i, 0)), pl.BlockSpec((tm, D), lambda i: (i, 1)),
                  pl.BlockSpec((1, 2 * D), lambda i: (0, 0)), wspec(0), wspec(1), wspec(2)],
        out_specs=[row, row, row, row],
        out_shape=[SDS((t, D), F32), SDS((t, D), BF16), SDS((t, D), BF16), SDS((t, D), BF16)],
        args=(x, y_c, o, u_gl, u_gl, gate_b, w_sm, w_sm, w_sm), vmem=VMEM_BIG)


def _loss_head(y, tgt):
    t = y.shape[0]
    tm = min(1024, t)

    def body(y_ref, t_ref, dy_ref, acc_ref):
        @pl.when(pl.program_id(0) == 0)
        def _():
            acc_ref[...] = jnp.zeros_like(acc_ref)
        err = y_ref[...] - t_ref[...]
        dy_ref[...] = err * (1.0 / D)
        sq = _fold8(err * err)
        tot = sq[:, 0:128]
        for k in range(1, D // 128):
            tot = tot + sq[:, 128 * k:128 * (k + 1)]
        acc_ref[...] += tot

    row = pl.BlockSpec((tm, D), lambda i: (i, 0))
    return pl.pallas_call(
        body, name="loss_head", grid=(t // tm,), in_specs=[row, row],
        out_specs=[row, pl.BlockSpec((8, 128), lambda i: (0, 0))],
        out_shape=[SDS((t, D), F32), SDS((8, 128), F32)], compiler_params=_cp("arbitrary"),
    )(y, tgt)


def _out_proj_bwd(dout, y_a, y_b, u_gl, gate_b, w_sm, rider):
    t = dout.shape[0]
    tm = min(512, t)

    def body(do_ref, ya_ref, yb_ref, gla_ref, glb_ref, gb_ref, wco_ref, wao_ref, wout_ref,
             dya_ref, dyb_ref, dgl_ref, dyc_ref, dob_ref, dgb_ref):
        @pl.when(pl.program_id(0) == 0)
        def _():
            dgb_ref[...] = jnp.zeros_like(dgb_ref)
        dm = _dot_nt(do_ref[...].astype(BF16), wout_ref[...])
        gb = gb_ref[...]
        ga_ = _sigmoid(gla_ref[...].astype(F32) + gb[:, :D])
        gb_ = _sigmoid(glb_ref[...].astype(F32) + gb[:, D:])
        dya = (ga_ * dm).astype(BF16)
        dyb = (gb_ * dm).astype(BF16)
        dgla = ya_ref[...].astype(F32) * dm * (ga_ * (1.0 - ga_))
        dglb = yb_ref[...].astype(F32) * dm * (gb_ * (1.0 - gb_))
        dya_ref[...] = dya
        dyb_ref[...] = dyb
        dgl_ref[:, :D] = dgla.astype(BF16)
        dgl_ref[:, D:] = dglb.astype(BF16)
        dgb_ref[:, :D] += _fold8(dgla)
        dgb_ref[:, D:] += _fold8(dglb)
        dyc_ref[...] = _dot_nt(dya, wco_ref[...]).astype(BF16)
        dob_ref[...] = _dot_nt(dyb, wao_ref[...]).astype(BF16)

    row = pl.BlockSpec((tm, D), lambda i: (i, 0))
    wspec = lambda a: pl.BlockSpec((None, D, D), lambda i: (a, 0, 0))
    return _call(
        body, rider, name="out_proj_bwd", grid=(t // tm,),
        in_specs=[row, row, row, pl.BlockSpec((tm, D), lambda i: (i, 0)), pl.BlockSpec((tm, D), lambda i: (i, 1)),
                  pl.BlockSpec((1, 2 * D), lambda i: (0, 0)), wspec(0), wspec(1), wspec(2)],
        out_specs=[row, row, pl.BlockSpec((tm, 2 * D), lambda i: (i, 0)), row, row,
                   pl.BlockSpec((8, 2 * D), lambda i: (0, 0))],
        out_shape=[SDS((t, D), BF16), SDS((t, D), BF16), SDS((t, 2 * D), BF16), SDS((t, D), BF16), SDS((t, D), BF16),
                   SDS((8, 2 * D), F32)],
        args=(dout, y_a, y_b, u_gl, u_gl, gate_b, w_sm, w_sm, w_sm), vmem=VMEM_BIG)


def _small_wgrads(y_c, d_ya, o, d_yb, merged, dout):
    t = y_c.shape[0]
    tk = min(512, t)

    def body(yc_ref, dya_ref, o_ref, dyb_ref, mg_ref, do_ref, g_ref):
        @pl.when(pl.program_id(0) == 0)
        def _():
            g_ref[...] = jnp.zeros_like(g_ref)
        g_ref[0] += _dot_tn(yc_ref[...], dya_ref[...])
        g_ref[1] += _dot_tn(o_ref[...], dyb_ref[...])
        g_ref[2] += _dot_tn(mg_ref[...], do_ref[...].astype(BF16))

    row = pl.BlockSpec((tk, D), lambda k: (k, 0))
    return pl.pallas_call(
        body, name="small_wgrads", grid=(t // tk,), in_specs=[row] * 6,
        out_specs=pl.BlockSpec((3, D, D), lambda k: (0, 0, 0)), out_shape=SDS((3, D, D), F32),
        compiler_params=_cp("arbitrary", vmem=VMEM_BIG),
    )(y_c, d_ya, o, d_yb, merged, dout)


def _conv_bwd(d_yc, u_conv, conv_w, rider):
    t = d_yc.shape[0]
    tm = min(512, t)
    hb = tm // 16
    last_halo = t // 16 - 1
    n_steps = t // tm

    def body(dy_ref, v_ref, b_ref, c_ref, z_ref, hv_ref, hc_ref, ndy_ref, nb_ref, nz_ref, w_ref, du_ref, dw_ref):
        i = pl.program_id(0)

        @pl.when(i == 0)
        def _():
            dw_ref[...] = jnp.zeros_like(dw_ref)
        v, c = v_ref[...].astype(F32), c_ref[...].astype(F32)
        b, z = b_ref[...].astype(F32), z_ref[...].astype(F32)
        cv = c * v
        halo = jnp.where(i > 0, hc_ref[...].astype(F32) * hv_ref[...].astype(F32), 0.0)
        row = lax.broadcasted_iota(jnp.int32, (tm, 1), 0)
        s1 = jnp.where(row == 0, halo[15:16], pltpu.roll(cv, 1, 0))
        s2 = jnp.where(row == 0, halo[14:15], jnp.where(row == 1, halo[15:16], pltpu.roll(cv, 2, 0)))
        w0, w1, w2 = w_ref[0:1, :], w_ref[1:2, :], w_ref[2:3, :]
        conv = w0 * s2 + w1 * s1 + w2 * cv
        sig = _sigmoid(z)
        sz = z * sig
        dsz = sig * (1.0 + z * (1.0 - sig))
        dy = dy_ref[...].astype(F32)
        dconv = dy * b * sz
        nz = nz_ref[...].astype(F32)
        nxt = ndy_ref[...].astype(F32) * nb_ref[...].astype(F32) * (nz * _sigmoid(nz))
        nxt = jnp.where(i < n_steps - 1, nxt, 0.0)
        a1 = jnp.where(row == tm - 1, nxt[0:1], pltpu.roll(dconv, tm - 1, 0))
        a2 = jnp.where(row == tm - 2, nxt[0:1], jnp.where(row == tm - 1, nxt[1:2], pltpu.roll(dconv, tm - 2, 0)))
        dcv = w2 * dconv + w1 * a1 + w0 * a2
        du_ref[:, 0:D] = (dcv * c).astype(BF16)
        du_ref[:, D:2 * D] = (dy * conv * sz).astype(BF16)
        du_ref[:, 2 * D:3 * D] = (dcv * v).astype(BF16)
        du_ref[:, 3 * D:4 * D] = (dy * b * conv * dsz).astype(BF16)
        r8 = lax.broadcasted_iota(jnp.int32, (8, 1), 0)
        dw_ref[...] += jnp.where(r8 == 0, jnp.sum(dconv * s2, axis=0, keepdims=True),
                                 jnp.where(r8 == 1, jnp.sum(dconv * s1, axis=0, keepdims=True),
                                           jnp.where(r8 == 2, jnp.sum(dconv * cv, axis=0, keepdims=True), 0.0)))

    def col(k):
        return pl.BlockSpec((tm, D), lambda i: (i, k))

    def halo(k):
        return pl.BlockSpec((16, D), lambda i: (jnp.maximum(i * hb - 1, 0), k))

    def nxt(k):
        return pl.BlockSpec((16, D), lambda i: (jnp.minimum((i + 1) * hb, last_halo), k))

    return _call(
        body, rider, name="conv_bwd", grid=(t // tm,),
        in_specs=[col(0), col(0), col(1), col(2), col(3), halo(0), halo(2), nxt(0), nxt(1), nxt(3),
                  pl.BlockSpec((3, D), lambda i: (0, 0))],
        out_specs=[pl.BlockSpec((tm, 4 * D), lambda i: (i, 0)), pl.BlockSpec((8, D), lambda i: (0, 0))],
        out_shape=[SDS((t, 4 * D), BF16), SDS((8, D), F32)],
        args=(d_yc, u_conv, u_conv, u_conv, u_conv, u_conv, u_conv, d_yc, u_conv, u_conv, conv_w), vmem=VMEM_BIG)


def _attn_bwd(d_o, qs, kvx, u_za, lse, sinks, bias, rider):
    t = d_o.shape[0]
    nb = t // BLK

    def body(q_ref, kc_ref, kp_ref, za_ref, do_ref, lse_ref, sink_ref, bias_ref,
             dq_ref, dkv_ref, dza_ref, dsk_ref, carry_ref):
        n = pl.program_id(0)

        @pl.when(n == 0)
        def _():
            carry_ref[...] = jnp.zeros_like(carry_ref)
            dsk_ref[...] = jnp.zeros_like(dsk_ref)

        live = n < nb
        kvb = jnp.concatenate([kp_ref[...], kc_ref[...]], axis=0)
        bias_v = bias_ref[...]
        za = za_ref[...].astype(F32)
        sig = _sigmoid(za)
        dsa = sig * (1.0 + za * (1.0 - sig))
        do = jnp.where(live, do_ref[...].astype(F32), 0.0)
        dattn_f = do * (za * sig)
        dattn = dattn_f.astype(BF16)
        dattn_ln2 = (dattn_f * LN2).astype(BF16)
        lo_lanes = lax.broadcasted_iota(jnp.int32, (1, 128), 1) < HEAD
        dq_cols, attn_cols, dk_cols, dv_cols, dsk_rows = [], [], [], [], []
        for hk in range(N_KV):
            qpp = _pair_rows(q_ref, hk)
            dapp = _pair_rows(dattn, hk)
            dapp_ln2 = _pair_rows(dattn_ln2, hk)
            probs, dss, xk, xv = [], [], [], []
            for half in range(2):
                kx, vx = _kv_operands(kvb, hk, half)
                lse = lse_ref[0, 2 * hk + half:2 * hk + half + 1, :]
                prob = jnp.exp2(_dot_nt(kx, qpp) + bias_v - lse)
                psink = jnp.exp2(_sink_row(sink_ref, hk, half) - lse)
                tdp = prob * _dot_nt(vx, dapp_ln2)
                drow = jnp.sum(tdp, axis=0, keepdims=True)
                ds = (tdp - prob * drow).astype(BF16)
                prob_b = prob.astype(BF16)
                xk.append(_dot(ds, qpp))
                xv.append(_dot(prob_b, dapp))
                probs.append(prob_b)
                dss.append(ds)
                dsk_rows.append(-psink * drow * LOG2E)
            kcat = jnp.concatenate([kvb[:, 128 * hk:128 * hk + 128], kvb[:, 512 + 128 * hk:512 + 128 * hk + 128]], axis=0)
            vcat = jnp.concatenate([kvb[:, 1024 + 128 * hk:1024 + 128 * hk + 128],
                                    kvb[:, 1536 + 128 * hk:1536 + 128 * hk + 128]], axis=0)
            app = _dot_tn(jnp.concatenate(probs, axis=0), vcat)
            dqpp = _dot_tn(jnp.concatenate(dss, axis=0), kcat)
            dq_cols += [dqpp[:BLK], dqpp[BLK:]]
            attn_cols += [app[:BLK], app[BLK:]]
            dk_cols.append(jnp.where(lo_lanes, xk[0], xk[1]))
            dv_cols.append(jnp.where(lo_lanes, xv[0], xv[1]))

        @pl.when(live)
        def _():
            dq_ref[...] = jnp.concatenate(dq_cols, axis=1).astype(BF16)
            dza_ref[...] = (do * jnp.concatenate(attn_cols, axis=1) * dsa).astype(BF16)

        band = jnp.concatenate(dk_cols + dv_cols, axis=1)
        dkv_ref[...] = (band[:BLK] + carry_ref[...]).astype(BF16)
        carry_ref[...] = band[BLK:]
        dsk_ref[...] += jnp.broadcast_to(jnp.concatenate(dsk_rows, axis=1), (8, 2 * N_KV * 2 * BLK))

    cur = lambda n: jnp.minimum(n, nb - 1)
    prev = lambda n: jnp.maximum(n - 1, 0)
    return _call(
        body, rider, name="attn_bwd", grid=(nb + 1,),
        in_specs=[pl.BlockSpec((BLK, D), lambda n: (cur(n), 0)),
                  pl.BlockSpec((BLK, KVX), lambda n: (cur(n), 0)), pl.BlockSpec((BLK, KVX), lambda n: (prev(n), 0)),
                  pl.BlockSpec((BLK, D), lambda n: (cur(n), 0)), pl.BlockSpec((BLK, D), lambda n: (cur(n), 0)),
                  pl.BlockSpec((1, 8, 2 * BLK), lambda n: (cur(n), 0, 0)), pl.BlockSpec(memory_space=pltpu.SMEM),
                  pl.BlockSpec((None, 2 * BLK, 2 * BLK), lambda n: (jnp.minimum(n, 1), 0, 0))],
        out_specs=[pl.BlockSpec((BLK, D), lambda n: (cur(n), 0)), pl.BlockSpec((BLK, D), lambda n: (prev(n), 0)),
                   pl.BlockSpec((BLK, D), lambda n: (cur(n), 0)), pl.BlockSpec((8, 2 * D), lambda n: (0, 0))],
        out_shape=[SDS((t, D), BF16), SDS((t, D), BF16), SDS((t, D), BF16), SDS((8, 2 * D), F32)],
        scratch_shapes=[pltpu.VMEM((BLK, D), F32)],
        args=(qs, kvx, kvx, u_za, d_o, lse, sinks, bias), vmem=VMEM_BIG)


def _qkv_post(u_qkv, dqs, dkv, dza, qg_s, kg_t, rider):
    t = u_qkv.shape[0]
    tm = min(512, t)

    def norm_bwd(x, dy, g):
        r = lax.rsqrt(_head_sum(x * x) * (1.0 / HEAD) + EPS)
        xhat = x * r
        dxh = dy * g
        return r * (dxh - xhat * (_head_sum(dxh * xhat) * (1.0 / HEAD))), _fold8(dy * xhat)

    def body(u_ref, dq_ref, dkv_ref, dza_ref, qg_ref, kg_ref, du_ref, dqg_ref, dkg_ref):
        @pl.when(pl.program_id(0) == 0)
        def _():
            dqg_ref[...] = jnp.zeros_like(dqg_ref)
            dkg_ref[...] = jnp.zeros_like(dkg_ref)
        dq, gq = norm_bwd(u_ref[:, 0:D].astype(F32), dq_ref[...].astype(F32), qg_ref[...])
        fold = _fold_mat()
        dk, gk = norm_bwd(u_ref[:, D:D + 256].astype(F32), _dot(dkv_ref[:, 0:512], fold), kg_ref[...])
        du_ref[:, 0:D] = dq.astype(BF16)
        du_ref[:, D:D + 256] = dk.astype(BF16)
        du_ref[:, D + 256:D + 512] = _dot(dkv_ref[:, 512:1024], fold).astype(BF16)
        du_ref[:, D + 512:2 * D + 512] = dza_ref[...]
        dqg_ref[...] += gq
        dkg_ref[...] += gk

    row = pl.BlockSpec((tm, D), lambda i: (i, 0))
    return _call(
        body, rider, name="qkv_post", grid=(t // tm,),
        in_specs=[pl.BlockSpec((tm, 1536), lambda i: (i, 0)), row, row, row,
                  pl.BlockSpec((1, D), lambda i: (0, 0)), pl.BlockSpec((1, 256), lambda i: (0, 0))],
        out_specs=[pl.BlockSpec((tm, 2560), lambda i: (i, 0)), pl.BlockSpec((8, D), lambda i: (0, 0)),
                   pl.BlockSpec((8, 256), lambda i: (0, 0))],
        out_shape=[SDS((t, 2560), BF16), SDS((8, D), F32), SDS((8, 256), F32)],
        args=(u_qkv, dqs, dkv, dza, qg_s, kg_t), vmem=VMEM_BIG)


N_GRAN = N_IN // CB
DU_COLS = ((0, 4096), (4096, 6656), (6656, N_IN))


def _du_granule(j):
    return jnp.clip(j, 0, 7), jnp.clip(j - 8, 0, 4), jnp.clip(j - 13, 0, 3)


def _du_select(j, refs, fn):
    for ref, lo, hi in zip(refs, (0, 8, 13), (8, 13, 17)):
        @pl.when((j >= lo) & (j < hi))
        def _():
            fn(ref)


def _in_proj_bwd(du, w_full, x, g, dout, rider):
    t = du[0].shape[0]
    tn = min(256, t)

    def body(a0, a1, a2, w_hbm, x_ref, g_ref, do_ref, dx_ref, dg_ref, w_ref, sems):
        first = pl.program_id(0) == 0
        pieces = [pltpu.make_async_copy(w_hbm.at[:, lo:hi], w_ref.at[:, lo:hi], sems.at[k])
                  for k, (lo, hi) in enumerate(DU_COLS)]

        @pl.when(first)
        def _():
            for cp in pieces:
                cp.start()
            dg_ref[...] = jnp.zeros_like(dg_ref)

        acc = None
        for a_ref, (lo, hi), cp in zip((a0, a1, a2), DU_COLS, pieces):
            @pl.when(first)
            def _():
                cp.wait()
            part = _dot_nt(w_ref[:, lo:hi], a_ref[...])
            acc = part if acc is None else acc + part
        dh = acc.T
        xv = x_ref[...]
        r = lax.rsqrt(jnp.mean(xv * xv, axis=-1, keepdims=True) + EPS)
        xhat = xv * r
        dg_ref[...] += _fold8(dh * xhat)
        dxh = dh * g_ref[...]
        dx_ref[...] = do_ref[...] + r * (dxh - xhat * jnp.mean(dxh * xhat, axis=-1, keepdims=True))

    row = pl.BlockSpec((tn, D), lambda i: (i, 0))
    return _call(
        body, rider, name="in_proj_bwd", grid=(t // tn,),
        in_specs=[pl.BlockSpec((tn, hi - lo), lambda i: (i, 0)) for lo, hi in DU_COLS]
        + [ANY, row, pl.BlockSpec((1, D), lambda i: (0, 0)), row],
        out_specs=[row, pl.BlockSpec((8, D), lambda i: (0, 0))], out_shape=[SDS((t, D), F32), SDS((8, D), F32)],
        scratch_shapes=[pltpu.VMEM((D, N_IN), BF16), pltpu.SemaphoreType.DMA((3,))],
        args=(*du, w_full, x, g, dout), vmem=VMEM_BIG)


def _in_proj_wgrad(ht, du, rider):
    t = ht.shape[1]
    tk = min(4096, t)
    n_k = t // tk

    def body(h_ref, b0, b1, b2, g_ref):
        j, k = pl.program_id(0), pl.program_id(1)

        if n_k > 1:
            @pl.when(k == 0)
            def _():
                g_ref[...] = jnp.zeros_like(g_ref)

        def add(b_ref):
            if n_k > 1:
                g_ref[...] += _dot(h_ref[...], b_ref[...])
            else:
                g_ref[...] = _dot(h_ref[...], b_ref[...])
        _du_select(j, (b0, b1, b2), add)

    seg = lambda q: pl.BlockSpec((tk, CB), lambda j, k: (k, _du_granule(j)[q]))
    (g,), got = _call(
        body, rider, name="in_proj_wgrad", grid=(N_GRAN, t // tk),
        in_specs=[pl.BlockSpec((D, tk), lambda j, k: (0, k)), seg(0), seg(1), seg(2)],
        out_specs=[pl.BlockSpec((D, CB), lambda j, k: (0, j))], out_shape=[SDS((D, N_IN), F32)],
        args=(ht, *du), vmem=VMEM_BIG)
    return g, got


def _swap_rider(g_in, g_sm):
    def copies(ins, outs, send, recv, base=0):
        x, y, c = _mesh_pos()
        cps = []
        for src, dst in zip(ins, outs):
            half = src.at[1 - c] if len(src.shape) == 3 else src.at[:, :, 1 - c]
            cps.append(_rcopy(half, dst, send, recv, base + len(cps), (x, y, 1 - c)))
        return cps

    arrays = [g for g in (g_in, g_sm) if g is not None]
    shapes = [SDS((512, N_IN), F32) if len(g.shape) == 3 else SDS((3, 4, 128, D), F32) for g in arrays]
    return _Rider(arrays, shapes, len(arrays), copies)


def _add_halves_in(cc_idx, g_in, r_in):
    def body(cc_ref, a_ref, b_ref, f_ref, h_ref):
        s = a_ref[...] + b_ref[...]
        h_ref[...] = s.astype(BF16)

        @pl.when(pl.program_id(1) == cc_ref[1])
        def _():
            f_ref[...] = s

    blk = pl.BlockSpec((256, SH_IN), lambda i, j, cc: (i, j))
    return pl.pallas_call(
        body, name="add_halves_in",
        grid_spec=pltpu.PrefetchScalarGridSpec(
            num_scalar_prefetch=1, grid=(2, 4),
            in_specs=[pl.BlockSpec((None, 256, SH_IN), lambda i, j, cc: (cc[0], i, j)), blk],
            out_specs=[pl.BlockSpec((256, SH_IN), lambda i, j, cc: (i, 0)), blk]),
        out_shape=[SDS((512, SH_IN), F32), SDS((512, N_IN), BF16)],
        compiler_params=_cp("arbitrary", "arbitrary", vmem=VMEM_BIG),
    )(cc_idx, g_in, r_in)


def _add_halves_sm(c_idx, g_sm, r_sm):
    def body(c_ref, a_ref, b_ref, f_ref, h_ref):
        s = a_ref[...] + b_ref[...]
        f_ref[...] = s
        h_ref[...] = s.astype(BF16)

    blk = pl.BlockSpec((1, 4, 128, D), lambda a, c: (a, 0, 0, 0))
    return pl.pallas_call(
        body, name="add_halves_sm",
        grid_spec=pltpu.PrefetchScalarGridSpec(
            num_scalar_prefetch=1, grid=(3,),
            in_specs=[pl.BlockSpec((1, 4, None, 128, D), lambda a, c: (a, 0, c[0], 0, 0)), blk], out_specs=[blk, blk]),
        out_shape=[SDS((3, 4, 128, D), F32), SDS((3, 4, 128, D), BF16)], compiler_params=_cp("parallel"),
    )(c_idx, g_sm, r_sm)


def _scatter_rider(h_in, h_sm):
    def copies(ins, outs, send, recv, base=0):
        x, y, c = _mesh_pos()
        cps = []
        for src, dst in zip(ins, outs):
            for k, chip in enumerate(_other_chips(x, y)):
                their = 2 * chip[0] + chip[1]
                part = src.at[:, pl.ds(pl.multiple_of(their * SH_IN, 128), SH_IN)] if len(src.shape) == 2 else src.at[:, their]
                cps.append(_rcopy(part, dst.at[k], send, recv, base + len(cps), (*chip, c)))
        return cps

    arrays = [h for h in (h_in, h_sm) if h is not None]
    shapes = [SDS((3, 512, SH_IN), BF16) if len(h.shape) == 2 else SDS((3, 3, 128, D), BF16) for h in arrays]
    return _Rider(arrays, shapes, 3 * len(arrays), copies)


def _ride_alone(rider, name):
    return _hosted_call(None, rider, name=name, grid=(), in_specs=[], out_specs=[], out_shape=[], args=())[1]


def _final_sum_in(f_in, r_in):
    def body(a_ref, r_ref, o_ref):
        o_ref[...] = a_ref[...] + r_ref[0].astype(F32) + r_ref[1].astype(F32) + r_ref[2].astype(F32)

    return pl.pallas_call(
        body, name="final_sum_in", grid=(4,),
        in_specs=[pl.BlockSpec((128, SH_IN), lambda i: (i, 0)), pl.BlockSpec((3, 128, SH_IN), lambda i: (0, i, 0))],
        out_specs=pl.BlockSpec((128, SH_IN), lambda i: (i, 0)),
        out_shape=SDS((512, SH_IN), F32), compiler_params=_cp("parallel"),
    )(f_in, r_in)


def _final_sum_sm(chip_idx, f_sm, r_sm):
    def body(j_ref, a_ref, r_ref, o_ref):
        o_ref[...] = a_ref[...] + r_ref[0].astype(F32) + r_ref[1].astype(F32) + r_ref[2].astype(F32)

    return pl.pallas_call(
        body, name="final_sum_sm",
        grid_spec=pltpu.PrefetchScalarGridSpec(
            num_scalar_prefetch=1, grid=(3,),
            in_specs=[pl.BlockSpec((1, None, 128, D), lambda a, j: (a, j[0], 0, 0)),
                      pl.BlockSpec((3, 1, 128, D), lambda a, j: (0, a, 0, 0))],
            out_specs=pl.BlockSpec((1, 128, D), lambda a, j: (a, 0, 0))),
        out_shape=SDS((3, 128, D), F32), compiler_params=_cp("parallel"),
    )(chip_idx, f_sm, r_sm)


def _join_halves(t_in, t_sm):
    n_cp = N_LAYERS * 4
    args, plan = [], []
    for l in range(N_LAYERS):
        if t_in[l] is not None:
            plan.append((l, 0, len(args)))
            args.append(t_in[l])
        plan += [(l, a, len(args)) for a in (1, 2, 3)]
        args.append(t_sm[l])

    def body(*refs):
        ins, outs = refs[:len(args)], refs[len(args):len(args) + 4]
        send, recv, loc_in, loc_out, stage_in, stage_sm = refs[len(args) + 4:]
        x, y, c = _mesh_pos()
        cps, own = [], []

        def place(l, a, half):
            rows = 512 if a == 0 else 128
            return outs[a].at[l, pl.ds(pl.multiple_of(half * rows, rows), rows), :]

        for s, (l, a, k) in enumerate(plan):
            src = ins[k] if a == 0 else ins[k].at[a - 1]
            own.append((src, place(l, a, c), min(a, 1)))
            cp = pltpu.make_async_remote_copy(src_ref=src, dst_ref=place(l, a, c), send_sem=send.at[s],
                                              recv_sem=recv.at[s], device_id=(x, y, 1 - c), device_id_type=MESH)
            cp.start()
            cps.append(cp)
        _staged_copies(own, (stage_in, stage_sm), loc_in, loc_out)
        for s, (l, a, k) in enumerate(plan):
            got = place(l, a, 1 - c)
            pltpu.make_async_remote_copy(src_ref=got, dst_ref=got, send_sem=send.at[s], recv_sem=recv.at[s],
                                         device_id=(x, y, 1 - c), device_id_type=MESH).wait_recv()
        for cp in cps:
            cp.wait_send()

    sm = SDS((N_LAYERS, SH_ROW, D), F32)
    return pl.pallas_call(
        body, name="join_halves", in_specs=[ANY] * len(args), out_specs=[ANY] * 4,
        out_shape=[SDS((N_LAYERS, D, SH_IN), F32), sm, sm, sm],
        scratch_shapes=[pltpu.SemaphoreType.DMA((n_cp,))] * 4
        + [pltpu.VMEM((2, 512, SH_IN), F32), pltpu.VMEM((2, 128, D), F32)],
        compiler_params=_cp(vmem=VMEM_BIG),
    )(*args)


def _adam_math(w, g, m, v):
    m = ADAM_B1 * m + (1.0 - ADAM_B1) * g
    v = ADAM_B2 * v + (1.0 - ADAM_B2) * (g * g)
    m_hat = m / (1.0 - ADAM_B1 ** ADAM_STEP)
    v_hat = v / (1.0 - ADAM_B2 ** ADAM_STEP)
    delta = -ADAM_LR * (m_hat / (jnp.sqrt(v_hat) + ADAM_EPS) + ADAM_WD * w)
    return delta, m, v


def _adamw_big(w, g, m, v, name):
    rows, cols = w.shape
    tr = 256

    def body(w_ref, g_ref, m_ref, v_ref, go_ref, d_ref, nm_ref, nv_ref):
        g = g_ref[...]
        go_ref[...] = g
        d_ref[...], nm_ref[...], nv_ref[...] = _adam_math(w_ref[...], g, m_ref[...], v_ref[...])

    blk = pl.BlockSpec((tr, cols), lambda i: (i, 0))
    return pl.pallas_call(
        body, name=name, grid=(rows // tr,), in_specs=[blk] * 4, out_specs=[blk] * 4,
        out_shape=[SDS((rows, cols), F32)] * 4, compiler_params=_cp("parallel", vmem=VMEM_BIG),
    )(w, g, m, v)


def _adamw_small(ws, gs, ms, vs):
    n = len(ws)

    def body(*refs):
        for k in range(n):
            w_ref, g_ref, m_ref, v_ref = (refs[q * n + k] for q in range(4))
            d, nm, nv = _adam_math(w_ref[...], g_ref[...], m_ref[...], v_ref[...])
            refs[4 * n + k][...] = d
            refs[5 * n + k][...] = nm
            refs[6 * n + k][...] = nv

    vm = pl.BlockSpec(memory_space=pltpu.VMEM)
    shapes = [SDS(w.shape, F32) for w in ws]
    res = pl.pallas_call(
        body, name="adamw_small", in_specs=[vm] * (4 * n), out_specs=[vm] * (3 * n), out_shape=shapes * 3,
    )(*ws, *gs, *ms, *vs)
    return res[:n], res[n:2 * n], res[2 * n:]


def _pad_rows(a, rows):
    flat = a.reshape(-1)
    return jnp.pad(flat, (0, rows * 128 - flat.shape[0])).reshape(rows, 128)


def kernel(x, norm_g, w_in, conv_w, q_norm_g, k_norm_g, sinks, w_conv_out, w_attn_out, gate_b, w_out, loss_target, m_norm_g, m_w_in, m_conv_w, m_q_norm_g, m_k_norm_g, m_sinks, m_w_conv_out, m_w_attn_out, m_gate_b, m_w_out, v_norm_g, v_w_in, v_conv_w, v_q_norm_g, v_k_norm_g, v_sinks, v_w_conv_out, v_w_attn_out, v_gate_b, v_w_out):
    xi, yi, ci = _mesh_pos()
    chip = 2 * xi + yi
    c_idx = jnp.reshape(ci, (1,)).astype(jnp.int32)
    chip_idx = jnp.reshape(chip, (1,)).astype(jnp.int32)
    cc_idx = jnp.stack([ci, chip]).astype(jnp.int32)
    t = x.shape[1]
    xs = [x.reshape(t, D)]
    tgt = loss_target.reshape(t, D)

    full_w = [[_cast_w_in(chip_idx, w_in, l), _cast_w_small(chip_idx, w_conv_out, w_attn_out, w_out, l)]
              for l in range(N_LAYERS)]
    conv32 = lax.dynamic_update_slice(jnp.zeros((32, D), F32), jnp.pad(conv_w.reshape(3 * N_LAYERS, SH_ROW), ((0, 20), (0, 0))),
                                      (0, chip * SH_ROW))
    qg_s = jnp.tile(q_norm_g, (1, N_Q)) * (SCALE * LOG2E)
    kg_t = jnp.tile(k_norm_g, (1, N_KV))
    bias = _band_bias()
    saved = []
    for l in range(N_LAYERS):
        nxt = full_w[l + 1] if l + 1 < N_LAYERS else None
        (h, ht), got = _rmsnorm_fwd(xs[l], norm_g[l:l + 1], _gather_rider([full_w[0][0], conv32], "N") if l == 0 else None)
        if l == 0:
            got = _ride_alone(_gather_rider(got, "F"), "gather_first_forward")
            full_w[0][0], conv32 = _ride_alone(_gather_rider(got, "B"), "gather_first_d2d")
            conv_full = conv32[:3 * N_LAYERS].reshape(N_LAYERS, 3, D)
        (u_conv, u_qkv, u_za, u_gl), got = _in_proj(h, full_w[l][0], _gather_rider(nxt, "N") if nxt else None)
        if nxt:
            nxt[0], nxt[1] = got
        (qs, kvx), got = _qkv_prep(u_qkv, qg_s[l:l + 1], kg_t[l:l + 1],
                                   _gather_rider(full_w[0][1:], "N") if l == 0 else None)
        y_c, got = _conv_fwd(u_conv, conv_full[l], _gather_rider(got, "F") if l == 0 else None)
        o, lse, got = _attn_fwd(qs, kvx, u_za, sinks[l:l + 1], bias, _merge_riders(
            _gather_rider(nxt, "FB1") if nxt else None, _gather_rider(got, "B") if l == 0 else None))
        if nxt:
            nxt[0], nxt[1] = got[:2]
        if l == 0:
            full_w[0][1] = got[-1]
        (x_next, y_a, y_b, merged), got = _out_proj_fwd(xs[l], y_c, o, u_gl, gate_b[l:l + 1], full_w[l][1],
                                                        _gather_rider(nxt, "B2") if nxt else None)
        if nxt:
            nxt[0], nxt[1] = got
        xs.append(x_next)
        saved.append((ht, u_conv, u_qkv, u_za, u_gl, y_c, o, y_a, y_b, merged, qs, kvx, lse))

    dout, sq = _loss_head(xs[N_LAYERS], tgt)

    small, t_in, t_sm = [None] * N_LAYERS, [None] * N_LAYERS, [None] * N_LAYERS
    halves = None

    for l in reversed(range(N_LAYERS)):
        w_full, w_sm = full_w[l]
        last = l == 0
        ht, u_conv, u_qkv, u_za, u_gl, y_c, o, y_a, y_b, merged, qs, kvx, lse = saved[l]
        (d_ya, d_yb, du_gl, d_yc, d_o, dgb), _ = _out_proj_bwd(dout, y_a, y_b, u_gl, gate_b[l:l + 1], w_sm, None)
        g_sm = _small_wgrads(y_c, d_ya, o, d_yb, merged, dout).reshape(3, 4, 2, 128, D)
        (du_conv, dcw), got = _conv_bwd(d_yc, u_conv, conv_full[l], _swap_rider(None, g_sm) if last else None)
        if last:
            f_sm0, h_sm0 = _add_halves_sm(c_idx, g_sm, got[0])
        (dqs, dkv, dza, dsk), got = _attn_bwd(d_o, qs, kvx, u_za, lse, sinks[l:l + 1], bias,
                                              _scatter_rider(halves[1], None) if halves else None)
        if halves:
            t_in[l + 1] = _final_sum_in(halves[0], got[0])
        dsk = jnp.sum(dsk[0].reshape(N_KV, 2, 2, BLK), axis=-1).transpose(0, 2, 1).reshape(N_Q)
        (du_attn, dqg, dkg), _ = _qkv_post(u_qkv, dqs, dkv, dza, qg_s[l:l + 1], kg_t[l:l + 1], None)
        du = (du_conv, du_attn, du_gl)
        g_in, got = _in_proj_wgrad(ht, du, _merge_riders(
            _scatter_rider(None, halves[3]) if halves else None, _scatter_rider(None, h_sm0) if last else None))
        g_in = g_in.reshape(2, 512, N_IN)
        if halves:
            t_sm[l + 1] = _final_sum_sm(chip_idx, halves[2], got[0])
        if last:
            t_sm[0] = _final_sum_sm(chip_idx, f_sm0, got[-1])
        if last:
            f_in0, h_in0 = _add_halves_in(cc_idx, g_in, _ride_alone(_swap_rider(g_in, None), "swap_last")[0])
        (dout, dng), got = _in_proj_bwd(du, w_full, xs[l], norm_g[l:l + 1], dout,
                                        _scatter_rider(h_in0, None) if last else _swap_rider(g_in, g_sm))
        if last:
            t_in[0] = _final_sum_in(f_in0, got[0])
        else:
            halves = _add_halves_in(cc_idx, g_in, got[0]) + _add_halves_sm(c_idx, g_sm, got[1])
        small[l] = (jnp.sum(dng, axis=0), (SCALE * LOG2E) * jnp.sum(dqg.reshape(8 * N_Q, HEAD), axis=0),
                    jnp.sum(dkg.reshape(8 * N_KV, HEAD), axis=0), dsk, jnp.sum(dgb, axis=0), dcw[:3])
    grad_x = dout.reshape(1, t, D)

    stack = lambda k: jnp.stack([small[l][k] for l in range(N_LAYERS)])
    pack = jnp.concatenate([_pad_rows(stack(0), 32), _pad_rows(stack(1), 8), _pad_rows(stack(2), 8),
                            _pad_rows(stack(3), 8), _pad_rows(stack(4), 64), _pad_rows(stack(5), 96),
                            _pad_rows(jnp.sum(sq) * (0.5 / D), 8)], axis=0)
    red = _allreduce_small(pack)
    loss = red[216, 0]
    g_norm_g = red[0:32].reshape(N_LAYERS, D)
    g_q_norm_g = red[32:40].reshape(-1)[:N_LAYERS * HEAD].reshape(N_LAYERS, HEAD)
    g_k_norm_g = red[40:48].reshape(-1)[:N_LAYERS * HEAD].reshape(N_LAYERS, HEAD)
    g_sinks = red[48:56].reshape(-1)[:N_LAYERS * N_Q].reshape(N_LAYERS, N_Q)
    g_gate_b = red[56:120].reshape(N_LAYERS, 2 * D)
    g_conv_full = red[120:216].reshape(N_LAYERS, 3, D)
    g_conv_w = lax.dynamic_slice(g_conv_full, (0, 0, chip * SH_ROW), (N_LAYERS, 3, SH_ROW))

    g_w_in, g_w_co, g_w_ao, g_w_out = _join_halves(t_in, t_sm)

    r_in = N_LAYERS * D
    g_w_in, d_in, nm_in, nv_in = (a.reshape(N_LAYERS, D, SH_IN) for a in _adamw_big(
        w_in.reshape(r_in, SH_IN), g_w_in.reshape(r_in, SH_IN), m_w_in.reshape(r_in, SH_IN),
        v_w_in.reshape(r_in, SH_IN), "adamw_w_in"))
    r_sm = N_LAYERS * SH_ROW
    big = {}
    for nm, w, g, m, v in (("co", w_conv_out, g_w_co, m_w_conv_out, v_w_conv_out),
                           ("ao", w_attn_out, g_w_ao, m_w_attn_out, v_w_attn_out),
                           ("out", w_out, g_w_out, m_w_out, v_w_out)):
        big[nm] = tuple(a.reshape(N_LAYERS, SH_ROW, D) for a in _adamw_big(
            w.reshape(r_sm, D), g.reshape(r_sm, D), m.reshape(r_sm, D), v.reshape(r_sm, D), "adamw_w_small"))
    g_w_co, g_w_ao, g_w_out = big["co"][0], big["ao"][0], big["out"][0]
    sm_w = [norm_g, conv_w, q_norm_g, k_norm_g, sinks, gate_b]
    sm_g = [g_norm_g, g_conv_w, g_q_norm_g, g_k_norm_g, g_sinks, g_gate_b]
    sm_m = [m_norm_g, m_conv_w, m_q_norm_g, m_k_norm_g, m_sinks, m_gate_b]
    sm_v = [v_norm_g, v_conv_w, v_q_norm_g, v_k_norm_g, v_sinks, v_gate_b]
    sd, snm, snv = _adamw_small(sm_w, sm_g, sm_m, sm_v)

    def order(norm, w_in_, conv, qn, kn, sk, co, ao, gb, wo):
        return [norm, w_in_, conv, qn, kn, sk, co, ao, gb, wo]

    grads = order(g_norm_g, g_w_in, g_conv_w, g_q_norm_g, g_k_norm_g, g_sinks, g_w_co, g_w_ao, g_gate_b, g_w_out)
    deltas = order(sd[0], d_in, sd[1], sd[2], sd[3], sd[4], big["co"][1], big["ao"][1], sd[5], big["out"][1])
    new_m = order(snm[0], nm_in, snm[1], snm[2], snm[3], snm[4], big["co"][2], big["ao"][2], snm[5], big["out"][2])
    new_v = order(snv[0], nv_in, snv[1], snv[2], snv[3], snv[4], big["co"][3], big["ao"][3], snv[5], big["out"][3])
    return (loss, grad_x, *grads, *deltas, *new_m, *new_v)
```

```python
import functools

import jax
import jax.numpy as jnp
from jax import lax
from jax.experimental import pallas as pl
from jax.experimental.pallas import tpu as pltpu

F32, BF16 = jnp.float32, jnp.bfloat16
SDS = jax.ShapeDtypeStruct
MESH = pl.DeviceIdType.MESH
ANY = pl.BlockSpec(memory_space=pl.ANY)

D = 1024
N_IN = 8704
N_LAYERS = 4
N_Q, N_KV, HEAD = 16, 4, 64
GROUP = N_Q // N_KV
BLK = 128
EPS = 1e-6
NEG = -1e30
SCALE = HEAD ** -0.5
SH_IN = N_IN // 4
SH_ROW = D // 4
CB = 512
VMEM_BIG = 56 * 1024 * 1024

ADAM_LR, ADAM_B1, ADAM_B2, ADAM_EPS, ADAM_WD, ADAM_STEP = 0.001, 0.9, 0.999, 1e-08, 0.01, 10


def _cp(*sem, vmem=None):
    return pltpu.CompilerParams(dimension_semantics=sem if sem else None, vmem_limit_bytes=vmem)


def _sigmoid(z):
    return 1.0 / (1.0 + jnp.exp(-z))


def _dot(a, b):
    return jnp.dot(a, b, preferred_element_type=F32)


def _dot_nt(a, b):
    return lax.dot_general(a, b, (((1,), (1,)), ((), ())), preferred_element_type=F32)


def _dot_tn(a, b):
    return lax.dot_general(a, b, (((0,), (0,)), ((), ())), preferred_element_type=F32)


def _fold8(v):
    return jnp.sum(v.reshape(v.shape[0] // 8, 8, v.shape[1]), axis=0)


def _cast_w_in(chip_idx, w, layer):
    def body(j_ref, i_ref, o_ref):
        o_ref[...] = i_ref[...].astype(BF16)

    return pl.pallas_call(
        body, name="cast_w_in",
        grid_spec=pltpu.PrefetchScalarGridSpec(
            num_scalar_prefetch=1, grid=(2,),
            in_specs=[pl.BlockSpec((None, 512, SH_IN), lambda i, j: (layer, i, 0))],
            out_specs=pl.BlockSpec((512, SH_IN), lambda i, j: (i, j[0]))),
        out_shape=SDS((D, N_IN), BF16), compiler_params=_cp("parallel"),
    )(chip_idx, w)


def _cast_w_small(chip_idx, a, b, c, layer):
    def body(j_ref, a_ref, b_ref, c_ref, o_ref):
        o_ref[0] = a_ref[...].astype(BF16)
        o_ref[1] = b_ref[...].astype(BF16)
        o_ref[2] = c_ref[...].astype(BF16)

    spec = pl.BlockSpec((None, SH_ROW, D), lambda i, j: (layer, 0, 0))
    return pl.pallas_call(
        body, name="cast_w_small",
        grid_spec=pltpu.PrefetchScalarGridSpec(
            num_scalar_prefetch=1, grid=(1,), in_specs=[spec, spec, spec],
            out_specs=pl.BlockSpec((3, SH_ROW, D), lambda i, j: (0, j[0], 0))),
        out_shape=SDS((3, D, D), BF16), compiler_params=_cp("parallel"),
    )(chip_idx, a, b, c)


def _mesh_pos():
    return lax.axis_index("x"), lax.axis_index("y"), lax.axis_index("c")


def _other_chips(x, y):
    return [(1 - x, y), (x, 1 - y), (1 - x, 1 - y)]


class _Rider:
    def __init__(self, ins, out_shape, n, copies, aliases=()):
        self.ins, self.out_shape, self.n, self.copies, self.aliases = list(ins), list(out_shape), n, copies, aliases


def _merge_riders(*riders):
    riders = [r for r in riders if r is not None]
    if len(riders) < 2:
        return riders[0] if riders else None

    def copies(ins, outs, send, recv, base=0):
        cps, i0, o0 = [], 0, 0
        for r in riders:
            cps += r.copies(ins[i0:i0 + len(r.ins)], outs[o0:o0 + len(r.out_shape)], send, recv, base + len(cps))
            i0, o0 = i0 + len(r.ins), o0 + len(r.out_shape)
        return cps

    aliases, i0, o0 = [], 0, 0
    for r in riders:
        aliases += [(i0 + i, o0 + o) for i, o in r.aliases]
        i0, o0 = i0 + len(r.ins), o0 + len(r.out_shape)
    return _Rider(sum((r.ins for r in riders), []), sum((r.out_shape for r in riders), []),
                  sum(r.n for r in riders), copies, tuple(aliases))


def _rcopy(src, dst, send, recv, k, to):
    return pltpu.make_async_remote_copy(src_ref=src, dst_ref=dst, send_sem=send.at[k], recv_sem=recv.at[k],
                                        device_id=to, device_id_type=MESH)


def _hosted_call(body, rider, *, name, grid, in_specs, out_specs, out_shape, args, scratch_shapes=(), vmem=None):
    n_in, n_out, n_scr = len(in_specs), len(out_specs), len(scratch_shapes)
    r_in, r_out = len(rider.ins), len(rider.out_shape)

    def full_body(*refs):
        host_in, rid_in = refs[:n_in], refs[n_in:n_in + r_in]
        o0 = n_in + r_in
        host_out, rid_out = refs[o0:o0 + n_out], refs[o0 + n_out:o0 + n_out + r_out]
        s0 = o0 + n_out + r_out
        host_scr, (send, recv) = refs[s0:s0 + n_scr], refs[s0 + n_scr:]
        if body is None:
            cps = rider.copies(rid_in, rid_out, send, recv)
            for cp in cps:
                cp.start()
            for cp in cps:
                cp.wait()
            return
        ids = [pl.program_id(a) for a in range(len(grid))]
        first = functools.reduce(lambda p, q: p & q, [i == 0 for i in ids])
        last = functools.reduce(lambda p, q: p & q, [i == g - 1 for i, g in zip(ids, grid)])

        @pl.when(first)
        def _():
            for cp in rider.copies(rid_in, rid_out, send, recv):
                cp.start()

        body(*host_in, *host_out, *host_scr)

        @pl.when(last)
        def _():
            for cp in rider.copies(rid_in, rid_out, send, recv):
                cp.wait()

    res = pl.pallas_call(
        full_body, name=name, grid=grid if body is not None else (),
        in_specs=list(in_specs) + [ANY] * r_in, out_specs=list(out_specs) + [ANY] * r_out,
        out_shape=list(out_shape) + rider.out_shape,
        scratch_shapes=list(scratch_shapes) + [pltpu.SemaphoreType.DMA((rider.n,))] * 2,
        input_output_aliases={n_in + i: n_out + o for i, o in rider.aliases},
        compiler_params=_cp(*(("arbitrary",) * len(grid) if body is not None else ()), vmem=vmem),
    )(*args, *rider.ins)
    return res[:n_out], res[n_out:]


def _call(body, rider, **kw):
    if rider is not None:
        return _hosted_call(body, rider, **kw)
    res = pl.pallas_call(
        body, name=kw["name"], grid=kw["grid"], in_specs=list(kw["in_specs"]), out_specs=list(kw["out_specs"]),
        out_shape=list(kw["out_shape"]), scratch_shapes=list(kw.get("scratch_shapes", ())),
        compiler_params=_cp(*(("arbitrary",) * len(kw["grid"])), vmem=kw.get("vmem")),
    )(*kw["args"])
    return res, []


def _gather_rider(arrays, stage):
    def region(full, whose, c, sub):
        if len(full.shape) == 2:
            rows, cols = full.shape[0] // 2, full.shape[1] // 4
            first, n = (c * rows, rows) if sub is None else (c * rows + sub * (rows // 2), rows // 2)
            return full.at[pl.ds(pl.multiple_of(first, n), n), pl.ds(pl.multiple_of(whose * cols, 128), cols)]
        first, n = (whose * SH_ROW + c * 128, 128) if sub is None else (whose * SH_ROW + c * 128 + sub * 64, 64)
        return full.at[:, pl.ds(pl.multiple_of(first, n), n), :]

    def copies(ins, outs, send, recv, base=0):
        x, y, c = _mesh_pos()
        nbr_x, nbr_y = (1 - x, y), (x, 1 - y)
        cps = []
        for full in outs:
            plan = []
            if stage == "N":
                plan = [(region(full, 2 * x + y, c, None), (*nbr_x, c)), (region(full, 2 * x + y, c, None), (*nbr_y, c))]
            if stage in ("F", "FB1"):
                plan = [(region(full, 2 * nbr_x[0] + nbr_x[1], c, 0), (*nbr_y, c)),
                        (region(full, 2 * nbr_y[0] + nbr_y[1], c, 1), (*nbr_x, c))]
            if stage in ("B", "FB1", "B2"):
                chips = {"B": _other_chips(x, y), "FB1": [nbr_x, nbr_y], "B2": [(1 - x, 1 - y)]}[stage]
                plan += [(region(full, 2 * chip[0] + chip[1], c, None), (x, y, 1 - c)) for chip in chips]
            for reg, to in plan:
                cps.append(_rcopy(reg, reg, send, recv, base + len(cps), to))
        return cps

    per_array = {"N": 2, "F": 2, "B": 3, "FB1": 4, "B2": 1}[stage]
    return _Rider(arrays, [SDS(v.shape, v.dtype) for v in arrays], per_array * len(arrays), copies,
                  aliases=tuple((i, i) for i in range(len(arrays))))


def _staged_copies(copies, stages, sem_in, sem_out):
    busy, count = {}, {}
    for idx, (src, dst, kind) in enumerate(copies):
        slot = count.get(kind, 0) % 2
        count[kind] = count.get(kind, 0) + 1
        if (kind, slot) in busy:
            busy.pop((kind, slot)).wait()
        buf = stages[kind].at[slot]
        cin = pltpu.make_async_copy(src, buf, sem_in.at[idx])
        cin.start()
        cin.wait()
        cout = pltpu.make_async_copy(buf, dst, sem_out.at[idx])
        cout.start()
        busy[(kind, slot)] = cout
    for cp in busy.values():
        cp.wait()


def _allreduce_small(pack):
    rows = pack.shape[0]

    def body(p_ref, o_ref, buf, send, recv):
        x, y, c = _mesh_pos()
        me = 4 * x + 2 * y + c
        sends = []
        for r in range(1, 8):
            to = (x if not (r & 4) else 1 - x, y if not (r & 2) else 1 - y, c if not (r & 1) else 1 - c)
            cp = pltpu.make_async_remote_copy(src_ref=p_ref, dst_ref=buf.at[me], send_sem=send.at[r - 1],
                                              recv_sem=recv.at[r - 1], device_id=to, device_id_type=MESH)
            cp.start()
            sends.append(cp)
        buf[me] = p_ref[...]
        for r in range(1, 8):
            frm = (4 * x + 2 * y + c) ^ r
            pltpu.make_async_remote_copy(src_ref=p_ref, dst_ref=buf.at[frm], send_sem=send.at[r - 1],
                                         recv_sem=recv.at[r - 1], device_id=(x, y, c), device_id_type=MESH).wait_recv()
        acc = buf[0]
        for d in range(1, 8):
            acc = acc + buf[d]
        o_ref[...] = acc
        for cp in sends:
            cp.wait_send()

    vm = pl.BlockSpec(memory_space=pltpu.VMEM)
    return pl.pallas_call(
        body, name="allreduce_small", in_specs=[vm], out_specs=vm, out_shape=SDS((rows, 128), F32),
        scratch_shapes=[pltpu.VMEM((8, rows, 128), F32), pltpu.SemaphoreType.DMA((7,)), pltpu.SemaphoreType.DMA((7,))],
    )(pack)


def _rmsnorm_fwd(x, g, rider):
    t = x.shape[0]
    tm = min(1024, t)

    def body(x_ref, g_ref, h_ref, ht_ref):
        xv = x_ref[...]
        r = lax.rsqrt(jnp.mean(xv * xv, axis=-1, keepdims=True) + EPS)
        h = xv * r * g_ref[...]
        h_ref[...] = h.astype(BF16)
        ht_ref[...] = h.T.astype(BF16)

    return _call(
        body, rider, name="rmsnorm_fwd", grid=(t // tm,),
        in_specs=[pl.BlockSpec((tm, D), lambda i: (i, 0)), pl.BlockSpec((1, D), lambda i: (0, 0))],
        out_specs=[pl.BlockSpec((tm, D), lambda i: (i, 0)), pl.BlockSpec((D, tm), lambda i: (0, i))],
        out_shape=[SDS((t, D), BF16), SDS((D, t), BF16)], args=(x, g), vmem=VMEM_BIG)


FWD_SEGS = ((0, 8), (8, 3), (11, 2), (13, 4))


def _in_proj(h, w_full, rider):
    t = h.shape[0]
    tm = min(2048, t)

    def body(a_ref, b_ref, *o_refs):
        j = pl.program_id(1)
        for o_ref, (off, nblk) in zip(o_refs, FWD_SEGS):
            @pl.when((j >= off) & (j < off + nblk))
            def _():
                o_ref[...] = _dot(a_ref[...], b_ref[...]).astype(BF16)

    def out(seg):
        off, nblk = seg
        return pl.BlockSpec((tm, CB), lambda i, j: (i, jnp.clip(j - off, 0, nblk - 1)))

    res, got = _call(
        body, rider, name="in_proj", grid=(t // tm, N_IN // CB),
        in_specs=[pl.BlockSpec((tm, D), lambda i, j: (i, 0)), pl.BlockSpec((D, CB), lambda i, j: (0, j))],
        out_specs=[out(s) for s in FWD_SEGS], out_shape=[SDS((t, s[1] * CB), BF16) for s in FWD_SEGS],
        args=(h, w_full), vmem=VMEM_BIG)
    return res, got


def _conv_fwd(u_conv, conv_w, rider):
    t = u_conv.shape[0]
    tm = min(512, t)
    hb = tm // 16

    def body(v_ref, b_ref, c_ref, z_ref, hv_ref, hc_ref, w_ref, y_ref):
        i = pl.program_id(0)
        cv = c_ref[...].astype(F32) * v_ref[...].astype(F32)
        halo = hc_ref[...].astype(F32) * hv_ref[...].astype(F32)
        halo = jnp.where(i > 0, halo, 0.0)
        row = lax.broadcasted_iota(jnp.int32, (tm, 1), 0)
        s1 = jnp.where(row == 0, halo[15:16], pltpu.roll(cv, 1, 0))
        s2 = jnp.where(row == 0, halo[14:15], jnp.where(row == 1, halo[15:16], pltpu.roll(cv, 2, 0)))
        conv = w_ref[0:1, :] * s2 + w_ref[1:2, :] * s1 + w_ref[2:3, :] * cv
        z = z_ref[...].astype(F32)
        y_ref[...] = (b_ref[...].astype(F32) * conv * (z * _sigmoid(z))).astype(BF16)

    def col(k):
        return pl.BlockSpec((tm, D), lambda i: (i, k))

    def halo(k):
        return pl.BlockSpec((16, D), lambda i: (jnp.maximum(i * hb - 1, 0), k))

    (y_c,), got = _call(
        body, rider, name="conv_fwd", grid=(t // tm,),
        in_specs=[col(0), col(1), col(2), col(3), halo(0), halo(2), pl.BlockSpec((3, D), lambda i: (0, 0))],
        out_specs=[pl.BlockSpec((tm, D), lambda i: (i, 0))], out_shape=[SDS((t, D), BF16)],
        args=(u_conv, u_conv, u_conv, u_conv, u_conv, u_conv, conv_w), vmem=VMEM_BIG)
    return y_c, got


KVX = 4 * N_KV * 128


def _iota2(shape):
    return lax.broadcasted_iota(jnp.int32, shape, 0), lax.broadcasted_iota(jnp.int32, shape, 1)


def _head_sum(v):
    r, c = _iota2((128, 128))
    ones = ((r >> 6) == (c >> 6)).astype(BF16)
    hi = v.astype(BF16)
    lo = (v - hi.astype(F32)).astype(BF16)
    return jnp.concatenate([_dot(hi[:, g:g + 128], ones) + _dot(lo[:, g:g + 128], ones)
                            for g in range(0, v.shape[1], 128)], axis=1)


def _expand_mats():
    r, c = _iota2((N_KV * HEAD, N_KV * 128))
    base = ((r >> 6) << 7) + (r & 63)
    return (c == base).astype(BF16), (c == base + 64).astype(BF16)


def _fold_mat():
    r, c = _iota2((N_KV * 128, N_KV * HEAD))
    return (((r >> 7) == (c >> 6)) & ((r & 63) == (c & 63))).astype(BF16)


def _qkv_prep(u_qkv, qg_s, kg_t, rider):
    t = u_qkv.shape[0]
    tm = min(1024, t)

    def body(u_ref, qg_ref, kg_ref, qs_ref, kvx_ref):
        q = u_ref[:, 0:D].astype(F32)
        rq = lax.rsqrt(_head_sum(q * q) * (1.0 / HEAD) + EPS)
        qs_ref[...] = (q * rq * qg_ref[...]).astype(BF16)
        k = u_ref[:, D:D + 256].astype(F32)
        rk = lax.rsqrt(_head_sum(k * k) * (1.0 / HEAD) + EPS)
        kn = (k * rk * kg_ref[...]).astype(BF16)
        v = u_ref[:, D + 256:D + 512]
        e_lo, e_hi = _expand_mats()
        kvx_ref[:, 0:512] = _dot(kn, e_lo).astype(BF16)
        kvx_ref[:, 512:1024] = _dot(kn, e_hi).astype(BF16)
        kvx_ref[:, 1024:1536] = _dot(v, e_lo).astype(BF16)
        kvx_ref[:, 1536:2048] = _dot(v, e_hi).astype(BF16)

    return _call(
        body, rider, name="qkv_prep", grid=(t // tm,),
        in_specs=[pl.BlockSpec((tm, 1536), lambda i: (i, 0)), pl.BlockSpec((1, D), lambda i: (0, 0)),
                  pl.BlockSpec((1, 256), lambda i: (0, 0))],
        out_specs=[pl.BlockSpec((tm, D), lambda i: (i, 0)), pl.BlockSpec((tm, KVX), lambda i: (i, 0))],
        out_shape=[SDS((t, D), BF16), SDS((t, KVX), BF16)], args=(u_qkv, qg_s, kg_t), vmem=VMEM_BIG)


def _band_bias():
    j, r = _iota2((2 * BLK, 2 * BLK))
    diff = (r & (BLK - 1)) - j + BLK
    band = (diff >= 0) & (diff < BLK)
    return jnp.stack([jnp.where(band & (j >= BLK), 0.0, NEG), jnp.where(band, 0.0, NEG)]).astype(F32)


def _pair_rows(ref_or_val, hk):
    return jnp.concatenate([ref_or_val[:, 256 * hk:256 * hk + 128], ref_or_val[:, 256 * hk + 128:256 * hk + 256]], axis=0)


LOG2E, LN2 = 1.4426950408889634, 0.6931471805599453


def _sink_row(sink_ref, hk, half):
    return jnp.concatenate([jnp.full((1, BLK), sink_ref[0, GROUP * hk + half] * LOG2E, F32),
                            jnp.full((1, BLK), sink_ref[0, GROUP * hk + 2 + half] * LOG2E, F32)], axis=1)


def _kv_operands(kvb, hk, half):
    return (kvb[:, 512 * half + 128 * hk:512 * half + 128 * hk + 128],
            kvb[:, 1024 + 512 * half + 128 * hk:1024 + 512 * half + 128 * hk + 128])


def _attn_fwd(qs, kvx, u_za, sinks, bias, rider):
    t = qs.shape[0]
    nb = t // BLK

    def body(q_ref, kc_ref, kp_ref, za_ref, sink_ref, bias_ref, o_ref, lse_ref):
        kvb = jnp.concatenate([kp_ref[...], kc_ref[...]], axis=0)
        bias_v = bias_ref[...]
        key0 = lax.broadcasted_iota(jnp.int32, (2 * BLK, 1), 0) == 0
        ones = jnp.ones((2 * BLK, 128), BF16)
        cols = []
        for hk in range(N_KV):
            qpp = _pair_rows(q_ref, hk)
            opp = None
            for half in range(2):
                kx, vx = _kv_operands(kvb, hk, half)
                s = _dot_nt(kx, qpp) + bias_v
                sink = _sink_row(sink_ref, hk, half)
                m = jnp.maximum(jnp.max(s, axis=0, keepdims=True), sink)
                p = jnp.exp2(s - m)
                es = jnp.exp2(sink - m)
                lse_ref[0, 2 * hk + half:2 * hk + half + 1, :] = m + jnp.log(jnp.sum(p, axis=0, keepdims=True) + es) * LOG2E
                pe = jnp.where(key0, es, p).astype(BF16)
                rhs = jnp.concatenate([jnp.where(key0, jnp.zeros_like(vx), vx), ones], axis=1)
                nd = _dot_tn(pe, rhs)
                o = nd[:, :128] * (1.0 / nd[:, 128:])
                opp = o if opp is None else opp + o
            cols += [opp[:BLK], opp[BLK:]]
        za = za_ref[...].astype(F32)
        o_ref[...] = (jnp.concatenate(cols, axis=1) * (za * _sigmoid(za))).astype(BF16)

    prev = lambda n: jnp.maximum(n - 1, 0)
    (o, lse), got = _call(
        body, rider, name="attn_fwd", grid=(nb,),
        in_specs=[pl.BlockSpec((BLK, D), lambda n: (n, 0)),
                  pl.BlockSpec((BLK, KVX), lambda n: (n, 0)), pl.BlockSpec((BLK, KVX), lambda n: (prev(n), 0)),
                  pl.BlockSpec((BLK, D), lambda n: (n, 0)), pl.BlockSpec(memory_space=pltpu.SMEM),
                  pl.BlockSpec((None, 2 * BLK, 2 * BLK), lambda n: (jnp.minimum(n, 1), 0, 0))],
        out_specs=[pl.BlockSpec((BLK, D), lambda n: (n, 0)), pl.BlockSpec((1, 8, 2 * BLK), lambda n: (n, 0, 0))],
        out_shape=[SDS((t, D), BF16), SDS((nb, 8, 2 * BLK), F32)],
        args=(qs, kvx, kvx, u_za, sinks, bias), vmem=VMEM_BIG)
    return o, lse, got


def _out_proj_fwd(x, y_c, o, u_gl, gate_b, w_sm, rider):
    t = x.shape[0]
    tm = min(512, t)

    def body(x_ref, yc_ref, o_ref, gla_ref, glb_ref, gb_ref, wco_ref, wao_ref, wout_ref,
             xn_ref, ya_ref, yb_ref, mg_ref):
        ya = _dot(yc_ref[...], wco_ref[...])
        yb = _dot(o_ref[...], wao_ref[...])
        gb = gb_ref[...]
        ga_ = _sigmoid(gla_ref[...].astype(F32) + gb[:, :D])
        gb_ = _sigmoid(glb_ref[...].astype(F32) + gb[:, D:])
        merged = (ga_ * ya + gb_ * yb).astype(BF16)
        ya_ref[...] = ya.astype(BF16)
        yb_ref[...] = yb.astype(BF16)
        mg_ref[...] = merged
        xn_ref[...] = x_ref[...] + _dot(merged, wout_ref[...])

    row = pl.BlockSpec((tm, D), lambda i: (i, 0))
    wspec = lambda a: pl.BlockSpec((None, D, D), lambda i: (a, 0, 0))
    return _call(
        body, rider, name="out_proj_fwd", grid=(t // tm,),
        in_specs=[row, row, row, pl.BlockSpec((tm, D), lambda i: (i, 0)), pl.BlockSpec((tm, D), lambda i: (i, 1)),
                  pl.BlockSpec((1, 2 * D), lambda i: (0, 0)), wspec(0), wspec(1), wspec(2)],
        out_specs=[row, row, row, row],
        out_shape=[SDS((t, D), F32), SDS((t, D), BF16), SDS((t, D), BF16), SDS((t, D), BF16)],
        args=(x, y_c, o, u_gl, u_gl, gate_b, w_sm, w_sm, w_sm), vmem=VMEM_BIG)


def _loss_head(y, tgt):
    t = y.shape[0]
    tm = min(1024, t)

    def body(y_ref, t_ref, dy_ref, acc_ref):
        @pl.when(pl.program_id(0) == 0)
        def _():
            acc_ref[...] = jnp.zeros_like(acc_ref)
        err = y_ref[...] - t_ref[...]
        dy_ref[...] = err * (1.0 / D)
        sq = _fold8(err * err)
        tot = sq[:, 0:128]
        for k in range(1, D // 128):
            tot = tot + sq[:, 128 * k:128 * (k + 1)]
        acc_ref[...] += tot

    row = pl.BlockSpec((tm, D), lambda i: (i, 0))
    return pl.pallas_call(
        body, name="loss_head", grid=(t // tm,), in_specs=[row, row],
        out_specs=[row, pl.BlockSpec((8, 128), lambda i: (0, 0))],
        out_shape=[SDS((t, D), F32), SDS((8, 128), F32)], compiler_params=_cp("arbitrary"),
    )(y, tgt)


def _out_proj_bwd(dout, y_a, y_b, u_gl, gate_b, w_sm, rider):
    t = dout.shape[0]
    tm = min(512, t)

    def body(do_ref, ya_ref, yb_ref, gla_ref, glb_ref, gb_ref, wco_ref, wao_ref, wout_ref,
             dya_ref, dyb_ref, dgl_ref, dyc_ref, dob_ref, dgb_ref):
        @pl.when(pl.program_id(0) == 0)
        def _():
            dgb_ref[...] = jnp.zeros_like(dgb_ref)
        dm = _dot_nt(do_ref[...].astype(BF16), wout_ref[...])
        gb = gb_ref[...]
        ga_ = _sigmoid(gla_ref[...].astype(F32) + gb[:, :D])
        gb_ = _sigmoid(glb_ref[...].astype(F32) + gb[:, D:])
        dya = (ga_ * dm).astype(BF16)
        dyb = (gb_ * dm).astype(BF16)
        dgla = ya_ref[...].astype(F32) * dm * (ga_ * (1.0 - ga_))
        dglb = yb_ref[...].astype(F32) * dm * (gb_ * (1.0 - gb_))
        dya_ref[...] = dya
        dyb_ref[...] = dyb
        dgl_ref[:, :D] = dgla.astype(BF16)
        dgl_ref[:, D:] = dglb.astype(BF16)
        dgb_ref[:, :D] += _fold8(dgla)
        dgb_ref[:, D:] += _fold8(dglb)
        dyc_ref[...] = _dot_nt(dya, wco_ref[...]).astype(BF16)
        dob_ref[...] = _dot_nt(dyb, wao_ref[...]).astype(BF16)

    row = pl.BlockSpec((tm, D), lambda i: (i, 0))
    wspec = lambda a: pl.BlockSpec((None, D, D), lambda i: (a, 0, 0))
    return _call(
        body, rider, name="out_proj_bwd", grid=(t // tm,),
        in_specs=[row, row, row, pl.BlockSpec((tm, D), lambda i: (i, 0)), pl.BlockSpec((tm, D), lambda i: (i, 1)),
                  pl.BlockSpec((1, 2 * D), lambda i: (0, 0)), wspec(0), wspec(1), wspec(2)],
        out_specs=[row, row, pl.BlockSpec((tm, 2 * D), lambda i: (i, 0)), row, row,
                   pl.BlockSpec((8, 2 * D), lambda i: (0, 0))],
        out_shape=[SDS((t, D), BF16), SDS((t, D), BF16), SDS((t, 2 * D), BF16), SDS((t, D), BF16), SDS((t, D), BF16),
                   SDS((8, 2 * D), F32)],
        args=(dout, y_a, y_b, u_gl, u_gl, gate_b, w_sm, w_sm, w_sm), vmem=VMEM_BIG)


def _small_wgrads(y_c, d_ya, o, d_yb, merged, dout):
    t = y_c.shape[0]
    tk = min(512, t)

    def body(yc_ref, dya_ref, o_ref, dyb_ref, mg_ref, do_ref, g_ref):
        @pl.when(pl.program_id(0) == 0)
        def _():
            g_ref[...] = jnp.zeros_like(g_ref)
        g_ref[0] += _dot_tn(yc_ref[...], dya_ref[...])
        g_ref[1] += _dot_tn(o_ref[...], dyb_ref[...])
        g_ref[2] += _dot_tn(mg_ref[...], do_ref[...].astype(BF16))

    row = pl.BlockSpec((tk, D), lambda k: (k, 0))
    return pl.pallas_call(
        body, name="small_wgrads", grid=(t // tk,), in_specs=[row] * 6,
        out_specs=pl.BlockSpec((3, D, D), lambda k: (0, 0, 0)), out_shape=SDS((3, D, D), F32),
        compiler_params=_cp("arbitrary", vmem=VMEM_BIG),
    )(y_c, d_ya, o, d_yb, merged, dout)


def _conv_bwd(d_yc, u_conv, conv_w, rider):
    t = d_yc.shape[0]
    tm = min(512, t)
    hb = tm // 16
    last_halo = t // 16 - 1
    n_steps = t // tm

    def body(dy_ref, v_ref, b_ref, c_ref, z_ref, hv_ref, hc_ref, ndy_ref, nb_ref, nz_ref, w_ref, du_ref, dw_ref):
        i = pl.program_id(0)

        @pl.when(i == 0)
        def _():
            dw_ref[...] = jnp.zeros_like(dw_ref)
        v, c = v_ref[...].astype(F32), c_ref[...].astype(F32)
        b, z = b_ref[...].astype(F32), z_ref[...].astype(F32)
        cv = c * v
        halo = jnp.where(i > 0, hc_ref[...].astype(F32) * hv_ref[...].astype(F32), 0.0)
        row = lax.broadcasted_iota(jnp.int32, (tm, 1), 0)
        s1 = jnp.where(row == 0, halo[15:16], pltpu.roll(cv, 1, 0))
        s2 = jnp.where(row == 0, halo[14:15], jnp.where(row == 1, halo[15:16], pltpu.roll(cv, 2, 0)))
        w0, w1, w2 = w_ref[0:1, :], w_ref[1:2, :], w_ref[2:3, :]
        conv = w0 * s2 + w1 * s1 + w2 * cv
        sig = _sigmoid(z)
        sz = z * sig
        dsz = sig * (1.0 + z * (1.0 - sig))
        dy = dy_ref[...].astype(F32)
        dconv = dy * b * sz
        nz = nz_ref[...].astype(F32)
        nxt = ndy_ref[...].astype(F32) * nb_ref[...].astype(F32) * (nz * _sigmoid(nz))
        nxt = jnp.where(i < n_steps - 1, nxt, 0.0)
        a1 = jnp.where(row == tm - 1, nxt[0:1], pltpu.roll(dconv, tm - 1, 0))
        a2 = jnp.where(row == tm - 2, nxt[0:1], jnp.where(row == tm - 1, nxt[1:2], pltpu.roll(dconv, tm - 2, 0)))
        dcv = w2 * dconv + w1 * a1 + w0 * a2
        du_ref[:, 0:D] = (dcv * c).astype(BF16)
        du_ref[:, D:2 * D] = (dy * conv * sz).astype(BF16)
        du_ref[:, 2 * D:3 * D] = (dcv * v).astype(BF16)
        du_ref[:, 3 * D:4 * D] = (dy * b * conv * dsz).astype(BF16)
        r8 = lax.broadcasted_iota(jnp.int32, (8, 1), 0)
        dw_ref[...] += jnp.where(r8 == 0, jnp.sum(dconv * s2, axis=0, keepdims=True),
                                 jnp.where(r8 == 1, jnp.sum(dconv * s1, axis=0, keepdims=True),
                                           jnp.where(r8 == 2, jnp.sum(dconv * cv, axis=0, keepdims=True), 0.0)))

    def col(k):
        return pl.BlockSpec((tm, D), lambda i: (i, k))

    def halo(k):
        return pl.BlockSpec((16, D), lambda i: (jnp.maximum(i * hb - 1, 0), k))

    def nxt(k):
        return pl.BlockSpec((16, D), lambda i: (jnp.minimum((i + 1) * hb, last_halo), k))

    return _call(
        body, rider, name="conv_bwd", grid=(t // tm,),
        in_specs=[col(0), col(0), col(1), col(2), col(3), halo(0), halo(2), nxt(0), nxt(1), nxt(3),
                  pl.BlockSpec((3, D), lambda i: (0, 0))],
        out_specs=[pl.BlockSpec((tm, 4 * D), lambda i: (i, 0)), pl.BlockSpec((8, D), lambda i: (0, 0))],
        out_shape=[SDS((t, 4 * D), BF16), SDS((8, D), F32)],
        args=(d_yc, u_conv, u_conv, u_conv, u_conv, u_conv, u_conv, d_yc, u_conv, u_conv, conv_w), vmem=VMEM_BIG)


def _attn_bwd(d_o, qs, kvx, u_za, lse, sinks, bias, rider):
    t = d_o.shape[0]
    nb = t // BLK

    def body(q_ref, kc_ref, kp_ref, za_ref, do_ref, lse_ref, sink_ref, bias_ref,
             dq_ref, dkv_ref, dza_ref, dsk_ref, carry_ref):
        n = pl.program_id(0)

        @pl.when(n == 0)
        def _():
            carry_ref[...] = jnp.zeros_like(carry_ref)
            dsk_ref[...] = jnp.zeros_like(dsk_ref)

        live = n < nb
        kvb = jnp.concatenate([kp_ref[...], kc_ref[...]], axis=0)
        bias_v = bias_ref[...]
        za = za_ref[...].astype(F32)
        sig = _sigmoid(za)
        dsa = sig * (1.0 + za * (1.0 - sig))
        do = jnp.where(live, do_ref[...].astype(F32), 0.0)
        dattn_f = do * (za * sig)
        dattn = dattn_f.astype(BF16)
        dattn_ln2 = (dattn_f * LN2).astype(BF16)
        lo_lanes = lax.broadcasted_iota(jnp.int32, (1, 128), 1) < HEAD
        dq_cols, attn_cols, dk_cols, dv_cols, dsk_rows = [], [], [], [], []
        for hk in range(N_KV):
            qpp = _pair_rows(q_ref, hk)
            dapp = _pair_rows(dattn, hk)
            dapp_ln2 = _pair_rows(dattn_ln2, hk)
            probs, dss, xk, xv = [], [], [], []
            for half in range(2):
                kx, vx = _kv_operands(kvb, hk, half)
                lse = lse_ref[0, 2 * hk + half:2 * hk + half + 1, :]
                prob = jnp.exp2(_dot_nt(kx, qpp) + bias_v - lse)
                psink = jnp.exp2(_sink_row(sink_ref, hk, half) - lse)
                tdp = prob * _dot_nt(vx, dapp_ln2)
                drow = jnp.sum(tdp, axis=0, keepdims=True)
                ds = (tdp - prob * drow).astype(BF16)
                prob_b = prob.astype(BF16)
                xk.append(_dot(ds, qpp))
                xv.append(_dot(prob_b, dapp))
                probs.append(prob_b)
                dss.append(ds)
                dsk_rows.append(-psink * drow * LOG2E)
            kcat = jnp.concatenate([kvb[:, 128 * hk:128 * hk + 128], kvb[:, 512 + 128 * hk:512 + 128 * hk + 128]], axis=0)
            vcat = jnp.concatenate([kvb[:, 1024 + 128 * hk:1024 + 128 * hk + 128],
                                    kvb[:, 1536 + 128 * hk:1536 + 128 * hk + 128]], axis=0)
            app = _dot_tn(jnp.concatenate(probs, axis=0), vcat)
            dqpp = _dot_tn(jnp.concatenate(dss, axis=0), kcat)
            dq_cols += [dqpp[:BLK], dqpp[BLK:]]
            attn_cols += [app[:BLK], app[BLK:]]
            dk_cols.append(jnp.where(lo_lanes, xk[0], xk[1]))
            dv_cols.append(jnp.where(lo_lanes, xv[0], xv[1]))

        @pl.when(live)
        def _():
            dq_ref[...] = jnp.concatenate(dq_cols, axis=1).astype(BF16)
            dza_ref[...] = (do * jnp.concatenate(attn_cols, axis=1) * dsa).astype(BF16)

        band = jnp.concatenate(dk_cols + dv_cols, axis=1)
        dkv_ref[...] = (band[:BLK] + carry_ref[...]).astype(BF16)
        carry_ref[...] = band[BLK:]
        dsk_ref[...] += jnp.broadcast_to(jnp.concatenate(dsk_rows, axis=1), (8, 2 * N_KV * 2 * BLK))

    cur = lambda n: jnp.minimum(n, nb - 1)
    prev = lambda n: jnp.maximum(n - 1, 0)
    return _call(
        body, rider, name="attn_bwd", grid=(nb + 1,),
        in_specs=[pl.BlockSpec((BLK, D), lambda n: (cur(n), 0)),
                  pl.BlockSpec((BLK, KVX), lambda n: (cur(n), 0)), pl.BlockSpec((BLK, KVX), lambda n: (prev(n), 0)),
                  pl.BlockSpec((BLK, D), lambda n: (cur(n), 0)), pl.BlockSpec((BLK, D), lambda n: (cur(n), 0)),
                  pl.BlockSpec((1, 8, 2 * BLK), lambda n: (cur(n), 0, 0)), pl.BlockSpec(memory_space=pltpu.SMEM),
                  pl.BlockSpec((None, 2 * BLK, 2 * BLK), lambda n: (jnp.minimum(n, 1), 0, 0))],
        out_specs=[pl.BlockSpec((BLK, D), lambda n: (cur(n), 0)), pl.BlockSpec((BLK, D), lambda n: (prev(n), 0)),
                   pl.BlockSpec((BLK, D), lambda n: (cur(n), 0)), pl.BlockSpec((8, 2 * D), lambda n: (0, 0))],
        out_shape=[SDS((t, D), BF16), SDS((t, D), BF16), SDS((t, D), BF16), SDS((8, 2 * D), F32)],
        scratch_shapes=[pltpu.VMEM((BLK, D), F32)],
        args=(qs, kvx, kvx, u_za, d_o, lse, sinks, bias), vmem=VMEM_BIG)


def _qkv_post(u_qkv, dqs, dkv, dza, qg_s, kg_t, rider):
    t = u_qkv.shape[0]
    tm = min(512, t)

    def norm_bwd(x, dy, g):
        r = lax.rsqrt(_head_sum(x * x) * (1.0 / HEAD) + EPS)
        xhat = x * r
        dxh = dy * g
        return r * (dxh - xhat * (_head_sum(dxh * xhat) * (1.0 / HEAD))), _fold8(dy * xhat)

    def body(u_ref, dq_ref, dkv_ref, dza_ref, qg_ref, kg_ref, du_ref, dqg_ref, dkg_ref):
        @pl.when(pl.program_id(0) == 0)
        def _():
            dqg_ref[...] = jnp.zeros_like(dqg_ref)
            dkg_ref[...] = jnp.zeros_like(dkg_ref)
        dq, gq = norm_bwd(u_ref[:, 0:D].astype(F32), dq_ref[...].astype(F32), qg_ref[...])
        fold = _fold_mat()
        dk, gk = norm_bwd(u_ref[:, D:D + 256].astype(F32), _dot(dkv_ref[:, 0:512], fold), kg_ref[...])
        du_ref[:, 0:D] = dq.astype(BF16)
        du_ref[:, D:D + 256] = dk.astype(BF16)
        du_ref[:, D + 256:D + 512] = _dot(dkv_ref[:, 512:1024], fold).astype(BF16)
        du_ref[:, D + 512:2 * D + 512] = dza_ref[...]
        dqg_ref[...] += gq
        dkg_ref[...] += gk

    row = pl.BlockSpec((tm, D), lambda i: (i, 0))
    return _call(
        body, rider, name="qkv_post", grid=(t // tm,),
        in_specs=[pl.BlockSpec((tm, 1536), lambda i: (i, 0)), row, row, row,
                  pl.BlockSpec((1, D), lambda i: (0, 0)), pl.BlockSpec((1, 256), lambda i: (0, 0))],
        out_specs=[pl.BlockSpec((tm, 2560), lambda i: (i, 0)), pl.BlockSpec((8, D), lambda i: (0, 0)),
                   pl.BlockSpec((8, 256), lambda i: (0, 0))],
        out_shape=[SDS((t, 2560), BF16), SDS((8, D), F32), SDS((8, 256), F32)],
        args=(u_qkv, dqs, dkv, dza, qg_s, kg_t), vmem=VMEM_BIG)


N_GRAN = N_IN // CB
DU_COLS = ((0, 4096), (4096, 6656), (6656, N_IN))


def _du_granule(j):
    return jnp.clip(j, 0, 7), jnp.clip(j - 8, 0, 4), jnp.clip(j - 13, 0, 3)


def _du_select(j, refs, fn):
    for ref, lo, hi in zip(refs, (0, 8, 13), (8, 13, 17)):
        @pl.when((j >= lo) & (j < hi))
        def _():
            fn(ref)


def _in_proj_bwd(du, w_full, x, g, dout, rider):
    t = du[0].shape[0]
    tn = min(256, t)

    def body(a0, a1, a2, w_hbm, x_ref, g_ref, do_ref, dx_ref, dg_ref, w_ref, sem):
        @pl.when(pl.program_id(0) == 0)
        def _():
            cp = pltpu.make_async_copy(w_hbm, w_ref, sem)
            cp.start()
            dg_ref[...] = jnp.zeros_like(dg_ref)
            cp.wait()
        acc = None
        for a_ref, (lo, hi) in zip((a0, a1, a2), DU_COLS):
            part = _dot_nt(w_ref[:, lo:hi], a_ref[...])
            acc = part if acc is None else acc + part
        dh = acc.T
        xv = x_ref[...]
        r = lax.rsqrt(jnp.mean(xv * xv, axis=-1, keepdims=True) + EPS)
        xhat = xv * r
        dg_ref[...] += _fold8(dh * xhat)
        dxh = dh * g_ref[...]
        dx_ref[...] = do_ref[...] + r * (dxh - xhat * jnp.mean(dxh * xhat, axis=-1, keepdims=True))

    row = pl.BlockSpec((tn, D), lambda i: (i, 0))
    return _call(
        body, rider, name="in_proj_bwd", grid=(t // tn,),
        in_specs=[pl.BlockSpec((tn, hi - lo), lambda i: (i, 0)) for lo, hi in DU_COLS]
        + [ANY, row, pl.BlockSpec((1, D), lambda i: (0, 0)), row],
        out_specs=[row, pl.BlockSpec((8, D), lambda i: (0, 0))], out_shape=[SDS((t, D), F32), SDS((8, D), F32)],
        scratch_shapes=[pltpu.VMEM((D, N_IN), BF16), pltpu.SemaphoreType.DMA(())],
        args=(*du, w_full, x, g, dout), vmem=VMEM_BIG)


def _in_proj_wgrad(ht, du, rider):
    t = ht.shape[1]
    tk = min(4096, t)
    n_k = t // tk

    def body(h_ref, b0, b1, b2, g_ref):
        j, k = pl.program_id(0), pl.program_id(1)

        if n_k > 1:
            @pl.when(k == 0)
            def _():
                g_ref[...] = jnp.zeros_like(g_ref)

        def add(b_ref):
            if n_k > 1:
                g_ref[...] += _dot(h_ref[...], b_ref[...])
            else:
                g_ref[...] = _dot(h_ref[...], b_ref[...])
        _du_select(j, (b0, b1, b2), add)

    seg = lambda q: pl.BlockSpec((tk, CB), lambda j, k: (k, _du_granule(j)[q]))
    (g,), got = _call(
        body, rider, name="in_proj_wgrad", grid=(N_GRAN, t // tk),
        in_specs=[pl.BlockSpec((D, tk), lambda j, k: (0, k)), seg(0), seg(1), seg(2)],
        out_specs=[pl.BlockSpec((D, CB), lambda j, k: (0, j))], out_shape=[SDS((D, N_IN), F32)],
        args=(ht, *du), vmem=VMEM_BIG)
    return g, got


def _swap_rider(g_in, g_sm):
    def copies(ins, outs, send, recv, base=0):
        x, y, c = _mesh_pos()
        cps = []
        for src, dst in zip(ins, outs):
            half = src.at[1 - c] if len(src.shape) == 3 else src.at[:, :, 1 - c]
            cps.append(_rcopy(half, dst, send, recv, base + len(cps), (x, y, 1 - c)))
        return cps

    arrays = [g for g in (g_in, g_sm) if g is not None]
    shapes = [SDS((512, N_IN), F32) if len(g.shape) == 3 else SDS((3, 4, 128, D), F32) for g in arrays]
    return _Rider(arrays, shapes, len(arrays), copies)


def _add_halves_in(cc_idx, g_in, r_in):
    def body(cc_ref, a_ref, b_ref, f_ref, h_ref):
        s = a_ref[...] + b_ref[...]
        h_ref[...] = s.astype(BF16)

        @pl.when(pl.program_id(1) == cc_ref[1])
        def _():
            f_ref[...] = s

    blk = pl.BlockSpec((256, SH_IN), lambda i, j, cc: (i, j))
    return pl.pallas_call(
        body, name="add_halves_in",
        grid_spec=pltpu.PrefetchScalarGridSpec(
            num_scalar_prefetch=1, grid=(2, 4),
            in_specs=[pl.BlockSpec((None, 256, SH_IN), lambda i, j, cc: (cc[0], i, j)), blk],
            out_specs=[pl.BlockSpec((256, SH_IN), lambda i, j, cc: (i, 0)), blk]),
        out_shape=[SDS((512, SH_IN), F32), SDS((512, N_IN), BF16)],
        compiler_params=_cp("arbitrary", "arbitrary", vmem=VMEM_BIG),
    )(cc_idx, g_in, r_in)


def _add_halves_sm(c_idx, g_sm, r_sm):
    def body(c_ref, a_ref, b_ref, f_ref, h_ref):
        s = a_ref[...] + b_ref[...]
        f_ref[...] = s
        h_ref[...] = s.astype(BF16)

    blk = pl.BlockSpec((1, 4, 128, D), lambda a, c: (a, 0, 0, 0))
    return pl.pallas_call(
        body, name="add_halves_sm",
        grid_spec=pltpu.PrefetchScalarGridSpec(
            num_scalar_prefetch=1, grid=(3,),
            in_specs=[pl.BlockSpec((1, 4, None, 128, D), lambda a, c: (a, 0, c[0], 0, 0)), blk], out_specs=[blk, blk]),
        out_shape=[SDS((3, 4, 128, D), F32), SDS((3, 4, 128, D), BF16)], compiler_params=_cp("parallel"),
    )(c_idx, g_sm, r_sm)


def _scatter_rider(h_in, h_sm):
    def copies(ins, outs, send, recv, base=0):
        x, y, c = _mesh_pos()
        cps = []
        for src, dst in zip(ins, outs):
            for k, chip in enumerate(_other_chips(x, y)):
                their = 2 * chip[0] + chip[1]
                part = src.at[:, pl.ds(pl.multiple_of(their * SH_IN, 128), SH_IN)] if len(src.shape) == 2 else src.at[:, their]
                cps.append(_rcopy(part, dst.at[k], send, recv, base + len(cps), (*chip, c)))
        return cps

    arrays = [h for h in (h_in, h_sm) if h is not None]
    shapes = [SDS((3, 512, SH_IN), BF16) if len(h.shape) == 2 else SDS((3, 3, 128, D), BF16) for h in arrays]
    return _Rider(arrays, shapes, 3 * len(arrays), copies)


def _ride_alone(rider, name):
    return _hosted_call(None, rider, name=name, grid=(), in_specs=[], out_specs=[], out_shape=[], args=())[1]


def _final_sum_in(f_in, r_in):
    def body(a_ref, r_ref, o_ref):
        o_ref[...] = a_ref[...] + r_ref[0].astype(F32) + r_ref[1].astype(F32) + r_ref[2].astype(F32)

    return pl.pallas_call(
        body, name="final_sum_in", grid=(4,),
        in_specs=[pl.BlockSpec((128, SH_IN), lambda i: (i, 0)), pl.BlockSpec((3, 128, SH_IN), lambda i: (0, i, 0))],
        out_specs=pl.BlockSpec((128, SH_IN), lambda i: (i, 0)),
        out_shape=SDS((512, SH_IN), F32), compiler_params=_cp("parallel"),
    )(f_in, r_in)


def _final_sum_sm(chip_idx, f_sm, r_sm):
    def body(j_ref, a_ref, r_ref, o_ref):
        o_ref[...] = a_ref[...] + r_ref[0].astype(F32) + r_ref[1].astype(F32) + r_ref[2].astype(F32)

    return pl.pallas_call(
        body, name="final_sum_sm",
        grid_spec=pltpu.PrefetchScalarGridSpec(
            num_scalar_prefetch=1, grid=(3,),
            in_specs=[pl.BlockSpec((1, None, 128, D), lambda a, j: (a, j[0], 0, 0)),
                      pl.BlockSpec((3, 1, 128, D), lambda a, j: (0, a, 0, 0))],
            out_specs=pl.BlockSpec((1, 128, D), lambda a, j: (a, 0, 0))),
        out_shape=SDS((3, 128, D), F32), compiler_params=_cp("parallel"),
    )(chip_idx, f_sm, r_sm)


def _join_halves(t_in, t_sm):
    n_cp = N_LAYERS * 4
    args, plan = [], []
    for l in range(N_LAYERS):
        if t_in[l] is not None:
            plan.append((l, 0, len(args)))
            args.append(t_in[l])
        plan += [(l, a, len(args)) for a in (1, 2, 3)]
        args.append(t_sm[l])

    def body(*refs):
        ins, outs = refs[:len(args)], refs[len(args):len(args) + 4]
        send, recv, loc_in, loc_out, stage_in, stage_sm = refs[len(args) + 4:]
        x, y, c = _mesh_pos()
        cps, own = [], []

        def place(l, a, half):
            rows = 512 if a == 0 else 128
            return outs[a].at[l, pl.ds(pl.multiple_of(half * rows, rows), rows), :]

        for s, (l, a, k) in enumerate(plan):
            src = ins[k] if a == 0 else ins[k].at[a - 1]
            own.append((src, place(l, a, c), min(a, 1)))
            cp = pltpu.make_async_remote_copy(src_ref=src, dst_ref=place(l, a, c), send_sem=send.at[s],
                                              recv_sem=recv.at[s], device_id=(x, y, 1 - c), device_id_type=MESH)
            cp.start()
            cps.append(cp)
        _staged_copies(own, (stage_in, stage_sm), loc_in, loc_out)
        for s, (l, a, k) in enumerate(plan):
            got = place(l, a, 1 - c)
            pltpu.make_async_remote_copy(src_ref=got, dst_ref=got, send_sem=send.at[s], recv_sem=recv.at[s],
                                         device_id=(x, y, 1 - c), device_id_type=MESH).wait_recv()
        for cp in cps:
            cp.wait_send()

    sm = SDS((N_LAYERS, SH_ROW, D), F32)
    return pl.pallas_call(
        body, name="join_halves", in_specs=[ANY] * len(args), out_specs=[ANY] * 4,
        out_shape=[SDS((N_LAYERS, D, SH_IN), F32), sm, sm, sm],
        scratch_shapes=[pltpu.SemaphoreType.DMA((n_cp,))] * 4
        + [pltpu.VMEM((2, 512, SH_IN), F32), pltpu.VMEM((2, 128, D), F32)],
        compiler_params=_cp(vmem=VMEM_BIG),
    )(*args)


def _adam_math(w, g, m, v):
    m = ADAM_B1 * m + (1.0 - ADAM_B1) * g
    v = ADAM_B2 * v + (1.0 - ADAM_B2) * (g * g)
    m_hat = m / (1.0 - ADAM_B1 ** ADAM_STEP)
    v_hat = v / (1.0 - ADAM_B2 ** ADAM_STEP)
    delta = -ADAM_LR * (m_hat / (jnp.sqrt(v_hat) + ADAM_EPS) + ADAM_WD * w)
    return delta, m, v


def _adamw_big(w, g, m, v, name):
    rows, cols = w.shape
    tr = 256

    def body(w_ref, g_ref, m_ref, v_ref, go_ref, d_ref, nm_ref, nv_ref):
        g = g_ref[...]
        go_ref[...] = g
        d_ref[...], nm_ref[...], nv_ref[...] = _adam_math(w_ref[...], g, m_ref[...], v_ref[...])

    blk = pl.BlockSpec((tr, cols), lambda i: (i, 0))
    return pl.pallas_call(
        body, name=name, grid=(rows // tr,), in_specs=[blk] * 4, out_specs=[blk] * 4,
        out_shape=[SDS((rows, cols), F32)] * 4, compiler_params=_cp("parallel", vmem=VMEM_BIG),
    )(w, g, m, v)


def _adamw_small(ws, gs, ms, vs):
    n = len(ws)

    def body(*refs):
        for k in range(n):
            w_ref, g_ref, m_ref, v_ref = (refs[q * n + k] for q in range(4))
            d, nm, nv = _adam_math(w_ref[...], g_ref[...], m_ref[...], v_ref[...])
            refs[4 * n + k][...] = d
            refs[5 * n + k][...] = nm
            refs[6 * n + k][...] = nv

    vm = pl.BlockSpec(memory_space=pltpu.VMEM)
    shapes = [SDS(w.shape, F32) for w in ws]
    res = pl.pallas_call(
        body, name="adamw_small", in_specs=[vm] * (4 * n), out_specs=[vm] * (3 * n), out_shape=shapes * 3,
    )(*ws, *gs, *ms, *vs)
    return res[:n], res[n:2 * n], res[2 * n:]


def _pad_rows(a, rows):
    flat = a.reshape(-1)
    return jnp.pad(flat, (0, rows * 128 - flat.shape[0])).reshape(rows, 128)


def kernel(x, norm_g, w_in, conv_w, q_norm_g, k_norm_g, sinks, w_conv_out, w_attn_out, gate_b, w_out, loss_target, m_norm_g, m_w_in, m_conv_w, m_q_norm_g, m_k_norm_g, m_sinks, m_w_conv_out, m_w_attn_out, m_gate_b, m_w_out, v_norm_g, v_w_in, v_conv_w, v_q_norm_g, v_k_norm_g, v_sinks, v_w_conv_out, v_w_attn_out, v_gate_b, v_w_out):
    xi, yi, ci = _mesh_pos()
    chip = 2 * xi + yi
    c_idx = jnp.reshape(ci, (1,)).astype(jnp.int32)
    chip_idx = jnp.reshape(chip, (1,)).astype(jnp.int32)
    cc_idx = jnp.stack([ci, chip]).astype(jnp.int32)
    t = x.shape[1]
    xs = [x.reshape(t, D)]
    tgt = loss_target.reshape(t, D)

    full_w = [[_cast_w_in(chip_idx, w_in, l), _cast_w_small(chip_idx, w_conv_out, w_attn_out, w_out, l)]
              for l in range(N_LAYERS)]
    conv32 = lax.dynamic_update_slice(jnp.zeros((32, D), F32), jnp.pad(conv_w.reshape(3 * N_LAYERS, SH_ROW), ((0, 20), (0, 0))),
                                      (0, chip * SH_ROW))
    qg_s = jnp.tile(q_norm_g, (1, N_Q)) * (SCALE * LOG2E)
    kg_t = jnp.tile(k_norm_g, (1, N_KV))
    bias = _band_bias()
    saved = []
    for l in range(N_LAYERS):
        nxt = full_w[l + 1] if l + 1 < N_LAYERS else None
        (h, ht), got = _rmsnorm_fwd(xs[l], norm_g[l:l + 1], _gather_rider(
            [full_w[0][0], conv32], "N") if l == 0 else _gather_rider(full_w[l][:1], "B2"))
        if l == 0:
            got = _ride_alone(_gather_rider(got, "F"), "gather_first_forward")
            full_w[0][0], conv32 = _ride_alone(_gather_rider(got, "B"), "gather_first_d2d")
            conv_full = conv32[:3 * N_LAYERS].reshape(N_LAYERS, 3, D)
        else:
            full_w[l][0] = got[0]
        (u_conv, u_qkv, u_za, u_gl), got = _in_proj(h, full_w[l][0], _merge_riders(
            _gather_rider(nxt, "N") if nxt else None, _gather_rider(full_w[l][1:], "B2") if l > 0 else None))
        if nxt:
            nxt[0], nxt[1] = got[:2]
        if l > 0:
            full_w[l][1] = got[-1]
        (qs, kvx), got = _qkv_prep(u_qkv, qg_s[l:l + 1], kg_t[l:l + 1],
                                   _gather_rider(full_w[0][1:], "N") if l == 0 else None)
        y_c, got = _conv_fwd(u_conv, conv_full[l], _gather_rider(got, "F") if l == 0 else None)
        o, lse, got = _attn_fwd(qs, kvx, u_za, sinks[l:l + 1], bias, _merge_riders(
            _gather_rider(nxt, "FB1") if nxt else None, _gather_rider(got, "B") if l == 0 else None))
        if nxt:
            nxt[0], nxt[1] = got[:2]
        if l == 0:
            full_w[0][1] = got[-1]
        (x_next, y_a, y_b, merged), _ = _out_proj_fwd(xs[l], y_c, o, u_gl, gate_b[l:l + 1], full_w[l][1], None)
        xs.append(x_next)
        saved.append((ht, u_conv, u_qkv, u_za, u_gl, y_c, o, y_a, y_b, merged, qs, kvx, lse))

    dout, sq = _loss_head(xs[N_LAYERS], tgt)

    small, t_in, t_sm = [None] * N_LAYERS, [None] * N_LAYERS, [None] * N_LAYERS
    halves = None

    for l in reversed(range(N_LAYERS)):
        w_full, w_sm = full_w[l]
        last = l == 0
        ht, u_conv, u_qkv, u_za, u_gl, y_c, o, y_a, y_b, merged, qs, kvx, lse = saved[l]
        (d_ya, d_yb, du_gl, d_yc, d_o, dgb), _ = _out_proj_bwd(dout, y_a, y_b, u_gl, gate_b[l:l + 1], w_sm, None)
        g_sm = _small_wgrads(y_c, d_ya, o, d_yb, merged, dout).reshape(3, 4, 2, 128, D)
        (du_conv, dcw), got = _conv_bwd(d_yc, u_conv, conv_full[l], _swap_rider(None, g_sm) if last else None)
        if last:
            f_sm0, h_sm0 = _add_halves_sm(c_idx, g_sm, got[0])
        (dqs, dkv, dza, dsk), got = _attn_bwd(d_o, qs, kvx, u_za, lse, sinks[l:l + 1], bias,
                                              _scatter_rider(halves[1], None) if halves else None)
        if halves:
            t_in[l + 1] = _final_sum_in(halves[0], got[0])
        dsk = jnp.sum(dsk[0].reshape(N_KV, 2, 2, BLK), axis=-1).transpose(0, 2, 1).reshape(N_Q)
        (du_attn, dqg, dkg), _ = _qkv_post(u_qkv, dqs, dkv, dza, qg_s[l:l + 1], kg_t[l:l + 1], None)
        du = (du_conv, du_attn, du_gl)
        g_in, got = _in_proj_wgrad(ht, du, _merge_riders(
            _scatter_rider(None, halves[3]) if halves else None, _scatter_rider(None, h_sm0) if last else None))
        g_in = g_in.reshape(2, 512, N_IN)
        if halves:
            t_sm[l + 1] = _final_sum_sm(chip_idx, halves[2], got[0])
        if last:
            t_sm[0] = _final_sum_sm(chip_idx, f_sm0, got[-1])
        if last:
            f_in0, h_in0 = _add_halves_in(cc_idx, g_in, _ride_alone(_swap_rider(g_in, None), "swap_last")[0])
        (dout, dng), got = _in_proj_bwd(du, w_full, xs[l], norm_g[l:l + 1], dout,
                                        _scatter_rider(h_in0, None) if last else _swap_rider(g_in, g_sm))
        if last:
            t_in[0] = _final_sum_in(f_in0, got[0])
        else:
            halves = _add_halves_in(cc_idx, g_in, got[0]) + _add_halves_sm(c_idx, g_sm, got[1])
        small[l] = (jnp.sum(dng, axis=0), (SCALE * LOG2E) * jnp.sum(dqg.reshape(8 * N_Q, HEAD), axis=0),
                    jnp.sum(dkg.reshape(8 * N_KV, HEAD), axis=0), dsk, jnp.sum(dgb, axis=0), dcw[:3])
    grad_x = dout.reshape(1, t, D)

    stack = lambda k: jnp.stack([small[l][k] for l in range(N_LAYERS)])
    pack = jnp.concatenate([_pad_rows(stack(0), 32), _pad_rows(stack(1), 8), _pad_rows(stack(2), 8),
                            _pad_rows(stack(3), 8), _pad_rows(stack(4), 64), _pad_rows(stack(5), 96),
                            _pad_rows(jnp.sum(sq) * (0.5 / D), 8)], axis=0)
    red = _allreduce_small(pack)
    loss = red[216, 0]
    g_norm_g = red[0:32].reshape(N_LAYERS, D)
    g_q_norm_g = red[32:40].reshape(-1)[:N_LAYERS * HEAD].reshape(N_LAYERS, HEAD)
    g_k_norm_g = red[40:48].reshape(-1)[:N_LAYERS * HEAD].reshape(N_LAYERS, HEAD)
    g_sinks = red[48:56].reshape(-1)[:N_LAYERS * N_Q].reshape(N_LAYERS, N_Q)
    g_gate_b = red[56:120].reshape(N_LAYERS, 2 * D)
    g_conv_full = red[120:216].reshape(N_LAYERS, 3, D)
    g_conv_w = lax.dynamic_slice(g_conv_full, (0, 0, chip * SH_ROW), (N_LAYERS, 3, SH_ROW))

    g_w_in, g_w_co, g_w_ao, g_w_out = _join_halves(t_in, t_sm)

    r_in = N_LAYERS * D
    g_w_in, d_in, nm_in, nv_in = (a.reshape(N_LAYERS, D, SH_IN) for a in _adamw_big(
        w_in.reshape(r_in, SH_IN), g_w_in.reshape(r_in, SH_IN), m_w_in.reshape(r_in, SH_IN),
        v_w_in.reshape(r_in, SH_IN), "adamw_w_in"))
    r_sm = N_LAYERS * SH_ROW
    big = {}
    for nm, w, g, m, v in (("co", w_conv_out, g_w_co, m_w_conv_out, v_w_conv_out),
                           ("ao", w_attn_out, g_w_ao, m_w_attn_out, v_w_attn_out),
                           ("out", w_out, g_w_out, m_w_out, v_w_out)):
        big[nm] = tuple(a.reshape(N_LAYERS, SH_ROW, D) for a in _adamw_big(
            w.reshape(r_sm, D), g.reshape(r_sm, D), m.reshape(r_sm, D), v.reshape(r_sm, D), "adamw_w_small"))
    g_w_co, g_w_ao, g_w_out = big["co"][0], big["ao"][0], big["out"][0]
    sm_w = [norm_g, conv_w, q_norm_g, k_norm_g, sinks, gate_b]
    sm_g = [g_norm_g, g_conv_w, g_q_norm_g, g_k_norm_g, g_sinks, g_gate_b]
    sm_m = [m_norm_g, m_conv_w, m_q_norm_g, m_k_norm_g, m_sinks, m_gate_b]
    sm_v = [v_norm_g, v_conv_w, v_q_norm_g, v_k_norm_g, v_sinks, v_gate_b]
    sd, snm, snv = _adamw_small(sm_w, sm_g, sm_m, sm_v)

    def order(norm, w_in_, conv, qn, kn, sk, co, ao, gb, wo):
        return [norm, w_in_, conv, qn, kn, sk, co, ao, gb, wo]

    grads = order(g_norm_g, g_w_in, g_conv_w, g_q_norm_g, g_k_norm_g, g_sinks, g_w_co, g_w_ao, g_gate_b, g_w_out)
    deltas = order(sd[0], d_in, sd[1], sd[2], sd[3], sd[4], big["co"][1], big["ao"][1], sd[5], big["out"][1])
    new_m = order(snm[0], nm_in, snm[1], snm[2], snm[3], snm[4], big["co"][2], big["ao"][2], snm[5], big["out"][2])
    new_v = order(snv[0], nv_in, snv[1], snv[2], snv[3], snv[4], big["co"][3], big["ao"][3], snv[5], big["out"][3])
    return (loss, grad_x, *grads, *deltas, *new_m, *new_v)
```

```python
import functools

import jax
import jax.numpy as jnp
from jax import lax
from jax.experimental import pallas as pl
from jax.experimental.pallas import tpu as pltpu

F32, BF16 = jnp.float32, jnp.bfloat16
SDS = jax.ShapeDtypeStruct
MESH = pl.DeviceIdType.MESH
ANY = pl.BlockSpec(memory_space=pl.ANY)

D = 1024
N_IN = 8704
N_LAYERS = 4
N_Q, N_KV, HEAD = 16, 4, 64
GROUP = N_Q // N_KV
BLK = 128
EPS = 1e-6
NEG = -1e30
SCALE = HEAD ** -0.5
SH_IN = N_IN // 4
SH_ROW = D // 4
CB = 512
VMEM_BIG = 56 * 1024 * 1024

ADAM_LR, ADAM_B1, ADAM_B2, ADAM_EPS, ADAM_WD, ADAM_STEP = 0.001, 0.9, 0.999, 1e-08, 0.01, 10


def _cp(*sem, vmem=None):
    return pltpu.CompilerParams(dimension_semantics=sem if sem else None, vmem_limit_bytes=vmem)


def _sigmoid(z):
    return 1.0 / (1.0 + jnp.exp(-z))


def _dot(a, b):
    return jnp.dot(a, b, preferred_element_type=F32)


def _dot_nt(a, b):
    return lax.dot_general(a, b, (((1,), (1,)), ((), ())), preferred_element_type=F32)


def _dot_tn(a, b):
    return lax.dot_general(a, b, (((0,), (0,)), ((), ())), preferred_element_type=F32)


def _fold8(v):
    return jnp.sum(v.reshape(v.shape[0] // 8, 8, v.shape[1]), axis=0)


def _cast_weights(chip_idx, w, a, b, c, layer):
    def body(j_ref, w_ref, a_ref, b_ref, c_ref, o_ref, s_ref):
        o_ref[...] = w_ref[...].astype(BF16)
        s_ref[0] = a_ref[...].astype(BF16)
        s_ref[1] = b_ref[...].astype(BF16)
        s_ref[2] = c_ref[...].astype(BF16)

    small = pl.BlockSpec((None, SH_ROW // 2, D), lambda i, j: (layer, i, 0))
    return pl.pallas_call(
        body, name="cast_weights",
        grid_spec=pltpu.PrefetchScalarGridSpec(
            num_scalar_prefetch=1, grid=(2,),
            in_specs=[pl.BlockSpec((None, 512, SH_IN), lambda i, j: (layer, i, 0)), small, small, small],
            out_specs=[pl.BlockSpec((512, SH_IN), lambda i, j: (i, j[0])),
                       pl.BlockSpec((3, SH_ROW // 2, D), lambda i, j: (0, 2 * j[0] + i, 0))]),
        out_shape=[SDS((D, N_IN), BF16), SDS((3, D, D), BF16)], compiler_params=_cp("parallel"),
    )(chip_idx, w, a, b, c)


def _mesh_pos():
    return lax.axis_index("x"), lax.axis_index("y"), lax.axis_index("c")


def _other_chips(x, y):
    return [(1 - x, y), (x, 1 - y), (1 - x, 1 - y)]


class _Rider:
    def __init__(self, ins, out_shape, n, copies, aliases=()):
        self.ins, self.out_shape, self.n, self.copies, self.aliases = list(ins), list(out_shape), n, copies, aliases


def _merge_riders(*riders):
    riders = [r for r in riders if r is not None]
    if len(riders) < 2:
        return riders[0] if riders else None

    def copies(ins, outs, send, recv, base=0):
        cps, i0, o0 = [], 0, 0
        for r in riders:
            cps += r.copies(ins[i0:i0 + len(r.ins)], outs[o0:o0 + len(r.out_shape)], send, recv, base + len(cps))
            i0, o0 = i0 + len(r.ins), o0 + len(r.out_shape)
        return cps

    aliases, i0, o0 = [], 0, 0
    for r in riders:
        aliases += [(i0 + i, o0 + o) for i, o in r.aliases]
        i0, o0 = i0 + len(r.ins), o0 + len(r.out_shape)
    return _Rider(sum((r.ins for r in riders), []), sum((r.out_shape for r in riders), []),
                  sum(r.n for r in riders), copies, tuple(aliases))


def _rcopy(src, dst, send, recv, k, to):
    return pltpu.make_async_remote_copy(src_ref=src, dst_ref=dst, send_sem=send.at[k], recv_sem=recv.at[k],
                                        device_id=to, device_id_type=MESH)


def _hosted_call(body, rider, *, name, grid, in_specs, out_specs, out_shape, args, scratch_shapes=(), vmem=None):
    n_in, n_out, n_scr = len(in_specs), len(out_specs), len(scratch_shapes)
    r_in, r_out = len(rider.ins), len(rider.out_shape)

    def full_body(*refs):
        host_in, rid_in = refs[:n_in], refs[n_in:n_in + r_in]
        o0 = n_in + r_in
        host_out, rid_out = refs[o0:o0 + n_out], refs[o0 + n_out:o0 + n_out + r_out]
        s0 = o0 + n_out + r_out
        host_scr, (send, recv) = refs[s0:s0 + n_scr], refs[s0 + n_scr:]
        if body is None:
            cps = rider.copies(rid_in, rid_out, send, recv)
            for cp in cps:
                cp.start()
            for cp in cps:
                cp.wait()
            return
        ids = [pl.program_id(a) for a in range(len(grid))]
        first = functools.reduce(lambda p, q: p & q, [i == 0 for i in ids])
        last = functools.reduce(lambda p, q: p & q, [i == g - 1 for i, g in zip(ids, grid)])

        @pl.when(first)
        def _():
            for cp in rider.copies(rid_in, rid_out, send, recv):
                cp.start()

        body(*host_in, *host_out, *host_scr)

        @pl.when(last)
        def _():
            for cp in rider.copies(rid_in, rid_out, send, recv):
                cp.wait()

    res = pl.pallas_call(
        full_body, name=name, grid=grid if body is not None else (),
        in_specs=list(in_specs) + [ANY] * r_in, out_specs=list(out_specs) + [ANY] * r_out,
        out_shape=list(out_shape) + rider.out_shape,
        scratch_shapes=list(scratch_shapes) + [pltpu.SemaphoreType.DMA((rider.n,))] * 2,
        input_output_aliases={n_in + i: n_out + o for i, o in rider.aliases},
        compiler_params=_cp(*(("arbitrary",) * len(grid) if body is not None else ()), vmem=vmem),
    )(*args, *rider.ins)
    return res[:n_out], res[n_out:]


def _call(body, rider, **kw):
    if rider is not None:
        return _hosted_call(body, rider, **kw)
    res = pl.pallas_call(
        body, name=kw["name"], grid=kw["grid"], in_specs=list(kw["in_specs"]), out_specs=list(kw["out_specs"]),
        out_shape=list(kw["out_shape"]), scratch_shapes=list(kw.get("scratch_shapes", ())),
        compiler_params=_cp(*(("arbitrary",) * len(kw["grid"])), vmem=kw.get("vmem")),
    )(*kw["args"])
    return res, []


def _gather_rider(arrays, stage):
    def region(full, whose, c, sub):
        if len(full.shape) == 2:
            rows, cols = full.shape[0] // 2, full.shape[1] // 4
            first, n = (c * rows, rows) if sub is None else (c * rows + sub * (rows // 2), rows // 2)
            return full.at[pl.ds(pl.multiple_of(first, n), n), pl.ds(pl.multiple_of(whose * cols, 128), cols)]
        first, n = (whose * SH_ROW + c * 128, 128) if sub is None else (whose * SH_ROW + c * 128 + sub * 64, 64)
        return full.at[:, pl.ds(pl.multiple_of(first, n), n), :]

    def copies(ins, outs, send, recv, base=0):
        x, y, c = _mesh_pos()
        nbr_x, nbr_y = (1 - x, y), (x, 1 - y)
        cps = []
        for full in outs:
            plan = []
            if stage == "N":
                plan = [(region(full, 2 * x + y, c, None), (*nbr_x, c)), (region(full, 2 * x + y, c, None), (*nbr_y, c))]
            if stage in ("F", "FB1"):
                plan = [(region(full, 2 * nbr_x[0] + nbr_x[1], c, 0), (*nbr_y, c)),
                        (region(full, 2 * nbr_y[0] + nbr_y[1], c, 1), (*nbr_x, c))]
            if stage in ("B", "FB1", "B2"):
                chips = {"B": _other_chips(x, y), "FB1": [nbr_x, nbr_y], "B2": [(1 - x, 1 - y)]}[stage]
                plan += [(region(full, 2 * chip[0] + chip[1], c, None), (x, y, 1 - c)) for chip in chips]
            for reg, to in plan:
                cps.append(_rcopy(reg, reg, send, recv, base + len(cps), to))
        return cps

    per_array = {"N": 2, "F": 2, "B": 3, "FB1": 4, "B2": 1}[stage]
    return _Rider(arrays, [SDS(v.shape, v.dtype) for v in arrays], per_array * len(arrays), copies,
                  aliases=tuple((i, i) for i in range(len(arrays))))


def _staged_copies(copies, stages, sem_in, sem_out):
    busy, count = {}, {}
    for idx, (src, dst, kind) in enumerate(copies):
        slot = count.get(kind, 0) % 2
        count[kind] = count.get(kind, 0) + 1
        if (kind, slot) in busy:
            busy.pop((kind, slot)).wait()
        buf = stages[kind].at[slot]
        cin = pltpu.make_async_copy(src, buf, sem_in.at[idx])
        cin.start()
        cin.wait()
        cout = pltpu.make_async_copy(buf, dst, sem_out.at[idx])
        cout.start()
        busy[(kind, slot)] = cout
    for cp in busy.values():
        cp.wait()


def _allreduce_small(pack):
    rows = pack.shape[0]

    def body(p_ref, o_ref, buf, send, recv):
        x, y, c = _mesh_pos()
        me = 4 * x + 2 * y + c
        sends = []
        for r in range(1, 8):
            to = (x if not (r & 4) else 1 - x, y if not (r & 2) else 1 - y, c if not (r & 1) else 1 - c)
            cp = pltpu.make_async_remote_copy(src_ref=p_ref, dst_ref=buf.at[me], send_sem=send.at[r - 1],
                                              recv_sem=recv.at[r - 1], device_id=to, device_id_type=MESH)
            cp.start()
            sends.append(cp)
        buf[me] = p_ref[...]
        for r in range(1, 8):
            frm = (4 * x + 2 * y + c) ^ r
            pltpu.make_async_remote_copy(src_ref=p_ref, dst_ref=buf.at[frm], send_sem=send.at[r - 1],
                                         recv_sem=recv.at[r - 1], device_id=(x, y, c), device_id_type=MESH).wait_recv()
        acc = buf[0]
        for d in range(1, 8):
            acc = acc + buf[d]
        o_ref[...] = acc
        for cp in sends:
            cp.wait_send()

    vm = pl.BlockSpec(memory_space=pltpu.VMEM)
    return pl.pallas_call(
        body, name="allreduce_small", in_specs=[vm], out_specs=vm, out_shape=SDS((rows, 128), F32),
        scratch_shapes=[pltpu.VMEM((8, rows, 128), F32), pltpu.SemaphoreType.DMA((7,)), pltpu.SemaphoreType.DMA((7,))],
    )(pack)


def _rmsnorm_fwd(x, g, rider):
    t = x.shape[0]
    tm = min(1024, t)

    def body(x_ref, g_ref, h_ref, ht_ref):
        xv = x_ref[...]
        r = lax.rsqrt(jnp.mean(xv * xv, axis=-1, keepdims=True) + EPS)
        h = xv * r * g_ref[...]
        h_ref[...] = h.astype(BF16)
        ht_ref[...] = h.T.astype(BF16)

    return _call(
        body, rider, name="rmsnorm_fwd", grid=(t // tm,),
        in_specs=[pl.BlockSpec((tm, D), lambda i: (i, 0)), pl.BlockSpec((1, D), lambda i: (0, 0))],
        out_specs=[pl.BlockSpec((tm, D), lambda i: (i, 0)), pl.BlockSpec((D, tm), lambda i: (0, i))],
        out_shape=[SDS((t, D), BF16), SDS((D, t), BF16)], args=(x, g), vmem=VMEM_BIG)


FWD_SEGS = ((0, 8), (8, 3), (11, 2), (13, 4))


def _in_proj(h, w_full, rider):
    t = h.shape[0]
    tm = min(2048, t)

    def body(a_ref, b_ref, *o_refs):
        j = pl.program_id(1)
        for o_ref, (off, nblk) in zip(o_refs, FWD_SEGS):
            @pl.when((j >= off) & (j < off + nblk))
            def _():
                o_ref[...] = _dot(a_ref[...], b_ref[...]).astype(BF16)

    def out(seg):
        off, nblk = seg
        return pl.BlockSpec((tm, CB), lambda i, j: (i, jnp.clip(j - off, 0, nblk - 1)))

    res, got = _call(
        body, rider, name="in_proj", grid=(t // tm, N_IN // CB),
        in_specs=[pl.BlockSpec((tm, D), lambda i, j: (i, 0)), pl.BlockSpec((D, CB), lambda i, j: (0, j))],
        out_specs=[out(s) for s in FWD_SEGS], out_shape=[SDS((t, s[1] * CB), BF16) for s in FWD_SEGS],
        args=(h, w_full), vmem=VMEM_BIG)
    return res, got


def _conv_fwd(u_conv, conv_w, rider):
    t = u_conv.shape[0]
    tm = min(512, t)
    hb = tm // 16

    def body(v_ref, b_ref, c_ref, z_ref, hv_ref, hc_ref, w_ref, y_ref):
        i = pl.program_id(0)
        cv = c_ref[...].astype(F32) * v_ref[...].astype(F32)
        halo = hc_ref[...].astype(F32) * hv_ref[...].astype(F32)
        halo = jnp.where(i > 0, halo, 0.0)
        row = lax.broadcasted_iota(jnp.int32, (tm, 1), 0)
        s1 = jnp.where(row == 0, halo[15:16], pltpu.roll(cv, 1, 0))
        s2 = jnp.where(row == 0, halo[14:15], jnp.where(row == 1, halo[15:16], pltpu.roll(cv, 2, 0)))
        conv = w_ref[0:1, :] * s2 + w_ref[1:2, :] * s1 + w_ref[2:3, :] * cv
        z = z_ref[...].astype(F32)
        y_ref[...] = (b_ref[...].astype(F32) * conv * (z * _sigmoid(z))).astype(BF16)

    def col(k):
        return pl.BlockSpec((tm, D), lambda i: (i, k))

    def halo(k):
        return pl.BlockSpec((16, D), lambda i: (jnp.maximum(i * hb - 1, 0), k))

    (y_c,), got = _call(
        body, rider, name="conv_fwd", grid=(t // tm,),
        in_specs=[col(0), col(1), col(2), col(3), halo(0), halo(2), pl.BlockSpec((3, D), lambda i: (0, 0))],
        out_specs=[pl.BlockSpec((tm, D), lambda i: (i, 0))], out_shape=[SDS((t, D), BF16)],
        args=(u_conv, u_conv, u_conv, u_conv, u_conv, u_conv, conv_w), vmem=VMEM_BIG)
    return y_c, got


KVX = 4 * N_KV * 128


def _iota2(shape):
    return lax.broadcasted_iota(jnp.int32, shape, 0), lax.broadcasted_iota(jnp.int32, shape, 1)


def _head_sum(v):
    r, c = _iota2((128, 128))
    ones = ((r >> 6) == (c >> 6)).astype(BF16)
    hi = v.astype(BF16)
    lo = (v - hi.astype(F32)).astype(BF16)
    return jnp.concatenate([_dot(hi[:, g:g + 128], ones) + _dot(lo[:, g:g + 128], ones)
                            for g in range(0, v.shape[1], 128)], axis=1)


def _expand_mats():
    r, c = _iota2((N_KV * HEAD, N_KV * 128))
    base = ((r >> 6) << 7) + (r & 63)
    return (c == base).astype(BF16), (c == base + 64).astype(BF16)


def _fold_mat():
    r, c = _iota2((N_KV * 128, N_KV * HEAD))
    return (((r >> 7) == (c >> 6)) & ((r & 63) == (c & 63))).astype(BF16)


def _qkv_prep(u_qkv, qg_s, kg_t, rider):
    t = u_qkv.shape[0]
    tm = min(1024, t)

    def body(u_ref, qg_ref, kg_ref, qs_ref, kvx_ref):
        q = u_ref[:, 0:D].astype(F32)
        rq = lax.rsqrt(_head_sum(q * q) * (1.0 / HEAD) + EPS)
        qs_ref[...] = (q * rq * qg_ref[...]).astype(BF16)
        k = u_ref[:, D:D + 256].astype(F32)
        rk = lax.rsqrt(_head_sum(k * k) * (1.0 / HEAD) + EPS)
        kn = (k * rk * kg_ref[...]).astype(BF16)
        v = u_ref[:, D + 256:D + 512]
        e_lo, e_hi = _expand_mats()
        kvx_ref[:, 0:512] = _dot(kn, e_lo).astype(BF16)
        kvx_ref[:, 512:1024] = _dot(kn, e_hi).astype(BF16)
        kvx_ref[:, 1024:1536] = _dot(v, e_lo).astype(BF16)
        kvx_ref[:, 1536:2048] = _dot(v, e_hi).astype(BF16)

    return _call(
        body, rider, name="qkv_prep", grid=(t // tm,),
        in_specs=[pl.BlockSpec((tm, 1536), lambda i: (i, 0)), pl.BlockSpec((1, D), lambda i: (0, 0)),
                  pl.BlockSpec((1, 256), lambda i: (0, 0))],
        out_specs=[pl.BlockSpec((tm, D), lambda i: (i, 0)), pl.BlockSpec((tm, KVX), lambda i: (i, 0))],
        out_shape=[SDS((t, D), BF16), SDS((t, KVX), BF16)], args=(u_qkv, qg_s, kg_t), vmem=VMEM_BIG)


def _band_bias():
    j, r = _iota2((2 * BLK, 2 * BLK))
    diff = (r & (BLK - 1)) - j + BLK
    band = (diff >= 0) & (diff < BLK)
    return jnp.stack([jnp.where(band & (j >= BLK), 0.0, NEG), jnp.where(band, 0.0, NEG)]).astype(F32)


def _pair_rows(ref_or_val, hk):
    return jnp.concatenate([ref_or_val[:, 256 * hk:256 * hk + 128], ref_or_val[:, 256 * hk + 128:256 * hk + 256]], axis=0)


LOG2E, LN2 = 1.4426950408889634, 0.6931471805599453


def _sink_row(sink_ref, hk, half):
    return jnp.concatenate([jnp.full((1, BLK), sink_ref[0, GROUP * hk + half] * LOG2E, F32),
                            jnp.full((1, BLK), sink_ref[0, GROUP * hk + 2 + half] * LOG2E, F32)], axis=1)


def _kv_operands(kvb, hk, half):
    return (kvb[:, 512 * half + 128 * hk:512 * half + 128 * hk + 128],
            kvb[:, 1024 + 512 * half + 128 * hk:1024 + 512 * half + 128 * hk + 128])


def _attn_fwd(qs, kvx, u_za, sinks, bias, rider):
    t = qs.shape[0]
    nb = t // BLK

    def body(q_ref, kc_ref, kp_ref, za_ref, sink_ref, bias_ref, o_ref, lse_ref):
        kvb = jnp.concatenate([kp_ref[...], kc_ref[...]], axis=0)
        bias_v = bias_ref[...]
        key0 = lax.broadcasted_iota(jnp.int32, (2 * BLK, 1), 0) == 0
        ones = jnp.ones((2 * BLK, 128), BF16)
        cols = []
        for hk in range(N_KV):
            qpp = _pair_rows(q_ref, hk)
            opp = None
            for half in range(2):
                kx, vx = _kv_operands(kvb, hk, half)
                s = _dot_nt(kx, qpp) + bias_v
                sink = _sink_row(sink_ref, hk, half)
                m = jnp.maximum(jnp.max(s, axis=0, keepdims=True), sink)
                p = jnp.exp2(s - m)
                es = jnp.exp2(sink - m)
                lse_ref[0, 2 * hk + half:2 * hk + half + 1, :] = m + jnp.log(jnp.sum(p, axis=0, keepdims=True) + es) * LOG2E
                pe = jnp.where(key0, es, p).astype(BF16)
                rhs = jnp.concatenate([jnp.where(key0, jnp.zeros_like(vx), vx), ones], axis=1)
                nd = _dot_tn(pe, rhs)
                o = nd[:, :128] * (1.0 / nd[:, 128:])
                opp = o if opp is None else opp + o
            cols += [opp[:BLK], opp[BLK:]]
        za = za_ref[...].astype(F32)
        o_ref[...] = (jnp.concatenate(cols, axis=1) * (za * _sigmoid(za))).astype(BF16)

    prev = lambda n: jnp.maximum(n - 1, 0)
    (o, lse), got = _call(
        body, rider, name="attn_fwd", grid=(nb,),
        in_specs=[pl.BlockSpec((BLK, D), lambda n: (n, 0)),
                  pl.BlockSpec((BLK, KVX), lambda n: (n, 0)), pl.BlockSpec((BLK, KVX), lambda n: (prev(n), 0)),
                  pl.BlockSpec((BLK, D), lambda n: (n, 0)), pl.BlockSpec(memory_space=pltpu.SMEM),
                  pl.BlockSpec((None, 2 * BLK, 2 * BLK), lambda n: (jnp.minimum(n, 1), 0, 0))],
        out_specs=[pl.BlockSpec((BLK, D), lambda n: (n, 0)), pl.BlockSpec((1, 8, 2 * BLK), lambda n: (n, 0, 0))],
        out_shape=[SDS((t, D), BF16), SDS((nb, 8, 2 * BLK), F32)],
        args=(qs, kvx, kvx, u_za, sinks, bias), vmem=VMEM_BIG)
    return o, lse, got


def _out_proj_fwd(x, y_c, o, u_gl, gate_b, w_sm, rider):
    t = x.shape[0]
    tm = min(512, t)

    def body(x_ref, yc_ref, o_ref, gla_ref, glb_ref, gb_ref, wco_ref, wao_ref, wout_ref,
             xn_ref, ya_ref, yb_ref, mg_ref):
        ya = _dot(yc_ref[...], wco_ref[...])
        yb = _dot(o_ref[...], wao_ref[...])
        gb = gb_ref[...]
        ga_ = _sigmoid(gla_ref[...].astype(F32) + gb[:, :D])
        gb_ = _sigmoid(glb_ref[...].astype(F32) + gb[:, D:])
        merged = (ga_ * ya + gb_ * yb).astype(BF16)
        ya_ref[...] = ya.astype(BF16)
        yb_ref[...] = yb.astype(BF16)
        mg_ref[...] = merged
        xn_ref[...] = x_ref[...] + _dot(merged, wout_ref[...])

    row = pl.BlockSpec((tm, D), lambda i: (i, 0))
    wspec = lambda a: pl.BlockSpec((None, D, D), lambda i: (a, 0, 0))
    return _call(
        body, rider, name="out_proj_fwd", grid=(t // tm,),
        in_specs=[row, row, row, pl.BlockSpec((tm, D), lambda i: (i, 0)), pl.BlockSpec((tm, D), lambda i: (i, 1)),
                  pl.BlockSpec((1, 2 * D), lambda i: (0, 0)), wspec(0), wspec(1), wspec(2)],
        out_specs=[row, row, row, row],
        out_shape=[SDS((t, D), F32), SDS((t, D), BF16), SDS((t, D), BF16), SDS((t, D), BF16)],
        args=(x, y_c, o, u_gl, u_gl, gate_b, w_sm, w_sm, w_sm), vmem=VMEM_BIG)


def _loss_head(y, tgt):
    t = y.shape[0]
    tm = min(1024, t)

    def body(y_ref, t_ref, dy_ref, acc_ref):
        @pl.when(pl.program_id(0) == 0)
        def _():
            acc_ref[...] = jnp.zeros_like(acc_ref)
        err = y_ref[...] - t_ref[...]
        dy_ref[...] = err * (1.0 / D)
        sq = _fold8(err * err)
        tot = sq[:, 0:128]
        for k in range(1, D // 128):
            tot = tot + sq[:, 128 * k:128 * (k + 1)]
        acc_ref[...] += tot

    row = pl.BlockSpec((tm, D), lambda i: (i, 0))
    return pl.pallas_call(
        body, name="loss_head", grid=(t // tm,), in_specs=[row, row],
        out_specs=[row, pl.BlockSpec((8, 128), lambda i: (0, 0))],
        out_shape=[SDS((t, D), F32), SDS((8, 128), F32)], compiler_params=_cp("arbitrary"),
    )(y, tgt)


def _out_proj_bwd(dout, y_a, y_b, u_gl, gate_b, w_sm, rider):
    t = dout.shape[0]
    tm = min(512, t)

    def body(do_ref, ya_ref, yb_ref, gla_ref, glb_ref, gb_ref, wco_ref, wao_ref, wout_ref,
             dya_ref, dyb_ref, dgl_ref, dyc_ref, dob_ref, dgb_ref):
        @pl.when(pl.program_id(0) == 0)
        def _():
            dgb_ref[...] = jnp.zeros_like(dgb_ref)
        dm = _dot_nt(do_ref[...].astype(BF16), wout_ref[...])
        gb = gb_ref[...]
        ga_ = _sigmoid(gla_ref[...].astype(F32) + gb[:, :D])
        gb_ = _sigmoid(glb_ref[...].astype(F32) + gb[:, D:])
        dya = (ga_ * dm).astype(BF16)
        dyb = (gb_ * dm).astype(BF16)
        dgla = ya_ref[...].astype(F32) * dm * (ga_ * (1.0 - ga_))
        dglb = yb_ref[...].astype(F32) * dm * (gb_ * (1.0 - gb_))
        dya_ref[...] = dya
        dyb_ref[...] = dyb
        dgl_ref[:, :D] = dgla.astype(BF16)
        dgl_ref[:, D:] = dglb.astype(BF16)
        dgb_ref[:, :D] += _fold8(dgla)
        dgb_ref[:, D:] += _fold8(dglb)
        dyc_ref[...] = _dot_nt(dya, wco_ref[...]).astype(BF16)
        dob_ref[...] = _dot_nt(dyb, wao_ref[...]).astype(BF16)

    row = pl.BlockSpec((tm, D), lambda i: (i, 0))
    wspec = lambda a: pl.BlockSpec((None, D, D), lambda i: (a, 0, 0))
    return _call(
        body, rider, name="out_proj_bwd", grid=(t // tm,),
        in_specs=[row, row, row, pl.BlockSpec((tm, D), lambda i: (i, 0)), pl.BlockSpec((tm, D), lambda i: (i, 1)),
                  pl.BlockSpec((1, 2 * D), lambda i: (0, 0)), wspec(0), wspec(1), wspec(2)],
        out_specs=[row, row, pl.BlockSpec((tm, 2 * D), lambda i: (i, 0)), row, row,
                   pl.BlockSpec((8, 2 * D), lambda i: (0, 0))],
        out_shape=[SDS((t, D), BF16), SDS((t, D), BF16), SDS((t, 2 * D), BF16), SDS((t, D), BF16), SDS((t, D), BF16),
                   SDS((8, 2 * D), F32)],
        args=(dout, y_a, y_b, u_gl, u_gl, gate_b, w_sm, w_sm, w_sm), vmem=VMEM_BIG)


def _small_wgrads(y_c, d_ya, o, d_yb, merged, dout):
    t = y_c.shape[0]
    tk = min(512, t)

    def body(yc_ref, dya_ref, o_ref, dyb_ref, mg_ref, do_ref, g_ref):
        @pl.when(pl.program_id(0) == 0)
        def _():
            g_ref[...] = jnp.zeros_like(g_ref)
        g_ref[0] += _dot_tn(yc_ref[...], dya_ref[...])
        g_ref[1] += _dot_tn(o_ref[...], dyb_ref[...])
        g_ref[2] += _dot_tn(mg_ref[...], do_ref[...].astype(BF16))

    row = pl.BlockSpec((tk, D), lambda k: (k, 0))
    return pl.pallas_call(
        body, name="small_wgrads", grid=(t // tk,), in_specs=[row] * 6,
        out_specs=pl.BlockSpec((3, D, D), lambda k: (0, 0, 0)), out_shape=SDS((3, D, D), F32),
        compiler_params=_cp("arbitrary", vmem=VMEM_BIG),
    )(y_c, d_ya, o, d_yb, merged, dout)


def _conv_bwd(d_yc, u_conv, conv_w, rider):
    t = d_yc.shape[0]
    tm = min(512, t)
    hb = tm // 16
    last_halo = t // 16 - 1
    n_steps = t // tm

    def body(dy_ref, v_ref, b_ref, c_ref, z_ref, hv_ref, hc_ref, ndy_ref, nb_ref, nz_ref, w_ref, du_ref, dw_ref):
        i = pl.program_id(0)

        @pl.when(i == 0)
        def _():
            dw_ref[...] = jnp.zeros_like(dw_ref)
        v, c = v_ref[...].astype(F32), c_ref[...].astype(F32)
        b, z = b_ref[...].astype(F32), z_ref[...].astype(F32)
        cv = c * v
        halo = jnp.where(i > 0, hc_ref[...].astype(F32) * hv_ref[...].astype(F32), 0.0)
        row = lax.broadcasted_iota(jnp.int32, (tm, 1), 0)
        s1 = jnp.where(row == 0, halo[15:16], pltpu.roll(cv, 1, 0))
        s2 = jnp.where(row == 0, halo[14:15], jnp.where(row == 1, halo[15:16], pltpu.roll(cv, 2, 0)))
        w0, w1, w2 = w_ref[0:1, :], w_ref[1:2, :], w_ref[2:3, :]
        conv = w0 * s2 + w1 * s1 + w2 * cv
        sig = _sigmoid(z)
        sz = z * sig
        dsz = sig * (1.0 + z * (1.0 - sig))
        dy = dy_ref[...].astype(F32)
        dconv = dy * b * sz
        nz = nz_ref[...].astype(F32)
        nxt = ndy_ref[...].astype(F32) * nb_ref[...].astype(F32) * (nz * _sigmoid(nz))
        nxt = jnp.where(i < n_steps - 1, nxt, 0.0)
        a1 = jnp.where(row == tm - 1, nxt[0:1], pltpu.roll(dconv, tm - 1, 0))
        a2 = jnp.where(row == tm - 2, nxt[0:1], jnp.where(row == tm - 1, nxt[1:2], pltpu.roll(dconv, tm - 2, 0)))
        dcv = w2 * dconv + w1 * a1 + w0 * a2
        du_ref[:, 0:D] = (dcv * c).astype(BF16)
        du_ref[:, D:2 * D] = (dy * conv * sz).astype(BF16)
        du_ref[:, 2 * D:3 * D] = (dcv * v).astype(BF16)
        du_ref[:, 3 * D:4 * D] = (dy * b * conv * dsz).astype(BF16)
        r8 = lax.broadcasted_iota(jnp.int32, (8, 1), 0)
        dw_ref[...] += jnp.where(r8 == 0, jnp.sum(dconv * s2, axis=0, keepdims=True),
                                 jnp.where(r8 == 1, jnp.sum(dconv * s1, axis=0, keepdims=True),
                                           jnp.where(r8 == 2, jnp.sum(dconv * cv, axis=0, keepdims=True), 0.0)))

    def col(k):
        return pl.BlockSpec((tm, D), lambda i: (i, k))

    def halo(k):
        return pl.BlockSpec((16, D), lambda i: (jnp.maximum(i * hb - 1, 0), k))

    def nxt(k):
        return pl.BlockSpec((16, D), lambda i: (jnp.minimum((i + 1) * hb, last_halo), k))

    return _call(
        body, rider, name="conv_bwd", grid=(t // tm,),
        in_specs=[col(0), col(0), col(1), col(2), col(3), halo(0), halo(2), nxt(0), nxt(1), nxt(3),
                  pl.BlockSpec((3, D), lambda i: (0, 0))],
        out_specs=[pl.BlockSpec((tm, 4 * D), lambda i: (i, 0)), pl.BlockSpec((8, D), lambda i: (0, 0))],
        out_shape=[SDS((t, 4 * D), BF16), SDS((8, D), F32)],
        args=(d_yc, u_conv, u_conv, u_conv, u_conv, u_conv, u_conv, d_yc, u_conv, u_conv, conv_w), vmem=VMEM_BIG)


def _attn_bwd(d_o, qs, kvx, u_za, lse, sinks, bias, rider):
    t = d_o.shape[0]
    nb = t // BLK

    def body(q_ref, kc_ref, kp_ref, za_ref, do_ref, lse_ref, sink_ref, bias_ref,
             dq_ref, dkv_ref, dza_ref, dsk_ref, carry_ref):
        n = pl.program_id(0)

        @pl.when(n == 0)
        def _():
            carry_ref[...] = jnp.zeros_like(carry_ref)
            dsk_ref[...] = jnp.zeros_like(dsk_ref)

        live = n < nb
        kvb = jnp.concatenate([kp_ref[...], kc_ref[...]], axis=0)
        bias_v = bias_ref[...]
        za = za_ref[...].astype(F32)
        sig = _sigmoid(za)
        dsa = sig * (1.0 + za * (1.0 - sig))
        do = jnp.where(live, do_ref[...].astype(F32), 0.0)
        dattn_f = do * (za * sig)
        dattn = dattn_f.astype(BF16)
        dattn_ln2 = (dattn_f * LN2).astype(BF16)
        lo_lanes = lax.broadcasted_iota(jnp.int32, (1, 128), 1) < HEAD
        dq_cols, attn_cols, dk_cols, dv_cols, dsk_rows = [], [], [], [], []
        for hk in range(N_KV):
            qpp = _pair_rows(q_ref, hk)
            dapp = _pair_rows(dattn, hk)
            dapp_ln2 = _pair_rows(dattn_ln2, hk)
            probs, dss, xk, xv = [], [], [], []
            for half in range(2):
                kx, vx = _kv_operands(kvb, hk, half)
                lse = lse_ref[0, 2 * hk + half:2 * hk + half + 1, :]
                prob = jnp.exp2(_dot_nt(kx, qpp) + bias_v - lse)
                psink = jnp.exp2(_sink_row(sink_ref, hk, half) - lse)
                tdp = prob * _dot_nt(vx, dapp_ln2)
                drow = jnp.sum(tdp, axis=0, keepdims=True)
                ds = (tdp - prob * drow).astype(BF16)
                prob_b = prob.astype(BF16)
                xk.append(_dot(ds, qpp))
                xv.append(_dot(prob_b, dapp))
                probs.append(prob_b)
                dss.append(ds)
                dsk_rows.append(-psink * drow * LOG2E)
            kcat = jnp.concatenate([kvb[:, 128 * hk:128 * hk + 128], kvb[:, 512 + 128 * hk:512 + 128 * hk + 128]], axis=0)
            vcat = jnp.concatenate([kvb[:, 1024 + 128 * hk:1024 + 128 * hk + 128],
                                    kvb[:, 1536 + 128 * hk:1536 + 128 * hk + 128]], axis=0)
            app = _dot_tn(jnp.concatenate(probs, axis=0), vcat)
            dqpp = _dot_tn(jnp.concatenate(dss, axis=0), kcat)
            dq_cols += [dqpp[:BLK], dqpp[BLK:]]
            attn_cols += [app[:BLK], app[BLK:]]
            dk_cols.append(jnp.where(lo_lanes, xk[0], xk[1]))
            dv_cols.append(jnp.where(lo_lanes, xv[0], xv[1]))

        @pl.when(live)
        def _():
            dq_ref[...] = jnp.concatenate(dq_cols, axis=1).astype(BF16)
            dza_ref[...] = (do * jnp.concatenate(attn_cols, axis=1) * dsa).astype(BF16)

        band = jnp.concatenate(dk_cols + dv_cols, axis=1)
        dkv_ref[...] = (band[:BLK] + carry_ref[...]).astype(BF16)
        carry_ref[...] = band[BLK:]
        dsk_ref[...] += jnp.broadcast_to(jnp.concatenate(dsk_rows, axis=1), (8, 2 * N_KV * 2 * BLK))

    cur = lambda n: jnp.minimum(n, nb - 1)
    prev = lambda n: jnp.maximum(n - 1, 0)
    return _call(
        body, rider, name="attn_bwd", grid=(nb + 1,),
        in_specs=[pl.BlockSpec((BLK, D), lambda n: (cur(n), 0)),
                  pl.BlockSpec((BLK, KVX), lambda n: (cur(n), 0)), pl.BlockSpec((BLK, KVX), lambda n: (prev(n), 0)),
                  pl.BlockSpec((BLK, D), lambda n: (cur(n), 0)), pl.BlockSpec((BLK, D), lambda n: (cur(n), 0)),
                  pl.BlockSpec((1, 8, 2 * BLK), lambda n: (cur(n), 0, 0)), pl.BlockSpec(memory_space=pltpu.SMEM),
                  pl.BlockSpec((None, 2 * BLK, 2 * BLK), lambda n: (jnp.minimum(n, 1), 0, 0))],
        out_specs=[pl.BlockSpec((BLK, D), lambda n: (cur(n), 0)), pl.BlockSpec((BLK, D), lambda n: (prev(n), 0)),
                   pl.BlockSpec((BLK, D), lambda n: (cur(n), 0)), pl.BlockSpec((8, 2 * D), lambda n: (0, 0))],
        out_shape=[SDS((t, D), BF16), SDS((t, D), BF16), SDS((t, D), BF16), SDS((8, 2 * D), F32)],
        scratch_shapes=[pltpu.VMEM((BLK, D), F32)],
        args=(qs, kvx, kvx, u_za, d_o, lse, sinks, bias), vmem=VMEM_BIG)


def _qkv_post(u_qkv, dqs, dkv, dza, qg_s, kg_t, rider):
    t = u_qkv.shape[0]
    tm = min(512, t)

    def norm_bwd(x, dy, g):
        r = lax.rsqrt(_head_sum(x * x) * (1.0 / HEAD) + EPS)
        xhat = x * r
        dxh = dy * g
        return r * (dxh - xhat * (_head_sum(dxh * xhat) * (1.0 / HEAD))), _fold8(dy * xhat)

    def body(u_ref, dq_ref, dkv_ref, dza_ref, qg_ref, kg_ref, du_ref, dqg_ref, dkg_ref):
        @pl.when(pl.program_id(0) == 0)
        def _():
            dqg_ref[...] = jnp.zeros_like(dqg_ref)
            dkg_ref[...] = jnp.zeros_like(dkg_ref)
        dq, gq = norm_bwd(u_ref[:, 0:D].astype(F32), dq_ref[...].astype(F32), qg_ref[...])
        fold = _fold_mat()
        dk, gk = norm_bwd(u_ref[:, D:D + 256].astype(F32), _dot(dkv_ref[:, 0:512], fold), kg_ref[...])
        du_ref[:, 0:D] = dq.astype(BF16)
        du_ref[:, D:D + 256] = dk.astype(BF16)
        du_ref[:, D + 256:D + 512] = _dot(dkv_ref[:, 512:1024], fold).astype(BF16)
        du_ref[:, D + 512:2 * D + 512] = dza_ref[...]
        dqg_ref[...] += gq
        dkg_ref[...] += gk

    row = pl.BlockSpec((tm, D), lambda i: (i, 0))
    return _call(
        body, rider, name="qkv_post", grid=(t // tm,),
        in_specs=[pl.BlockSpec((tm, 1536), lambda i: (i, 0)), row, row, row,
                  pl.BlockSpec((1, D), lambda i: (0, 0)), pl.BlockSpec((1, 256), lambda i: (0, 0))],
        out_specs=[pl.BlockSpec((tm, 2560), lambda i: (i, 0)), pl.BlockSpec((8, D), lambda i: (0, 0)),
                   pl.BlockSpec((8, 256), lambda i: (0, 0))],
        out_shape=[SDS((t, 2560), BF16), SDS((8, D), F32), SDS((8, 256), F32)],
        args=(u_qkv, dqs, dkv, dza, qg_s, kg_t), vmem=VMEM_BIG)


N_GRAN = N_IN // CB
DU_COLS = ((0, 4096), (4096, 6656), (6656, N_IN))


def _du_granule(j):
    return jnp.clip(j, 0, 7), jnp.clip(j - 8, 0, 4), jnp.clip(j - 13, 0, 3)


def _du_select(j, refs, fn):
    for ref, lo, hi in zip(refs, (0, 8, 13), (8, 13, 17)):
        @pl.when((j >= lo) & (j < hi))
        def _():
            fn(ref)


def _in_proj_bwd(du, w_full, x, g, dout, rider):
    t = du[0].shape[0]
    tn = min(256, t)

    def body(a0, a1, a2, w_hbm, x_ref, g_ref, do_ref, dx_ref, dg_ref, w_ref, sem):
        @pl.when(pl.program_id(0) == 0)
        def _():
            cp = pltpu.make_async_copy(w_hbm, w_ref, sem)
            cp.start()
            dg_ref[...] = jnp.zeros_like(dg_ref)
            cp.wait()
        acc = None
        for a_ref, (lo, hi) in zip((a0, a1, a2), DU_COLS):
            part = _dot_nt(w_ref[:, lo:hi], a_ref[...])
            acc = part if acc is None else acc + part
        dh = acc.T
        xv = x_ref[...]
        r = lax.rsqrt(jnp.mean(xv * xv, axis=-1, keepdims=True) + EPS)
        xhat = xv * r
        dg_ref[...] += _fold8(dh * xhat)
        dxh = dh * g_ref[...]
        dx_ref[...] = do_ref[...] + r * (dxh - xhat * jnp.mean(dxh * xhat, axis=-1, keepdims=True))

    row = pl.BlockSpec((tn, D), lambda i: (i, 0))
    return _call(
        body, rider, name="in_proj_bwd", grid=(t // tn,),
        in_specs=[pl.BlockSpec((tn, hi - lo), lambda i: (i, 0)) for lo, hi in DU_COLS]
        + [ANY, row, pl.BlockSpec((1, D), lambda i: (0, 0)), row],
        out_specs=[row, pl.BlockSpec((8, D), lambda i: (0, 0))], out_shape=[SDS((t, D), F32), SDS((8, D), F32)],
        scratch_shapes=[pltpu.VMEM((D, N_IN), BF16), pltpu.SemaphoreType.DMA(())],
        args=(*du, w_full, x, g, dout), vmem=VMEM_BIG)


def _in_proj_wgrad(ht, du, rider):
    t = ht.shape[1]
    tk = min(4096, t)
    n_k = t // tk

    def body(h_ref, b0, b1, b2, g_ref):
        j, k = pl.program_id(0), pl.program_id(1)

        if n_k > 1:
            @pl.when(k == 0)
            def _():
                g_ref[...] = jnp.zeros_like(g_ref)

        def add(b_ref):
            if n_k > 1:
                g_ref[...] += _dot(h_ref[...], b_ref[...])
            else:
                g_ref[...] = _dot(h_ref[...], b_ref[...])
        _du_select(j, (b0, b1, b2), add)

    seg = lambda q: pl.BlockSpec((tk, CB), lambda j, k: (k, _du_granule(j)[q]))
    (g,), got = _call(
        body, rider, name="in_proj_wgrad", grid=(N_GRAN, t // tk),
        in_specs=[pl.BlockSpec((D, tk), lambda j, k: (0, k)), seg(0), seg(1), seg(2)],
        out_specs=[pl.BlockSpec((D, CB), lambda j, k: (0, j))], out_shape=[SDS((D, N_IN), F32)],
        args=(ht, *du), vmem=VMEM_BIG)
    return g, got


def _swap_rider(g_in, g_sm):
    def copies(ins, outs, send, recv, base=0):
        x, y, c = _mesh_pos()
        cps = []
        for src, dst in zip(ins, outs):
            half = src.at[1 - c] if len(src.shape) == 3 else src.at[:, :, 1 - c]
            cps.append(_rcopy(half, dst, send, recv, base + len(cps), (x, y, 1 - c)))
        return cps

    arrays = [g for g in (g_in, g_sm) if g is not None]
    shapes = [SDS((512, N_IN), F32) if len(g.shape) == 3 else SDS((3, 4, 128, D), F32) for g in arrays]
    return _Rider(arrays, shapes, len(arrays), copies)


def _add_halves_in(cc_idx, g_in, r_in):
    def body(cc_ref, a_ref, b_ref, f_ref, h_ref):
        s = a_ref[...] + b_ref[...]
        h_ref[...] = s.astype(BF16)

        @pl.when(pl.program_id(1) == cc_ref[1])
        def _():
            f_ref[...] = s

    blk = pl.BlockSpec((256, SH_IN), lambda i, j, cc: (i, j))
    return pl.pallas_call(
        body, name="add_halves_in",
        grid_spec=pltpu.PrefetchScalarGridSpec(
            num_scalar_prefetch=1, grid=(2, 4),
            in_specs=[pl.BlockSpec((None, 256, SH_IN), lambda i, j, cc: (cc[0], i, j)), blk],
            out_specs=[pl.BlockSpec((256, SH_IN), lambda i, j, cc: (i, 0)), blk]),
        out_shape=[SDS((512, SH_IN), F32), SDS((512, N_IN), BF16)],
        compiler_params=_cp("arbitrary", "arbitrary", vmem=VMEM_BIG),
    )(cc_idx, g_in, r_in)


def _add_halves_sm(c_idx, g_sm, r_sm):
    def body(c_ref, a_ref, b_ref, f_ref, h_ref):
        s = a_ref[...] + b_ref[...]
        f_ref[...] = s
        h_ref[...] = s.astype(BF16)

    blk = pl.BlockSpec((1, 4, 128, D), lambda a, c: (a, 0, 0, 0))
    return pl.pallas_call(
        body, name="add_halves_sm",
        grid_spec=pltpu.PrefetchScalarGridSpec(
            num_scalar_prefetch=1, grid=(3,),
            in_specs=[pl.BlockSpec((1, 4, None, 128, D), lambda a, c: (a, 0, c[0], 0, 0)), blk], out_specs=[blk, blk]),
        out_shape=[SDS((3, 4, 128, D), F32), SDS((3, 4, 128, D), BF16)], compiler_params=_cp("parallel"),
    )(c_idx, g_sm, r_sm)


def _scatter_rider(h_in, h_sm):
    def copies(ins, outs, send, recv, base=0):
        x, y, c = _mesh_pos()
        cps = []
        for src, dst in zip(ins, outs):
            for k, chip in enumerate(_other_chips(x, y)):
                their = 2 * chip[0] + chip[1]
                part = src.at[:, pl.ds(pl.multiple_of(their * SH_IN, 128), SH_IN)] if len(src.shape) == 2 else src.at[:, their]
                cps.append(_rcopy(part, dst.at[k], send, recv, base + len(cps), (*chip, c)))
        return cps

    arrays = [h for h in (h_in, h_sm) if h is not None]
    shapes = [SDS((3, 512, SH_IN), BF16) if len(h.shape) == 2 else SDS((3, 3, 128, D), BF16) for h in arrays]
    return _Rider(arrays, shapes, 3 * len(arrays), copies)


def _ride_alone(rider, name):
    return _hosted_call(None, rider, name=name, grid=(), in_specs=[], out_specs=[], out_shape=[], args=())[1]


def _final_sum_in(f_in, r_in):
    def body(a_ref, r_ref, o_ref):
        o_ref[...] = a_ref[...] + r_ref[0].astype(F32) + r_ref[1].astype(F32) + r_ref[2].astype(F32)

    return pl.pallas_call(
        body, name="final_sum_in", grid=(4,),
        in_specs=[pl.BlockSpec((128, SH_IN), lambda i: (i, 0)), pl.BlockSpec((3, 128, SH_IN), lambda i: (0, i, 0))],
        out_specs=pl.BlockSpec((128, SH_IN), lambda i: (i, 0)),
        out_shape=SDS((512, SH_IN), F32), compiler_params=_cp("parallel"),
    )(f_in, r_in)


def _final_sum_sm(chip_idx, f_sm, r_sm):
    def body(j_ref, a_ref, r_ref, o_ref):
        o_ref[...] = a_ref[...] + r_ref[0].astype(F32) + r_ref[1].astype(F32) + r_ref[2].astype(F32)

    return pl.pallas_call(
        body, name="final_sum_sm",
        grid_spec=pltpu.PrefetchScalarGridSpec(
            num_scalar_prefetch=1, grid=(3,),
            in_specs=[pl.BlockSpec((1, None, 128, D), lambda a, j: (a, j[0], 0, 0)),
                      pl.BlockSpec((3, 1, 128, D), lambda a, j: (0, a, 0, 0))],
            out_specs=pl.BlockSpec((1, 128, D), lambda a, j: (a, 0, 0))),
        out_shape=SDS((3, 128, D), F32), compiler_params=_cp("parallel"),
    )(chip_idx, f_sm, r_sm)


def _join_halves(t_in, t_sm):
    n_cp = N_LAYERS * 4
    args, plan = [], []
    for l in range(N_LAYERS):
        if t_in[l] is not None:
            plan.append((l, 0, len(args)))
            args.append(t_in[l])
        plan += [(l, a, len(args)) for a in (1, 2, 3)]
        args.append(t_sm[l])

    def body(*refs):
        ins, outs = refs[:len(args)], refs[len(args):len(args) + 4]
        send, recv, loc_in, loc_out, stage_in, stage_sm = refs[len(args) + 4:]
        x, y, c = _mesh_pos()
        cps, own = [], []

        def place(l, a, half):
            rows = 512 if a == 0 else 128
            return outs[a].at[l, pl.ds(pl.multiple_of(half * rows, rows), rows), :]

        for s, (l, a, k) in enumerate(plan):
            src = ins[k] if a == 0 else ins[k].at[a - 1]
            own.append((src, place(l, a, c), min(a, 1)))
            cp = pltpu.make_async_remote_copy(src_ref=src, dst_ref=place(l, a, c), send_sem=send.at[s],
                                              recv_sem=recv.at[s], device_id=(x, y, 1 - c), device_id_type=MESH)
            cp.start()
            cps.append(cp)
        _staged_copies(own, (stage_in, stage_sm), loc_in, loc_out)
        for s, (l, a, k) in enumerate(plan):
            got = place(l, a, 1 - c)
            pltpu.make_async_remote_copy(src_ref=got, dst_ref=got, send_sem=send.at[s], recv_sem=recv.at[s],
                                         device_id=(x, y, 1 - c), device_id_type=MESH).wait_recv()
        for cp in cps:
            cp.wait_send()

    sm = SDS((N_LAYERS, SH_ROW, D), F32)
    return pl.pallas_call(
        body, name="join_halves", in_specs=[ANY] * len(args), out_specs=[ANY] * 4,
        out_shape=[SDS((N_LAYERS, D, SH_IN), F32), sm, sm, sm],
        scratch_shapes=[pltpu.SemaphoreType.DMA((n_cp,))] * 4
        + [pltpu.VMEM((2, 512, SH_IN), F32), pltpu.VMEM((2, 128, D), F32)],
        compiler_params=_cp(vmem=VMEM_BIG),
    )(*args)


def _adam_math(w, g, m, v):
    m = ADAM_B1 * m + (1.0 - ADAM_B1) * g
    v = ADAM_B2 * v + (1.0 - ADAM_B2) * (g * g)
    m_hat = m / (1.0 - ADAM_B1 ** ADAM_STEP)
    v_hat = v / (1.0 - ADAM_B2 ** ADAM_STEP)
    delta = -ADAM_LR * (m_hat / (jnp.sqrt(v_hat) + ADAM_EPS) + ADAM_WD * w)
    return delta, m, v


def _adamw_big(w, g, m, v, name):
    rows, cols = w.shape
    tr = 256

    def body(w_ref, g_ref, m_ref, v_ref, go_ref, d_ref, nm_ref, nv_ref):
        g = g_ref[...]
        go_ref[...] = g
        d_ref[...], nm_ref[...], nv_ref[...] = _adam_math(w_ref[...], g, m_ref[...], v_ref[...])

    blk = pl.BlockSpec((tr, cols), lambda i: (i, 0))
    return pl.pallas_call(
        body, name=name, grid=(rows // tr,), in_specs=[blk] * 4, out_specs=[blk] * 4,
        out_shape=[SDS((rows, cols), F32)] * 4, compiler_params=_cp("parallel", vmem=VMEM_BIG),
    )(w, g, m, v)


def _adamw_small(ws, gs, ms, vs):
    n = len(ws)

    def body(*refs):
        for k in range(n):
            w_ref, g_ref, m_ref, v_ref = (refs[q * n + k] for q in range(4))
            d, nm, nv = _adam_math(w_ref[...], g_ref[...], m_ref[...], v_ref[...])
            refs[4 * n + k][...] = d
            refs[5 * n + k][...] = nm
            refs[6 * n + k][...] = nv

    vm = pl.BlockSpec(memory_space=pltpu.VMEM)
    shapes = [SDS(w.shape, F32) for w in ws]
    res = pl.pallas_call(
        body, name="adamw_small", in_specs=[vm] * (4 * n), out_specs=[vm] * (3 * n), out_shape=shapes * 3,
    )(*ws, *gs, *ms, *vs)
    return res[:n], res[n:2 * n], res[2 * n:]


def _pad_rows(a, rows):
    flat = a.reshape(-1)
    return jnp.pad(flat, (0, rows * 128 - flat.shape[0])).reshape(rows, 128)


def kernel(x, norm_g, w_in, conv_w, q_norm_g, k_norm_g, sinks, w_conv_out, w_attn_out, gate_b, w_out, loss_target, m_norm_g, m_w_in, m_conv_w, m_q_norm_g, m_k_norm_g, m_sinks, m_w_conv_out, m_w_attn_out, m_gate_b, m_w_out, v_norm_g, v_w_in, v_conv_w, v_q_norm_g, v_k_norm_g, v_sinks, v_w_conv_out, v_w_attn_out, v_gate_b, v_w_out):
    xi, yi, ci = _mesh_pos()
    chip = 2 * xi + yi
    c_idx = jnp.reshape(ci, (1,)).astype(jnp.int32)
    chip_idx = jnp.reshape(chip, (1,)).astype(jnp.int32)
    cc_idx = jnp.stack([ci, chip]).astype(jnp.int32)
    t = x.shape[1]
    xs = [x.reshape(t, D)]
    tgt = loss_target.reshape(t, D)

    full_w = [list(_cast_weights(chip_idx, w_in, w_conv_out, w_attn_out, w_out, l)) for l in range(N_LAYERS)]
    conv32 = lax.dynamic_update_slice(jnp.zeros((32, D), F32), jnp.pad(conv_w.reshape(3 * N_LAYERS, SH_ROW), ((0, 20), (0, 0))),
                                      (0, chip * SH_ROW))
    qg_s = jnp.tile(q_norm_g, (1, N_Q)) * (SCALE * LOG2E)
    kg_t = jnp.tile(k_norm_g, (1, N_KV))
    bias = _band_bias()
    saved = []
    for l in range(N_LAYERS):
        nxt = full_w[l + 1] if l + 1 < N_LAYERS else None
        (h, ht), got = _rmsnorm_fwd(xs[l], norm_g[l:l + 1], _gather_rider([full_w[0][0], conv32], "N") if l == 0 else None)
        if l == 0:
            got = _ride_alone(_gather_rider(got, "FB1"), "gather_first_forward")
            full_w[0][0], conv32 = _ride_alone(_gather_rider(got, "B2"), "gather_first_d2d")
            conv_full = conv32[:3 * N_LAYERS].reshape(N_LAYERS, 3, D)
        (u_conv, u_qkv, u_za, u_gl), got = _in_proj(h, full_w[l][0], _gather_rider(nxt, "N") if nxt else None)
        if nxt:
            nxt[0], nxt[1] = got
        (qs, kvx), got = _qkv_prep(u_qkv, qg_s[l:l + 1], kg_t[l:l + 1],
                                   _gather_rider(full_w[0][1:], "N") if l == 0 else None)
        y_c, got = _conv_fwd(u_conv, conv_full[l], _gather_rider(got, "F") if l == 0 else None)
        o, lse, got = _attn_fwd(qs, kvx, u_za, sinks[l:l + 1], bias, _merge_riders(
            _gather_rider(nxt, "FB1") if nxt else None, _gather_rider(got, "B") if l == 0 else None))
        if nxt:
            nxt[0], nxt[1] = got[:2]
        if l == 0:
            full_w[0][1] = got[-1]
        (x_next, y_a, y_b, merged), got = _out_proj_fwd(xs[l], y_c, o, u_gl, gate_b[l:l + 1], full_w[l][1],
                                                        _gather_rider(nxt, "B2") if nxt else None)
        if nxt:
            nxt[0], nxt[1] = got
        xs.append(x_next)
        saved.append((ht, u_conv, u_qkv, u_za, u_gl, y_c, o, y_a, y_b, merged, qs, kvx, lse))

    dout, sq = _loss_head(xs[N_LAYERS], tgt)

    small, t_in, t_sm = [None] * N_LAYERS, [None] * N_LAYERS, [None] * N_LAYERS
    halves = None

    for l in reversed(range(N_LAYERS)):
        w_full, w_sm = full_w[l]
        last = l == 0
        ht, u_conv, u_qkv, u_za, u_gl, y_c, o, y_a, y_b, merged, qs, kvx, lse = saved[l]
        (d_ya, d_yb, du_gl, d_yc, d_o, dgb), _ = _out_proj_bwd(dout, y_a, y_b, u_gl, gate_b[l:l + 1], w_sm, None)
        g_sm = _small_wgrads(y_c, d_ya, o, d_yb, merged, dout).reshape(3, 4, 2, 128, D)
        (du_conv, dcw), got = _conv_bwd(d_yc, u_conv, conv_full[l], _swap_rider(None, g_sm) if last else None)
        if last:
            f_sm0, h_sm0 = _add_halves_sm(c_idx, g_sm, got[0])
        (dqs, dkv, dza, dsk), got = _attn_bwd(d_o, qs, kvx, u_za, lse, sinks[l:l + 1], bias,
                                              _scatter_rider(halves[1], None) if halves else None)
        if halves:
            t_in[l + 1] = _final_sum_in(halves[0], got[0])
        dsk = jnp.sum(dsk[0].reshape(N_KV, 2, 2, BLK), axis=-1).transpose(0, 2, 1).reshape(N_Q)
        (du_attn, dqg, dkg), _ = _qkv_post(u_qkv, dqs, dkv, dza, qg_s[l:l + 1], kg_t[l:l + 1], None)
        du = (du_conv, du_attn, du_gl)
        g_in, got = _in_proj_wgrad(ht, du, _merge_riders(
            _scatter_rider(None, halves[3]) if halves else None, _scatter_rider(None, h_sm0) if last else None))
        g_in = g_in.reshape(2, 512, N_IN)
        if halves:
            t_sm[l + 1] = _final_sum_sm(chip_idx, halves[2], got[0])
        if last:
            t_sm[0] = _final_sum_sm(chip_idx, f_sm0, got[-1])
        if last:
            f_in0, h_in0 = _add_halves_in(cc_idx, g_in, _ride_alone(_swap_rider(g_in, None), "swap_last")[0])
        (dout, dng), got = _in_proj_bwd(du, w_full, xs[l], norm_g[l:l + 1], dout,
                                        _scatter_rider(h_in0, None) if last else _swap_rider(g_in, g_sm))
        if last:
            t_in[0] = _final_sum_in(f_in0, got[0])
        else:
            halves = _add_halves_in(cc_idx, g_in, got[0]) + _add_halves_sm(c_idx, g_sm, got[1])
        small[l] = (jnp.sum(dng, axis=0), (SCALE * LOG2E) * jnp.sum(dqg.reshape(8 * N_Q, HEAD), axis=0),
                    jnp.sum(dkg.reshape(8 * N_KV, HEAD), axis=0), dsk, jnp.sum(dgb, axis=0), dcw[:3])
    grad_x = dout.reshape(1, t, D)

    stack = lambda k: jnp.stack([small[l][k] for l in range(N_LAYERS)])
    pack = jnp.concatenate([_pad_rows(stack(0), 32), _pad_rows(stack(1), 8), _pad_rows(stack(2), 8),
                            _pad_rows(stack(3), 8), _pad_rows(stack(4), 64), _pad_rows(stack(5), 96),
                            _pad_rows(jnp.sum(sq) * (0.5 / D), 8)], axis=0)
    red = _allreduce_small(pack)
    loss = red[216, 0]
    g_norm_g = red[0:32].reshape(N_LAYERS, D)
    g_q_norm_g = red[32:40].reshape(-1)[:N_LAYERS * HEAD].reshape(N_LAYERS, HEAD)
    g_k_norm_g = red[40:48].reshape(-1)[:N_LAYERS * HEAD].reshape(N_LAYERS, HEAD)
    g_sinks = red[48:56].reshape(-1)[:N_LAYERS * N_Q].reshape(N_LAYERS, N_Q)
    g_gate_b = red[56:120].reshape(N_LAYERS, 2 * D)
    g_conv_full = red[120:216].reshape(N_LAYERS, 3, D)
    g_conv_w = lax.dynamic_slice(g_conv_full, (0, 0, chip * SH_ROW), (N_LAYERS, 3, SH_ROW))

    g_w_in, g_w_co, g_w_ao, g_w_out = _join_halves(t_in, t_sm)

    r_in = N_LAYERS * D
    g_w_in, d_in, nm_in, nv_in = (a.reshape(N_LAYERS, D, SH_IN) for a in _adamw_big(
        w_in.reshape(r_in, SH_IN), g_w_in.reshape(r_in, SH_IN), m_w_in.reshape(r_in, SH_IN),
        v_w_in.reshape(r_in, SH_IN), "adamw_w_in"))
    r_sm = N_LAYERS * SH_ROW
    big = {}
    for nm, w, g, m, v in (("co", w_conv_out, g_w_co, m_w_conv_out, v_w_conv_out),
                           ("ao", w_attn_out, g_w_ao, m_w_attn_out, v_w_attn_out),
                           ("out", w_out, g_w_out, m_w_out, v_w_out)):
        big[nm] = tuple(a.reshape(N_LAYERS, SH_ROW, D) for a in _adamw_big(
            w.reshape(r_sm, D), g.reshape(r_sm, D), m.reshape(r_sm, D), v.reshape(r_sm, D), "adamw_w_small"))
    g_w_co, g_w_ao, g_w_out = big["co"][0], big["ao"][0], big["out"][0]
    sm_w = [norm_g, conv_w, q_norm_g, k_norm_g, sinks, gate_b]
    sm_g = [g_norm_g, g_conv_w, g_q_norm_g, g_k_norm_g, g_sinks, g_gate_b]
    sm_m = [m_norm_g, m_conv_w, m_q_norm_g, m_k_norm_g, m_sinks, m_gate_b]
    sm_v = [v_norm_g, v_conv_w, v_q_norm_g, v_k_norm_g, v_sinks, v_gate_b]
    sd, snm, snv = _adamw_small(sm_w, sm_g, sm_m, sm_v)

    def order(norm, w_in_, conv, qn, kn, sk, co, ao, gb, wo):
        return [norm, w_in_, conv, qn, kn, sk, co, ao, gb, wo]

    grads = order(g_norm_g, g_w_in, g_conv_w, g_q_norm_g, g_k_norm_g, g_sinks, g_w_co, g_w_ao, g_gate_b, g_w_out)
    deltas = order(sd[0], d_in, sd[1], sd[2], sd[3], sd[4], big["co"][1], big["ao"][1], sd[5], big["out"][1])
    new_m = order(snm[0], nm_in, snm[1], snm[2], snm[3], snm[4], big["co"][2], big["ao"][2], snm[5], big["out"][2])
    new_v = order(snv[0], nv_in, snv[1], snv[2], snv[3], snv[4], big["co"][3], big["ao"][3], snv[5], big["out"][3])
    return (loss, grad_x, *grads, *deltas, *new_m, *new_v)
```

```python
import functools

import jax
import jax.numpy as jnp
from jax import lax
from jax.experimental import pallas as pl
from jax.experimental.pallas import tpu as pltpu

F32, BF16 = jnp.float32, jnp.bfloat16
SDS = jax.ShapeDtypeStruct
MESH = pl.DeviceIdType.MESH
ANY = pl.BlockSpec(memory_space=pl.ANY)

D = 1024
N_IN = 8704
N_LAYERS = 4
N_Q, N_KV, HEAD = 16, 4, 64
GROUP = N_Q // N_KV
BLK = 128
EPS = 1e-6
NEG = -1e30
SCALE = HEAD ** -0.5
SH_IN = N_IN // 4
SH_ROW = D // 4
CB = 512
VMEM_BIG = 56 * 1024 * 1024

ADAM_LR, ADAM_B1, ADAM_B2, ADAM_EPS, ADAM_WD, ADAM_STEP = 0.001, 0.9, 0.999, 1e-08, 0.01, 10


def _cp(*sem, vmem=None):
    return pltpu.CompilerParams(dimension_semantics=sem if sem else None, vmem_limit_bytes=vmem)


def _sigmoid(z):
    return 1.0 / (1.0 + jnp.exp(-z))


def _dot(a, b):
    return jnp.dot(a, b, preferred_element_type=F32)


def _dot_nt(a, b):
    return lax.dot_general(a, b, (((1,), (1,)), ((), ())), preferred_element_type=F32)


def _dot_tn(a, b):
    return lax.dot_general(a, b, (((0,), (0,)), ((), ())), preferred_element_type=F32)


def _fold8(v):
    return jnp.sum(v.reshape(v.shape[0] // 8, 8, v.shape[1]), axis=0)


def _cast_w_in(chip_idx, w, layer):
    def body(j_ref, i_ref, o_ref):
        o_ref[...] = i_ref[...].astype(BF16)

    return pl.pallas_call(
        body, name="cast_w_in",
        grid_spec=pltpu.PrefetchScalarGridSpec(
            num_scalar_prefetch=1, grid=(2,),
            in_specs=[pl.BlockSpec((None, 512, SH_IN), lambda i, j: (layer, i, 0))],
            out_specs=pl.BlockSpec((512, SH_IN), lambda i, j: (i, j[0]))),
        out_shape=SDS((D, N_IN), BF16), compiler_params=_cp("parallel"),
    )(chip_idx, w)


def _cast_w_small(chip_idx, a, b, c, layer):
    def body(j_ref, a_ref, b_ref, c_ref, o_ref):
        o_ref[0] = a_ref[...].astype(BF16)
        o_ref[1] = b_ref[...].astype(BF16)
        o_ref[2] = c_ref[...].astype(BF16)

    spec = pl.BlockSpec((None, SH_ROW, D), lambda i, j: (layer, 0, 0))
    return pl.pallas_call(
        body, name="cast_w_small",
        grid_spec=pltpu.PrefetchScalarGridSpec(
            num_scalar_prefetch=1, grid=(1,), in_specs=[spec, spec, spec],
            out_specs=pl.BlockSpec((3, SH_ROW, D), lambda i, j: (0, j[0], 0))),
        out_shape=SDS((3, D, D), BF16), compiler_params=_cp("parallel"),
    )(chip_idx, a, b, c)


def _mesh_pos():
    return lax.axis_index("x"), lax.axis_index("y"), lax.axis_index("c")


def _other_chips(x, y):
    return [(1 - x, y), (x, 1 - y), (1 - x, 1 - y)]


class _Rider:
    def __init__(self, ins, out_shape, n, copies, aliases=()):
        self.ins, self.out_shape, self.n, self.copies, self.aliases = list(ins), list(out_shape), n, copies, aliases


def _merge_riders(*riders):
    riders = [r for r in riders if r is not None]
    if len(riders) < 2:
        return riders[0] if riders else None

    def copies(ins, outs, send, recv, base=0):
        cps, i0, o0 = [], 0, 0
        for r in riders:
            cps += r.copies(ins[i0:i0 + len(r.ins)], outs[o0:o0 + len(r.out_shape)], send, recv, base + len(cps))
            i0, o0 = i0 + len(r.ins), o0 + len(r.out_shape)
        return cps

    aliases, i0, o0 = [], 0, 0
    for r in riders:
        aliases += [(i0 + i, o0 + o) for i, o in r.aliases]
        i0, o0 = i0 + len(r.ins), o0 + len(r.out_shape)
    return _Rider(sum((r.ins for r in riders), []), sum((r.out_shape for r in riders), []),
                  sum(r.n for r in riders), copies, tuple(aliases))


def _rcopy(src, dst, send, recv, k, to):
    return pltpu.make_async_remote_copy(src_ref=src, dst_ref=dst, send_sem=send.at[k], recv_sem=recv.at[k],
                                        device_id=to, device_id_type=MESH)


def _hosted_call(body, rider, *, name, grid, in_specs, out_specs, out_shape, args, scratch_shapes=(), vmem=None):
    n_in, n_out, n_scr = len(in_specs), len(out_specs), len(scratch_shapes)
    r_in, r_out = len(rider.ins), len(rider.out_shape)

    def full_body(*refs):
        host_in, rid_in = refs[:n_in], refs[n_in:n_in + r_in]
        o0 = n_in + r_in
        host_out, rid_out = refs[o0:o0 + n_out], refs[o0 + n_out:o0 + n_out + r_out]
        s0 = o0 + n_out + r_out
        host_scr, (send, recv) = refs[s0:s0 + n_scr], refs[s0 + n_scr:]
        if body is None:
            cps = rider.copies(rid_in, rid_out, send, recv)
            for cp in cps:
                cp.start()
            for cp in cps:
                cp.wait()
            return
        ids = [pl.program_id(a) for a in range(len(grid))]
        first = functools.reduce(lambda p, q: p & q, [i == 0 for i in ids])
        last = functools.reduce(lambda p, q: p & q, [i == g - 1 for i, g in zip(ids, grid)])

        @pl.when(first)
        def _():
            for cp in rider.copies(rid_in, rid_out, send, recv):
                cp.start()

        body(*host_in, *host_out, *host_scr)

        @pl.when(last)
        def _():
            for cp in rider.copies(rid_in, rid_out, send, recv):
                cp.wait()

    res = pl.pallas_call(
        full_body, name=name, grid=grid if body is not None else (),
        in_specs=list(in_specs) + [ANY] * r_in, out_specs=list(out_specs) + [ANY] * r_out,
        out_shape=list(out_shape) + rider.out_shape,
        scratch_shapes=list(scratch_shapes) + [pltpu.SemaphoreType.DMA((rider.n,))] * 2,
        input_output_aliases={n_in + i: n_out + o for i, o in rider.aliases},
        compiler_params=_cp(*(("arbitrary",) * len(grid) if body is not None else ()), vmem=vmem),
    )(*args, *rider.ins)
    return res[:n_out], res[n_out:]


def _call(body, rider, **kw):
    if rider is not None:
        return _hosted_call(body, rider, **kw)
    res = pl.pallas_call(
        body, name=kw["name"], grid=kw["grid"], in_specs=list(kw["in_specs"]), out_specs=list(kw["out_specs"]),
        out_shape=list(kw["out_shape"]), scratch_shapes=list(kw.get("scratch_shapes", ())),
        compiler_params=_cp(*(("arbitrary",) * len(kw["grid"])), vmem=kw.get("vmem")),
    )(*kw["args"])
    return res, []


def _gather_rider(arrays, stage):
    def region(full, whose, c, sub):
        if len(full.shape) == 2:
            rows, cols = full.shape[0] // 2, full.shape[1] // 4
            first, n = (c * rows, rows) if sub is None else (c * rows + sub * (rows // 2), rows // 2)
            return full.at[pl.ds(pl.multiple_of(first, n), n), pl.ds(pl.multiple_of(whose * cols, 128), cols)]
        first, n = (whose * SH_ROW + c * 128, 128) if sub is None else (whose * SH_ROW + c * 128 + sub * 64, 64)
        return full.at[:, pl.ds(pl.multiple_of(first, n), n), :]

    def copies(ins, outs, send, recv, base=0):
        x, y, c = _mesh_pos()
        nbr_x, nbr_y = (1 - x, y), (x, 1 - y)
        cps = []
        for full in outs:
            plan = []
            if stage == "N":
                plan = [(region(full, 2 * x + y, c, None), (*nbr_x, c)), (region(full, 2 * x + y, c, None), (*nbr_y, c))]
            if stage in ("F", "FB1"):
                plan = [(region(full, 2 * nbr_x[0] + nbr_x[1], c, 0), (*nbr_y, c)),
                        (region(full, 2 * nbr_y[0] + nbr_y[1], c, 1), (*nbr_x, c))]
            if stage in ("B", "FB1", "B2"):
                chips = {"B": _other_chips(x, y), "FB1": [nbr_x, nbr_y], "B2": [(1 - x, 1 - y)]}[stage]
                plan += [(region(full, 2 * chip[0] + chip[1], c, None), (x, y, 1 - c)) for chip in chips]
            for reg, to in plan:
                cps.append(_rcopy(reg, reg, send, recv, base + len(cps), to))
        return cps

    per_array = {"N": 2, "F": 2, "B": 3, "FB1": 4, "B2": 1}[stage]
    return _Rider(arrays, [SDS(v.shape, v.dtype) for v in arrays], per_array * len(arrays), copies,
                  aliases=tuple((i, i) for i in range(len(arrays))))


def _staged_copies(copies, stages, sem_in, sem_out):
    busy, count = {}, {}
    for idx, (src, dst, kind) in enumerate(copies):
        slot = count.get(kind, 0) % 2
        count[kind] = count.get(kind, 0) + 1
        if (kind, slot) in busy:
            busy.pop((kind, slot)).wait()
        buf = stages[kind].at[slot]
        cin = pltpu.make_async_copy(src, buf, sem_in.at[idx])
        cin.start()
        cin.wait()
        cout = pltpu.make_async_copy(buf, dst, sem_out.at[idx])
        cout.start()
        busy[(kind, slot)] = cout
    for cp in busy.values():
        cp.wait()


def _allreduce_small(pack):
    rows = pack.shape[0]

    def body(p_ref, o_ref, buf, send, recv):
        x, y, c = _mesh_pos()
        me = 4 * x + 2 * y + c
        sends = []
        for r in range(1, 8):
            to = (x if not (r & 4) else 1 - x, y if not (r & 2) else 1 - y, c if not (r & 1) else 1 - c)
            cp = pltpu.make_async_remote_copy(src_ref=p_ref, dst_ref=buf.at[me], send_sem=send.at[r - 1],
                                              recv_sem=recv.at[r - 1], device_id=to, device_id_type=MESH)
            cp.start()
            sends.append(cp)
        buf[me] = p_ref[...]
        for r in range(1, 8):
            frm = (4 * x + 2 * y + c) ^ r
            pltpu.make_async_remote_copy(src_ref=p_ref, dst_ref=buf.at[frm], send_sem=send.at[r - 1],
                                         recv_sem=recv.at[r - 1], device_id=(x, y, c), device_id_type=MESH).wait_recv()
        acc = buf[0]
        for d in range(1, 8):
            acc = acc + buf[d]
        o_ref[...] = acc
        for cp in sends:
            cp.wait_send()

    vm = pl.BlockSpec(memory_space=pltpu.VMEM)
    return pl.pallas_call(
        body, name="allreduce_small", in_specs=[vm], out_specs=vm, out_shape=SDS((rows, 128), F32),
        scratch_shapes=[pltpu.VMEM((8, rows, 128), F32), pltpu.SemaphoreType.DMA((7,)), pltpu.SemaphoreType.DMA((7,))],
    )(pack)


def _rmsnorm_fwd(x, g, rider):
    t = x.shape[0]
    tm = min(1024, t)

    def body(x_ref, g_ref, h_ref, ht_ref):
        xv = x_ref[...]
        r = lax.rsqrt(jnp.mean(xv * xv, axis=-1, keepdims=True) + EPS)
        h = xv * r * g_ref[...]
        h_ref[...] = h.astype(BF16)
        ht_ref[...] = h.T.astype(BF16)

    return _call(
        body, rider, name="rmsnorm_fwd", grid=(t // tm,),
        in_specs=[pl.BlockSpec((tm, D), lambda i: (i, 0)), pl.BlockSpec((1, D), lambda i: (0, 0))],
        out_specs=[pl.BlockSpec((tm, D), lambda i: (i, 0)), pl.BlockSpec((D, tm), lambda i: (0, i))],
        out_shape=[SDS((t, D), BF16), SDS((D, t), BF16)], args=(x, g), vmem=VMEM_BIG)


FWD_SEGS = ((0, 8), (8, 3), (11, 2), (13, 4))


def _in_proj(h, w_full, rider):
    t = h.shape[0]
    tm = min(2048, t)

    def body(a_ref, b_ref, *o_refs):
        j = pl.program_id(1)
        for o_ref, (off, nblk) in zip(o_refs, FWD_SEGS):
            @pl.when((j >= off) & (j < off + nblk))
            def _():
                o_ref[...] = _dot(a_ref[...], b_ref[...]).astype(BF16)

    def out(seg):
        off, nblk = seg
        return pl.BlockSpec((tm, CB), lambda i, j: (i, jnp.clip(j - off, 0, nblk - 1)))

    res, got = _call(
        body, rider, name="in_proj", grid=(t // tm, N_IN // CB),
        in_specs=[pl.BlockSpec((tm, D), lambda i, j: (i, 0)), pl.BlockSpec((D, CB), lambda i, j: (0, j))],
        out_specs=[out(s) for s in FWD_SEGS], out_shape=[SDS((t, s[1] * CB), BF16) for s in FWD_SEGS],
        args=(h, w_full), vmem=VMEM_BIG)
    return res, got


def _conv_fwd(u_conv, conv_w, rider):
    t = u_conv.shape[0]
    tm = min(512, t)
    hb = tm // 16

    def body(v_ref, b_ref, c_ref, z_ref, hv_ref, hc_ref, w_ref, y_ref):
        i = pl.program_id(0)
        cv = c_ref[...].astype(F32) * v_ref[...].astype(F32)
        halo = hc_ref[...].astype(F32) * hv_ref[...].astype(F32)
        halo = jnp.where(i > 0, halo, 0.0)
        row = lax.broadcasted_iota(jnp.int32, (tm, 1), 0)
        s1 = jnp.where(row == 0, halo[15:16], pltpu.roll(cv, 1, 0))
        s2 = jnp.where(row == 0, halo[14:15], jnp.where(row == 1, halo[15:16], pltpu.roll(cv, 2, 0)))
        conv = w_ref[0:1, :] * s2 + w_ref[1:2, :] * s1 + w_ref[2:3, :] * cv
        z = z_ref[...].astype(F32)
        y_ref[...] = (b_ref[...].astype(F32) * conv * (z * _sigmoid(z))).astype(BF16)

    def col(k):
        return pl.BlockSpec((tm, D), lambda i: (i, k))

    def halo(k):
        return pl.BlockSpec((16, D), lambda i: (jnp.maximum(i * hb - 1, 0), k))

    (y_c,), got = _call(
        body, rider, name="conv_fwd", grid=(t // tm,),
        in_specs=[col(0), col(1), col(2), col(3), halo(0), halo(2), pl.BlockSpec((3, D), lambda i: (0, 0))],
        out_specs=[pl.BlockSpec((tm, D), lambda i: (i, 0))], out_shape=[SDS((t, D), BF16)],
        args=(u_conv, u_conv, u_conv, u_conv, u_conv, u_conv, conv_w), vmem=VMEM_BIG)
    return y_c, got


KVX = 4 * N_KV * 128


def _iota2(shape):
    return lax.broadcasted_iota(jnp.int32, shape, 0), lax.broadcasted_iota(jnp.int32, shape, 1)


def _head_sum(v):
    r, c = _iota2((128, 128))
    ones = ((r >> 6) == (c >> 6)).astype(BF16)
    hi = v.astype(BF16)
    lo = (v - hi.astype(F32)).astype(BF16)
    return jnp.concatenate([_dot(hi[:, g:g + 128], ones) + _dot(lo[:, g:g + 128], ones)
                            for g in range(0, v.shape[1], 128)], axis=1)


def _expand_mats():
    r, c = _iota2((N_KV * HEAD, N_KV * 128))
    base = ((r >> 6) << 7) + (r & 63)
    return (c == base).astype(BF16), (c == base + 64).astype(BF16)


def _fold_mat():
    r, c = _iota2((N_KV * 128, N_KV * HEAD))
    return (((r >> 7) == (c >> 6)) & ((r & 63) == (c & 63))).astype(BF16)


def _qkv_prep(u_qkv, qg_s, kg_t, rider):
    t = u_qkv.shape[0]
    tm = min(1024, t)

    def body(u_ref, qg_ref, kg_ref, qs_ref, kvx_ref):
        q = u_ref[:, 0:D].astype(F32)
        rq = lax.rsqrt(_head_sum(q * q) * (1.0 / HEAD) + EPS)
        qs_ref[...] = (q * rq * qg_ref[...]).astype(BF16)
        k = u_ref[:, D:D + 256].astype(F32)
        rk = lax.rsqrt(_head_sum(k * k) * (1.0 / HEAD) + EPS)
        kn = (k * rk * kg_ref[...]).astype(BF16)
        v = u_ref[:, D + 256:D + 512]
        e_lo, e_hi = _expand_mats()
        kvx_ref[:, 0:512] = _dot(kn, e_lo).astype(BF16)
        kvx_ref[:, 512:1024] = _dot(kn, e_hi).astype(BF16)
        kvx_ref[:, 1024:1536] = _dot(v, e_lo).astype(BF16)
        kvx_ref[:, 1536:2048] = _dot(v, e_hi).astype(BF16)

    return _call(
        body, rider, name="qkv_prep", grid=(t // tm,),
        in_specs=[pl.BlockSpec((tm, 1536), lambda i: (i, 0)), pl.BlockSpec((1, D), lambda i: (0, 0)),
                  pl.BlockSpec((1, 256), lambda i: (0, 0))],
        out_specs=[pl.BlockSpec((tm, D), lambda i: (i, 0)), pl.BlockSpec((tm, KVX), lambda i: (i, 0))],
        out_shape=[SDS((t, D), BF16), SDS((t, KVX), BF16)], args=(u_qkv, qg_s, kg_t), vmem=VMEM_BIG)


def _band_bias():
    j, r = _iota2((2 * BLK, 2 * BLK))
    diff = (r & (BLK - 1)) - j + BLK
    band = (diff >= 0) & (diff < BLK)
    return jnp.stack([jnp.where(band & (j >= BLK), 0.0, NEG), jnp.where(band, 0.0, NEG)]).astype(F32)


def _pair_rows(ref_or_val, hk):
    return jnp.concatenate([ref_or_val[:, 256 * hk:256 * hk + 128], ref_or_val[:, 256 * hk + 128:256 * hk + 256]], axis=0)


LOG2E, LN2 = 1.4426950408889634, 0.6931471805599453


def _sink_row(sink_ref, hk, half):
    return jnp.concatenate([jnp.full((1, BLK), sink_ref[0, GROUP * hk + half] * LOG2E, F32),
                            jnp.full((1, BLK), sink_ref[0, GROUP * hk + 2 + half] * LOG2E, F32)], axis=1)


def _kv_operands(kvb, hk, half):
    return (kvb[:, 512 * half + 128 * hk:512 * half + 128 * hk + 128],
            kvb[:, 1024 + 512 * half + 128 * hk:1024 + 512 * half + 128 * hk + 128])


def _attn_fwd(qs, kvx, u_za, sinks, bias, rider):
    t = qs.shape[0]
    nb = t // BLK

    def body(q_ref, kc_ref, kp_ref, za_ref, sink_ref, bias_ref, o_ref, lse_ref):
        kvb = jnp.concatenate([kp_ref[...], kc_ref[...]], axis=0)
        bias_v = bias_ref[...]
        key0 = lax.broadcasted_iota(jnp.int32, (2 * BLK, 1), 0) == 0
        ones = jnp.ones((2 * BLK, 128), BF16)
        cols = []
        for hk in range(N_KV):
            qpp = _pair_rows(q_ref, hk)
            opp = None
            for half in range(2):
                kx, vx = _kv_operands(kvb, hk, half)
                s = _dot_nt(kx, qpp) + bias_v
                sink = _sink_row(sink_ref, hk, half)
                m = jnp.maximum(jnp.max(s, axis=0, keepdims=True), sink)
                p = jnp.exp2(s - m)
                es = jnp.exp2(sink - m)
                lse_ref[0, 2 * hk + half:2 * hk + half + 1, :] = m + jnp.log(jnp.sum(p, axis=0, keepdims=True) + es) * LOG2E
                pe = jnp.where(key0, es, p).astype(BF16)
                rhs = jnp.concatenate([jnp.where(key0, jnp.zeros_like(vx), vx), ones], axis=1)
                nd = _dot_tn(pe, rhs)
                o = nd[:, :128] * (1.0 / nd[:, 128:])
                opp = o if opp is None else opp + o
            cols += [opp[:BLK], opp[BLK:]]
        za = za_ref[...].astype(F32)
        o_ref[...] = (jnp.concatenate(cols, axis=1) * (za * _sigmoid(za))).astype(BF16)

    prev = lambda n: jnp.maximum(n - 1, 0)
    (o, lse), got = _call(
        body, rider, name="attn_fwd", grid=(nb,),
        in_specs=[pl.BlockSpec((BLK, D), lambda n: (n, 0)),
                  pl.BlockSpec((BLK, KVX), lambda n: (n, 0)), pl.BlockSpec((BLK, KVX), lambda n: (prev(n), 0)),
                  pl.BlockSpec((BLK, D), lambda n: (n, 0)), pl.BlockSpec(memory_space=pltpu.SMEM),
                  pl.BlockSpec((None, 2 * BLK, 2 * BLK), lambda n: (jnp.minimum(n, 1), 0, 0))],
        out_specs=[pl.BlockSpec((BLK, D), lambda n: (n, 0)), pl.BlockSpec((1, 8, 2 * BLK), lambda n: (n, 0, 0))],
        out_shape=[SDS((t, D), BF16), SDS((nb, 8, 2 * BLK), F32)],
        args=(qs, kvx, kvx, u_za, sinks, bias), vmem=VMEM_BIG)
    return o, lse, got


def _out_proj_fwd(x, y_c, o, u_gl, gate_b, w_sm, rider):
    t = x.shape[0]
    tm = min(512, t)

    def body(x_ref, yc_ref, o_ref, gla_ref, glb_ref, gb_ref, wco_ref, wao_ref, wout_ref,
             xn_ref, ya_ref, yb_ref, mg_ref):
        ya = _dot(yc_ref[...], wco_ref[...])
        yb = _dot(o_ref[...], wao_ref[...])
        gb = gb_ref[...]
        ga_ = _sigmoid(gla_ref[...].astype(F32) + gb[:, :D])
        gb_ = _sigmoid(glb_ref[...].astype(F32) + gb[:, D:])
        merged = (ga_ * ya + gb_ * yb).astype(BF16)
        ya_ref[...] = ya.astype(BF16)
        yb_ref[...] = yb.astype(BF16)
        mg_ref[...] = merged
        xn_ref[...] = x_ref[...] + _dot(merged, wout_ref[...])

    row = pl.BlockSpec((tm, D), lambda i: (i, 0))
    wspec = lambda a: pl.BlockSpec((None, D, D), lambda i: (a, 0, 0))
    return _call(
        body, rider, name="out_proj_fwd", grid=(t // tm,),
        in_specs=[row, row, row, pl.BlockSpec((tm, D), lambda i: (i, 0)), pl.BlockSpec((tm, D), lambda i: (i, 1)),
                  pl.BlockSpec((1, 2 * D), lambda i: (0, 0)), wspec(0), wspec(1), wspec(2)],
        out_specs=[row, row, row, row],
        out_shape=[SDS((t, D), F32), SDS((t, D), BF16), SDS((t, D), BF16), SDS((t, D), BF16)],
        args=(x, y_c, o, u_gl, u_gl, gate_b, w_sm, w_sm, w_sm), vmem=VMEM_BIG)


def _loss_head(y, tgt):
    t = y.shape[0]
    tm = min(1024, t)

    def body(y_ref, t_ref, dy_ref, acc_ref):
        @pl.when(pl.program_id(0) == 0)
        def _():
            acc_ref[...] = jnp.zeros_like(acc_ref)
        err = y_ref[...] - t_ref[...]
        dy_ref[...] = err * (1.0 / D)
        sq = _fold8(err * err)
        tot = sq[:, 0:128]
        for k in range(1, D // 128):
            tot = tot + sq[:, 128 * k:128 * (k + 1)]
        acc_ref[...] += tot

    row = pl.BlockSpec((tm, D), lambda i: (i, 0))
    return pl.pallas_call(
        body, name="loss_head", grid=(t // tm,), in_specs=[row, row],
        out_specs=[row, pl.BlockSpec((8, 128), lambda i: (0, 0))],
        out_shape=[SDS((t, D), F32), SDS((8, 128), F32)], compiler_params=_cp("arbitrary"),
    )(y, tgt)


def _out_proj_bwd(dout, y_a, y_b, u_gl, gate_b, w_sm, rider):
    t = dout.shape[0]
    tm = min(512, t)

    def body(do_ref, ya_ref, yb_ref, gla_ref, glb_ref, gb_ref, wco_ref, wao_ref, wout_ref,
             dya_ref, dyb_ref, dgl_ref, dyc_ref, dob_ref, dgb_ref):
        @pl.when(pl.program_id(0) == 0)
        def _():
            dgb_ref[...] = jnp.zeros_like(dgb_ref)
        dm = _dot_nt(do_ref[...].astype(BF16), wout_ref[...])
        gb = gb_ref[...]
        ga_ = _sigmoid(gla_ref[...].astype(F32) + gb[:, :D])
        gb_ = _sigmoid(glb_ref[...].astype(F32) + gb[:, D:])
        dya = (ga_ * dm).astype(BF16)
        dyb = (gb_ * dm).astype(BF16)
        dgla = ya_ref[...].astype(F32) * dm * (ga_ * (1.0 - ga_))
        dglb = yb_ref[...].astype(F32) * dm * (gb_ * (1.0 - gb_))
        dya_ref[...] = dya
        dyb_ref[...] = dyb
        dgl_ref[:, :D] = dgla.astype(BF16)
        dgl_ref[:, D:] = dglb.astype(BF16)
        dgb_ref[:, :D] += _fold8(dgla)
        dgb_ref[:, D:] += _fold8(dglb)
        dyc_ref[...] = _dot_nt(dya, wco_ref[...]).astype(BF16)
        dob_ref[...] = _dot_nt(dyb, wao_ref[...]).astype(BF16)

    row = pl.BlockSpec((tm, D), lambda i: (i, 0))
    wspec = lambda a: pl.BlockSpec((None, D, D), lambda i: (a, 0, 0))
    return _call(
        body, rider, name="out_proj_bwd", grid=(t // tm,),
        in_specs=[row, row, row, pl.BlockSpec((tm, D), lambda i: (i, 0)), pl.BlockSpec((tm, D), lambda i: (i, 1)),
                  pl.BlockSpec((1, 2 * D), lambda i: (0, 0)), wspec(0), wspec(1), wspec(2)],
        out_specs=[row, row, pl.BlockSpec((tm, 2 * D), lambda i: (i, 0)), row, row,
                   pl.BlockSpec((8, 2 * D), lambda i: (0, 0))],
        out_shape=[SDS((t, D), BF16), SDS((t, D), BF16), SDS((t, 2 * D), BF16), SDS((t, D), BF16), SDS((t, D), BF16),
                   SDS((8, 2 * D), F32)],
        args=(dout, y_a, y_b, u_gl, u_gl, gate_b, w_sm, w_sm, w_sm), vmem=VMEM_BIG)


def _small_wgrads(y_c, d_ya, o, d_yb, merged, dout):
    t = y_c.shape[0]
    tk = min(512, t)

    def body(yc_ref, dya_ref, o_ref, dyb_ref, mg_ref, do_ref, g_ref):
        @pl.when(pl.program_id(0) == 0)
        def _():
            g_ref[...] = jnp.zeros_like(g_ref)
        g_ref[0] += _dot_tn(yc_ref[...], dya_ref[...])
        g_ref[1] += _dot_tn(o_ref[...], dyb_ref[...])
        g_ref[2] += _dot_tn(mg_ref[...], do_ref[...].astype(BF16))

    row = pl.BlockSpec((tk, D), lambda k: (k, 0))
    return pl.pallas_call(
        body, name="small_wgrads", grid=(t // tk,), in_specs=[row] * 6,
        out_specs=pl.BlockSpec((3, D, D), lambda k: (0, 0, 0)), out_shape=SDS((3, D, D), F32),
        compiler_params=_cp("arbitrary", vmem=VMEM_BIG),
    )(y_c, d_ya, o, d_yb, merged, dout)


def _conv_bwd(d_yc, u_conv, conv_w, rider):
    t = d_yc.shape[0]
    tm = min(512, t)
    hb = tm // 16
    last_halo = t // 16 - 1
    n_steps = t // tm

    def body(dy_ref, v_ref, b_ref, c_ref, z_ref, hv_ref, hc_ref, ndy_ref, nb_ref, nz_ref, w_ref, du_ref, dw_ref):
        i = pl.program_id(0)

        @pl.when(i == 0)
        def _():
            dw_ref[...] = jnp.zeros_like(dw_ref)
        v, c = v_ref[...].astype(F32), c_ref[...].astype(F32)
        b, z = b_ref[...].astype(F32), z_ref[...].astype(F32)
        cv = c * v
        halo = jnp.where(i > 0, hc_ref[...].astype(F32) * hv_ref[...].astype(F32), 0.0)
        row = lax.broadcasted_iota(jnp.int32, (tm, 1), 0)
        s1 = jnp.where(row == 0, halo[15:16], pltpu.roll(cv, 1, 0))
        s2 = jnp.where(row == 0, halo[14:15], jnp.where(row == 1, halo[15:16], pltpu.roll(cv, 2, 0)))
        w0, w1, w2 = w_ref[0:1, :], w_ref[1:2, :], w_ref[2:3, :]
        conv = w0 * s2 + w1 * s1 + w2 * cv
        sig = _sigmoid(z)
        sz = z * sig
        dsz = sig * (1.0 + z * (1.0 - sig))
        dy = dy_ref[...].astype(F32)
        dconv = dy * b * sz
        nz = nz_ref[...].astype(F32)
        nxt = ndy_ref[...].astype(F32) * nb_ref[...].astype(F32) * (nz * _sigmoid(nz))
        nxt = jnp.where(i < n_steps - 1, nxt, 0.0)
        a1 = jnp.where(row == tm - 1, nxt[0:1], pltpu.roll(dconv, tm - 1, 0))
        a2 = jnp.where(row == tm - 2, nxt[0:1], jnp.where(row == tm - 1, nxt[1:2], pltpu.roll(dconv, tm - 2, 0)))
        dcv = w2 * dconv + w1 * a1 + w0 * a2
        du_ref[:, 0:D] = (dcv * c).astype(BF16)
        du_ref[:, D:2 * D] = (dy * conv * sz).astype(BF16)
        du_ref[:, 2 * D:3 * D] = (dcv * v).astype(BF16)
        du_ref[:, 3 * D:4 * D] = (dy * b * conv * dsz).astype(BF16)
        r8 = lax.broadcasted_iota(jnp.int32, (8, 1), 0)
        dw_ref[...] += jnp.where(r8 == 0, jnp.sum(dconv * s2, axis=0, keepdims=True),
                                 jnp.where(r8 == 1, jnp.sum(dconv * s1, axis=0, keepdims=True),
                                           jnp.where(r8 == 2, jnp.sum(dconv * cv, axis=0, keepdims=True), 0.0)))

    def col(k):
        return pl.BlockSpec((tm, D), lambda i: (i, k))

    def halo(k):
        return pl.BlockSpec((16, D), lambda i: (jnp.maximum(i * hb - 1, 0), k))

    def nxt(k):
        return pl.BlockSpec((16, D), lambda i: (jnp.minimum((i + 1) * hb, last_halo), k))

    return _call(
        body, rider, name="conv_bwd", grid=(t // tm,),
        in_specs=[col(0), col(0), col(1), col(2), col(3), halo(0), halo(2), nxt(0), nxt(1), nxt(3),
                  pl.BlockSpec((3, D), lambda i: (0, 0))],
        out_specs=[pl.BlockSpec((tm, 4 * D), lambda i: (i, 0)), pl.BlockSpec((8, D), lambda i: (0, 0))],
        out_shape=[SDS((t, 4 * D), BF16), SDS((8, D), F32)],
        args=(d_yc, u_conv, u_conv, u_conv, u_conv, u_conv, u_conv, d_yc, u_conv, u_conv, conv_w), vmem=VMEM_BIG)


def _attn_bwd(d_o, qs, kvx, u_za, lse, sinks, bias, rider):
    t = d_o.shape[0]
    nb = t // BLK

    def body(q_ref, kc_ref, kp_ref, za_ref, do_ref, lse_ref, sink_ref, bias_ref,
             dq_ref, dkv_ref, dza_ref, dsk_ref, carry_ref):
        n = pl.program_id(0)

        @pl.when(n == 0)
        def _():
            carry_ref[...] = jnp.zeros_like(carry_ref)
            dsk_ref[...] = jnp.zeros_like(dsk_ref)

        live = n < nb
        kvb = jnp.concatenate([kp_ref[...], kc_ref[...]], axis=0)
        bias_v = bias_ref[...]
        za = za_ref[...].astype(F32)
        sig = _sigmoid(za)
        dsa = sig * (1.0 + za * (1.0 - sig))
        do = jnp.where(live, do_ref[...].astype(F32), 0.0)
        dattn_f = do * (za * sig)
        dattn = dattn_f.astype(BF16)
        dattn_ln2 = (dattn_f * LN2).astype(BF16)
        lo_lanes = lax.broadcasted_iota(jnp.int32, (1, 128), 1) < HEAD
        dq_cols, attn_cols, dk_cols, dv_cols, dsk_rows = [], [], [], [], []
        for hk in range(N_KV):
            qpp = _pair_rows(q_ref, hk)
            dapp = _pair_rows(dattn, hk)
            dapp_ln2 = _pair_rows(dattn_ln2, hk)
            probs, dss, xk, xv = [], [], [], []
            for half in range(2):
                kx, vx = _kv_operands(kvb, hk, half)
                lse = lse_ref[0, 2 * hk + half:2 * hk + half + 1, :]
                prob = jnp.exp2(_dot_nt(kx, qpp) + bias_v - lse)
                psink = jnp.exp2(_sink_row(sink_ref, hk, half) - lse)
                tdp = prob * _dot_nt(vx, dapp_ln2)
                drow = jnp.sum(tdp, axis=0, keepdims=True)
                ds = (tdp - prob * drow).astype(BF16)
                prob_b = prob.astype(BF16)
                xk.append(_dot(ds, qpp))
                xv.append(_dot(prob_b, dapp))
                probs.append(prob_b)
                dss.append(ds)
                dsk_rows.append(-psink * drow * LOG2E)
            kcat = jnp.concatenate([kvb[:, 128 * hk:128 * hk + 128], kvb[:, 512 + 128 * hk:512 + 128 * hk + 128]], axis=0)
            vcat = jnp.concatenate([kvb[:, 1024 + 128 * hk:1024 + 128 * hk + 128],
                                    kvb[:, 1536 + 128 * hk:1536 + 128 * hk + 128]], axis=0)
            app = _dot_tn(jnp.concatenate(probs, axis=0), vcat)
            dqpp = _dot_tn(jnp.concatenate(dss, axis=0), kcat)
            dq_cols += [dqpp[:BLK], dqpp[BLK:]]
            attn_cols += [app[:BLK], app[BLK:]]
            dk_cols.append(jnp.where(lo_lanes, xk[0], xk[1]))
            dv_cols.append(jnp.where(lo_lanes, xv[0], xv[1]))

        @pl.when(live)
        def _():
            dq_ref[...] = jnp.concatenate(dq_cols, axis=1).astype(BF16)
            dza_ref[...] = (do * jnp.concatenate(attn_cols, axis=1) * dsa).astype(BF16)

        band = jnp.concatenate(dk_cols + dv_cols, axis=1)
        dkv_ref[...] = (band[:BLK] + carry_ref[...]).astype(BF16)
        carry_ref[...] = band[BLK:]
        dsk_ref[...] += jnp.broadcast_to(jnp.concatenate(dsk_rows, axis=1), (8, 2 * N_KV * 2 * BLK))

    cur = lambda n: jnp.minimum(n, nb - 1)
    prev = lambda n: jnp.maximum(n - 1, 0)
    return _call(
        body, rider, name="attn_bwd", grid=(nb + 1,),
        in_specs=[pl.BlockSpec((BLK, D), lambda n: (cur(n), 0)),
                  pl.BlockSpec((BLK, KVX), lambda n: (cur(n), 0)), pl.BlockSpec((BLK, KVX), lambda n: (prev(n), 0)),
                  pl.BlockSpec((BLK, D), lambda n: (cur(n), 0)), pl.BlockSpec((BLK, D), lambda n: (cur(n), 0)),
                  pl.BlockSpec((1, 8, 2 * BLK), lambda n: (cur(n), 0, 0)), pl.BlockSpec(memory_space=pltpu.SMEM),
                  pl.BlockSpec((None, 2 * BLK, 2 * BLK), lambda n: (jnp.minimum(n, 1), 0, 0))],
        out_specs=[pl.BlockSpec((BLK, D), lambda n: (cur(n), 0)), pl.BlockSpec((BLK, D), lambda n: (prev(n), 0)),
                   pl.BlockSpec((BLK, D), lambda n: (cur(n), 0)), pl.BlockSpec((8, 2 * D), lambda n: (0, 0))],
        out_shape=[SDS((t, D), BF16), SDS((t, D), BF16), SDS((t, D), BF16), SDS((8, 2 * D), F32)],
        scratch_shapes=[pltpu.VMEM((BLK, D), F32)],
        args=(qs, kvx, kvx, u_za, d_o, lse, sinks, bias), vmem=VMEM_BIG)


def _qkv_post(u_qkv, dqs, dkv, dza, qg_s, kg_t, rider):
    t = u_qkv.shape[0]
    tm = min(512, t)

    def norm_bwd(x, dy, g):
        r = lax.rsqrt(_head_sum(x * x) * (1.0 / HEAD) + EPS)
        xhat = x * r
        dxh = dy * g
        return r * (dxh - xhat * (_head_sum(dxh * xhat) * (1.0 / HEAD))), _fold8(dy * xhat)

    def body(u_ref, dq_ref, dkv_ref, dza_ref, qg_ref, kg_ref, du_ref, dqg_ref, dkg_ref):
        @pl.when(pl.program_id(0) == 0)
        def _():
            dqg_ref[...] = jnp.zeros_like(dqg_ref)
            dkg_ref[...] = jnp.zeros_like(dkg_ref)
        dq, gq = norm_bwd(u_ref[:, 0:D].astype(F32), dq_ref[...].astype(F32), qg_ref[...])
        fold = _fold_mat()
        dk, gk = norm_bwd(u_ref[:, D:D + 256].astype(F32), _dot(dkv_ref[:, 0:512], fold), kg_ref[...])
        du_ref[:, 0:D] = dq.astype(BF16)
        du_ref[:, D:D + 256] = dk.astype(BF16)
        du_ref[:, D + 256:D + 512] = _dot(dkv_ref[:, 512:1024], fold).astype(BF16)
        du_ref[:, D + 512:2 * D + 512] = dza_ref[...]
        dqg_ref[...] += gq
        dkg_ref[...] += gk

    row = pl.BlockSpec((tm, D), lambda i: (i, 0))
    return _call(
        body, rider, name="qkv_post", grid=(t // tm,),
        in_specs=[pl.BlockSpec((tm, 1536), lambda i: (i, 0)), row, row, row,
                  pl.BlockSpec((1, D), lambda i: (0, 0)), pl.BlockSpec((1, 256), lambda i: (0, 0))],
        out_specs=[pl.BlockSpec((tm, 2560), lambda i: (i, 0)), pl.BlockSpec((8, D), lambda i: (0, 0)),
                   pl.BlockSpec((8, 256), lambda i: (0, 0))],
        out_shape=[SDS((t, 2560), BF16), SDS((8, D), F32), SDS((8, 256), F32)],
        args=(u_qkv, dqs, dkv, dza, qg_s, kg_t), vmem=VMEM_BIG)


N_GRAN = N_IN // CB
DU_COLS = ((0, 4096), (4096, 6656), (6656, N_IN))


def _du_granule(j):
    return jnp.clip(j, 0, 7), jnp.clip(j - 8, 0, 4), jnp.clip(j - 13, 0, 3)


def _du_select(j, refs, fn):
    for ref, lo, hi in zip(refs, (0, 8, 13), (8, 13, 17)):
        @pl.when((j >= lo) & (j < hi))
        def _():
            fn(ref)


def _in_proj_bwd(du, w_full, x, g, dout, rider):
    t = du[0].shape[0]
    tn = min(256, t)

    def body(a0, a1, a2, w_hbm, x_ref, g_ref, do_ref, dx_ref, dg_ref, w_ref, sem):
        @pl.when(pl.program_id(0) == 0)
        def _():
            cp = pltpu.make_async_copy(w_hbm, w_ref, sem)
            cp.start()
            dg_ref[...] = jnp.zeros_like(dg_ref)
            cp.wait()
        acc = None
        for a_ref, (lo, hi) in zip((a0, a1, a2), DU_COLS):
            part = _dot_nt(w_ref[:, lo:hi], a_ref[...])
            acc = part if acc is None else acc + part
        dh = acc.T
        xv = x_ref[...]
        r = lax.rsqrt(jnp.mean(xv * xv, axis=-1, keepdims=True) + EPS)
        xhat = xv * r
        dg_ref[...] += _fold8(dh * xhat)
        dxh = dh * g_ref[...]
        dx_ref[...] = do_ref[...] + r * (dxh - xhat * jnp.mean(dxh * xhat, axis=-1, keepdims=True))

    row = pl.BlockSpec((tn, D), lambda i: (i, 0))
    return _call(
        body, rider, name="in_proj_bwd", grid=(t // tn,),
        in_specs=[pl.BlockSpec((tn, hi - lo), lambda i: (i, 0)) for lo, hi in DU_COLS]
        + [ANY, row, pl.BlockSpec((1, D), lambda i: (0, 0)), row],
        out_specs=[row, pl.BlockSpec((8, D), lambda i: (0, 0))], out_shape=[SDS((t, D), F32), SDS((8, D), F32)],
        scratch_shapes=[pltpu.VMEM((D, N_IN), BF16), pltpu.SemaphoreType.DMA(())],
        args=(*du, w_full, x, g, dout), vmem=VMEM_BIG)


def _in_proj_wgrad(ht, du, rider):
    t = ht.shape[1]
    tk = min(4096, t)
    n_k = t // tk

    def body(h_ref, b0, b1, b2, g_ref):
        j, k = pl.program_id(0), pl.program_id(1)

        if n_k > 1:
            @pl.when(k == 0)
            def _():
                g_ref[...] = jnp.zeros_like(g_ref)

        def add(b_ref):
            if n_k > 1:
                g_ref[...] += _dot(h_ref[...], b_ref[...])
            else:
                g_ref[...] = _dot(h_ref[...], b_ref[...])
        _du_select(j, (b0, b1, b2), add)

    seg = lambda q: pl.BlockSpec((tk, CB), lambda j, k: (k, _du_granule(j)[q]))
    (g,), got = _call(
        body, rider, name="in_proj_wgrad", grid=(N_GRAN, t // tk),
        in_specs=[pl.BlockSpec((D, tk), lambda j, k: (0, k)), seg(0), seg(1), seg(2)],
        out_specs=[pl.BlockSpec((D, CB), lambda j, k: (0, j))], out_shape=[SDS((D, N_IN), F32)],
        args=(ht, *du), vmem=VMEM_BIG)
    return g, got


def _swap_rider(g_in, g_sm):
    def copies(ins, outs, send, recv, base=0):
        x, y, c = _mesh_pos()
        cps = []
        for src, dst in zip(ins, outs):
            half = src.at[1 - c] if len(src.shape) == 3 else src.at[:, :, 1 - c]
            cps.append(_rcopy(half, dst, send, recv, base + len(cps), (x, y, 1 - c)))
        return cps

    arrays = [g for g in (g_in, g_sm) if g is not None]
    shapes = [SDS((512, N_IN), F32) if len(g.shape) == 3 else SDS((3, 4, 128, D), F32) for g in arrays]
    return _Rider(arrays, shapes, len(arrays), copies)


def _add_halves_in(cc_idx, g_in, r_in):
    rows, n_buf = 64, 3
    n_chunk = 512 // rows

    def body(cc_ref, g_ref, r_ref, f_ref, h_ref, abuf, bbuf, sbuf, hbuf, in_sem, out_sem):
        core, chip = cc_ref[0], cc_ref[1]

        def fetch(i):
            s = i % n_buf
            return (pltpu.make_async_copy(g_ref.at[core, pl.ds(i * rows, rows), :], abuf.at[s], in_sem.at[0, s]),
                    pltpu.make_async_copy(r_ref.at[pl.ds(i * rows, rows), :], bbuf.at[s], in_sem.at[1, s]))

        def drain(i):
            s = i % 2
            own = sbuf.at[s, :, pl.ds(pl.multiple_of(chip * SH_IN, 128), SH_IN)]
            return (pltpu.make_async_copy(hbuf.at[s], h_ref.at[pl.ds(i * rows, rows), :], out_sem.at[0, s]),
                    pltpu.make_async_copy(own, f_ref.at[pl.ds(i * rows, rows), :], out_sem.at[1, s]))

        for i in range(n_buf):
            for cp in fetch(i):
                cp.start()
        for i in range(n_chunk):
            for cp in fetch(i):
                cp.wait()
            if i >= 2:
                for cp in drain(i - 2):
                    cp.wait()
            s = abuf[i % n_buf] + bbuf[i % n_buf]
            sbuf[i % 2] = s
            hbuf[i % 2] = s.astype(BF16)
            for cp in drain(i):
                cp.start()
            if i + n_buf < n_chunk:
                for cp in fetch(i + n_buf):
                    cp.start()
        for i in (n_chunk - 2, n_chunk - 1):
            for cp in drain(i):
                cp.wait()

    return pl.pallas_call(
        body, name="add_halves_in",
        grid_spec=pltpu.PrefetchScalarGridSpec(
            num_scalar_prefetch=1, grid=(1,), in_specs=[ANY, ANY], out_specs=[ANY, ANY],
            scratch_shapes=[pltpu.VMEM((n_buf, rows, N_IN), F32), pltpu.VMEM((n_buf, rows, N_IN), F32),
                            pltpu.VMEM((2, rows, N_IN), F32), pltpu.VMEM((2, rows, N_IN), BF16),
                            pltpu.SemaphoreType.DMA((2, n_buf)), pltpu.SemaphoreType.DMA((2, 2))]),
        out_shape=[SDS((512, SH_IN), F32), SDS((512, N_IN), BF16)],
        compiler_params=_cp("arbitrary", vmem=VMEM_BIG),
    )(cc_idx, g_in, r_in)


def _add_halves_sm(c_idx, g_sm, r_sm):
    def body(c_ref, a_ref, b_ref, f_ref, h_ref):
        s = a_ref[...] + b_ref[...]
        f_ref[...] = s
        h_ref[...] = s.astype(BF16)

    blk = pl.BlockSpec((1, 4, 128, D), lambda a, c: (a, 0, 0, 0))
    return pl.pallas_call(
        body, name="add_halves_sm",
        grid_spec=pltpu.PrefetchScalarGridSpec(
            num_scalar_prefetch=1, grid=(3,),
            in_specs=[pl.BlockSpec((1, 4, None, 128, D), lambda a, c: (a, 0, c[0], 0, 0)), blk], out_specs=[blk, blk]),
        out_shape=[SDS((3, 4, 128, D), F32), SDS((3, 4, 128, D), BF16)], compiler_params=_cp("parallel"),
    )(c_idx, g_sm, r_sm)


def _scatter_rider(h_in, h_sm):
    def copies(ins, outs, send, recv, base=0):
        x, y, c = _mesh_pos()
        cps = []
        for src, dst in zip(ins, outs):
            for k, chip in enumerate(_other_chips(x, y)):
                their = 2 * chip[0] + chip[1]
                part = src.at[:, pl.ds(pl.multiple_of(their * SH_IN, 128), SH_IN)] if len(src.shape) == 2 else src.at[:, their]
                cps.append(_rcopy(part, dst.at[k], send, recv, base + len(cps), (*chip, c)))
        return cps

    arrays = [h for h in (h_in, h_sm) if h is not None]
    shapes = [SDS((3, 512, SH_IN), BF16) if len(h.shape) == 2 else SDS((3, 3, 128, D), BF16) for h in arrays]
    return _Rider(arrays, shapes, 3 * len(arrays), copies)


def _ride_alone(rider, name):
    return _hosted_call(None, rider, name=name, grid=(), in_specs=[], out_specs=[], out_shape=[], args=())[1]


def _final_sum_in(f_in, r_in):
    def body(a_ref, r_ref, o_ref):
        o_ref[...] = a_ref[...] + r_ref[0].astype(F32) + r_ref[1].astype(F32) + r_ref[2].astype(F32)

    return pl.pallas_call(
        body, name="final_sum_in", grid=(4,),
        in_specs=[pl.BlockSpec((128, SH_IN), lambda i: (i, 0)), pl.BlockSpec((3, 128, SH_IN), lambda i: (0, i, 0))],
        out_specs=pl.BlockSpec((128, SH_IN), lambda i: (i, 0)),
        out_shape=SDS((512, SH_IN), F32), compiler_params=_cp("parallel"),
    )(f_in, r_in)


def _final_sum_sm(chip_idx, f_sm, r_sm):
    def body(j_ref, a_ref, r_ref, o_ref):
        o_ref[...] = a_ref[...] + r_ref[0].astype(F32) + r_ref[1].astype(F32) + r_ref[2].astype(F32)

    return pl.pallas_call(
        body, name="final_sum_sm",
        grid_spec=pltpu.PrefetchScalarGridSpec(
            num_scalar_prefetch=1, grid=(3,),
            in_specs=[pl.BlockSpec((1, None, 128, D), lambda a, j: (a, j[0], 0, 0)),
                      pl.BlockSpec((3, 1, 128, D), lambda a, j: (0, a, 0, 0))],
            out_specs=pl.BlockSpec((1, 128, D), lambda a, j: (a, 0, 0))),
        out_shape=SDS((3, 128, D), F32), compiler_params=_cp("parallel"),
    )(chip_idx, f_sm, r_sm)


def _join_halves(t_in, t_sm):
    n_cp = N_LAYERS * 4
    args, plan = [], []
    for l in range(N_LAYERS):
        if t_in[l] is not None:
            plan.append((l, 0, len(args)))
            args.append(t_in[l])
        plan += [(l, a, len(args)) for a in (1, 2, 3)]
        args.append(t_sm[l])

    def body(*refs):
        ins, outs = refs[:len(args)], refs[len(args):len(args) + 4]
        send, recv, loc_in, loc_out, stage_in, stage_sm = refs[len(args) + 4:]
        x, y, c = _mesh_pos()
        cps, own = [], []

        def place(l, a, half):
            rows = 512 if a == 0 else 128
            return outs[a].at[l, pl.ds(pl.multiple_of(half * rows, rows), rows), :]

        for s, (l, a, k) in enumerate(plan):
            src = ins[k] if a == 0 else ins[k].at[a - 1]
            own.append((src, place(l, a, c), min(a, 1)))
            cp = pltpu.make_async_remote_copy(src_ref=src, dst_ref=place(l, a, c), send_sem=send.at[s],
                                              recv_sem=recv.at[s], device_id=(x, y, 1 - c), device_id_type=MESH)
            cp.start()
            cps.append(cp)
        _staged_copies(own, (stage_in, stage_sm), loc_in, loc_out)
        for s, (l, a, k) in enumerate(plan):
            got = place(l, a, 1 - c)
            pltpu.make_async_remote_copy(src_ref=got, dst_ref=got, send_sem=send.at[s], recv_sem=recv.at[s],
                                         device_id=(x, y, 1 - c), device_id_type=MESH).wait_recv()
        for cp in cps:
            cp.wait_send()

    sm = SDS((N_LAYERS, SH_ROW, D), F32)
    return pl.pallas_call(
        body, name="join_halves", in_specs=[ANY] * len(args), out_specs=[ANY] * 4,
        out_shape=[SDS((N_LAYERS, D, SH_IN), F32), sm, sm, sm],
        scratch_shapes=[pltpu.SemaphoreType.DMA((n_cp,))] * 4
        + [pltpu.VMEM((2, 512, SH_IN), F32), pltpu.VMEM((2, 128, D), F32)],
        compiler_params=_cp(vmem=VMEM_BIG),
    )(*args)


def _adam_math(w, g, m, v):
    m = ADAM_B1 * m + (1.0 - ADAM_B1) * g
    v = ADAM_B2 * v + (1.0 - ADAM_B2) * (g * g)
    m_hat = m / (1.0 - ADAM_B1 ** ADAM_STEP)
    v_hat = v / (1.0 - ADAM_B2 ** ADAM_STEP)
    delta = -ADAM_LR * (m_hat / (jnp.sqrt(v_hat) + ADAM_EPS) + ADAM_WD * w)
    return delta, m, v


def _adamw_big(w, g, m, v, name):
    rows, cols = w.shape
    tr = 256

    def body(w_ref, g_ref, m_ref, v_ref, go_ref, d_ref, nm_ref, nv_ref):
        g = g_ref[...]
        go_ref[...] = g
        d_ref[...], nm_ref[...], nv_ref[...] = _adam_math(w_ref[...], g, m_ref[...], v_ref[...])

    blk = pl.BlockSpec((tr, cols), lambda i: (i, 0))
    return pl.pallas_call(
        body, name=name, grid=(rows // tr,), in_specs=[blk] * 4, out_specs=[blk] * 4,
        out_shape=[SDS((rows, cols), F32)] * 4, compiler_params=_cp("parallel", vmem=VMEM_BIG),
    )(w, g, m, v)


def _adamw_small(ws, gs, ms, vs):
    n = len(ws)

    def body(*refs):
        for k in range(n):
            w_ref, g_ref, m_ref, v_ref = (refs[q * n + k] for q in range(4))
            d, nm, nv = _adam_math(w_ref[...], g_ref[...], m_ref[...], v_ref[...])
            refs[4 * n + k][...] = d
            refs[5 * n + k][...] = nm
            refs[6 * n + k][...] = nv

    vm = pl.BlockSpec(memory_space=pltpu.VMEM)
    shapes = [SDS(w.shape, F32) for w in ws]
    res = pl.pallas_call(
        body, name="adamw_small", in_specs=[vm] * (4 * n), out_specs=[vm] * (3 * n), out_shape=shapes * 3,
    )(*ws, *gs, *ms, *vs)
    return res[:n], res[n:2 * n], res[2 * n:]


def _pad_rows(a, rows):
    flat = a.reshape(-1)
    return jnp.pad(flat, (0, rows * 128 - flat.shape[0])).reshape(rows, 128)


def kernel(x, norm_g, w_in, conv_w, q_norm_g, k_norm_g, sinks, w_conv_out, w_attn_out, gate_b, w_out, loss_target, m_norm_g, m_w_in, m_conv_w, m_q_norm_g, m_k_norm_g, m_sinks, m_w_conv_out, m_w_attn_out, m_gate_b, m_w_out, v_norm_g, v_w_in, v_conv_w, v_q_norm_g, v_k_norm_g, v_sinks, v_w_conv_out, v_w_attn_out, v_gate_b, v_w_out):
    xi, yi, ci = _mesh_pos()
    chip = 2 * xi + yi
    c_idx = jnp.reshape(ci, (1,)).astype(jnp.int32)
    chip_idx = jnp.reshape(chip, (1,)).astype(jnp.int32)
    cc_idx = jnp.stack([ci, chip]).astype(jnp.int32)
    t = x.shape[1]
    xs = [x.reshape(t, D)]
    tgt = loss_target.reshape(t, D)

    full_w = [[_cast_w_in(chip_idx, w_in, l), _cast_w_small(chip_idx, w_conv_out, w_attn_out, w_out, l)]
              for l in range(N_LAYERS)]
    conv32 = lax.dynamic_update_slice(jnp.zeros((32, D), F32), jnp.pad(conv_w.reshape(3 * N_LAYERS, SH_ROW), ((0, 20), (0, 0))),
                                      (0, chip * SH_ROW))
    qg_s = jnp.tile(q_norm_g, (1, N_Q)) * (SCALE * LOG2E)
    kg_t = jnp.tile(k_norm_g, (1, N_KV))
    bias = _band_bias()
    saved = []
    for l in range(N_LAYERS):
        nxt = full_w[l + 1] if l + 1 < N_LAYERS else None
        (h, ht), got = _rmsnorm_fwd(xs[l], norm_g[l:l + 1], _gather_rider([full_w[0][0], conv32], "N") if l == 0 else None)
        if l == 0:
            got = _ride_alone(_gather_rider(got, "F"), "gather_first_forward")
            full_w[0][0], conv32 = _ride_alone(_gather_rider(got, "B"), "gather_first_d2d")
            conv_full = conv32[:3 * N_LAYERS].reshape(N_LAYERS, 3, D)
        (u_conv, u_qkv, u_za, u_gl), got = _in_proj(h, full_w[l][0], _gather_rider(nxt, "N") if nxt else None)
        if nxt:
            nxt[0], nxt[1] = got
        (qs, kvx), got = _qkv_prep(u_qkv, qg_s[l:l + 1], kg_t[l:l + 1],
                                   _gather_rider(full_w[0][1:], "N") if l == 0 else None)
        y_c, got = _conv_fwd(u_conv, conv_full[l], _gather_rider(got, "F") if l == 0 else None)
        o, lse, got = _attn_fwd(qs, kvx, u_za, sinks[l:l + 1], bias, _merge_riders(
            _gather_rider(nxt, "FB1") if nxt else None, _gather_rider(got, "B") if l == 0 else None))
        if nxt:
            nxt[0], nxt[1] = got[:2]
        if l == 0:
            full_w[0][1] = got[-1]
        (x_next, y_a, y_b, merged), got = _out_proj_fwd(xs[l], y_c, o, u_gl, gate_b[l:l + 1], full_w[l][1],
                                                        _gather_rider(nxt, "B2") if nxt else None)
        if nxt:
            nxt[0], nxt[1] = got
        xs.append(x_next)
        saved.append((ht, u_conv, u_qkv, u_za, u_gl, y_c, o, y_a, y_b, merged, qs, kvx, lse))

    dout, sq = _loss_head(xs[N_LAYERS], tgt)

    small, t_in, t_sm = [None] * N_LAYERS, [None] * N_LAYERS, [None] * N_LAYERS
    halves = None

    for l in reversed(range(N_LAYERS)):
        w_full, w_sm = full_w[l]
        last = l == 0
        ht, u_conv, u_qkv, u_za, u_gl, y_c, o, y_a, y_b, merged, qs, kvx, lse = saved[l]
        (d_ya, d_yb, du_gl, d_yc, d_o, dgb), _ = _out_proj_bwd(dout, y_a, y_b, u_gl, gate_b[l:l + 1], w_sm, None)
        g_sm = _small_wgrads(y_c, d_ya, o, d_yb, merged, dout).reshape(3, 4, 2, 128, D)
        (du_conv, dcw), got = _conv_bwd(d_yc, u_conv, conv_full[l], _swap_rider(None, g_sm) if last else None)
        if last:
            f_sm0, h_sm0 = _add_halves_sm(c_idx, g_sm, got[0])
        (dqs, dkv, dza, dsk), got = _attn_bwd(d_o, qs, kvx, u_za, lse, sinks[l:l + 1], bias,
                                              _scatter_rider(halves[1], None) if halves else None)
        if halves:
            t_in[l + 1] = _final_sum_in(halves[0], got[0])
        dsk = jnp.sum(dsk[0].reshape(N_KV, 2, 2, BLK), axis=-1).transpose(0, 2, 1).reshape(N_Q)
        (du_attn, dqg, dkg), _ = _qkv_post(u_qkv, dqs, dkv, dza, qg_s[l:l + 1], kg_t[l:l + 1], None)
        du = (du_conv, du_attn, du_gl)
        g_in, got = _in_proj_wgrad(ht, du, _merge_riders(
            _scatter_rider(None, halves[3]) if halves else None, _scatter_rider(None, h_sm0) if last else None))
        g_in = g_in.reshape(2, 512, N_IN)
        if halves:
            t_sm[l + 1] = _final_sum_sm(chip_idx, halves[2], got[0])
        if last:
            t_sm[0] = _final_sum_sm(chip_idx, f_sm0, got[-1])
        if last:
            f_in0, h_in0 = _add_halves_in(cc_idx, g_in, _ride_alone(_swap_rider(g_in, None), "swap_last")[0])
        (dout, dng), got = _in_proj_bwd(du, w_full, xs[l], norm_g[l:l + 1], dout,
                                        _scatter_rider(h_in0, None) if last else _swap_rider(g_in, g_sm))
        if last:
            t_in[0] = _final_sum_in(f_in0, got[0])
        else:
            halves = _add_halves_in(cc_idx, g_in, got[0]) + _add_halves_sm(c_idx, g_sm, got[1])
        small[l] = (jnp.sum(dng, axis=0), (SCALE * LOG2E) * jnp.sum(dqg.reshape(8 * N_Q, HEAD), axis=0),
                    jnp.sum(dkg.reshape(8 * N_KV, HEAD), axis=0), dsk, jnp.sum(dgb, axis=0), dcw[:3])
    grad_x = dout.reshape(1, t, D)

    stack = lambda k: jnp.stack([small[l][k] for l in range(N_LAYERS)])
    pack = jnp.concatenate([_pad_rows(stack(0), 32), _pad_rows(stack(1), 8), _pad_rows(stack(2), 8),
                            _pad_rows(stack(3), 8), _pad_rows(stack(4), 64), _pad_rows(stack(5), 96),
                            _pad_rows(jnp.sum(sq) * (0.5 / D), 8)], axis=0)
    red = _allreduce_small(pack)
    loss = red[216, 0]
    g_norm_g = red[0:32].reshape(N_LAYERS, D)
    g_q_norm_g = red[32:40].reshape(-1)[:N_LAYERS * HEAD].reshape(N_LAYERS, HEAD)
    g_k_norm_g = red[40:48].reshape(-1)[:N_LAYERS * HEAD].reshape(N_LAYERS, HEAD)
    g_sinks = red[48:56].reshape(-1)[:N_LAYERS * N_Q].reshape(N_LAYERS, N_Q)
    g_gate_b = red[56:120].reshape(N_LAYERS, 2 * D)
    g_conv_full = red[120:216].reshape(N_LAYERS, 3, D)
    g_conv_w = lax.dynamic_slice(g_conv_full, (0, 0, chip * SH_ROW), (N_LAYERS, 3, SH_ROW))

    g_w_in, g_w_co, g_w_ao, g_w_out = _join_halves(t_in, t_sm)

    r_in = N_LAYERS * D
    g_w_in, d_in, nm_in, nv_in = (a.reshape(N_LAYERS, D, SH_IN) for a in _adamw_big(
        w_in.reshape(r_in, SH_IN), g_w_in.reshape(r_in, SH_IN), m_w_in.reshape(r_in, SH_IN),
        v_w_in.reshape(r_in, SH_IN), "adamw_w_in"))
    r_sm = N_LAYERS * SH_ROW
    big = {}
    for nm, w, g, m, v in (("co", w_conv_out, g_w_co, m_w_conv_out, v_w_conv_out),
                           ("ao", w_attn_out, g_w_ao, m_w_attn_out, v_w_attn_out),
                           ("out", w_out, g_w_out, m_w_out, v_w_out)):
        big[nm] = tuple(a.reshape(N_LAYERS, SH_ROW, D) for a in _adamw_big(
            w.reshape(r_sm, D), g.reshape(r_sm, D), m.reshape(r_sm, D), v.reshape(r_sm, D), "adamw_w_small"))
    g_w_co, g_w_ao, g_w_out = big["co"][0], big["ao"][0], big["out"][0]
    sm_w = [norm_g, conv_w, q_norm_g, k_norm_g, sinks, gate_b]
    sm_g = [g_norm_g, g_conv_w, g_q_norm_g, g_k_norm_g, g_sinks, g_gate_b]
    sm_m = [m_norm_g, m_conv_w, m_q_norm_g, m_k_norm_g, m_sinks, m_gate_b]
    sm_v = [v_norm_g, v_conv_w, v_q_norm_g, v_k_norm_g, v_sinks, v_gate_b]
    sd, snm, snv = _adamw_small(sm_w, sm_g, sm_m, sm_v)

    def order(norm, w_in_, conv, qn, kn, sk, co, ao, gb, wo):
        return [norm, w_in_, conv, qn, kn, sk, co, ao, gb, wo]

    grads = order(g_norm_g, g_w_in, g_conv_w, g_q_norm_g, g_k_norm_g, g_sinks, g_w_co, g_w_ao, g_gate_b, g_w_out)
    deltas = order(sd[0], d_in, sd[1], sd[2], sd[3], sd[4], big["co"][1], big["ao"][1], sd[5], big["out"][1])
    new_m = order(snm[0], nm_in, snm[1], snm[2], snm[3], snm[4], big["co"][2], big["ao"][2], snm[5], big["out"][2])
    new_v = order(snv[0], nv_in, snv[1], snv[2], snv[3], snv[4], big["co"][3], big["ao"][3], snv[5], big["out"][3])
    return (loss, grad_x, *grads, *deltas, *new_m, *new_v)
```

```python
import functools

import jax
import jax.numpy as jnp
from jax import lax
from jax.experimental import pallas as pl
from jax.experimental.pallas import tpu as pltpu

F32, BF16 = jnp.float32, jnp.bfloat16
SDS = jax.ShapeDtypeStruct
MESH = pl.DeviceIdType.MESH
ANY = pl.BlockSpec(memory_space=pl.ANY)

D = 1024
N_IN = 8704
N_LAYERS = 4
N_Q, N_KV, HEAD = 16, 4, 64
GROUP = N_Q // N_KV
BLK = 128
EPS = 1e-6
NEG = -1e30
SCALE = HEAD ** -0.5
SH_IN = N_IN // 4
SH_ROW = D // 4
CB = 512
VMEM_BIG = 56 * 1024 * 1024

ADAM_LR, ADAM_B1, ADAM_B2, ADAM_EPS, ADAM_WD, ADAM_STEP = 0.001, 0.9, 0.999, 1e-08, 0.01, 10


def _cp(*sem, vmem=None):
    return pltpu.CompilerParams(dimension_semantics=sem if sem else None, vmem_limit_bytes=vmem)


def _sigmoid(z):
    return 1.0 / (1.0 + jnp.exp(-z))


def _dot(a, b):
    return jnp.dot(a, b, preferred_element_type=F32)


def _dot_nt(a, b):
    return lax.dot_general(a, b, (((1,), (1,)), ((), ())), preferred_element_type=F32)


def _dot_tn(a, b):
    return lax.dot_general(a, b, (((0,), (0,)), ((), ())), preferred_element_type=F32)


def _fold8(v):
    return jnp.sum(v.reshape(v.shape[0] // 8, 8, v.shape[1]), axis=0)


def _cast_w_in(chip_idx, w, layer):
    def body(j_ref, i_ref, o_ref):
        o_ref[...] = i_ref[...].astype(BF16)

    return pl.pallas_call(
        body, name="cast_w_in",
        grid_spec=pltpu.PrefetchScalarGridSpec(
            num_scalar_prefetch=1, grid=(2,),
            in_specs=[pl.BlockSpec((None, 512, SH_IN), lambda i, j: (layer, i, 0))],
            out_specs=pl.BlockSpec((512, SH_IN), lambda i, j: (i, j[0]))),
        out_shape=SDS((D, N_IN), BF16), compiler_params=_cp("parallel"),
    )(chip_idx, w)


def _cast_w_small(chip_idx, a, b, c, layer):
    def body(j_ref, a_ref, b_ref, c_ref, o_ref):
        o_ref[0] = a_ref[...].astype(BF16)
        o_ref[1] = b_ref[...].astype(BF16)
        o_ref[2] = c_ref[...].astype(BF16)

    spec = pl.BlockSpec((None, SH_ROW, D), lambda i, j: (layer, 0, 0))
    return pl.pallas_call(
        body, name="cast_w_small",
        grid_spec=pltpu.PrefetchScalarGridSpec(
            num_scalar_prefetch=1, grid=(1,), in_specs=[spec, spec, spec],
            out_specs=pl.BlockSpec((3, SH_ROW, D), lambda i, j: (0, j[0], 0))),
        out_shape=SDS((3, D, D), BF16), compiler_params=_cp("parallel"),
    )(chip_idx, a, b, c)


def _mesh_pos():
    return lax.axis_index("x"), lax.axis_index("y"), lax.axis_index("c")


def _other_chips(x, y):
    return [(1 - x, y), (x, 1 - y), (1 - x, 1 - y)]


class _Rider:
    def __init__(self, ins, out_shape, n, copies, aliases=()):
        self.ins, self.out_shape, self.n, self.copies, self.aliases = list(ins), list(out_shape), n, copies, aliases


def _merge_riders(*riders):
    riders = [r for r in riders if r is not None]
    if len(riders) < 2:
        return riders[0] if riders else None

    def copies(ins, outs, send, recv, base=0):
        cps, i0, o0 = [], 0, 0
        for r in riders:
            cps += r.copies(ins[i0:i0 + len(r.ins)], outs[o0:o0 + len(r.out_shape)], send, recv, base + len(cps))
            i0, o0 = i0 + len(r.ins), o0 + len(r.out_shape)
        return cps

    aliases, i0, o0 = [], 0, 0
    for r in riders:
        aliases += [(i0 + i, o0 + o) for i, o in r.aliases]
        i0, o0 = i0 + len(r.ins), o0 + len(r.out_shape)
    return _Rider(sum((r.ins for r in riders), []), sum((r.out_shape for r in riders), []),
                  sum(r.n for r in riders), copies, tuple(aliases))


def _rcopy(src, dst, send, recv, k, to):
    return pltpu.make_async_remote_copy(src_ref=src, dst_ref=dst, send_sem=send.at[k], recv_sem=recv.at[k],
                                        device_id=to, device_id_type=MESH)


def _hosted_call(body, rider, *, name, grid, in_specs, out_specs, out_shape, args, scratch_shapes=(), vmem=None):
    n_in, n_out, n_scr = len(in_specs), len(out_specs), len(scratch_shapes)
    r_in, r_out = len(rider.ins), len(rider.out_shape)

    def full_body(*refs):
        host_in, rid_in = refs[:n_in], refs[n_in:n_in + r_in]
        o0 = n_in + r_in
        host_out, rid_out = refs[o0:o0 + n_out], refs[o0 + n_out:o0 + n_out + r_out]
        s0 = o0 + n_out + r_out
        host_scr, (send, recv) = refs[s0:s0 + n_scr], refs[s0 + n_scr:]
        if body is None:
            cps = rider.copies(rid_in, rid_out, send, recv)
            for cp in cps:
                cp.start()
            for cp in cps:
                cp.wait()
            return
        ids = [pl.program_id(a) for a in range(len(grid))]
        first = functools.reduce(lambda p, q: p & q, [i == 0 for i in ids])
        last = functools.reduce(lambda p, q: p & q, [i == g - 1 for i, g in zip(ids, grid)])

        @pl.when(first)
        def _():
            for cp in rider.copies(rid_in, rid_out, send, recv):
                cp.start()

        body(*host_in, *host_out, *host_scr)

        @pl.when(last)
        def _():
            for cp in rider.copies(rid_in, rid_out, send, recv):
                cp.wait()

    res = pl.pallas_call(
        full_body, name=name, grid=grid if body is not None else (),
        in_specs=list(in_specs) + [ANY] * r_in, out_specs=list(out_specs) + [ANY] * r_out,
        out_shape=list(out_shape) + rider.out_shape,
        scratch_shapes=list(scratch_shapes) + [pltpu.SemaphoreType.DMA((rider.n,))] * 2,
        input_output_aliases={n_in + i: n_out + o for i, o in rider.aliases},
        compiler_params=_cp(*(("arbitrary",) * len(grid) if body is not None else ()), vmem=vmem),
    )(*args, *rider.ins)
    return res[:n_out], res[n_out:]


def _call(body, rider, **kw):
    if rider is not None:
        return _hosted_call(body, rider, **kw)
    res = pl.pallas_call(
        body, name=kw["name"], grid=kw["grid"], in_specs=list(kw["in_specs"]), out_specs=list(kw["out_specs"]),
        out_shape=list(kw["out_shape"]), scratch_shapes=list(kw.get("scratch_shapes", ())),
        compiler_params=_cp(*(("arbitrary",) * len(kw["grid"])), vmem=kw.get("vmem")),
    )(*kw["args"])
    return res, []


def _gather_rider(arrays, stage):
    def region(full, whose, c, sub):
        if len(full.shape) == 2:
            rows, cols = full.shape[0] // 2, full.shape[1] // 4
            first, n = (c * rows, rows) if sub is None else (c * rows + sub * (rows // 2), rows // 2)
            return full.at[pl.ds(pl.multiple_of(first, n), n), pl.ds(pl.multiple_of(whose * cols, 128), cols)]
        first, n = (whose * SH_ROW + c * 128, 128) if sub is None else (whose * SH_ROW + c * 128 + sub * 64, 64)
        return full.at[:, pl.ds(pl.multiple_of(first, n), n), :]

    def copies(ins, outs, send, recv, base=0):
        x, y, c = _mesh_pos()
        nbr_x, nbr_y = (1 - x, y), (x, 1 - y)
        cps = []
        for full in outs:
            plan = []
            if stage == "N":
                plan = [(region(full, 2 * x + y, c, None), (*nbr_x, c)), (region(full, 2 * x + y, c, None), (*nbr_y, c))]
            if stage in ("F", "FB1"):
                plan = [(region(full, 2 * nbr_x[0] + nbr_x[1], c, 0), (*nbr_y, c)),
                        (region(full, 2 * nbr_y[0] + nbr_y[1], c, 1), (*nbr_x, c))]
            if stage in ("B", "FB1", "B2"):
                chips = {"B": _other_chips(x, y), "FB1": [nbr_x, nbr_y], "B2": [(1 - x, 1 - y)]}[stage]
                plan += [(region(full, 2 * chip[0] + chip[1], c, None), (x, y, 1 - c)) for chip in chips]
            for reg, to in plan:
                cps.append(_rcopy(reg, reg, send, recv, base + len(cps), to))
        return cps

    per_array = {"N": 2, "F": 2, "B": 3, "FB1": 4, "B2": 1}[stage]
    return _Rider(arrays, [SDS(v.shape, v.dtype) for v in arrays], per_array * len(arrays), copies,
                  aliases=tuple((i, i) for i in range(len(arrays))))


def _staged_copies(copies, stages, sem_in, sem_out):
    busy, count = {}, {}
    for idx, (src, dst, kind) in enumerate(copies):
        slot = count.get(kind, 0) % 2
        count[kind] = count.get(kind, 0) + 1
        if (kind, slot) in busy:
            busy.pop((kind, slot)).wait()
        buf = stages[kind].at[slot]
        cin = pltpu.make_async_copy(src, buf, sem_in.at[idx])
        cin.start()
        cin.wait()
        cout = pltpu.make_async_copy(buf, dst, sem_out.at[idx])
        cout.start()
        busy[(kind, slot)] = cout
    for cp in busy.values():
        cp.wait()


def _allreduce_small(pack):
    rows = pack.shape[0]

    def body(p_ref, o_ref, buf, send, recv):
        x, y, c = _mesh_pos()
        me = 4 * x + 2 * y + c
        sends = []
        for r in range(1, 8):
            to = (x if not (r & 4) else 1 - x, y if not (r & 2) else 1 - y, c if not (r & 1) else 1 - c)
            cp = pltpu.make_async_remote_copy(src_ref=p_ref, dst_ref=buf.at[me], send_sem=send.at[r - 1],
                                              recv_sem=recv.at[r - 1], device_id=to, device_id_type=MESH)
            cp.start()
            sends.append(cp)
        buf[me] = p_ref[...]
        for r in range(1, 8):
            frm = (4 * x + 2 * y + c) ^ r
            pltpu.make_async_remote_copy(src_ref=p_ref, dst_ref=buf.at[frm], send_sem=send.at[r - 1],
                                         recv_sem=recv.at[r - 1], device_id=(x, y, c), device_id_type=MESH).wait_recv()
        acc = buf[0]
        for d in range(1, 8):
            acc = acc + buf[d]
        o_ref[...] = acc
        for cp in sends:
            cp.wait_send()

    vm = pl.BlockSpec(memory_space=pltpu.VMEM)
    return pl.pallas_call(
        body, name="allreduce_small", in_specs=[vm], out_specs=vm, out_shape=SDS((rows, 128), F32),
        scratch_shapes=[pltpu.VMEM((8, rows, 128), F32), pltpu.SemaphoreType.DMA((7,)), pltpu.SemaphoreType.DMA((7,))],
    )(pack)


def _rmsnorm_fwd(x, g, rider):
    t = x.shape[0]
    tm = min(1024, t)

    def body(x_ref, g_ref, h_ref, ht_ref):
        xv = x_ref[...]
        r = lax.rsqrt(jnp.mean(xv * xv, axis=-1, keepdims=True) + EPS)
        h = xv * r * g_ref[...]
        h_ref[...] = h.astype(BF16)
        ht_ref[...] = h.T.astype(BF16)

    return _call(
        body, rider, name="rmsnorm_fwd", grid=(t // tm,),
        in_specs=[pl.BlockSpec((tm, D), lambda i: (i, 0)), pl.BlockSpec((1, D), lambda i: (0, 0))],
        out_specs=[pl.BlockSpec((tm, D), lambda i: (i, 0)), pl.BlockSpec((D, tm), lambda i: (0, i))],
        out_shape=[SDS((t, D), BF16), SDS((D, t), BF16)], args=(x, g), vmem=VMEM_BIG)


FWD_SEGS = ((0, 8), (8, 3), (11, 2), (13, 4))


def _in_proj(h, w_full, rider):
    t = h.shape[0]
    tm = min(2048, t)

    def body(a_ref, b_ref, *o_refs):
        j = pl.program_id(1)
        for o_ref, (off, nblk) in zip(o_refs, FWD_SEGS):
            @pl.when((j >= off) & (j < off + nblk))
            def _():
                o_ref[...] = _dot(a_ref[...], b_ref[...]).astype(BF16)

    def out(seg):
        off, nblk = seg
        return pl.BlockSpec((tm, CB), lambda i, j: (i, jnp.clip(j - off, 0, nblk - 1)))

    res, got = _call(
        body, rider, name="in_proj", grid=(t // tm, N_IN // CB),
        in_specs=[pl.BlockSpec((tm, D), lambda i, j: (i, 0)), pl.BlockSpec((D, CB), lambda i, j: (0, j))],
        out_specs=[out(s) for s in FWD_SEGS], out_shape=[SDS((t, s[1] * CB), BF16) for s in FWD_SEGS],
        args=(h, w_full), vmem=VMEM_BIG)
    return res, got


def _conv_fwd(u_conv, conv_w, rider):
    t = u_conv.shape[0]
    tm = min(512, t)
    hb = tm // 16

    def body(v_ref, b_ref, c_ref, z_ref, hv_ref, hc_ref, w_ref, y_ref):
        i = pl.program_id(0)
        cv = c_ref[...].astype(F32) * v_ref[...].astype(F32)
        halo = hc_ref[...].astype(F32) * hv_ref[...].astype(F32)
        halo = jnp.where(i > 0, halo, 0.0)
        row = lax.broadcasted_iota(jnp.int32, (tm, 1), 0)
        s1 = jnp.where(row == 0, halo[15:16], pltpu.roll(cv, 1, 0))
        s2 = jnp.where(row == 0, halo[14:15], jnp.where(row == 1, halo[15:16], pltpu.roll(cv, 2, 0)))
        conv = w_ref[0:1, :] * s2 + w_ref[1:2, :] * s1 + w_ref[2:3, :] * cv
        z = z_ref[...].astype(F32)
        y_ref[...] = (b_ref[...].astype(F32) * conv * (z * _sigmoid(z))).astype(BF16)

    def col(k):
        return pl.BlockSpec((tm, D), lambda i: (i, k))

    def halo(k):
        return pl.BlockSpec((16, D), lambda i: (jnp.maximum(i * hb - 1, 0), k))

    (y_c,), got = _call(
        body, rider, name="conv_fwd", grid=(t // tm,),
        in_specs=[col(0), col(1), col(2), col(3), halo(0), halo(2), pl.BlockSpec((3, D), lambda i: (0, 0))],
        out_specs=[pl.BlockSpec((tm, D), lambda i: (i, 0))], out_shape=[SDS((t, D), BF16)],
        args=(u_conv, u_conv, u_conv, u_conv, u_conv, u_conv, conv_w), vmem=VMEM_BIG)
    return y_c, got


KVX = 4 * N_KV * 128


def _iota2(shape):
    return lax.broadcasted_iota(jnp.int32, shape, 0), lax.broadcasted_iota(jnp.int32, shape, 1)


def _head_sum(v):
    r, c = _iota2((128, 128))
    ones = ((r >> 6) == (c >> 6)).astype(BF16)
    hi = v.astype(BF16)
    lo = (v - hi.astype(F32)).astype(BF16)
    return jnp.concatenate([_dot(hi[:, g:g + 128], ones) + _dot(lo[:, g:g + 128], ones)
                            for g in range(0, v.shape[1], 128)], axis=1)


def _expand_mats():
    r, c = _iota2((N_KV * HEAD, N_KV * 128))
    base = ((r >> 6) << 7) + (r & 63)
    return (c == base).astype(BF16), (c == base + 64).astype(BF16)


def _fold_mat():
    r, c = _iota2((N_KV * 128, N_KV * HEAD))
    return (((r >> 7) == (c >> 6)) & ((r & 63) == (c & 63))).astype(BF16)


def _qkv_prep(u_qkv, qg_s, kg_t, rider):
    t = u_qkv.shape[0]
    tm = min(1024, t)

    def body(u_ref, qg_ref, kg_ref, qs_ref, kvx_ref):
        q = u_ref[:, 0:D].astype(F32)
        rq = lax.rsqrt(_head_sum(q * q) * (1.0 / HEAD) + EPS)
        qs_ref[...] = (q * rq * qg_ref[...]).astype(BF16)
        k = u_ref[:, D:D + 256].astype(F32)
        rk = lax.rsqrt(_head_sum(k * k) * (1.0 / HEAD) + EPS)
        kn = (k * rk * kg_ref[...]).astype(BF16)
        v = u_ref[:, D + 256:D + 512]
        e_lo, e_hi = _expand_mats()
        kvx_ref[:, 0:512] = _dot(kn, e_lo).astype(BF16)
        kvx_ref[:, 512:1024] = _dot(kn, e_hi).astype(BF16)
        kvx_ref[:, 1024:1536] = _dot(v, e_lo).astype(BF16)
        kvx_ref[:, 1536:2048] = _dot(v, e_hi).astype(BF16)

    return _call(
        body, rider, name="qkv_prep", grid=(t // tm,),
        in_specs=[pl.BlockSpec((tm, 1536), lambda i: (i, 0)), pl.BlockSpec((1, D), lambda i: (0, 0)),
                  pl.BlockSpec((1, 256), lambda i: (0, 0))],
        out_specs=[pl.BlockSpec((tm, D), lambda i: (i, 0)), pl.BlockSpec((tm, KVX), lambda i: (i, 0))],
        out_shape=[SDS((t, D), BF16), SDS((t, KVX), BF16)], args=(u_qkv, qg_s, kg_t), vmem=VMEM_BIG)


def _band_bias():
    j, r = _iota2((2 * BLK, 2 * BLK))
    diff = (r & (BLK - 1)) - j + BLK
    band = (diff >= 0) & (diff < BLK)
    return jnp.stack([jnp.where(band & (j >= BLK), 0.0, NEG), jnp.where(band, 0.0, NEG)]).astype(F32)


def _pair_rows(ref_or_val, hk):
    return jnp.concatenate([ref_or_val[:, 256 * hk:256 * hk + 128], ref_or_val[:, 256 * hk + 128:256 * hk + 256]], axis=0)


LOG2E, LN2 = 1.4426950408889634, 0.6931471805599453


def _sink_row(sink_ref, hk, half):
    return jnp.concatenate([jnp.full((1, BLK), sink_ref[0, GROUP * hk + half] * LOG2E, F32),
                            jnp.full((1, BLK), sink_ref[0, GROUP * hk + 2 + half] * LOG2E, F32)], axis=1)


class _Band:
    def __init__(self, kp_ref, kc_ref):
        self.kp_ref, self.kc_ref = kp_ref, kc_ref

    def __getitem__(self, idx):
        return jnp.concatenate([self.kp_ref[idx], self.kc_ref[idx]], axis=0)


def _kv_operands(kvb, hk, half):
    return (kvb[:, 512 * half + 128 * hk:512 * half + 128 * hk + 128],
            kvb[:, 1024 + 512 * half + 128 * hk:1024 + 512 * half + 128 * hk + 128])


def _attn_fwd(qs, kvx, u_za, sinks, bias, rider):
    t = qs.shape[0]
    nb = t // BLK

    def body(q_ref, kc_ref, kp_ref, za_ref, sink_ref, bias_ref, o_ref, lse_ref):
        kvb = _Band(kp_ref, kc_ref)
        bias_v = bias_ref[...]
        key0 = lax.broadcasted_iota(jnp.int32, (2 * BLK, 1), 0) == 0
        ones = jnp.ones((2 * BLK, 128), BF16)
        cols = []
        for hk in range(N_KV):
            qpp = _pair_rows(q_ref, hk)
            opp = None
            for half in range(2):
                kx, vx = _kv_operands(kvb, hk, half)
                s = _dot_nt(kx, qpp) + bias_v
                sink = _sink_row(sink_ref, hk, half)
                m = jnp.maximum(jnp.max(s, axis=0, keepdims=True), sink)
                p = jnp.exp2(s - m)
                es = jnp.exp2(sink - m)
                lse_ref[0, 2 * hk + half:2 * hk + half + 1, :] = m + jnp.log(jnp.sum(p, axis=0, keepdims=True) + es) * LOG2E
                pe = jnp.where(key0, es, p).astype(BF16)
                rhs = jnp.concatenate([jnp.where(key0, jnp.zeros_like(vx), vx), ones], axis=1)
                nd = _dot_tn(pe, rhs)
                o = nd[:, :128] * (1.0 / nd[:, 128:])
                opp = o if opp is None else opp + o
            cols += [opp[:BLK], opp[BLK:]]
        za = za_ref[...].astype(F32)
        o_ref[...] = (jnp.concatenate(cols, axis=1) * (za * _sigmoid(za))).astype(BF16)

    prev = lambda n: jnp.maximum(n - 1, 0)
    (o, lse), got = _call(
        body, rider, name="attn_fwd", grid=(nb,),
        in_specs=[pl.BlockSpec((BLK, D), lambda n: (n, 0)),
                  pl.BlockSpec((BLK, KVX), lambda n: (n, 0)), pl.BlockSpec((BLK, KVX), lambda n: (prev(n), 0)),
                  pl.BlockSpec((BLK, D), lambda n: (n, 0)), pl.BlockSpec(memory_space=pltpu.SMEM),
                  pl.BlockSpec((None, 2 * BLK, 2 * BLK), lambda n: (jnp.minimum(n, 1), 0, 0))],
        out_specs=[pl.BlockSpec((BLK, D), lambda n: (n, 0)), pl.BlockSpec((1, 8, 2 * BLK), lambda n: (n, 0, 0))],
        out_shape=[SDS((t, D), BF16), SDS((nb, 8, 2 * BLK), F32)],
        args=(qs, kvx, kvx, u_za, sinks, bias), vmem=VMEM_BIG)
    return o, lse, got


def _out_proj_fwd(x, y_c, o, u_gl, gate_b, w_sm, rider):
    t = x.shape[0]
    tm = min(512, t)

    def body(x_ref, yc_ref, o_ref, gla_ref, glb_ref, gb_ref, wco_ref, wao_ref, wout_ref,
             xn_ref, ya_ref, yb_ref, mg_ref):
        ya = _dot(yc_ref[...], wco_ref[...])
        yb = _dot(o_ref[...], wao_ref[...])
        gb = gb_ref[...]
        ga_ = _sigmoid(gla_ref[...].astype(F32) + gb[:, :D])
        gb_ = _sigmoid(glb_ref[...].astype(F32) + gb[:, D:])
        merged = (ga_ * ya + gb_ * yb).astype(BF16)
        ya_ref[...] = ya.astype(BF16)
        yb_ref[...] = yb.astype(BF16)
        mg_ref[...] = merged
        xn_ref[...] = x_ref[...] + _dot(merged, wout_ref[...])

    row = pl.BlockSpec((tm, D), lambda i: (i, 0))
    wspec = lambda a: pl.BlockSpec((None, D, D), lambda i: (a, 0, 0))
    return _call(
        body, rider, name="out_proj_fwd", grid=(t // tm,),
        in_specs=[row, row, row, pl.BlockSpec((tm, D), lambda i: (i, 0)), pl.BlockSpec((tm, D), lambda i: (i, 1)),
                  pl.BlockSpec((1, 2 * D), lambda i: (0, 0)), wspec(0), wspec(1), wspec(2)],
        out_specs=[row, row, row, row],
        out_shape=[SDS((t, D), F32), SDS((t, D), BF16), SDS((t, D), BF16), SDS((t, D), BF16)],
        args=(x, y_c, o, u_gl, u_gl, gate_b, w_sm, w_sm, w_sm), vmem=VMEM_BIG)


def _loss_head(y, tgt):
    t = y.shape[0]
    tm = min(1024, t)

    def body(y_ref, t_ref, dy_ref, acc_ref):
        @pl.when(pl.program_id(0) == 0)
        def _():
            acc_ref[...] = jnp.zeros_like(acc_ref)
        err = y_ref[...] - t_ref[...]
        dy_ref[...] = err * (1.0 / D)
        sq = _fold8(err * err)
        tot = sq[:, 0:128]
        for k in range(1, D // 128):
            tot = tot + sq[:, 128 * k:128 * (k + 1)]
        acc_ref[...] += tot

    row = pl.BlockSpec((tm, D), lambda i: (i, 0))
    return pl.pallas_call(
        body, name="loss_head", grid=(t // tm,), in_specs=[row, row],
        out_specs=[row, pl.BlockSpec((8, 128), lambda i: (0, 0))],
        out_shape=[SDS((t, D), F32), SDS((8, 128), F32)], compiler_params=_cp("arbitrary"),
    )(y, tgt)


def _out_proj_bwd(dout, y_a, y_b, u_gl, gate_b, w_sm, rider):
    t = dout.shape[0]
    tm = min(512, t)

    def body(do_ref, ya_ref, yb_ref, gla_ref, glb_ref, gb_ref, wco_ref, wao_ref, wout_ref,
             dya_ref, dyb_ref, dgl_ref, dyc_ref, dob_ref, dgb_ref):
        @pl.when(pl.program_id(0) == 0)
        def _():
            dgb_ref[...] = jnp.zeros_like(dgb_ref)
        dm = _dot_nt(do_ref[...].astype(BF16), wout_ref[...])
        gb = gb_ref[...]
        ga_ = _sigmoid(gla_ref[...].astype(F32) + gb[:, :D])
        gb_ = _sigmoid(glb_ref[...].astype(F32) + gb[:, D:])
        dya = (ga_ * dm).astype(BF16)
        dyb = (gb_ * dm).astype(BF16)
        dgla = ya_ref[...].astype(F32) * dm * (ga_ * (1.0 - ga_))
        dglb = yb_ref[...].astype(F32) * dm * (gb_ * (1.0 - gb_))
        dya_ref[...] = dya
        dyb_ref[...] = dyb
        dgl_ref[:, :D] = dgla.astype(BF16)
        dgl_ref[:, D:] = dglb.astype(BF16)
        dgb_ref[:, :D] += _fold8(dgla)
        dgb_ref[:, D:] += _fold8(dglb)
        dyc_ref[...] = _dot_nt(dya, wco_ref[...]).astype(BF16)
        dob_ref[...] = _dot_nt(dyb, wao_ref[...]).astype(BF16)

    row = pl.BlockSpec((tm, D), lambda i: (i, 0))
    wspec = lambda a: pl.BlockSpec((None, D, D), lambda i: (a, 0, 0))
    return _call(
        body, rider, name="out_proj_bwd", grid=(t // tm,),
        in_specs=[row, row, row, pl.BlockSpec((tm, D), lambda i: (i, 0)), pl.BlockSpec((tm, D), lambda i: (i, 1)),
                  pl.BlockSpec((1, 2 * D), lambda i: (0, 0)), wspec(0), wspec(1), wspec(2)],
        out_specs=[row, row, pl.BlockSpec((tm, 2 * D), lambda i: (i, 0)), row, row,
                   pl.BlockSpec((8, 2 * D), lambda i: (0, 0))],
        out_shape=[SDS((t, D), BF16), SDS((t, D), BF16), SDS((t, 2 * D), BF16), SDS((t, D), BF16), SDS((t, D), BF16),
                   SDS((8, 2 * D), F32)],
        args=(dout, y_a, y_b, u_gl, u_gl, gate_b, w_sm, w_sm, w_sm), vmem=VMEM_BIG)


def _small_wgrads(y_c, d_ya, o, d_yb, merged, dout):
    t = y_c.shape[0]
    tk = min(512, t)

    def body(yc_ref, dya_ref, o_ref, dyb_ref, mg_ref, do_ref, g_ref):
        @pl.when(pl.program_id(0) == 0)
        def _():
            g_ref[...] = jnp.zeros_like(g_ref)
        g_ref[0] += _dot_tn(yc_ref[...], dya_ref[...])
        g_ref[1] += _dot_tn(o_ref[...], dyb_ref[...])
        g_ref[2] += _dot_tn(mg_ref[...], do_ref[...].astype(BF16))

    row = pl.BlockSpec((tk, D), lambda k: (k, 0))
    return pl.pallas_call(
        body, name="small_wgrads", grid=(t // tk,), in_specs=[row] * 6,
        out_specs=pl.BlockSpec((3, D, D), lambda k: (0, 0, 0)), out_shape=SDS((3, D, D), F32),
        compiler_params=_cp("arbitrary", vmem=VMEM_BIG),
    )(y_c, d_ya, o, d_yb, merged, dout)


def _conv_bwd(d_yc, u_conv, conv_w, rider):
    t = d_yc.shape[0]
    tm = min(512, t)
    hb = tm // 16
    last_halo = t // 16 - 1
    n_steps = t // tm

    def body(dy_ref, v_ref, b_ref, c_ref, z_ref, hv_ref, hc_ref, ndy_ref, nb_ref, nz_ref, w_ref, du_ref, dw_ref):
        i = pl.program_id(0)

        @pl.when(i == 0)
        def _():
            dw_ref[...] = jnp.zeros_like(dw_ref)
        v, c = v_ref[...].astype(F32), c_ref[...].astype(F32)
        b, z = b_ref[...].astype(F32), z_ref[...].astype(F32)
        cv = c * v
        halo = jnp.where(i > 0, hc_ref[...].astype(F32) * hv_ref[...].astype(F32), 0.0)
        row = lax.broadcasted_iota(jnp.int32, (tm, 1), 0)
        s1 = jnp.where(row == 0, halo[15:16], pltpu.roll(cv, 1, 0))
        s2 = jnp.where(row == 0, halo[14:15], jnp.where(row == 1, halo[15:16], pltpu.roll(cv, 2, 0)))
        w0, w1, w2 = w_ref[0:1, :], w_ref[1:2, :], w_ref[2:3, :]
        conv = w0 * s2 + w1 * s1 + w2 * cv
        sig = _sigmoid(z)
        sz = z * sig
        dsz = sig * (1.0 + z * (1.0 - sig))
        dy = dy_ref[...].astype(F32)
        dconv = dy * b * sz
        nz = nz_ref[...].astype(F32)
        nxt = ndy_ref[...].astype(F32) * nb_ref[...].astype(F32) * (nz * _sigmoid(nz))
        nxt = jnp.where(i < n_steps - 1, nxt, 0.0)
        a1 = jnp.where(row == tm - 1, nxt[0:1], pltpu.roll(dconv, tm - 1, 0))
        a2 = jnp.where(row == tm - 2, nxt[0:1], jnp.where(row == tm - 1, nxt[1:2], pltpu.roll(dconv, tm - 2, 0)))
        dcv = w2 * dconv + w1 * a1 + w0 * a2
        du_ref[:, 0:D] = (dcv * c).astype(BF16)
        du_ref[:, D:2 * D] = (dy * conv * sz).astype(BF16)
        du_ref[:, 2 * D:3 * D] = (dcv * v).astype(BF16)
        du_ref[:, 3 * D:4 * D] = (dy * b * conv * dsz).astype(BF16)
        r8 = lax.broadcasted_iota(jnp.int32, (8, 1), 0)
        dw_ref[...] += jnp.where(r8 == 0, jnp.sum(dconv * s2, axis=0, keepdims=True),
                                 jnp.where(r8 == 1, jnp.sum(dconv * s1, axis=0, keepdims=True),
                                           jnp.where(r8 == 2, jnp.sum(dconv * cv, axis=0, keepdims=True), 0.0)))

    def col(k):
        return pl.BlockSpec((tm, D), lambda i: (i, k))

    def halo(k):
        return pl.BlockSpec((16, D), lambda i: (jnp.maximum(i * hb - 1, 0), k))

    def nxt(k):
        return pl.BlockSpec((16, D), lambda i: (jnp.minimum((i + 1) * hb, last_halo), k))

    return _call(
        body, rider, name="conv_bwd", grid=(t // tm,),
        in_specs=[col(0), col(0), col(1), col(2), col(3), halo(0), halo(2), nxt(0), nxt(1), nxt(3),
                  pl.BlockSpec((3, D), lambda i: (0, 0))],
        out_specs=[pl.BlockSpec((tm, 4 * D), lambda i: (i, 0)), pl.BlockSpec((8, D), lambda i: (0, 0))],
        out_shape=[SDS((t, 4 * D), BF16), SDS((8, D), F32)],
        args=(d_yc, u_conv, u_conv, u_conv, u_conv, u_conv, u_conv, d_yc, u_conv, u_conv, conv_w), vmem=VMEM_BIG)


def _attn_bwd(d_o, qs, kvx, u_za, lse, sinks, bias, rider):
    t = d_o.shape[0]
    nb = t // BLK

    def body(q_ref, kc_ref, kp_ref, za_ref, do_ref, lse_ref, sink_ref, bias_ref,
             dq_ref, dkv_ref, dza_ref, dsk_ref, carry_ref):
        n = pl.program_id(0)

        @pl.when(n == 0)
        def _():
            carry_ref[...] = jnp.zeros_like(carry_ref)
            dsk_ref[...] = jnp.zeros_like(dsk_ref)

        live = n < nb
        kvb = _Band(kp_ref, kc_ref)
        bias_v = bias_ref[...]
        za = za_ref[...].astype(F32)
        sig = _sigmoid(za)
        dsa = sig * (1.0 + za * (1.0 - sig))
        do = jnp.where(live, do_ref[...].astype(F32), 0.0)
        dattn_f = do * (za * sig)
        dattn = dattn_f.astype(BF16)
        dattn_ln2 = (dattn_f * LN2).astype(BF16)
        lo_lanes = lax.broadcasted_iota(jnp.int32, (1, 128), 1) < HEAD
        dq_cols, attn_cols, dk_cols, dv_cols, dsk_rows = [], [], [], [], []
        for hk in range(N_KV):
            qpp = _pair_rows(q_ref, hk)
            dapp = _pair_rows(dattn, hk)
            dapp_ln2 = _pair_rows(dattn_ln2, hk)
            probs, dss, xk, xv = [], [], [], []
            for half in range(2):
                kx, vx = _kv_operands(kvb, hk, half)
                lse = lse_ref[0, 2 * hk + half:2 * hk + half + 1, :]
                prob = jnp.exp2(_dot_nt(kx, qpp) + bias_v - lse)
                psink = jnp.exp2(_sink_row(sink_ref, hk, half) - lse)
                tdp = prob * _dot_nt(vx, dapp_ln2)
                drow = jnp.sum(tdp, axis=0, keepdims=True)
                ds = (tdp - prob * drow).astype(BF16)
                prob_b = prob.astype(BF16)
                xk.append(_dot(ds, qpp))
                xv.append(_dot(prob_b, dapp))
                probs.append(prob_b)
                dss.append(ds)
                dsk_rows.append(-psink * drow * LOG2E)
            kcat = jnp.concatenate([kvb[:, 128 * hk:128 * hk + 128], kvb[:, 512 + 128 * hk:512 + 128 * hk + 128]], axis=0)
            vcat = jnp.concatenate([kvb[:, 1024 + 128 * hk:1024 + 128 * hk + 128],
                                    kvb[:, 1536 + 128 * hk:1536 + 128 * hk + 128]], axis=0)
            app = _dot_tn(jnp.concatenate(probs, axis=0), vcat)
            dqpp = _dot_tn(jnp.concatenate(dss, axis=0), kcat)
            dq_cols += [dqpp[:BLK], dqpp[BLK:]]
            attn_cols += [app[:BLK], app[BLK:]]
            dk_cols.append(jnp.where(lo_lanes, xk[0], xk[1]))
            dv_cols.append(jnp.where(lo_lanes, xv[0], xv[1]))

        @pl.when(live)
        def _():
            dq_ref[...] = jnp.concatenate(dq_cols, axis=1).astype(BF16)
            dza_ref[...] = (do * jnp.concatenate(attn_cols, axis=1) * dsa).astype(BF16)

        band = jnp.concatenate(dk_cols + dv_cols, axis=1)
        dkv_ref[...] = (band[:BLK] + carry_ref[...]).astype(BF16)
        carry_ref[...] = band[BLK:]
        dsk_ref[...] += jnp.broadcast_to(jnp.concatenate(dsk_rows, axis=1), (8, 2 * N_KV * 2 * BLK))

    cur = lambda n: jnp.minimum(n, nb - 1)
    prev = lambda n: jnp.maximum(n - 1, 0)
    return _call(
        body, rider, name="attn_bwd", grid=(nb + 1,),
        in_specs=[pl.BlockSpec((BLK, D), lambda n: (cur(n), 0)),
                  pl.BlockSpec((BLK, KVX), lambda n: (cur(n), 0)), pl.BlockSpec((BLK, KVX), lambda n: (prev(n), 0)),
                  pl.BlockSpec((BLK, D), lambda n: (cur(n), 0)), pl.BlockSpec((BLK, D), lambda n: (cur(n), 0)),
                  pl.BlockSpec((1, 8, 2 * BLK), lambda n: (cur(n), 0, 0)), pl.BlockSpec(memory_space=pltpu.SMEM),
                  pl.BlockSpec((None, 2 * BLK, 2 * BLK), lambda n: (jnp.minimum(n, 1), 0, 0))],
        out_specs=[pl.BlockSpec((BLK, D), lambda n: (cur(n), 0)), pl.BlockSpec((BLK, D), lambda n: (prev(n), 0)),
                   pl.BlockSpec((BLK, D), lambda n: (cur(n), 0)), pl.BlockSpec((8, 2 * D), lambda n: (0, 0))],
        out_shape=[SDS((t, D), BF16), SDS((t, D), BF16), SDS((t, D), BF16), SDS((8, 2 * D), F32)],
        scratch_shapes=[pltpu.VMEM((BLK, D), F32)],
        args=(qs, kvx, kvx, u_za, d_o, lse, sinks, bias), vmem=VMEM_BIG)


def _qkv_post(u_qkv, dqs, dkv, dza, qg_s, kg_t, rider):
    t = u_qkv.shape[0]
    tm = min(512, t)

    def norm_bwd(x, dy, g):
        r = lax.rsqrt(_head_sum(x * x) * (1.0 / HEAD) + EPS)
        xhat = x * r
        dxh = dy * g
        return r * (dxh - xhat * (_head_sum(dxh * xhat) * (1.0 / HEAD))), _fold8(dy * xhat)

    def body(u_ref, dq_ref, dkv_ref, dza_ref, qg_ref, kg_ref, du_ref, dqg_ref, dkg_ref):
        @pl.when(pl.program_id(0) == 0)
        def _():
            dqg_ref[...] = jnp.zeros_like(dqg_ref)
            dkg_ref[...] = jnp.zeros_like(dkg_ref)
        dq, gq = norm_bwd(u_ref[:, 0:D].astype(F32), dq_ref[...].astype(F32), qg_ref[...])
        fold = _fold_mat()
        dk, gk = norm_bwd(u_ref[:, D:D + 256].astype(F32), _dot(dkv_ref[:, 0:512], fold), kg_ref[...])
        du_ref[:, 0:D] = dq.astype(BF16)
        du_ref[:, D:D + 256] = dk.astype(BF16)
        du_ref[:, D + 256:D + 512] = _dot(dkv_ref[:, 512:1024], fold).astype(BF16)
        du_ref[:, D + 512:2 * D + 512] = dza_ref[...]
        dqg_ref[...] += gq
        dkg_ref[...] += gk

    row = pl.BlockSpec((tm, D), lambda i: (i, 0))
    return _call(
        body, rider, name="qkv_post", grid=(t // tm,),
        in_specs=[pl.BlockSpec((tm, 1536), lambda i: (i, 0)), row, row, row,
                  pl.BlockSpec((1, D), lambda i: (0, 0)), pl.BlockSpec((1, 256), lambda i: (0, 0))],
        out_specs=[pl.BlockSpec((tm, 2560), lambda i: (i, 0)), pl.BlockSpec((8, D), lambda i: (0, 0)),
                   pl.BlockSpec((8, 256), lambda i: (0, 0))],
        out_shape=[SDS((t, 2560), BF16), SDS((8, D), F32), SDS((8, 256), F32)],
        args=(u_qkv, dqs, dkv, dza, qg_s, kg_t), vmem=VMEM_BIG)


N_GRAN = N_IN // CB
DU_COLS = ((0, 4096), (4096, 6656), (6656, N_IN))


def _du_granule(j):
    return jnp.clip(j, 0, 7), jnp.clip(j - 8, 0, 4), jnp.clip(j - 13, 0, 3)


def _du_select(j, refs, fn):
    for ref, lo, hi in zip(refs, (0, 8, 13), (8, 13, 17)):
        @pl.when((j >= lo) & (j < hi))
        def _():
            fn(ref)


def _in_proj_bwd(du, w_full, x, g, dout, rider):
    t = du[0].shape[0]
    tn = min(256, t)

    def body(a0, a1, a2, w_hbm, x_ref, g_ref, do_ref, dx_ref, dg_ref, w_ref, sem):
        @pl.when(pl.program_id(0) == 0)
        def _():
            cp = pltpu.make_async_copy(w_hbm, w_ref, sem)
            cp.start()
            dg_ref[...] = jnp.zeros_like(dg_ref)
            cp.wait()
        acc = None
        for a_ref, (lo, hi) in zip((a0, a1, a2), DU_COLS):
            part = _dot_nt(w_ref[:, lo:hi], a_ref[...])
            acc = part if acc is None else acc + part
        dh = acc.T
        xv = x_ref[...]
        r = lax.rsqrt(jnp.mean(xv * xv, axis=-1, keepdims=True) + EPS)
        xhat = xv * r
        dg_ref[...] += _fold8(dh * xhat)
        dxh = dh * g_ref[...]
        dx_ref[...] = do_ref[...] + r * (dxh - xhat * jnp.mean(dxh * xhat, axis=-1, keepdims=True))

    row = pl.BlockSpec((tn, D), lambda i: (i, 0))
    return _call(
        body, rider, name="in_proj_bwd", grid=(t // tn,),
        in_specs=[pl.BlockSpec((tn, hi - lo), lambda i: (i, 0)) for lo, hi in DU_COLS]
        + [ANY, row, pl.BlockSpec((1, D), lambda i: (0, 0)), row],
        out_specs=[row, pl.BlockSpec((8, D), lambda i: (0, 0))], out_shape=[SDS((t, D), F32), SDS((8, D), F32)],
        scratch_shapes=[pltpu.VMEM((D, N_IN), BF16), pltpu.SemaphoreType.DMA(())],
        args=(*du, w_full, x, g, dout), vmem=VMEM_BIG)


def _in_proj_wgrad(ht, du, rider):
    t = ht.shape[1]
    tk = min(4096, t)
    n_k = t // tk

    def body(h_ref, b0, b1, b2, g_ref):
        j, k = pl.program_id(0), pl.program_id(1)

        if n_k > 1:
            @pl.when(k == 0)
            def _():
                g_ref[...] = jnp.zeros_like(g_ref)

        def add(b_ref):
            if n_k > 1:
                g_ref[...] += _dot(h_ref[...], b_ref[...])
            else:
                g_ref[...] = _dot(h_ref[...], b_ref[...])
        _du_select(j, (b0, b1, b2), add)

    seg = lambda q: pl.BlockSpec((tk, CB), lambda j, k: (k, _du_granule(j)[q]))
    (g,), got = _call(
        body, rider, name="in_proj_wgrad", grid=(N_GRAN, t // tk),
        in_specs=[pl.BlockSpec((D, tk), lambda j, k: (0, k)), seg(0), seg(1), seg(2)],
        out_specs=[pl.BlockSpec((D, CB), lambda j, k: (0, j))], out_shape=[SDS((D, N_IN), F32)],
        args=(ht, *du), vmem=VMEM_BIG)
    return g, got


def _swap_rider(g_in, g_sm):
    def copies(ins, outs, send, recv, base=0):
        x, y, c = _mesh_pos()
        cps = []
        for src, dst in zip(ins, outs):
            half = src.at[1 - c] if len(src.shape) == 3 else src.at[:, :, 1 - c]
            cps.append(_rcopy(half, dst, send, recv, base + len(cps), (x, y, 1 - c)))
        return cps

    arrays = [g for g in (g_in, g_sm) if g is not None]
    shapes = [SDS((512, N_IN), F32) if len(g.shape) == 3 else SDS((3, 4, 128, D), F32) for g in arrays]
    return _Rider(arrays, shapes, len(arrays), copies)


def _add_halves_in(cc_idx, g_in, r_in):
    def body(cc_ref, a_ref, b_ref, f_ref, h_ref):
        s = a_ref[...] + b_ref[...]
        h_ref[...] = s.astype(BF16)

        @pl.when(pl.program_id(1) == cc_ref[1])
        def _():
            f_ref[...] = s

    blk = pl.BlockSpec((256, SH_IN), lambda i, j, cc: (i, j))
    return pl.pallas_call(
        body, name="add_halves_in",
        grid_spec=pltpu.PrefetchScalarGridSpec(
            num_scalar_prefetch=1, grid=(2, 4),
            in_specs=[pl.BlockSpec((None, 256, SH_IN), lambda i, j, cc: (cc[0], i, j)), blk],
            out_specs=[pl.BlockSpec((256, SH_IN), lambda i, j, cc: (i, 0)), blk]),
        out_shape=[SDS((512, SH_IN), F32), SDS((512, N_IN), BF16)],
        compiler_params=_cp("arbitrary", "arbitrary", vmem=VMEM_BIG),
    )(cc_idx, g_in, r_in)


def _add_halves_sm(c_idx, g_sm, r_sm):
    def body(c_ref, a_ref, b_ref, f_ref, h_ref):
        s = a_ref[...] + b_ref[...]
        f_ref[...] = s
        h_ref[...] = s.astype(BF16)

    blk = pl.BlockSpec((1, 4, 128, D), lambda a, c: (a, 0, 0, 0))
    return pl.pallas_call(
        body, name="add_halves_sm",
        grid_spec=pltpu.PrefetchScalarGridSpec(
            num_scalar_prefetch=1, grid=(3,),
            in_specs=[pl.BlockSpec((1, 4, None, 128, D), lambda a, c: (a, 0, c[0], 0, 0)), blk], out_specs=[blk, blk]),
        out_shape=[SDS((3, 4, 128, D), F32), SDS((3, 4, 128, D), BF16)], compiler_params=_cp("parallel"),
    )(c_idx, g_sm, r_sm)


def _scatter_rider(h_in, h_sm):
    def copies(ins, outs, send, recv, base=0):
        x, y, c = _mesh_pos()
        cps = []
        for src, dst in zip(ins, outs):
            for k, chip in enumerate(_other_chips(x, y)):
                their = 2 * chip[0] + chip[1]
                part = src.at[:, pl.ds(pl.multiple_of(their * SH_IN, 128), SH_IN)] if len(src.shape) == 2 else src.at[:, their]
                cps.append(_rcopy(part, dst.at[k], send, recv, base + len(cps), (*chip, c)))
        return cps

    arrays = [h for h in (h_in, h_sm) if h is not None]
    shapes = [SDS((3, 512, SH_IN), BF16) if len(h.shape) == 2 else SDS((3, 3, 128, D), BF16) for h in arrays]
    return _Rider(arrays, shapes, 3 * len(arrays), copies)


def _ride_alone(rider, name):
    return _hosted_call(None, rider, name=name, grid=(), in_specs=[], out_specs=[], out_shape=[], args=())[1]


def _final_sum_in(f_in, r_in):
    def body(a_ref, r_ref, o_ref):
        o_ref[...] = a_ref[...] + r_ref[0].astype(F32) + r_ref[1].astype(F32) + r_ref[2].astype(F32)

    return pl.pallas_call(
        body, name="final_sum_in", grid=(4,),
        in_specs=[pl.BlockSpec((128, SH_IN), lambda i: (i, 0)), pl.BlockSpec((3, 128, SH_IN), lambda i: (0, i, 0))],
        out_specs=pl.BlockSpec((128, SH_IN), lambda i: (i, 0)),
        out_shape=SDS((512, SH_IN), F32), compiler_params=_cp("parallel"),
    )(f_in, r_in)


def _final_sum_sm(chip_idx, f_sm, r_sm):
    def body(j_ref, a_ref, r_ref, o_ref):
        o_ref[...] = a_ref[...] + r_ref[0].astype(F32) + r_ref[1].astype(F32) + r_ref[2].astype(F32)

    return pl.pallas_call(
        body, name="final_sum_sm",
        grid_spec=pltpu.PrefetchScalarGridSpec(
            num_scalar_prefetch=1, grid=(3,),
            in_specs=[pl.BlockSpec((1, None, 128, D), lambda a, j: (a, j[0], 0, 0)),
                      pl.BlockSpec((3, 1, 128, D), lambda a, j: (0, a, 0, 0))],
            out_specs=pl.BlockSpec((1, 128, D), lambda a, j: (a, 0, 0))),
        out_shape=SDS((3, 128, D), F32), compiler_params=_cp("parallel"),
    )(chip_idx, f_sm, r_sm)


def _join_halves(t_in, t_sm):
    n_cp = N_LAYERS * 4
    args, plan = [], []
    for l in range(N_LAYERS):
        if t_in[l] is not None:
            plan.append((l, 0, len(args)))
            args.append(t_in[l])
        plan += [(l, a, len(args)) for a in (1, 2, 3)]
        args.append(t_sm[l])

    def body(*refs):
        ins, outs = refs[:len(args)], refs[len(args):len(args) + 4]
        send, recv, loc_in, loc_out, stage_in, stage_sm = refs[len(args) + 4:]
        x, y, c = _mesh_pos()
        cps, own = [], []

        def place(l, a, half):
            rows = 512 if a == 0 else 128
            return outs[a].at[l, pl.ds(pl.multiple_of(half * rows, rows), rows), :]

        for s, (l, a, k) in enumerate(plan):
            src = ins[k] if a == 0 else ins[k].at[a - 1]
            own.append((src, place(l, a, c), min(a, 1)))
            cp = pltpu.make_async_remote_copy(src_ref=src, dst_ref=place(l, a, c), send_sem=send.at[s],
                                              recv_sem=recv.at[s], device_id=(x, y, 1 - c), device_id_type=MESH)
            cp.start()
            cps.append(cp)
        _staged_copies(own, (stage_in, stage_sm), loc_in, loc_out)
        for s, (l, a, k) in enumerate(plan):
            got = place(l, a, 1 - c)
            pltpu.make_async_remote_copy(src_ref=got, dst_ref=got, send_sem=send.at[s], recv_sem=recv.at[s],
                                         device_id=(x, y, 1 - c), device_id_type=MESH).wait_recv()
        for cp in cps:
            cp.wait_send()

    sm = SDS((N_LAYERS, SH_ROW, D), F32)
    return pl.pallas_call(
        body, name="join_halves", in_specs=[ANY] * len(args), out_specs=[ANY] * 4,
        out_shape=[SDS((N_LAYERS, D, SH_IN), F32), sm, sm, sm],
        scratch_shapes=[pltpu.SemaphoreType.DMA((n_cp,))] * 4
        + [pltpu.VMEM((2, 512, SH_IN), F32), pltpu.VMEM((2, 128, D), F32)],
        compiler_params=_cp(vmem=VMEM_BIG),
    )(*args)


def _adam_math(w, g, m, v):
    m = ADAM_B1 * m + (1.0 - ADAM_B1) * g
    v = ADAM_B2 * v + (1.0 - ADAM_B2) * (g * g)
    m_hat = m / (1.0 - ADAM_B1 ** ADAM_STEP)
    v_hat = v / (1.0 - ADAM_B2 ** ADAM_STEP)
    delta = -ADAM_LR * (m_hat / (jnp.sqrt(v_hat) + ADAM_EPS) + ADAM_WD * w)
    return delta, m, v


def _adamw_big(w, g, m, v, name):
    rows, cols = w.shape
    tr = 256

    def body(w_ref, g_ref, m_ref, v_ref, go_ref, d_ref, nm_ref, nv_ref):
        g = g_ref[...]
        go_ref[...] = g
        d_ref[...], nm_ref[...], nv_ref[...] = _adam_math(w_ref[...], g, m_ref[...], v_ref[...])

    blk = pl.BlockSpec((tr, cols), lambda i: (i, 0))
    return pl.pallas_call(
        body, name=name, grid=(rows // tr,), in_specs=[blk] * 4, out_specs=[blk] * 4,
        out_shape=[SDS((rows, cols), F32)] * 4, compiler_params=_cp("parallel", vmem=VMEM_BIG),
    )(w, g, m, v)


def _adamw_small(ws, gs, ms, vs):
    n = len(ws)

    def body(*refs):
        for k in range(n):
            w_ref, g_ref, m_ref, v_ref = (refs[q * n + k] for q in range(4))
            d, nm, nv = _adam_math(w_ref[...], g_ref[...], m_ref[...], v_ref[...])
            refs[4 * n + k][...] = d
            refs[5 * n + k][...] = nm
            refs[6 * n + k][...] = nv

    vm = pl.BlockSpec(memory_space=pltpu.VMEM)
    shapes = [SDS(w.shape, F32) for w in ws]
    res = pl.pallas_call(
        body, name="adamw_small", in_specs=[vm] * (4 * n), out_specs=[vm] * (3 * n), out_shape=shapes * 3,
    )(*ws, *gs, *ms, *vs)
    return res[:n], res[n:2 * n], res[2 * n:]


def _pad_rows(a, rows):
    flat = a.reshape(-1)
    return jnp.pad(flat, (0, rows * 128 - flat.shape[0])).reshape(rows, 128)


def kernel(x, norm_g, w_in, conv_w, q_norm_g, k_norm_g, sinks, w_conv_out, w_attn_out, gate_b, w_out, loss_target, m_norm_g, m_w_in, m_conv_w, m_q_norm_g, m_k_norm_g, m_sinks, m_w_conv_out, m_w_attn_out, m_gate_b, m_w_out, v_norm_g, v_w_in, v_conv_w, v_q_norm_g, v_k_norm_g, v_sinks, v_w_conv_out, v_w_attn_out, v_gate_b, v_w_out):
    xi, yi, ci = _mesh_pos()
    chip = 2 * xi + yi
    c_idx = jnp.reshape(ci, (1,)).astype(jnp.int32)
    chip_idx = jnp.reshape(chip, (1,)).astype(jnp.int32)
    cc_idx = jnp.stack([ci, chip]).astype(jnp.int32)
    t = x.shape[1]
    xs = [x.reshape(t, D)]
    tgt = loss_target.reshape(t, D)

    full_w = [[_cast_w_in(chip_idx, w_in, l), _cast_w_small(chip_idx, w_conv_out, w_attn_out, w_out, l)]
              for l in range(N_LAYERS)]
    conv32 = lax.dynamic_update_slice(jnp.zeros((32, D), F32), jnp.pad(conv_w.reshape(3 * N_LAYERS, SH_ROW), ((0, 20), (0, 0))),
                                      (0, chip * SH_ROW))
    qg_s = jnp.tile(q_norm_g, (1, N_Q)) * (SCALE * LOG2E)
    kg_t = jnp.tile(k_norm_g, (1, N_KV))
    bias = _band_bias()
    saved = []
    for l in range(N_LAYERS):
        nxt = full_w[l + 1] if l + 1 < N_LAYERS else None
        (h, ht), got = _rmsnorm_fwd(xs[l], norm_g[l:l + 1], _gather_rider([full_w[0][0], conv32], "N") if l == 0 else None)
        if l == 0:
            got = _ride_alone(_gather_rider(got, "F"), "gather_first_forward")
            full_w[0][0], conv32 = _ride_alone(_gather_rider(got, "B"), "gather_first_d2d")
            conv_full = conv32[:3 * N_LAYERS].reshape(N_LAYERS, 3, D)
        (u_conv, u_qkv, u_za, u_gl), got = _in_proj(h, full_w[l][0], _gather_rider(nxt, "N") if nxt else None)
        if nxt:
            nxt[0], nxt[1] = got
        (qs, kvx), got = _qkv_prep(u_qkv, qg_s[l:l + 1], kg_t[l:l + 1],
                                   _gather_rider(full_w[0][1:], "N") if l == 0 else None)
        y_c, got = _conv_fwd(u_conv, conv_full[l], _gather_rider(got, "F") if l == 0 else None)
        o, lse, got = _attn_fwd(qs, kvx, u_za, sinks[l:l + 1], bias, _merge_riders(
            _gather_rider(nxt, "FB1") if nxt else None, _gather_rider(got, "B") if l == 0 else None))
        if nxt:
            nxt[0], nxt[1] = got[:2]
        if l == 0:
            full_w[0][1] = got[-1]
        (x_next, y_a, y_b, merged), got = _out_proj_fwd(xs[l], y_c, o, u_gl, gate_b[l:l + 1], full_w[l][1],
                                                        _gather_rider(nxt, "B2") if nxt else None)
        if nxt:
            nxt[0], nxt[1] = got
        xs.append(x_next)
        saved.append((ht, u_conv, u_qkv, u_za, u_gl, y_c, o, y_a, y_b, merged, qs, kvx, lse))

    dout, sq = _loss_head(xs[N_LAYERS], tgt)

    small, t_in, t_sm = [None] * N_LAYERS, [None] * N_LAYERS, [None] * N_LAYERS
    halves = None

    for l in reversed(range(N_LAYERS)):
        w_full, w_sm = full_w[l]
        last = l == 0
        ht, u_conv, u_qkv, u_za, u_gl, y_c, o, y_a, y_b, merged, qs, kvx, lse = saved[l]
        (d_ya, d_yb, du_gl, d_yc, d_o, dgb), _ = _out_proj_bwd(dout, y_a, y_b, u_gl, gate_b[l:l + 1], w_sm, None)
        g_sm = _small_wgrads(y_c, d_ya, o, d_yb, merged, dout).reshape(3, 4, 2, 128, D)
        (du_conv, dcw), got = _conv_bwd(d_yc, u_conv, conv_full[l], _swap_rider(None, g_sm) if last else None)
        if last:
            f_sm0, h_sm0 = _add_halves_sm(c_idx, g_sm, got[0])
        (dqs, dkv, dza, dsk), got = _attn_bwd(d_o, qs, kvx, u_za, lse, sinks[l:l + 1], bias,
                                              _scatter_rider(halves[1], None) if halves else None)
        if halves:
            t_in[l + 1] = _final_sum_in(halves[0], got[0])
        dsk = jnp.sum(dsk[0].reshape(N_KV, 2, 2, BLK), axis=-1).transpose(0, 2, 1).reshape(N_Q)
        (du_attn, dqg, dkg), _ = _qkv_post(u_qkv, dqs, dkv, dza, qg_s[l:l + 1], kg_t[l:l + 1], None)
        du = (du_conv, du_attn, du_gl)
        g_in, got = _in_proj_wgrad(ht, du, _merge_riders(
            _scatter_rider(None, halves[3]) if halves else None, _scatter_rider(None, h_sm0) if last else None))
        g_in = g_in.reshape(2, 512, N_IN)
        if halves:
            t_sm[l + 1] = _final_sum_sm(chip_idx, halves[2], got[0])
        if last:
            t_sm[0] = _final_sum_sm(chip_idx, f_sm0, got[-1])
        if last:
            f_in0, h_in0 = _add_halves_in(cc_idx, g_in, _ride_alone(_swap_rider(g_in, None), "swap_last")[0])
        (dout, dng), got = _in_proj_bwd(du, w_full, xs[l], norm_g[l:l + 1], dout,
                                        _scatter_rider(h_in0, None) if last else _swap_rider(g_in, g_sm))
        if last:
            t_in[0] = _final_sum_in(f_in0, got[0])
        else:
            halves = _add_halves_in(cc_idx, g_in, got[0]) + _add_halves_sm(c_idx, g_sm, got[1])
        small[l] = (jnp.sum(dng, axis=0), (SCALE * LOG2E) * jnp.sum(dqg.reshape(8 * N_Q, HEAD), axis=0),
                    jnp.sum(dkg.reshape(8 * N_KV, HEAD), axis=0), dsk, jnp.sum(dgb, axis=0), dcw[:3])
    grad_x = dout.reshape(1, t, D)

    stack = lambda k: jnp.stack([small[l][k] for l in range(N_LAYERS)])
    pack = jnp.concatenate([_pad_rows(stack(0), 32), _pad_rows(stack(1), 8), _pad_rows(stack(2), 8),
                            _pad_rows(stack(3), 8), _pad_rows(stack(4), 64), _pad_rows(stack(5), 96),
                            _pad_rows(jnp.sum(sq) * (0.5 / D), 8)], axis=0)
    red = _allreduce_small(pack)
    loss = red[216, 0]
    g_norm_g = red[0:32].reshape(N_LAYERS, D)
    g_q_norm_g = red[32:40].reshape(-1)[:N_LAYERS * HEAD].reshape(N_LAYERS, HEAD)
    g_k_norm_g = red[40:48].reshape(-1)[:N_LAYERS * HEAD].reshape(N_LAYERS, HEAD)
    g_sinks = red[48:56].reshape(-1)[:N_LAYERS * N_Q].reshape(N_LAYERS, N_Q)
    g_gate_b = red[56:120].reshape(N_LAYERS, 2 * D)
    g_conv_full = red[120:216].reshape(N_LAYERS, 3, D)
    g_conv_w = lax.dynamic_slice(g_conv_full, (0, 0, chip * SH_ROW), (N_LAYERS, 3, SH_ROW))

    g_w_in, g_w_co, g_w_ao, g_w_out = _join_halves(t_in, t_sm)

    r_in = N_LAYERS * D
    g_w_in, d_in, nm_in, nv_in = (a.reshape(N_LAYERS, D, SH_IN) for a in _adamw_big(
        w_in.reshape(r_in, SH_IN), g_w_in.reshape(r_in, SH_IN), m_w_in.reshape(r_in, SH_IN),
        v_w_in.reshape(r_in, SH_IN), "adamw_w_in"))
    r_sm = N_LAYERS * SH_ROW
    big = {}
    for nm, w, g, m, v in (("co", w_conv_out, g_w_co, m_w_conv_out, v_w_conv_out),
                           ("ao", w_attn_out, g_w_ao, m_w_attn_out, v_w_attn_out),
                           ("out", w_out, g_w_out, m_w_out, v_w_out)):
        big[nm] = tuple(a.reshape(N_LAYERS, SH_ROW, D) for a in _adamw_big(
            w.reshape(r_sm, D), g.reshape(r_sm, D), m.reshape(r_sm, D), v.reshape(r_sm, D), "adamw_w_small"))
    g_w_co, g_w_ao, g_w_out = big["co"][0], big["ao"][0], big["out"][0]
    sm_w = [norm_g, conv_w, q_norm_g, k_norm_g, sinks, gate_b]
    sm_g = [g_norm_g, g_conv_w, g_q_norm_g, g_k_norm_g, g_sinks, g_gate_b]
    sm_m = [m_norm_g, m_conv_w, m_q_norm_g, m_k_norm_g, m_sinks, m_gate_b]
    sm_v = [v_norm_g, v_conv_w, v_q_norm_g, v_k_norm_g, v_sinks, v_gate_b]
    sd, snm, snv = _adamw_small(sm_w, sm_g, sm_m, sm_v)

    def order(norm, w_in_, conv, qn, kn, sk, co, ao, gb, wo):
        return [norm, w_in_, conv, qn, kn, sk, co, ao, gb, wo]

    grads = order(g_norm_g, g_w_in, g_conv_w, g_q_norm_g, g_k_norm_g, g_sinks, g_w_co, g_w_ao, g_gate_b, g_w_out)
    deltas = order(sd[0], d_in, sd[1], sd[2], sd[3], sd[4], big["co"][1], big["ao"][1], sd[5], big["out"][1])
    new_m = order(snm[0], nm_in, snm[1], snm[2], snm[3], snm[4], big["co"][2], big["ao"][2], snm[5], big["out"][2])
    new_v = order(snv[0], nv_in, snv[1], snv[2], snv[3], snv[4], big["co"][3], big["ao"][3], snv[5], big["out"][3])
    return (loss, grad_x, *grads, *deltas, *new_m, *new_v)
```
